```python
import jax, jax.numpy as jnp
from jax import lax
import numpy as np

D_MODEL = 1024
BATCH = 8
SEQ = 4096
DEPTH = 1

D_PLE = 256
D_MIX = 2 * D_MODEL
D_SSM = D_MIX // 2
D_CONF = D_MIX - D_SSM
SSM_HEAD_DIM = 64
SSM_HEADS = D_SSM // SSM_HEAD_DIM
SSM_GROUPS = 2
SSM_HPG = SSM_HEADS // SSM_GROUPS
SSM_STATE = 128
SSM_CONV_K = 4
CHUNK = 128
CONF_K = 31
LN_EPS = 1e-5
RMS_EPS = 1e-5

COLS = (D_SSM,
        D_SSM,
        SSM_GROUPS * SSM_STATE,
        SSM_GROUPS * SSM_STATE,
        SSM_HEADS,
        2 * D_CONF,
        D_CONF)
D_IN_PROJ = sum(COLS)
SPLITS = tuple(int(s) for s in np.cumsum(COLS)[:-1])

DEEPNORM_ALPHA = (2.0 * DEPTH) ** 0.25
DEEPNORM_BETA = (8.0 * DEPTH) ** -0.25

kernel_name = "hybrid_ssd_conformer_deepnorm_block"


def layer_norm(x, g, b):
    xf = x.astype(jnp.float32)
    mu = jnp.mean(xf, axis=-1, keepdims=True)
    var = jnp.mean(jnp.square(xf - mu), axis=-1, keepdims=True)
    return ((xf - mu) * lax.rsqrt(var + LN_EPS)).astype(x.dtype) * g + b


def causal_depthwise_conv(x, w, b):
    k = w.shape[0]
    xp = jnp.pad(x, ((0, 0), (k - 1, 0), (0, 0)))
    y = lax.conv_general_dilated(xp, w[:, None, :], window_strides=(1,), padding='VALID',
                                 dimension_numbers=('NWC', 'WIO', 'NWC'),
                                 feature_group_count=x.shape[-1])
    return y + b


def ssd_chunked(xh, dt, a, bm, cm):
    bsz, l, g, j, p = xh.shape
    n = bm.shape[-1]
    c = l // CHUNK
    xdt = (xh * dt[..., None]).reshape(bsz, c, CHUNK, g, j, p)
    adt = jnp.moveaxis((dt * a).reshape(bsz, c, CHUNK, g, j), 2, -1)
    a_cs = jnp.cumsum(adt, axis=-1)
    bc = bm.reshape(bsz, c, CHUNK, g, n)
    cc = cm.reshape(bsz, c, CHUNK, g, n)
    causal = jnp.tril(jnp.ones((CHUNK, CHUNK), dtype=bool))
    seg = a_cs[..., :, None] - a_cs[..., None, :]
    ldec = jnp.exp(jnp.where(causal, seg, -jnp.inf))
    cb = jnp.einsum('bclgn,bcsgn->bcgls', cc, bc)
    y_diag = jnp.einsum('bcgjls,bcsgjp->bclgjp', cb[:, :, :, None] * ldec, xdt)
    decay_states = jnp.exp(a_cs[..., -1:] - a_cs)
    states = jnp.einsum('bclgn,bcgjl,bclgjp->bcgjpn', bc, decay_states, xdt)
    chunk_decay = jnp.exp(a_cs[..., -1])

    def step(h, inp):
        s, d = inp
        return h * d[..., None, None] + s, h

    h0 = jnp.zeros((bsz, g, j, p, n), dtype=xdt.dtype)
    _, prev = lax.scan(step, h0, (jnp.moveaxis(states, 1, 0), jnp.moveaxis(chunk_decay, 1, 0)))
    prev = jnp.moveaxis(prev, 0, 1)
    y_off = jnp.einsum('bclgn,bcgjpn,bcgjl->bclgjp', cc, prev, jnp.exp(a_cs))
    return (y_diag + y_off).reshape(bsz, l, g, j, p)


def _fwd_setup_inputs(seed: int = 0) -> dict:
    key = jax.random.key(seed)
    ks = jax.random.split(key, 26)
    nrm = jax.random.normal
    f32 = jnp.float32
    x = nrm(ks[0], (BATCH, SEQ, D_MODEL), f32)
    p = nrm(ks[1], (DEPTH, BATCH, SEQ, D_PLE), f32)
    ln_emb_g = 1.0 + 0.02 * nrm(ks[2], (D_MODEL,), f32)
    ln_emb_b = 0.02 * nrm(ks[3], (D_MODEL,), f32)
    w_in = nrm(ks[4], (DEPTH, D_MODEL, D_IN_PROJ), f32) * D_MODEL ** -0.5
    d_xbc = D_SSM + 2 * SSM_GROUPS * SSM_STATE
    ssm_conv_w = nrm(ks[5], (DEPTH, SSM_CONV_K, d_xbc), f32) * SSM_CONV_K ** -0.5
    ssm_conv_b = 0.02 * nrm(ks[6], (DEPTH, d_xbc), f32)
    dt0 = jnp.exp(jax.random.uniform(ks[7], (DEPTH, SSM_HEADS), f32, np.log(1e-3), np.log(1e-1)))
    dt_bias = dt0 + jnp.log(-jnp.expm1(-dt0))
    a_log = jnp.log(jax.random.uniform(ks[8], (DEPTH, SSM_HEADS), f32, 1.0, 16.0))
    d_skip = 1.0 + 0.1 * nrm(ks[9], (DEPTH, SSM_HEADS), f32)
    ssm_norm_g = 1.0 + 0.02 * nrm(ks[10], (DEPTH, D_SSM), f32)
    b_glu = 0.02 * nrm(ks[11], (DEPTH, 2 * D_CONF), f32)
    conf_conv_w = nrm(ks[12], (DEPTH, CONF_K, D_CONF), f32) * CONF_K ** -0.5
    conf_conv_b = 0.02 * nrm(ks[13], (DEPTH, D_CONF), f32)
    conf_ln_g = 1.0 + 0.02 * nrm(ks[14], (DEPTH, D_CONF), f32)
    conf_ln_b = 0.02 * nrm(ks[15], (DEPTH, D_CONF), f32)
    w_out = nrm(ks[16], (DEPTH, D_MIX, D_MODEL), f32) * (DEEPNORM_BETA * (2.0 / (D_MIX + D_MODEL)) ** 0.5)
    b_out = 0.02 * nrm(ks[17], (DEPTH, D_MODEL), f32)
    ln1_g = 1.0 + 0.02 * nrm(ks[18], (DEPTH, D_MODEL), f32)
    ln1_b = 0.02 * nrm(ks[19], (DEPTH, D_MODEL), f32)
    w_ple_gate = nrm(ks[20], (DEPTH, D_MODEL, D_MODEL), f32) * D_MODEL ** -0.5
    w_ple_proj = nrm(ks[21], (DEPTH, D_PLE, D_MODEL), f32) * (DEEPNORM_BETA * (2.0 / (D_PLE + D_MODEL)) ** 0.5)
    ln2_g = 1.0 + 0.02 * nrm(ks[22], (DEPTH, D_MODEL), f32)
    ln2_b = 0.02 * nrm(ks[23], (DEPTH, D_MODEL), f32)
    return {"x": x, "p": p, "ln_emb_g": ln_emb_g, "ln_emb_b": ln_emb_b, "w_in": w_in,
            "ssm_conv_w": ssm_conv_w, "ssm_conv_b": ssm_conv_b, "dt_bias": dt_bias,
            "a_log": a_log, "d_skip": d_skip, "ssm_norm_g": ssm_norm_g, "b_glu": b_glu,
            "conf_conv_w": conf_conv_w, "conf_conv_b": conf_conv_b, "conf_ln_g": conf_ln_g,
            "conf_ln_b": conf_ln_b, "w_out": w_out, "b_out": b_out, "ln1_g": ln1_g,
            "ln1_b": ln1_b, "w_ple_gate": w_ple_gate, "w_ple_proj": w_ple_proj,
            "ln2_g": ln2_g, "ln2_b": ln2_b}


def _fwd_reference(x, p, ln_emb_g, ln_emb_b, w_in, ssm_conv_w, ssm_conv_b, dt_bias, a_log,
              d_skip, ssm_norm_g, b_glu, conf_conv_w, conf_conv_b, conf_ln_g, conf_ln_b,
              w_out, b_out, ln1_g, ln1_b, w_ple_gate, w_ple_proj, ln2_g, ln2_b):
    bsz, l, _ = x.shape
    f32 = jnp.float32
    gn = SSM_GROUPS * SSM_STATE
    h = layer_norm(x, ln_emb_g, ln_emb_b)
    for i in range(DEPTH):
        proj = jnp.einsum('bld,de->ble', h, w_in[i])
        xs, z, bm, cm, dt_raw, glu, cgate = jnp.split(proj, SPLITS, axis=-1)

        xbc = jnp.concatenate([xs, bm, cm], axis=-1)
        xbc = jax.nn.silu(causal_depthwise_conv(xbc, ssm_conv_w[i], ssm_conv_b[i]))
        xs_c = xbc[..., :D_SSM]
        bm_c = xbc[..., D_SSM:D_SSM + gn].reshape(bsz, l, SSM_GROUPS, SSM_STATE).astype(f32)
        cm_c = xbc[..., D_SSM + gn:].reshape(bsz, l, SSM_GROUPS, SSM_STATE).astype(f32)
        xh = xs_c.reshape(bsz, l, SSM_GROUPS, SSM_HPG, SSM_HEAD_DIM).astype(f32)
        dt = jax.nn.softplus((dt_raw + dt_bias[i]).astype(f32)).reshape(bsz, l, SSM_GROUPS, SSM_HPG)
        a = -jnp.exp(a_log[i].astype(f32)).reshape(SSM_GROUPS, SSM_HPG)
        y = ssd_chunked(xh, dt, a, bm_c, cm_c)
        y = y + d_skip[i].astype(f32).reshape(SSM_GROUPS, SSM_HPG)[:, :, None] * xh
        yz = y.reshape(bsz, l, SSM_GROUPS, D_SSM // SSM_GROUPS) * \
            jax.nn.silu(z.astype(f32)).reshape(bsz, l, SSM_GROUPS, D_SSM // SSM_GROUPS)
        yz = yz * lax.rsqrt(jnp.mean(jnp.square(yz), axis=-1, keepdims=True) + RMS_EPS)
        y_ssm = yz.reshape(bsz, l, D_SSM).astype(x.dtype) * ssm_norm_g[i]

        glu = glu + b_glu[i]
        u = glu[..., :D_CONF] * jax.nn.sigmoid(glu[..., D_CONF:])
        u = causal_depthwise_conv(u, conf_conv_w[i], conf_conv_b[i])
        u = jax.nn.silu(layer_norm(u, conf_ln_g[i], conf_ln_b[i]))
        y_conf = u * jax.nn.silu(cgate)

        mix = jnp.concatenate([y_ssm, y_conf], axis=-1)
        out = jnp.einsum('ble,ed->bld', mix, w_out[i]) + b_out[i]
        h = layer_norm(DEEPNORM_ALPHA * h + out, ln1_g[i], ln1_b[i])

        gate = jax.nn.sigmoid(jnp.einsum('bld,de->ble', h, w_ple_gate[i]))
        ple = jnp.einsum('blq,qd->bld', p[i], w_ple_proj[i])
        h = layer_norm(DEEPNORM_ALPHA * h + gate * ple, ln2_g[i], ln2_b[i])
    return h


import jax as _jax
import jax.numpy as _jnp

TWIN_FORMAT = 'train_step'
FWD_PARAMS = ['x', 'p', 'ln_emb_g', 'ln_emb_b', 'w_in', 'ssm_conv_w', 'ssm_conv_b', 'dt_bias', 'a_log', 'd_skip', 'ssm_norm_g', 'b_glu', 'conf_conv_w', 'conf_conv_b', 'conf_ln_g', 'conf_ln_b', 'w_out', 'b_out', 'ln1_g', 'ln1_b', 'w_ple_gate', 'w_ple_proj', 'ln2_g', 'ln2_b']
TWIN_WEIGHTS = ['ln_emb_g', 'ln_emb_b', 'w_in', 'ssm_conv_w', 'ssm_conv_b', 'dt_bias', 'a_log', 'd_skip', 'ssm_norm_g', 'b_glu', 'conf_conv_w', 'conf_conv_b', 'conf_ln_g', 'conf_ln_b', 'w_out', 'b_out', 'ln1_g', 'ln1_b', 'w_ple_gate', 'w_ple_proj', 'ln2_g', 'ln2_b']
TWIN_DIFF_INPUT = 'x'
TWIN_INPUTS = ['x', 'p', 'ln_emb_g', 'ln_emb_b', 'w_in', 'ssm_conv_w', 'ssm_conv_b', 'dt_bias', 'a_log', 'd_skip', 'ssm_norm_g', 'b_glu', 'conf_conv_w', 'conf_conv_b', 'conf_ln_g', 'conf_ln_b', 'w_out', 'b_out', 'ln1_g', 'ln1_b', 'w_ple_gate', 'w_ple_proj', 'ln2_g', 'ln2_b', 'loss_target', 'm_ln_emb_g', 'm_ln_emb_b', 'm_w_in', 'm_ssm_conv_w', 'm_ssm_conv_b', 'm_dt_bias', 'm_a_log', 'm_d_skip', 'm_ssm_norm_g', 'm_b_glu', 'm_conf_conv_w', 'm_conf_conv_b', 'm_conf_ln_g', 'm_conf_ln_b', 'm_w_out', 'm_b_out', 'm_ln1_g', 'm_ln1_b', 'm_w_ple_gate', 'm_w_ple_proj', 'm_ln2_g', 'm_ln2_b', 'v_ln_emb_g', 'v_ln_emb_b', 'v_w_in', 'v_ssm_conv_w', 'v_ssm_conv_b', 'v_dt_bias', 'v_a_log', 'v_d_skip', 'v_ssm_norm_g', 'v_b_glu', 'v_conf_conv_w', 'v_conf_conv_b', 'v_conf_ln_g', 'v_conf_ln_b', 'v_w_out', 'v_b_out', 'v_ln1_g', 'v_ln1_b', 'v_w_ple_gate', 'v_w_ple_proj', 'v_ln2_g', 'v_ln2_b']
TWIN_OUTPUTS = ['loss', 'grad_x', 'grad_ln_emb_g', 'grad_ln_emb_b', 'grad_w_in', 'grad_ssm_conv_w', 'grad_ssm_conv_b', 'grad_dt_bias', 'grad_a_log', 'grad_d_skip', 'grad_ssm_norm_g', 'grad_b_glu', 'grad_conf_conv_w', 'grad_conf_conv_b', 'grad_conf_ln_g', 'grad_conf_ln_b', 'grad_w_out', 'grad_b_out', 'grad_ln1_g', 'grad_ln1_b', 'grad_w_ple_gate', 'grad_w_ple_proj', 'grad_ln2_g', 'grad_ln2_b', 'delta_ln_emb_g', 'delta_ln_emb_b', 'delta_w_in', 'delta_ssm_conv_w', 'delta_ssm_conv_b', 'delta_dt_bias', 'delta_a_log', 'delta_d_skip', 'delta_ssm_norm_g', 'delta_b_glu', 'delta_conf_conv_w', 'delta_conf_conv_b', 'delta_conf_ln_g', 'delta_conf_ln_b', 'delta_w_out', 'delta_b_out', 'delta_ln1_g', 'delta_ln1_b', 'delta_w_ple_gate', 'delta_w_ple_proj', 'delta_ln2_g', 'delta_ln2_b', 'new_m_ln_emb_g', 'new_m_ln_emb_b', 'new_m_w_in', 'new_m_ssm_conv_w', 'new_m_ssm_conv_b', 'new_m_dt_bias', 'new_m_a_log', 'new_m_d_skip', 'new_m_ssm_norm_g', 'new_m_b_glu', 'new_m_conf_conv_w', 'new_m_conf_conv_b', 'new_m_conf_ln_g', 'new_m_conf_ln_b', 'new_m_w_out', 'new_m_b_out', 'new_m_ln1_g', 'new_m_ln1_b', 'new_m_w_ple_gate', 'new_m_w_ple_proj', 'new_m_ln2_g', 'new_m_ln2_b', 'new_v_ln_emb_g', 'new_v_ln_emb_b', 'new_v_w_in', 'new_v_ssm_conv_w', 'new_v_ssm_conv_b', 'new_v_dt_bias', 'new_v_a_log', 'new_v_d_skip', 'new_v_ssm_norm_g', 'new_v_b_glu', 'new_v_conf_conv_w', 'new_v_conf_conv_b', 'new_v_conf_ln_g', 'new_v_conf_ln_b', 'new_v_w_out', 'new_v_b_out', 'new_v_ln1_g', 'new_v_ln1_b', 'new_v_w_ple_gate', 'new_v_w_ple_proj', 'new_v_ln2_g', 'new_v_ln2_b']
TWIN_LEAF_KINDS = {'loss': 'loss', 'grad_x': 'grad_x', 'grad_ln_emb_g': 'grad_w', 'grad_ln_emb_b': 'grad_w', 'grad_w_in': 'grad_w', 'grad_ssm_conv_w': 'grad_w', 'grad_ssm_conv_b': 'grad_w', 'grad_dt_bias': 'grad_w', 'grad_a_log': 'grad_w', 'grad_d_skip': 'grad_w', 'grad_ssm_norm_g': 'grad_w', 'grad_b_glu': 'grad_w', 'grad_conf_conv_w': 'grad_w', 'grad_conf_conv_b': 'grad_w', 'grad_conf_ln_g': 'grad_w', 'grad_conf_ln_b': 'grad_w', 'grad_w_out': 'grad_w', 'grad_b_out': 'grad_w', 'grad_ln1_g': 'grad_w', 'grad_ln1_b': 'grad_w', 'grad_w_ple_gate': 'grad_w', 'grad_w_ple_proj': 'grad_w', 'grad_ln2_g': 'grad_w', 'grad_ln2_b': 'grad_w', 'delta_ln_emb_g': 'delta_w', 'delta_ln_emb_b': 'delta_w', 'delta_w_in': 'delta_w', 'delta_ssm_conv_w': 'delta_w', 'delta_ssm_conv_b': 'delta_w', 'delta_dt_bias': 'delta_w', 'delta_a_log': 'delta_w', 'delta_d_skip': 'delta_w', 'delta_ssm_norm_g': 'delta_w', 'delta_b_glu': 'delta_w', 'delta_conf_conv_w': 'delta_w', 'delta_conf_conv_b': 'delta_w', 'delta_conf_ln_g': 'delta_w', 'delta_conf_ln_b': 'delta_w', 'delta_w_out': 'delta_w', 'delta_b_out': 'delta_w', 'delta_ln1_g': 'delta_w', 'delta_ln1_b': 'delta_w', 'delta_w_ple_gate': 'delta_w', 'delta_w_ple_proj': 'delta_w', 'delta_ln2_g': 'delta_w', 'delta_ln2_b': 'delta_w', 'new_m_ln_emb_g': 'new_m', 'new_m_ln_emb_b': 'new_m', 'new_m_w_in': 'new_m', 'new_m_ssm_conv_w': 'new_m', 'new_m_ssm_conv_b': 'new_m', 'new_m_dt_bias': 'new_m', 'new_m_a_log': 'new_m', 'new_m_d_skip': 'new_m', 'new_m_ssm_norm_g': 'new_m', 'new_m_b_glu': 'new_m', 'new_m_conf_conv_w': 'new_m', 'new_m_conf_conv_b': 'new_m', 'new_m_conf_ln_g': 'new_m', 'new_m_conf_ln_b': 'new_m', 'new_m_w_out': 'new_m', 'new_m_b_out': 'new_m', 'new_m_ln1_g': 'new_m', 'new_m_ln1_b': 'new_m', 'new_m_w_ple_gate': 'new_m', 'new_m_w_ple_proj': 'new_m', 'new_m_ln2_g': 'new_m', 'new_m_ln2_b': 'new_m', 'new_v_ln_emb_g': 'new_v', 'new_v_ln_emb_b': 'new_v', 'new_v_w_in': 'new_v', 'new_v_ssm_conv_w': 'new_v', 'new_v_ssm_conv_b': 'new_v', 'new_v_dt_bias': 'new_v', 'new_v_a_log': 'new_v', 'new_v_d_skip': 'new_v', 'new_v_ssm_norm_g': 'new_v', 'new_v_b_glu': 'new_v', 'new_v_conf_conv_w': 'new_v', 'new_v_conf_conv_b': 'new_v', 'new_v_conf_ln_g': 'new_v', 'new_v_conf_ln_b': 'new_v', 'new_v_w_out': 'new_v', 'new_v_b_out': 'new_v', 'new_v_ln1_g': 'new_v', 'new_v_ln1_b': 'new_v', 'new_v_w_ple_gate': 'new_v', 'new_v_w_ple_proj': 'new_v', 'new_v_ln2_g': 'new_v', 'new_v_ln2_b': 'new_v'}


def _forward(args):
    return _fwd_reference(*[args[k] for k in FWD_PARAMS])


def _output_shape():
    def fwd():
        inp = _fwd_setup_inputs(0)
        return _fwd_reference(*[inp[k] for k in FWD_PARAMS])
    out = _jax.eval_shape(fwd)
    return out.shape, out.dtype

N_MICROBATCH = 1
ADAM_LR = 0.001
ADAM_B1 = 0.9
ADAM_B2 = 0.999
ADAM_EPS = 1e-08
ADAM_WD = 0.01
ADAM_STEP = 10
PER_EXAMPLE_BATCH_AXIS = {'x': 0, 'p': 1, 'loss_target': 0}
SHARED_INPUTS = []
_WEIGHT_DTYPES = {'ln_emb_g': _jnp.float32, 'ln_emb_b': _jnp.float32, 'w_in': _jnp.float32, 'ssm_conv_w': _jnp.float32, 'ssm_conv_b': _jnp.float32, 'dt_bias': _jnp.float32, 'a_log': _jnp.float32, 'd_skip': _jnp.float32, 'ssm_norm_g': _jnp.float32, 'b_glu': _jnp.float32, 'conf_conv_w': _jnp.float32, 'conf_conv_b': _jnp.float32, 'conf_ln_g': _jnp.float32, 'conf_ln_b': _jnp.float32, 'w_out': _jnp.float32, 'b_out': _jnp.float32, 'ln1_g': _jnp.float32, 'ln1_b': _jnp.float32, 'w_ple_gate': _jnp.float32, 'w_ple_proj': _jnp.float32, 'ln2_g': _jnp.float32, 'ln2_b': _jnp.float32}
MOMENT_SCALE = {'ln_emb_g': 9.752694e-01, 'ln_emb_b': 5.131034e-01, 'w_in': 4.350612e-02, 'ssm_conv_w': 5.776454e-02, 'ssm_conv_b': 1.077792e-01, 'dt_bias': 1.605880e-01, 'a_log': 1.977496e-01, 'd_skip': 3.554798e-01, 'ssm_norm_g': 7.355354e-02, 'b_glu': 2.096712e-02, 'conf_conv_w': 2.416220e-02, 'conf_conv_b': 5.327358e-02, 'conf_ln_g': 2.996892e-02, 'conf_ln_b': 2.764113e-02, 'w_out': 1.131605e-01, 'b_out': 4.304673e-01, 'ln1_g': 1.194448e+00, 'ln1_b': 5.583519e-01, 'w_ple_gate': 1.165787e-02, 'w_ple_proj': 7.949512e-02, 'ln2_g': 3.205309e+01, 'ln2_b': 2.784253e+00}


def _to_microbatches(a, axis):
    t = _jnp.moveaxis(a, axis, 0)
    t = t.reshape((N_MICROBATCH, t.shape[0] // N_MICROBATCH) + t.shape[1:])
    return _jnp.moveaxis(t, 1, axis + 1)


def setup_inputs(seed: int = 0) -> dict:
    inp = _fwd_setup_inputs(seed)
    key = _jax.random.fold_in(_jax.random.key(seed), 7919)
    shape, _ = _output_shape()
    out = dict(inp)
    out["loss_target"] = _jax.random.normal(_jax.random.fold_in(key, 0), shape, _jnp.float32)
    for i, name in enumerate(TWIN_WEIGHTS):
        w = inp[name].astype(_jnp.float32)
        if MOMENT_SCALE is None:
            s = _jnp.sqrt(_jnp.mean(_jnp.square(w)) + 1e-30)
        else:
            s = MOMENT_SCALE[name]
        km, kv = _jax.random.split(_jax.random.fold_in(key, i + 1))
        out[name] = w
        out["m_" + name] = s * _jax.random.normal(km, w.shape, _jnp.float32)
        out["v_" + name] = (s * s) * _jax.random.uniform(kv, w.shape, _jnp.float32, 0.5, 1.5)
    if N_MICROBATCH > 1:
        for name, axis in PER_EXAMPLE_BATCH_AXIS.items():
            out[name] = _to_microbatches(out[name], axis)
    return {'x': out['x'], 'p': out['p'], 'ln_emb_g': out['ln_emb_g'], 'ln_emb_b': out['ln_emb_b'], 'w_in': out['w_in'], 'ssm_conv_w': out['ssm_conv_w'], 'ssm_conv_b': out['ssm_conv_b'], 'dt_bias': out['dt_bias'], 'a_log': out['a_log'], 'd_skip': out['d_skip'], 'ssm_norm_g': out['ssm_norm_g'], 'b_glu': out['b_glu'], 'conf_conv_w': out['conf_conv_w'], 'conf_conv_b': out['conf_conv_b'], 'conf_ln_g': out['conf_ln_g'], 'conf_ln_b': out['conf_ln_b'], 'w_out': out['w_out'], 'b_out': out['b_out'], 'ln1_g': out['ln1_g'], 'ln1_b': out['ln1_b'], 'w_ple_gate': out['w_ple_gate'], 'w_ple_proj': out['w_ple_proj'], 'ln2_g': out['ln2_g'], 'ln2_b': out['ln2_b'], 'loss_target': out['loss_target'], 'm_ln_emb_g': out['m_ln_emb_g'], 'm_ln_emb_b': out['m_ln_emb_b'], 'm_w_in': out['m_w_in'], 'm_ssm_conv_w': out['m_ssm_conv_w'], 'm_ssm_conv_b': out['m_ssm_conv_b'], 'm_dt_bias': out['m_dt_bias'], 'm_a_log': out['m_a_log'], 'm_d_skip': out['m_d_skip'], 'm_ssm_norm_g': out['m_ssm_norm_g'], 'm_b_glu': out['m_b_glu'], 'm_conf_conv_w': out['m_conf_conv_w'], 'm_conf_conv_b': out['m_conf_conv_b'], 'm_conf_ln_g': out['m_conf_ln_g'], 'm_conf_ln_b': out['m_conf_ln_b'], 'm_w_out': out['m_w_out'], 'm_b_out': out['m_b_out'], 'm_ln1_g': out['m_ln1_g'], 'm_ln1_b': out['m_ln1_b'], 'm_w_ple_gate': out['m_w_ple_gate'], 'm_w_ple_proj': out['m_w_ple_proj'], 'm_ln2_g': out['m_ln2_g'], 'm_ln2_b': out['m_ln2_b'], 'v_ln_emb_g': out['v_ln_emb_g'], 'v_ln_emb_b': out['v_ln_emb_b'], 'v_w_in': out['v_w_in'], 'v_ssm_conv_w': out['v_ssm_conv_w'], 'v_ssm_conv_b': out['v_ssm_conv_b'], 'v_dt_bias': out['v_dt_bias'], 'v_a_log': out['v_a_log'], 'v_d_skip': out['v_d_skip'], 'v_ssm_norm_g': out['v_ssm_norm_g'], 'v_b_glu': out['v_b_glu'], 'v_conf_conv_w': out['v_conf_conv_w'], 'v_conf_conv_b': out['v_conf_conv_b'], 'v_conf_ln_g': out['v_conf_ln_g'], 'v_conf_ln_b': out['v_conf_ln_b'], 'v_w_out': out['v_w_out'], 'v_b_out': out['v_b_out'], 'v_ln1_g': out['v_ln1_g'], 'v_ln1_b': out['v_ln1_b'], 'v_w_ple_gate': out['v_w_ple_gate'], 'v_w_ple_proj': out['v_w_ple_proj'], 'v_ln2_g': out['v_ln2_g'], 'v_ln2_b': out['v_ln2_b']}


def _loss(weights, diff, rest, loss_target):
    with _jax.named_scope("forward"):
        args = {**rest, TWIN_DIFF_INPUT: diff, **{k: w.astype(_WEIGHT_DTYPES[k]) for k, w in weights.items()}}
        y = _forward(args)
    with _jax.named_scope("loss_head"):
        err = _jnp.square(y.astype(_jnp.float32) - loss_target)
        return 0.5 * _jnp.sum(_jnp.mean(err, axis=-1)) if err.ndim else 0.5 * err


def _adamw(w, g, m, v):
    m = ADAM_B1 * m + (1.0 - ADAM_B1) * g
    v = ADAM_B2 * v + (1.0 - ADAM_B2) * _jnp.square(g)
    m_hat = m / (1.0 - ADAM_B1 ** ADAM_STEP)
    v_hat = v / (1.0 - ADAM_B2 ** ADAM_STEP)
    delta = -ADAM_LR * (m_hat / (_jnp.sqrt(v_hat) + ADAM_EPS) + ADAM_WD * w)
    return delta, m, v


def reference(x, p, ln_emb_g, ln_emb_b, w_in, ssm_conv_w, ssm_conv_b, dt_bias, a_log, d_skip, ssm_norm_g, b_glu, conf_conv_w, conf_conv_b, conf_ln_g, conf_ln_b, w_out, b_out, ln1_g, ln1_b, w_ple_gate, w_ple_proj, ln2_g, ln2_b, loss_target, m_ln_emb_g, m_ln_emb_b, m_w_in, m_ssm_conv_w, m_ssm_conv_b, m_dt_bias, m_a_log, m_d_skip, m_ssm_norm_g, m_b_glu, m_conf_conv_w, m_conf_conv_b, m_conf_ln_g, m_conf_ln_b, m_w_out, m_b_out, m_ln1_g, m_ln1_b, m_w_ple_gate, m_w_ple_proj, m_ln2_g, m_ln2_b, v_ln_emb_g, v_ln_emb_b, v_w_in, v_ssm_conv_w, v_ssm_conv_b, v_dt_bias, v_a_log, v_d_skip, v_ssm_norm_g, v_b_glu, v_conf_conv_w, v_conf_conv_b, v_conf_ln_g, v_conf_ln_b, v_w_out, v_b_out, v_ln1_g, v_ln1_b, v_w_ple_gate, v_w_ple_proj, v_ln2_g, v_ln2_b):
    given = dict(x=x, p=p, ln_emb_g=ln_emb_g, ln_emb_b=ln_emb_b, w_in=w_in, ssm_conv_w=ssm_conv_w, ssm_conv_b=ssm_conv_b, dt_bias=dt_bias, a_log=a_log, d_skip=d_skip, ssm_norm_g=ssm_norm_g, b_glu=b_glu, conf_conv_w=conf_conv_w, conf_conv_b=conf_conv_b, conf_ln_g=conf_ln_g, conf_ln_b=conf_ln_b, w_out=w_out, b_out=b_out, ln1_g=ln1_g, ln1_b=ln1_b, w_ple_gate=w_ple_gate, w_ple_proj=w_ple_proj, ln2_g=ln2_g, ln2_b=ln2_b, loss_target=loss_target, m_ln_emb_g=m_ln_emb_g, m_ln_emb_b=m_ln_emb_b, m_w_in=m_w_in, m_ssm_conv_w=m_ssm_conv_w, m_ssm_conv_b=m_ssm_conv_b, m_dt_bias=m_dt_bias, m_a_log=m_a_log, m_d_skip=m_d_skip, m_ssm_norm_g=m_ssm_norm_g, m_b_glu=m_b_glu, m_conf_conv_w=m_conf_conv_w, m_conf_conv_b=m_conf_conv_b, m_conf_ln_g=m_conf_ln_g, m_conf_ln_b=m_conf_ln_b, m_w_out=m_w_out, m_b_out=m_b_out, m_ln1_g=m_ln1_g, m_ln1_b=m_ln1_b, m_w_ple_gate=m_w_ple_gate, m_w_ple_proj=m_w_ple_proj, m_ln2_g=m_ln2_g, m_ln2_b=m_ln2_b, v_ln_emb_g=v_ln_emb_g, v_ln_emb_b=v_ln_emb_b, v_w_in=v_w_in, v_ssm_conv_w=v_ssm_conv_w, v_ssm_conv_b=v_ssm_conv_b, v_dt_bias=v_dt_bias, v_a_log=v_a_log, v_d_skip=v_d_skip, v_ssm_norm_g=v_ssm_norm_g, v_b_glu=v_b_glu, v_conf_conv_w=v_conf_conv_w, v_conf_conv_b=v_conf_conv_b, v_conf_ln_g=v_conf_ln_g, v_conf_ln_b=v_conf_ln_b, v_w_out=v_w_out, v_b_out=v_b_out, v_ln1_g=v_ln1_g, v_ln1_b=v_ln1_b, v_w_ple_gate=v_w_ple_gate, v_w_ple_proj=v_w_ple_proj, v_ln2_g=v_ln2_g, v_ln2_b=v_ln2_b)
    weights = {n: given[n] for n in TWIN_WEIGHTS}
    shared = {n: given[n] for n in SHARED_INPUTS}
    per_example = {n: given[n] for n in ['x', 'p']}
    grad_fn = _jax.value_and_grad(_loss, argnums=(0, 1))

    def one_microbatch(ex, loss_target):
        ex = dict(ex)
        diff = ex.pop(TWIN_DIFF_INPUT)
        return grad_fn(weights, diff, {**shared, **ex}, loss_target)

    if N_MICROBATCH == 1:
        loss, (grad_w, grad_x) = one_microbatch(per_example, given["loss_target"])
    else:
        def body(carry, xs):
            loss_sum, grad_sum = carry
            l_k, (gw_k, gx_k) = one_microbatch(xs[0], xs[1])
            with _jax.named_scope("update"):
                return (loss_sum + l_k, _jax.tree.map(_jnp.add, grad_sum, gw_k)), gx_k

        init = (_jnp.zeros((), _jnp.float32), _jax.tree.map(_jnp.zeros_like, weights))
        (loss, grad_w), grad_x = _jax.lax.scan(body, init, (per_example, given["loss_target"]))
    with _jax.named_scope("update"):
        delta_w, new_m, new_v = {}, {}, {}
        for n in TWIN_WEIGHTS:
            delta_w[n], new_m[n], new_v[n] = _adamw(weights[n], grad_w[n], given["m_" + n], given["v_" + n])
    return (loss, grad_x, *[grad_w[n] for n in TWIN_WEIGHTS], *[delta_w[n] for n in TWIN_WEIGHTS],
            *[new_m[n] for n in TWIN_WEIGHTS], *[new_v[n] for n in TWIN_WEIGHTS])
```

```python
import functools

import jax
import jax.numpy as jnp
from jax import lax
from jax.experimental import pallas as pl
from jax.experimental.pallas import tpu as pltpu

F32 = jnp.float32
BF16 = jnp.bfloat16

D_MODEL = 1024
D_PLE = 256
D_SSM = 1024
D_CONF = 1024
N_HEADS = 16
HEAD_DIM = 64
N_GROUPS = 2
N_STATE = 128
CHUNK = 128
SSM_K = 4
CONF_K = 31
D_XBC = D_SSM + 2 * N_GROUPS * N_STATE
D_IN = 5648
R_XBC, R_Z, R_GLU, R_CG, R_DT, D_INR = 0, 1536, 2560, 4608, 5632, 5760
LN_EPS = 1e-5
RMS_EPS = 1e-5
ALPHA = 2.0 ** 0.25
ADAM_LR, ADAM_B1, ADAM_B2, ADAM_EPS, ADAM_WD, ADAM_STEP = 0.001, 0.9, 0.999, 1e-08, 0.01, 10
NEG_BIG = -1e30
LANES = 128
SUBLANES = 8
VMEM_LIMIT = 56 * 1024 * 1024
HIGHEST = lax.Precision.HIGHEST
NT_DIMS = (((1,), (1,)), ((), ()))
TN_DIMS = (((0,), (0,)), ((), ()))


def _sig(v):
    return jax.nn.sigmoid(v)


def _dsilu(v, s):
    return s * (1.0 + v * (1.0 - s))


def _ln_stats(v):
    mu = jnp.mean(v, axis=-1, keepdims=True)
    c = v - mu
    var = jnp.mean(c * c, axis=-1, keepdims=True)
    rstd = lax.rsqrt(var + LN_EPS)
    return c * rstd, rstd


def _ln_bwd(dxhat, xhat, rstd):
    m1 = jnp.mean(dxhat, axis=-1, keepdims=True)
    m2 = jnp.mean(dxhat * xhat, axis=-1, keepdims=True)
    return rstd * (dxhat - m1 - xhat * m2)


def _softplus(v):
    return jnp.maximum(v, 0.0) + jnp.log1p(jnp.exp(-jnp.abs(v)))


def _colsum(v):
    return jnp.sum(v, axis=0, keepdims=True)


def _dot(a, b):
    return jnp.dot(a, b, preferred_element_type=F32)


def _dot_nt(a, b):
    return lax.dot_general(a, b, NT_DIMS, preferred_element_type=F32)


def _tile(tl, c, rev_of=None):
    if rev_of is None:
        return pl.BlockSpec((tl, c), lambda i: (i, 0))
    return pl.BlockSpec((tl, c), lambda i: (rev_of - 1 - i, 0))


def _full(shape, single=False):
    nd = len(shape)
    if single:
        return pl.BlockSpec(shape, lambda i: (0,) * nd, pipeline_mode=pl.Buffered(1))
    return pl.BlockSpec(shape, lambda i: (0,) * nd)


def _params(seq=True):
    return pltpu.CompilerParams(dimension_semantics=("arbitrary",) if seq else ("parallel",), vmem_limit_bytes=VMEM_LIMIT)


def _ln_inproj(x, g, b, w_r, tl):
    n_tok = x.shape[0]

    def body(x_ref, g_ref, b_ref, w_ref, h0b_ref, xbc_ref, z_ref, glu_ref, cg_ref, dtr_ref):
        xhat, _ = _ln_stats(x_ref[...])
        hb = (xhat * g_ref[...] + b_ref[...]).astype(BF16)
        h0b_ref[...] = hb
        xbc_ref[...] = _dot(hb, w_ref[:, R_XBC:R_Z])
        z_ref[...] = _dot(hb, w_ref[:, R_Z:R_GLU])
        glu_ref[...] = _dot(hb, w_ref[:, R_GLU:R_CG])
        cg_ref[...] = _dot(hb, w_ref[:, R_CG:R_DT])
        dtr_ref[...] = _dot(hb, w_ref[:, R_DT:D_INR])

    widths = (D_MODEL, D_XBC, D_SSM, 2 * D_CONF, D_CONF, LANES)
    dtypes = (BF16, F32, F32, F32, F32, F32)
    return pl.pallas_call(
        body, grid=(n_tok // tl,), name="ln_inproj",
        in_specs=[_tile(tl, D_MODEL), _full((1, D_MODEL)), _full((1, D_MODEL)), _full((D_MODEL, D_INR), single=True)],
        out_specs=[_tile(tl, w) for w in widths],
        out_shape=[jax.ShapeDtypeStruct((n_tok, w), dt) for w, dt in zip(widths, dtypes)],
        compiler_params=_params(seq=False),
    )(x, g, b, w_r)


def _chunk_common(adt_c):
    row = lax.broadcasted_iota(jnp.int32, (CHUNK, CHUNK), 0)
    col = lax.broadcasted_iota(jnp.int32, (CHUNK, CHUNK), 1)
    tril = row >= col
    acs = jnp.dot(tril.astype(F32), adt_c, precision=HIGHEST, preferred_element_type=F32)
    last = acs[CHUNK - 1:CHUNK, :]
    return dict(row=row, col=col, tril=tril, lo=col < HEAD_DIM, acs=acs, acs_t=acs.T, e=jnp.exp(acs),
                dec=jnp.exp(last - acs), cd=jnp.exp(last))


def _pick(cm, v, ha):
    return jnp.where(cm["lo"], v[:, ha:ha + 1], v[:, ha + 1:ha + 2])


def _decay_mask(cm, h):
    return jnp.exp(jnp.where(cm["tril"], cm["acs"][:, h:h + 1] - cm["acs_t"][h:h + 1, :], NEG_BIG))


def _ssd_fwd(xbc_in, z, dtr, cw, cb, dtb, alog, dsk, gnorm, tl):
    n_tok = xbc_in.shape[0]
    nq = tl // CHUNK

    def body(xin_ref, z_ref, dtr_ref, cw_ref, cb_ref, dtb_ref, alog_ref, dsk_ref, gn_ref,
             pre_ref, y_ref, yssm_ref, hprev_ref, buf, hst):
        @pl.when(pl.program_id(0) == 0)
        def _():
            buf[0:SUBLANES, :] = jnp.zeros((SUBLANES, D_XBC), F32)
            hst[...] = jnp.zeros_like(hst)

        buf[SUBLANES:SUBLANES + tl, :] = xin_ref[...]
        pre = cb_ref[...] + jnp.zeros((tl, D_XBC), F32)
        for k in range(SSM_K):
            off = SUBLANES - (SSM_K - 1) + k
            pre = pre + buf[off:off + tl, :] * cw_ref[k:k + 1, :]
        buf[0:SUBLANES, :] = buf[tl:tl + SUBLANES, :]
        pre_ref[...] = pre
        xbc = pre * _sig(pre)
        dt = _softplus(dtr_ref[...] + dtb_ref[...])
        a = -jnp.exp(alog_ref[...])
        adt = dt * a
        for q in range(nq):
            r0 = q * CHUNK
            cm = _chunk_common(adt[r0:r0 + CHUNK, :])
            dt_c = dt[r0:r0 + CHUNK, :]
            for g in range(N_GROUPS):
                bg = xbc[r0:r0 + CHUNK, D_SSM + g * N_STATE:D_SSM + (g + 1) * N_STATE].astype(BF16)
                cg_ = xbc[r0:r0 + CHUNK, D_SSM + (N_GROUPS + g) * N_STATE:D_SSM + (N_GROUPS + g + 1) * N_STATE].astype(BF16)
                gm = _dot_nt(cg_, bg)
                for k in range(N_HEADS // N_GROUPS // 2):
                    ha = (N_HEADS // N_GROUPS) * g + 2 * k
                    c0 = ha * HEAD_DIM
                    xh2 = xbc[r0:r0 + CHUNK, c0:c0 + LANES]
                    x2 = xh2 * _pick(cm, dt_c, ha)
                    x2b = x2.astype(BF16)
                    ya = _dot((gm * _decay_mask(cm, ha)).astype(BF16), x2b)
                    yb = _dot((gm * _decay_mask(cm, ha + 1)).astype(BF16), x2b)
                    h2 = hst[c0:c0 + LANES, :]
                    hprev_ref[q, c0:c0 + LANES, :] = h2
                    z2 = _dot_nt(cg_, h2.astype(BF16))
                    y2 = jnp.where(cm["lo"], ya, yb) + z2 * _pick(cm, cm["e"], ha) + dsk_ref[:, c0:c0 + LANES] * xh2
                    y_ref[r0:r0 + CHUNK, c0:c0 + LANES] = y2
                    s2 = _dot((x2 * _pick(cm, cm["dec"], ha)).T.astype(BF16), bg)
                    cd2 = jnp.where(cm["row"] < HEAD_DIM, cm["cd"][:, ha:ha + 1], cm["cd"][:, ha + 1:ha + 2])
                    hst[c0:c0 + LANES, :] = cd2 * h2 + s2
        yv = y_ref[...]
        zv = z_ref[...]
        yz = yv * (zv * _sig(zv))
        gw = D_SSM // N_GROUPS
        for g in range(N_GROUPS):
            seg = yz[:, g * gw:(g + 1) * gw]
            r = lax.rsqrt(jnp.mean(seg * seg, axis=-1, keepdims=True) + RMS_EPS)
            yssm_ref[:, g * gw:(g + 1) * gw] = (seg * r * gn_ref[:, g * gw:(g + 1) * gw]).astype(BF16)

    return pl.pallas_call(
        body, grid=(n_tok // tl,), name="ssd_fwd",
        in_specs=[_tile(tl, D_XBC), _tile(tl, D_SSM), _tile(tl, LANES), _full((SUBLANES, D_XBC)), _full((1, D_XBC)),
                  _full((1, LANES)), _full((1, LANES)), _full((1, D_SSM)), _full((1, D_SSM))],
        out_specs=[_tile(tl, D_XBC), _tile(tl, D_SSM), _tile(tl, D_SSM),
                   pl.BlockSpec((nq, D_SSM, N_STATE), lambda i: (i, 0, 0))],
        out_shape=[jax.ShapeDtypeStruct((n_tok, D_XBC), F32), jax.ShapeDtypeStruct((n_tok, D_SSM), F32),
                   jax.ShapeDtypeStruct((n_tok, D_SSM), BF16), jax.ShapeDtypeStruct((n_tok // CHUNK, D_SSM, N_STATE), F32)],
        scratch_shapes=[pltpu.VMEM((tl + SUBLANES, D_XBC), F32), pltpu.VMEM((D_SSM, N_STATE), F32)],
        compiler_params=_params(),
    )(xbc_in, z, dtr, cw, cb, dtb, alog, dsk, gnorm)


def _ssd_bwd(dys, y, z, pre, xbc_in, dtr, hprev, cw, dtb, alog, dsk, gnorm, tl):
    n_tok = y.shape[0]
    n_t = n_tok // tl
    nq = tl // CHUNK

    def body(dys_ref, y_ref, z_ref, pre_ref, xin_ref, dtr_ref, hprev_ref, cw_ref, dtb_ref, alog_ref, dsk_ref, gn_ref,
             dxin_ref, dz_ref, ddtr_ref, dcw_ref, dcb_ref, dgn_ref, ddsk_ref, da_ref, ddtb_ref,
             dxs, dh, cs_s, dskc):
        i = pl.program_id(0)

        @pl.when(i == 0)
        def _():
            dcw_ref[...] = jnp.zeros_like(dcw_ref)
            dcb_ref[...] = jnp.zeros_like(dcb_ref)
            dgn_ref[...] = jnp.zeros_like(dgn_ref)
            da_ref[...] = jnp.zeros_like(da_ref)
            ddtb_ref[...] = jnp.zeros_like(ddtb_ref)
            dskc[...] = jnp.zeros_like(dskc)
            dh[...] = jnp.zeros_like(dh)
            dxs[tl:tl + SUBLANES, :] = jnp.zeros((SUBLANES, D_XBC), F32)

        yv = y_ref[...]
        zv = z_ref[...]
        dysv = dys_ref[...]
        sz = _sig(zv)
        silz = zv * sz
        yz = yv * silz
        gw = D_SSM // N_GROUPS
        dyz_parts = []
        for g in range(N_GROUPS):
            sl = slice(g * gw, (g + 1) * gw)
            seg = yz[:, sl]
            r = lax.rsqrt(jnp.mean(seg * seg, axis=-1, keepdims=True) + RMS_EPS)
            yzn = seg * r
            dgn_ref[:, sl] += _colsum(dysv[:, sl] * yzn)
            dyzn = dysv[:, sl] * gn_ref[:, sl]
            dyz_parts.append(r * (dyzn - yzn * jnp.mean(dyzn * yzn, axis=-1, keepdims=True)))
        dyz = jnp.concatenate(dyz_parts, axis=1)
        dy = dyz * silz
        dz_ref[...] = dyz * yv * _dsilu(zv, sz)

        prev = pre_ref[...]
        sp = _sig(prev)
        xbc = prev * sp
        dskc[...] += _colsum(dy * xbc[:, 0:D_SSM])
        dt_in = dtr_ref[...] + dtb_ref[...]
        dt = _softplus(dt_in)
        dsp = _sig(dt_in)
        a = -jnp.exp(alog_ref[...])
        adt = dt * a
        for q in reversed(range(nq)):
            r0 = q * CHUNK
            cm = _chunk_common(adt[r0:r0 + CHUNK, :])
            row, col, lo = cm["row"], cm["col"], cm["lo"]
            triu = (col >= row).astype(F32)
            dt_c = dt[r0:r0 + CHUNK, :]
            dacs = jnp.zeros((CHUNK, LANES), F32)
            ddtx = jnp.zeros((CHUNK, LANES), F32)
            cs_s[...] = jnp.zeros_like(cs_s)
            for g in range(N_GROUPS):
                bcol = D_SSM + g * N_STATE
                ccol = D_SSM + (N_GROUPS + g) * N_STATE
                bg = xbc[r0:r0 + CHUNK, bcol:bcol + N_STATE].astype(BF16)
                cg_ = xbc[r0:r0 + CHUNK, ccol:ccol + N_STATE].astype(BF16)
                gm = _dot_nt(cg_, bg)
                dgm = jnp.zeros((CHUNK, CHUNK), F32)
                dbg = jnp.zeros((CHUNK, N_STATE), F32)
                dcg = jnp.zeros((CHUNK, N_STATE), F32)
                for k in range(N_HEADS // N_GROUPS // 2):
                    ha = (N_HEADS // N_GROUPS) * g + 2 * k
                    hb = ha + 1
                    c0 = ha * HEAD_DIM
                    xh2 = xbc[r0:r0 + CHUNK, c0:c0 + LANES]
                    dt2 = _pick(cm, dt_c, ha)
                    x2 = xh2 * dt2
                    x2b = x2.astype(BF16)
                    la = _decay_mask(cm, ha)
                    lb = _decay_mask(cm, hb)
                    ma = gm * la
                    mb = gm * lb
                    dy2 = dy[r0:r0 + CHUNK, c0:c0 + LANES]
                    dy2b = dy2.astype(BF16)
                    dma = _dot_nt(jnp.where(lo, dy2, 0.0).astype(BF16), x2b)
                    dmb = _dot_nt(jnp.where(lo, 0.0, dy2).astype(BF16), x2b)
                    dx2 = jnp.where(lo, _dot(ma.T.astype(BF16), dy2b), _dot(mb.T.astype(BF16), dy2b))
                    qa = dma * ma
                    qb = dmb * mb
                    dgm = dgm + dma * la + dmb * lb
                    cs_s[ha:ha + 1, :] = _colsum(qa)
                    cs_s[hb:hb + 1, :] = _colsum(qb)
                    col_a = jnp.sum(qa, axis=1, keepdims=True)
                    col_b = jnp.sum(qb, axis=1, keepdims=True)
                    h2 = hprev_ref[q, c0:c0 + LANES, :]
                    h2b = h2.astype(BF16)
                    e2 = _pick(cm, cm["e"], ha)
                    yoff = _dot_nt(cg_, h2b) * e2
                    dz2 = dy2 * e2
                    dcg = dcg + _dot(dz2.astype(BF16), h2b)
                    t_e = dy2 * yoff
                    col_a = col_a + jnp.sum(jnp.where(lo, t_e, 0.0), axis=1, keepdims=True)
                    col_b = col_b + jnp.sum(jnp.where(lo, 0.0, t_e), axis=1, keepdims=True)
                    dhn = dh[c0:c0 + LANES, :]
                    dhnb = dhn.astype(BF16)
                    cd_a = cm["cd"][:, ha:ha + 1]
                    cd_b = cm["cd"][:, hb:hb + 1]
                    top = row < HEAD_DIM
                    hh = dhn * h2
                    dcd_a = jnp.sum(jnp.sum(jnp.where(top, hh, 0.0), axis=1, keepdims=True), axis=0, keepdims=True)
                    dcd_b = jnp.sum(jnp.sum(jnp.where(top, 0.0, hh), axis=1, keepdims=True), axis=0, keepdims=True)
                    dh[c0:c0 + LANES, :] = jnp.where(top, cd_a, cd_b) * dhn + _dot(dz2.T.astype(BF16), cg_)
                    w2 = _dot_nt(bg, dhnb)
                    dec2 = _pick(cm, cm["dec"], ha)
                    dx2 = dx2 + dec2 * w2
                    xw = x2 * w2
                    dec_a = cm["dec"][:, ha:ha + 1]
                    dec_b = cm["dec"][:, hb:hb + 1]
                    dd_a = jnp.sum(jnp.where(lo, xw, 0.0), axis=1, keepdims=True) * dec_a
                    dd_b = jnp.sum(jnp.where(lo, 0.0, xw), axis=1, keepdims=True) * dec_b
                    dbg = dbg + _dot((x2 * dec2).astype(BF16), dhnb)
                    is_last = row[:, 0:1] == CHUNK - 1
                    col_a = col_a - dd_a + jnp.where(is_last, dcd_a * cd_a + _colsum(dd_a), 0.0)
                    col_b = col_b - dd_b + jnp.where(is_last, dcd_b * cd_b + _colsum(dd_b), 0.0)
                    dacs = dacs + jnp.where(col == ha, col_a, 0.0) + jnp.where(col == hb, col_b, 0.0)
                    xhd = dx2 * xh2
                    ddtx = ddtx + jnp.where(col == ha, jnp.sum(jnp.where(lo, xhd, 0.0), axis=1, keepdims=True), 0.0) \
                        + jnp.where(col == hb, jnp.sum(jnp.where(lo, 0.0, xhd), axis=1, keepdims=True), 0.0)
                    dxs[r0:r0 + CHUNK, c0:c0 + LANES] = dx2 * dt2 + dsk_ref[:, c0:c0 + LANES] * dy2
                dgmb = dgm.astype(BF16)
                dxs[r0:r0 + CHUNK, bcol:bcol + N_STATE] = dbg + _dot(dgm.T.astype(BF16), cg_)
                dxs[r0:r0 + CHUNK, ccol:ccol + N_STATE] = dcg + _dot(dgmb, bg)
            dacs = dacs - cs_s[...].T
            dadt = jnp.dot(triu, dacs, precision=HIGHEST, preferred_element_type=F32)
            da_ref[...] += _colsum(dadt * dt_c)
            ddtr_c = (dadt * a + ddtx) * dsp[r0:r0 + CHUNK, :]
            ddtr_ref[r0:r0 + CHUNK, :] = ddtr_c
            ddtb_ref[...] += _colsum(ddtr_c)

        dpre = dxs[0:tl, :] * _dsilu(prev, sp)
        dxs[0:tl, :] = dpre
        dcb_ref[...] += _colsum(dpre)
        xin = xin_ref[...]
        dxin = jnp.zeros((tl, D_XBC), F32)
        for k in range(SSM_K):
            sh = dxs[SSM_K - 1 - k:SSM_K - 1 - k + tl, :]
            dxin = dxin + sh * cw_ref[k:k + 1, :]
            dcw_ref[k:k + 1, :] += _colsum(xin * sh)
        dxin_ref[...] = dxin
        dxs[tl:tl + SUBLANES, :] = dxs[0:SUBLANES, :]

        @pl.when(i == n_t - 1)
        def _():
            da_ref[...] = da_ref[...] * a
            sel = (lax.broadcasted_iota(jnp.int32, (D_SSM, LANES), 0) // HEAD_DIM
                   == lax.broadcasted_iota(jnp.int32, (D_SSM, LANES), 1)).astype(F32)
            rows = jnp.broadcast_to(dskc[...], (SUBLANES, D_SSM))
            ddsk_ref[...] = jnp.dot(rows, sel, precision=HIGHEST, preferred_element_type=F32)[0:1, :]

    rev = functools.partial(_tile, tl, rev_of=n_t)
    return pl.pallas_call(
        body, grid=(n_t,), name="ssd_bwd",
        in_specs=[rev(D_SSM), rev(D_SSM), rev(D_SSM), rev(D_XBC), rev(D_XBC), rev(LANES),
                  pl.BlockSpec((nq, D_SSM, N_STATE), lambda i: (n_t - 1 - i, 0, 0)),
                  _full((SUBLANES, D_XBC)), _full((1, LANES)), _full((1, LANES)), _full((1, D_SSM)), _full((1, D_SSM))],
        out_specs=[rev(D_XBC), rev(D_SSM), rev(LANES), _full((SUBLANES, D_XBC)), _full((1, D_XBC)), _full((1, D_SSM)),
                   _full((1, LANES)), _full((1, LANES)), _full((1, LANES))],
        out_shape=[jax.ShapeDtypeStruct((n_tok, D_XBC), F32), jax.ShapeDtypeStruct((n_tok, D_SSM), F32),
                   jax.ShapeDtypeStruct((n_tok, LANES), F32), jax.ShapeDtypeStruct((SUBLANES, D_XBC), F32),
                   jax.ShapeDtypeStruct((1, D_XBC), F32), jax.ShapeDtypeStruct((1, D_SSM), F32),
                   jax.ShapeDtypeStruct((1, LANES), F32), jax.ShapeDtypeStruct((1, LANES), F32),
                   jax.ShapeDtypeStruct((1, LANES), F32)],
        scratch_shapes=[pltpu.VMEM((tl + SUBLANES, D_XBC), F32), pltpu.VMEM((D_SSM, N_STATE), F32),
                        pltpu.VMEM((CHUNK, LANES), F32), pltpu.VMEM((1, D_SSM), F32)],
        compiler_params=_params(),
    )(dys, y, z, pre, xbc_in, dtr, hprev, cw, dtb, alog, dsk, gnorm)


CONF_HALO = 32


def _conf_fwd(glu, cgate, bglu, cw, cb, lg, lb, tl):
    n_tok = glu.shape[0]

    def body(glu_ref, cg_ref, bglu_ref, cw_ref, cb_ref, lg_ref, lb_ref, u0_ref, u1_ref, yc_ref, buf):
        @pl.when(pl.program_id(0) == 0)
        def _():
            buf[0:CONF_HALO, :] = jnp.zeros((CONF_HALO, D_CONF), F32)

        gl = glu_ref[...] + bglu_ref[...]
        u0 = gl[:, 0:D_CONF] * _sig(gl[:, D_CONF:2 * D_CONF])
        u0_ref[...] = u0
        buf[CONF_HALO:CONF_HALO + tl, :] = u0
        u1 = cb_ref[...] + jnp.zeros((tl, D_CONF), F32)
        for k in range(CONF_K):
            off = CONF_HALO - (CONF_K - 1) + k
            u1 = u1 + buf[off:off + tl, :] * cw_ref[k:k + 1, :]
        buf[0:CONF_HALO, :] = buf[tl:tl + CONF_HALO, :]
        u1_ref[...] = u1
        xhat, _ = _ln_stats(u1)
        n = xhat * lg_ref[...] + lb_ref[...]
        cgv = cg_ref[...]
        yc_ref[...] = (n * _sig(n) * (cgv * _sig(cgv))).astype(BF16)

    return pl.pallas_call(
        body, grid=(n_tok // tl,), name="conf_fwd",
        in_specs=[_tile(tl, 2 * D_CONF), _tile(tl, D_CONF), _full((1, 2 * D_CONF)), _full((CONF_HALO, D_CONF)),
                  _full((1, D_CONF)), _full((1, D_CONF)), _full((1, D_CONF))],
        out_specs=[_tile(tl, D_CONF)] * 3,
        out_shape=[jax.ShapeDtypeStruct((n_tok, D_CONF), F32), jax.ShapeDtypeStruct((n_tok, D_CONF), F32),
                   jax.ShapeDtypeStruct((n_tok, D_CONF), BF16)],
        scratch_shapes=[pltpu.VMEM((tl + CONF_HALO, D_CONF), F32)],
        compiler_params=_params(),
    )(glu, cgate, bglu, cw, cb, lg, lb)


def _conf_bwd(dyc, u0, u1, glu, cgate, bglu, cw, lg, lb, tl):
    n_tok = glu.shape[0]
    n_t = n_tok // tl

    def body(dyc_ref, u0_ref, u1_ref, glu_ref, cg_ref, bglu_ref, cw_ref, lg_ref, lb_ref,
             dglu_ref, dcg_ref, dcw_ref, dbglu_ref, small_ref, buf):
        @pl.when(pl.program_id(0) == 0)
        def _():
            dcw_ref[...] = jnp.zeros_like(dcw_ref)
            dbglu_ref[...] = jnp.zeros_like(dbglu_ref)
            small_ref[...] = jnp.zeros_like(small_ref)
            buf[tl:tl + CONF_HALO, :] = jnp.zeros((CONF_HALO, D_CONF), F32)

        xhat, rstd = _ln_stats(u1_ref[...])
        n = xhat * lg_ref[...] + lb_ref[...]
        sn = _sig(n)
        cgv = cg_ref[...]
        scg = _sig(cgv)
        dycv = dyc_ref[...]
        dcg_ref[...] = dycv * (n * sn) * _dsilu(cgv, scg)
        dn = dycv * (cgv * scg) * _dsilu(n, sn)
        small_ref[0:1, :] += _colsum(dn * xhat)
        small_ref[1:2, :] += _colsum(dn)
        du1 = _ln_bwd(dn * lg_ref[...], xhat, rstd)
        small_ref[2:3, :] += _colsum(du1)
        buf[0:tl, :] = du1
        u0 = u0_ref[...]
        du0 = jnp.zeros((tl, D_CONF), F32)
        for k in range(CONF_K):
            sh = buf[CONF_K - 1 - k:CONF_K - 1 - k + tl, :]
            du0 = du0 + sh * cw_ref[k:k + 1, :]
            dcw_ref[k:k + 1, :] += _colsum(u0 * sh)
        buf[tl:tl + CONF_HALO, :] = buf[0:CONF_HALO, :]
        gl = glu_ref[...] + bglu_ref[...]
        sg = _sig(gl[:, D_CONF:2 * D_CONF])
        dgv = du0 * sg
        dgg = du0 * gl[:, 0:D_CONF] * sg * (1.0 - sg)
        dglu_ref[:, 0:D_CONF] = dgv
        dglu_ref[:, D_CONF:2 * D_CONF] = dgg
        dbglu_ref[:, 0:D_CONF] += _colsum(dgv)
        dbglu_ref[:, D_CONF:2 * D_CONF] += _colsum(dgg)

    rev = functools.partial(_tile, tl, rev_of=n_t)
    return pl.pallas_call(
        body, grid=(n_t,), name="conf_bwd",
        in_specs=[rev(D_CONF), rev(D_CONF), rev(D_CONF), rev(2 * D_CONF), rev(D_CONF), _full((1, 2 * D_CONF)),
                  _full((CONF_HALO, D_CONF)), _full((1, D_CONF)), _full((1, D_CONF))],
        out_specs=[rev(2 * D_CONF), rev(D_CONF), _full((CONF_HALO, D_CONF)), _full((1, 2 * D_CONF)), _full((SUBLANES, D_CONF))],
        out_shape=[jax.ShapeDtypeStruct((n_tok, 2 * D_CONF), F32), jax.ShapeDtypeStruct((n_tok, D_CONF), F32),
                   jax.ShapeDtypeStruct((CONF_HALO, D_CONF), F32), jax.ShapeDtypeStruct((1, 2 * D_CONF), F32),
                   jax.ShapeDtypeStruct((SUBLANES, D_CONF), F32)],
        scratch_shapes=[pltpu.VMEM((tl + CONF_HALO, D_CONF), F32)],
        compiler_params=_params(),
    )(dyc, u0, u1, glu, cgate, bglu, cw, lg, lb)


def _tail(x, yssm, yconf, p, tgt, vec, w_out, w_out_t, wpg, wpg_t, wpp, tl):
    n_tok = x.shape[0]

    def body(x_ref, ys_ref, yc_ref, p_ref, t_ref, vec_ref, wo_ref, wot_ref, wg_ref, wgt_ref, wp_ref,
             dmix_ref, dr1_ref, dr1b_ref, h1b_ref, dgb_ref, dpb_ref, small_ref, loss_ref):
        @pl.when(pl.program_id(0) == 0)
        def _():
            small_ref[...] = jnp.zeros_like(small_ref)
            loss_ref[...] = jnp.zeros_like(loss_ref)

        xh0, _ = _ln_stats(x_ref[...])
        h0 = xh0 * vec_ref[0:1, :] + vec_ref[1:2, :]
        out = _dot(ys_ref[...], wo_ref[0:D_SSM, :]) + _dot(yc_ref[...], wo_ref[D_SSM:D_SSM + D_CONF, :]) + vec_ref[2:3, :]
        xh1, rstd1 = _ln_stats(ALPHA * h0 + out)
        h1 = xh1 * vec_ref[3:4, :] + vec_ref[4:5, :]
        h1b = h1.astype(BF16)
        h1b_ref[...] = h1b
        gate = _sig(_dot(h1b, wg_ref[...]))
        ple = _dot(p_ref[...].astype(BF16), wp_ref[...])
        xh2, rstd2 = _ln_stats(ALPHA * h1 + gate * ple)
        h2 = xh2 * vec_ref[5:6, :] + vec_ref[6:7, :]
        diff = h2 - t_ref[...]
        part = jnp.sum(jnp.sum(diff * diff, axis=1, keepdims=True), axis=0, keepdims=True) * (0.5 / D_MODEL)
        loss_ref[...] += jnp.broadcast_to(part, loss_ref.shape)
        dh2 = diff * (1.0 / D_MODEL)
        small_ref[3:4, :] += _colsum(dh2 * xh2)
        small_ref[4:5, :] += _colsum(dh2)
        dr2 = _ln_bwd(dh2 * vec_ref[5:6, :], xh2, rstd2)
        dgpre = (dr2 * ple * gate * (1.0 - gate)).astype(BF16)
        dgb_ref[...] = dgpre
        dpb_ref[...] = (dr2 * gate).astype(BF16)
        dh1 = ALPHA * dr2 + _dot(dgpre, wgt_ref[...])
        small_ref[1:2, :] += _colsum(dh1 * xh1)
        small_ref[2:3, :] += _colsum(dh1)
        dr1 = _ln_bwd(dh1 * vec_ref[3:4, :], xh1, rstd1)
        small_ref[0:1, :] += _colsum(dr1)
        dr1_ref[...] = dr1
        dr1b = dr1.astype(BF16)
        dr1b_ref[...] = dr1b
        dmix_ref[...] = _dot(dr1b, wot_ref[...])

    d_mix = D_SSM + D_CONF
    return pl.pallas_call(
        body, grid=(n_tok // tl,), name="tail",
        in_specs=[_tile(tl, D_MODEL), _tile(tl, D_SSM), _tile(tl, D_CONF), _tile(tl, D_PLE), _tile(tl, D_MODEL),
                  _full((SUBLANES, D_MODEL)), _full((d_mix, D_MODEL), True), _full((D_MODEL, d_mix), True),
                  _full((D_MODEL, D_MODEL), True), _full((D_MODEL, D_MODEL), True), _full((D_PLE, D_MODEL), True)],
        out_specs=[_tile(tl, d_mix), _tile(tl, D_MODEL), _tile(tl, D_MODEL), _tile(tl, D_MODEL), _tile(tl, D_MODEL),
                   _tile(tl, D_MODEL), _full((SUBLANES, D_MODEL)), _full((SUBLANES, LANES))],
        out_shape=[jax.ShapeDtypeStruct((n_tok, d_mix), F32), jax.ShapeDtypeStruct((n_tok, D_MODEL), F32),
                   jax.ShapeDtypeStruct((n_tok, D_MODEL), BF16), jax.ShapeDtypeStruct((n_tok, D_MODEL), BF16),
                   jax.ShapeDtypeStruct((n_tok, D_MODEL), BF16), jax.ShapeDtypeStruct((n_tok, D_MODEL), BF16),
                   jax.ShapeDtypeStruct((SUBLANES, D_MODEL), F32), jax.ShapeDtypeStruct((SUBLANES, LANES), F32)],
        compiler_params=_params(),
    )(x, yssm, yconf, p, tgt, vec, w_out, w_out_t, wpg, wpg_t, wpp)


def _inproj_bwd(dxin, dz, dglu, dcg, ddtr, dr1, x, g, b, w_rt, tl):
    n_tok = x.shape[0]

    def body(dxin_ref, dz_ref, dglu_ref, dcg_ref, ddtr_ref, dr1_ref, x_ref, g_ref, b_ref, w_ref,
             dx_ref, dpb_ref, small_ref):
        @pl.when(pl.program_id(0) == 0)
        def _():
            small_ref[...] = jnp.zeros_like(small_ref)

        dh0 = ALPHA * dr1_ref[...]
        for ref, lo, hi in ((dxin_ref, R_XBC, R_Z), (dz_ref, R_Z, R_GLU), (dglu_ref, R_GLU, R_CG), (dcg_ref, R_CG, R_DT),
                            (ddtr_ref, R_DT, D_INR)):
            piece = ref[...].astype(BF16)
            dpb_ref[:, lo:hi] = piece
            dh0 = dh0 + _dot(piece, w_ref[lo:hi, :])
        xhat, rstd = _ln_stats(x_ref[...])
        small_ref[0:1, :] += _colsum(dh0 * xhat)
        small_ref[1:2, :] += _colsum(dh0)
        dx_ref[...] = _ln_bwd(dh0 * g_ref[...], xhat, rstd)

    return pl.pallas_call(
        body, grid=(n_tok // tl,), name="inproj_bwd",
        in_specs=[_tile(tl, D_XBC), _tile(tl, D_SSM), _tile(tl, 2 * D_CONF), _tile(tl, D_CONF), _tile(tl, LANES),
                  _tile(tl, D_MODEL), _tile(tl, D_MODEL), _full((1, D_MODEL)), _full((1, D_MODEL)),
                  _full((D_INR, D_MODEL), True)],
        out_specs=[_tile(tl, D_MODEL), _tile(tl, D_INR), _full((SUBLANES, D_MODEL))],
        out_shape=[jax.ShapeDtypeStruct((n_tok, D_MODEL), F32), jax.ShapeDtypeStruct((n_tok, D_INR), BF16),
                   jax.ShapeDtypeStruct((SUBLANES, D_MODEL), F32)],
        compiler_params=_params(),
    )(dxin, dz, dglu, dcg, ddtr, dr1, x, g, b, w_rt)


def _tn_matmul(a, b, name, tn, tl):
    n_tok, m = a.shape
    n = b.shape[1]

    def body(a_ref, b_ref, o_ref):
        @pl.when(pl.program_id(1) == 0)
        def _():
            o_ref[...] = jnp.zeros_like(o_ref)

        o_ref[...] += lax.dot_general(a_ref[...], b_ref[...], TN_DIMS, preferred_element_type=F32)

    return pl.pallas_call(
        body, grid=(n // tn, n_tok // tl), name=name,
        in_specs=[pl.BlockSpec((tl, m), lambda j, l: (l, 0)), pl.BlockSpec((tl, tn), lambda j, l: (l, j))],
        out_specs=pl.BlockSpec((m, tn), lambda j, l: (0, j)),
        out_shape=jax.ShapeDtypeStruct((m, n), F32),
        compiler_params=pltpu.CompilerParams(dimension_semantics=("parallel", "arbitrary"), vmem_limit_bytes=VMEM_LIMIT),
    )(a, b)


def _pad_rows(a, rows):
    return jnp.pad(a, ((0, rows - a.shape[0]), (0, 0)))


def _pad_lanes(a):
    return jnp.pad(a, ((0, 0), (0, LANES - a.shape[1])))


def _local_grads(x, p, tgt, w_in_b, w_out_b, wpg_b, wpp_b, ssm_cw, conf_cw, sm):
    n_tok = x.shape[0]
    tl = min(256, n_tok)
    row = lambda v: v.reshape(1, -1)
    w_r = jnp.concatenate([w_in_b[:, 0:1024], w_in_b[:, 2048:2560], w_in_b[:, 1024:2048], w_in_b[:, 2576:4624],
                           w_in_b[:, 4624:5648], w_in_b[:, 2560:2576], jnp.zeros((D_MODEL, LANES - N_HEADS), BF16)], axis=1)
    g_e, b_e = row(sm["ln_emb_g"]), row(sm["ln_emb_b"])
    h0b, xbc_in, z, glu, cgate, dtr = _ln_inproj(x, g_e, b_e, w_r, tl)

    cw4 = _pad_rows(ssm_cw, SUBLANES)
    dtb, alog = _pad_lanes(sm["dt_bias"]), _pad_lanes(sm["a_log"])
    dsk = jnp.repeat(sm["d_skip"], HEAD_DIM, axis=1)
    pre, y, yssm, hprev = _ssd_fwd(xbc_in, z, dtr, cw4, sm["ssm_conv_b"], dtb, alog, dsk, sm["ssm_norm_g"], tl)

    cw31 = _pad_rows(conf_cw, CONF_HALO)
    u0, u1, yconf = _conf_fwd(glu, cgate, sm["b_glu"], cw31, sm["conf_conv_b"], sm["conf_ln_g"], sm["conf_ln_b"], tl)

    vec = jnp.concatenate([g_e, b_e, sm["b_out"], sm["ln1_g"], sm["ln1_b"], sm["ln2_g"], sm["ln2_b"],
                           jnp.zeros((1, D_MODEL), F32)], axis=0)
    dmix, dr1, dr1b, h1b, dgb, dpb, small_t, loss = _tail(
        x, yssm, yconf, p, tgt, vec, w_out_b, w_out_b.T, wpg_b, wpg_b.T, wpp_b, tl)

    dglu, dcg, dcw31, dbglu, small_c = _conf_bwd(dmix[:, D_SSM:], u0, u1, glu, cgate, sm["b_glu"], cw31,
                                                  sm["conf_ln_g"], sm["conf_ln_b"], tl)
    dxin, dz, ddtr, dcw4, dcb4, dgn, ddsk, dalog, ddtb = _ssd_bwd(
        dmix[:, :D_SSM], y, z, pre, xbc_in, dtr, hprev, cw4, dtb, alog, dsk, sm["ssm_norm_g"], tl)
    dx, dprojb, small_e = _inproj_bwd(dxin, dz, dglu, dcg, ddtr, dr1, x, g_e, b_e, w_r.T, tl)

    tlm = min(512, n_tok)
    dw_r = _tn_matmul(h0b, dprojb, "dw_in", 640, tlm)
    dw_in = jnp.concatenate([dw_r[:, 0:1024], dw_r[:, 1536:2560], dw_r[:, 1024:1536], dw_r[:, 5632:5648],
                             dw_r[:, 2560:4608], dw_r[:, 4608:5632]], axis=1)
    dw_out = jnp.concatenate([_tn_matmul(yssm, dr1b, "dw_out_ssm", 512, tlm),
                              _tn_matmul(yconf, dr1b, "dw_out_conf", 512, tlm)], axis=0)
    dwpg = _tn_matmul(h1b, dgb, "dw_ple_gate", 512, tlm)
    dwpp = _tn_matmul(p.astype(BF16), dpb, "dw_ple_proj", 512, tlm)
    grads = dict(
        ln_emb_g=small_e[0], ln_emb_b=small_e[1], w_in=dw_in, ssm_conv_w=dcw4[0:SSM_K], ssm_conv_b=dcb4,
        dt_bias=ddtb[:, 0:N_HEADS], a_log=dalog[:, 0:N_HEADS], d_skip=ddsk[:, 0:N_HEADS], ssm_norm_g=dgn, b_glu=dbglu,
        conf_conv_w=dcw31[0:CONF_K], conf_conv_b=small_c[2:3], conf_ln_g=small_c[0:1], conf_ln_b=small_c[1:2],
        w_out=dw_out, b_out=small_t[0:1], ln1_g=small_t[1:2], ln1_b=small_t[2:3], w_ple_gate=dwpg, w_ple_proj=dwpp,
        ln2_g=small_t[3:4], ln2_b=small_t[4:5])
    return loss[0, 0], dx, grads


N_CHIPS = 4
N_DEV = 8
W_IN_SH = D_IN // N_CHIPS
ROWS_W_IN = D_MODEL * W_IN_SH // D_MODEL
ROWS_W_OUT = (D_SSM + D_CONF) // N_CHIPS
ROWS_WPG = D_MODEL // N_CHIPS
ROWS_WPP = D_PLE * (D_MODEL // N_CHIPS) // D_MODEL
ROWS_CW4 = 2
ROWS_CW31 = 8
ROWS_MAT = ROWS_W_IN + ROWS_W_OUT + ROWS_WPG + ROWS_WPP
ROWS_PACK = 2256
ROWS_CONV = 16
HALF = D_MODEL // 2
SMALL_ROWS = 24
SMALL_LAYOUT = (("ln_emb_g", 0, 1), ("ln_emb_b", 1, 1), ("ssm_conv_b", 2, 2), ("dt_bias", 4, 1), ("a_log", 5, 1),
                ("d_skip", 6, 1), ("ssm_norm_g", 7, 1), ("b_glu", 8, 2), ("conf_conv_b", 10, 1), ("conf_ln_g", 11, 1),
                ("conf_ln_b", 12, 1), ("b_out", 13, 1), ("ln1_g", 14, 1), ("ln1_b", 15, 1), ("ln2_g", 16, 1), ("ln2_b", 17, 1))
SHARDED = ("w_in", "w_out", "w_ple_gate", "w_ple_proj", "ssm_conv_w", "conf_conv_w")


def _rows_of(v, rows):
    flat = v.reshape(-1)
    return jnp.pad(flat, (0, rows * D_MODEL - flat.shape[0])).reshape(rows, D_MODEL)


def _pack_small(d):
    parts = [_rows_of(d[n], r) for n, _, r in SMALL_LAYOUT]
    parts.append(jnp.zeros((SMALL_ROWS - 18, D_MODEL), F32))
    return jnp.concatenate(parts, axis=0)


def _unpack_small(a, like):
    return {n: a[r0:r0 + r].reshape(-1)[:like[n].size].reshape(like[n].shape) for n, r0, r in SMALL_LAYOUT}


def _shards_as_rows(full, dtype):
    w_in = full["w_in"].reshape(D_MODEL, N_CHIPS, W_IN_SH).transpose(1, 0, 2).reshape(N_CHIPS, ROWS_W_IN, D_MODEL)
    w_out = full["w_out"].reshape(N_CHIPS, ROWS_W_OUT, D_MODEL)
    wpg = full["w_ple_gate"].reshape(N_CHIPS, ROWS_WPG, D_MODEL)
    wpp = full["w_ple_proj"].reshape(D_PLE, N_CHIPS, D_MODEL // N_CHIPS).transpose(1, 0, 2).reshape(N_CHIPS, ROWS_WPP, D_MODEL)
    cw4 = full["ssm_conv_w"].reshape(SSM_K, N_CHIPS, D_XBC // N_CHIPS).transpose(1, 0, 2).reshape(N_CHIPS, -1)
    cw4 = jnp.pad(cw4, ((0, 0), (0, ROWS_CW4 * D_MODEL - cw4.shape[1]))).reshape(N_CHIPS, ROWS_CW4, D_MODEL)
    cw31 = full["conf_conv_w"].reshape(CONF_K, N_CHIPS, D_CONF // N_CHIPS).transpose(1, 0, 2).reshape(N_CHIPS, -1)
    cw31 = jnp.pad(cw31, ((0, 0), (0, ROWS_CW31 * D_MODEL - cw31.shape[1]))).reshape(N_CHIPS, ROWS_CW31, D_MODEL)
    pad = jnp.zeros((N_CHIPS, ROWS_PACK - ROWS_MAT - ROWS_CW4 - ROWS_CW31, D_MODEL), dtype)
    return jnp.concatenate([a.astype(dtype) for a in (w_in, w_out, wpg, wpp, cw4, cw31)] + [pad], axis=1)


def _shard_from_rows(rows):
    r = 0
    out = {}
    for n, k, shape in (("w_in", ROWS_W_IN, (1, D_MODEL, W_IN_SH)), ("w_out", ROWS_W_OUT, (1, ROWS_W_OUT, D_MODEL)),
                        ("w_ple_gate", ROWS_WPG, (1, ROWS_WPG, D_MODEL)), ("w_ple_proj", ROWS_WPP, (1, D_PLE, D_MODEL // N_CHIPS)),
                        ("ssm_conv_w", ROWS_CW4, (1, SSM_K, D_XBC // N_CHIPS)), ("conf_conv_w", ROWS_CW31, (1, CONF_K, D_CONF // N_CHIPS))):
        size = shape[1] * shape[2]
        out[n] = rows[r:r + k].reshape(-1)[:size].reshape(shape)
        r += k
    return out


def _my_place():
    return lax.axis_index("x"), lax.axis_index("y"), lax.axis_index("c")


MESH_ID = pl.DeviceIdType.MESH
ANY = pl.BlockSpec(memory_space=pl.ANY)
CHIP_FLIPS = ((1, 0), (0, 1), (1, 1))


def _gather_weights(mat_b, conv_f):
    def body(mat_ref, conv_ref, mat_all, conv_all, send_sems, recv_sems, loc_sems):
        x, y, c = _my_place()
        s = 2 * x + y
        own = [pltpu.make_async_copy(mat_ref, mat_all.at[s], loc_sems.at[0]),
               pltpu.make_async_copy(conv_ref, conv_all.at[s], loc_sems.at[1])]
        sends = []
        for k, (fx, fy) in enumerate(CHIP_FLIPS):
            peer = (x ^ fx, y ^ fy, c)
            sends.append(pltpu.make_async_remote_copy(mat_ref, mat_all.at[s], send_sems.at[k, 0], recv_sems.at[k, 0],
                                                      device_id=peer, device_id_type=MESH_ID))
            sends.append(pltpu.make_async_remote_copy(conv_ref, conv_all.at[s], send_sems.at[k, 1], recv_sems.at[k, 1],
                                                      device_id=peer, device_id_type=MESH_ID))
        for cp in own + sends:
            cp.start()
        for cp in sends:
            cp.wait()
        for cp in own:
            cp.wait()

    return pl.pallas_call(
        body, name="gather_weights", in_specs=[ANY, ANY], out_specs=[ANY, ANY],
        out_shape=[jax.ShapeDtypeStruct((N_CHIPS,) + mat_b.shape, mat_b.dtype),
                   jax.ShapeDtypeStruct((N_CHIPS,) + conv_f.shape, conv_f.dtype)],
        scratch_shapes=[pltpu.SemaphoreType.DMA((3, 2)), pltpu.SemaphoreType.DMA((3, 2)), pltpu.SemaphoreType.DMA((2,))],
    )(mat_b, conv_f)


def _exchange_cores(gpack, spack):
    def body(g_ref, s_ref, mine_ref, theirs_ref, small_all, send_sems, recv_sems, loc_sems):
        x, y, c = _my_place()
        me = 4 * x + 2 * y + c
        keep = pl.multiple_of(c * HALF, HALF)
        give = pl.multiple_of((1 - c) * HALF, HALF)
        own = [pltpu.make_async_copy(g_ref.at[:, :, pl.ds(keep, HALF)], mine_ref, loc_sems.at[0]),
               pltpu.make_async_copy(s_ref, small_all.at[me], loc_sems.at[1])]
        sends = [pltpu.make_async_remote_copy(g_ref.at[:, :, pl.ds(give, HALF)], theirs_ref, send_sems.at[0], recv_sems.at[0],
                                              device_id=(x, y, 1 - c), device_id_type=MESH_ID)]
        for m in range(1, N_DEV):
            peer = (x ^ (m >> 2), y ^ ((m >> 1) & 1), c ^ (m & 1))
            sends.append(pltpu.make_async_remote_copy(s_ref, small_all.at[me], send_sems.at[m], recv_sems.at[m],
                                                      device_id=peer, device_id_type=MESH_ID))
        for cp in own + sends:
            cp.start()
        for cp in sends:
            cp.wait()
        for cp in own:
            cp.wait()

    half = jax.ShapeDtypeStruct((N_CHIPS, ROWS_PACK, HALF), F32)
    return pl.pallas_call(
        body, name="exchange_cores", in_specs=[ANY, ANY], out_specs=[ANY, ANY, ANY],
        out_shape=[half, half, jax.ShapeDtypeStruct((N_DEV, SMALL_ROWS, D_MODEL), F32)],
        scratch_shapes=[pltpu.SemaphoreType.DMA((N_DEV,)), pltpu.SemaphoreType.DMA((N_DEV,)), pltpu.SemaphoreType.DMA((2,))],
    )(gpack, spack)


def _exchange_chips(psum):
    def body(p_ref, got_ref, send_sems, recv_sems, loc_sem):
        x, y, c = _my_place()
        s = 2 * x + y
        own = pltpu.make_async_copy(p_ref.at[s], got_ref.at[s], loc_sem)
        sends = []
        for k, (fx, fy) in enumerate(CHIP_FLIPS):
            tx, ty = x ^ fx, y ^ fy
            sends.append(pltpu.make_async_remote_copy(p_ref.at[2 * tx + ty], got_ref.at[s], send_sems.at[k], recv_sems.at[k],
                                                      device_id=(tx, ty, c), device_id_type=MESH_ID))
        own.start()
        for cp in sends:
            cp.start()
        for cp in sends:
            cp.wait()
        own.wait()

    return pl.pallas_call(
        body, name="exchange_chips", in_specs=[ANY], out_specs=ANY,
        out_shape=jax.ShapeDtypeStruct(psum.shape, psum.dtype),
        scratch_shapes=[pltpu.SemaphoreType.DMA((3,)), pltpu.SemaphoreType.DMA((3,)), pltpu.SemaphoreType.DMA],
    )(psum)


def _share_halves(tot):
    def body(t_ref, both_ref, send_sem, recv_sem, loc_sem):
        x, y, c = _my_place()
        own = pltpu.make_async_copy(t_ref, both_ref.at[c], loc_sem)
        send = pltpu.make_async_remote_copy(t_ref, both_ref.at[c], send_sem, recv_sem, device_id=(x, y, 1 - c),
                                            device_id_type=MESH_ID)
        own.start()
        send.start()
        send.wait()
        own.wait()

    return pl.pallas_call(
        body, name="share_halves", in_specs=[ANY], out_specs=ANY,
        out_shape=jax.ShapeDtypeStruct((2,) + tot.shape, tot.dtype),
        scratch_shapes=[pltpu.SemaphoreType.DMA, pltpu.SemaphoreType.DMA, pltpu.SemaphoreType.DMA],
    )(tot)


SUM_ROWS = 376


def _add_pair(a, b):
    def body(a_ref, b_ref, o_ref):
        o_ref[...] = a_ref[...] + b_ref[...]

    spec = pl.BlockSpec((N_CHIPS, SUM_ROWS, HALF), lambda i: (0, i, 0))
    return pl.pallas_call(body, grid=(ROWS_PACK // SUM_ROWS,), name="add_cores", in_specs=[spec, spec], out_specs=spec,
                          out_shape=jax.ShapeDtypeStruct(a.shape, F32), compiler_params=_params(seq=False))(a, b)


def _add_chips(got):
    def body(g_ref, o_ref):
        o_ref[...] = (g_ref[0] + g_ref[1]) + (g_ref[2] + g_ref[3])

    return pl.pallas_call(
        body, grid=(ROWS_PACK // SUM_ROWS,), name="add_chips",
        in_specs=[pl.BlockSpec((N_CHIPS, SUM_ROWS, HALF), lambda i: (0, i, 0))],
        out_specs=pl.BlockSpec((SUM_ROWS, HALF), lambda i: (i, 0)),
        out_shape=jax.ShapeDtypeStruct((ROWS_PACK, HALF), F32), compiler_params=_params(seq=False))(got)


def _adam_math(w, g, m, v):
    m = ADAM_B1 * m + (1.0 - ADAM_B1) * g
    v = ADAM_B2 * v + (1.0 - ADAM_B2) * (g * g)
    m_hat = m / (1.0 - ADAM_B1 ** ADAM_STEP)
    v_hat = v / (1.0 - ADAM_B2 ** ADAM_STEP)
    return -ADAM_LR * (m_hat / (jnp.sqrt(v_hat) + ADAM_EPS) + ADAM_WD * w), m, v


def _adam(w, g, m, v, name):
    rows, cols = w.shape
    tr = rows if rows <= 256 else 256

    def body(w_ref, g_ref, m_ref, v_ref, d_ref, nm_ref, nv_ref):
        d_ref[...], nm_ref[...], nv_ref[...] = _adam_math(w_ref[...], g_ref[...], m_ref[...], v_ref[...])

    spec = pl.BlockSpec((tr, cols), lambda i: (i, 0))
    return pl.pallas_call(body, grid=(rows // tr,), name=name, in_specs=[spec] * 4, out_specs=[spec] * 3,
                          out_shape=[jax.ShapeDtypeStruct(w.shape, F32)] * 3, compiler_params=_params(seq=False))(w, g, m, v)


def _adam_small(w, parts, m, v):
    def body(w_ref, p_ref, m_ref, v_ref, g_ref, d_ref, nm_ref, nv_ref):
        g = p_ref[0]
        for k in range(1, N_DEV):
            g = g + p_ref[k]
        g_ref[...] = g
        d_ref[...], nm_ref[...], nv_ref[...] = _adam_math(w_ref[...], g, m_ref[...], v_ref[...])

    return pl.pallas_call(body, name="adam_small", out_shape=[jax.ShapeDtypeStruct(w.shape, F32)] * 4)(w, parts, m, v)


def kernel(x, p, ln_emb_g, ln_emb_b, w_in, ssm_conv_w, ssm_conv_b, dt_bias, a_log, d_skip, ssm_norm_g, b_glu, conf_conv_w, conf_conv_b, conf_ln_g, conf_ln_b, w_out, b_out, ln1_g, ln1_b, w_ple_gate, w_ple_proj, ln2_g, ln2_b, loss_target, m_ln_emb_g, m_ln_emb_b, m_w_in, m_ssm_conv_w, m_ssm_conv_b, m_dt_bias, m_a_log, m_d_skip, m_ssm_norm_g, m_b_glu, m_conf_conv_w, m_conf_conv_b, m_conf_ln_g, m_conf_ln_b, m_w_out, m_b_out, m_ln1_g, m_ln1_b, m_w_ple_gate, m_w_ple_proj, m_ln2_g, m_ln2_b, v_ln_emb_g, v_ln_emb_b, v_w_in, v_ssm_conv_w, v_ssm_conv_b, v_dt_bias, v_a_log, v_d_skip, v_ssm_norm_g, v_b_glu, v_conf_conv_w, v_conf_conv_b, v_conf_ln_g, v_conf_ln_b, v_w_out, v_b_out, v_ln1_g, v_ln1_b, v_w_ple_gate, v_w_ple_proj, v_ln2_g, v_ln2_b):
    order = ("ln_emb_g", "ln_emb_b", "w_in", "ssm_conv_w", "ssm_conv_b", "dt_bias", "a_log", "d_skip", "ssm_norm_g", "b_glu",
             "conf_conv_w", "conf_conv_b", "conf_ln_g", "conf_ln_b", "w_out", "b_out", "ln1_g", "ln1_b", "w_ple_gate",
             "w_ple_proj", "ln2_g", "ln2_b")
    w = dict(zip(order, (ln_emb_g, ln_emb_b, w_in, ssm_conv_w, ssm_conv_b, dt_bias, a_log, d_skip, ssm_norm_g, b_glu,
                         conf_conv_w, conf_conv_b, conf_ln_g, conf_ln_b, w_out, b_out, ln1_g, ln1_b, w_ple_gate, w_ple_proj,
                         ln2_g, ln2_b)))
    m = dict(zip(order, (m_ln_emb_g, m_ln_emb_b, m_w_in, m_ssm_conv_w, m_ssm_conv_b, m_dt_bias, m_a_log, m_d_skip,
                         m_ssm_norm_g, m_b_glu, m_conf_conv_w, m_conf_conv_b, m_conf_ln_g, m_conf_ln_b, m_w_out, m_b_out,
                         m_ln1_g, m_ln1_b, m_w_ple_gate, m_w_ple_proj, m_ln2_g, m_ln2_b)))
    v = dict(zip(order, (v_ln_emb_g, v_ln_emb_b, v_w_in, v_ssm_conv_w, v_ssm_conv_b, v_dt_bias, v_a_log, v_d_skip,
                         v_ssm_norm_g, v_b_glu, v_conf_conv_w, v_conf_conv_b, v_conf_ln_g, v_conf_ln_b, v_w_out, v_b_out,
                         v_ln1_g, v_ln1_b, v_w_ple_gate, v_w_ple_proj, v_ln2_g, v_ln2_b)))

    mat_b = jnp.concatenate([w["w_in"].astype(BF16).reshape(ROWS_W_IN, D_MODEL), w["w_out"][0].astype(BF16),
                             w["w_ple_gate"][0].astype(BF16), w["w_ple_proj"].astype(BF16).reshape(ROWS_WPP, D_MODEL),
                             jnp.zeros((ROWS_PACK - ROWS_MAT, D_MODEL), BF16)], axis=0)
    conv_f = jnp.concatenate([_rows_of(w["ssm_conv_w"], ROWS_CW4), _rows_of(w["conf_conv_w"], ROWS_CW31),
                              jnp.zeros((ROWS_CONV - ROWS_CW4 - ROWS_CW31, D_MODEL), F32)], axis=0)
    mat_all, conv_all = _gather_weights(mat_b, conv_f)
    r1, r2, r3 = ROWS_W_IN, ROWS_W_IN + ROWS_W_OUT, ROWS_W_IN + ROWS_W_OUT + ROWS_WPG
    w_in_b = mat_all[:, 0:r1].reshape(N_CHIPS, D_MODEL, W_IN_SH).transpose(1, 0, 2).reshape(D_MODEL, D_IN)
    w_out_b = mat_all[:, r1:r2].reshape(D_SSM + D_CONF, D_MODEL)
    wpg_b = mat_all[:, r2:r3].reshape(D_MODEL, D_MODEL)
    wpp_b = mat_all[:, r3:ROWS_MAT].reshape(N_CHIPS, D_PLE, D_MODEL // N_CHIPS).transpose(1, 0, 2).reshape(D_PLE, D_MODEL)
    cw4 = conv_all[:, 0:ROWS_CW4].reshape(N_CHIPS, -1)[:, :SSM_K * D_XBC // N_CHIPS]
    cw4 = cw4.reshape(N_CHIPS, SSM_K, D_XBC // N_CHIPS).transpose(1, 0, 2).reshape(SSM_K, D_XBC)
    cw31 = conv_all[:, ROWS_CW4:ROWS_CW4 + ROWS_CW31].reshape(N_CHIPS, -1)[:, :CONF_K * D_CONF // N_CHIPS]
    cw31 = cw31.reshape(N_CHIPS, CONF_K, D_CONF // N_CHIPS).transpose(1, 0, 2).reshape(CONF_K, D_CONF)

    small_names = [n for n, _, _ in SMALL_LAYOUT]
    sm = {n: w[n] for n in small_names}
    loss_part, dx, grads = _local_grads(x[0], p[0, 0], loss_target[0], w_in_b, w_out_b, wpg_b, wpp_b, cw4, cw31, sm)
    loss = lax.psum(loss_part, ("x", "y", "c"))

    gpack = _shards_as_rows({n: grads[n] for n in SHARDED}, F32)
    spack = _pack_small({n: grads[n] for n in small_names})
    mine, theirs, small_all = _exchange_cores(gpack, spack)
    got = _exchange_chips(_add_pair(mine, theirs))
    both = _share_halves(_add_chips(got))
    g_shard = _shard_from_rows(jnp.concatenate([both[0], both[1]], axis=1))

    out_g, out_d, out_m, out_v = {}, {}, {}, {}
    for n in SHARDED:
        shape = w[n].shape
        two_d = (shape[1], shape[2])
        d, nm, nv = _adam(w[n].reshape(two_d), g_shard[n].reshape(two_d), m[n].reshape(two_d), v[n].reshape(two_d), "adam_" + n)
        out_g[n], out_d[n], out_m[n], out_v[n] = g_shard[n], d.reshape(shape), nm.reshape(shape), nv.reshape(shape)
    g_s, d_s, m_s, v_s = _adam_small(_pack_small(sm), small_all, _pack_small({n: m[n] for n in small_names}),
                                     _pack_small({n: v[n] for n in small_names}))
    for dst, src in ((out_g, g_s), (out_d, d_s), (out_m, m_s), (out_v, v_s)):
        dst.update(_unpack_small(src, sm))
    return (loss, dx[None], *[out_g[n] for n in order], *[out_d[n] for n in order], *[out_m[n] for n in order],
            *[out_v[n] for n in order])
```

```python
import functools

import jax
import jax.numpy as jnp
from jax import lax
from jax.experimental import pallas as pl
from jax.experimental.pallas import tpu as pltpu

F32 = jnp.float32
BF16 = jnp.bfloat16

D_MODEL = 1024
D_PLE = 256
D_SSM = 1024
D_CONF = 1024
N_HEADS = 16
HEAD_DIM = 64
N_GROUPS = 2
N_STATE = 128
CHUNK = 128
SSM_K = 4
CONF_K = 31
D_XBC = D_SSM + 2 * N_GROUPS * N_STATE
D_IN = 5648
R_XBC, R_Z, R_GLU, R_CG, R_DT, D_INR = 0, 1536, 2560, 4608, 5632, 5760
LN_EPS = 1e-5
RMS_EPS = 1e-5
ALPHA = 2.0 ** 0.25
ADAM_LR, ADAM_B1, ADAM_B2, ADAM_EPS, ADAM_WD, ADAM_STEP = 0.001, 0.9, 0.999, 1e-08, 0.01, 10
NEG_BIG = -1e30
LANES = 128
SUBLANES = 8
VMEM_LIMIT = 56 * 1024 * 1024
HIGHEST = lax.Precision.HIGHEST
NT_DIMS = (((1,), (1,)), ((), ()))
TN_DIMS = (((0,), (0,)), ((), ()))


def _sig(v):
    return jax.nn.sigmoid(v)


def _dsilu(v, s):
    return s * (1.0 + v * (1.0 - s))


def _ln_stats(v):
    mu = jnp.mean(v, axis=-1, keepdims=True)
    c = v - mu
    var = jnp.mean(c * c, axis=-1, keepdims=True)
    rstd = lax.rsqrt(var + LN_EPS)
    return c * rstd, rstd


def _ln_bwd(dxhat, xhat, rstd):
    m1 = jnp.mean(dxhat, axis=-1, keepdims=True)
    m2 = jnp.mean(dxhat * xhat, axis=-1, keepdims=True)
    return rstd * (dxhat - m1 - xhat * m2)


def _softplus(v):
    return jnp.maximum(v, 0.0) + jnp.log1p(jnp.exp(-jnp.abs(v)))


def _colsum(v):
    return jnp.sum(v, axis=0, keepdims=True)


def _dot(a, b):
    return jnp.dot(a, b, preferred_element_type=F32)


def _dot_nt(a, b):
    return lax.dot_general(a, b, NT_DIMS, preferred_element_type=F32)


def _tile(tl, c, rev_of=None):
    if rev_of is None:
        return pl.BlockSpec((tl, c), lambda i: (i, 0))
    return pl.BlockSpec((tl, c), lambda i: (rev_of - 1 - i, 0))


def _full(shape, single=False):
    nd = len(shape)
    if single:
        return pl.BlockSpec(shape, lambda i: (0,) * nd, pipeline_mode=pl.Buffered(1))
    return pl.BlockSpec(shape, lambda i: (0,) * nd)


def _params(seq=True):
    return pltpu.CompilerParams(dimension_semantics=("arbitrary",) if seq else ("parallel",), vmem_limit_bytes=VMEM_LIMIT)


def _ln_inproj(x, g, b, w_r, tl):
    n_tok = x.shape[0]

    def body(x_ref, g_ref, b_ref, w_ref, h0b_ref, xbc_ref, z_ref, glu_ref, cg_ref, dtr_ref):
        xhat, _ = _ln_stats(x_ref[...])
        hb = (xhat * g_ref[...] + b_ref[...]).astype(BF16)
        h0b_ref[...] = hb
        xbc_ref[...] = _dot(hb, w_ref[:, R_XBC:R_Z])
        z_ref[...] = _dot(hb, w_ref[:, R_Z:R_GLU])
        glu_ref[...] = _dot(hb, w_ref[:, R_GLU:R_CG])
        cg_ref[...] = _dot(hb, w_ref[:, R_CG:R_DT])
        dtr_ref[...] = _dot(hb, w_ref[:, R_DT:D_INR])

    widths = (D_MODEL, D_XBC, D_SSM, 2 * D_CONF, D_CONF, LANES)
    dtypes = (BF16, F32, F32, F32, F32, F32)
    return pl.pallas_call(
        body, grid=(n_tok // tl,), name="ln_inproj",
        in_specs=[_tile(tl, D_MODEL), _full((1, D_MODEL)), _full((1, D_MODEL)), _full((D_MODEL, D_INR), single=True)],
        out_specs=[_tile(tl, w) for w in widths],
        out_shape=[jax.ShapeDtypeStruct((n_tok, w), dt) for w, dt in zip(widths, dtypes)],
        compiler_params=_params(seq=False),
    )(x, g, b, w_r)


def _chunk_common(adt_c):
    row = lax.broadcasted_iota(jnp.int32, (CHUNK, CHUNK), 0)
    col = lax.broadcasted_iota(jnp.int32, (CHUNK, CHUNK), 1)
    tril = row >= col
    acs = jnp.dot(tril.astype(F32), adt_c, precision=HIGHEST, preferred_element_type=F32)
    last = acs[CHUNK - 1:CHUNK, :]
    return dict(row=row, col=col, tril=tril, lo=col < HEAD_DIM, acs=acs, acs_t=acs.T, e=jnp.exp(acs),
                dec=jnp.exp(last - acs), cd=jnp.exp(last))


def _pick(cm, v, ha):
    return jnp.where(cm["lo"], v[:, ha:ha + 1], v[:, ha + 1:ha + 2])


def _decay_mask(cm, h):
    return jnp.exp(jnp.where(cm["tril"], cm["acs"][:, h:h + 1] - cm["acs_t"][h:h + 1, :], NEG_BIG))


def _ssd_fwd(xbc_in, z, dtr, cw, cb, dtb, alog, dsk, gnorm, tl):
    n_tok = xbc_in.shape[0]
    nq = tl // CHUNK

    def body(xin_ref, z_ref, dtr_ref, cw_ref, cb_ref, dtb_ref, alog_ref, dsk_ref, gn_ref,
             pre_ref, y_ref, yssm_ref, hprev_ref, buf, hst):
        @pl.when(pl.program_id(0) == 0)
        def _():
            buf[0:SUBLANES, :] = jnp.zeros((SUBLANES, D_XBC), F32)
            hst[...] = jnp.zeros_like(hst)

        buf[SUBLANES:SUBLANES + tl, :] = xin_ref[...]
        pre = cb_ref[...] + jnp.zeros((tl, D_XBC), F32)
        for k in range(SSM_K):
            off = SUBLANES - (SSM_K - 1) + k
            pre = pre + buf[off:off + tl, :] * cw_ref[k:k + 1, :]
        buf[0:SUBLANES, :] = buf[tl:tl + SUBLANES, :]
        pre_ref[...] = pre
        xbc = pre * _sig(pre)
        dt = _softplus(dtr_ref[...] + dtb_ref[...])
        a = -jnp.exp(alog_ref[...])
        adt = dt * a
        for q in range(nq):
            r0 = q * CHUNK
            cm = _chunk_common(adt[r0:r0 + CHUNK, :])
            dt_c = dt[r0:r0 + CHUNK, :]
            for g in range(N_GROUPS):
                bg = xbc[r0:r0 + CHUNK, D_SSM + g * N_STATE:D_SSM + (g + 1) * N_STATE].astype(BF16)
                cg_ = xbc[r0:r0 + CHUNK, D_SSM + (N_GROUPS + g) * N_STATE:D_SSM + (N_GROUPS + g + 1) * N_STATE].astype(BF16)
                gm = _dot_nt(cg_, bg)
                for k in range(N_HEADS // N_GROUPS // 2):
                    ha = (N_HEADS // N_GROUPS) * g + 2 * k
                    c0 = ha * HEAD_DIM
                    xh2 = xbc[r0:r0 + CHUNK, c0:c0 + LANES]
                    x2 = xh2 * _pick(cm, dt_c, ha)
                    x2b = x2.astype(BF16)
                    ya = _dot((gm * _decay_mask(cm, ha)).astype(BF16), x2b)
                    yb = _dot((gm * _decay_mask(cm, ha + 1)).astype(BF16), x2b)
                    h2 = hst[c0:c0 + LANES, :]
                    hprev_ref[q, c0:c0 + LANES, :] = h2
                    z2 = _dot_nt(cg_, h2.astype(BF16))
                    y2 = jnp.where(cm["lo"], ya, yb) + z2 * _pick(cm, cm["e"], ha) + dsk_ref[:, c0:c0 + LANES] * xh2
                    y_ref[r0:r0 + CHUNK, c0:c0 + LANES] = y2
                    s2 = _dot((x2 * _pick(cm, cm["dec"], ha)).T.astype(BF16), bg)
                    cd2 = jnp.where(cm["row"] < HEAD_DIM, cm["cd"][:, ha:ha + 1], cm["cd"][:, ha + 1:ha + 2])
                    hst[c0:c0 + LANES, :] = cd2 * h2 + s2
        yv = y_ref[...]
        zv = z_ref[...]
        yz = yv * (zv * _sig(zv))
        gw = D_SSM // N_GROUPS
        for g in range(N_GROUPS):
            seg = yz[:, g * gw:(g + 1) * gw]
            r = lax.rsqrt(jnp.mean(seg * seg, axis=-1, keepdims=True) + RMS_EPS)
            yssm_ref[:, g * gw:(g + 1) * gw] = (seg * r * gn_ref[:, g * gw:(g + 1) * gw]).astype(BF16)

    return pl.pallas_call(
        body, grid=(n_tok // tl,), name="ssd_fwd",
        in_specs=[_tile(tl, D_XBC), _tile(tl, D_SSM), _tile(tl, LANES), _full((SUBLANES, D_XBC)), _full((1, D_XBC)),
                  _full((1, LANES)), _full((1, LANES)), _full((1, D_SSM)), _full((1, D_SSM))],
        out_specs=[_tile(tl, D_XBC), _tile(tl, D_SSM), _tile(tl, D_SSM),
                   pl.BlockSpec((nq, D_SSM, N_STATE), lambda i: (i, 0, 0))],
        out_shape=[jax.ShapeDtypeStruct((n_tok, D_XBC), F32), jax.ShapeDtypeStruct((n_tok, D_SSM), F32),
                   jax.ShapeDtypeStruct((n_tok, D_SSM), BF16), jax.ShapeDtypeStruct((n_tok // CHUNK, D_SSM, N_STATE), F32)],
        scratch_shapes=[pltpu.VMEM((tl + SUBLANES, D_XBC), F32), pltpu.VMEM((D_SSM, N_STATE), F32)],
        compiler_params=_params(),
    )(xbc_in, z, dtr, cw, cb, dtb, alog, dsk, gnorm)


def _ssd_bwd(dys, y, z, pre, xbc_in, dtr, hprev, cw, dtb, alog, dsk, gnorm, tl):
    n_tok = y.shape[0]
    n_t = n_tok // tl
    nq = tl // CHUNK

    def body(dys_ref, y_ref, z_ref, pre_ref, xin_ref, dtr_ref, hprev_ref, cw_ref, dtb_ref, alog_ref, dsk_ref, gn_ref,
             dxin_ref, dz_ref, ddtr_ref, dcw_ref, dcb_ref, dgn_ref, ddsk_ref, da_ref, ddtb_ref,
             dxs, dh, cs_s, dskc):
        i = pl.program_id(0)

        @pl.when(i == 0)
        def _():
            dcw_ref[...] = jnp.zeros_like(dcw_ref)
            dcb_ref[...] = jnp.zeros_like(dcb_ref)
            dgn_ref[...] = jnp.zeros_like(dgn_ref)
            da_ref[...] = jnp.zeros_like(da_ref)
            ddtb_ref[...] = jnp.zeros_like(ddtb_ref)
            dskc[...] = jnp.zeros_like(dskc)
            dh[...] = jnp.zeros_like(dh)
            dxs[tl:tl + SUBLANES, :] = jnp.zeros((SUBLANES, D_XBC), F32)

        yv = y_ref[...]
        zv = z_ref[...]
        dysv = dys_ref[...]
        sz = _sig(zv)
        silz = zv * sz
        yz = yv * silz
        gw = D_SSM // N_GROUPS
        dyz_parts = []
        for g in range(N_GROUPS):
            sl = slice(g * gw, (g + 1) * gw)
            seg = yz[:, sl]
            r = lax.rsqrt(jnp.mean(seg * seg, axis=-1, keepdims=True) + RMS_EPS)
            yzn = seg * r
            dgn_ref[:, sl] += _colsum(dysv[:, sl] * yzn)
            dyzn = dysv[:, sl] * gn_ref[:, sl]
            dyz_parts.append(r * (dyzn - yzn * jnp.mean(dyzn * yzn, axis=-1, keepdims=True)))
        dyz = jnp.concatenate(dyz_parts, axis=1)
        dy = dyz * silz
        dz_ref[...] = dyz * yv * _dsilu(zv, sz)

        prev = pre_ref[...]
        sp = _sig(prev)
        xbc = prev * sp
        dskc[...] += _colsum(dy * xbc[:, 0:D_SSM])
        dt_in = dtr_ref[...] + dtb_ref[...]
        dt = _softplus(dt_in)
        dsp = _sig(dt_in)
        a = -jnp.exp(alog_ref[...])
        adt = dt * a
        for q in reversed(range(nq)):
            r0 = q * CHUNK
            cm = _chunk_common(adt[r0:r0 + CHUNK, :])
            row, col, lo = cm["row"], cm["col"], cm["lo"]
            triu = (col >= row).astype(F32)
            dt_c = dt[r0:r0 + CHUNK, :]
            dacs = jnp.zeros((CHUNK, LANES), F32)
            ddtx = jnp.zeros((CHUNK, LANES), F32)
            cs_s[...] = jnp.zeros_like(cs_s)
            for g in range(N_GROUPS):
                bcol = D_SSM + g * N_STATE
                ccol = D_SSM + (N_GROUPS + g) * N_STATE
                bg = xbc[r0:r0 + CHUNK, bcol:bcol + N_STATE].astype(BF16)
                cg_ = xbc[r0:r0 + CHUNK, ccol:ccol + N_STATE].astype(BF16)
                gm = _dot_nt(cg_, bg)
                dgm = jnp.zeros((CHUNK, CHUNK), F32)
                dbg = jnp.zeros((CHUNK, N_STATE), F32)
                dcg = jnp.zeros((CHUNK, N_STATE), F32)
                for k in range(N_HEADS // N_GROUPS // 2):
                    ha = (N_HEADS // N_GROUPS) * g + 2 * k
                    hb = ha + 1
                    c0 = ha * HEAD_DIM
                    xh2 = xbc[r0:r0 + CHUNK, c0:c0 + LANES]
                    dt2 = _pick(cm, dt_c, ha)
                    x2 = xh2 * dt2
                    x2b = x2.astype(BF16)
                    la = _decay_mask(cm, ha)
                    lb = _decay_mask(cm, hb)
                    ma = gm * la
                    mb = gm * lb
                    dy2 = dy[r0:r0 + CHUNK, c0:c0 + LANES]
                    dy2b = dy2.astype(BF16)
                    dma = _dot_nt(jnp.where(lo, dy2, 0.0).astype(BF16), x2b)
                    dmb = _dot_nt(jnp.where(lo, 0.0, dy2).astype(BF16), x2b)
                    dx2 = jnp.where(lo, _dot(ma.T.astype(BF16), dy2b), _dot(mb.T.astype(BF16), dy2b))
                    qa = dma * ma
                    qb = dmb * mb
                    dgm = dgm + dma * la + dmb * lb
                    cs_s[ha:ha + 1, :] = _colsum(qa)
                    cs_s[hb:hb + 1, :] = _colsum(qb)
                    col_a = jnp.sum(qa, axis=1, keepdims=True)
                    col_b = jnp.sum(qb, axis=1, keepdims=True)
                    h2 = hprev_ref[q, c0:c0 + LANES, :]
                    h2b = h2.astype(BF16)
                    e2 = _pick(cm, cm["e"], ha)
                    yoff = _dot_nt(cg_, h2b) * e2
                    dz2 = dy2 * e2
                    dcg = dcg + _dot(dz2.astype(BF16), h2b)
                    t_e = dy2 * yoff
                    col_a = col_a + jnp.sum(jnp.where(lo, t_e, 0.0), axis=1, keepdims=True)
                    col_b = col_b + jnp.sum(jnp.where(lo, 0.0, t_e), axis=1, keepdims=True)
                    dhn = dh[c0:c0 + LANES, :]
                    dhnb = dhn.astype(BF16)
                    cd_a = cm["cd"][:, ha:ha + 1]
                    cd_b = cm["cd"][:, hb:hb + 1]
                    top = row < HEAD_DIM
                    hh = dhn * h2
                    dcd_a = jnp.sum(jnp.sum(jnp.where(top, hh, 0.0), axis=1, keepdims=True), axis=0, keepdims=True)
                    dcd_b = jnp.sum(jnp.sum(jnp.where(top, 0.0, hh), axis=1, keepdims=True), axis=0, keepdims=True)
                    dh[c0:c0 + LANES, :] = jnp.where(top, cd_a, cd_b) * dhn + _dot(dz2.T.astype(BF16), cg_)
                    w2 = _dot_nt(bg, dhnb)
                    dec2 = _pick(cm, cm["dec"], ha)
                    dx2 = dx2 + dec2 * w2
                    xw = x2 * w2
                    dec_a = cm["dec"][:, ha:ha + 1]
                    dec_b = cm["dec"][:, hb:hb + 1]
                    dd_a = jnp.sum(jnp.where(lo, xw, 0.0), axis=1, keepdims=True) * dec_a
                    dd_b = jnp.sum(jnp.where(lo, 0.0, xw), axis=1, keepdims=True) * dec_b
                    dbg = dbg + _dot((x2 * dec2).astype(BF16), dhnb)
                    is_last = row[:, 0:1] == CHUNK - 1
                    col_a = col_a - dd_a + jnp.where(is_last, dcd_a * cd_a + _colsum(dd_a), 0.0)
                    col_b = col_b - dd_b + jnp.where(is_last, dcd_b * cd_b + _colsum(dd_b), 0.0)
                    dacs = dacs + jnp.where(col == ha, col_a, 0.0) + jnp.where(col == hb, col_b, 0.0)
                    xhd = dx2 * xh2
                    ddtx = ddtx + jnp.where(col == ha, jnp.sum(jnp.where(lo, xhd, 0.0), axis=1, keepdims=True), 0.0) \
                        + jnp.where(col == hb, jnp.sum(jnp.where(lo, 0.0, xhd), axis=1, keepdims=True), 0.0)
                    dxs[r0:r0 + CHUNK, c0:c0 + LANES] = dx2 * dt2 + dsk_ref[:, c0:c0 + LANES] * dy2
                dgmb = dgm.astype(BF16)
                dxs[r0:r0 + CHUNK, bcol:bcol + N_STATE] = dbg + _dot(dgm.T.astype(BF16), cg_)
                dxs[r0:r0 + CHUNK, ccol:ccol + N_STATE] = dcg + _dot(dgmb, bg)
            dacs = dacs - cs_s[...].T
            dadt = jnp.dot(triu, dacs, precision=HIGHEST, preferred_element_type=F32)
            da_ref[...] += _colsum(dadt * dt_c)
            ddtr_c = (dadt * a + ddtx) * dsp[r0:r0 + CHUNK, :]
            ddtr_ref[r0:r0 + CHUNK, :] = ddtr_c
            ddtb_ref[...] += _colsum(ddtr_c)

        dpre = dxs[0:tl, :] * _dsilu(prev, sp)
        dxs[0:tl, :] = dpre
        dcb_ref[...] += _colsum(dpre)
        xin = xin_ref[...]
        dxin = jnp.zeros((tl, D_XBC), F32)
        for k in range(SSM_K):
            sh = dxs[SSM_K - 1 - k:SSM_K - 1 - k + tl, :]
            dxin = dxin + sh * cw_ref[k:k + 1, :]
            dcw_ref[k:k + 1, :] += _colsum(xin * sh)
        dxin_ref[...] = dxin
        dxs[tl:tl + SUBLANES, :] = dxs[0:SUBLANES, :]

        @pl.when(i == n_t - 1)
        def _():
            da_ref[...] = da_ref[...] * a
            sel = (lax.broadcasted_iota(jnp.int32, (D_SSM, LANES), 0) // HEAD_DIM
                   == lax.broadcasted_iota(jnp.int32, (D_SSM, LANES), 1)).astype(F32)
            rows = jnp.broadcast_to(dskc[...], (SUBLANES, D_SSM))
            ddsk_ref[...] = jnp.dot(rows, sel, precision=HIGHEST, preferred_element_type=F32)[0:1, :]

    rev = functools.partial(_tile, tl, rev_of=n_t)
    return pl.pallas_call(
        body, grid=(n_t,), name="ssd_bwd",
        in_specs=[rev(D_SSM), rev(D_SSM), rev(D_SSM), rev(D_XBC), rev(D_XBC), rev(LANES),
                  pl.BlockSpec((nq, D_SSM, N_STATE), lambda i: (n_t - 1 - i, 0, 0)),
                  _full((SUBLANES, D_XBC)), _full((1, LANES)), _full((1, LANES)), _full((1, D_SSM)), _full((1, D_SSM))],
        out_specs=[rev(D_XBC), rev(D_SSM), rev(LANES), _full((SUBLANES, D_XBC)), _full((1, D_XBC)), _full((1, D_SSM)),
                   _full((1, LANES)), _full((1, LANES)), _full((1, LANES))],
        out_shape=[jax.ShapeDtypeStruct((n_tok, D_XBC), F32), jax.ShapeDtypeStruct((n_tok, D_SSM), F32),
                   jax.ShapeDtypeStruct((n_tok, LANES), F32), jax.ShapeDtypeStruct((SUBLANES, D_XBC), F32),
                   jax.ShapeDtypeStruct((1, D_XBC), F32), jax.ShapeDtypeStruct((1, D_SSM), F32),
                   jax.ShapeDtypeStruct((1, LANES), F32), jax.ShapeDtypeStruct((1, LANES), F32),
                   jax.ShapeDtypeStruct((1, LANES), F32)],
        scratch_shapes=[pltpu.VMEM((tl + SUBLANES, D_XBC), F32), pltpu.VMEM((D_SSM, N_STATE), F32),
                        pltpu.VMEM((CHUNK, LANES), F32), pltpu.VMEM((1, D_SSM), F32)],
        compiler_params=_params(),
    )(dys, y, z, pre, xbc_in, dtr, hprev, cw, dtb, alog, dsk, gnorm)


CONF_HALO = 32


def _conf_fwd(glu, cgate, bglu, cw, cb, lg, lb, tl):
    n_tok = glu.shape[0]

    def body(glu_ref, cg_ref, bglu_ref, cw_ref, cb_ref, lg_ref, lb_ref, u0_ref, u1_ref, yc_ref, buf):
        @pl.when(pl.program_id(0) == 0)
        def _():
            buf[0:CONF_HALO, :] = jnp.zeros((CONF_HALO, D_CONF), F32)

        gl = glu_ref[...] + bglu_ref[...]
        u0 = gl[:, 0:D_CONF] * _sig(gl[:, D_CONF:2 * D_CONF])
        u0_ref[...] = u0
        buf[CONF_HALO:CONF_HALO + tl, :] = u0
        u1 = cb_ref[...] + jnp.zeros((tl, D_CONF), F32)
        for k in range(CONF_K):
            off = CONF_HALO - (CONF_K - 1) + k
            u1 = u1 + buf[off:off + tl, :] * cw_ref[k:k + 1, :]
        buf[0:CONF_HALO, :] = buf[tl:tl + CONF_HALO, :]
        u1_ref[...] = u1
        xhat, _ = _ln_stats(u1)
        n = xhat * lg_ref[...] + lb_ref[...]
        cgv = cg_ref[...]
        yc_ref[...] = (n * _sig(n) * (cgv * _sig(cgv))).astype(BF16)

    return pl.pallas_call(
        body, grid=(n_tok // tl,), name="conf_fwd",
        in_specs=[_tile(tl, 2 * D_CONF), _tile(tl, D_CONF), _full((1, 2 * D_CONF)), _full((CONF_HALO, D_CONF)),
                  _full((1, D_CONF)), _full((1, D_CONF)), _full((1, D_CONF))],
        out_specs=[_tile(tl, D_CONF)] * 3,
        out_shape=[jax.ShapeDtypeStruct((n_tok, D_CONF), F32), jax.ShapeDtypeStruct((n_tok, D_CONF), F32),
                   jax.ShapeDtypeStruct((n_tok, D_CONF), BF16)],
        scratch_shapes=[pltpu.VMEM((tl + CONF_HALO, D_CONF), F32)],
        compiler_params=_params(),
    )(glu, cgate, bglu, cw, cb, lg, lb)


def _conf_bwd(dyc, u0, u1, glu, cgate, bglu, cw, lg, lb, tl):
    n_tok = glu.shape[0]
    n_t = n_tok // tl

    def body(dyc_ref, u0_ref, u1_ref, glu_ref, cg_ref, bglu_ref, cw_ref, lg_ref, lb_ref,
             dglu_ref, dcg_ref, dcw_ref, dbglu_ref, small_ref, buf):
        @pl.when(pl.program_id(0) == 0)
        def _():
            dcw_ref[...] = jnp.zeros_like(dcw_ref)
            dbglu_ref[...] = jnp.zeros_like(dbglu_ref)
            small_ref[...] = jnp.zeros_like(small_ref)
            buf[tl:tl + CONF_HALO, :] = jnp.zeros((CONF_HALO, D_CONF), F32)

        xhat, rstd = _ln_stats(u1_ref[...])
        n = xhat * lg_ref[...] + lb_ref[...]
        sn = _sig(n)
        cgv = cg_ref[...]
        scg = _sig(cgv)
        dycv = dyc_ref[...]
        dcg_ref[...] = dycv * (n * sn) * _dsilu(cgv, scg)
        dn = dycv * (cgv * scg) * _dsilu(n, sn)
        small_ref[0:1, :] += _colsum(dn * xhat)
        small_ref[1:2, :] += _colsum(dn)
        du1 = _ln_bwd(dn * lg_ref[...], xhat, rstd)
        small_ref[2:3, :] += _colsum(du1)
        buf[0:tl, :] = du1
        u0 = u0_ref[...]
        du0 = jnp.zeros((tl, D_CONF), F32)
        for k in range(CONF_K):
            sh = buf[CONF_K - 1 - k:CONF_K - 1 - k + tl, :]
            du0 = du0 + sh * cw_ref[k:k + 1, :]
            dcw_ref[k:k + 1, :] += _colsum(u0 * sh)
        buf[tl:tl + CONF_HALO, :] = buf[0:CONF_HALO, :]
        gl = glu_ref[...] + bglu_ref[...]
        sg = _sig(gl[:, D_CONF:2 * D_CONF])
        dgv = du0 * sg
        dgg = du0 * gl[:, 0:D_CONF] * sg * (1.0 - sg)
        dglu_ref[:, 0:D_CONF] = dgv
        dglu_ref[:, D_CONF:2 * D_CONF] = dgg
        dbglu_ref[:, 0:D_CONF] += _colsum(dgv)
        dbglu_ref[:, D_CONF:2 * D_CONF] += _colsum(dgg)

    rev = functools.partial(_tile, tl, rev_of=n_t)
    return pl.pallas_call(
        body, grid=(n_t,), name="conf_bwd",
        in_specs=[rev(D_CONF), rev(D_CONF), rev(D_CONF), rev(2 * D_CONF), rev(D_CONF), _full((1, 2 * D_CONF)),
                  _full((CONF_HALO, D_CONF)), _full((1, D_CONF)), _full((1, D_CONF))],
        out_specs=[rev(2 * D_CONF), rev(D_CONF), _full((CONF_HALO, D_CONF)), _full((1, 2 * D_CONF)), _full((SUBLANES, D_CONF))],
        out_shape=[jax.ShapeDtypeStruct((n_tok, 2 * D_CONF), F32), jax.ShapeDtypeStruct((n_tok, D_CONF), F32),
                   jax.ShapeDtypeStruct((CONF_HALO, D_CONF), F32), jax.ShapeDtypeStruct((1, 2 * D_CONF), F32),
                   jax.ShapeDtypeStruct((SUBLANES, D_CONF), F32)],
        scratch_shapes=[pltpu.VMEM((tl + CONF_HALO, D_CONF), F32)],
        compiler_params=_params(),
    )(dyc, u0, u1, glu, cgate, bglu, cw, lg, lb)


def _tail(x, yssm, yconf, p, tgt, vec, w_out, wpg, wpp, tl):
    n_tok = x.shape[0]

    def body(x_ref, ys_ref, yc_ref, p_ref, t_ref, vec_ref, wo_ref, wg_ref, wp_ref,
             dmix_ref, dr1_ref, dr1b_ref, h1b_ref, dgb_ref, dpb_ref, small_ref, loss_ref):
        @pl.when(pl.program_id(0) == 0)
        def _():
            small_ref[...] = jnp.zeros_like(small_ref)
            loss_ref[...] = jnp.zeros_like(loss_ref)

        xh0, _ = _ln_stats(x_ref[...])
        h0 = xh0 * vec_ref[0:1, :] + vec_ref[1:2, :]
        out = _dot(ys_ref[...], wo_ref[0:D_SSM, :]) + _dot(yc_ref[...], wo_ref[D_SSM:D_SSM + D_CONF, :]) + vec_ref[2:3, :]
        xh1, rstd1 = _ln_stats(ALPHA * h0 + out)
        h1 = xh1 * vec_ref[3:4, :] + vec_ref[4:5, :]
        h1b = h1.astype(BF16)
        h1b_ref[...] = h1b
        gate = _sig(_dot(h1b, wg_ref[...]))
        ple = _dot(p_ref[...].astype(BF16), wp_ref[...])
        xh2, rstd2 = _ln_stats(ALPHA * h1 + gate * ple)
        h2 = xh2 * vec_ref[5:6, :] + vec_ref[6:7, :]
        diff = h2 - t_ref[...]
        part = jnp.sum(jnp.sum(diff * diff, axis=1, keepdims=True), axis=0, keepdims=True) * (0.5 / D_MODEL)
        loss_ref[...] += jnp.broadcast_to(part, loss_ref.shape)
        dh2 = diff * (1.0 / D_MODEL)
        small_ref[3:4, :] += _colsum(dh2 * xh2)
        small_ref[4:5, :] += _colsum(dh2)
        dr2 = _ln_bwd(dh2 * vec_ref[5:6, :], xh2, rstd2)
        dgpre = (dr2 * ple * gate * (1.0 - gate)).astype(BF16)
        dgb_ref[...] = dgpre
        dpb_ref[...] = (dr2 * gate).astype(BF16)
        dh1 = ALPHA * dr2 + _dot_nt(dgpre, wg_ref[...])
        small_ref[1:2, :] += _colsum(dh1 * xh1)
        small_ref[2:3, :] += _colsum(dh1)
        dr1 = _ln_bwd(dh1 * vec_ref[3:4, :], xh1, rstd1)
        small_ref[0:1, :] += _colsum(dr1)
        dr1_ref[...] = dr1
        dr1b = dr1.astype(BF16)
        dr1b_ref[...] = dr1b
        dmix_ref[...] = _dot_nt(dr1b, wo_ref[...])

    d_mix = D_SSM + D_CONF
    return pl.pallas_call(
        body, grid=(n_tok // tl,), name="tail",
        in_specs=[_tile(tl, D_MODEL), _tile(tl, D_SSM), _tile(tl, D_CONF), _tile(tl, D_PLE), _tile(tl, D_MODEL),
                  _full((SUBLANES, D_MODEL)), _full((d_mix, D_MODEL), True), _full((D_MODEL, D_MODEL), True),
                  _full((D_PLE, D_MODEL), True)],
        out_specs=[_tile(tl, d_mix), _tile(tl, D_MODEL), _tile(tl, D_MODEL), _tile(tl, D_MODEL), _tile(tl, D_MODEL),
                   _tile(tl, D_MODEL), _full((SUBLANES, D_MODEL)), _full((SUBLANES, LANES))],
        out_shape=[jax.ShapeDtypeStruct((n_tok, d_mix), F32), jax.ShapeDtypeStruct((n_tok, D_MODEL), F32),
                   jax.ShapeDtypeStruct((n_tok, D_MODEL), BF16), jax.ShapeDtypeStruct((n_tok, D_MODEL), BF16),
                   jax.ShapeDtypeStruct((n_tok, D_MODEL), BF16), jax.ShapeDtypeStruct((n_tok, D_MODEL), BF16),
                   jax.ShapeDtypeStruct((SUBLANES, D_MODEL), F32), jax.ShapeDtypeStruct((SUBLANES, LANES), F32)],
        compiler_params=_params(),
    )(x, yssm, yconf, p, tgt, vec, w_out, wpg, wpp)


def _inproj_bwd(dxin, dz, dglu, dcg, ddtr, dr1, x, g, b, w_r, tl):
    n_tok = x.shape[0]

    def body(dxin_ref, dz_ref, dglu_ref, dcg_ref, ddtr_ref, dr1_ref, x_ref, g_ref, b_ref, w_ref,
             dx_ref, dpb_ref, small_ref):
        @pl.when(pl.program_id(0) == 0)
        def _():
            small_ref[...] = jnp.zeros_like(small_ref)

        dh0 = ALPHA * dr1_ref[...]
        for ref, lo, hi in ((dxin_ref, R_XBC, R_Z), (dz_ref, R_Z, R_GLU), (dglu_ref, R_GLU, R_CG), (dcg_ref, R_CG, R_DT),
                            (ddtr_ref, R_DT, D_INR)):
            piece = ref[...].astype(BF16)
            dpb_ref[:, lo:hi] = piece
            dh0 = dh0 + _dot_nt(piece, w_ref[:, lo:hi])
        xhat, rstd = _ln_stats(x_ref[...])
        small_ref[0:1, :] += _colsum(dh0 * xhat)
        small_ref[1:2, :] += _colsum(dh0)
        dx_ref[...] = _ln_bwd(dh0 * g_ref[...], xhat, rstd)

    return pl.pallas_call(
        body, grid=(n_tok // tl,), name="inproj_bwd",
        in_specs=[_tile(tl, D_XBC), _tile(tl, D_SSM), _tile(tl, 2 * D_CONF), _tile(tl, D_CONF), _tile(tl, LANES),
                  _tile(tl, D_MODEL), _tile(tl, D_MODEL), _full((1, D_MODEL)), _full((1, D_MODEL)),
                  _full((D_MODEL, D_INR), True)],
        out_specs=[_tile(tl, D_MODEL), _tile(tl, D_INR), _full((SUBLANES, D_MODEL))],
        out_shape=[jax.ShapeDtypeStruct((n_tok, D_MODEL), F32), jax.ShapeDtypeStruct((n_tok, D_INR), BF16),
                   jax.ShapeDtypeStruct((SUBLANES, D_MODEL), F32)],
        compiler_params=_params(),
    )(dxin, dz, dglu, dcg, ddtr, dr1, x, g, b, w_r)


def _tn_matmul(a, b, name, tn, tl):
    n_tok, m = a.shape
    n = b.shape[1]

    def body(a_ref, b_ref, o_ref):
        @pl.when(pl.program_id(1) == 0)
        def _():
            o_ref[...] = jnp.zeros_like(o_ref)

        o_ref[...] += lax.dot_general(a_ref[...], b_ref[...], TN_DIMS, preferred_element_type=F32)

    return pl.pallas_call(
        body, grid=(n // tn, n_tok // tl), name=name,
        in_specs=[pl.BlockSpec((tl, m), lambda j, l: (l, 0)), pl.BlockSpec((tl, tn), lambda j, l: (l, j))],
        out_specs=pl.BlockSpec((m, tn), lambda j, l: (0, j)),
        out_shape=jax.ShapeDtypeStruct((m, n), F32),
        compiler_params=pltpu.CompilerParams(dimension_semantics=("parallel", "arbitrary"), vmem_limit_bytes=VMEM_LIMIT),
    )(a, b)


def _pad_rows(a, rows):
    return jnp.pad(a, ((0, rows - a.shape[0]), (0, 0)))


def _pad_lanes(a):
    return jnp.pad(a, ((0, 0), (0, LANES - a.shape[1])))


def _local_grads(x, p, tgt, w_r, w_out_b, wpg_b, wpp_b, ssm_cw, conf_cw, sm):
    n_tok = x.shape[0]
    tl = min(256, n_tok)
    row = lambda v: v.reshape(1, -1)
    g_e, b_e = row(sm["ln_emb_g"]), row(sm["ln_emb_b"])
    h0b, xbc_in, z, glu, cgate, dtr = _ln_inproj(x, g_e, b_e, w_r, tl)

    cw4 = _pad_rows(ssm_cw, SUBLANES)
    dtb, alog = _pad_lanes(sm["dt_bias"]), _pad_lanes(sm["a_log"])
    dsk = jnp.repeat(sm["d_skip"], HEAD_DIM, axis=1)
    pre, y, yssm, hprev = _ssd_fwd(xbc_in, z, dtr, cw4, sm["ssm_conv_b"], dtb, alog, dsk, sm["ssm_norm_g"], tl)

    cw31 = _pad_rows(conf_cw, CONF_HALO)
    u0, u1, yconf = _conf_fwd(glu, cgate, sm["b_glu"], cw31, sm["conf_conv_b"], sm["conf_ln_g"], sm["conf_ln_b"], tl)

    vec = jnp.concatenate([g_e, b_e, sm["b_out"], sm["ln1_g"], sm["ln1_b"], sm["ln2_g"], sm["ln2_b"],
                           jnp.zeros((1, D_MODEL), F32)], axis=0)
    dmix, dr1, dr1b, h1b, dgb, dpb, small_t, loss = _tail(
        x, yssm, yconf, p, tgt, vec, w_out_b, wpg_b, wpp_b, tl)

    dglu, dcg, dcw31, dbglu, small_c = _conf_bwd(dmix[:, D_SSM:], u0, u1, glu, cgate, sm["b_glu"], cw31,
                                                  sm["conf_ln_g"], sm["conf_ln_b"], tl)
    dxin, dz, ddtr, dcw4, dcb4, dgn, ddsk, dalog, ddtb = _ssd_bwd(
        dmix[:, :D_SSM], y, z, pre, xbc_in, dtr, hprev, cw4, dtb, alog, dsk, sm["ssm_norm_g"], tl)
    dx, dprojb, small_e = _inproj_bwd(dxin, dz, dglu, dcg, ddtr, dr1, x, g_e, b_e, w_r, tl)

    tlm = min(512, n_tok)
    dw_r = _tn_matmul(h0b, dprojb, "dw_in", 640, tlm)
    dw_out = jnp.concatenate([_tn_matmul(yssm, dr1b, "dw_out_ssm", 512, tlm),
                              _tn_matmul(yconf, dr1b, "dw_out_conf", 512, tlm)], axis=0)
    dwpg = _tn_matmul(h1b, dgb, "dw_ple_gate", 512, tlm)
    dwpp = _tn_matmul(p.astype(BF16), dpb, "dw_ple_proj", 512, tlm)
    grads = dict(
        ln_emb_g=small_e[0], ln_emb_b=small_e[1], w_in=dw_r, ssm_conv_w=dcw4[0:SSM_K], ssm_conv_b=dcb4,
        dt_bias=ddtb[:, 0:N_HEADS], a_log=dalog[:, 0:N_HEADS], d_skip=ddsk[:, 0:N_HEADS], ssm_norm_g=dgn, b_glu=dbglu,
        conf_conv_w=dcw31[0:CONF_K], conf_conv_b=small_c[2:3], conf_ln_g=small_c[0:1], conf_ln_b=small_c[1:2],
        w_out=dw_out, b_out=small_t[0:1], ln1_g=small_t[1:2], ln1_b=small_t[2:3], w_ple_gate=dwpg, w_ple_proj=dwpp,
        ln2_g=small_t[3:4], ln2_b=small_t[4:5])
    return loss[0, 0], dx, grads


N_CHIPS = 4
N_DEV = 8
W_IN_SH = D_IN // N_CHIPS
BIG = (("w_in", D_MODEL, W_IN_SH), ("w_out", (D_SSM + D_CONF) // N_CHIPS, D_MODEL),
       ("w_ple_gate", D_MODEL // N_CHIPS, D_MODEL), ("w_ple_proj", D_PLE, D_MODEL // N_CHIPS))
SEGS = ((R_XBC, 0, D_SSM), (R_XBC + D_SSM, 2048, 256), (R_XBC + D_SSM + 256, 2304, 256), (R_Z, 1024, D_SSM),
        (R_GLU, 2576, 2 * D_CONF), (R_CG, 4624, D_CONF), (R_DT, 2560, N_HEADS))
ROWS_CW4 = 2
ROWS_CW31 = 8
ROWS_CONV = 16
SMALL_ROWS = 56
SMALL_LAYOUT = (("ln_emb_g", 0, 1), ("ln_emb_b", 1, 1), ("ssm_conv_b", 2, 2), ("dt_bias", 4, 1), ("a_log", 5, 1),
                ("d_skip", 6, 1), ("ssm_norm_g", 7, 1), ("b_glu", 8, 2), ("conf_conv_b", 10, 1), ("conf_ln_g", 11, 1),
                ("conf_ln_b", 12, 1), ("b_out", 13, 1), ("ln1_g", 14, 1), ("ln1_b", 15, 1), ("ln2_g", 16, 1), ("ln2_b", 17, 1))
CONV_LAYOUT = (("ssm_conv_w", 18, 6, (SSM_K, D_XBC)), ("conf_conv_w", 24, 31, (CONF_K, D_CONF)))


def _rows_of(v, rows):
    flat = v.reshape(-1)
    return jnp.pad(flat, (0, rows * D_MODEL - flat.shape[0])).reshape(rows, D_MODEL)


def _pack_small(d):
    parts = [_rows_of(d[n], r) for n, _, r in SMALL_LAYOUT]
    for n, _, r, _ in CONV_LAYOUT:
        parts.append(_rows_of(d[n], r) if n in d else jnp.zeros((r, D_MODEL), F32))
    parts.append(jnp.zeros((SMALL_ROWS - 55, D_MODEL), F32))
    return jnp.concatenate(parts, axis=0)


def _unpack_small(a, like):
    return {n: a[r0:r0 + r].reshape(-1)[:like[n].size].reshape(like[n].shape) for n, r0, r in SMALL_LAYOUT}


def _w_r_from_shards(w4):
    parts = []
    for _, o, wd in SEGS:
        lo, hi = o, o + wd
        while lo < hi:
            s = lo // W_IN_SH
            e = min(hi, (s + 1) * W_IN_SH)
            parts.append(w4[s][:, lo - s * W_IN_SH:e - s * W_IN_SH])
            lo = e
    parts.append(jnp.zeros((D_MODEL, LANES - N_HEADS), w4.dtype))
    return jnp.concatenate(parts, axis=1)


def _shards_from_dw_r(dw_r):
    slabs = []
    for s in range(N_CHIPS):
        lo, hi = s * W_IN_SH, (s + 1) * W_IN_SH
        parts = []
        for kcol, o, wd in sorted(SEGS, key=lambda t: t[1]):
            a, b = max(lo, o), min(hi, o + wd)
            if a < b:
                parts.append(dw_r[:, kcol + a - o:kcol + b - o])
        slabs.append(jnp.concatenate(parts, axis=1))
    return jnp.stack(slabs)


def _row_chunks(rows, n):
    return [(j * (rows // n), rows // n) for j in range(n)]


def _my_place():
    return lax.axis_index("x"), lax.axis_index("y"), lax.axis_index("c")


MESH_ID = pl.DeviceIdType.MESH
ANY = pl.BlockSpec(memory_space=pl.ANY)
CHIP_FLIPS = ((1, 0), (0, 1), (1, 1))


def _remote(src, dst, send_sem, recv_sem, peer):
    return pltpu.make_async_remote_copy(src, dst, send_sem, recv_sem, device_id=peer, device_id_type=MESH_ID)


GATHER_CHUNKS = (2, 1, 1, 1)


def _gather_weights(shards, conv_f):
    n_big = len(BIG)
    plan = [(a, o, n, rows // 2) for a, (_, rows, _) in enumerate(BIG) for o, n in _row_chunks(rows // 2, GATHER_CHUNKS[a])]

    def body(*refs):
        ins, outs = refs[0:n_big + 1], refs[n_big + 1:2 * n_big + 2]
        send_a, recv_a, send_b, recv_b, loc = refs[2 * n_big + 2:]
        x, y, c = _my_place()
        s = 2 * x + y
        sibling = (x, y, 1 - c)
        own = [pltpu.make_async_copy(ins[a], outs[a].at[s], loc.at[a]) for a in range(n_big + 1)]
        first = []
        for k, (fx, fy) in enumerate(CHIP_FLIPS):
            peer = (x ^ fx, y ^ fy, c)
            for j, (a, o, n, h) in enumerate(plan):
                r0 = pl.multiple_of(c * h + o, 16)
                first.append(_remote(ins[a].at[pl.ds(r0, n)], outs[a].at[s, pl.ds(r0, n)], send_a.at[k, j], recv_a.at[k, j], peer))
            first.append(_remote(ins[n_big], outs[n_big].at[s], send_a.at[k, len(plan)], recv_a.at[k, len(plan)], peer))
        for cp in own + first:
            cp.start()
        passed = []
        for k, (fx, fy) in enumerate(CHIP_FLIPS):
            peer = (x ^ fx, y ^ fy, c)
            sk = 2 * (x ^ fx) + (y ^ fy)
            for j, (a, o, n, h) in enumerate(plan):
                land = outs[a].at[sk, pl.ds(pl.multiple_of(c * h + o, 16), n)]
                _remote(land, land, send_a.at[k, j], recv_a.at[k, j], peer).wait_recv()
                cp = _remote(land, land, send_b.at[k, j], recv_b.at[k, j], sibling)
                cp.start()
                passed.append(cp)
            land = outs[n_big].at[sk]
            _remote(land, land, send_a.at[k, len(plan)], recv_a.at[k, len(plan)], peer).wait_recv()
        for k, (fx, fy) in enumerate(CHIP_FLIPS):
            sk = 2 * (x ^ fx) + (y ^ fy)
            for j, (a, o, n, h) in enumerate(plan):
                land = outs[a].at[sk, pl.ds(pl.multiple_of((1 - c) * h + o, 16), n)]
                _remote(land, land, send_b.at[k, j], recv_b.at[k, j], sibling).wait_recv()
        for cp in first + passed:
            cp.wait_send()
        for cp in own:
            cp.wait()

    arrays = list(shards) + [conv_f]
    return pl.pallas_call(
        body, name="gather_weights", in_specs=[ANY] * len(arrays), out_specs=[ANY] * len(arrays),
        out_shape=[jax.ShapeDtypeStruct((N_CHIPS,) + a.shape, a.dtype) for a in arrays],
        scratch_shapes=[pltpu.SemaphoreType.DMA((3, len(plan) + 1)), pltpu.SemaphoreType.DMA((3, len(plan) + 1)),
                        pltpu.SemaphoreType.DMA((3, len(plan))), pltpu.SemaphoreType.DMA((3, len(plan))),
                        pltpu.SemaphoreType.DMA((len(arrays),))],
    )(*arrays)


CORES_CHUNKS = (2, 1, 1, 1)
CHIPS_CHUNKS = (2, 1, 1, 1)
SHARE_CHUNKS = (4, 2, 1, 1)


def _exchange_cores(gbig, spack):
    n_big = len(BIG)
    plan = [(a, t, o, n, rows // 2) for a, (_, rows, _) in enumerate(BIG) for t in range(N_CHIPS)
            for o, n in _row_chunks(rows // 2, CORES_CHUNKS[a])]

    def body(*refs):
        g_refs, s_ref = refs[0:n_big], refs[n_big]
        theirs, small_all = refs[n_big + 1:2 * n_big + 1], refs[2 * n_big + 1]
        send_sems, recv_sems, send_small, recv_small, loc_sem = refs[2 * n_big + 2:]
        x, y, c = _my_place()
        me = 4 * x + 2 * y + c
        own = pltpu.make_async_copy(s_ref, small_all.at[me], loc_sem)
        sends = []
        for j, (a, t, o, n, h) in enumerate(plan):
            give = pl.multiple_of((1 - c) * h + o, SUBLANES)
            sends.append(_remote(g_refs[a].at[t, pl.ds(give, n)], theirs[a].at[t, pl.ds(o, n)], send_sems.at[j], recv_sems.at[j],
                                 (x, y, 1 - c)))
        for m in range(1, N_DEV):
            peer = (x ^ (m >> 2), y ^ ((m >> 1) & 1), c ^ (m & 1))
            sends.append(_remote(s_ref, small_all.at[me], send_small.at[m - 1], recv_small.at[m - 1], peer))
        own.start()
        for cp in sends:
            cp.start()
        for cp in sends:
            cp.wait()
        own.wait()

    return pl.pallas_call(
        body, name="exchange_cores", in_specs=[ANY] * (n_big + 1), out_specs=[ANY] * (n_big + 1),
        out_shape=[jax.ShapeDtypeStruct((N_CHIPS, rows // 2, cols), F32) for _, rows, cols in BIG]
        + [jax.ShapeDtypeStruct((N_DEV, SMALL_ROWS, D_MODEL), F32)],
        scratch_shapes=[pltpu.SemaphoreType.DMA((len(plan),)), pltpu.SemaphoreType.DMA((len(plan),)),
                        pltpu.SemaphoreType.DMA((N_DEV - 1,)), pltpu.SemaphoreType.DMA((N_DEV - 1,)), pltpu.SemaphoreType.DMA],
    )(*gbig, spack)


def _exchange_chips(psums):
    n_big = len(BIG)
    plan = [(a, o, n) for a, (_, rows, _) in enumerate(BIG) for o, n in _row_chunks(rows // 2, CHIPS_CHUNKS[a])]

    def body(*refs):
        p_refs, got = refs[0:n_big], refs[n_big:2 * n_big]
        send_sems, recv_sems = refs[2 * n_big:]
        x, y, c = _my_place()
        sends = []
        for k, (fx, fy) in enumerate(CHIP_FLIPS):
            tx, ty = x ^ fx, y ^ fy
            for j, (a, o, n) in enumerate(plan):
                sends.append(_remote(p_refs[a].at[2 * tx + ty, pl.ds(o, n)], got[a].at[k, pl.ds(o, n)], send_sems.at[k, j],
                                     recv_sems.at[k, j], (tx, ty, c)))
        for cp in sends:
            cp.start()
        for cp in sends:
            cp.wait()

    return pl.pallas_call(
        body, name="exchange_chips", in_specs=[ANY] * n_big, out_specs=[ANY] * n_big,
        out_shape=[jax.ShapeDtypeStruct((3, rows // 2, cols), BF16) for _, rows, cols in BIG],
        scratch_shapes=[pltpu.SemaphoreType.DMA((3, len(plan))), pltpu.SemaphoreType.DMA((3, len(plan)))],
    )(*psums)


def _share_halves(tots):
    n_big = len(BIG)
    plan = [(a, o, n) for a, (_, rows, _) in enumerate(BIG) for o, n in _row_chunks(rows // 2, SHARE_CHUNKS[a])]

    def body(*refs):
        t_refs, both = refs[0:n_big], refs[n_big:2 * n_big]
        send_sems, recv_sems, loc = refs[2 * n_big:]
        x, y, c = _my_place()
        own = [pltpu.make_async_copy(t_refs[a], both[a].at[c], loc.at[a]) for a in range(n_big)]
        sends = [_remote(t_refs[a].at[pl.ds(o, n)], both[a].at[c, pl.ds(o, n)], send_sems.at[j], recv_sems.at[j], (x, y, 1 - c))
                 for j, (a, o, n) in enumerate(plan)]
        for cp in own + sends:
            cp.start()
        for cp in sends:
            cp.wait()
        for cp in own:
            cp.wait()

    return pl.pallas_call(
        body, name="share_halves", in_specs=[ANY] * n_big, out_specs=[ANY] * n_big,
        out_shape=[jax.ShapeDtypeStruct((2, rows // 2, cols), F32) for _, rows, cols in BIG],
        scratch_shapes=[pltpu.SemaphoreType.DMA((len(plan),)), pltpu.SemaphoreType.DMA((len(plan),)),
                        pltpu.SemaphoreType.DMA((n_big,))],
    )(*tots)


def _sum_rows(half):
    return half if half <= 256 else 256


def _add_cores(core, g, theirs, name):
    _, rows, cols = g.shape
    half = rows // 2
    tr = _sum_rows(half)
    nb = half // tr

    def body(c_ref, g_ref, t_ref, o_ref):
        o_ref[...] = (g_ref[...] + t_ref[...]).astype(BF16)

    grid_spec = pltpu.PrefetchScalarGridSpec(
        num_scalar_prefetch=1, grid=(N_CHIPS, nb),
        in_specs=[pl.BlockSpec((1, tr, cols), lambda t, i, c_ref: (t, c_ref[0] * nb + i, 0)),
                  pl.BlockSpec((1, tr, cols), lambda t, i, c_ref: (t, i, 0))],
        out_specs=pl.BlockSpec((1, tr, cols), lambda t, i, c_ref: (t, i, 0)))
    return pl.pallas_call(body, grid_spec=grid_spec, name=name, out_shape=jax.ShapeDtypeStruct((N_CHIPS, half, cols), BF16),
                          compiler_params=pltpu.CompilerParams(vmem_limit_bytes=VMEM_LIMIT))(core, g, theirs)


def _add_chips(chip, psum, got, name):
    _, half, cols = psum.shape
    tr = _sum_rows(half)

    def body(s_ref, p_ref, g_ref, o_ref):
        o_ref[...] = ((p_ref[0].astype(F32) + g_ref[0].astype(F32)) + g_ref[1].astype(F32)) + g_ref[2].astype(F32)

    grid_spec = pltpu.PrefetchScalarGridSpec(
        num_scalar_prefetch=1, grid=(half // tr,),
        in_specs=[pl.BlockSpec((1, tr, cols), lambda i, s_ref: (s_ref[0], i, 0)),
                  pl.BlockSpec((3, tr, cols), lambda i, s_ref: (0, i, 0))],
        out_specs=pl.BlockSpec((tr, cols), lambda i, s_ref: (i, 0)))
    return pl.pallas_call(body, grid_spec=grid_spec, name=name, out_shape=jax.ShapeDtypeStruct((half, cols), F32),
                          compiler_params=pltpu.CompilerParams(vmem_limit_bytes=VMEM_LIMIT))(chip, psum, got)


def _adam_math(w, g, m, v):
    m = ADAM_B1 * m + (1.0 - ADAM_B1) * g
    v = ADAM_B2 * v + (1.0 - ADAM_B2) * (g * g)
    m_hat = m / (1.0 - ADAM_B1 ** ADAM_STEP)
    v_hat = v / (1.0 - ADAM_B2 ** ADAM_STEP)
    return -ADAM_LR * (m_hat / (jnp.sqrt(v_hat) + ADAM_EPS) + ADAM_WD * w), m, v


def _adam(w, g, m, v, name):
    rows, cols = w.shape
    tr = rows if rows <= 256 else 256

    def body(w_ref, g_ref, m_ref, v_ref, d_ref, nm_ref, nv_ref):
        d_ref[...], nm_ref[...], nv_ref[...] = _adam_math(w_ref[...], g_ref[...], m_ref[...], v_ref[...])

    spec = pl.BlockSpec((tr, cols), lambda i: (i, 0))
    return pl.pallas_call(body, grid=(rows // tr,), name=name, in_specs=[spec] * 4, out_specs=[spec] * 3,
                          out_shape=[jax.ShapeDtypeStruct(w.shape, F32)] * 3, compiler_params=_params(seq=False))(w, g, m, v)


def _adam_small(w, parts, m, v):
    def body(w_ref, p_ref, m_ref, v_ref, g_ref, d_ref, nm_ref, nv_ref):
        g = p_ref[0]
        for k in range(1, N_DEV):
            g = g + p_ref[k]
        g_ref[...] = g
        d_ref[...], nm_ref[...], nv_ref[...] = _adam_math(w_ref[...], g, m_ref[...], v_ref[...])

    return pl.pallas_call(body, name="adam_small", out_shape=[jax.ShapeDtypeStruct(w.shape, F32)] * 4)(w, parts, m, v)


def kernel(x, p, ln_emb_g, ln_emb_b, w_in, ssm_conv_w, ssm_conv_b, dt_bias, a_log, d_skip, ssm_norm_g, b_glu, conf_conv_w, conf_conv_b, conf_ln_g, conf_ln_b, w_out, b_out, ln1_g, ln1_b, w_ple_gate, w_ple_proj, ln2_g, ln2_b, loss_target, m_ln_emb_g, m_ln_emb_b, m_w_in, m_ssm_conv_w, m_ssm_conv_b, m_dt_bias, m_a_log, m_d_skip, m_ssm_norm_g, m_b_glu, m_conf_conv_w, m_conf_conv_b, m_conf_ln_g, m_conf_ln_b, m_w_out, m_b_out, m_ln1_g, m_ln1_b, m_w_ple_gate, m_w_ple_proj, m_ln2_g, m_ln2_b, v_ln_emb_g, v_ln_emb_b, v_w_in, v_ssm_conv_w, v_ssm_conv_b, v_dt_bias, v_a_log, v_d_skip, v_ssm_norm_g, v_b_glu, v_conf_conv_w, v_conf_conv_b, v_conf_ln_g, v_conf_ln_b, v_w_out, v_b_out, v_ln1_g, v_ln1_b, v_w_ple_gate, v_w_ple_proj, v_ln2_g, v_ln2_b):
    order = ("ln_emb_g", "ln_emb_b", "w_in", "ssm_conv_w", "ssm_conv_b", "dt_bias", "a_log", "d_skip", "ssm_norm_g", "b_glu",
             "conf_conv_w", "conf_conv_b", "conf_ln_g", "conf_ln_b", "w_out", "b_out", "ln1_g", "ln1_b", "w_ple_gate",
             "w_ple_proj", "ln2_g", "ln2_b")
    w = dict(zip(order, (ln_emb_g, ln_emb_b, w_in, ssm_conv_w, ssm_conv_b, dt_bias, a_log, d_skip, ssm_norm_g, b_glu,
                         conf_conv_w, conf_conv_b, conf_ln_g, conf_ln_b, w_out, b_out, ln1_g, ln1_b, w_ple_gate, w_ple_proj,
                         ln2_g, ln2_b)))
    m = dict(zip(order, (m_ln_emb_g, m_ln_emb_b, m_w_in, m_ssm_conv_w, m_ssm_conv_b, m_dt_bias, m_a_log, m_d_skip,
                         m_ssm_norm_g, m_b_glu, m_conf_conv_w, m_conf_conv_b, m_conf_ln_g, m_conf_ln_b, m_w_out, m_b_out,
                         m_ln1_g, m_ln1_b, m_w_ple_gate, m_w_ple_proj, m_ln2_g, m_ln2_b)))
    v = dict(zip(order, (v_ln_emb_g, v_ln_emb_b, v_w_in, v_ssm_conv_w, v_ssm_conv_b, v_dt_bias, v_a_log, v_d_skip,
                         v_ssm_norm_g, v_b_glu, v_conf_conv_w, v_conf_conv_b, v_conf_ln_g, v_conf_ln_b, v_w_out, v_b_out,
                         v_ln1_g, v_ln1_b, v_w_ple_gate, v_w_ple_proj, v_ln2_g, v_ln2_b)))

    conv_f = jnp.concatenate([_rows_of(w["ssm_conv_w"], ROWS_CW4), _rows_of(w["conf_conv_w"], ROWS_CW31),
                              jnp.zeros((ROWS_CONV - ROWS_CW4 - ROWS_CW31, D_MODEL), F32)], axis=0)
    w_in_all, w_out_all, wpg_all, wpp_all, conv_all = _gather_weights([w[n][0].astype(BF16) for n, _, _ in BIG], conv_f)
    w_r = _w_r_from_shards(w_in_all)
    w_out_b = w_out_all.reshape(D_SSM + D_CONF, D_MODEL)
    wpg_b = wpg_all.reshape(D_MODEL, D_MODEL)
    wpp_b = wpp_all.transpose(1, 0, 2).reshape(D_PLE, D_MODEL)
    cw4 = conv_all[:, 0:ROWS_CW4].reshape(N_CHIPS, -1)[:, :SSM_K * D_XBC // N_CHIPS]
    cw4 = cw4.reshape(N_CHIPS, SSM_K, D_XBC // N_CHIPS).transpose(1, 0, 2).reshape(SSM_K, D_XBC)
    cw31 = conv_all[:, ROWS_CW4:ROWS_CW4 + ROWS_CW31].reshape(N_CHIPS, -1)[:, :CONF_K * D_CONF // N_CHIPS]
    cw31 = cw31.reshape(N_CHIPS, CONF_K, D_CONF // N_CHIPS).transpose(1, 0, 2).reshape(CONF_K, D_CONF)

    small_names = [n for n, _, _ in SMALL_LAYOUT]
    sm = {n: w[n] for n in small_names}
    loss_part, dx, grads = _local_grads(x[0], p[0, 0], loss_target[0], w_r, w_out_b, wpg_b, wpp_b, cw4, cw31, sm)
    loss = lax.psum(loss_part, ("x", "y", "c"))

    core = lax.axis_index("c").astype(jnp.int32).reshape(1)
    chip_i = 2 * lax.axis_index("x") + lax.axis_index("y")
    chip = chip_i.astype(jnp.int32).reshape(1)
    gbig = [_shards_from_dw_r(grads["w_in"]), grads["w_out"].reshape(N_CHIPS, -1, D_MODEL),
            grads["w_ple_gate"].reshape(N_CHIPS, -1, D_MODEL),
            grads["w_ple_proj"].reshape(D_PLE, N_CHIPS, D_MODEL // N_CHIPS).transpose(1, 0, 2)]
    spack = _pack_small({n: grads[n] for n in small_names + [n for n, _, _, _ in CONV_LAYOUT]})
    *theirs, small_all = _exchange_cores(gbig, spack)
    psums = [_add_cores(core, g, t, "add_cores_" + n) for g, t, (n, _, _) in zip(gbig, theirs, BIG)]
    got = _exchange_chips(psums)
    tots = [_add_chips(chip, ps, gt, "add_chips_" + n) for ps, gt, (n, _, _) in zip(psums, got, BIG)]
    both = _share_halves(tots)

    out_g, out_d, out_m, out_v = {}, {}, {}, {}
    g_s, d_s, m_s, v_s = _adam_small(_pack_small(sm), small_all, _pack_small({n: m[n] for n in small_names}),
                                     _pack_small({n: v[n] for n in small_names}))
    g_shard = {n: b.reshape(rows, cols) for b, (n, rows, cols) in zip(both, BIG)}
    for n, r0, r, shape in CONV_LAYOUT:
        whole = g_s[r0:r0 + r].reshape(-1)[:shape[0] * shape[1]].reshape(shape)
        g_shard[n] = lax.dynamic_slice_in_dim(whole, chip_i * (shape[1] // N_CHIPS), shape[1] // N_CHIPS, axis=1)
    for n, g in g_shard.items():
        shape = w[n].shape
        d, nm, nv = _adam(w[n][0], g, m[n][0], v[n][0], "adam_" + n)
        out_g[n], out_d[n], out_m[n], out_v[n] = g[None], d[None], nm[None], nv[None]
    for dst, src in ((out_g, g_s), (out_d, d_s), (out_m, m_s), (out_v, v_s)):
        dst.update(_unpack_small(src, sm))
    return (loss, dx[None], *[out_g[n] for n in order], *[out_d[n] for n in order], *[out_m[n] for n in order],
            *[out_v[n] for n in order])
```

```python
import functools

import jax
import jax.numpy as jnp
from jax import lax
from jax.experimental import pallas as pl
from jax.experimental.pallas import tpu as pltpu

F32 = jnp.float32
BF16 = jnp.bfloat16

D_MODEL = 1024
D_PLE = 256
D_SSM = 1024
D_CONF = 1024
N_HEADS = 16
HEAD_DIM = 64
N_GROUPS = 2
N_STATE = 128
CHUNK = 128
SSM_K = 4
CONF_K = 31
D_XBC = D_SSM + 2 * N_GROUPS * N_STATE
D_IN = 5648
R_XBC, R_Z, R_GLU, R_CG, R_DT, D_INR = 0, 1536, 2560, 4608, 5632, 5760
LN_EPS = 1e-5
RMS_EPS = 1e-5
ALPHA = 2.0 ** 0.25
ADAM_LR, ADAM_B1, ADAM_B2, ADAM_EPS, ADAM_WD, ADAM_STEP = 0.001, 0.9, 0.999, 1e-08, 0.01, 10
NEG_BIG = -1e30
LANES = 128
SUBLANES = 8
VMEM_LIMIT = 56 * 1024 * 1024
HIGHEST = lax.Precision.HIGHEST
NT_DIMS = (((1,), (1,)), ((), ()))
TN_DIMS = (((0,), (0,)), ((), ()))


def _sig(v):
    return jax.nn.sigmoid(v)


def _dsilu(v, s):
    return s * (1.0 + v * (1.0 - s))


def _ln_stats(v):
    mu = jnp.mean(v, axis=-1, keepdims=True)
    c = v - mu
    var = jnp.mean(c * c, axis=-1, keepdims=True)
    rstd = lax.rsqrt(var + LN_EPS)
    return c * rstd, rstd


def _ln_bwd(dxhat, xhat, rstd):
    m1 = jnp.mean(dxhat, axis=-1, keepdims=True)
    m2 = jnp.mean(dxhat * xhat, axis=-1, keepdims=True)
    return rstd * (dxhat - m1 - xhat * m2)


def _softplus(v):
    return jnp.maximum(v, 0.0) + jnp.log1p(jnp.exp(-jnp.abs(v)))


def _colsum(v):
    return jnp.sum(v, axis=0, keepdims=True)


def _dot(a, b):
    return jnp.dot(a, b, preferred_element_type=F32)


def _dot_nt(a, b):
    return lax.dot_general(a, b, NT_DIMS, preferred_element_type=F32)


def _tile(tl, c, rev_of=None):
    if rev_of is None:
        return pl.BlockSpec((tl, c), lambda i: (i, 0))
    return pl.BlockSpec((tl, c), lambda i: (rev_of - 1 - i, 0))


def _full(shape, single=False):
    nd = len(shape)
    if single:
        return pl.BlockSpec(shape, lambda i: (0,) * nd, pipeline_mode=pl.Buffered(1))
    return pl.BlockSpec(shape, lambda i: (0,) * nd)


def _params(seq=True):
    return pltpu.CompilerParams(dimension_semantics=("arbitrary",) if seq else ("parallel",), vmem_limit_bytes=VMEM_LIMIT)


def _ln_inproj(x, g, b, w_r, tl):
    n_tok = x.shape[0]

    def body(x_ref, g_ref, b_ref, w_ref, h0b_ref, xbc_ref, z_ref, glu_ref, cg_ref, dtr_ref):
        xhat, _ = _ln_stats(x_ref[...])
        hb = (xhat * g_ref[...] + b_ref[...]).astype(BF16)
        h0b_ref[...] = hb
        xbc_ref[...] = _dot(hb, w_ref[:, R_XBC:R_Z])
        z_ref[...] = _dot(hb, w_ref[:, R_Z:R_GLU])
        glu_ref[...] = _dot(hb, w_ref[:, R_GLU:R_CG])
        cg_ref[...] = _dot(hb, w_ref[:, R_CG:R_DT])
        dtr_ref[...] = _dot(hb, w_ref[:, R_DT:D_INR])

    widths = (D_MODEL, D_XBC, D_SSM, 2 * D_CONF, D_CONF, LANES)
    dtypes = (BF16, F32, F32, F32, F32, F32)
    return pl.pallas_call(
        body, grid=(n_tok // tl,), name="ln_inproj",
        in_specs=[_tile(tl, D_MODEL), _full((1, D_MODEL)), _full((1, D_MODEL)), _full((D_MODEL, D_INR), single=True)],
        out_specs=[_tile(tl, w) for w in widths],
        out_shape=[jax.ShapeDtypeStruct((n_tok, w), dt) for w, dt in zip(widths, dtypes)],
        compiler_params=_params(seq=False),
    )(x, g, b, w_r)


def _chunk_common(adt_c):
    row = lax.broadcasted_iota(jnp.int32, (CHUNK, CHUNK), 0)
    col = lax.broadcasted_iota(jnp.int32, (CHUNK, CHUNK), 1)
    tril = row >= col
    acs = jnp.dot(tril.astype(F32), adt_c, precision=HIGHEST, preferred_element_type=F32)
    last = acs[CHUNK - 1:CHUNK, :]
    return dict(row=row, col=col, tril=tril, lo=col < HEAD_DIM, acs=acs, acs_t=acs.T, e=jnp.exp(acs),
                dec=jnp.exp(last - acs), cd=jnp.exp(last))


def _pick(cm, v, ha):
    return jnp.where(cm["lo"], v[:, ha:ha + 1], v[:, ha + 1:ha + 2])


def _decay_mask(cm, h):
    return jnp.exp(jnp.where(cm["tril"], cm["acs"][:, h:h + 1] - cm["acs_t"][h:h + 1, :], NEG_BIG))


def _ssd_fwd(xbc_in, z, dtr, cw, cb, dtb, alog, dsk, gnorm, tl):
    n_tok = xbc_in.shape[0]
    nq = tl // CHUNK

    def body(xin_ref, z_ref, dtr_ref, cw_ref, cb_ref, dtb_ref, alog_ref, dsk_ref, gn_ref,
             pre_ref, y_ref, yssm_ref, hprev_ref, buf, hst):
        @pl.when(pl.program_id(0) == 0)
        def _():
            buf[0:SUBLANES, :] = jnp.zeros((SUBLANES, D_XBC), F32)
            hst[...] = jnp.zeros_like(hst)

        buf[SUBLANES:SUBLANES + tl, :] = xin_ref[...]
        pre = cb_ref[...] + jnp.zeros((tl, D_XBC), F32)
        for k in range(SSM_K):
            off = SUBLANES - (SSM_K - 1) + k
            pre = pre + buf[off:off + tl, :] * cw_ref[k:k + 1, :]
        buf[0:SUBLANES, :] = buf[tl:tl + SUBLANES, :]
        pre_ref[...] = pre
        xbc = pre * _sig(pre)
        dt = _softplus(dtr_ref[...] + dtb_ref[...])
        a = -jnp.exp(alog_ref[...])
        adt = dt * a
        for q in range(nq):
            r0 = q * CHUNK
            cm = _chunk_common(adt[r0:r0 + CHUNK, :])
            dt_c = dt[r0:r0 + CHUNK, :]
            for g in range(N_GROUPS):
                bg = xbc[r0:r0 + CHUNK, D_SSM + g * N_STATE:D_SSM + (g + 1) * N_STATE].astype(BF16)
                cg_ = xbc[r0:r0 + CHUNK, D_SSM + (N_GROUPS + g) * N_STATE:D_SSM + (N_GROUPS + g + 1) * N_STATE].astype(BF16)
                gm = _dot_nt(cg_, bg)
                for k in range(N_HEADS // N_GROUPS // 2):
                    ha = (N_HEADS // N_GROUPS) * g + 2 * k
                    c0 = ha * HEAD_DIM
                    xh2 = xbc[r0:r0 + CHUNK, c0:c0 + LANES]
                    x2 = xh2 * _pick(cm, dt_c, ha)
                    x2b = x2.astype(BF16)
                    ya = _dot((gm * _decay_mask(cm, ha)).astype(BF16), x2b)
                    yb = _dot((gm * _decay_mask(cm, ha + 1)).astype(BF16), x2b)
                    h2 = hst[c0:c0 + LANES, :]
                    hprev_ref[q, c0:c0 + LANES, :] = h2
                    z2 = _dot_nt(cg_, h2.astype(BF16))
                    y2 = jnp.where(cm["lo"], ya, yb) + z2 * _pick(cm, cm["e"], ha) + dsk_ref[:, c0:c0 + LANES] * xh2
                    y_ref[r0:r0 + CHUNK, c0:c0 + LANES] = y2
                    s2 = _dot((x2 * _pick(cm, cm["dec"], ha)).T.astype(BF16), bg)
                    cd2 = jnp.where(cm["row"] < HEAD_DIM, cm["cd"][:, ha:ha + 1], cm["cd"][:, ha + 1:ha + 2])
                    hst[c0:c0 + LANES, :] = cd2 * h2 + s2
        yv = y_ref[...]
        zv = z_ref[...]
        yz = yv * (zv * _sig(zv))
        gw = D_SSM // N_GROUPS
        for g in range(N_GROUPS):
            seg = yz[:, g * gw:(g + 1) * gw]
            r = lax.rsqrt(jnp.mean(seg * seg, axis=-1, keepdims=True) + RMS_EPS)
            yssm_ref[:, g * gw:(g + 1) * gw] = (seg * r * gn_ref[:, g * gw:(g + 1) * gw]).astype(BF16)

    return pl.pallas_call(
        body, grid=(n_tok // tl,), name="ssd_fwd",
        in_specs=[_tile(tl, D_XBC), _tile(tl, D_SSM), _tile(tl, LANES), _full((SUBLANES, D_XBC)), _full((1, D_XBC)),
                  _full((1, LANES)), _full((1, LANES)), _full((1, D_SSM)), _full((1, D_SSM))],
        out_specs=[_tile(tl, D_XBC), _tile(tl, D_SSM), _tile(tl, D_SSM),
                   pl.BlockSpec((nq, D_SSM, N_STATE), lambda i: (i, 0, 0))],
        out_shape=[jax.ShapeDtypeStruct((n_tok, D_XBC), F32), jax.ShapeDtypeStruct((n_tok, D_SSM), F32),
                   jax.ShapeDtypeStruct((n_tok, D_SSM), BF16), jax.ShapeDtypeStruct((n_tok // CHUNK, D_SSM, N_STATE), F32)],
        scratch_shapes=[pltpu.VMEM((tl + SUBLANES, D_XBC), F32), pltpu.VMEM((D_SSM, N_STATE), F32)],
        compiler_params=_params(),
    )(xbc_in, z, dtr, cw, cb, dtb, alog, dsk, gnorm)


def _ssd_bwd(dys, y, z, pre, xbc_in, dtr, hprev, cw, dtb, alog, dsk, gnorm, tl):
    n_tok = y.shape[0]
    n_t = n_tok // tl
    nq = tl // CHUNK

    def body(dys_ref, y_ref, z_ref, pre_ref, xin_ref, dtr_ref, hprev_ref, cw_ref, dtb_ref, alog_ref, dsk_ref, gn_ref,
             dxin_ref, dz_ref, ddtr_ref, dcw_ref, dcb_ref, dgn_ref, ddsk_ref, da_ref, ddtb_ref,
             dxs, dh, cs_s, dskc):
        i = pl.program_id(0)

        @pl.when(i == 0)
        def _():
            dcw_ref[...] = jnp.zeros_like(dcw_ref)
            dcb_ref[...] = jnp.zeros_like(dcb_ref)
            dgn_ref[...] = jnp.zeros_like(dgn_ref)
            da_ref[...] = jnp.zeros_like(da_ref)
            ddtb_ref[...] = jnp.zeros_like(ddtb_ref)
            dskc[...] = jnp.zeros_like(dskc)
            dh[...] = jnp.zeros_like(dh)
            dxs[tl:tl + SUBLANES, :] = jnp.zeros((SUBLANES, D_XBC), F32)

        yv = y_ref[...]
        zv = z_ref[...]
        dysv = dys_ref[...]
        sz = _sig(zv)
        silz = zv * sz
        yz = yv * silz
        gw = D_SSM // N_GROUPS
        dyz_parts = []
        for g in range(N_GROUPS):
            sl = slice(g * gw, (g + 1) * gw)
            seg = yz[:, sl]
            r = lax.rsqrt(jnp.mean(seg * seg, axis=-1, keepdims=True) + RMS_EPS)
            yzn = seg * r
            dgn_ref[:, sl] += _colsum(dysv[:, sl] * yzn)
            dyzn = dysv[:, sl] * gn_ref[:, sl]
            dyz_parts.append(r * (dyzn - yzn * jnp.mean(dyzn * yzn, axis=-1, keepdims=True)))
        dyz = jnp.concatenate(dyz_parts, axis=1)
        dy = dyz * silz
        dz_ref[...] = dyz * yv * _dsilu(zv, sz)

        prev = pre_ref[...]
        sp = _sig(prev)
        xbc = prev * sp
        dskc[...] += _colsum(dy * xbc[:, 0:D_SSM])
        dt_in = dtr_ref[...] + dtb_ref[...]
        dt = _softplus(dt_in)
        dsp = _sig(dt_in)
        a = -jnp.exp(alog_ref[...])
        adt = dt * a
        for q in reversed(range(nq)):
            r0 = q * CHUNK
            cm = _chunk_common(adt[r0:r0 + CHUNK, :])
            row, col, lo = cm["row"], cm["col"], cm["lo"]
            triu = (col >= row).astype(F32)
            dt_c = dt[r0:r0 + CHUNK, :]
            dacs = jnp.zeros((CHUNK, LANES), F32)
            ddtx = jnp.zeros((CHUNK, LANES), F32)
            cs_s[...] = jnp.zeros_like(cs_s)
            for g in range(N_GROUPS):
                bcol = D_SSM + g * N_STATE
                ccol = D_SSM + (N_GROUPS + g) * N_STATE
                bg = xbc[r0:r0 + CHUNK, bcol:bcol + N_STATE].astype(BF16)
                cg_ = xbc[r0:r0 + CHUNK, ccol:ccol + N_STATE].astype(BF16)
                gm = _dot_nt(cg_, bg)
                dgm = jnp.zeros((CHUNK, CHUNK), F32)
                dbg = jnp.zeros((CHUNK, N_STATE), F32)
                dcg = jnp.zeros((CHUNK, N_STATE), F32)
                for k in range(N_HEADS // N_GROUPS // 2):
                    ha = (N_HEADS // N_GROUPS) * g + 2 * k
                    hb = ha + 1
                    c0 = ha * HEAD_DIM
                    xh2 = xbc[r0:r0 + CHUNK, c0:c0 + LANES]
                    dt2 = _pick(cm, dt_c, ha)
                    x2 = xh2 * dt2
                    x2b = x2.astype(BF16)
                    la = _decay_mask(cm, ha)
                    lb = _decay_mask(cm, hb)
                    ma = gm * la
                    mb = gm * lb
                    dy2 = dy[r0:r0 + CHUNK, c0:c0 + LANES]
                    dy2b = dy2.astype(BF16)
                    dma = _dot_nt(jnp.where(lo, dy2, 0.0).astype(BF16), x2b)
                    dmb = _dot_nt(jnp.where(lo, 0.0, dy2).astype(BF16), x2b)
                    dx2 = jnp.where(lo, _dot(ma.T.astype(BF16), dy2b), _dot(mb.T.astype(BF16), dy2b))
                    qa = dma * ma
                    qb = dmb * mb
                    dgm = dgm + dma * la + dmb * lb
                    cs_s[ha:ha + 1, :] = _colsum(qa)
                    cs_s[hb:hb + 1, :] = _colsum(qb)
                    col_a = jnp.sum(qa, axis=1, keepdims=True)
                    col_b = jnp.sum(qb, axis=1, keepdims=True)
                    h2 = hprev_ref[q, c0:c0 + LANES, :]
                    h2b = h2.astype(BF16)
                    e2 = _pick(cm, cm["e"], ha)
                    yoff = _dot_nt(cg_, h2b) * e2
                    dz2 = dy2 * e2
                    dcg = dcg + _dot(dz2.astype(BF16), h2b)
                    t_e = dy2 * yoff
                    col_a = col_a + jnp.sum(jnp.where(lo, t_e, 0.0), axis=1, keepdims=True)
                    col_b = col_b + jnp.sum(jnp.where(lo, 0.0, t_e), axis=1, keepdims=True)
                    dhn = dh[c0:c0 + LANES, :]
                    dhnb = dhn.astype(BF16)
                    cd_a = cm["cd"][:, ha:ha + 1]
                    cd_b = cm["cd"][:, hb:hb + 1]
                    top = row < HEAD_DIM
                    hh = dhn * h2
                    dcd_a = jnp.sum(jnp.sum(jnp.where(top, hh, 0.0), axis=1, keepdims=True), axis=0, keepdims=True)
                    dcd_b = jnp.sum(jnp.sum(jnp.where(top, 0.0, hh), axis=1, keepdims=True), axis=0, keepdims=True)
                    dh[c0:c0 + LANES, :] = jnp.where(top, cd_a, cd_b) * dhn + _dot(dz2.T.astype(BF16), cg_)
                    w2 = _dot_nt(bg, dhnb)
                    dec2 = _pick(cm, cm["dec"], ha)
                    dx2 = dx2 + dec2 * w2
                    xw = x2 * w2
                    dec_a = cm["dec"][:, ha:ha + 1]
                    dec_b = cm["dec"][:, hb:hb + 1]
                    dd_a = jnp.sum(jnp.where(lo, xw, 0.0), axis=1, keepdims=True) * dec_a
                    dd_b = jnp.sum(jnp.where(lo, 0.0, xw), axis=1, keepdims=True) * dec_b
                    dbg = dbg + _dot((x2 * dec2).astype(BF16), dhnb)
                    is_last = row[:, 0:1] == CHUNK - 1
                    col_a = col_a - dd_a + jnp.where(is_last, dcd_a * cd_a + _colsum(dd_a), 0.0)
                    col_b = col_b - dd_b + jnp.where(is_last, dcd_b * cd_b + _colsum(dd_b), 0.0)
                    dacs = dacs + jnp.where(col == ha, col_a, 0.0) + jnp.where(col == hb, col_b, 0.0)
                    xhd = dx2 * xh2
                    ddtx = ddtx + jnp.where(col == ha, jnp.sum(jnp.where(lo, xhd, 0.0), axis=1, keepdims=True), 0.0) \
                        + jnp.where(col == hb, jnp.sum(jnp.where(lo, 0.0, xhd), axis=1, keepdims=True), 0.0)
                    dxs[r0:r0 + CHUNK, c0:c0 + LANES] = dx2 * dt2 + dsk_ref[:, c0:c0 + LANES] * dy2
                dgmb = dgm.astype(BF16)
                dxs[r0:r0 + CHUNK, bcol:bcol + N_STATE] = dbg + _dot(dgm.T.astype(BF16), cg_)
                dxs[r0:r0 + CHUNK, ccol:ccol + N_STATE] = dcg + _dot(dgmb, bg)
            dacs = dacs - cs_s[...].T
            dadt = jnp.dot(triu, dacs, precision=HIGHEST, preferred_element_type=F32)
            da_ref[...] += _colsum(dadt * dt_c)
            ddtr_c = (dadt * a + ddtx) * dsp[r0:r0 + CHUNK, :]
            ddtr_ref[r0:r0 + CHUNK, :] = ddtr_c
            ddtb_ref[...] += _colsum(ddtr_c)

        dpre = dxs[0:tl, :] * _dsilu(prev, sp)
        dxs[0:tl, :] = dpre
        dcb_ref[...] += _colsum(dpre)
        xin = xin_ref[...]
        dxin = jnp.zeros((tl, D_XBC), F32)
        for k in range(SSM_K):
            sh = dxs[SSM_K - 1 - k:SSM_K - 1 - k + tl, :]
            dxin = dxin + sh * cw_ref[k:k + 1, :]
            dcw_ref[k:k + 1, :] += _colsum(xin * sh)
        dxin_ref[...] = dxin
        dxs[tl:tl + SUBLANES, :] = dxs[0:SUBLANES, :]

        @pl.when(i == n_t - 1)
        def _():
            da_ref[...] = da_ref[...] * a
            sel = (lax.broadcasted_iota(jnp.int32, (D_SSM, LANES), 0) // HEAD_DIM
                   == lax.broadcasted_iota(jnp.int32, (D_SSM, LANES), 1)).astype(F32)
            rows = jnp.broadcast_to(dskc[...], (SUBLANES, D_SSM))
            ddsk_ref[...] = jnp.dot(rows, sel, precision=HIGHEST, preferred_element_type=F32)[0:1, :]

    rev = functools.partial(_tile, tl, rev_of=n_t)
    return pl.pallas_call(
        body, grid=(n_t,), name="ssd_bwd",
        in_specs=[rev(D_SSM), rev(D_SSM), rev(D_SSM), rev(D_XBC), rev(D_XBC), rev(LANES),
                  pl.BlockSpec((nq, D_SSM, N_STATE), lambda i: (n_t - 1 - i, 0, 0)),
                  _full((SUBLANES, D_XBC)), _full((1, LANES)), _full((1, LANES)), _full((1, D_SSM)), _full((1, D_SSM))],
        out_specs=[rev(D_XBC), rev(D_SSM), rev(LANES), _full((SUBLANES, D_XBC)), _full((1, D_XBC)), _full((1, D_SSM)),
                   _full((1, LANES)), _full((1, LANES)), _full((1, LANES))],
        out_shape=[jax.ShapeDtypeStruct((n_tok, D_XBC), F32), jax.ShapeDtypeStruct((n_tok, D_SSM), F32),
                   jax.ShapeDtypeStruct((n_tok, LANES), F32), jax.ShapeDtypeStruct((SUBLANES, D_XBC), F32),
                   jax.ShapeDtypeStruct((1, D_XBC), F32), jax.ShapeDtypeStruct((1, D_SSM), F32),
                   jax.ShapeDtypeStruct((1, LANES), F32), jax.ShapeDtypeStruct((1, LANES), F32),
                   jax.ShapeDtypeStruct((1, LANES), F32)],
        scratch_shapes=[pltpu.VMEM((tl + SUBLANES, D_XBC), F32), pltpu.VMEM((D_SSM, N_STATE), F32),
                        pltpu.VMEM((CHUNK, LANES), F32), pltpu.VMEM((1, D_SSM), F32)],
        compiler_params=_params(),
    )(dys, y, z, pre, xbc_in, dtr, hprev, cw, dtb, alog, dsk, gnorm)


CONF_HALO = 32
CONV_RS = 32
CONV_CS = 256


def _shift_copies(buf, shifts, n_rows):
    for r in range(1, SUBLANES):
        shifts[r - 1, 0:n_rows, :] = buf[r:r + n_rows, :]


def _tap_rows(buf, shifts, off, i0, c0):
    q, r = divmod(off, SUBLANES)
    rows = pl.ds(pl.multiple_of(i0 + SUBLANES * q, SUBLANES), CONV_RS)
    if r == 0:
        return buf[rows, c0:c0 + CONV_CS]
    return shifts[r - 1, rows, c0:c0 + CONV_CS]


def _conf_fwd(glu, cgate, bglu, cw, cb, lg, lb, tl):
    n_tok = glu.shape[0]

    def body(glu_ref, cg_ref, bglu_ref, cw_ref, cb_ref, lg_ref, lb_ref, u0_ref, u1_ref, yc_ref, buf, shifts):
        @pl.when(pl.program_id(0) == 0)
        def _():
            buf[0:CONF_HALO, :] = jnp.zeros((CONF_HALO, D_CONF), F32)

        gl = glu_ref[...] + bglu_ref[...]
        u0 = gl[:, 0:D_CONF] * _sig(gl[:, D_CONF:2 * D_CONF])
        u0_ref[...] = u0
        buf[CONF_HALO:CONF_HALO + tl, :] = u0
        _shift_copies(buf, shifts, tl + CONF_HALO - SUBLANES)

        def strip(rb, carry):
            i0 = pl.multiple_of(rb * CONV_RS, CONV_RS)
            for c0 in range(0, D_CONF, CONV_CS):
                acc = jnp.broadcast_to(cb_ref[:, c0:c0 + CONV_CS], (CONV_RS, CONV_CS))
                for k in range(CONF_K):
                    acc = acc + _tap_rows(buf, shifts, CONF_HALO - (CONF_K - 1) + k, i0, c0) * cw_ref[k:k + 1, c0:c0 + CONV_CS]
                u1_ref[pl.ds(i0, CONV_RS), c0:c0 + CONV_CS] = acc
            return carry

        lax.fori_loop(0, tl // CONV_RS, strip, 0)
        buf[0:CONF_HALO, :] = buf[tl:tl + CONF_HALO, :]
        u1 = u1_ref[...]
        xhat, _ = _ln_stats(u1)
        n = xhat * lg_ref[...] + lb_ref[...]
        cgv = cg_ref[...]
        yc_ref[...] = (n * _sig(n) * (cgv * _sig(cgv))).astype(BF16)

    return pl.pallas_call(
        body, grid=(n_tok // tl,), name="conf_fwd",
        in_specs=[_tile(tl, 2 * D_CONF), _tile(tl, D_CONF), _full((1, 2 * D_CONF)), _full((CONF_HALO, D_CONF)),
                  _full((1, D_CONF)), _full((1, D_CONF)), _full((1, D_CONF))],
        out_specs=[_tile(tl, D_CONF)] * 3,
        out_shape=[jax.ShapeDtypeStruct((n_tok, D_CONF), F32), jax.ShapeDtypeStruct((n_tok, D_CONF), F32),
                   jax.ShapeDtypeStruct((n_tok, D_CONF), BF16)],
        scratch_shapes=[pltpu.VMEM((tl + CONF_HALO, D_CONF), F32),
                        pltpu.VMEM((SUBLANES - 1, tl + CONF_HALO - SUBLANES, D_CONF), F32)],
        compiler_params=_params(),
    )(glu, cgate, bglu, cw, cb, lg, lb)


def _conf_bwd(dyc, u0, u1, glu, cgate, bglu, cw, lg, lb, tl):
    n_tok = glu.shape[0]
    n_t = n_tok // tl

    def body(dyc_ref, u0_ref, u1_ref, glu_ref, cg_ref, bglu_ref, cw_ref, lg_ref, lb_ref,
             dglu_ref, dcg_ref, dcw_ref, dbglu_ref, small_ref, buf, shifts, du0_s, dwp):
        @pl.when(pl.program_id(0) == 0)
        def _():
            dwp[...] = jnp.zeros_like(dwp)
            dbglu_ref[...] = jnp.zeros_like(dbglu_ref)
            small_ref[...] = jnp.zeros_like(small_ref)
            buf[tl:tl + CONF_HALO, :] = jnp.zeros((CONF_HALO, D_CONF), F32)

        xhat, rstd = _ln_stats(u1_ref[...])
        n = xhat * lg_ref[...] + lb_ref[...]
        sn = _sig(n)
        cgv = cg_ref[...]
        scg = _sig(cgv)
        dycv = dyc_ref[...]
        dcg_ref[...] = dycv * (n * sn) * _dsilu(cgv, scg)
        dn = dycv * (cgv * scg) * _dsilu(n, sn)
        small_ref[0:1, :] += _colsum(dn * xhat)
        small_ref[1:2, :] += _colsum(dn)
        du1 = _ln_bwd(dn * lg_ref[...], xhat, rstd)
        small_ref[2:3, :] += _colsum(du1)
        buf[0:tl, :] = du1
        _shift_copies(buf, shifts, tl + CONF_HALO - SUBLANES)

        def strip(rb, carry):
            i0 = pl.multiple_of(rb * CONV_RS, CONV_RS)
            for c0 in range(0, D_CONF, CONV_CS):
                u0s = u0_ref[pl.ds(i0, CONV_RS), c0:c0 + CONV_CS]
                acc = jnp.zeros((CONV_RS, CONV_CS), F32)
                for k in range(CONF_K):
                    sh = _tap_rows(buf, shifts, CONF_K - 1 - k, i0, c0)
                    acc = acc + sh * cw_ref[k:k + 1, c0:c0 + CONV_CS]
                    t = u0s * sh
                    part = t[0:SUBLANES]
                    for j in range(1, CONV_RS // SUBLANES):
                        part = part + t[j * SUBLANES:(j + 1) * SUBLANES]
                    dwp[k * SUBLANES:(k + 1) * SUBLANES, c0:c0 + CONV_CS] += part
                du0_s[pl.ds(i0, CONV_RS), c0:c0 + CONV_CS] = acc
            return carry

        lax.fori_loop(0, tl // CONV_RS, strip, 0)
        du0 = du0_s[...]
        buf[tl:tl + CONF_HALO, :] = buf[0:CONF_HALO, :]
        gl = glu_ref[...] + bglu_ref[...]
        sg = _sig(gl[:, D_CONF:2 * D_CONF])
        dgv = du0 * sg
        dgg = du0 * gl[:, 0:D_CONF] * sg * (1.0 - sg)
        dglu_ref[:, 0:D_CONF] = dgv
        dglu_ref[:, D_CONF:2 * D_CONF] = dgg
        dbglu_ref[:, 0:D_CONF] += _colsum(dgv)
        dbglu_ref[:, D_CONF:2 * D_CONF] += _colsum(dgg)

        @pl.when(pl.program_id(0) == n_t - 1)
        def _():
            for k in range(CONF_HALO):
                dcw_ref[k:k + 1, :] = _colsum(dwp[k * SUBLANES:(k + 1) * SUBLANES, :])

    rev = functools.partial(_tile, tl, rev_of=n_t)
    return pl.pallas_call(
        body, grid=(n_t,), name="conf_bwd",
        in_specs=[rev(D_CONF), rev(D_CONF), rev(D_CONF), rev(2 * D_CONF), rev(D_CONF), _full((1, 2 * D_CONF)),
                  _full((CONF_HALO, D_CONF)), _full((1, D_CONF)), _full((1, D_CONF))],
        out_specs=[rev(2 * D_CONF), rev(D_CONF), _full((CONF_HALO, D_CONF)), _full((1, 2 * D_CONF)), _full((SUBLANES, D_CONF))],
        out_shape=[jax.ShapeDtypeStruct((n_tok, 2 * D_CONF), F32), jax.ShapeDtypeStruct((n_tok, D_CONF), F32),
                   jax.ShapeDtypeStruct((CONF_HALO, D_CONF), F32), jax.ShapeDtypeStruct((1, 2 * D_CONF), F32),
                   jax.ShapeDtypeStruct((SUBLANES, D_CONF), F32)],
        scratch_shapes=[pltpu.VMEM((tl + CONF_HALO, D_CONF), F32),
                        pltpu.VMEM((SUBLANES - 1, tl + CONF_HALO - SUBLANES, D_CONF), F32),
                        pltpu.VMEM((tl, D_CONF), F32), pltpu.VMEM((CONF_HALO * SUBLANES, D_CONF), F32)],
        compiler_params=_params(),
    )(dyc, u0, u1, glu, cgate, bglu, cw, lg, lb)


def _tail(x, yssm, yconf, p, tgt, vec, w_out, wpg, wpp, tl):
    n_tok = x.shape[0]

    def body(x_ref, ys_ref, yc_ref, p_ref, t_ref, vec_ref, wo_ref, wg_ref, wp_ref,
             dmix_ref, dr1_ref, dr1b_ref, h1b_ref, dgb_ref, dpb_ref, small_ref, loss_ref):
        @pl.when(pl.program_id(0) == 0)
        def _():
            small_ref[...] = jnp.zeros_like(small_ref)
            loss_ref[...] = jnp.zeros_like(loss_ref)

        xh0, _ = _ln_stats(x_ref[...])
        h0 = xh0 * vec_ref[0:1, :] + vec_ref[1:2, :]
        out = _dot(ys_ref[...], wo_ref[0:D_SSM, :]) + _dot(yc_ref[...], wo_ref[D_SSM:D_SSM + D_CONF, :]) + vec_ref[2:3, :]
        xh1, rstd1 = _ln_stats(ALPHA * h0 + out)
        h1 = xh1 * vec_ref[3:4, :] + vec_ref[4:5, :]
        h1b = h1.astype(BF16)
        h1b_ref[...] = h1b
        gate = _sig(_dot(h1b, wg_ref[...]))
        ple = _dot(p_ref[...].astype(BF16), wp_ref[...])
        xh2, rstd2 = _ln_stats(ALPHA * h1 + gate * ple)
        h2 = xh2 * vec_ref[5:6, :] + vec_ref[6:7, :]
        diff = h2 - t_ref[...]
        part = jnp.sum(jnp.sum(diff * diff, axis=1, keepdims=True), axis=0, keepdims=True) * (0.5 / D_MODEL)
        loss_ref[...] += jnp.broadcast_to(part, loss_ref.shape)
        dh2 = diff * (1.0 / D_MODEL)
        small_ref[3:4, :] += _colsum(dh2 * xh2)
        small_ref[4:5, :] += _colsum(dh2)
        dr2 = _ln_bwd(dh2 * vec_ref[5:6, :], xh2, rstd2)
        dgpre = (dr2 * ple * gate * (1.0 - gate)).astype(BF16)
        dgb_ref[...] = dgpre
        dpb_ref[...] = (dr2 * gate).astype(BF16)
        dh1 = ALPHA * dr2 + _dot_nt(dgpre, wg_ref[...])
        small_ref[1:2, :] += _colsum(dh1 * xh1)
        small_ref[2:3, :] += _colsum(dh1)
        dr1 = _ln_bwd(dh1 * vec_ref[3:4, :], xh1, rstd1)
        small_ref[0:1, :] += _colsum(dr1)
        dr1_ref[...] = dr1
        dr1b = dr1.astype(BF16)
        dr1b_ref[...] = dr1b
        dmix_ref[...] = _dot_nt(dr1b, wo_ref[...])

    d_mix = D_SSM + D_CONF
    return pl.pallas_call(
        body, grid=(n_tok // tl,), name="tail",
        in_specs=[_tile(tl, D_MODEL), _tile(tl, D_SSM), _tile(tl, D_CONF), _tile(tl, D_PLE), _tile(tl, D_MODEL),
                  _full((SUBLANES, D_MODEL)), _full((d_mix, D_MODEL), True), _full((D_MODEL, D_MODEL), True),
                  _full((D_PLE, D_MODEL), True)],
        out_specs=[_tile(tl, d_mix), _tile(tl, D_MODEL), _tile(tl, D_MODEL), _tile(tl, D_MODEL), _tile(tl, D_MODEL),
                   _tile(tl, D_MODEL), _full((SUBLANES, D_MODEL)), _full((SUBLANES, LANES))],
        out_shape=[jax.ShapeDtypeStruct((n_tok, d_mix), F32), jax.ShapeDtypeStruct((n_tok, D_MODEL), F32),
                   jax.ShapeDtypeStruct((n_tok, D_MODEL), BF16), jax.ShapeDtypeStruct((n_tok, D_MODEL), BF16),
                   jax.ShapeDtypeStruct((n_tok, D_MODEL), BF16), jax.ShapeDtypeStruct((n_tok, D_MODEL), BF16),
                   jax.ShapeDtypeStruct((SUBLANES, D_MODEL), F32), jax.ShapeDtypeStruct((SUBLANES, LANES), F32)],
        compiler_params=_params(),
    )(x, yssm, yconf, p, tgt, vec, w_out, wpg, wpp)


def _inproj_bwd(dxin, dz, dglu, dcg, ddtr, dr1, x, g, b, w_r, tl):
    n_tok = x.shape[0]

    def body(dxin_ref, dz_ref, dglu_ref, dcg_ref, ddtr_ref, dr1_ref, x_ref, g_ref, b_ref, w_ref,
             dx_ref, dpb_ref, small_ref):
        @pl.when(pl.program_id(0) == 0)
        def _():
            small_ref[...] = jnp.zeros_like(small_ref)

        dh0 = ALPHA * dr1_ref[...]
        for ref, lo, hi in ((dxin_ref, R_XBC, R_Z), (dz_ref, R_Z, R_GLU), (dglu_ref, R_GLU, R_CG), (dcg_ref, R_CG, R_DT),
                            (ddtr_ref, R_DT, D_INR)):
            piece = ref[...].astype(BF16)
            dpb_ref[:, lo:hi] = piece
            dh0 = dh0 + _dot_nt(piece, w_ref[:, lo:hi])
        xhat, rstd = _ln_stats(x_ref[...])
        small_ref[0:1, :] += _colsum(dh0 * xhat)
        small_ref[1:2, :] += _colsum(dh0)
        dx_ref[...] = _ln_bwd(dh0 * g_ref[...], xhat, rstd)

    return pl.pallas_call(
        body, grid=(n_tok // tl,), name="inproj_bwd",
        in_specs=[_tile(tl, D_XBC), _tile(tl, D_SSM), _tile(tl, 2 * D_CONF), _tile(tl, D_CONF), _tile(tl, LANES),
                  _tile(tl, D_MODEL), _tile(tl, D_MODEL), _full((1, D_MODEL)), _full((1, D_MODEL)),
                  _full((D_MODEL, D_INR), True)],
        out_specs=[_tile(tl, D_MODEL), _tile(tl, D_INR), _full((SUBLANES, D_MODEL))],
        out_shape=[jax.ShapeDtypeStruct((n_tok, D_MODEL), F32), jax.ShapeDtypeStruct((n_tok, D_INR), BF16),
                   jax.ShapeDtypeStruct((SUBLANES, D_MODEL), F32)],
        compiler_params=_params(),
    )(dxin, dz, dglu, dcg, ddtr, dr1, x, g, b, w_r)


def _tn_matmul(a, b, name, tn, tl):
    n_tok, m = a.shape
    n = b.shape[1]

    def body(a_ref, b_ref, o_ref):
        @pl.when(pl.program_id(1) == 0)
        def _():
            o_ref[...] = jnp.zeros_like(o_ref)

        o_ref[...] += lax.dot_general(a_ref[...], b_ref[...], TN_DIMS, preferred_element_type=F32)

    return pl.pallas_call(
        body, grid=(n // tn, n_tok // tl), name=name,
        in_specs=[pl.BlockSpec((tl, m), lambda j, l: (l, 0)), pl.BlockSpec((tl, tn), lambda j, l: (l, j))],
        out_specs=pl.BlockSpec((m, tn), lambda j, l: (0, j)),
        out_shape=jax.ShapeDtypeStruct((m, n), F32),
        compiler_params=pltpu.CompilerParams(dimension_semantics=("parallel", "arbitrary"), vmem_limit_bytes=VMEM_LIMIT),
    )(a, b)


def _pad_rows(a, rows):
    return jnp.pad(a, ((0, rows - a.shape[0]), (0, 0)))


def _pad_lanes(a):
    return jnp.pad(a, ((0, 0), (0, LANES - a.shape[1])))


def _local_grads(x, p, tgt, w_r, w_out_b, wpg_b, wpp_b, ssm_cw, conf_cw, sm):
    n_tok = x.shape[0]
    tl = min(256, n_tok)
    row = lambda v: v.reshape(1, -1)
    g_e, b_e = row(sm["ln_emb_g"]), row(sm["ln_emb_b"])
    h0b, xbc_in, z, glu, cgate, dtr = _ln_inproj(x, g_e, b_e, w_r, tl)

    cw4 = _pad_rows(ssm_cw, SUBLANES)
    dtb, alog = _pad_lanes(sm["dt_bias"]), _pad_lanes(sm["a_log"])
    dsk = jnp.repeat(sm["d_skip"], HEAD_DIM, axis=1)
    pre, y, yssm, hprev = _ssd_fwd(xbc_in, z, dtr, cw4, sm["ssm_conv_b"], dtb, alog, dsk, sm["ssm_norm_g"], tl)

    cw31 = _pad_rows(conf_cw, CONF_HALO)
    u0, u1, yconf = _conf_fwd(glu, cgate, sm["b_glu"], cw31, sm["conf_conv_b"], sm["conf_ln_g"], sm["conf_ln_b"], tl)

    vec = jnp.concatenate([g_e, b_e, sm["b_out"], sm["ln1_g"], sm["ln1_b"], sm["ln2_g"], sm["ln2_b"],
                           jnp.zeros((1, D_MODEL), F32)], axis=0)
    dmix, dr1, dr1b, h1b, dgb, dpb, small_t, loss = _tail(
        x, yssm, yconf, p, tgt, vec, w_out_b, wpg_b, wpp_b, tl)

    dglu, dcg, dcw31, dbglu, small_c = _conf_bwd(dmix[:, D_SSM:], u0, u1, glu, cgate, sm["b_glu"], cw31,
                                                  sm["conf_ln_g"], sm["conf_ln_b"], tl)
    dxin, dz, ddtr, dcw4, dcb4, dgn, ddsk, dalog, ddtb = _ssd_bwd(
        dmix[:, :D_SSM], y, z, pre, xbc_in, dtr, hprev, cw4, dtb, alog, dsk, sm["ssm_norm_g"], tl)
    dx, dprojb, small_e = _inproj_bwd(dxin, dz, dglu, dcg, ddtr, dr1, x, g_e, b_e, w_r, tl)

    tlm = min(512, n_tok)
    dw_r = _tn_matmul(h0b, dprojb, "dw_in", 640, tlm)
    dw_out = jnp.concatenate([_tn_matmul(yssm, dr1b, "dw_out_ssm", 512, tlm),
                              _tn_matmul(yconf, dr1b, "dw_out_conf", 512, tlm)], axis=0)
    dwpg = _tn_matmul(h1b, dgb, "dw_ple_gate", 512, tlm)
    dwpp = _tn_matmul(p.astype(BF16), dpb, "dw_ple_proj", 512, tlm)
    grads = dict(
        ln_emb_g=small_e[0], ln_emb_b=small_e[1], w_in=dw_r, ssm_conv_w=dcw4[0:SSM_K], ssm_conv_b=dcb4,
        dt_bias=ddtb[:, 0:N_HEADS], a_log=dalog[:, 0:N_HEADS], d_skip=ddsk[:, 0:N_HEADS], ssm_norm_g=dgn, b_glu=dbglu,
        conf_conv_w=dcw31[0:CONF_K], conf_conv_b=small_c[2:3], conf_ln_g=small_c[0:1], conf_ln_b=small_c[1:2],
        w_out=dw_out, b_out=small_t[0:1], ln1_g=small_t[1:2], ln1_b=small_t[2:3], w_ple_gate=dwpg, w_ple_proj=dwpp,
        ln2_g=small_t[3:4], ln2_b=small_t[4:5])
    return loss[0, 0], dx, grads


N_CHIPS = 4
N_DEV = 8
W_IN_SH = D_IN // N_CHIPS
BIG = (("w_in", D_MODEL, W_IN_SH), ("w_out", (D_SSM + D_CONF) // N_CHIPS, D_MODEL),
       ("w_ple_gate", D_MODEL // N_CHIPS, D_MODEL), ("w_ple_proj", D_PLE, D_MODEL // N_CHIPS))
SEGS = ((R_XBC, 0, D_SSM), (R_XBC + D_SSM, 2048, 256), (R_XBC + D_SSM + 256, 2304, 256), (R_Z, 1024, D_SSM),
        (R_GLU, 2576, 2 * D_CONF), (R_CG, 4624, D_CONF), (R_DT, 2560, N_HEADS))
ROWS_CW4 = 2
ROWS_CW31 = 8
ROWS_CONV = 16
SMALL_ROWS = 56
SMALL_LAYOUT = (("ln_emb_g", 0, 1), ("ln_emb_b", 1, 1), ("ssm_conv_b", 2, 2), ("dt_bias", 4, 1), ("a_log", 5, 1),
                ("d_skip", 6, 1), ("ssm_norm_g", 7, 1), ("b_glu", 8, 2), ("conf_conv_b", 10, 1), ("conf_ln_g", 11, 1),
                ("conf_ln_b", 12, 1), ("b_out", 13, 1), ("ln1_g", 14, 1), ("ln1_b", 15, 1), ("ln2_g", 16, 1), ("ln2_b", 17, 1))
CONV_LAYOUT = (("ssm_conv_w", 18, 6, (SSM_K, D_XBC)), ("conf_conv_w", 24, 31, (CONF_K, D_CONF)))


def _rows_of(v, rows):
    flat = v.reshape(-1)
    return jnp.pad(flat, (0, rows * D_MODEL - flat.shape[0])).reshape(rows, D_MODEL)


def _pack_small(d):
    parts = [_rows_of(d[n], r) for n, _, r in SMALL_LAYOUT]
    for n, _, r, _ in CONV_LAYOUT:
        parts.append(_rows_of(d[n], r) if n in d else jnp.zeros((r, D_MODEL), F32))
    parts.append(jnp.zeros((SMALL_ROWS - 55, D_MODEL), F32))
    return jnp.concatenate(parts, axis=0)


def _unpack_small(a, like):
    return {n: a[r0:r0 + r].reshape(-1)[:like[n].size].reshape(like[n].shape) for n, r0, r in SMALL_LAYOUT}


def _w_r_from_shards(w4):
    parts = []
    for _, o, wd in SEGS:
        lo, hi = o, o + wd
        while lo < hi:
            s = lo // W_IN_SH
            e = min(hi, (s + 1) * W_IN_SH)
            parts.append(w4[s][:, lo - s * W_IN_SH:e - s * W_IN_SH])
            lo = e
    parts.append(jnp.zeros((D_MODEL, LANES - N_HEADS), w4.dtype))
    return jnp.concatenate(parts, axis=1)


def _shards_from_dw_r(dw_r):
    slabs = []
    for s in range(N_CHIPS):
        lo, hi = s * W_IN_SH, (s + 1) * W_IN_SH
        parts = []
        for kcol, o, wd in sorted(SEGS, key=lambda t: t[1]):
            a, b = max(lo, o), min(hi, o + wd)
            if a < b:
                parts.append(dw_r[:, kcol + a - o:kcol + b - o])
        slabs.append(jnp.concatenate(parts, axis=1))
    return jnp.stack(slabs)


def _row_chunks(rows, n):
    return [(j * (rows // n), rows // n) for j in range(n)]


def _my_place():
    return lax.axis_index("x"), lax.axis_index("y"), lax.axis_index("c")


MESH_ID = pl.DeviceIdType.MESH
ANY = pl.BlockSpec(memory_space=pl.ANY)
CHIP_FLIPS = ((1, 0), (0, 1), (1, 1))


def _remote(src, dst, send_sem, recv_sem, peer):
    return pltpu.make_async_remote_copy(src, dst, send_sem, recv_sem, device_id=peer, device_id_type=MESH_ID)


GATHER_CHUNKS = (4, 2, 1, 1)


def _gather_weights(shards, conv_f):
    n_big = len(BIG)
    plan = [(a, o, n, rows // 2) for a, (_, rows, _) in enumerate(BIG) for o, n in _row_chunks(rows // 2, GATHER_CHUNKS[a])]
    own_plan = [(a, o, n) for a, (_, rows, _) in enumerate(BIG) for o, n in _row_chunks(rows, 2 * GATHER_CHUNKS[a])]
    own_plan.append((n_big, 0, ROWS_CONV))

    def body(*refs):
        ins, outs = refs[0:n_big + 1], refs[n_big + 1:2 * n_big + 2]
        send_a, recv_a, send_b, recv_b, loc = refs[2 * n_big + 2:]
        x, y, c = _my_place()
        s = 2 * x + y
        sibling = (x, y, 1 - c)
        own = [pltpu.make_async_copy(ins[a].at[pl.ds(o, n)], outs[a].at[s, pl.ds(o, n)], loc.at[j])
               for j, (a, o, n) in enumerate(own_plan)]
        first = []
        for k, (fx, fy) in enumerate(CHIP_FLIPS):
            peer = (x ^ fx, y ^ fy, c)
            for j, (a, o, n, h) in enumerate(plan):
                r0 = pl.multiple_of(c * h + o, 16)
                first.append(_remote(ins[a].at[pl.ds(r0, n)], outs[a].at[s, pl.ds(r0, n)], send_a.at[k, j], recv_a.at[k, j], peer))
            first.append(_remote(ins[n_big], outs[n_big].at[s], send_a.at[k, len(plan)], recv_a.at[k, len(plan)], peer))
        for cp in own + first:
            cp.start()
        passed = []
        for k, (fx, fy) in enumerate(CHIP_FLIPS):
            peer = (x ^ fx, y ^ fy, c)
            sk = 2 * (x ^ fx) + (y ^ fy)
            for j, (a, o, n, h) in enumerate(plan):
                land = outs[a].at[sk, pl.ds(pl.multiple_of(c * h + o, 16), n)]
                _remote(land, land, send_a.at[k, j], recv_a.at[k, j], peer).wait_recv()
                cp = _remote(land, land, send_b.at[k, j], recv_b.at[k, j], sibling)
                cp.start()
                passed.append(cp)
            land = outs[n_big].at[sk]
            _remote(land, land, send_a.at[k, len(plan)], recv_a.at[k, len(plan)], peer).wait_recv()
        for k, (fx, fy) in enumerate(CHIP_FLIPS):
            sk = 2 * (x ^ fx) + (y ^ fy)
            for j, (a, o, n, h) in enumerate(plan):
                land = outs[a].at[sk, pl.ds(pl.multiple_of((1 - c) * h + o, 16), n)]
                _remote(land, land, send_b.at[k, j], recv_b.at[k, j], sibling).wait_recv()
        for cp in first + passed:
            cp.wait_send()
        for cp in own:
            cp.wait()

    arrays = list(shards) + [conv_f]
    return pl.pallas_call(
        body, name="gather_weights", in_specs=[ANY] * len(arrays), out_specs=[ANY] * len(arrays),
        out_shape=[jax.ShapeDtypeStruct((N_CHIPS,) + a.shape, a.dtype) for a in arrays],
        scratch_shapes=[pltpu.SemaphoreType.DMA((3, len(plan) + 1)), pltpu.SemaphoreType.DMA((3, len(plan) + 1)),
                        pltpu.SemaphoreType.DMA((3, len(plan))), pltpu.SemaphoreType.DMA((3, len(plan))),
                        pltpu.SemaphoreType.DMA((len(own_plan),))],
    )(*arrays)


CORES_CHUNKS = (4, 2, 1, 1)
CHIPS_CHUNKS = (2, 1, 1, 1)
SHARE_CHUNKS = (8, 4, 2, 1)


def _exchange_cores(gbig, spack):
    n_big = len(BIG)
    plan = [(a, t, o, n, rows // 2) for a, (_, rows, _) in enumerate(BIG) for t in range(N_CHIPS)
            for o, n in _row_chunks(rows // 2, CORES_CHUNKS[a])]

    def body(*refs):
        g_refs, s_ref = refs[0:n_big], refs[n_big]
        theirs, small_all = refs[n_big + 1:2 * n_big + 1], refs[2 * n_big + 1]
        send_sems, recv_sems, send_small, recv_small, loc_sem = refs[2 * n_big + 2:]
        x, y, c = _my_place()
        me = 4 * x + 2 * y + c
        own = pltpu.make_async_copy(s_ref, small_all.at[me], loc_sem)
        sends = []
        for j, (a, t, o, n, h) in enumerate(plan):
            give = pl.multiple_of((1 - c) * h + o, SUBLANES)
            sends.append(_remote(g_refs[a].at[t, pl.ds(give, n)], theirs[a].at[t, pl.ds(o, n)], send_sems.at[j], recv_sems.at[j],
                                 (x, y, 1 - c)))
        for m in range(1, N_DEV):
            peer = (x ^ (m >> 2), y ^ ((m >> 1) & 1), c ^ (m & 1))
            sends.append(_remote(s_ref, small_all.at[me], send_small.at[m - 1], recv_small.at[m - 1], peer))
        own.start()
        for cp in sends:
            cp.start()
        for cp in sends:
            cp.wait()
        own.wait()

    return pl.pallas_call(
        body, name="exchange_cores", in_specs=[ANY] * (n_big + 1), out_specs=[ANY] * (n_big + 1),
        out_shape=[jax.ShapeDtypeStruct((N_CHIPS, rows // 2, cols), F32) for _, rows, cols in BIG]
        + [jax.ShapeDtypeStruct((N_DEV, SMALL_ROWS, D_MODEL), F32)],
        scratch_shapes=[pltpu.SemaphoreType.DMA((len(plan),)), pltpu.SemaphoreType.DMA((len(plan),)),
                        pltpu.SemaphoreType.DMA((N_DEV - 1,)), pltpu.SemaphoreType.DMA((N_DEV - 1,)), pltpu.SemaphoreType.DMA],
    )(*gbig, spack)


def _exchange_chips(psums):
    n_big = len(BIG)
    plan = [(a, o, n) for a, (_, rows, _) in enumerate(BIG) for o, n in _row_chunks(rows // 2, CHIPS_CHUNKS[a])]

    def body(*refs):
        p_refs, got = refs[0:n_big], refs[n_big:2 * n_big]
        send_sems, recv_sems = refs[2 * n_big:]
        x, y, c = _my_place()
        sends = []
        for k, (fx, fy) in enumerate(CHIP_FLIPS):
            tx, ty = x ^ fx, y ^ fy
            for j, (a, o, n) in enumerate(plan):
                sends.append(_remote(p_refs[a].at[2 * tx + ty, pl.ds(o, n)], got[a].at[k, pl.ds(o, n)], send_sems.at[k, j],
                                     recv_sems.at[k, j], (tx, ty, c)))
        for cp in sends:
            cp.start()
        for cp in sends:
            cp.wait()

    return pl.pallas_call(
        body, name="exchange_chips", in_specs=[ANY] * n_big, out_specs=[ANY] * n_big,
        out_shape=[jax.ShapeDtypeStruct((3, rows // 2, cols), BF16) for _, rows, cols in BIG],
        scratch_shapes=[pltpu.SemaphoreType.DMA((3, len(plan))), pltpu.SemaphoreType.DMA((3, len(plan)))],
    )(*psums)


def _share_halves(tots):
    n_big = len(BIG)
    plan = [(a, o, n) for a, (_, rows, _) in enumerate(BIG) for o, n in _row_chunks(rows // 2, SHARE_CHUNKS[a])]

    def body(*refs):
        t_refs, both = refs[0:n_big], refs[n_big:2 * n_big]
        send_sems, recv_sems, loc = refs[2 * n_big:]
        x, y, c = _my_place()
        own = [pltpu.make_async_copy(t_refs[a].at[pl.ds(o, n)], both[a].at[c, pl.ds(o, n)], loc.at[j])
               for j, (a, o, n) in enumerate(plan)]
        sends = [_remote(t_refs[a].at[pl.ds(o, n)], both[a].at[c, pl.ds(o, n)], send_sems.at[j], recv_sems.at[j], (x, y, 1 - c))
                 for j, (a, o, n) in enumerate(plan)]
        for cp in own + sends:
            cp.start()
        for cp in sends:
            cp.wait()
        for cp in own:
            cp.wait()

    return pl.pallas_call(
        body, name="share_halves", in_specs=[ANY] * n_big, out_specs=[ANY] * n_big,
        out_shape=[jax.ShapeDtypeStruct((2, rows // 2, cols), F32) for _, rows, cols in BIG],
        scratch_shapes=[pltpu.SemaphoreType.DMA((len(plan),)), pltpu.SemaphoreType.DMA((len(plan),)),
                        pltpu.SemaphoreType.DMA((len(plan),))],
    )(*tots)


def _sum_rows(half):
    return half if half <= 256 else 256


def _add_cores(core, g, theirs, name):
    _, rows, cols = g.shape
    half = rows // 2
    tr = _sum_rows(half)
    nb = half // tr

    def body(c_ref, g_ref, t_ref, o_ref):
        o_ref[...] = (g_ref[...] + t_ref[...]).astype(BF16)

    grid_spec = pltpu.PrefetchScalarGridSpec(
        num_scalar_prefetch=1, grid=(N_CHIPS, nb),
        in_specs=[pl.BlockSpec((1, tr, cols), lambda t, i, c_ref: (t, c_ref[0] * nb + i, 0)),
                  pl.BlockSpec((1, tr, cols), lambda t, i, c_ref: (t, i, 0))],
        out_specs=pl.BlockSpec((1, tr, cols), lambda t, i, c_ref: (t, i, 0)))
    return pl.pallas_call(body, grid_spec=grid_spec, name=name, out_shape=jax.ShapeDtypeStruct((N_CHIPS, half, cols), BF16),
                          compiler_params=pltpu.CompilerParams(vmem_limit_bytes=VMEM_LIMIT))(core, g, theirs)


def _add_chips(chip, psum, got, name):
    _, half, cols = psum.shape
    tr = _sum_rows(half)

    def body(s_ref, p_ref, g_ref, o_ref):
        o_ref[...] = ((p_ref[0].astype(F32) + g_ref[0].astype(F32)) + g_ref[1].astype(F32)) + g_ref[2].astype(F32)

    grid_spec = pltpu.PrefetchScalarGridSpec(
        num_scalar_prefetch=1, grid=(half // tr,),
        in_specs=[pl.BlockSpec((1, tr, cols), lambda i, s_ref: (s_ref[0], i, 0)),
                  pl.BlockSpec((3, tr, cols), lambda i, s_ref: (0, i, 0))],
        out_specs=pl.BlockSpec((tr, cols), lambda i, s_ref: (i, 0)))
    return pl.pallas_call(body, grid_spec=grid_spec, name=name, out_shape=jax.ShapeDtypeStruct((half, cols), F32),
                          compiler_params=pltpu.CompilerParams(vmem_limit_bytes=VMEM_LIMIT))(chip, psum, got)


def _adam_math(w, g, m, v):
    m = ADAM_B1 * m + (1.0 - ADAM_B1) * g
    v = ADAM_B2 * v + (1.0 - ADAM_B2) * (g * g)
    m_hat = m / (1.0 - ADAM_B1 ** ADAM_STEP)
    v_hat = v / (1.0 - ADAM_B2 ** ADAM_STEP)
    return -ADAM_LR * (m_hat / (jnp.sqrt(v_hat) + ADAM_EPS) + ADAM_WD * w), m, v


def _adam(w, g, m, v, name):
    rows, cols = w.shape
    tr = rows if rows <= 256 else 256

    def body(w_ref, g_ref, m_ref, v_ref, d_ref, nm_ref, nv_ref):
        d_ref[...], nm_ref[...], nv_ref[...] = _adam_math(w_ref[...], g_ref[...], m_ref[...], v_ref[...])

    spec = pl.BlockSpec((tr, cols), lambda i: (i, 0))
    return pl.pallas_call(body, grid=(rows // tr,), name=name, in_specs=[spec] * 4, out_specs=[spec] * 3,
                          out_shape=[jax.ShapeDtypeStruct(w.shape, F32)] * 3, compiler_params=_params(seq=False))(w, g, m, v)


def _adam_small(w, parts, m, v):
    def body(w_ref, p_ref, m_ref, v_ref, g_ref, d_ref, nm_ref, nv_ref):
        g = p_ref[0]
        for k in range(1, N_DEV):
            g = g + p_ref[k]
        g_ref[...] = g
        d_ref[...], nm_ref[...], nv_ref[...] = _adam_math(w_ref[...], g, m_ref[...], v_ref[...])

    return pl.pallas_call(body, name="adam_small", out_shape=[jax.ShapeDtypeStruct(w.shape, F32)] * 4)(w, parts, m, v)


def kernel(x, p, ln_emb_g, ln_emb_b, w_in, ssm_conv_w, ssm_conv_b, dt_bias, a_log, d_skip, ssm_norm_g, b_glu, conf_conv_w, conf_conv_b, conf_ln_g, conf_ln_b, w_out, b_out, ln1_g, ln1_b, w_ple_gate, w_ple_proj, ln2_g, ln2_b, loss_target, m_ln_emb_g, m_ln_emb_b, m_w_in, m_ssm_conv_w, m_ssm_conv_b, m_dt_bias, m_a_log, m_d_skip, m_ssm_norm_g, m_b_glu, m_conf_conv_w, m_conf_conv_b, m_conf_ln_g, m_conf_ln_b, m_w_out, m_b_out, m_ln1_g, m_ln1_b, m_w_ple_gate, m_w_ple_proj, m_ln2_g, m_ln2_b, v_ln_emb_g, v_ln_emb_b, v_w_in, v_ssm_conv_w, v_ssm_conv_b, v_dt_bias, v_a_log, v_d_skip, v_ssm_norm_g, v_b_glu, v_conf_conv_w, v_conf_conv_b, v_conf_ln_g, v_conf_ln_b, v_w_out, v_b_out, v_ln1_g, v_ln1_b, v_w_ple_gate, v_w_ple_proj, v_ln2_g, v_ln2_b):
    order = ("ln_emb_g", "ln_emb_b", "w_in", "ssm_conv_w", "ssm_conv_b", "dt_bias", "a_log", "d_skip", "ssm_norm_g", "b_glu",
             "conf_conv_w", "conf_conv_b", "conf_ln_g", "conf_ln_b", "w_out", "b_out", "ln1_g", "ln1_b", "w_ple_gate",
             "w_ple_proj", "ln2_g", "ln2_b")
    w = dict(zip(order, (ln_emb_g, ln_emb_b, w_in, ssm_conv_w, ssm_conv_b, dt_bias, a_log, d_skip, ssm_norm_g, b_glu,
                         conf_conv_w, conf_conv_b, conf_ln_g, conf_ln_b, w_out, b_out, ln1_g, ln1_b, w_ple_gate, w_ple_proj,
                         ln2_g, ln2_b)))
    m = dict(zip(order, (m_ln_emb_g, m_ln_emb_b, m_w_in, m_ssm_conv_w, m_ssm_conv_b, m_dt_bias, m_a_log, m_d_skip,
                         m_ssm_norm_g, m_b_glu, m_conf_conv_w, m_conf_conv_b, m_conf_ln_g, m_conf_ln_b, m_w_out, m_b_out,
                         m_ln1_g, m_ln1_b, m_w_ple_gate, m_w_ple_proj, m_ln2_g, m_ln2_b)))
    v = dict(zip(order, (v_ln_emb_g, v_ln_emb_b, v_w_in, v_ssm_conv_w, v_ssm_conv_b, v_dt_bias, v_a_log, v_d_skip,
                         v_ssm_norm_g, v_b_glu, v_conf_conv_w, v_conf_conv_b, v_conf_ln_g, v_conf_ln_b, v_w_out, v_b_out,
                         v_ln1_g, v_ln1_b, v_w_ple_gate, v_w_ple_proj, v_ln2_g, v_ln2_b)))

    conv_f = jnp.concatenate([_rows_of(w["ssm_conv_w"], ROWS_CW4), _rows_of(w["conf_conv_w"], ROWS_CW31),
                              jnp.zeros((ROWS_CONV - ROWS_CW4 - ROWS_CW31, D_MODEL), F32)], axis=0)
    w_in_all, w_out_all, wpg_all, wpp_all, conv_all = _gather_weights([w[n][0].astype(BF16) for n, _, _ in BIG], conv_f)
    w_r = _w_r_from_shards(w_in_all)
    w_out_b = w_out_all.reshape(D_SSM + D_CONF, D_MODEL)
    wpg_b = wpg_all.reshape(D_MODEL, D_MODEL)
    wpp_b = wpp_all.transpose(1, 0, 2).reshape(D_PLE, D_MODEL)
    cw4 = conv_all[:, 0:ROWS_CW4].reshape(N_CHIPS, -1)[:, :SSM_K * D_XBC // N_CHIPS]
    cw4 = cw4.reshape(N_CHIPS, SSM_K, D_XBC // N_CHIPS).transpose(1, 0, 2).reshape(SSM_K, D_XBC)
    cw31 = conv_all[:, ROWS_CW4:ROWS_CW4 + ROWS_CW31].reshape(N_CHIPS, -1)[:, :CONF_K * D_CONF // N_CHIPS]
    cw31 = cw31.reshape(N_CHIPS, CONF_K, D_CONF // N_CHIPS).transpose(1, 0, 2).reshape(CONF_K, D_CONF)

    small_names = [n for n, _, _ in SMALL_LAYOUT]
    sm = {n: w[n] for n in small_names}
    loss_part, dx, grads = _local_grads(x[0], p[0, 0], loss_target[0], w_r, w_out_b, wpg_b, wpp_b, cw4, cw31, sm)
    loss = lax.psum(loss_part, ("x", "y", "c"))

    core = lax.axis_index("c").astype(jnp.int32).reshape(1)
    chip_i = 2 * lax.axis_index("x") + lax.axis_index("y")
    chip = chip_i.astype(jnp.int32).reshape(1)
    gbig = [_shards_from_dw_r(grads["w_in"]), grads["w_out"].reshape(N_CHIPS, -1, D_MODEL),
            grads["w_ple_gate"].reshape(N_CHIPS, -1, D_MODEL),
            grads["w_ple_proj"].reshape(D_PLE, N_CHIPS, D_MODEL // N_CHIPS).transpose(1, 0, 2)]
    spack = _pack_small({n: grads[n] for n in small_names + [n for n, _, _, _ in CONV_LAYOUT]})
    *theirs, small_all = _exchange_cores(gbig, spack)
    psums = [_add_cores(core, g, t, "add_cores_" + n) for g, t, (n, _, _) in zip(gbig, theirs, BIG)]
    got = _exchange_chips(psums)
    tots = [_add_chips(chip, ps, gt, "add_chips_" + n) for ps, gt, (n, _, _) in zip(psums, got, BIG)]
    both = _share_halves(tots)

    out_g, out_d, out_m, out_v = {}, {}, {}, {}
    g_s, d_s, m_s, v_s = _adam_small(_pack_small(sm), small_all, _pack_small({n: m[n] for n in small_names}),
                                     _pack_small({n: v[n] for n in small_names}))
    g_shard = {n: b.reshape(rows, cols) for b, (n, rows, cols) in zip(both, BIG)}
    for n, r0, r, shape in CONV_LAYOUT:
        whole = g_s[r0:r0 + r].reshape(-1)[:shape[0] * shape[1]].reshape(shape)
        g_shard[n] = lax.dynamic_slice_in_dim(whole, chip_i * (shape[1] // N_CHIPS), shape[1] // N_CHIPS, axis=1)
    for n, g in g_shard.items():
        shape = w[n].shape
        d, nm, nv = _adam(w[n][0], g, m[n][0], v[n][0], "adam_" + n)
        out_g[n], out_d[n], out_m[n], out_v[n] = g[None], d[None], nm[None], nv[None]
    for dst, src in ((out_g, g_s), (out_d, d_s), (out_m, m_s), (out_v, v_s)):
        dst.update(_unpack_small(src, sm))
    return (loss, dx[None], *[out_g[n] for n in order], *[out_d[n] for n in order], *[out_m[n] for n in order],
            *[out_v[n] for n in order])
```

```python
import functools

import jax
import jax.numpy as jnp
from jax import lax
from jax.experimental import pallas as pl
from jax.experimental.pallas import tpu as pltpu

F32 = jnp.float32
BF16 = jnp.bfloat16

D_MODEL = 1024
D_PLE = 256
D_SSM = 1024
D_CONF = 1024
N_HEADS = 16
HEAD_DIM = 64
N_GROUPS = 2
N_STATE = 128
CHUNK = 128
SSM_K = 4
CONF_K = 31
D_XBC = D_SSM + 2 * N_GROUPS * N_STATE
D_IN = 5648
R_XBC, R_Z, R_GLU, R_CG, R_DT, D_INR = 0, 1536, 2560, 4608, 5632, 5760
LN_EPS = 1e-5
RMS_EPS = 1e-5
ALPHA = 2.0 ** 0.25
ADAM_LR, ADAM_B1, ADAM_B2, ADAM_EPS, ADAM_WD, ADAM_STEP = 0.001, 0.9, 0.999, 1e-08, 0.01, 10
NEG_BIG = -1e30
LANES = 128
SUBLANES = 8
VMEM_LIMIT = 56 * 1024 * 1024
HIGHEST = lax.Precision.HIGHEST
NT_DIMS = (((1,), (1,)), ((), ()))
TN_DIMS = (((0,), (0,)), ((), ()))


def _sig(v):
    return jax.nn.sigmoid(v)


def _dsilu(v, s):
    return s * (1.0 + v * (1.0 - s))


def _ln_stats(v):
    mu = jnp.mean(v, axis=-1, keepdims=True)
    c = v - mu
    var = jnp.mean(c * c, axis=-1, keepdims=True)
    rstd = lax.rsqrt(var + LN_EPS)
    return c * rstd, rstd


def _ln_bwd(dxhat, xhat, rstd):
    m1 = jnp.mean(dxhat, axis=-1, keepdims=True)
    m2 = jnp.mean(dxhat * xhat, axis=-1, keepdims=True)
    return rstd * (dxhat - m1 - xhat * m2)


def _softplus(v):
    return jnp.maximum(v, 0.0) + jnp.log1p(jnp.exp(-jnp.abs(v)))


def _colsum(v):
    return jnp.sum(v, axis=0, keepdims=True)


def _dot(a, b):
    return jnp.dot(a, b, preferred_element_type=F32)


def _dot_nt(a, b):
    return lax.dot_general(a, b, NT_DIMS, preferred_element_type=F32)


def _tile(tl, c, rev_of=None):
    if rev_of is None:
        return pl.BlockSpec((tl, c), lambda i: (i, 0))
    return pl.BlockSpec((tl, c), lambda i: (rev_of - 1 - i, 0))


def _full(shape, single=False):
    nd = len(shape)
    if single:
        return pl.BlockSpec(shape, lambda i: (0,) * nd, pipeline_mode=pl.Buffered(1))
    return pl.BlockSpec(shape, lambda i: (0,) * nd)


def _params(seq=True):
    return pltpu.CompilerParams(dimension_semantics=("arbitrary",) if seq else ("parallel",), vmem_limit_bytes=VMEM_LIMIT)


def _ln_inproj(x, g, b, w_r, tl):
    n_tok = x.shape[0]

    def body(x_ref, g_ref, b_ref, w_ref, h0b_ref, xbc_ref, z_ref, glu_ref, cg_ref, dtr_ref):
        xhat, _ = _ln_stats(x_ref[...])
        hb = (xhat * g_ref[...] + b_ref[...]).astype(BF16)
        h0b_ref[...] = hb
        xbc_ref[...] = _dot(hb, w_ref[:, R_XBC:R_Z])
        z_ref[...] = _dot(hb, w_ref[:, R_Z:R_GLU])
        glu_ref[...] = _dot(hb, w_ref[:, R_GLU:R_CG])
        cg_ref[...] = _dot(hb, w_ref[:, R_CG:R_DT])
        dtr_ref[...] = _dot(hb, w_ref[:, R_DT:D_INR])

    widths = (D_MODEL, D_XBC, D_SSM, 2 * D_CONF, D_CONF, LANES)
    dtypes = (BF16, F32, F32, F32, F32, F32)
    return pl.pallas_call(
        body, grid=(n_tok // tl,), name="ln_inproj",
        in_specs=[_tile(tl, D_MODEL), _full((1, D_MODEL)), _full((1, D_MODEL)), _full((D_MODEL, D_INR), single=True)],
        out_specs=[_tile(tl, w) for w in widths],
        out_shape=[jax.ShapeDtypeStruct((n_tok, w), dt) for w, dt in zip(widths, dtypes)],
        compiler_params=_params(seq=False),
    )(x, g, b, w_r)


def _chunk_common(adt_c):
    row = lax.broadcasted_iota(jnp.int32, (CHUNK, CHUNK), 0)
    col = lax.broadcasted_iota(jnp.int32, (CHUNK, CHUNK), 1)
    tril = row >= col
    acs = jnp.dot(tril.astype(F32), adt_c, precision=HIGHEST, preferred_element_type=F32)
    last = acs[CHUNK - 1:CHUNK, :]
    return dict(row=row, col=col, tril=tril, lo=col < HEAD_DIM, acs=acs, acs_t=acs.T, e=jnp.exp(acs),
                dec=jnp.exp(last - acs), cd=jnp.exp(last))


def _pick(cm, v, ha):
    return jnp.where(cm["lo"], v[:, ha:ha + 1], v[:, ha + 1:ha + 2])


def _decay_mask(cm, h):
    return jnp.exp(jnp.where(cm["tril"], cm["acs"][:, h:h + 1] - cm["acs_t"][h:h + 1, :], NEG_BIG))


def _ssd_fwd(xbc_in, z, dtr, cw, cb, dtb, alog, dsk, gnorm, tl):
    n_tok = xbc_in.shape[0]
    nq = tl // CHUNK

    def body(xin_ref, z_ref, dtr_ref, cw_ref, cb_ref, dtb_ref, alog_ref, dsk_ref, gn_ref,
             pre_ref, y_ref, yssm_ref, hprev_ref, buf, hst):
        @pl.when(pl.program_id(0) == 0)
        def _():
            buf[0:SUBLANES, :] = jnp.zeros((SUBLANES, D_XBC), F32)
            hst[...] = jnp.zeros_like(hst)

        buf[SUBLANES:SUBLANES + tl, :] = xin_ref[...]
        pre = cb_ref[...] + jnp.zeros((tl, D_XBC), F32)
        for k in range(SSM_K):
            off = SUBLANES - (SSM_K - 1) + k
            pre = pre + buf[off:off + tl, :] * cw_ref[k:k + 1, :]
        buf[0:SUBLANES, :] = buf[tl:tl + SUBLANES, :]
        pre_ref[...] = pre
        xbc = pre * _sig(pre)
        dt = _softplus(dtr_ref[...] + dtb_ref[...])
        a = -jnp.exp(alog_ref[...])
        adt = dt * a
        for q in range(nq):
            r0 = q * CHUNK
            cm = _chunk_common(adt[r0:r0 + CHUNK, :])
            dt_c = dt[r0:r0 + CHUNK, :]
            for g in range(N_GROUPS):
                bg = xbc[r0:r0 + CHUNK, D_SSM + g * N_STATE:D_SSM + (g + 1) * N_STATE].astype(BF16)
                cg_ = xbc[r0:r0 + CHUNK, D_SSM + (N_GROUPS + g) * N_STATE:D_SSM + (N_GROUPS + g + 1) * N_STATE].astype(BF16)
                gm = _dot_nt(cg_, bg)
                for k in range(N_HEADS // N_GROUPS // 2):
                    ha = (N_HEADS // N_GROUPS) * g + 2 * k
                    c0 = ha * HEAD_DIM
                    xh2 = xbc[r0:r0 + CHUNK, c0:c0 + LANES]
                    x2 = xh2 * _pick(cm, dt_c, ha)
                    x2b = x2.astype(BF16)
                    ya = _dot((gm * _decay_mask(cm, ha)).astype(BF16), x2b)
                    yb = _dot((gm * _decay_mask(cm, ha + 1)).astype(BF16), x2b)
                    h2 = hst[c0:c0 + LANES, :]
                    hprev_ref[q, c0:c0 + LANES, :] = h2
                    z2 = _dot_nt(cg_, h2.astype(BF16))
                    y2 = jnp.where(cm["lo"], ya, yb) + z2 * _pick(cm, cm["e"], ha) + dsk_ref[:, c0:c0 + LANES] * xh2
                    y_ref[r0:r0 + CHUNK, c0:c0 + LANES] = y2
                    s2 = _dot((x2 * _pick(cm, cm["dec"], ha)).T.astype(BF16), bg)
                    cd2 = jnp.where(cm["row"] < HEAD_DIM, cm["cd"][:, ha:ha + 1], cm["cd"][:, ha + 1:ha + 2])
                    hst[c0:c0 + LANES, :] = cd2 * h2 + s2
        yv = y_ref[...]
        zv = z_ref[...]
        yz = yv * (zv * _sig(zv))
        gw = D_SSM // N_GROUPS
        for g in range(N_GROUPS):
            seg = yz[:, g * gw:(g + 1) * gw]
            r = lax.rsqrt(jnp.mean(seg * seg, axis=-1, keepdims=True) + RMS_EPS)
            yssm_ref[:, g * gw:(g + 1) * gw] = (seg * r * gn_ref[:, g * gw:(g + 1) * gw]).astype(BF16)

    return pl.pallas_call(
        body, grid=(n_tok // tl,), name="ssd_fwd",
        in_specs=[_tile(tl, D_XBC), _tile(tl, D_SSM), _tile(tl, LANES), _full((SUBLANES, D_XBC)), _full((1, D_XBC)),
                  _full((1, LANES)), _full((1, LANES)), _full((1, D_SSM)), _full((1, D_SSM))],
        out_specs=[_tile(tl, D_XBC), _tile(tl, D_SSM), _tile(tl, D_SSM),
                   pl.BlockSpec((nq, D_SSM, N_STATE), lambda i: (i, 0, 0))],
        out_shape=[jax.ShapeDtypeStruct((n_tok, D_XBC), F32), jax.ShapeDtypeStruct((n_tok, D_SSM), F32),
                   jax.ShapeDtypeStruct((n_tok, D_SSM), BF16), jax.ShapeDtypeStruct((n_tok // CHUNK, D_SSM, N_STATE), F32)],
        scratch_shapes=[pltpu.VMEM((tl + SUBLANES, D_XBC), F32), pltpu.VMEM((D_SSM, N_STATE), F32)],
        compiler_params=_params(),
    )(xbc_in, z, dtr, cw, cb, dtb, alog, dsk, gnorm)


def _ssd_bwd(dys, y, z, pre, xbc_in, dtr, hprev, cw, dtb, alog, dsk, gnorm, tl):
    n_tok = y.shape[0]
    n_t = n_tok // tl
    nq = tl // CHUNK

    def body(dys_ref, y_ref, z_ref, pre_ref, xin_ref, dtr_ref, hprev_ref, cw_ref, dtb_ref, alog_ref, dsk_ref, gn_ref,
             dxin_ref, dz_ref, ddtr_ref, dcw_ref, dcb_ref, dgn_ref, ddsk_ref, da_ref, ddtb_ref,
             dxs, dh, cs_s, dskc):
        i = pl.program_id(0)

        @pl.when(i == 0)
        def _():
            dcw_ref[...] = jnp.zeros_like(dcw_ref)
            dcb_ref[...] = jnp.zeros_like(dcb_ref)
            dgn_ref[...] = jnp.zeros_like(dgn_ref)
            da_ref[...] = jnp.zeros_like(da_ref)
            ddtb_ref[...] = jnp.zeros_like(ddtb_ref)
            dskc[...] = jnp.zeros_like(dskc)
            dh[...] = jnp.zeros_like(dh)
            dxs[tl:tl + SUBLANES, :] = jnp.zeros((SUBLANES, D_XBC), F32)

        yv = y_ref[...]
        zv = z_ref[...]
        dysv = dys_ref[...]
        sz = _sig(zv)
        silz = zv * sz
        yz = yv * silz
        gw = D_SSM // N_GROUPS
        dyz_parts = []
        for g in range(N_GROUPS):
            sl = slice(g * gw, (g + 1) * gw)
            seg = yz[:, sl]
            r = lax.rsqrt(jnp.mean(seg * seg, axis=-1, keepdims=True) + RMS_EPS)
            yzn = seg * r
            dgn_ref[:, sl] += _colsum(dysv[:, sl] * yzn)
            dyzn = dysv[:, sl] * gn_ref[:, sl]
            dyz_parts.append(r * (dyzn - yzn * jnp.mean(dyzn * yzn, axis=-1, keepdims=True)))
        dyz = jnp.concatenate(dyz_parts, axis=1)
        dy = dyz * silz
        dz_ref[...] = dyz * yv * _dsilu(zv, sz)

        prev = pre_ref[...]
        sp = _sig(prev)
        xbc = prev * sp
        dskc[...] += _colsum(dy * xbc[:, 0:D_SSM])
        dt_in = dtr_ref[...] + dtb_ref[...]
        dt = _softplus(dt_in)
        dsp = _sig(dt_in)
        a = -jnp.exp(alog_ref[...])
        adt = dt * a
        for q in reversed(range(nq)):
            r0 = q * CHUNK
            cm = _chunk_common(adt[r0:r0 + CHUNK, :])
            row, col, lo = cm["row"], cm["col"], cm["lo"]
            triu = (col >= row).astype(F32)
            dt_c = dt[r0:r0 + CHUNK, :]
            dacs = jnp.zeros((CHUNK, LANES), F32)
            ddtx = jnp.zeros((CHUNK, LANES), F32)
            cs_s[...] = jnp.zeros_like(cs_s)
            for g in range(N_GROUPS):
                bcol = D_SSM + g * N_STATE
                ccol = D_SSM + (N_GROUPS + g) * N_STATE
                bg = xbc[r0:r0 + CHUNK, bcol:bcol + N_STATE].astype(BF16)
                cg_ = xbc[r0:r0 + CHUNK, ccol:ccol + N_STATE].astype(BF16)
                gm = _dot_nt(cg_, bg)
                dgm = jnp.zeros((CHUNK, CHUNK), F32)
                dbg = jnp.zeros((CHUNK, N_STATE), F32)
                dcg = jnp.zeros((CHUNK, N_STATE), F32)
                for k in range(N_HEADS // N_GROUPS // 2):
                    ha = (N_HEADS // N_GROUPS) * g + 2 * k
                    hb = ha + 1
                    c0 = ha * HEAD_DIM
                    xh2 = xbc[r0:r0 + CHUNK, c0:c0 + LANES]
                    dt2 = _pick(cm, dt_c, ha)
                    x2 = xh2 * dt2
                    x2b = x2.astype(BF16)
                    la = _decay_mask(cm, ha)
                    lb = _decay_mask(cm, hb)
                    ma = gm * la
                    mb = gm * lb
                    dy2 = dy[r0:r0 + CHUNK, c0:c0 + LANES]
                    dy2b = dy2.astype(BF16)
                    dma = _dot_nt(jnp.where(lo, dy2, 0.0).astype(BF16), x2b)
                    dmb = _dot_nt(jnp.where(lo, 0.0, dy2).astype(BF16), x2b)
                    dx2 = jnp.where(lo, _dot(ma.T.astype(BF16), dy2b), _dot(mb.T.astype(BF16), dy2b))
                    qa = dma * ma
                    qb = dmb * mb
                    dgm = dgm + dma * la + dmb * lb
                    cs_s[ha:ha + 1, :] = _colsum(qa)
                    cs_s[hb:hb + 1, :] = _colsum(qb)
                    col_a = jnp.sum(qa, axis=1, keepdims=True)
                    col_b = jnp.sum(qb, axis=1, keepdims=True)
                    h2 = hprev_ref[q, c0:c0 + LANES, :]
                    h2b = h2.astype(BF16)
                    e2 = _pick(cm, cm["e"], ha)
                    yoff = _dot_nt(cg_, h2b) * e2
                    dz2 = dy2 * e2
                    dcg = dcg + _dot(dz2.astype(BF16), h2b)
                    t_e = dy2 * yoff
                    col_a = col_a + jnp.sum(jnp.where(lo, t_e, 0.0), axis=1, keepdims=True)
                    col_b = col_b + jnp.sum(jnp.where(lo, 0.0, t_e), axis=1, keepdims=True)
                    dhn = dh[c0:c0 + LANES, :]
                    dhnb = dhn.astype(BF16)
                    cd_a = cm["cd"][:, ha:ha + 1]
                    cd_b = cm["cd"][:, hb:hb + 1]
                    top = row < HEAD_DIM
                    hh = dhn * h2
                    dcd_a = jnp.sum(jnp.sum(jnp.where(top, hh, 0.0), axis=1, keepdims=True), axis=0, keepdims=True)
                    dcd_b = jnp.sum(jnp.sum(jnp.where(top, 0.0, hh), axis=1, keepdims=True), axis=0, keepdims=True)
                    dh[c0:c0 + LANES, :] = jnp.where(top, cd_a, cd_b) * dhn + _dot(dz2.T.astype(BF16), cg_)
                    w2 = _dot_nt(bg, dhnb)
                    dec2 = _pick(cm, cm["dec"], ha)
                    dx2 = dx2 + dec2 * w2
                    xw = x2 * w2
                    dec_a = cm["dec"][:, ha:ha + 1]
                    dec_b = cm["dec"][:, hb:hb + 1]
                    dd_a = jnp.sum(jnp.where(lo, xw, 0.0), axis=1, keepdims=True) * dec_a
                    dd_b = jnp.sum(jnp.where(lo, 0.0, xw), axis=1, keepdims=True) * dec_b
                    dbg = dbg + _dot((x2 * dec2).astype(BF16), dhnb)
                    is_last = row[:, 0:1] == CHUNK - 1
                    col_a = col_a - dd_a + jnp.where(is_last, dcd_a * cd_a + _colsum(dd_a), 0.0)
                    col_b = col_b - dd_b + jnp.where(is_last, dcd_b * cd_b + _colsum(dd_b), 0.0)
                    dacs = dacs + jnp.where(col == ha, col_a, 0.0) + jnp.where(col == hb, col_b, 0.0)
                    xhd = dx2 * xh2
                    ddtx = ddtx + jnp.where(col == ha, jnp.sum(jnp.where(lo, xhd, 0.0), axis=1, keepdims=True), 0.0) \
                        + jnp.where(col == hb, jnp.sum(jnp.where(lo, 0.0, xhd), axis=1, keepdims=True), 0.0)
                    dxs[r0:r0 + CHUNK, c0:c0 + LANES] = dx2 * dt2 + dsk_ref[:, c0:c0 + LANES] * dy2
                dgmb = dgm.astype(BF16)
                dxs[r0:r0 + CHUNK, bcol:bcol + N_STATE] = dbg + _dot(dgm.T.astype(BF16), cg_)
                dxs[r0:r0 + CHUNK, ccol:ccol + N_STATE] = dcg + _dot(dgmb, bg)
            dacs = dacs - cs_s[...].T
            dadt = jnp.dot(triu, dacs, precision=HIGHEST, preferred_element_type=F32)
            da_ref[...] += _colsum(dadt * dt_c)
            ddtr_c = (dadt * a + ddtx) * dsp[r0:r0 + CHUNK, :]
            ddtr_ref[r0:r0 + CHUNK, :] = ddtr_c
            ddtb_ref[...] += _colsum(ddtr_c)

        dpre = dxs[0:tl, :] * _dsilu(prev, sp)
        dxs[0:tl, :] = dpre
        dcb_ref[...] += _colsum(dpre)
        xin = xin_ref[...]
        dxin = jnp.zeros((tl, D_XBC), F32)
        for k in range(SSM_K):
            sh = dxs[SSM_K - 1 - k:SSM_K - 1 - k + tl, :]
            dxin = dxin + sh * cw_ref[k:k + 1, :]
            dcw_ref[k:k + 1, :] += _colsum(xin * sh)
        dxin_ref[...] = dxin
        dxs[tl:tl + SUBLANES, :] = dxs[0:SUBLANES, :]

        @pl.when(i == n_t - 1)
        def _():
            da_ref[...] = da_ref[...] * a
            sel = (lax.broadcasted_iota(jnp.int32, (D_SSM, LANES), 0) // HEAD_DIM
                   == lax.broadcasted_iota(jnp.int32, (D_SSM, LANES), 1)).astype(F32)
            rows = jnp.broadcast_to(dskc[...], (SUBLANES, D_SSM))
            ddsk_ref[...] = jnp.dot(rows, sel, precision=HIGHEST, preferred_element_type=F32)[0:1, :]

    rev = functools.partial(_tile, tl, rev_of=n_t)
    return pl.pallas_call(
        body, grid=(n_t,), name="ssd_bwd",
        in_specs=[rev(D_SSM), rev(D_SSM), rev(D_SSM), rev(D_XBC), rev(D_XBC), rev(LANES),
                  pl.BlockSpec((nq, D_SSM, N_STATE), lambda i: (n_t - 1 - i, 0, 0)),
                  _full((SUBLANES, D_XBC)), _full((1, LANES)), _full((1, LANES)), _full((1, D_SSM)), _full((1, D_SSM))],
        out_specs=[rev(D_XBC), rev(D_SSM), rev(LANES), _full((SUBLANES, D_XBC)), _full((1, D_XBC)), _full((1, D_SSM)),
                   _full((1, LANES)), _full((1, LANES)), _full((1, LANES))],
        out_shape=[jax.ShapeDtypeStruct((n_tok, D_XBC), F32), jax.ShapeDtypeStruct((n_tok, D_SSM), F32),
                   jax.ShapeDtypeStruct((n_tok, LANES), F32), jax.ShapeDtypeStruct((SUBLANES, D_XBC), F32),
                   jax.ShapeDtypeStruct((1, D_XBC), F32), jax.ShapeDtypeStruct((1, D_SSM), F32),
                   jax.ShapeDtypeStruct((1, LANES), F32), jax.ShapeDtypeStruct((1, LANES), F32),
                   jax.ShapeDtypeStruct((1, LANES), F32)],
        scratch_shapes=[pltpu.VMEM((tl + SUBLANES, D_XBC), F32), pltpu.VMEM((D_SSM, N_STATE), F32),
                        pltpu.VMEM((CHUNK, LANES), F32), pltpu.VMEM((1, D_SSM), F32)],
        compiler_params=_params(),
    )(dys, y, z, pre, xbc_in, dtr, hprev, cw, dtb, alog, dsk, gnorm)


CONF_HALO = 32
CONV_RS = 32
CONV_CS = 256


def _shift_copies(buf, shifts, n_rows):
    for r in range(1, SUBLANES):
        shifts[r - 1, 0:n_rows, :] = buf[r:r + n_rows, :]


def _tap_rows(buf, shifts, off, i0, c0):
    q, r = divmod(off, SUBLANES)
    rows = pl.ds(pl.multiple_of(i0 + SUBLANES * q, SUBLANES), CONV_RS)
    if r == 0:
        return buf[rows, c0:c0 + CONV_CS]
    return shifts[r - 1, rows, c0:c0 + CONV_CS]


def _conf_fwd(glu, cgate, bglu, cw, cb, lg, lb, tl):
    n_tok = glu.shape[0]

    def body(glu_ref, cg_ref, bglu_ref, cw_ref, cb_ref, lg_ref, lb_ref, u0_ref, u1_ref, yc_ref, buf, shifts):
        @pl.when(pl.program_id(0) == 0)
        def _():
            buf[0:CONF_HALO, :] = jnp.zeros((CONF_HALO, D_CONF), F32)

        gl = glu_ref[...] + bglu_ref[...]
        u0 = gl[:, 0:D_CONF] * _sig(gl[:, D_CONF:2 * D_CONF])
        u0_ref[...] = u0
        buf[CONF_HALO:CONF_HALO + tl, :] = u0
        _shift_copies(buf, shifts, tl + CONF_HALO - SUBLANES)

        def strip(rb, carry):
            i0 = pl.multiple_of(rb * CONV_RS, CONV_RS)
            for c0 in range(0, D_CONF, CONV_CS):
                acc = jnp.broadcast_to(cb_ref[:, c0:c0 + CONV_CS], (CONV_RS, CONV_CS))
                for k in range(CONF_K):
                    acc = acc + _tap_rows(buf, shifts, CONF_HALO - (CONF_K - 1) + k, i0, c0) * cw_ref[k:k + 1, c0:c0 + CONV_CS]
                u1_ref[pl.ds(i0, CONV_RS), c0:c0 + CONV_CS] = acc
            return carry

        lax.fori_loop(0, tl // CONV_RS, strip, 0)
        buf[0:CONF_HALO, :] = buf[tl:tl + CONF_HALO, :]
        u1 = u1_ref[...]
        xhat, _ = _ln_stats(u1)
        n = xhat * lg_ref[...] + lb_ref[...]
        cgv = cg_ref[...]
        yc_ref[...] = (n * _sig(n) * (cgv * _sig(cgv))).astype(BF16)

    return pl.pallas_call(
        body, grid=(n_tok // tl,), name="conf_fwd",
        in_specs=[_tile(tl, 2 * D_CONF), _tile(tl, D_CONF), _full((1, 2 * D_CONF)), _full((CONF_HALO, D_CONF)),
                  _full((1, D_CONF)), _full((1, D_CONF)), _full((1, D_CONF))],
        out_specs=[_tile(tl, D_CONF)] * 3,
        out_shape=[jax.ShapeDtypeStruct((n_tok, D_CONF), F32), jax.ShapeDtypeStruct((n_tok, D_CONF), F32),
                   jax.ShapeDtypeStruct((n_tok, D_CONF), BF16)],
        scratch_shapes=[pltpu.VMEM((tl + CONF_HALO, D_CONF), F32),
                        pltpu.VMEM((SUBLANES - 1, tl + CONF_HALO - SUBLANES, D_CONF), F32)],
        compiler_params=_params(),
    )(glu, cgate, bglu, cw, cb, lg, lb)


def _conf_bwd(dyc, dyc_block, u0, u1, glu, cgate, bglu, cw, lg, lb, tl):
    n_tok = glu.shape[0]
    n_t = n_tok // tl

    def body(dyc_ref, u0_ref, u1_ref, glu_ref, cg_ref, bglu_ref, cw_ref, lg_ref, lb_ref,
             dglu_ref, dcg_ref, dcw_ref, dbglu_ref, small_ref, buf, shifts, du0_s, dwp):
        @pl.when(pl.program_id(0) == 0)
        def _():
            dwp[...] = jnp.zeros_like(dwp)
            dbglu_ref[...] = jnp.zeros_like(dbglu_ref)
            small_ref[...] = jnp.zeros_like(small_ref)
            buf[tl:tl + CONF_HALO, :] = jnp.zeros((CONF_HALO, D_CONF), F32)

        xhat, rstd = _ln_stats(u1_ref[...])
        n = xhat * lg_ref[...] + lb_ref[...]
        sn = _sig(n)
        cgv = cg_ref[...]
        scg = _sig(cgv)
        dycv = dyc_ref[...]
        dcg_ref[...] = dycv * (n * sn) * _dsilu(cgv, scg)
        dn = dycv * (cgv * scg) * _dsilu(n, sn)
        small_ref[0:1, :] += _colsum(dn * xhat)
        small_ref[1:2, :] += _colsum(dn)
        du1 = _ln_bwd(dn * lg_ref[...], xhat, rstd)
        small_ref[2:3, :] += _colsum(du1)
        buf[0:tl, :] = du1
        _shift_copies(buf, shifts, tl + CONF_HALO - SUBLANES)

        def strip(rb, carry):
            i0 = pl.multiple_of(rb * CONV_RS, CONV_RS)
            for c0 in range(0, D_CONF, CONV_CS):
                u0s = u0_ref[pl.ds(i0, CONV_RS), c0:c0 + CONV_CS]
                acc = jnp.zeros((CONV_RS, CONV_CS), F32)
                for k in range(CONF_K):
                    sh = _tap_rows(buf, shifts, CONF_K - 1 - k, i0, c0)
                    acc = acc + sh * cw_ref[k:k + 1, c0:c0 + CONV_CS]
                    t = u0s * sh
                    part = t[0:SUBLANES]
                    for j in range(1, CONV_RS // SUBLANES):
                        part = part + t[j * SUBLANES:(j + 1) * SUBLANES]
                    dwp[k * SUBLANES:(k + 1) * SUBLANES, c0:c0 + CONV_CS] += part
                du0_s[pl.ds(i0, CONV_RS), c0:c0 + CONV_CS] = acc
            return carry

        lax.fori_loop(0, tl // CONV_RS, strip, 0)
        du0 = du0_s[...]
        buf[tl:tl + CONF_HALO, :] = buf[0:CONF_HALO, :]
        gl = glu_ref[...] + bglu_ref[...]
        sg = _sig(gl[:, D_CONF:2 * D_CONF])
        dgv = du0 * sg
        dgg = du0 * gl[:, 0:D_CONF] * sg * (1.0 - sg)
        dglu_ref[:, 0:D_CONF] = dgv
        dglu_ref[:, D_CONF:2 * D_CONF] = dgg
        dbglu_ref[:, 0:D_CONF] += _colsum(dgv)
        dbglu_ref[:, D_CONF:2 * D_CONF] += _colsum(dgg)

        @pl.when(pl.program_id(0) == n_t - 1)
        def _():
            for k in range(CONF_HALO):
                dcw_ref[k:k + 1, :] = _colsum(dwp[k * SUBLANES:(k + 1) * SUBLANES, :])

    rev = functools.partial(_tile, tl, rev_of=n_t)
    return pl.pallas_call(
        body, grid=(n_t,), name="conf_bwd",
        in_specs=[pl.BlockSpec((tl, D_CONF), lambda i: (n_t - 1 - i, dyc_block)),
                  rev(D_CONF), rev(D_CONF), rev(2 * D_CONF), rev(D_CONF), _full((1, 2 * D_CONF)),
                  _full((CONF_HALO, D_CONF)), _full((1, D_CONF)), _full((1, D_CONF))],
        out_specs=[rev(2 * D_CONF), rev(D_CONF), _full((CONF_HALO, D_CONF)), _full((1, 2 * D_CONF)), _full((SUBLANES, D_CONF))],
        out_shape=[jax.ShapeDtypeStruct((n_tok, 2 * D_CONF), F32), jax.ShapeDtypeStruct((n_tok, D_CONF), F32),
                   jax.ShapeDtypeStruct((CONF_HALO, D_CONF), F32), jax.ShapeDtypeStruct((1, 2 * D_CONF), F32),
                   jax.ShapeDtypeStruct((SUBLANES, D_CONF), F32)],
        scratch_shapes=[pltpu.VMEM((tl + CONF_HALO, D_CONF), F32),
                        pltpu.VMEM((SUBLANES - 1, tl + CONF_HALO - SUBLANES, D_CONF), F32),
                        pltpu.VMEM((tl, D_CONF), F32), pltpu.VMEM((CONF_HALO * SUBLANES, D_CONF), F32)],
        compiler_params=_params(),
    )(dyc, u0, u1, glu, cgate, bglu, cw, lg, lb)


def _tail(x, yssm, yconf, p, tgt, vec, w_out, wpg, wpp, tl):
    n_tok = x.shape[0]

    def body(x_ref, ys_ref, yc_ref, p_ref, t_ref, vec_ref, wo_ref, wg_ref, wp_ref,
             dmix_ref, dr1_ref, dr1b_ref, h1b_ref, dgb_ref, dpb_ref, small_ref, loss_ref):
        @pl.when(pl.program_id(0) == 0)
        def _():
            small_ref[...] = jnp.zeros_like(small_ref)
            loss_ref[...] = jnp.zeros_like(loss_ref)

        xh0, _ = _ln_stats(x_ref[...])
        h0 = xh0 * vec_ref[0:1, :] + vec_ref[1:2, :]
        out = _dot(ys_ref[...], wo_ref[0:D_SSM, :]) + _dot(yc_ref[...], wo_ref[D_SSM:D_SSM + D_CONF, :]) + vec_ref[2:3, :]
        xh1, rstd1 = _ln_stats(ALPHA * h0 + out)
        h1 = xh1 * vec_ref[3:4, :] + vec_ref[4:5, :]
        h1b = h1.astype(BF16)
        h1b_ref[...] = h1b
        gate = _sig(_dot(h1b, wg_ref[...]))
        ple = _dot(p_ref[...].astype(BF16), wp_ref[...])
        xh2, rstd2 = _ln_stats(ALPHA * h1 + gate * ple)
        h2 = xh2 * vec_ref[5:6, :] + vec_ref[6:7, :]
        diff = h2 - t_ref[...]
        part = jnp.sum(jnp.sum(diff * diff, axis=1, keepdims=True), axis=0, keepdims=True) * (0.5 / D_MODEL)
        loss_ref[...] += jnp.broadcast_to(part, loss_ref.shape)
        dh2 = diff * (1.0 / D_MODEL)
        small_ref[3:4, :] += _colsum(dh2 * xh2)
        small_ref[4:5, :] += _colsum(dh2)
        dr2 = _ln_bwd(dh2 * vec_ref[5:6, :], xh2, rstd2)
        dgpre = (dr2 * ple * gate * (1.0 - gate)).astype(BF16)
        dgb_ref[...] = dgpre
        dpb_ref[...] = (dr2 * gate).astype(BF16)
        dh1 = ALPHA * dr2 + _dot_nt(dgpre, wg_ref[...])
        small_ref[1:2, :] += _colsum(dh1 * xh1)
        small_ref[2:3, :] += _colsum(dh1)
        dr1 = _ln_bwd(dh1 * vec_ref[3:4, :], xh1, rstd1)
        small_ref[0:1, :] += _colsum(dr1)
        dr1_ref[...] = dr1
        dr1b = dr1.astype(BF16)
        dr1b_ref[...] = dr1b
        dmix_ref[...] = _dot_nt(dr1b, wo_ref[...])

    d_mix = D_SSM + D_CONF
    return pl.pallas_call(
        body, grid=(n_tok // tl,), name="tail",
        in_specs=[_tile(tl, D_MODEL), _tile(tl, D_SSM), _tile(tl, D_CONF), _tile(tl, D_PLE), _tile(tl, D_MODEL),
                  _full((SUBLANES, D_MODEL)), _full((d_mix, D_MODEL), True), _full((D_MODEL, D_MODEL), True),
                  _full((D_PLE, D_MODEL), True)],
        out_specs=[_tile(tl, d_mix), _tile(tl, D_MODEL), _tile(tl, D_MODEL), _tile(tl, D_MODEL), _tile(tl, D_MODEL),
                   _tile(tl, D_MODEL), _full((SUBLANES, D_MODEL)), _full((SUBLANES, LANES))],
        out_shape=[jax.ShapeDtypeStruct((n_tok, d_mix), F32), jax.ShapeDtypeStruct((n_tok, D_MODEL), F32),
                   jax.ShapeDtypeStruct((n_tok, D_MODEL), BF16), jax.ShapeDtypeStruct((n_tok, D_MODEL), BF16),
                   jax.ShapeDtypeStruct((n_tok, D_MODEL), BF16), jax.ShapeDtypeStruct((n_tok, D_MODEL), BF16),
                   jax.ShapeDtypeStruct((SUBLANES, D_MODEL), F32), jax.ShapeDtypeStruct((SUBLANES, LANES), F32)],
        compiler_params=_params(),
    )(x, yssm, yconf, p, tgt, vec, w_out, wpg, wpp)


def _inproj_bwd(dxin, dz, dglu, dcg, ddtr, dr1, x, g, b, w_r, tl):
    n_tok = x.shape[0]

    def body(dxin_ref, dz_ref, dglu_ref, dcg_ref, ddtr_ref, dr1_ref, x_ref, g_ref, b_ref, w_ref,
             dx_ref, dpb_ref, small_ref):
        @pl.when(pl.program_id(0) == 0)
        def _():
            small_ref[...] = jnp.zeros_like(small_ref)

        dh0 = ALPHA * dr1_ref[...]
        for ref, lo, hi in ((dxin_ref, R_XBC, R_Z), (dz_ref, R_Z, R_GLU), (dglu_ref, R_GLU, R_CG), (dcg_ref, R_CG, R_DT),
                            (ddtr_ref, R_DT, D_INR)):
            piece = ref[...].astype(BF16)
            dpb_ref[:, lo:hi] = piece
            dh0 = dh0 + _dot_nt(piece, w_ref[:, lo:hi])
        xhat, rstd = _ln_stats(x_ref[...])
        small_ref[0:1, :] += _colsum(dh0 * xhat)
        small_ref[1:2, :] += _colsum(dh0)
        dx_ref[...] = _ln_bwd(dh0 * g_ref[...], xhat, rstd)

    return pl.pallas_call(
        body, grid=(n_tok // tl,), name="inproj_bwd",
        in_specs=[_tile(tl, D_XBC), _tile(tl, D_SSM), _tile(tl, 2 * D_CONF), _tile(tl, D_CONF), _tile(tl, LANES),
                  _tile(tl, D_MODEL), _tile(tl, D_MODEL), _full((1, D_MODEL)), _full((1, D_MODEL)),
                  _full((D_MODEL, D_INR), True)],
        out_specs=[_tile(tl, D_MODEL), _tile(tl, D_INR), _full((SUBLANES, D_MODEL))],
        out_shape=[jax.ShapeDtypeStruct((n_tok, D_MODEL), F32), jax.ShapeDtypeStruct((n_tok, D_INR), BF16),
                   jax.ShapeDtypeStruct((SUBLANES, D_MODEL), F32)],
        compiler_params=_params(),
    )(dxin, dz, dglu, dcg, ddtr, dr1, x, g, b, w_r)


def _tn_matmul(a, b, name, tn, tl):
    n_tok, m = a.shape
    n = b.shape[1]

    def body(a_ref, b_ref, o_ref):
        @pl.when(pl.program_id(1) == 0)
        def _():
            o_ref[...] = jnp.zeros_like(o_ref)

        o_ref[...] += lax.dot_general(a_ref[...], b_ref[...], TN_DIMS, preferred_element_type=F32)

    return pl.pallas_call(
        body, grid=(n // tn, n_tok // tl), name=name,
        in_specs=[pl.BlockSpec((tl, m), lambda j, l: (l, 0)), pl.BlockSpec((tl, tn), lambda j, l: (l, j))],
        out_specs=pl.BlockSpec((m, tn), lambda j, l: (0, j)),
        out_shape=jax.ShapeDtypeStruct((m, n), F32),
        compiler_params=pltpu.CompilerParams(dimension_semantics=("parallel", "arbitrary"), vmem_limit_bytes=VMEM_LIMIT),
    )(a, b)


def _pad_rows(a, rows):
    return jnp.pad(a, ((0, rows - a.shape[0]), (0, 0)))


def _pad_lanes(a):
    return jnp.pad(a, ((0, 0), (0, LANES - a.shape[1])))


def _local_grads(x, p, tgt, w_r, w_out_b, wpg_b, wpp_b, ssm_cw, conf_cw, sm):
    n_tok = x.shape[0]
    tl = min(256, n_tok)
    row = lambda v: v.reshape(1, -1)
    g_e, b_e = row(sm["ln_emb_g"]), row(sm["ln_emb_b"])
    h0b, xbc_in, z, glu, cgate, dtr = _ln_inproj(x, g_e, b_e, w_r, tl)

    cw4 = _pad_rows(ssm_cw, SUBLANES)
    dtb, alog = _pad_lanes(sm["dt_bias"]), _pad_lanes(sm["a_log"])
    dsk = jnp.repeat(sm["d_skip"], HEAD_DIM, axis=1)
    pre, y, yssm, hprev = _ssd_fwd(xbc_in, z, dtr, cw4, sm["ssm_conv_b"], dtb, alog, dsk, sm["ssm_norm_g"], tl)

    cw31 = _pad_rows(conf_cw, CONF_HALO)
    u0, u1, yconf = _conf_fwd(glu, cgate, sm["b_glu"], cw31, sm["conf_conv_b"], sm["conf_ln_g"], sm["conf_ln_b"], tl)

    vec = jnp.concatenate([g_e, b_e, sm["b_out"], sm["ln1_g"], sm["ln1_b"], sm["ln2_g"], sm["ln2_b"],
                           jnp.zeros((1, D_MODEL), F32)], axis=0)
    dmix, dr1, dr1b, h1b, dgb, dpb, small_t, loss = _tail(
        x, yssm, yconf, p, tgt, vec, w_out_b, wpg_b, wpp_b, tl)

    dglu, dcg, dcw31, dbglu, small_c = _conf_bwd(dmix, D_SSM // D_CONF, u0, u1, glu, cgate, sm["b_glu"], cw31,
                                                  sm["conf_ln_g"], sm["conf_ln_b"], tl)
    dxin, dz, ddtr, dcw4, dcb4, dgn, ddsk, dalog, ddtb = _ssd_bwd(
        dmix, y, z, pre, xbc_in, dtr, hprev, cw4, dtb, alog, dsk, sm["ssm_norm_g"], tl)
    dx, dprojb, small_e = _inproj_bwd(dxin, dz, dglu, dcg, ddtr, dr1, x, g_e, b_e, w_r, tl)

    tlm = min(512, n_tok)
    dw_r = _tn_matmul(h0b, dprojb, "dw_in", 640, tlm)
    dw_out = jnp.concatenate([_tn_matmul(yssm, dr1b, "dw_out_ssm", 512, tlm),
                              _tn_matmul(yconf, dr1b, "dw_out_conf", 512, tlm)], axis=0)
    dwpg = _tn_matmul(h1b, dgb, "dw_ple_gate", 512, tlm)
    dwpp = _tn_matmul(p.astype(BF16), dpb, "dw_ple_proj", 512, tlm)
    grads = dict(
        ln_emb_g=small_e[0], ln_emb_b=small_e[1], w_in=dw_r, ssm_conv_w=dcw4[0:SSM_K], ssm_conv_b=dcb4,
        dt_bias=ddtb[:, 0:N_HEADS], a_log=dalog[:, 0:N_HEADS], d_skip=ddsk[:, 0:N_HEADS], ssm_norm_g=dgn, b_glu=dbglu,
        conf_conv_w=dcw31[0:CONF_K], conf_conv_b=small_c[2:3], conf_ln_g=small_c[0:1], conf_ln_b=small_c[1:2],
        w_out=dw_out, b_out=small_t[0:1], ln1_g=small_t[1:2], ln1_b=small_t[2:3], w_ple_gate=dwpg, w_ple_proj=dwpp,
        ln2_g=small_t[3:4], ln2_b=small_t[4:5])
    return loss[0, 0], dx, grads


N_CHIPS = 4
N_DEV = 8
W_IN_SH = D_IN // N_CHIPS
BIG = (("w_in", D_MODEL, W_IN_SH), ("w_out", (D_SSM + D_CONF) // N_CHIPS, D_MODEL),
       ("w_ple_gate", D_MODEL // N_CHIPS, D_MODEL), ("w_ple_proj", D_PLE, D_MODEL // N_CHIPS))
SEGS = ((R_XBC, 0, D_SSM), (R_XBC + D_SSM, 2048, 256), (R_XBC + D_SSM + 256, 2304, 256), (R_Z, 1024, D_SSM),
        (R_GLU, 2576, 2 * D_CONF), (R_CG, 4624, D_CONF), (R_DT, 2560, N_HEADS))
ROWS_CW4 = 2
ROWS_CW31 = 8
ROWS_CONV = 16
SMALL_ROWS = 56
SMALL_LAYOUT = (("ln_emb_g", 0, 1), ("ln_emb_b", 1, 1), ("ssm_conv_b", 2, 2), ("dt_bias", 4, 1), ("a_log", 5, 1),
                ("d_skip", 6, 1), ("ssm_norm_g", 7, 1), ("b_glu", 8, 2), ("conf_conv_b", 10, 1), ("conf_ln_g", 11, 1),
                ("conf_ln_b", 12, 1), ("b_out", 13, 1), ("ln1_g", 14, 1), ("ln1_b", 15, 1), ("ln2_g", 16, 1), ("ln2_b", 17, 1))
CONV_LAYOUT = (("ssm_conv_w", 18, 6, (SSM_K, D_XBC)), ("conf_conv_w", 24, 31, (CONF_K, D_CONF)))


def _rows_of(v, rows):
    flat = v.reshape(-1)
    return jnp.pad(flat, (0, rows * D_MODEL - flat.shape[0])).reshape(rows, D_MODEL)


def _pack_small(d):
    parts = [_rows_of(d[n], r) for n, _, r in SMALL_LAYOUT]
    for n, _, r, _ in CONV_LAYOUT:
        parts.append(_rows_of(d[n], r) if n in d else jnp.zeros((r, D_MODEL), F32))
    parts.append(jnp.zeros((SMALL_ROWS - 55, D_MODEL), F32))
    return jnp.concatenate(parts, axis=0)


def _unpack_small(a, like):
    return {n: a[r0:r0 + r].reshape(-1)[:like[n].size].reshape(like[n].shape) for n, r0, r in SMALL_LAYOUT}


def _w_r_from_shards(w4):
    parts = []
    for _, o, wd in SEGS:
        lo, hi = o, o + wd
        while lo < hi:
            s = lo // W_IN_SH
            e = min(hi, (s + 1) * W_IN_SH)
            parts.append(w4[s][:, lo - s * W_IN_SH:e - s * W_IN_SH])
            lo = e
    parts.append(jnp.zeros((D_MODEL, LANES - N_HEADS), w4.dtype))
    return jnp.concatenate(parts, axis=1)


def _shards_from_dw_r(dw_r):
    slabs = []
    for s in range(N_CHIPS):
        lo, hi = s * W_IN_SH, (s + 1) * W_IN_SH
        parts = []
        for kcol, o, wd in sorted(SEGS, key=lambda t: t[1]):
            a, b = max(lo, o), min(hi, o + wd)
            if a < b:
                parts.append(dw_r[:, kcol + a - o:kcol + b - o])
        slabs.append(jnp.concatenate(parts, axis=1))
    return jnp.stack(slabs)


def _row_chunks(rows, n):
    return [(j * (rows // n), rows // n) for j in range(n)]


def _my_place():
    return lax.axis_index("x"), lax.axis_index("y"), lax.axis_index("c")


MESH_ID = pl.DeviceIdType.MESH
ANY = pl.BlockSpec(memory_space=pl.ANY)
IN_VMEM = pl.BlockSpec(memory_space=pltpu.VMEM)
CHIP_FLIPS = ((1, 0), (0, 1), (1, 1))


def _remote(src, dst, send_sem, recv_sem, peer):
    return pltpu.make_async_remote_copy(src, dst, send_sem, recv_sem, device_id=peer, device_id_type=MESH_ID)


GATHER_CHUNKS = (4, 2, 1, 1)


def _gather_weights(shards, conv_f):
    n_big = len(BIG)
    plan = [(a, o, n, rows // 2) for a, (_, rows, _) in enumerate(BIG) for o, n in _row_chunks(rows // 2, GATHER_CHUNKS[a])]
    own_plan = [(a, o, n) for a, (_, rows, _) in enumerate(BIG) for o, n in _row_chunks(rows, 2 * GATHER_CHUNKS[a])]
    own_plan.append((n_big, 0, ROWS_CONV))

    def body(*refs):
        ins, outs = refs[0:n_big + 1], refs[n_big + 1:2 * n_big + 2]
        send_a, recv_a, send_b, recv_b, loc = refs[2 * n_big + 2:]
        x, y, c = _my_place()
        s = 2 * x + y
        sibling = (x, y, 1 - c)
        own = [pltpu.make_async_copy(ins[a].at[pl.ds(o, n)], outs[a].at[s, pl.ds(o, n)], loc.at[j])
               for j, (a, o, n) in enumerate(own_plan)]
        first = []
        for k, (fx, fy) in enumerate(CHIP_FLIPS):
            peer = (x ^ fx, y ^ fy, c)
            for j, (a, o, n, h) in enumerate(plan):
                r0 = pl.multiple_of(c * h + o, 16)
                first.append(_remote(ins[a].at[pl.ds(r0, n)], outs[a].at[s, pl.ds(r0, n)], send_a.at[k, j], recv_a.at[k, j], peer))
            first.append(_remote(ins[n_big], outs[n_big].at[s], send_a.at[k, len(plan)], recv_a.at[k, len(plan)], peer))
        for cp in own + first:
            cp.start()
        passed = []
        for k, (fx, fy) in enumerate(CHIP_FLIPS):
            peer = (x ^ fx, y ^ fy, c)
            sk = 2 * (x ^ fx) + (y ^ fy)
            for j, (a, o, n, h) in enumerate(plan):
                land = outs[a].at[sk, pl.ds(pl.multiple_of(c * h + o, 16), n)]
                _remote(land, land, send_a.at[k, j], recv_a.at[k, j], peer).wait_recv()
                cp = _remote(land, land, send_b.at[k, j], recv_b.at[k, j], sibling)
                cp.start()
                passed.append(cp)
            land = outs[n_big].at[sk]
            _remote(land, land, send_a.at[k, len(plan)], recv_a.at[k, len(plan)], peer).wait_recv()
        for k, (fx, fy) in enumerate(CHIP_FLIPS):
            sk = 2 * (x ^ fx) + (y ^ fy)
            for j, (a, o, n, h) in enumerate(plan):
                land = outs[a].at[sk, pl.ds(pl.multiple_of((1 - c) * h + o, 16), n)]
                _remote(land, land, send_b.at[k, j], recv_b.at[k, j], sibling).wait_recv()
        for cp in first + passed:
            cp.wait_send()
        for cp in own:
            cp.wait()

    arrays = list(shards) + [conv_f]
    return pl.pallas_call(
        body, name="gather_weights", in_specs=[IN_VMEM] * len(arrays), out_specs=[ANY] * len(arrays),
        out_shape=[jax.ShapeDtypeStruct((N_CHIPS,) + a.shape, a.dtype) for a in arrays],
        scratch_shapes=[pltpu.SemaphoreType.DMA((3, len(plan) + 1)), pltpu.SemaphoreType.DMA((3, len(plan) + 1)),
                        pltpu.SemaphoreType.DMA((3, len(plan))), pltpu.SemaphoreType.DMA((3, len(plan))),
                        pltpu.SemaphoreType.DMA((len(own_plan),))],
    )(*arrays)


CORES_CHUNKS = (4, 2, 1, 1)
CHIPS_CHUNKS = (2, 1, 1, 1)
SHARE_CHUNKS = (8, 4, 2, 1)


def _exchange_cores(gbig, spack):
    n_big = len(BIG)
    plan = [(a, t, o, n, rows // 2) for a, (_, rows, _) in enumerate(BIG) for t in range(N_CHIPS)
            for o, n in _row_chunks(rows // 2, CORES_CHUNKS[a])]

    def body(*refs):
        g_refs, s_ref = refs[0:n_big], refs[n_big]
        theirs, small_all = refs[n_big + 1:2 * n_big + 1], refs[2 * n_big + 1]
        send_sems, recv_sems, send_small, recv_small, loc_sem = refs[2 * n_big + 2:]
        x, y, c = _my_place()
        me = 4 * x + 2 * y + c
        own = pltpu.make_async_copy(s_ref, small_all.at[me], loc_sem)
        sends = []
        for j, (a, t, o, n, h) in enumerate(plan):
            give = pl.multiple_of((1 - c) * h + o, SUBLANES)
            sends.append(_remote(g_refs[a].at[t, pl.ds(give, n)], theirs[a].at[t, pl.ds(o, n)], send_sems.at[j], recv_sems.at[j],
                                 (x, y, 1 - c)))
        for m in range(1, N_DEV):
            peer = (x ^ (m >> 2), y ^ ((m >> 1) & 1), c ^ (m & 1))
            sends.append(_remote(s_ref, small_all.at[me], send_small.at[m - 1], recv_small.at[m - 1], peer))
        own.start()
        for cp in sends:
            cp.start()
        for cp in sends:
            cp.wait()
        own.wait()

    return pl.pallas_call(
        body, name="exchange_cores", in_specs=[ANY] * n_big + [IN_VMEM], out_specs=[ANY] * (n_big + 1),
        out_shape=[jax.ShapeDtypeStruct((N_CHIPS, rows // 2, cols), F32) for _, rows, cols in BIG]
        + [jax.ShapeDtypeStruct((N_DEV, SMALL_ROWS, D_MODEL), F32)],
        scratch_shapes=[pltpu.SemaphoreType.DMA((len(plan),)), pltpu.SemaphoreType.DMA((len(plan),)),
                        pltpu.SemaphoreType.DMA((N_DEV - 1,)), pltpu.SemaphoreType.DMA((N_DEV - 1,)), pltpu.SemaphoreType.DMA],
    )(*gbig, spack)


def _exchange_chips(psums):
    n_big = len(BIG)
    plan = [(a, o, n) for a, (_, rows, _) in enumerate(BIG) for o, n in _row_chunks(rows // 2, CHIPS_CHUNKS[a])]

    def body(*refs):
        p_refs, got = refs[0:n_big], refs[n_big:2 * n_big]
        send_sems, recv_sems = refs[2 * n_big:]
        x, y, c = _my_place()
        sends = []
        for k, (fx, fy) in enumerate(CHIP_FLIPS):
            tx, ty = x ^ fx, y ^ fy
            for j, (a, o, n) in enumerate(plan):
                sends.append(_remote(p_refs[a].at[2 * tx + ty, pl.ds(o, n)], got[a].at[k, pl.ds(o, n)], send_sems.at[k, j],
                                     recv_sems.at[k, j], (tx, ty, c)))
        for cp in sends:
            cp.start()
        for cp in sends:
            cp.wait()

    return pl.pallas_call(
        body, name="exchange_chips", in_specs=[ANY] * n_big, out_specs=[ANY] * n_big,
        out_shape=[jax.ShapeDtypeStruct((3, rows // 2, cols), BF16) for _, rows, cols in BIG],
        scratch_shapes=[pltpu.SemaphoreType.DMA((3, len(plan))), pltpu.SemaphoreType.DMA((3, len(plan)))],
    )(*psums)


def _share_halves(tots):
    n_big = len(BIG)
    plan = [(a, o, n) for a, (_, rows, _) in enumerate(BIG) for o, n in _row_chunks(rows // 2, SHARE_CHUNKS[a])]

    def body(*refs):
        t_refs, both = refs[0:n_big], refs[n_big:2 * n_big]
        send_sems, recv_sems, loc = refs[2 * n_big:]
        x, y, c = _my_place()
        own = [pltpu.make_async_copy(t_refs[a].at[pl.ds(o, n)], both[a].at[c, pl.ds(o, n)], loc.at[j])
               for j, (a, o, n) in enumerate(plan)]
        sends = [_remote(t_refs[a].at[pl.ds(o, n)], both[a].at[c, pl.ds(o, n)], send_sems.at[j], recv_sems.at[j], (x, y, 1 - c))
                 for j, (a, o, n) in enumerate(plan)]
        for cp in own + sends:
            cp.start()
        for cp in sends:
            cp.wait()
        for cp in own:
            cp.wait()

    return pl.pallas_call(
        body, name="share_halves", in_specs=[IN_VMEM] * n_big, out_specs=[ANY] * n_big,
        out_shape=[jax.ShapeDtypeStruct((2, rows // 2, cols), F32) for _, rows, cols in BIG],
        scratch_shapes=[pltpu.SemaphoreType.DMA((len(plan),)), pltpu.SemaphoreType.DMA((len(plan),)),
                        pltpu.SemaphoreType.DMA((len(plan),))],
    )(*tots)


def _sum_rows(half):
    return half if half <= 256 else 256


def _add_cores(core, g, theirs, name):
    _, rows, cols = g.shape
    half = rows // 2
    tr = _sum_rows(half)
    nb = half // tr

    def body(c_ref, g_ref, t_ref, o_ref):
        o_ref[...] = (g_ref[...] + t_ref[...]).astype(BF16)

    grid_spec = pltpu.PrefetchScalarGridSpec(
        num_scalar_prefetch=1, grid=(N_CHIPS, nb),
        in_specs=[pl.BlockSpec((1, tr, cols), lambda t, i, c_ref: (t, c_ref[0] * nb + i, 0)),
                  pl.BlockSpec((1, tr, cols), lambda t, i, c_ref: (t, i, 0))],
        out_specs=pl.BlockSpec((1, tr, cols), lambda t, i, c_ref: (t, i, 0)))
    return pl.pallas_call(body, grid_spec=grid_spec, name=name, out_shape=jax.ShapeDtypeStruct((N_CHIPS, half, cols), BF16),
                          compiler_params=pltpu.CompilerParams(vmem_limit_bytes=VMEM_LIMIT))(core, g, theirs)


def _add_chips(chip, psum, got, name):
    _, half, cols = psum.shape
    tr = _sum_rows(half)

    def body(s_ref, p_ref, g_ref, o_ref):
        o_ref[...] = ((p_ref[0].astype(F32) + g_ref[0].astype(F32)) + g_ref[1].astype(F32)) + g_ref[2].astype(F32)

    grid_spec = pltpu.PrefetchScalarGridSpec(
        num_scalar_prefetch=1, grid=(half // tr,),
        in_specs=[pl.BlockSpec((1, tr, cols), lambda i, s_ref: (s_ref[0], i, 0)),
                  pl.BlockSpec((3, tr, cols), lambda i, s_ref: (0, i, 0))],
        out_specs=pl.BlockSpec((tr, cols), lambda i, s_ref: (i, 0)))
    return pl.pallas_call(body, grid_spec=grid_spec, name=name, out_shape=jax.ShapeDtypeStruct((half, cols), F32),
                          compiler_params=pltpu.CompilerParams(vmem_limit_bytes=VMEM_LIMIT))(chip, psum, got)


def _adam_math(w, g, m, v):
    m = ADAM_B1 * m + (1.0 - ADAM_B1) * g
    v = ADAM_B2 * v + (1.0 - ADAM_B2) * (g * g)
    m_hat = m / (1.0 - ADAM_B1 ** ADAM_STEP)
    v_hat = v / (1.0 - ADAM_B2 ** ADAM_STEP)
    return -ADAM_LR * (m_hat / (jnp.sqrt(v_hat) + ADAM_EPS) + ADAM_WD * w), m, v


def _adam(w, g, m, v, name):
    rows, cols = w.shape
    tr = rows if rows <= 256 else 256

    def body(w_ref, g_ref, m_ref, v_ref, d_ref, nm_ref, nv_ref):
        d_ref[...], nm_ref[...], nv_ref[...] = _adam_math(w_ref[...], g_ref[...], m_ref[...], v_ref[...])

    spec = pl.BlockSpec((tr, cols), lambda i: (i, 0))
    return pl.pallas_call(body, grid=(rows // tr,), name=name, in_specs=[spec] * 4, out_specs=[spec] * 3,
                          out_shape=[jax.ShapeDtypeStruct(w.shape, F32)] * 3, compiler_params=_params(seq=False))(w, g, m, v)


def _adam_small(w, parts, m, v):
    def body(w_ref, p_ref, m_ref, v_ref, g_ref, d_ref, nm_ref, nv_ref):
        g = p_ref[0]
        for k in range(1, N_DEV):
            g = g + p_ref[k]
        g_ref[...] = g
        d_ref[...], nm_ref[...], nv_ref[...] = _adam_math(w_ref[...], g, m_ref[...], v_ref[...])

    return pl.pallas_call(body, name="adam_small", out_shape=[jax.ShapeDtypeStruct(w.shape, F32)] * 4)(w, parts, m, v)


def kernel(x, p, ln_emb_g, ln_emb_b, w_in, ssm_conv_w, ssm_conv_b, dt_bias, a_log, d_skip, ssm_norm_g, b_glu, conf_conv_w, conf_conv_b, conf_ln_g, conf_ln_b, w_out, b_out, ln1_g, ln1_b, w_ple_gate, w_ple_proj, ln2_g, ln2_b, loss_target, m_ln_emb_g, m_ln_emb_b, m_w_in, m_ssm_conv_w, m_ssm_conv_b, m_dt_bias, m_a_log, m_d_skip, m_ssm_norm_g, m_b_glu, m_conf_conv_w, m_conf_conv_b, m_conf_ln_g, m_conf_ln_b, m_w_out, m_b_out, m_ln1_g, m_ln1_b, m_w_ple_gate, m_w_ple_proj, m_ln2_g, m_ln2_b, v_ln_emb_g, v_ln_emb_b, v_w_in, v_ssm_conv_w, v_ssm_conv_b, v_dt_bias, v_a_log, v_d_skip, v_ssm_norm_g, v_b_glu, v_conf_conv_w, v_conf_conv_b, v_conf_ln_g, v_conf_ln_b, v_w_out, v_b_out, v_ln1_g, v_ln1_b, v_w_ple_gate, v_w_ple_proj, v_ln2_g, v_ln2_b):
    order = ("ln_emb_g", "ln_emb_b", "w_in", "ssm_conv_w", "ssm_conv_b", "dt_bias", "a_log", "d_skip", "ssm_norm_g", "b_glu",
             "conf_conv_w", "conf_conv_b", "conf_ln_g", "conf_ln_b", "w_out", "b_out", "ln1_g", "ln1_b", "w_ple_gate",
             "w_ple_proj", "ln2_g", "ln2_b")
    w = dict(zip(order, (ln_emb_g, ln_emb_b, w_in, ssm_conv_w, ssm_conv_b, dt_bias, a_log, d_skip, ssm_norm_g, b_glu,
                         conf_conv_w, conf_conv_b, conf_ln_g, conf_ln_b, w_out, b_out, ln1_g, ln1_b, w_ple_gate, w_ple_proj,
                         ln2_g, ln2_b)))
    m = dict(zip(order, (m_ln_emb_g, m_ln_emb_b, m_w_in, m_ssm_conv_w, m_ssm_conv_b, m_dt_bias, m_a_log, m_d_skip,
                         m_ssm_norm_g, m_b_glu, m_conf_conv_w, m_conf_conv_b, m_conf_ln_g, m_conf_ln_b, m_w_out, m_b_out,
                         m_ln1_g, m_ln1_b, m_w_ple_gate, m_w_ple_proj, m_ln2_g, m_ln2_b)))
    v = dict(zip(order, (v_ln_emb_g, v_ln_emb_b, v_w_in, v_ssm_conv_w, v_ssm_conv_b, v_dt_bias, v_a_log, v_d_skip,
                         v_ssm_norm_g, v_b_glu, v_conf_conv_w, v_conf_conv_b, v_conf_ln_g, v_conf_ln_b, v_w_out, v_b_out,
                         v_ln1_g, v_ln1_b, v_w_ple_gate, v_w_ple_proj, v_ln2_g, v_ln2_b)))

    conv_f = jnp.concatenate([_rows_of(w["ssm_conv_w"], ROWS_CW4), _rows_of(w["conf_conv_w"], ROWS_CW31),
                              jnp.zeros((ROWS_CONV - ROWS_CW4 - ROWS_CW31, D_MODEL), F32)], axis=0)
    w_in_all, w_out_all, wpg_all, wpp_all, conv_all = _gather_weights([w[n][0].astype(BF16) for n, _, _ in BIG], conv_f)
    w_r = _w_r_from_shards(w_in_all)
    w_out_b = w_out_all.reshape(D_SSM + D_CONF, D_MODEL)
    wpg_b = wpg_all.reshape(D_MODEL, D_MODEL)
    wpp_b = wpp_all.transpose(1, 0, 2).reshape(D_PLE, D_MODEL)
    cw4 = conv_all[:, 0:ROWS_CW4].reshape(N_CHIPS, -1)[:, :SSM_K * D_XBC // N_CHIPS]
    cw4 = cw4.reshape(N_CHIPS, SSM_K, D_XBC // N_CHIPS).transpose(1, 0, 2).reshape(SSM_K, D_XBC)
    cw31 = conv_all[:, ROWS_CW4:ROWS_CW4 + ROWS_CW31].reshape(N_CHIPS, -1)[:, :CONF_K * D_CONF // N_CHIPS]
    cw31 = cw31.reshape(N_CHIPS, CONF_K, D_CONF // N_CHIPS).transpose(1, 0, 2).reshape(CONF_K, D_CONF)

    small_names = [n for n, _, _ in SMALL_LAYOUT]
    sm = {n: w[n] for n in small_names}
    loss_part, dx, grads = _local_grads(x[0], p[0, 0], loss_target[0], w_r, w_out_b, wpg_b, wpp_b, cw4, cw31, sm)
    loss = lax.psum(loss_part, ("x", "y", "c"))

    core = lax.axis_index("c").astype(jnp.int32).reshape(1)
    chip_i = 2 * lax.axis_index("x") + lax.axis_index("y")
    chip = chip_i.astype(jnp.int32).reshape(1)
    gbig = [_shards_from_dw_r(grads["w_in"]), grads["w_out"].reshape(N_CHIPS, -1, D_MODEL),
            grads["w_ple_gate"].reshape(N_CHIPS, -1, D_MODEL),
            grads["w_ple_proj"].reshape(D_PLE, N_CHIPS, D_MODEL // N_CHIPS).transpose(1, 0, 2)]
    spack = _pack_small({n: grads[n] for n in small_names + [n for n, _, _, _ in CONV_LAYOUT]})
    *theirs, small_all = _exchange_cores(gbig, spack)
    psums = [_add_cores(core, g, t, "add_cores_" + n) for g, t, (n, _, _) in zip(gbig, theirs, BIG)]
    got = _exchange_chips(psums)
    tots = [_add_chips(chip, ps, gt, "add_chips_" + n) for ps, gt, (n, _, _) in zip(psums, got, BIG)]
    both = _share_halves(tots)

    out_g, out_d, out_m, out_v = {}, {}, {}, {}
    g_s, d_s, m_s, v_s = _adam_small(_pack_small(sm), small_all, _pack_small({n: m[n] for n in small_names}),
                                     _pack_small({n: v[n] for n in small_names}))
    g_shard = {n: b.reshape(rows, cols) for b, (n, rows, cols) in zip(both, BIG)}
    for n, r0, r, shape in CONV_LAYOUT:
        whole = g_s[r0:r0 + r].reshape(-1)[:shape[0] * shape[1]].reshape(shape)
        g_shard[n] = lax.dynamic_slice_in_dim(whole, chip_i * (shape[1] // N_CHIPS), shape[1] // N_CHIPS, axis=1)
    for n, g in g_shard.items():
        shape = w[n].shape
        d, nm, nv = _adam(w[n][0], g, m[n][0], v[n][0], "adam_" + n)
        out_g[n], out_d[n], out_m[n], out_v[n] = g[None], d[None], nm[None], nv[None]
    for dst, src in ((out_g, g_s), (out_d, d_s), (out_m, m_s), (out_v, v_s)):
        dst.update(_unpack_small(src, sm))
    return (loss, dx[None], *[out_g[n] for n in order], *[out_d[n] for n in order], *[out_m[n] for n in order],
            *[out_v[n] for n in order])
```

```python
import functools

import jax
import jax.numpy as jnp
from jax import lax
from jax.experimental import pallas as pl
from jax.experimental.pallas import tpu as pltpu

F32 = jnp.float32
BF16 = jnp.bfloat16

D_MODEL = 1024
D_PLE = 256
D_SSM = 1024
D_CONF = 1024
N_HEADS = 16
HEAD_DIM = 64
N_GROUPS = 2
N_STATE = 128
CHUNK = 128
SSM_K = 4
CONF_K = 31
D_XBC = D_SSM + 2 * N_GROUPS * N_STATE
D_IN = 5648
R_XBC, R_Z, R_GLU, R_CG, R_DT, D_INR = 0, 1536, 2560, 4608, 5632, 5760
LN_EPS = 1e-5
RMS_EPS = 1e-5
ALPHA = 2.0 ** 0.25
ADAM_LR, ADAM_B1, ADAM_B2, ADAM_EPS, ADAM_WD, ADAM_STEP = 0.001, 0.9, 0.999, 1e-08, 0.01, 10
NEG_BIG = -1e30
LANES = 128
SUBLANES = 8
VMEM_LIMIT = 56 * 1024 * 1024
HIGHEST = lax.Precision.HIGHEST
NT_DIMS = (((1,), (1,)), ((), ()))
TN_DIMS = (((0,), (0,)), ((), ()))


def _sig(v):
    return jax.nn.sigmoid(v)


def _dsilu(v, s):
    return s * (1.0 + v * (1.0 - s))


def _ln_stats(v):
    mu = jnp.mean(v, axis=-1, keepdims=True)
    c = v - mu
    var = jnp.mean(c * c, axis=-1, keepdims=True)
    rstd = lax.rsqrt(var + LN_EPS)
    return c * rstd, rstd


def _ln_bwd(dxhat, xhat, rstd):
    m1 = jnp.mean(dxhat, axis=-1, keepdims=True)
    m2 = jnp.mean(dxhat * xhat, axis=-1, keepdims=True)
    return rstd * (dxhat - m1 - xhat * m2)


def _softplus(v):
    return jnp.maximum(v, 0.0) + jnp.log1p(jnp.exp(-jnp.abs(v)))


def _colsum(v):
    return jnp.sum(v, axis=0, keepdims=True)


def _dot(a, b):
    return jnp.dot(a, b, preferred_element_type=F32)


def _dot_nt(a, b):
    return lax.dot_general(a, b, NT_DIMS, preferred_element_type=F32)


def _tile(tl, c, rev_of=None):
    if rev_of is None:
        return pl.BlockSpec((tl, c), lambda i: (i, 0))
    return pl.BlockSpec((tl, c), lambda i: (rev_of - 1 - i, 0))


def _full(shape, single=False):
    nd = len(shape)
    if single:
        return pl.BlockSpec(shape, lambda i: (0,) * nd, pipeline_mode=pl.Buffered(1))
    return pl.BlockSpec(shape, lambda i: (0,) * nd)


def _params(seq=True):
    return pltpu.CompilerParams(dimension_semantics=("arbitrary",) if seq else ("parallel",), vmem_limit_bytes=VMEM_LIMIT)


def _ln_inproj(x, g, b, w_r, tl):
    n_tok = x.shape[0]

    def body(x_ref, g_ref, b_ref, w_ref, h0b_ref, xbc_ref, z_ref, glu_ref, cg_ref, dtr_ref):
        xhat, _ = _ln_stats(x_ref[...])
        hb = (xhat * g_ref[...] + b_ref[...]).astype(BF16)
        h0b_ref[...] = hb
        xbc_ref[...] = _dot(hb, w_ref[:, R_XBC:R_Z])
        z_ref[...] = _dot(hb, w_ref[:, R_Z:R_GLU])
        glu_ref[...] = _dot(hb, w_ref[:, R_GLU:R_CG])
        cg_ref[...] = _dot(hb, w_ref[:, R_CG:R_DT])
        dtr_ref[...] = _dot(hb, w_ref[:, R_DT:D_INR])

    widths = (D_MODEL, D_XBC, D_SSM, 2 * D_CONF, D_CONF, LANES)
    dtypes = (BF16, F32, F32, F32, F32, F32)
    return pl.pallas_call(
        body, grid=(n_tok // tl,), name="ln_inproj",
        in_specs=[_tile(tl, D_MODEL), _full((1, D_MODEL)), _full((1, D_MODEL)), _full((D_MODEL, D_INR), single=True)],
        out_specs=[_tile(tl, w) for w in widths],
        out_shape=[jax.ShapeDtypeStruct((n_tok, w), dt) for w, dt in zip(widths, dtypes)],
        compiler_params=_params(seq=False),
    )(x, g, b, w_r)


def _chunk_common(adt_c):
    row = lax.broadcasted_iota(jnp.int32, (CHUNK, CHUNK), 0)
    col = lax.broadcasted_iota(jnp.int32, (CHUNK, CHUNK), 1)
    tril = row >= col
    acs = jnp.dot(tril.astype(F32), adt_c, precision=HIGHEST, preferred_element_type=F32)
    last = acs[CHUNK - 1:CHUNK, :]
    return dict(row=row, col=col, tril=tril, lo=col < HEAD_DIM, acs=acs, acs_t=acs.T, e=jnp.exp(acs),
                dec=jnp.exp(last - acs), cd=jnp.exp(last))


def _pick(cm, v, ha):
    return jnp.where(cm["lo"], v[:, ha:ha + 1], v[:, ha + 1:ha + 2])


def _decay_mask(cm, h):
    return jnp.exp(jnp.where(cm["tril"], cm["acs"][:, h:h + 1] - cm["acs_t"][h:h + 1, :], NEG_BIG))


SSM_FWD_SHIFTS = (5, 6, 7)
SSM_BWD_SHIFTS = (1, 2, 3)


def _ssd_fwd(xbc_in, z, dtr, cw, cb, dtb, alog, dsk, gnorm, tl):
    n_tok = xbc_in.shape[0]
    nq = tl // CHUNK

    def body(xin_ref, z_ref, dtr_ref, cw_ref, cb_ref, dtb_ref, alog_ref, dsk_ref, gn_ref,
             pre_ref, y_ref, yssm_ref, hprev_ref, buf, hst, shifts):
        @pl.when(pl.program_id(0) == 0)
        def _():
            buf[0:SUBLANES, :] = jnp.zeros((SUBLANES, D_XBC), F32)
            hst[...] = jnp.zeros_like(hst)

        buf[SUBLANES:SUBLANES + tl, :] = xin_ref[...]
        _shift_copies(buf, shifts, tl, SSM_FWD_SHIFTS)

        def strip(rb, carry):
            i0 = pl.multiple_of(rb * CONV_RS, CONV_RS)
            for c0 in range(0, D_XBC, CONV_CS):
                acc = jnp.broadcast_to(cb_ref[:, c0:c0 + CONV_CS], (CONV_RS, CONV_CS))
                for k in range(SSM_K):
                    acc = acc + _tap_rows(buf, shifts, SUBLANES - (SSM_K - 1) + k, i0, c0, SSM_FWD_SHIFTS) \
                        * cw_ref[k:k + 1, c0:c0 + CONV_CS]
                pre_ref[pl.ds(i0, CONV_RS), c0:c0 + CONV_CS] = acc
            return carry

        lax.fori_loop(0, tl // CONV_RS, strip, 0)
        buf[0:SUBLANES, :] = buf[tl:tl + SUBLANES, :]
        pre = pre_ref[...]
        xbc = pre * _sig(pre)
        dt = _softplus(dtr_ref[...] + dtb_ref[...])
        a = -jnp.exp(alog_ref[...])
        adt = dt * a
        for q in range(nq):
            r0 = q * CHUNK
            cm = _chunk_common(adt[r0:r0 + CHUNK, :])
            dt_c = dt[r0:r0 + CHUNK, :]
            for g in range(N_GROUPS):
                bg = xbc[r0:r0 + CHUNK, D_SSM + g * N_STATE:D_SSM + (g + 1) * N_STATE].astype(BF16)
                cg_ = xbc[r0:r0 + CHUNK, D_SSM + (N_GROUPS + g) * N_STATE:D_SSM + (N_GROUPS + g + 1) * N_STATE].astype(BF16)
                gm = _dot_nt(cg_, bg)
                for k in range(N_HEADS // N_GROUPS // 2):
                    ha = (N_HEADS // N_GROUPS) * g + 2 * k
                    c0 = ha * HEAD_DIM
                    xh2 = xbc[r0:r0 + CHUNK, c0:c0 + LANES]
                    x2 = xh2 * _pick(cm, dt_c, ha)
                    x2b = x2.astype(BF16)
                    ya = _dot((gm * _decay_mask(cm, ha)).astype(BF16), x2b)
                    yb = _dot((gm * _decay_mask(cm, ha + 1)).astype(BF16), x2b)
                    h2 = hst[c0:c0 + LANES, :]
                    hprev_ref[q, c0:c0 + LANES, :] = h2
                    z2 = _dot_nt(cg_, h2.astype(BF16))
                    y2 = jnp.where(cm["lo"], ya, yb) + z2 * _pick(cm, cm["e"], ha) + dsk_ref[:, c0:c0 + LANES] * xh2
                    y_ref[r0:r0 + CHUNK, c0:c0 + LANES] = y2
                    s2 = _dot((x2 * _pick(cm, cm["dec"], ha)).T.astype(BF16), bg)
                    cd2 = jnp.where(cm["row"] < HEAD_DIM, cm["cd"][:, ha:ha + 1], cm["cd"][:, ha + 1:ha + 2])
                    hst[c0:c0 + LANES, :] = cd2 * h2 + s2
        yv = y_ref[...]
        zv = z_ref[...]
        yz = yv * (zv * _sig(zv))
        gw = D_SSM // N_GROUPS
        for g in range(N_GROUPS):
            seg = yz[:, g * gw:(g + 1) * gw]
            r = lax.rsqrt(jnp.mean(seg * seg, axis=-1, keepdims=True) + RMS_EPS)
            yssm_ref[:, g * gw:(g + 1) * gw] = (seg * r * gn_ref[:, g * gw:(g + 1) * gw]).astype(BF16)

    return pl.pallas_call(
        body, grid=(n_tok // tl,), name="ssd_fwd",
        in_specs=[_tile(tl, D_XBC), _tile(tl, D_SSM), _tile(tl, LANES), _full((SUBLANES, D_XBC)), _full((1, D_XBC)),
                  _full((1, LANES)), _full((1, LANES)), _full((1, D_SSM)), _full((1, D_SSM))],
        out_specs=[_tile(tl, D_XBC), _tile(tl, D_SSM), _tile(tl, D_SSM),
                   pl.BlockSpec((nq, D_SSM, N_STATE), lambda i: (i, 0, 0))],
        out_shape=[jax.ShapeDtypeStruct((n_tok, D_XBC), F32), jax.ShapeDtypeStruct((n_tok, D_SSM), F32),
                   jax.ShapeDtypeStruct((n_tok, D_SSM), BF16), jax.ShapeDtypeStruct((n_tok // CHUNK, D_SSM, N_STATE), F32)],
        scratch_shapes=[pltpu.VMEM((tl + SUBLANES, D_XBC), F32), pltpu.VMEM((D_SSM, N_STATE), F32),
                        pltpu.VMEM((len(SSM_FWD_SHIFTS), tl, D_XBC), F32)],
        compiler_params=_params(),
    )(xbc_in, z, dtr, cw, cb, dtb, alog, dsk, gnorm)


def _ssd_bwd(dys, y, z, pre, xbc_in, dtr, hprev, cw, dtb, alog, dsk, gnorm, tl):
    n_tok = y.shape[0]
    n_t = n_tok // tl
    nq = tl // CHUNK

    def body(dys_ref, y_ref, z_ref, pre_ref, xin_ref, dtr_ref, hprev_ref, cw_ref, dtb_ref, alog_ref, dsk_ref, gn_ref,
             dxin_ref, dz_ref, ddtr_ref, dcw_ref, dcb_ref, dgn_ref, ddsk_ref, da_ref, ddtb_ref,
             dxs, dh, cs_s, dskc, shifts, dwp):
        i = pl.program_id(0)

        @pl.when(i == 0)
        def _():
            dcw_ref[...] = jnp.zeros_like(dcw_ref)
            dwp[...] = jnp.zeros_like(dwp)
            dcb_ref[...] = jnp.zeros_like(dcb_ref)
            dgn_ref[...] = jnp.zeros_like(dgn_ref)
            da_ref[...] = jnp.zeros_like(da_ref)
            ddtb_ref[...] = jnp.zeros_like(ddtb_ref)
            dskc[...] = jnp.zeros_like(dskc)
            dh[...] = jnp.zeros_like(dh)
            dxs[tl:tl + SUBLANES, :] = jnp.zeros((SUBLANES, D_XBC), F32)

        yv = y_ref[...]
        zv = z_ref[...]
        dysv = dys_ref[...]
        sz = _sig(zv)
        silz = zv * sz
        yz = yv * silz
        gw = D_SSM // N_GROUPS
        dyz_parts = []
        for g in range(N_GROUPS):
            sl = slice(g * gw, (g + 1) * gw)
            seg = yz[:, sl]
            r = lax.rsqrt(jnp.mean(seg * seg, axis=-1, keepdims=True) + RMS_EPS)
            yzn = seg * r
            dgn_ref[:, sl] += _colsum(dysv[:, sl] * yzn)
            dyzn = dysv[:, sl] * gn_ref[:, sl]
            dyz_parts.append(r * (dyzn - yzn * jnp.mean(dyzn * yzn, axis=-1, keepdims=True)))
        dyz = jnp.concatenate(dyz_parts, axis=1)
        dy = dyz * silz
        dz_ref[...] = dyz * yv * _dsilu(zv, sz)

        prev = pre_ref[...]
        sp = _sig(prev)
        xbc = prev * sp
        dskc[...] += _colsum(dy * xbc[:, 0:D_SSM])
        dt_in = dtr_ref[...] + dtb_ref[...]
        dt = _softplus(dt_in)
        dsp = _sig(dt_in)
        a = -jnp.exp(alog_ref[...])
        adt = dt * a
        for q in reversed(range(nq)):
            r0 = q * CHUNK
            cm = _chunk_common(adt[r0:r0 + CHUNK, :])
            row, col, lo = cm["row"], cm["col"], cm["lo"]
            triu = (col >= row).astype(F32)
            dt_c = dt[r0:r0 + CHUNK, :]
            dacs = jnp.zeros((CHUNK, LANES), F32)
            ddtx = jnp.zeros((CHUNK, LANES), F32)
            cs_s[...] = jnp.zeros_like(cs_s)
            for g in range(N_GROUPS):
                bcol = D_SSM + g * N_STATE
                ccol = D_SSM + (N_GROUPS + g) * N_STATE
                bg = xbc[r0:r0 + CHUNK, bcol:bcol + N_STATE].astype(BF16)
                cg_ = xbc[r0:r0 + CHUNK, ccol:ccol + N_STATE].astype(BF16)
                gm = _dot_nt(cg_, bg)
                dgm = jnp.zeros((CHUNK, CHUNK), F32)
                dbg = jnp.zeros((CHUNK, N_STATE), F32)
                dcg = jnp.zeros((CHUNK, N_STATE), F32)
                for k in range(N_HEADS // N_GROUPS // 2):
                    ha = (N_HEADS // N_GROUPS) * g + 2 * k
                    hb = ha + 1
                    c0 = ha * HEAD_DIM
                    xh2 = xbc[r0:r0 + CHUNK, c0:c0 + LANES]
                    dt2 = _pick(cm, dt_c, ha)
                    x2 = xh2 * dt2
                    x2b = x2.astype(BF16)
                    la = _decay_mask(cm, ha)
                    lb = _decay_mask(cm, hb)
                    ma = gm * la
                    mb = gm * lb
                    dy2 = dy[r0:r0 + CHUNK, c0:c0 + LANES]
                    dy2b = dy2.astype(BF16)
                    dma = _dot_nt(jnp.where(lo, dy2, 0.0).astype(BF16), x2b)
                    dmb = _dot_nt(jnp.where(lo, 0.0, dy2).astype(BF16), x2b)
                    dx2 = jnp.where(lo, _dot(ma.T.astype(BF16), dy2b), _dot(mb.T.astype(BF16), dy2b))
                    qa = dma * ma
                    qb = dmb * mb
                    dgm = dgm + dma * la + dmb * lb
                    cs_s[ha:ha + 1, :] = _colsum(qa)
                    cs_s[hb:hb + 1, :] = _colsum(qb)
                    col_a = jnp.sum(qa, axis=1, keepdims=True)
                    col_b = jnp.sum(qb, axis=1, keepdims=True)
                    h2 = hprev_ref[q, c0:c0 + LANES, :]
                    h2b = h2.astype(BF16)
                    e2 = _pick(cm, cm["e"], ha)
                    dz2 = dy2 * e2
                    dcg = dcg + _dot(dz2.astype(BF16), h2b)
                    t_e = dz2 * _dot_nt(cg_, h2b)
                    col_a = col_a + jnp.sum(jnp.where(lo, t_e, 0.0), axis=1, keepdims=True)
                    col_b = col_b + jnp.sum(jnp.where(lo, 0.0, t_e), axis=1, keepdims=True)
                    dhn = dh[c0:c0 + LANES, :]
                    dhnb = dhn.astype(BF16)
                    cd_a = cm["cd"][:, ha:ha + 1]
                    cd_b = cm["cd"][:, hb:hb + 1]
                    top = row < HEAD_DIM
                    hh = dhn * h2
                    dcd_a = jnp.sum(_colsum(jnp.where(top, hh, 0.0)), axis=1, keepdims=True)
                    dcd_b = jnp.sum(_colsum(jnp.where(top, 0.0, hh)), axis=1, keepdims=True)
                    dh[c0:c0 + LANES, :] = jnp.where(top, cd_a, cd_b) * dhn + _dot(dz2.T.astype(BF16), cg_)
                    w2 = _dot_nt(bg, dhnb)
                    dec2 = _pick(cm, cm["dec"], ha)
                    dx2 = dx2 + dec2 * w2
                    xw = x2 * w2
                    dd_a = jnp.sum(jnp.where(lo, xw, 0.0), axis=1, keepdims=True) * cm["dec"][:, ha:ha + 1]
                    dd_b = jnp.sum(jnp.where(lo, 0.0, xw), axis=1, keepdims=True) * cm["dec"][:, hb:hb + 1]
                    dbg = dbg + _dot((x2 * dec2).astype(BF16), dhnb)
                    is_last = row[:, 0:1] == CHUNK - 1
                    col_a = col_a - dd_a + jnp.where(is_last, dcd_a * cd_a + _colsum(dd_a), 0.0)
                    col_b = col_b - dd_b + jnp.where(is_last, dcd_b * cd_b + _colsum(dd_b), 0.0)
                    dacs = dacs + jnp.where(col == ha, col_a, 0.0) + jnp.where(col == hb, col_b, 0.0)
                    xhd = dx2 * xh2
                    ddtx = ddtx + jnp.where(col == ha, jnp.sum(jnp.where(lo, xhd, 0.0), axis=1, keepdims=True), 0.0) \
                        + jnp.where(col == hb, jnp.sum(jnp.where(lo, 0.0, xhd), axis=1, keepdims=True), 0.0)
                    dxs[r0:r0 + CHUNK, c0:c0 + LANES] = dx2 * dt2 + dsk_ref[:, c0:c0 + LANES] * dy2
                dgmb = dgm.astype(BF16)
                dxs[r0:r0 + CHUNK, bcol:bcol + N_STATE] = dbg + _dot(dgm.T.astype(BF16), cg_)
                dxs[r0:r0 + CHUNK, ccol:ccol + N_STATE] = dcg + _dot(dgmb, bg)
            dacs = dacs - cs_s[...].T
            dadt = jnp.dot(triu, dacs, precision=HIGHEST, preferred_element_type=F32)
            da_ref[...] += _colsum(dadt * dt_c)
            ddtr_c = (dadt * a + ddtx) * dsp[r0:r0 + CHUNK, :]
            ddtr_ref[r0:r0 + CHUNK, :] = ddtr_c
            ddtb_ref[...] += _colsum(ddtr_c)

        dpre = dxs[0:tl, :] * _dsilu(prev, sp)
        dxs[0:tl, :] = dpre
        dcb_ref[...] += _colsum(dpre)
        _shift_copies(dxs, shifts, tl, SSM_BWD_SHIFTS)

        def strip(rb, carry):
            i0 = pl.multiple_of(rb * CONV_RS, CONV_RS)
            for c0 in range(0, D_XBC, CONV_CS):
                xin_s = xin_ref[pl.ds(i0, CONV_RS), c0:c0 + CONV_CS]
                acc = jnp.zeros((CONV_RS, CONV_CS), F32)
                for k in range(SSM_K):
                    sh = _tap_rows(dxs, shifts, SSM_K - 1 - k, i0, c0, SSM_BWD_SHIFTS)
                    acc = acc + sh * cw_ref[k:k + 1, c0:c0 + CONV_CS]
                    t = xin_s * sh
                    part = t[0:SUBLANES]
                    for j in range(1, CONV_RS // SUBLANES):
                        part = part + t[j * SUBLANES:(j + 1) * SUBLANES]
                    dwp[k * SUBLANES:(k + 1) * SUBLANES, c0:c0 + CONV_CS] += part
                dxin_ref[pl.ds(i0, CONV_RS), c0:c0 + CONV_CS] = acc
            return carry

        lax.fori_loop(0, tl // CONV_RS, strip, 0)
        dxs[tl:tl + SUBLANES, :] = dxs[0:SUBLANES, :]

        @pl.when(i == n_t - 1)
        def _():
            for k in range(SSM_K):
                dcw_ref[k:k + 1, :] = _colsum(dwp[k * SUBLANES:(k + 1) * SUBLANES, :])
            da_ref[...] = da_ref[...] * a
            sel = (lax.broadcasted_iota(jnp.int32, (D_SSM, LANES), 0) // HEAD_DIM
                   == lax.broadcasted_iota(jnp.int32, (D_SSM, LANES), 1)).astype(F32)
            rows = jnp.broadcast_to(dskc[...], (SUBLANES, D_SSM))
            ddsk_ref[...] = jnp.dot(rows, sel, precision=HIGHEST, preferred_element_type=F32)[0:1, :]

    rev = functools.partial(_tile, tl, rev_of=n_t)
    return pl.pallas_call(
        body, grid=(n_t,), name="ssd_bwd",
        in_specs=[rev(D_SSM), rev(D_SSM), rev(D_SSM), rev(D_XBC), rev(D_XBC), rev(LANES),
                  pl.BlockSpec((nq, D_SSM, N_STATE), lambda i: (n_t - 1 - i, 0, 0)),
                  _full((SUBLANES, D_XBC)), _full((1, LANES)), _full((1, LANES)), _full((1, D_SSM)), _full((1, D_SSM))],
        out_specs=[rev(D_XBC), rev(D_SSM), rev(LANES), _full((SUBLANES, D_XBC)), _full((1, D_XBC)), _full((1, D_SSM)),
                   _full((1, LANES)), _full((1, LANES)), _full((1, LANES))],
        out_shape=[jax.ShapeDtypeStruct((n_tok, D_XBC), F32), jax.ShapeDtypeStruct((n_tok, D_SSM), F32),
                   jax.ShapeDtypeStruct((n_tok, LANES), F32), jax.ShapeDtypeStruct((SUBLANES, D_XBC), F32),
                   jax.ShapeDtypeStruct((1, D_XBC), F32), jax.ShapeDtypeStruct((1, D_SSM), F32),
                   jax.ShapeDtypeStruct((1, LANES), F32), jax.ShapeDtypeStruct((1, LANES), F32),
                   jax.ShapeDtypeStruct((1, LANES), F32)],
        scratch_shapes=[pltpu.VMEM((tl + SUBLANES, D_XBC), F32), pltpu.VMEM((D_SSM, N_STATE), F32),
                        pltpu.VMEM((CHUNK, LANES), F32), pltpu.VMEM((1, D_SSM), F32),
                        pltpu.VMEM((len(SSM_BWD_SHIFTS), tl, D_XBC), F32), pltpu.VMEM((SSM_K * SUBLANES, D_XBC), F32)],
        compiler_params=_params(),
    )(dys, y, z, pre, xbc_in, dtr, hprev, cw, dtb, alog, dsk, gnorm)


CONF_HALO = 32
CONV_RS = 32
CONV_CS = 256


ALL_SHIFTS = tuple(range(1, SUBLANES))


def _shift_copies(buf, shifts, n_rows, residues=ALL_SHIFTS):
    for j, r in enumerate(residues):
        shifts[j, 0:n_rows, :] = buf[r:r + n_rows, :]


def _tap_rows(buf, shifts, off, i0, c0, residues=ALL_SHIFTS):
    q, r = divmod(off, SUBLANES)
    rows = pl.ds(pl.multiple_of(i0 + SUBLANES * q, SUBLANES), CONV_RS)
    if r == 0:
        return buf[rows, c0:c0 + CONV_CS]
    return shifts[residues.index(r), rows, c0:c0 + CONV_CS]


def _conf_fwd(glu, cgate, bglu, cw, cb, lg, lb, tl):
    n_tok = glu.shape[0]

    def body(glu_ref, cg_ref, bglu_ref, cw_ref, cb_ref, lg_ref, lb_ref, u0_ref, u1_ref, yc_ref, buf, shifts):
        @pl.when(pl.program_id(0) == 0)
        def _():
            buf[0:CONF_HALO, :] = jnp.zeros((CONF_HALO, D_CONF), F32)

        gl = glu_ref[...] + bglu_ref[...]
        u0 = gl[:, 0:D_CONF] * _sig(gl[:, D_CONF:2 * D_CONF])
        u0_ref[...] = u0
        buf[CONF_HALO:CONF_HALO + tl, :] = u0
        _shift_copies(buf, shifts, tl + CONF_HALO - SUBLANES)

        def strip(rb, carry):
            i0 = pl.multiple_of(rb * CONV_RS, CONV_RS)
            for c0 in range(0, D_CONF, CONV_CS):
                acc = jnp.broadcast_to(cb_ref[:, c0:c0 + CONV_CS], (CONV_RS, CONV_CS))
                for k in range(CONF_K):
                    acc = acc + _tap_rows(buf, shifts, CONF_HALO - (CONF_K - 1) + k, i0, c0) * cw_ref[k:k + 1, c0:c0 + CONV_CS]
                u1_ref[pl.ds(i0, CONV_RS), c0:c0 + CONV_CS] = acc
            return carry

        lax.fori_loop(0, tl // CONV_RS, strip, 0)
        buf[0:CONF_HALO, :] = buf[tl:tl + CONF_HALO, :]
        u1 = u1_ref[...]
        xhat, _ = _ln_stats(u1)
        n = xhat * lg_ref[...] + lb_ref[...]
        cgv = cg_ref[...]
        yc_ref[...] = (n * _sig(n) * (cgv * _sig(cgv))).astype(BF16)

    return pl.pallas_call(
        body, grid=(n_tok // tl,), name="conf_fwd",
        in_specs=[_tile(tl, 2 * D_CONF), _tile(tl, D_CONF), _full((1, 2 * D_CONF)), _full((CONF_HALO, D_CONF)),
                  _full((1, D_CONF)), _full((1, D_CONF)), _full((1, D_CONF))],
        out_specs=[_tile(tl, D_CONF)] * 3,
        out_shape=[jax.ShapeDtypeStruct((n_tok, D_CONF), F32), jax.ShapeDtypeStruct((n_tok, D_CONF), F32),
                   jax.ShapeDtypeStruct((n_tok, D_CONF), BF16)],
        scratch_shapes=[pltpu.VMEM((tl + CONF_HALO, D_CONF), F32),
                        pltpu.VMEM((SUBLANES - 1, tl + CONF_HALO - SUBLANES, D_CONF), F32)],
        compiler_params=_params(),
    )(glu, cgate, bglu, cw, cb, lg, lb)


def _conf_bwd(dyc, dyc_block, u0, u1, glu, cgate, bglu, cw, lg, lb, tl):
    n_tok = glu.shape[0]
    n_t = n_tok // tl

    def body(dyc_ref, u0_ref, u1_ref, glu_ref, cg_ref, bglu_ref, cw_ref, lg_ref, lb_ref,
             dglu_ref, dcg_ref, dcw_ref, dbglu_ref, small_ref, buf, shifts, du0_s, dwp):
        @pl.when(pl.program_id(0) == 0)
        def _():
            dwp[...] = jnp.zeros_like(dwp)
            dbglu_ref[...] = jnp.zeros_like(dbglu_ref)
            small_ref[...] = jnp.zeros_like(small_ref)
            buf[tl:tl + CONF_HALO, :] = jnp.zeros((CONF_HALO, D_CONF), F32)

        xhat, rstd = _ln_stats(u1_ref[...])
        n = xhat * lg_ref[...] + lb_ref[...]
        sn = _sig(n)
        cgv = cg_ref[...]
        scg = _sig(cgv)
        dycv = dyc_ref[...]
        dcg_ref[...] = dycv * (n * sn) * _dsilu(cgv, scg)
        dn = dycv * (cgv * scg) * _dsilu(n, sn)
        small_ref[0:1, :] += _colsum(dn * xhat)
        small_ref[1:2, :] += _colsum(dn)
        du1 = _ln_bwd(dn * lg_ref[...], xhat, rstd)
        small_ref[2:3, :] += _colsum(du1)
        buf[0:tl, :] = du1
        _shift_copies(buf, shifts, tl + CONF_HALO - SUBLANES)

        def strip(rb, carry):
            i0 = pl.multiple_of(rb * CONV_RS, CONV_RS)
            for c0 in range(0, D_CONF, CONV_CS):
                u0s = u0_ref[pl.ds(i0, CONV_RS), c0:c0 + CONV_CS]
                acc = jnp.zeros((CONV_RS, CONV_CS), F32)
                for k in range(CONF_K):
                    sh = _tap_rows(buf, shifts, CONF_K - 1 - k, i0, c0)
                    acc = acc + sh * cw_ref[k:k + 1, c0:c0 + CONV_CS]
                    t = u0s * sh
                    part = t[0:SUBLANES]
                    for j in range(1, CONV_RS // SUBLANES):
                        part = part + t[j * SUBLANES:(j + 1) * SUBLANES]
                    dwp[k * SUBLANES:(k + 1) * SUBLANES, c0:c0 + CONV_CS] += part
                du0_s[pl.ds(i0, CONV_RS), c0:c0 + CONV_CS] = acc
            return carry

        lax.fori_loop(0, tl // CONV_RS, strip, 0)
        du0 = du0_s[...]
        buf[tl:tl + CONF_HALO, :] = buf[0:CONF_HALO, :]
        gl = glu_ref[...] + bglu_ref[...]
        sg = _sig(gl[:, D_CONF:2 * D_CONF])
        dgv = du0 * sg
        dgg = du0 * gl[:, 0:D_CONF] * sg * (1.0 - sg)
        dglu_ref[:, 0:D_CONF] = dgv
        dglu_ref[:, D_CONF:2 * D_CONF] = dgg
        dbglu_ref[:, 0:D_CONF] += _colsum(dgv)
        dbglu_ref[:, D_CONF:2 * D_CONF] += _colsum(dgg)

        @pl.when(pl.program_id(0) == n_t - 1)
        def _():
            for k in range(CONF_HALO):
                dcw_ref[k:k + 1, :] = _colsum(dwp[k * SUBLANES:(k + 1) * SUBLANES, :])

    rev = functools.partial(_tile, tl, rev_of=n_t)
    return pl.pallas_call(
        body, grid=(n_t,), name="conf_bwd",
        in_specs=[pl.BlockSpec((tl, D_CONF), lambda i: (n_t - 1 - i, dyc_block)),
                  rev(D_CONF), rev(D_CONF), rev(2 * D_CONF), rev(D_CONF), _full((1, 2 * D_CONF)),
                  _full((CONF_HALO, D_CONF)), _full((1, D_CONF)), _full((1, D_CONF))],
        out_specs=[rev(2 * D_CONF), rev(D_CONF), _full((CONF_HALO, D_CONF)), _full((1, 2 * D_CONF)), _full((SUBLANES, D_CONF))],
        out_shape=[jax.ShapeDtypeStruct((n_tok, 2 * D_CONF), F32), jax.ShapeDtypeStruct((n_tok, D_CONF), F32),
                   jax.ShapeDtypeStruct((CONF_HALO, D_CONF), F32), jax.ShapeDtypeStruct((1, 2 * D_CONF), F32),
                   jax.ShapeDtypeStruct((SUBLANES, D_CONF), F32)],
        scratch_shapes=[pltpu.VMEM((tl + CONF_HALO, D_CONF), F32),
                        pltpu.VMEM((SUBLANES - 1, tl + CONF_HALO - SUBLANES, D_CONF), F32),
                        pltpu.VMEM((tl, D_CONF), F32), pltpu.VMEM((CONF_HALO * SUBLANES, D_CONF), F32)],
        compiler_params=_params(),
    )(dyc, u0, u1, glu, cgate, bglu, cw, lg, lb)


def _tail(x, yssm, yconf, p, tgt, vec, w_out, wpg, wpp, tl):
    n_tok = x.shape[0]

    def body(x_ref, ys_ref, yc_ref, p_ref, t_ref, vec_ref, wo_ref, wg_ref, wp_ref,
             dmix_ref, dr1_ref, dr1b_ref, h1b_ref, dgb_ref, dpb_ref, small_ref, loss_ref):
        @pl.when(pl.program_id(0) == 0)
        def _():
            small_ref[...] = jnp.zeros_like(small_ref)
            loss_ref[...] = jnp.zeros_like(loss_ref)

        xh0, _ = _ln_stats(x_ref[...])
        h0 = xh0 * vec_ref[0:1, :] + vec_ref[1:2, :]
        out = _dot(ys_ref[...], wo_ref[0:D_SSM, :]) + _dot(yc_ref[...], wo_ref[D_SSM:D_SSM + D_CONF, :]) + vec_ref[2:3, :]
        xh1, rstd1 = _ln_stats(ALPHA * h0 + out)
        h1 = xh1 * vec_ref[3:4, :] + vec_ref[4:5, :]
        h1b = h1.astype(BF16)
        h1b_ref[...] = h1b
        gate = _sig(_dot(h1b, wg_ref[...]))
        ple = _dot(p_ref[...].astype(BF16), wp_ref[...])
        xh2, rstd2 = _ln_stats(ALPHA * h1 + gate * ple)
        h2 = xh2 * vec_ref[5:6, :] + vec_ref[6:7, :]
        diff = h2 - t_ref[...]
        part = jnp.sum(jnp.sum(diff * diff, axis=1, keepdims=True), axis=0, keepdims=True) * (0.5 / D_MODEL)
        loss_ref[...] += jnp.broadcast_to(part, loss_ref.shape)
        dh2 = diff * (1.0 / D_MODEL)
        small_ref[3:4, :] += _colsum(dh2 * xh2)
        small_ref[4:5, :] += _colsum(dh2)
        dr2 = _ln_bwd(dh2 * vec_ref[5:6, :], xh2, rstd2)
        dgpre = (dr2 * ple * gate * (1.0 - gate)).astype(BF16)
        dgb_ref[...] = dgpre
        dpb_ref[...] = (dr2 * gate).astype(BF16)
        dh1 = ALPHA * dr2 + _dot_nt(dgpre, wg_ref[...])
        small_ref[1:2, :] += _colsum(dh1 * xh1)
        small_ref[2:3, :] += _colsum(dh1)
        dr1 = _ln_bwd(dh1 * vec_ref[3:4, :], xh1, rstd1)
        small_ref[0:1, :] += _colsum(dr1)
        dr1_ref[...] = dr1
        dr1b = dr1.astype(BF16)
        dr1b_ref[...] = dr1b
        dmix_ref[...] = _dot_nt(dr1b, wo_ref[...])

    d_mix = D_SSM + D_CONF
    return pl.pallas_call(
        body, grid=(n_tok // tl,), name="tail",
        in_specs=[_tile(tl, D_MODEL), _tile(tl, D_SSM), _tile(tl, D_CONF), _tile(tl, D_PLE), _tile(tl, D_MODEL),
                  _full((SUBLANES, D_MODEL)), _full((d_mix, D_MODEL), True), _full((D_MODEL, D_MODEL), True),
                  _full((D_PLE, D_MODEL), True)],
        out_specs=[_tile(tl, d_mix), _tile(tl, D_MODEL), _tile(tl, D_MODEL), _tile(tl, D_MODEL), _tile(tl, D_MODEL),
                   _tile(tl, D_MODEL), _full((SUBLANES, D_MODEL)), _full((SUBLANES, LANES))],
        out_shape=[jax.ShapeDtypeStruct((n_tok, d_mix), F32), jax.ShapeDtypeStruct((n_tok, D_MODEL), F32),
                   jax.ShapeDtypeStruct((n_tok, D_MODEL), BF16), jax.ShapeDtypeStruct((n_tok, D_MODEL), BF16),
                   jax.ShapeDtypeStruct((n_tok, D_MODEL), BF16), jax.ShapeDtypeStruct((n_tok, D_MODEL), BF16),
                   jax.ShapeDtypeStruct((SUBLANES, D_MODEL), F32), jax.ShapeDtypeStruct((SUBLANES, LANES), F32)],
        compiler_params=_params(),
    )(x, yssm, yconf, p, tgt, vec, w_out, wpg, wpp)


def _inproj_bwd(dxin, dz, dglu, dcg, ddtr, dr1, x, g, b, w_r, tl):
    n_tok = x.shape[0]

    def body(dxin_ref, dz_ref, dglu_ref, dcg_ref, ddtr_ref, dr1_ref, x_ref, g_ref, b_ref, w_ref,
             dx_ref, dpb_ref, small_ref):
        @pl.when(pl.program_id(0) == 0)
        def _():
            small_ref[...] = jnp.zeros_like(small_ref)

        dh0 = ALPHA * dr1_ref[...]
        for ref, lo, hi in ((dxin_ref, R_XBC, R_Z), (dz_ref, R_Z, R_GLU), (dglu_ref, R_GLU, R_CG), (dcg_ref, R_CG, R_DT),
                            (ddtr_ref, R_DT, D_INR)):
            piece = ref[...].astype(BF16)
            dpb_ref[:, lo:hi] = piece
            dh0 = dh0 + _dot_nt(piece, w_ref[:, lo:hi])
        xhat, rstd = _ln_stats(x_ref[...])
        small_ref[0:1, :] += _colsum(dh0 * xhat)
        small_ref[1:2, :] += _colsum(dh0)
        dx_ref[...] = _ln_bwd(dh0 * g_ref[...], xhat, rstd)

    return pl.pallas_call(
        body, grid=(n_tok // tl,), name="inproj_bwd",
        in_specs=[_tile(tl, D_XBC), _tile(tl, D_SSM), _tile(tl, 2 * D_CONF), _tile(tl, D_CONF), _tile(tl, LANES),
                  _tile(tl, D_MODEL), _tile(tl, D_MODEL), _full((1, D_MODEL)), _full((1, D_MODEL)),
                  _full((D_MODEL, D_INR), True)],
        out_specs=[_tile(tl, D_MODEL), _tile(tl, D_INR), _full((SUBLANES, D_MODEL))],
        out_shape=[jax.ShapeDtypeStruct((n_tok, D_MODEL), F32), jax.ShapeDtypeStruct((n_tok, D_INR), BF16),
                   jax.ShapeDtypeStruct((SUBLANES, D_MODEL), F32)],
        compiler_params=_params(),
    )(dxin, dz, dglu, dcg, ddtr, dr1, x, g, b, w_r)


def _tn_matmul(a, b, name, tn, tl):
    n_tok, m = a.shape
    n = b.shape[1]

    def body(a_ref, b_ref, o_ref):
        @pl.when(pl.program_id(1) == 0)
        def _():
            o_ref[...] = jnp.zeros_like(o_ref)

        o_ref[...] += lax.dot_general(a_ref[...], b_ref[...], TN_DIMS, preferred_element_type=F32)

    return pl.pallas_call(
        body, grid=(n // tn, n_tok // tl), name=name,
        in_specs=[pl.BlockSpec((tl, m), lambda j, l: (l, 0)), pl.BlockSpec((tl, tn), lambda j, l: (l, j))],
        out_specs=pl.BlockSpec((m, tn), lambda j, l: (0, j)),
        out_shape=jax.ShapeDtypeStruct((m, n), F32),
        compiler_params=pltpu.CompilerParams(dimension_semantics=("parallel", "arbitrary"), vmem_limit_bytes=VMEM_LIMIT),
    )(a, b)


def _pad_rows(a, rows):
    return jnp.pad(a, ((0, rows - a.shape[0]), (0, 0)))


def _pad_lanes(a):
    return jnp.pad(a, ((0, 0), (0, LANES - a.shape[1])))


def _local_grads(x, p, tgt, w_r, w_out_b, wpg_b, wpp_b, ssm_cw, conf_cw, sm):
    n_tok = x.shape[0]
    tl = min(256, n_tok)
    row = lambda v: v.reshape(1, -1)
    g_e, b_e = row(sm["ln_emb_g"]), row(sm["ln_emb_b"])
    h0b, xbc_in, z, glu, cgate, dtr = _ln_inproj(x, g_e, b_e, w_r, tl)

    cw4 = _pad_rows(ssm_cw, SUBLANES)
    dtb, alog = _pad_lanes(sm["dt_bias"]), _pad_lanes(sm["a_log"])
    dsk = jnp.repeat(sm["d_skip"], HEAD_DIM, axis=1)
    pre, y, yssm, hprev = _ssd_fwd(xbc_in, z, dtr, cw4, sm["ssm_conv_b"], dtb, alog, dsk, sm["ssm_norm_g"], tl)

    cw31 = _pad_rows(conf_cw, CONF_HALO)
    u0, u1, yconf = _conf_fwd(glu, cgate, sm["b_glu"], cw31, sm["conf_conv_b"], sm["conf_ln_g"], sm["conf_ln_b"], tl)

    vec = jnp.concatenate([g_e, b_e, sm["b_out"], sm["ln1_g"], sm["ln1_b"], sm["ln2_g"], sm["ln2_b"],
                           jnp.zeros((1, D_MODEL), F32)], axis=0)
    dmix, dr1, dr1b, h1b, dgb, dpb, small_t, loss = _tail(
        x, yssm, yconf, p, tgt, vec, w_out_b, wpg_b, wpp_b, tl)

    dglu, dcg, dcw31, dbglu, small_c = _conf_bwd(dmix, D_SSM // D_CONF, u0, u1, glu, cgate, sm["b_glu"], cw31,
                                                  sm["conf_ln_g"], sm["conf_ln_b"], tl)
    dxin, dz, ddtr, dcw4, dcb4, dgn, ddsk, dalog, ddtb = _ssd_bwd(
        dmix, y, z, pre, xbc_in, dtr, hprev, cw4, dtb, alog, dsk, sm["ssm_norm_g"], tl)
    dx, dprojb, small_e = _inproj_bwd(dxin, dz, dglu, dcg, ddtr, dr1, x, g_e, b_e, w_r, tl)

    tlm = min(512, n_tok)
    dw_r = _tn_matmul(h0b, dprojb, "dw_in", D_INR // 3, tlm)
    dw_out = jnp.concatenate([_tn_matmul(yssm, dr1b, "dw_out_ssm", D_MODEL, tlm),
                              _tn_matmul(yconf, dr1b, "dw_out_conf", D_MODEL, tlm)], axis=0)
    dwpg = _tn_matmul(h1b, dgb, "dw_ple_gate", D_MODEL, tlm)
    dwpp = _tn_matmul(p.astype(BF16), dpb, "dw_ple_proj", D_MODEL, tlm)
    grads = dict(
        ln_emb_g=small_e[0], ln_emb_b=small_e[1], w_in=dw_r, ssm_conv_w=dcw4[0:SSM_K], ssm_conv_b=dcb4,
        dt_bias=ddtb[:, 0:N_HEADS], a_log=dalog[:, 0:N_HEADS], d_skip=ddsk[:, 0:N_HEADS], ssm_norm_g=dgn, b_glu=dbglu,
        conf_conv_w=dcw31[0:CONF_K], conf_conv_b=small_c[2:3], conf_ln_g=small_c[0:1], conf_ln_b=small_c[1:2],
        w_out=dw_out, b_out=small_t[0:1], ln1_g=small_t[1:2], ln1_b=small_t[2:3], w_ple_gate=dwpg, w_ple_proj=dwpp,
        ln2_g=small_t[3:4], ln2_b=small_t[4:5])
    return loss[0, 0], dx, grads


N_CHIPS = 4
N_DEV = 8
W_IN_SH = D_IN // N_CHIPS
BIG = (("w_in", D_MODEL, W_IN_SH), ("w_out", (D_SSM + D_CONF) // N_CHIPS, D_MODEL),
       ("w_ple_gate", D_MODEL // N_CHIPS, D_MODEL), ("w_ple_proj", D_PLE, D_MODEL // N_CHIPS))
SEGS = ((R_XBC, 0, D_SSM), (R_XBC + D_SSM, 2048, 256), (R_XBC + D_SSM + 256, 2304, 256), (R_Z, 1024, D_SSM),
        (R_GLU, 2576, 2 * D_CONF), (R_CG, 4624, D_CONF), (R_DT, 2560, N_HEADS))
ROWS_CW4 = 2
ROWS_CW31 = 8
ROWS_CONV = 16
SMALL_ROWS = 56
SMALL_LAYOUT = (("ln_emb_g", 0, 1), ("ln_emb_b", 1, 1), ("ssm_conv_b", 2, 2), ("dt_bias", 4, 1), ("a_log", 5, 1),
                ("d_skip", 6, 1), ("ssm_norm_g", 7, 1), ("b_glu", 8, 2), ("conf_conv_b", 10, 1), ("conf_ln_g", 11, 1),
                ("conf_ln_b", 12, 1), ("b_out", 13, 1), ("ln1_g", 14, 1), ("ln1_b", 15, 1), ("ln2_g", 16, 1), ("ln2_b", 17, 1))
CONV_LAYOUT = (("ssm_conv_w", 18, 6, (SSM_K, D_XBC)), ("conf_conv_w", 24, 31, (CONF_K, D_CONF)))


def _rows_of(v, rows):
    flat = v.reshape(-1)
    return jnp.pad(flat, (0, rows * D_MODEL - flat.shape[0])).reshape(rows, D_MODEL)


def _pack_small(d):
    parts = [_rows_of(d[n], r) for n, _, r in SMALL_LAYOUT]
    for n, _, r, _ in CONV_LAYOUT:
        parts.append(_rows_of(d[n], r) if n in d else jnp.zeros((r, D_MODEL), F32))
    parts.append(jnp.zeros((SMALL_ROWS - 55, D_MODEL), F32))
    return jnp.concatenate(parts, axis=0)


def _unpack_small(a, like):
    return {n: a[r0:r0 + r].reshape(-1)[:like[n].size].reshape(like[n].shape) for n, r0, r in SMALL_LAYOUT}


def _w_r_from_shards(w4):
    parts = []
    for _, o, wd in SEGS:
        lo, hi = o, o + wd
        while lo < hi:
            s = lo // W_IN_SH
            e = min(hi, (s + 1) * W_IN_SH)
            parts.append(w4[s][:, lo - s * W_IN_SH:e - s * W_IN_SH])
            lo = e
    parts.append(jnp.zeros((D_MODEL, LANES - N_HEADS), w4.dtype))
    return jnp.concatenate(parts, axis=1)


def _shards_from_dw_r(dw_r):
    slabs = []
    for s in range(N_CHIPS):
        lo, hi = s * W_IN_SH, (s + 1) * W_IN_SH
        parts = []
        for kcol, o, wd in sorted(SEGS, key=lambda t: t[1]):
            a, b = max(lo, o), min(hi, o + wd)
            if a < b:
                parts.append(dw_r[:, kcol + a - o:kcol + b - o])
        slabs.append(jnp.concatenate(parts, axis=1))
    return jnp.stack(slabs)


def _row_chunks(rows, n):
    return [(j * (rows // n), rows // n) for j in range(n)]


def _my_place():
    return lax.axis_index("x"), lax.axis_index("y"), lax.axis_index("c")


MESH_ID = pl.DeviceIdType.MESH
ANY = pl.BlockSpec(memory_space=pl.ANY)
IN_VMEM = pl.BlockSpec(memory_space=pltpu.VMEM)
CHIP_FLIPS = ((1, 0), (0, 1), (1, 1))


def _remote(src, dst, send_sem, recv_sem, peer):
    return pltpu.make_async_remote_copy(src, dst, send_sem, recv_sem, device_id=peer, device_id_type=MESH_ID)


GATHER_CHUNKS = (4, 2, 1, 1)


def _gather_weights(shards, conv_f):
    n_big = len(BIG)
    plan = [(a, o, n, rows // 2) for a, (_, rows, _) in enumerate(BIG) for o, n in _row_chunks(rows // 2, GATHER_CHUNKS[a])]
    own_plan = [(a, o, n) for a, (_, rows, _) in enumerate(BIG) for o, n in _row_chunks(rows, 2 * GATHER_CHUNKS[a])]
    own_plan.append((n_big, 0, ROWS_CONV))

    def body(*refs):
        ins, outs = refs[0:n_big + 1], refs[n_big + 1:2 * n_big + 2]
        send_a, recv_a, send_b, recv_b, loc = refs[2 * n_big + 2:]
        x, y, c = _my_place()
        s = 2 * x + y
        sibling = (x, y, 1 - c)
        own = [pltpu.make_async_copy(ins[a].at[pl.ds(o, n)], outs[a].at[s, pl.ds(o, n)], loc.at[j])
               for j, (a, o, n) in enumerate(own_plan)]
        first = []
        for k, (fx, fy) in enumerate(CHIP_FLIPS):
            peer = (x ^ fx, y ^ fy, c)
            for j, (a, o, n, h) in enumerate(plan):
                r0 = pl.multiple_of(c * h + o, 16)
                first.append(_remote(ins[a].at[pl.ds(r0, n)], outs[a].at[s, pl.ds(r0, n)], send_a.at[k, j], recv_a.at[k, j], peer))
            first.append(_remote(ins[n_big], outs[n_big].at[s], send_a.at[k, len(plan)], recv_a.at[k, len(plan)], peer))
        for cp in own + first:
            cp.start()
        passed = []
        for k, (fx, fy) in enumerate(CHIP_FLIPS):
            peer = (x ^ fx, y ^ fy, c)
            sk = 2 * (x ^ fx) + (y ^ fy)
            for j, (a, o, n, h) in enumerate(plan):
                land = outs[a].at[sk, pl.ds(pl.multiple_of(c * h + o, 16), n)]
                _remote(land, land, send_a.at[k, j], recv_a.at[k, j], peer).wait_recv()
                cp = _remote(land, land, send_b.at[k, j], recv_b.at[k, j], sibling)
                cp.start()
                passed.append(cp)
            land = outs[n_big].at[sk]
            _remote(land, land, send_a.at[k, len(plan)], recv_a.at[k, len(plan)], peer).wait_recv()
        for k, (fx, fy) in enumerate(CHIP_FLIPS):
            sk = 2 * (x ^ fx) + (y ^ fy)
            for j, (a, o, n, h) in enumerate(plan):
                land = outs[a].at[sk, pl.ds(pl.multiple_of((1 - c) * h + o, 16), n)]
                _remote(land, land, send_b.at[k, j], recv_b.at[k, j], sibling).wait_recv()
        for cp in first + passed:
            cp.wait_send()
        for cp in own:
            cp.wait()

    arrays = list(shards) + [conv_f]
    return pl.pallas_call(
        body, name="gather_weights", in_specs=[IN_VMEM] * len(arrays), out_specs=[ANY] * len(arrays),
        out_shape=[jax.ShapeDtypeStruct((N_CHIPS,) + a.shape, a.dtype) for a in arrays],
        scratch_shapes=[pltpu.SemaphoreType.DMA((3, len(plan) + 1)), pltpu.SemaphoreType.DMA((3, len(plan) + 1)),
                        pltpu.SemaphoreType.DMA((3, len(plan))), pltpu.SemaphoreType.DMA((3, len(plan))),
                        pltpu.SemaphoreType.DMA((len(own_plan),))],
    )(*arrays)


CORES_CHUNKS = (4, 2, 1, 1)
CHIPS_CHUNKS = (2, 1, 1, 1)
SHARE_CHUNKS = (8, 4, 2, 1)


def _exchange_cores(gbig, spack):
    n_big = len(BIG)
    plan = [(a, t, o, n, rows // 2) for a, (_, rows, _) in enumerate(BIG) for t in range(N_CHIPS)
            for o, n in _row_chunks(rows // 2, CORES_CHUNKS[a])]

    def body(*refs):
        g_refs, s_ref = refs[0:n_big], refs[n_big]
        theirs, small_all = refs[n_big + 1:2 * n_big + 1], refs[2 * n_big + 1]
        send_sems, recv_sems, send_small, recv_small, loc_sem = refs[2 * n_big + 2:]
        x, y, c = _my_place()
        me = 4 * x + 2 * y + c
        own = pltpu.make_async_copy(s_ref, small_all.at[me], loc_sem)
        sends = []
        for j, (a, t, o, n, h) in enumerate(plan):
            give = pl.multiple_of((1 - c) * h + o, SUBLANES)
            sends.append(_remote(g_refs[a].at[t, pl.ds(give, n)], theirs[a].at[t, pl.ds(o, n)], send_sems.at[j], recv_sems.at[j],
                                 (x, y, 1 - c)))
        for m in range(1, N_DEV):
            peer = (x ^ (m >> 2), y ^ ((m >> 1) & 1), c ^ (m & 1))
            sends.append(_remote(s_ref, small_all.at[me], send_small.at[m - 1], recv_small.at[m - 1], peer))
        own.start()
        for cp in sends:
            cp.start()
        for cp in sends:
            cp.wait()
        own.wait()

    return pl.pallas_call(
        body, name="exchange_cores", in_specs=[ANY] * n_big + [IN_VMEM], out_specs=[ANY] * (n_big + 1),
        out_shape=[jax.ShapeDtypeStruct((N_CHIPS, rows // 2, cols), F32) for _, rows, cols in BIG]
        + [jax.ShapeDtypeStruct((N_DEV, SMALL_ROWS, D_MODEL), F32)],
        scratch_shapes=[pltpu.SemaphoreType.DMA((len(plan),)), pltpu.SemaphoreType.DMA((len(plan),)),
                        pltpu.SemaphoreType.DMA((N_DEV - 1,)), pltpu.SemaphoreType.DMA((N_DEV - 1,)), pltpu.SemaphoreType.DMA],
    )(*gbig, spack)


def _exchange_chips(psums):
    n_big = len(BIG)
    plan = [(a, o, n) for a, (_, rows, _) in enumerate(BIG) for o, n in _row_chunks(rows // 2, CHIPS_CHUNKS[a])]

    def body(*refs):
        p_refs, got = refs[0:n_big], refs[n_big:2 * n_big]
        send_sems, recv_sems = refs[2 * n_big:]
        x, y, c = _my_place()
        sends = []
        for k, (fx, fy) in enumerate(CHIP_FLIPS):
            tx, ty = x ^ fx, y ^ fy
            for j, (a, o, n) in enumerate(plan):
                sends.append(_remote(p_refs[a].at[2 * tx + ty, pl.ds(o, n)], got[a].at[k, pl.ds(o, n)], send_sems.at[k, j],
                                     recv_sems.at[k, j], (tx, ty, c)))
        for cp in sends:
            cp.start()
        for cp in sends:
            cp.wait()

    return pl.pallas_call(
        body, name="exchange_chips", in_specs=[ANY] * n_big, out_specs=[ANY] * n_big,
        out_shape=[jax.ShapeDtypeStruct((3, rows // 2, cols), BF16) for _, rows, cols in BIG],
        scratch_shapes=[pltpu.SemaphoreType.DMA((3, len(plan))), pltpu.SemaphoreType.DMA((3, len(plan)))],
    )(*psums)


def _share_halves(tots):
    n_big = len(BIG)
    plan = [(a, o, n) for a, (_, rows, _) in enumerate(BIG) for o, n in _row_chunks(rows // 2, SHARE_CHUNKS[a])]

    def body(*refs):
        t_refs, both = refs[0:n_big], refs[n_big:2 * n_big]
        send_sems, recv_sems, loc = refs[2 * n_big:]
        x, y, c = _my_place()
        own = [pltpu.make_async_copy(t_refs[a].at[pl.ds(o, n)], both[a].at[c, pl.ds(o, n)], loc.at[j])
               for j, (a, o, n) in enumerate(plan)]
        sends = [_remote(t_refs[a].at[pl.ds(o, n)], both[a].at[c, pl.ds(o, n)], send_sems.at[j], recv_sems.at[j], (x, y, 1 - c))
                 for j, (a, o, n) in enumerate(plan)]
        for cp in own + sends:
            cp.start()
        for cp in sends:
            cp.wait()
        for cp in own:
            cp.wait()

    return pl.pallas_call(
        body, name="share_halves", in_specs=[IN_VMEM] * n_big, out_specs=[ANY] * n_big,
        out_shape=[jax.ShapeDtypeStruct((2, rows // 2, cols), F32) for _, rows, cols in BIG],
        scratch_shapes=[pltpu.SemaphoreType.DMA((len(plan),)), pltpu.SemaphoreType.DMA((len(plan),)),
                        pltpu.SemaphoreType.DMA((len(plan),))],
    )(*tots)


def _sum_rows(half):
    return half if half <= 256 else 256


def _add_cores(core, g, theirs, name):
    _, rows, cols = g.shape
    half = rows // 2
    tr = _sum_rows(half)
    nb = half // tr

    def body(c_ref, g_ref, t_ref, o_ref):
        o_ref[...] = (g_ref[...] + t_ref[...]).astype(BF16)

    grid_spec = pltpu.PrefetchScalarGridSpec(
        num_scalar_prefetch=1, grid=(N_CHIPS, nb),
        in_specs=[pl.BlockSpec((1, tr, cols), lambda t, i, c_ref: (t, c_ref[0] * nb + i, 0)),
                  pl.BlockSpec((1, tr, cols), lambda t, i, c_ref: (t, i, 0))],
        out_specs=pl.BlockSpec((1, tr, cols), lambda t, i, c_ref: (t, i, 0)))
    return pl.pallas_call(body, grid_spec=grid_spec, name=name, out_shape=jax.ShapeDtypeStruct((N_CHIPS, half, cols), BF16),
                          compiler_params=pltpu.CompilerParams(vmem_limit_bytes=VMEM_LIMIT))(core, g, theirs)


def _add_chips(chip, psum, got, name):
    _, half, cols = psum.shape
    tr = _sum_rows(half)

    def body(s_ref, p_ref, g_ref, o_ref):
        o_ref[...] = ((p_ref[0].astype(F32) + g_ref[0].astype(F32)) + g_ref[1].astype(F32)) + g_ref[2].astype(F32)

    grid_spec = pltpu.PrefetchScalarGridSpec(
        num_scalar_prefetch=1, grid=(half // tr,),
        in_specs=[pl.BlockSpec((1, tr, cols), lambda i, s_ref: (s_ref[0], i, 0)),
                  pl.BlockSpec((3, tr, cols), lambda i, s_ref: (0, i, 0))],
        out_specs=pl.BlockSpec((tr, cols), lambda i, s_ref: (i, 0)))
    return pl.pallas_call(body, grid_spec=grid_spec, name=name, out_shape=jax.ShapeDtypeStruct((half, cols), F32),
                          compiler_params=pltpu.CompilerParams(vmem_limit_bytes=VMEM_LIMIT))(chip, psum, got)


def _adam_math(w, g, m, v):
    m = ADAM_B1 * m + (1.0 - ADAM_B1) * g
    v = ADAM_B2 * v + (1.0 - ADAM_B2) * (g * g)
    m_hat = m / (1.0 - ADAM_B1 ** ADAM_STEP)
    v_hat = v / (1.0 - ADAM_B2 ** ADAM_STEP)
    return -ADAM_LR * (m_hat / (jnp.sqrt(v_hat) + ADAM_EPS) + ADAM_WD * w), m, v


def _adam(w, g, m, v, name):
    rows, cols = w.shape
    tr = rows if rows <= 256 else 256

    def body(w_ref, g_ref, m_ref, v_ref, d_ref, nm_ref, nv_ref):
        d_ref[...], nm_ref[...], nv_ref[...] = _adam_math(w_ref[...], g_ref[...], m_ref[...], v_ref[...])

    spec = pl.BlockSpec((tr, cols), lambda i: (i, 0))
    return pl.pallas_call(body, grid=(rows // tr,), name=name, in_specs=[spec] * 4, out_specs=[spec] * 3,
                          out_shape=[jax.ShapeDtypeStruct(w.shape, F32)] * 3, compiler_params=_params(seq=False))(w, g, m, v)


def _adam_small(w, parts, m, v):
    def body(w_ref, p_ref, m_ref, v_ref, g_ref, d_ref, nm_ref, nv_ref):
        g = p_ref[0]
        for k in range(1, N_DEV):
            g = g + p_ref[k]
        g_ref[...] = g
        d_ref[...], nm_ref[...], nv_ref[...] = _adam_math(w_ref[...], g, m_ref[...], v_ref[...])

    return pl.pallas_call(body, name="adam_small", out_shape=[jax.ShapeDtypeStruct(w.shape, F32)] * 4)(w, parts, m, v)


def kernel(x, p, ln_emb_g, ln_emb_b, w_in, ssm_conv_w, ssm_conv_b, dt_bias, a_log, d_skip, ssm_norm_g, b_glu, conf_conv_w, conf_conv_b, conf_ln_g, conf_ln_b, w_out, b_out, ln1_g, ln1_b, w_ple_gate, w_ple_proj, ln2_g, ln2_b, loss_target, m_ln_emb_g, m_ln_emb_b, m_w_in, m_ssm_conv_w, m_ssm_conv_b, m_dt_bias, m_a_log, m_d_skip, m_ssm_norm_g, m_b_glu, m_conf_conv_w, m_conf_conv_b, m_conf_ln_g, m_conf_ln_b, m_w_out, m_b_out, m_ln1_g, m_ln1_b, m_w_ple_gate, m_w_ple_proj, m_ln2_g, m_ln2_b, v_ln_emb_g, v_ln_emb_b, v_w_in, v_ssm_conv_w, v_ssm_conv_b, v_dt_bias, v_a_log, v_d_skip, v_ssm_norm_g, v_b_glu, v_conf_conv_w, v_conf_conv_b, v_conf_ln_g, v_conf_ln_b, v_w_out, v_b_out, v_ln1_g, v_ln1_b, v_w_ple_gate, v_w_ple_proj, v_ln2_g, v_ln2_b):
    order = ("ln_emb_g", "ln_emb_b", "w_in", "ssm_conv_w", "ssm_conv_b", "dt_bias", "a_log", "d_skip", "ssm_norm_g", "b_glu",
             "conf_conv_w", "conf_conv_b", "conf_ln_g", "conf_ln_b", "w_out", "b_out", "ln1_g", "ln1_b", "w_ple_gate",
             "w_ple_proj", "ln2_g", "ln2_b")
    w = dict(zip(order, (ln_emb_g, ln_emb_b, w_in, ssm_conv_w, ssm_conv_b, dt_bias, a_log, d_skip, ssm_norm_g, b_glu,
                         conf_conv_w, conf_conv_b, conf_ln_g, conf_ln_b, w_out, b_out, ln1_g, ln1_b, w_ple_gate, w_ple_proj,
                         ln2_g, ln2_b)))
    m = dict(zip(order, (m_ln_emb_g, m_ln_emb_b, m_w_in, m_ssm_conv_w, m_ssm_conv_b, m_dt_bias, m_a_log, m_d_skip,
                         m_ssm_norm_g, m_b_glu, m_conf_conv_w, m_conf_conv_b, m_conf_ln_g, m_conf_ln_b, m_w_out, m_b_out,
                         m_ln1_g, m_ln1_b, m_w_ple_gate, m_w_ple_proj, m_ln2_g, m_ln2_b)))
    v = dict(zip(order, (v_ln_emb_g, v_ln_emb_b, v_w_in, v_ssm_conv_w, v_ssm_conv_b, v_dt_bias, v_a_log, v_d_skip,
                         v_ssm_norm_g, v_b_glu, v_conf_conv_w, v_conf_conv_b, v_conf_ln_g, v_conf_ln_b, v_w_out, v_b_out,
                         v_ln1_g, v_ln1_b, v_w_ple_gate, v_w_ple_proj, v_ln2_g, v_ln2_b)))

    conv_f = jnp.concatenate([_rows_of(w["ssm_conv_w"], ROWS_CW4), _rows_of(w["conf_conv_w"], ROWS_CW31),
                              jnp.zeros((ROWS_CONV - ROWS_CW4 - ROWS_CW31, D_MODEL), F32)], axis=0)
    w_in_all, w_out_all, wpg_all, wpp_all, conv_all = _gather_weights([w[n][0].astype(BF16) for n, _, _ in BIG], conv_f)
    w_r = _w_r_from_shards(w_in_all)
    w_out_b = w_out_all.reshape(D_SSM + D_CONF, D_MODEL)
    wpg_b = wpg_all.reshape(D_MODEL, D_MODEL)
    wpp_b = wpp_all.transpose(1, 0, 2).reshape(D_PLE, D_MODEL)
    cw4 = conv_all[:, 0:ROWS_CW4].reshape(N_CHIPS, -1)[:, :SSM_K * D_XBC // N_CHIPS]
    cw4 = cw4.reshape(N_CHIPS, SSM_K, D_XBC // N_CHIPS).transpose(1, 0, 2).reshape(SSM_K, D_XBC)
    cw31 = conv_all[:, ROWS_CW4:ROWS_CW4 + ROWS_CW31].reshape(N_CHIPS, -1)[:, :CONF_K * D_CONF // N_CHIPS]
    cw31 = cw31.reshape(N_CHIPS, CONF_K, D_CONF // N_CHIPS).transpose(1, 0, 2).reshape(CONF_K, D_CONF)

    small_names = [n for n, _, _ in SMALL_LAYOUT]
    sm = {n: w[n] for n in small_names}
    loss_part, dx, grads = _local_grads(x[0], p[0, 0], loss_target[0], w_r, w_out_b, wpg_b, wpp_b, cw4, cw31, sm)
    loss = lax.psum(loss_part, ("x", "y", "c"))

    core = lax.axis_index("c").astype(jnp.int32).reshape(1)
    chip_i = 2 * lax.axis_index("x") + lax.axis_index("y")
    chip = chip_i.astype(jnp.int32).reshape(1)
    gbig = [_shards_from_dw_r(grads["w_in"]), grads["w_out"].reshape(N_CHIPS, -1, D_MODEL),
            grads["w_ple_gate"].reshape(N_CHIPS, -1, D_MODEL),
            grads["w_ple_proj"].reshape(D_PLE, N_CHIPS, D_MODEL // N_CHIPS).transpose(1, 0, 2)]
    spack = _pack_small({n: grads[n] for n in small_names + [n for n, _, _, _ in CONV_LAYOUT]})
    *theirs, small_all = _exchange_cores(gbig, spack)
    psums = [_add_cores(core, g, t, "add_cores_" + n) for g, t, (n, _, _) in zip(gbig, theirs, BIG)]
    got = _exchange_chips(psums)
    tots = [_add_chips(chip, ps, gt, "add_chips_" + n) for ps, gt, (n, _, _) in zip(psums, got, BIG)]
    both = _share_halves(tots)

    out_g, out_d, out_m, out_v = {}, {}, {}, {}
    g_s, d_s, m_s, v_s = _adam_small(_pack_small(sm), small_all, _pack_small({n: m[n] for n in small_names}),
                                     _pack_small({n: v[n] for n in small_names}))
    g_shard = {n: b.reshape(rows, cols) for b, (n, rows, cols) in zip(both, BIG)}
    for n, r0, r, shape in CONV_LAYOUT:
        whole = g_s[r0:r0 + r].reshape(-1)[:shape[0] * shape[1]].reshape(shape)
        g_shard[n] = lax.dynamic_slice_in_dim(whole, chip_i * (shape[1] // N_CHIPS), shape[1] // N_CHIPS, axis=1)
    for n, g in g_shard.items():
        shape = w[n].shape
        d, nm, nv = _adam(w[n][0], g, m[n][0], v[n][0], "adam_" + n)
        out_g[n], out_d[n], out_m[n], out_v[n] = g[None], d[None], nm[None], nv[None]
    for dst, src in ((out_g, g_s), (out_d, d_s), (out_m, m_s), (out_v, v_s)):
        dst.update(_unpack_small(src, sm))
    return (loss, dx[None], *[out_g[n] for n in order], *[out_d[n] for n in order], *[out_m[n] for n in order],
            *[out_v[n] for n in order])
```

```python
import functools

import jax
import jax.numpy as jnp
from jax import lax
from jax.experimental import pallas as pl
from jax.experimental.pallas import tpu as pltpu

F32 = jnp.float32
BF16 = jnp.bfloat16

D_MODEL = 1024
D_PLE = 256
D_SSM = 1024
D_CONF = 1024
N_HEADS = 16
HEAD_DIM = 64
N_GROUPS = 2
N_STATE = 128
CHUNK = 128
SSM_K = 4
CONF_K = 31
D_XBC = D_SSM + 2 * N_GROUPS * N_STATE
D_IN = 5648
R_XBC, R_Z, R_GLU, R_CG, R_DT, D_INR = 0, 1536, 2560, 4608, 5632, 5760
LN_EPS = 1e-5
RMS_EPS = 1e-5
ALPHA = 2.0 ** 0.25
ADAM_LR, ADAM_B1, ADAM_B2, ADAM_EPS, ADAM_WD, ADAM_STEP = 0.001, 0.9, 0.999, 1e-08, 0.01, 10
NEG_BIG = -1e30
LANES = 128
SUBLANES = 8
VMEM_LIMIT = 56 * 1024 * 1024
HIGHEST = lax.Precision.HIGHEST
NT_DIMS = (((1,), (1,)), ((), ()))
TN_DIMS = (((0,), (0,)), ((), ()))


def _sig(v):
    return jax.nn.sigmoid(v)


def _dsilu(v, s):
    return s * (1.0 + v * (1.0 - s))


def _ln_stats(v):
    mu = jnp.mean(v, axis=-1, keepdims=True)
    c = v - mu
    var = jnp.mean(c * c, axis=-1, keepdims=True)
    rstd = lax.rsqrt(var + LN_EPS)
    return c * rstd, rstd


def _ln_bwd(dxhat, xhat, rstd):
    m1 = jnp.mean(dxhat, axis=-1, keepdims=True)
    m2 = jnp.mean(dxhat * xhat, axis=-1, keepdims=True)
    return rstd * (dxhat - m1 - xhat * m2)


def _softplus(v):
    return jnp.maximum(v, 0.0) + jnp.log1p(jnp.exp(-jnp.abs(v)))


def _colsum(v):
    return jnp.sum(v, axis=0, keepdims=True)


def _dot(a, b):
    return jnp.dot(a, b, preferred_element_type=F32)


def _dot_nt(a, b):
    return lax.dot_general(a, b, NT_DIMS, preferred_element_type=F32)


def _tile(tl, c, rev_of=None):
    if rev_of is None:
        return pl.BlockSpec((tl, c), lambda i: (i, 0))
    return pl.BlockSpec((tl, c), lambda i: (rev_of - 1 - i, 0))


def _full(shape, single=False):
    nd = len(shape)
    if single:
        return pl.BlockSpec(shape, lambda i: (0,) * nd, pipeline_mode=pl.Buffered(1))
    return pl.BlockSpec(shape, lambda i: (0,) * nd)


def _params(seq=True):
    return pltpu.CompilerParams(dimension_semantics=("arbitrary",) if seq else ("parallel",), vmem_limit_bytes=VMEM_LIMIT)


def _ln_inproj(x, g, b, w_r, tl):
    n_tok = x.shape[0]

    def body(x_ref, g_ref, b_ref, w_ref, h0b_ref, xbc_ref, z_ref, glu_ref, cg_ref, dtr_ref):
        xhat, _ = _ln_stats(x_ref[...])
        hb = (xhat * g_ref[...] + b_ref[...]).astype(BF16)
        h0b_ref[...] = hb
        xbc_ref[...] = _dot(hb, w_ref[:, R_XBC:R_Z])
        z_ref[...] = _dot(hb, w_ref[:, R_Z:R_GLU])
        glu_ref[...] = _dot(hb, w_ref[:, R_GLU:R_CG])
        cg_ref[...] = _dot(hb, w_ref[:, R_CG:R_DT])
        dtr_ref[...] = _dot(hb, w_ref[:, R_DT:D_INR])

    widths = (D_MODEL, D_XBC, D_SSM, 2 * D_CONF, D_CONF, LANES)
    dtypes = (BF16, F32, F32, F32, F32, F32)
    return pl.pallas_call(
        body, grid=(n_tok // tl,), name="ln_inproj",
        in_specs=[_tile(tl, D_MODEL), _full((1, D_MODEL)), _full((1, D_MODEL)), _full((D_MODEL, D_INR), single=True)],
        out_specs=[_tile(tl, w) for w in widths],
        out_shape=[jax.ShapeDtypeStruct((n_tok, w), dt) for w, dt in zip(widths, dtypes)],
        compiler_params=_params(seq=False),
    )(x, g, b, w_r)


def _chunk_common(adt_c):
    row = lax.broadcasted_iota(jnp.int32, (CHUNK, CHUNK), 0)
    col = lax.broadcasted_iota(jnp.int32, (CHUNK, CHUNK), 1)
    tril = row >= col
    acs = jnp.dot(tril.astype(F32), adt_c, precision=HIGHEST, preferred_element_type=F32)
    last = acs[CHUNK - 1:CHUNK, :]
    return dict(row=row, col=col, tril=tril, lo=col < HEAD_DIM, acs=acs, acs_t=acs.T, e=jnp.exp(acs),
                dec=jnp.exp(last - acs), cd=jnp.exp(last))


def _decay_mask(cm, h):
    return jnp.exp(jnp.where(cm["tril"], cm["acs"][:, h:h + 1] - cm["acs_t"][h:h + 1, :], NEG_BIG))


def _head_lane_matrix():
    return (jnp.arange(D_SSM)[None, :] // HEAD_DIM == jnp.arange(LANES)[:, None]).astype(BF16)


def _per_head_lanes(v, exp_ref):
    hi = v.astype(BF16)
    lo = (v - hi.astype(F32)).astype(BF16)
    return _dot(hi, exp_ref[...]) + _dot(lo, exp_ref[...])


SSM_BWD_SHIFTS = (1, 2, 3)


def _ssd_fwd(xbc_in, z, dtr, cw, cb, dtb, alog, dsk, gnorm, tl):
    n_tok = xbc_in.shape[0]
    nq = tl // CHUNK

    def body(xin_ref, z_ref, dtr_ref, cw_ref, cb_ref, dtb_ref, alog_ref, dsk_ref, gn_ref, exp_ref,
             pre_ref, y_ref, yssm_ref, hprev_ref, buf, hst):
        @pl.when(pl.program_id(0) == 0)
        def _():
            buf[0:SUBLANES, :] = jnp.zeros((SUBLANES, D_XBC), F32)
            hst[...] = jnp.zeros_like(hst)

        buf[SUBLANES:SUBLANES + tl, :] = xin_ref[...]
        pre = cb_ref[...] + jnp.zeros((tl, D_XBC), F32)
        for k in range(SSM_K):
            off = SUBLANES - (SSM_K - 1) + k
            pre = pre + buf[off:off + tl, :] * cw_ref[k:k + 1, :]
        buf[0:SUBLANES, :] = buf[tl:tl + SUBLANES, :]
        pre_ref[...] = pre
        xbc = pre * _sig(pre)
        dt = _softplus(dtr_ref[...] + dtb_ref[...])
        a = -jnp.exp(alog_ref[...])
        adt = dt * a
        for q in range(nq):
            r0 = q * CHUNK
            cm = _chunk_common(adt[r0:r0 + CHUNK, :])
            dt_x = _per_head_lanes(dt[r0:r0 + CHUNK, :], exp_ref)
            e_x = _per_head_lanes(cm["e"], exp_ref)
            dec_x = _per_head_lanes(cm["dec"], exp_ref)
            for g in range(N_GROUPS):
                bg = xbc[r0:r0 + CHUNK, D_SSM + g * N_STATE:D_SSM + (g + 1) * N_STATE].astype(BF16)
                cg_ = xbc[r0:r0 + CHUNK, D_SSM + (N_GROUPS + g) * N_STATE:D_SSM + (N_GROUPS + g + 1) * N_STATE].astype(BF16)
                gm = _dot_nt(cg_, bg)
                for k in range(N_HEADS // N_GROUPS // 2):
                    ha = (N_HEADS // N_GROUPS) * g + 2 * k
                    c0 = ha * HEAD_DIM
                    xh2 = xbc[r0:r0 + CHUNK, c0:c0 + LANES]
                    x2 = xh2 * dt_x[:, c0:c0 + LANES]
                    x2b = x2.astype(BF16)
                    ya = _dot((gm * _decay_mask(cm, ha)).astype(BF16), x2b)
                    yb = _dot((gm * _decay_mask(cm, ha + 1)).astype(BF16), x2b)
                    h2 = hst[c0:c0 + LANES, :]
                    hprev_ref[q, c0:c0 + LANES, :] = h2
                    z2 = _dot_nt(cg_, h2.astype(BF16))
                    y2 = jnp.where(cm["lo"], ya, yb) + z2 * e_x[:, c0:c0 + LANES] + dsk_ref[:, c0:c0 + LANES] * xh2
                    y_ref[r0:r0 + CHUNK, c0:c0 + LANES] = y2
                    s2 = _dot((x2 * dec_x[:, c0:c0 + LANES]).T.astype(BF16), bg)
                    cd2 = jnp.where(cm["row"] < HEAD_DIM, cm["cd"][:, ha:ha + 1], cm["cd"][:, ha + 1:ha + 2])
                    hst[c0:c0 + LANES, :] = cd2 * h2 + s2
        yv = y_ref[...]
        zv = z_ref[...]
        yz = yv * (zv * _sig(zv))
        gw = D_SSM // N_GROUPS
        for g in range(N_GROUPS):
            seg = yz[:, g * gw:(g + 1) * gw]
            r = lax.rsqrt(jnp.mean(seg * seg, axis=-1, keepdims=True) + RMS_EPS)
            yssm_ref[:, g * gw:(g + 1) * gw] = (seg * r * gn_ref[:, g * gw:(g + 1) * gw]).astype(BF16)

    return pl.pallas_call(
        body, grid=(n_tok // tl,), name="ssd_fwd",
        in_specs=[_tile(tl, D_XBC), _tile(tl, D_SSM), _tile(tl, LANES), _full((SUBLANES, D_XBC)), _full((1, D_XBC)),
                  _full((1, LANES)), _full((1, LANES)), _full((1, D_SSM)), _full((1, D_SSM)), _full((LANES, D_SSM))],
        out_specs=[_tile(tl, D_XBC), _tile(tl, D_SSM), _tile(tl, D_SSM),
                   pl.BlockSpec((nq, D_SSM, N_STATE), lambda i: (i, 0, 0))],
        out_shape=[jax.ShapeDtypeStruct((n_tok, D_XBC), F32), jax.ShapeDtypeStruct((n_tok, D_SSM), F32),
                   jax.ShapeDtypeStruct((n_tok, D_SSM), BF16), jax.ShapeDtypeStruct((n_tok // CHUNK, D_SSM, N_STATE), F32)],
        scratch_shapes=[pltpu.VMEM((tl + SUBLANES, D_XBC), F32), pltpu.VMEM((D_SSM, N_STATE), F32)],
        compiler_params=_params(),
    )(xbc_in, z, dtr, cw, cb, dtb, alog, dsk, gnorm, _head_lane_matrix())


def _ssd_bwd(dys, y, z, pre, xbc_in, dtr, hprev, cw, dtb, alog, dsk, gnorm, tl):
    n_tok = y.shape[0]
    n_t = n_tok // tl
    nq = tl // CHUNK

    def body(dys_ref, y_ref, z_ref, pre_ref, xin_ref, dtr_ref, hprev_ref, cw_ref, dtb_ref, alog_ref, dsk_ref, gn_ref,
             exp_ref, redx_ref, redq_ref,
             dxin_ref, dz_ref, ddtr_ref, dcw_ref, dcb_ref, dgn_ref, ddsk_ref, da_ref, ddtb_ref,
             dxs, dh, cs_s, dskc, shifts, dwp, sums_q, sums_s):
        i = pl.program_id(0)

        @pl.when(i == 0)
        def _():
            dcw_ref[...] = jnp.zeros_like(dcw_ref)
            dwp[...] = jnp.zeros_like(dwp)
            dcb_ref[...] = jnp.zeros_like(dcb_ref)
            dgn_ref[...] = jnp.zeros_like(dgn_ref)
            da_ref[...] = jnp.zeros_like(da_ref)
            ddtb_ref[...] = jnp.zeros_like(ddtb_ref)
            dskc[...] = jnp.zeros_like(dskc)
            dh[...] = jnp.zeros_like(dh)
            dxs[tl:tl + SUBLANES, :] = jnp.zeros((SUBLANES, D_XBC), F32)

        yv = y_ref[...]
        zv = z_ref[...]
        dysv = dys_ref[...]
        sz = _sig(zv)
        silz = zv * sz
        yz = yv * silz
        gw = D_SSM // N_GROUPS
        dyz_parts = []
        for g in range(N_GROUPS):
            sl = slice(g * gw, (g + 1) * gw)
            seg = yz[:, sl]
            r = lax.rsqrt(jnp.mean(seg * seg, axis=-1, keepdims=True) + RMS_EPS)
            yzn = seg * r
            dgn_ref[:, sl] += _colsum(dysv[:, sl] * yzn)
            dyzn = dysv[:, sl] * gn_ref[:, sl]
            dyz_parts.append(r * (dyzn - yzn * jnp.mean(dyzn * yzn, axis=-1, keepdims=True)))
        dyz = jnp.concatenate(dyz_parts, axis=1)
        dy = dyz * silz
        dz_ref[...] = dyz * yv * _dsilu(zv, sz)

        prev = pre_ref[...]
        sp = _sig(prev)
        xbc = prev * sp
        dskc[...] += _colsum(dy * xbc[:, 0:D_SSM])
        dt_in = dtr_ref[...] + dtb_ref[...]
        dt = _softplus(dt_in)
        dsp = _sig(dt_in)
        a = -jnp.exp(alog_ref[...])
        adt = dt * a
        for q in reversed(range(nq)):
            r0 = q * CHUNK
            cm = _chunk_common(adt[r0:r0 + CHUNK, :])
            row, col, lo = cm["row"], cm["col"], cm["lo"]
            triu = (col >= row).astype(F32)
            dt_c = dt[r0:r0 + CHUNK, :]
            dt_x = _per_head_lanes(dt_c, exp_ref)
            e_x = _per_head_lanes(cm["e"], exp_ref)
            dec_x = _per_head_lanes(cm["dec"], exp_ref)
            dcd_row = jnp.zeros((1, LANES), F32)
            for g in range(N_GROUPS):
                bcol = D_SSM + g * N_STATE
                ccol = D_SSM + (N_GROUPS + g) * N_STATE
                bg = xbc[r0:r0 + CHUNK, bcol:bcol + N_STATE].astype(BF16)
                cg_ = xbc[r0:r0 + CHUNK, ccol:ccol + N_STATE].astype(BF16)
                gm = _dot_nt(cg_, bg)
                dgm = jnp.zeros((CHUNK, CHUNK), F32)
                dbg = jnp.zeros((CHUNK, N_STATE), F32)
                dcg = jnp.zeros((CHUNK, N_STATE), F32)
                for k in range(N_HEADS // N_GROUPS // 2):
                    ha = (N_HEADS // N_GROUPS) * g + 2 * k
                    hb = ha + 1
                    c0 = ha * HEAD_DIM
                    xh2 = xbc[r0:r0 + CHUNK, c0:c0 + LANES]
                    dt2 = dt_x[:, c0:c0 + LANES]
                    x2 = xh2 * dt2
                    x2b = x2.astype(BF16)
                    la = _decay_mask(cm, ha)
                    lb = _decay_mask(cm, hb)
                    ma = gm * la
                    mb = gm * lb
                    dy2 = dy[r0:r0 + CHUNK, c0:c0 + LANES]
                    dy2b = dy2.astype(BF16)
                    dma = _dot_nt(jnp.where(lo, dy2, 0.0).astype(BF16), x2b)
                    dmb = _dot_nt(jnp.where(lo, 0.0, dy2).astype(BF16), x2b)
                    dx2 = jnp.where(lo, _dot(ma.T.astype(BF16), dy2b), _dot(mb.T.astype(BF16), dy2b))
                    dgm = dgm + dma * la + dmb * lb
                    sums_q[:, ha * CHUNK:(ha + 1) * CHUNK] = (dma * ma).astype(BF16)
                    sums_q[:, hb * CHUNK:(hb + 1) * CHUNK] = (dmb * mb).astype(BF16)
                    h2 = hprev_ref[q, c0:c0 + LANES, :]
                    h2b = h2.astype(BF16)
                    dz2 = dy2 * e_x[:, c0:c0 + LANES]
                    dcg = dcg + _dot(dz2.astype(BF16), h2b)
                    sums_s[0, :, c0:c0 + LANES] = (dz2 * _dot_nt(cg_, h2b)).astype(BF16)
                    dhn = dh[c0:c0 + LANES, :]
                    dhnb = dhn.astype(BF16)
                    cd_a = cm["cd"][:, ha:ha + 1]
                    cd_b = cm["cd"][:, hb:hb + 1]
                    top = row < HEAD_DIM
                    hh = dhn * h2
                    dcd_a = jnp.sum(_colsum(jnp.where(top, hh, 0.0)), axis=1, keepdims=True)
                    dcd_b = jnp.sum(_colsum(jnp.where(top, 0.0, hh)), axis=1, keepdims=True)
                    dcd_row = dcd_row + jnp.where(col[0:1, :] == ha, dcd_a * cd_a, 0.0) + jnp.where(col[0:1, :] == hb, dcd_b * cd_b, 0.0)
                    dh[c0:c0 + LANES, :] = jnp.where(top, cd_a, cd_b) * dhn + _dot(dz2.T.astype(BF16), cg_)
                    w2 = _dot_nt(bg, dhnb)
                    dec2 = dec_x[:, c0:c0 + LANES]
                    dx2 = dx2 + dec2 * w2
                    sums_s[1, :, c0:c0 + LANES] = (x2 * w2).astype(BF16)
                    dbg = dbg + _dot((x2 * dec2).astype(BF16), dhnb)
                    sums_s[2, :, c0:c0 + LANES] = (dx2 * xh2).astype(BF16)
                    dxs[r0:r0 + CHUNK, c0:c0 + LANES] = dx2 * dt2 + dsk_ref[:, c0:c0 + LANES] * dy2
                dgmb = dgm.astype(BF16)
                dxs[r0:r0 + CHUNK, bcol:bcol + N_STATE] = dbg + _dot(dgm.T.astype(BF16), cg_)
                dxs[r0:r0 + CHUNK, ccol:ccol + N_STATE] = dcg + _dot(dgmb, bg)
            q_all = sums_q[...]
            q_cols = _dot(jnp.ones((SUBLANES, CHUNK), BF16), q_all)
            cs_s[...] = jnp.zeros_like(cs_s)
            for h in range(N_HEADS):
                cs_s[h:h + 1, :] = q_cols[0:1, h * CHUNK:(h + 1) * CHUNK]
            de = _dot(sums_s[0], redx_ref[...])
            dd = _dot(sums_s[1], redx_ref[...]) * cm["dec"]
            ddtx = _dot(sums_s[2], redx_ref[...])
            is_last = row == CHUNK - 1
            dacs = _dot(q_all, redq_ref[...]) - cs_s[...].T + de - dd + jnp.where(is_last, dcd_row + _colsum(dd), 0.0)
            dadt = jnp.dot(triu, dacs, precision=HIGHEST, preferred_element_type=F32)
            da_ref[...] += _colsum(dadt * dt_c)
            ddtr_c = (dadt * a + ddtx) * dsp[r0:r0 + CHUNK, :]
            ddtr_ref[r0:r0 + CHUNK, :] = ddtr_c
            ddtb_ref[...] += _colsum(ddtr_c)

        dpre = dxs[0:tl, :] * _dsilu(prev, sp)
        dxs[0:tl, :] = dpre
        dcb_ref[...] += _colsum(dpre)
        _shift_copies(dxs, shifts, tl, SSM_BWD_SHIFTS)

        def strip(rb, carry):
            i0 = pl.multiple_of(rb * CONV_RS, CONV_RS)
            for c0 in range(0, D_XBC, CONV_CS):
                xin_s = xin_ref[pl.ds(i0, CONV_RS), c0:c0 + CONV_CS]
                acc = jnp.zeros((CONV_RS, CONV_CS), F32)
                for k in range(SSM_K):
                    sh = _tap_rows(dxs, shifts, SSM_K - 1 - k, i0, c0, SSM_BWD_SHIFTS)
                    acc = acc + sh * cw_ref[k:k + 1, c0:c0 + CONV_CS]
                    t = xin_s * sh
                    part = t[0:SUBLANES]
                    for j in range(1, CONV_RS // SUBLANES):
                        part = part + t[j * SUBLANES:(j + 1) * SUBLANES]
                    dwp[k * SUBLANES:(k + 1) * SUBLANES, c0:c0 + CONV_CS] += part
                dxin_ref[pl.ds(i0, CONV_RS), c0:c0 + CONV_CS] = acc
            return carry

        lax.fori_loop(0, tl // CONV_RS, strip, 0)
        dxs[tl:tl + SUBLANES, :] = dxs[0:SUBLANES, :]

        @pl.when(i == n_t - 1)
        def _():
            for k in range(SSM_K):
                dcw_ref[k:k + 1, :] = _colsum(dwp[k * SUBLANES:(k + 1) * SUBLANES, :])
            da_ref[...] = da_ref[...] * a
            sel = (lax.broadcasted_iota(jnp.int32, (D_SSM, LANES), 0) // HEAD_DIM
                   == lax.broadcasted_iota(jnp.int32, (D_SSM, LANES), 1)).astype(F32)
            rows = jnp.broadcast_to(dskc[...], (SUBLANES, D_SSM))
            ddsk_ref[...] = jnp.dot(rows, sel, precision=HIGHEST, preferred_element_type=F32)[0:1, :]

    head_lanes = _head_lane_matrix()
    per_head = (jnp.arange(N_HEADS * CHUNK)[:, None] // CHUNK == jnp.arange(LANES)[None, :]).astype(BF16)
    rev = functools.partial(_tile, tl, rev_of=n_t)
    return pl.pallas_call(
        body, grid=(n_t,), name="ssd_bwd",
        in_specs=[rev(D_SSM), rev(D_SSM), rev(D_SSM), rev(D_XBC), rev(D_XBC), rev(LANES),
                  pl.BlockSpec((nq, D_SSM, N_STATE), lambda i: (n_t - 1 - i, 0, 0)),
                  _full((SUBLANES, D_XBC)), _full((1, LANES)), _full((1, LANES)), _full((1, D_SSM)), _full((1, D_SSM)),
                  _full((LANES, D_SSM)), _full((D_SSM, LANES)), _full((N_HEADS * CHUNK, LANES))],
        out_specs=[rev(D_XBC), rev(D_SSM), rev(LANES), _full((SUBLANES, D_XBC)), _full((1, D_XBC)), _full((1, D_SSM)),
                   _full((1, LANES)), _full((1, LANES)), _full((1, LANES))],
        out_shape=[jax.ShapeDtypeStruct((n_tok, D_XBC), F32), jax.ShapeDtypeStruct((n_tok, D_SSM), F32),
                   jax.ShapeDtypeStruct((n_tok, LANES), F32), jax.ShapeDtypeStruct((SUBLANES, D_XBC), F32),
                   jax.ShapeDtypeStruct((1, D_XBC), F32), jax.ShapeDtypeStruct((1, D_SSM), F32),
                   jax.ShapeDtypeStruct((1, LANES), F32), jax.ShapeDtypeStruct((1, LANES), F32),
                   jax.ShapeDtypeStruct((1, LANES), F32)],
        scratch_shapes=[pltpu.VMEM((tl + SUBLANES, D_XBC), F32), pltpu.VMEM((D_SSM, N_STATE), F32),
                        pltpu.VMEM((CHUNK, LANES), F32), pltpu.VMEM((1, D_SSM), F32),
                        pltpu.VMEM((len(SSM_BWD_SHIFTS), tl, D_XBC), F32), pltpu.VMEM((SSM_K * SUBLANES, D_XBC), F32),
                        pltpu.VMEM((CHUNK, N_HEADS * CHUNK), BF16), pltpu.VMEM((3, CHUNK, D_SSM), BF16)],
        compiler_params=_params(),
    )(dys, y, z, pre, xbc_in, dtr, hprev, cw, dtb, alog, dsk, gnorm, head_lanes, head_lanes.T, per_head)


CONF_HALO = 32
CONV_RS = 32
CONV_CS = 256


ALL_SHIFTS = tuple(range(1, SUBLANES))


def _shift_copies(buf, shifts, n_rows, residues=ALL_SHIFTS):
    for j, r in enumerate(residues):
        shifts[j, 0:n_rows, :] = buf[r:r + n_rows, :]


def _tap_rows(buf, shifts, off, i0, c0, residues=ALL_SHIFTS):
    q, r = divmod(off, SUBLANES)
    rows = pl.ds(pl.multiple_of(i0 + SUBLANES * q, SUBLANES), CONV_RS)
    if r == 0:
        return buf[rows, c0:c0 + CONV_CS]
    return shifts[residues.index(r), rows, c0:c0 + CONV_CS]


def _conf_fwd(glu, cgate, bglu, cw, cb, lg, lb, tl):
    n_tok = glu.shape[0]

    def body(glu_ref, cg_ref, bglu_ref, cw_ref, cb_ref, lg_ref, lb_ref, u0_ref, u1_ref, yc_ref, buf, shifts):
        @pl.when(pl.program_id(0) == 0)
        def _():
            buf[0:CONF_HALO, :] = jnp.zeros((CONF_HALO, D_CONF), F32)

        gl = glu_ref[...] + bglu_ref[...]
        u0 = gl[:, 0:D_CONF] * _sig(gl[:, D_CONF:2 * D_CONF])
        u0_ref[...] = u0
        buf[CONF_HALO:CONF_HALO + tl, :] = u0
        _shift_copies(buf, shifts, tl + CONF_HALO - SUBLANES)

        def strip(rb, carry):
            i0 = pl.multiple_of(rb * CONV_RS, CONV_RS)
            for c0 in range(0, D_CONF, CONV_CS):
                acc = jnp.broadcast_to(cb_ref[:, c0:c0 + CONV_CS], (CONV_RS, CONV_CS))
                for k in range(CONF_K):
                    acc = acc + _tap_rows(buf, shifts, CONF_HALO - (CONF_K - 1) + k, i0, c0) * cw_ref[k:k + 1, c0:c0 + CONV_CS]
                u1_ref[pl.ds(i0, CONV_RS), c0:c0 + CONV_CS] = acc
            return carry

        lax.fori_loop(0, tl // CONV_RS, strip, 0)
        buf[0:CONF_HALO, :] = buf[tl:tl + CONF_HALO, :]
        u1 = u1_ref[...]
        xhat, _ = _ln_stats(u1)
        n = xhat * lg_ref[...] + lb_ref[...]
        cgv = cg_ref[...]
        yc_ref[...] = (n * _sig(n) * (cgv * _sig(cgv))).astype(BF16)

    return pl.pallas_call(
        body, grid=(n_tok // tl,), name="conf_fwd",
        in_specs=[_tile(tl, 2 * D_CONF), _tile(tl, D_CONF), _full((1, 2 * D_CONF)), _full((CONF_HALO, D_CONF)),
                  _full((1, D_CONF)), _full((1, D_CONF)), _full((1, D_CONF))],
        out_specs=[_tile(tl, D_CONF)] * 3,
        out_shape=[jax.ShapeDtypeStruct((n_tok, D_CONF), F32), jax.ShapeDtypeStruct((n_tok, D_CONF), F32),
                   jax.ShapeDtypeStruct((n_tok, D_CONF), BF16)],
        scratch_shapes=[pltpu.VMEM((tl + CONF_HALO, D_CONF), F32),
                        pltpu.VMEM((SUBLANES - 1, tl + CONF_HALO - SUBLANES, D_CONF), F32)],
        compiler_params=_params(),
    )(glu, cgate, bglu, cw, cb, lg, lb)


def _conf_bwd(dyc, dyc_block, u0, u1, glu, cgate, bglu, cw, lg, lb, tl):
    n_tok = glu.shape[0]
    n_t = n_tok // tl

    def body(dyc_ref, u0_ref, u1_ref, glu_ref, cg_ref, bglu_ref, cw_ref, lg_ref, lb_ref,
             dglu_ref, dcg_ref, dcw_ref, dbglu_ref, small_ref, buf, shifts, du0_s, dwp):
        @pl.when(pl.program_id(0) == 0)
        def _():
            dwp[...] = jnp.zeros_like(dwp)
            dbglu_ref[...] = jnp.zeros_like(dbglu_ref)
            small_ref[...] = jnp.zeros_like(small_ref)
            buf[tl:tl + CONF_HALO, :] = jnp.zeros((CONF_HALO, D_CONF), F32)

        xhat, rstd = _ln_stats(u1_ref[...])
        n = xhat * lg_ref[...] + lb_ref[...]
        sn = _sig(n)
        cgv = cg_ref[...]
        scg = _sig(cgv)
        dycv = dyc_ref[...]
        dcg_ref[...] = dycv * (n * sn) * _dsilu(cgv, scg)
        dn = dycv * (cgv * scg) * _dsilu(n, sn)
        small_ref[0:1, :] += _colsum(dn * xhat)
        small_ref[1:2, :] += _colsum(dn)
        du1 = _ln_bwd(dn * lg_ref[...], xhat, rstd)
        small_ref[2:3, :] += _colsum(du1)
        buf[0:tl, :] = du1
        _shift_copies(buf, shifts, tl + CONF_HALO - SUBLANES)

        def strip(rb, carry):
            i0 = pl.multiple_of(rb * CONV_RS, CONV_RS)
            for c0 in range(0, D_CONF, CONV_CS):
                u0s = u0_ref[pl.ds(i0, CONV_RS), c0:c0 + CONV_CS]
                acc = jnp.zeros((CONV_RS, CONV_CS), F32)
                for k in range(CONF_K):
                    sh = _tap_rows(buf, shifts, CONF_K - 1 - k, i0, c0)
                    acc = acc + sh * cw_ref[k:k + 1, c0:c0 + CONV_CS]
                    t = u0s * sh
                    part = t[0:SUBLANES]
                    for j in range(1, CONV_RS // SUBLANES):
                        part = part + t[j * SUBLANES:(j + 1) * SUBLANES]
                    dwp[k * SUBLANES:(k + 1) * SUBLANES, c0:c0 + CONV_CS] += part
                du0_s[pl.ds(i0, CONV_RS), c0:c0 + CONV_CS] = acc
            return carry

        lax.fori_loop(0, tl // CONV_RS, strip, 0)
        du0 = du0_s[...]
        buf[tl:tl + CONF_HALO, :] = buf[0:CONF_HALO, :]
        gl = glu_ref[...] + bglu_ref[...]
        sg = _sig(gl[:, D_CONF:2 * D_CONF])
        dgv = du0 * sg
        dgg = du0 * gl[:, 0:D_CONF] * sg * (1.0 - sg)
        dglu_ref[:, 0:D_CONF] = dgv
        dglu_ref[:, D_CONF:2 * D_CONF] = dgg
        dbglu_ref[:, 0:D_CONF] += _colsum(dgv)
        dbglu_ref[:, D_CONF:2 * D_CONF] += _colsum(dgg)

        @pl.when(pl.program_id(0) == n_t - 1)
        def _():
            for k in range(CONF_HALO):
                dcw_ref[k:k + 1, :] = _colsum(dwp[k * SUBLANES:(k + 1) * SUBLANES, :])

    rev = functools.partial(_tile, tl, rev_of=n_t)
    return pl.pallas_call(
        body, grid=(n_t,), name="conf_bwd",
        in_specs=[pl.BlockSpec((tl, D_CONF), lambda i: (n_t - 1 - i, dyc_block)),
                  rev(D_CONF), rev(D_CONF), rev(2 * D_CONF), rev(D_CONF), _full((1, 2 * D_CONF)),
                  _full((CONF_HALO, D_CONF)), _full((1, D_CONF)), _full((1, D_CONF))],
        out_specs=[rev(2 * D_CONF), rev(D_CONF), _full((CONF_HALO, D_CONF)), _full((1, 2 * D_CONF)), _full((SUBLANES, D_CONF))],
        out_shape=[jax.ShapeDtypeStruct((n_tok, 2 * D_CONF), F32), jax.ShapeDtypeStruct((n_tok, D_CONF), F32),
                   jax.ShapeDtypeStruct((CONF_HALO, D_CONF), F32), jax.ShapeDtypeStruct((1, 2 * D_CONF), F32),
                   jax.ShapeDtypeStruct((SUBLANES, D_CONF), F32)],
        scratch_shapes=[pltpu.VMEM((tl + CONF_HALO, D_CONF), F32),
                        pltpu.VMEM((SUBLANES - 1, tl + CONF_HALO - SUBLANES, D_CONF), F32),
                        pltpu.VMEM((tl, D_CONF), F32), pltpu.VMEM((CONF_HALO * SUBLANES, D_CONF), F32)],
        compiler_params=_params(),
    )(dyc, u0, u1, glu, cgate, bglu, cw, lg, lb)


def _tail(x, yssm, yconf, p, tgt, vec, w_out, wpg, wpp, tl):
    n_tok = x.shape[0]

    def body(x_ref, ys_ref, yc_ref, p_ref, t_ref, vec_ref, wo_ref, wg_ref, wp_ref,
             dmix_ref, dr1_ref, dr1b_ref, h1b_ref, dgb_ref, dpb_ref, small_ref, loss_ref):
        @pl.when(pl.program_id(0) == 0)
        def _():
            small_ref[...] = jnp.zeros_like(small_ref)
            loss_ref[...] = jnp.zeros_like(loss_ref)

        xh0, _ = _ln_stats(x_ref[...])
        h0 = xh0 * vec_ref[0:1, :] + vec_ref[1:2, :]
        out = _dot(ys_ref[...], wo_ref[0:D_SSM, :]) + _dot(yc_ref[...], wo_ref[D_SSM:D_SSM + D_CONF, :]) + vec_ref[2:3, :]
        xh1, rstd1 = _ln_stats(ALPHA * h0 + out)
        h1 = xh1 * vec_ref[3:4, :] + vec_ref[4:5, :]
        h1b = h1.astype(BF16)
        h1b_ref[...] = h1b
        gate = _sig(_dot(h1b, wg_ref[...]))
        ple = _dot(p_ref[...].astype(BF16), wp_ref[...])
        xh2, rstd2 = _ln_stats(ALPHA * h1 + gate * ple)
        h2 = xh2 * vec_ref[5:6, :] + vec_ref[6:7, :]
        diff = h2 - t_ref[...]
        part = jnp.sum(jnp.sum(diff * diff, axis=1, keepdims=True), axis=0, keepdims=True) * (0.5 / D_MODEL)
        loss_ref[...] += jnp.broadcast_to(part, loss_ref.shape)
        dh2 = diff * (1.0 / D_MODEL)
        small_ref[3:4, :] += _colsum(dh2 * xh2)
        small_ref[4:5, :] += _colsum(dh2)
        dr2 = _ln_bwd(dh2 * vec_ref[5:6, :], xh2, rstd2)
        dgpre = (dr2 * ple * gate * (1.0 - gate)).astype(BF16)
        dgb_ref[...] = dgpre
        dpb_ref[...] = (dr2 * gate).astype(BF16)
        dh1 = ALPHA * dr2 + _dot_nt(dgpre, wg_ref[...])
        small_ref[1:2, :] += _colsum(dh1 * xh1)
        small_ref[2:3, :] += _colsum(dh1)
        dr1 = _ln_bwd(dh1 * vec_ref[3:4, :], xh1, rstd1)
        small_ref[0:1, :] += _colsum(dr1)
        dr1_ref[...] = dr1
        dr1b = dr1.astype(BF16)
        dr1b_ref[...] = dr1b
        dmix_ref[...] = _dot_nt(dr1b, wo_ref[...])

    d_mix = D_SSM + D_CONF
    return pl.pallas_call(
        body, grid=(n_tok // tl,), name="tail",
        in_specs=[_tile(tl, D_MODEL), _tile(tl, D_SSM), _tile(tl, D_CONF), _tile(tl, D_PLE), _tile(tl, D_MODEL),
                  _full((SUBLANES, D_MODEL)), _full((d_mix, D_MODEL), True), _full((D_MODEL, D_MODEL), True),
                  _full((D_PLE, D_MODEL), True)],
        out_specs=[_tile(tl, d_mix), _tile(tl, D_MODEL), _tile(tl, D_MODEL), _tile(tl, D_MODEL), _tile(tl, D_MODEL),
                   _tile(tl, D_MODEL), _full((SUBLANES, D_MODEL)), _full((SUBLANES, LANES))],
        out_shape=[jax.ShapeDtypeStruct((n_tok, d_mix), F32), jax.ShapeDtypeStruct((n_tok, D_MODEL), F32),
                   jax.ShapeDtypeStruct((n_tok, D_MODEL), BF16), jax.ShapeDtypeStruct((n_tok, D_MODEL), BF16),
                   jax.ShapeDtypeStruct((n_tok, D_MODEL), BF16), jax.ShapeDtypeStruct((n_tok, D_MODEL), BF16),
                   jax.ShapeDtypeStruct((SUBLANES, D_MODEL), F32), jax.ShapeDtypeStruct((SUBLANES, LANES), F32)],
        compiler_params=_params(),
    )(x, yssm, yconf, p, tgt, vec, w_out, wpg, wpp)


def _inproj_bwd(dxin, dz, dglu, dcg, ddtr, dr1, x, g, b, w_r, tl):
    n_tok = x.shape[0]

    def body(dxin_ref, dz_ref, dglu_ref, dcg_ref, ddtr_ref, dr1_ref, x_ref, g_ref, b_ref, w_ref,
             dx_ref, dpb_ref, small_ref):
        @pl.when(pl.program_id(0) == 0)
        def _():
            small_ref[...] = jnp.zeros_like(small_ref)

        dh0 = ALPHA * dr1_ref[...]
        for ref, lo, hi in ((dxin_ref, R_XBC, R_Z), (dz_ref, R_Z, R_GLU), (dglu_ref, R_GLU, R_CG), (dcg_ref, R_CG, R_DT),
                            (ddtr_ref, R_DT, D_INR)):
            piece = ref[...].astype(BF16)
            dpb_ref[:, lo:hi] = piece
            dh0 = dh0 + _dot_nt(piece, w_ref[:, lo:hi])
        xhat, rstd = _ln_stats(x_ref[...])
        small_ref[0:1, :] += _colsum(dh0 * xhat)
        small_ref[1:2, :] += _colsum(dh0)
        dx_ref[...] = _ln_bwd(dh0 * g_ref[...], xhat, rstd)

    return pl.pallas_call(
        body, grid=(n_tok // tl,), name="inproj_bwd",
        in_specs=[_tile(tl, D_XBC), _tile(tl, D_SSM), _tile(tl, 2 * D_CONF), _tile(tl, D_CONF), _tile(tl, LANES),
                  _tile(tl, D_MODEL), _tile(tl, D_MODEL), _full((1, D_MODEL)), _full((1, D_MODEL)),
                  _full((D_MODEL, D_INR), True)],
        out_specs=[_tile(tl, D_MODEL), _tile(tl, D_INR), _full((SUBLANES, D_MODEL))],
        out_shape=[jax.ShapeDtypeStruct((n_tok, D_MODEL), F32), jax.ShapeDtypeStruct((n_tok, D_INR), BF16),
                   jax.ShapeDtypeStruct((SUBLANES, D_MODEL), F32)],
        compiler_params=_params(),
    )(dxin, dz, dglu, dcg, ddtr, dr1, x, g, b, w_r)


def _tn_matmul(a, b, name, tn, tl):
    n_tok, m = a.shape
    n = b.shape[1]

    def body(a_ref, b_ref, o_ref):
        @pl.when(pl.program_id(1) == 0)
        def _():
            o_ref[...] = jnp.zeros_like(o_ref)

        o_ref[...] += lax.dot_general(a_ref[...], b_ref[...], TN_DIMS, preferred_element_type=F32)

    return pl.pallas_call(
        body, grid=(n // tn, n_tok // tl), name=name,
        in_specs=[pl.BlockSpec((tl, m), lambda j, l: (l, 0)), pl.BlockSpec((tl, tn), lambda j, l: (l, j))],
        out_specs=pl.BlockSpec((m, tn), lambda j, l: (0, j)),
        out_shape=jax.ShapeDtypeStruct((m, n), F32),
        compiler_params=pltpu.CompilerParams(dimension_semantics=("parallel", "arbitrary"), vmem_limit_bytes=VMEM_LIMIT),
    )(a, b)


def _pad_rows(a, rows):
    return jnp.pad(a, ((0, rows - a.shape[0]), (0, 0)))


def _pad_lanes(a):
    return jnp.pad(a, ((0, 0), (0, LANES - a.shape[1])))


def _local_grads(x, p, tgt, w_r, w_out_b, wpg_b, wpp_b, ssm_cw, conf_cw, sm):
    n_tok = x.shape[0]
    tl = min(256, n_tok)
    row = lambda v: v.reshape(1, -1)
    g_e, b_e = row(sm["ln_emb_g"]), row(sm["ln_emb_b"])
    h0b, xbc_in, z, glu, cgate, dtr = _ln_inproj(x, g_e, b_e, w_r, tl)

    cw4 = _pad_rows(ssm_cw, SUBLANES)
    dtb, alog = _pad_lanes(sm["dt_bias"]), _pad_lanes(sm["a_log"])
    dsk = jnp.repeat(sm["d_skip"], HEAD_DIM, axis=1)
    pre, y, yssm, hprev = _ssd_fwd(xbc_in, z, dtr, cw4, sm["ssm_conv_b"], dtb, alog, dsk, sm["ssm_norm_g"], tl)

    cw31 = _pad_rows(conf_cw, CONF_HALO)
    u0, u1, yconf = _conf_fwd(glu, cgate, sm["b_glu"], cw31, sm["conf_conv_b"], sm["conf_ln_g"], sm["conf_ln_b"], tl)

    vec = jnp.concatenate([g_e, b_e, sm["b_out"], sm["ln1_g"], sm["ln1_b"], sm["ln2_g"], sm["ln2_b"],
                           jnp.zeros((1, D_MODEL), F32)], axis=0)
    dmix, dr1, dr1b, h1b, dgb, dpb, small_t, loss = _tail(
        x, yssm, yconf, p, tgt, vec, w_out_b, wpg_b, wpp_b, tl)

    dglu, dcg, dcw31, dbglu, small_c = _conf_bwd(dmix, D_SSM // D_CONF, u0, u1, glu, cgate, sm["b_glu"], cw31,
                                                  sm["conf_ln_g"], sm["conf_ln_b"], tl)
    dxin, dz, ddtr, dcw4, dcb4, dgn, ddsk, dalog, ddtb = _ssd_bwd(
        dmix, y, z, pre, xbc_in, dtr, hprev, cw4, dtb, alog, dsk, sm["ssm_norm_g"], tl)
    dx, dprojb, small_e = _inproj_bwd(dxin, dz, dglu, dcg, ddtr, dr1, x, g_e, b_e, w_r, tl)

    tlm = min(512, n_tok)
    dw_r = _tn_matmul(h0b, dprojb, "dw_in", D_INR // 3, tlm)
    dw_out = jnp.concatenate([_tn_matmul(yssm, dr1b, "dw_out_ssm", D_MODEL, tlm),
                              _tn_matmul(yconf, dr1b, "dw_out_conf", D_MODEL, tlm)], axis=0)
    dwpg = _tn_matmul(h1b, dgb, "dw_ple_gate", D_MODEL, tlm)
    dwpp = _tn_matmul(p.astype(BF16), dpb, "dw_ple_proj", D_MODEL, tlm)
    grads = dict(
        ln_emb_g=small_e[0], ln_emb_b=small_e[1], w_in=dw_r, ssm_conv_w=dcw4[0:SSM_K], ssm_conv_b=dcb4,
        dt_bias=ddtb[:, 0:N_HEADS], a_log=dalog[:, 0:N_HEADS], d_skip=ddsk[:, 0:N_HEADS], ssm_norm_g=dgn, b_glu=dbglu,
        conf_conv_w=dcw31[0:CONF_K], conf_conv_b=small_c[2:3], conf_ln_g=small_c[0:1], conf_ln_b=small_c[1:2],
        w_out=dw_out, b_out=small_t[0:1], ln1_g=small_t[1:2], ln1_b=small_t[2:3], w_ple_gate=dwpg, w_ple_proj=dwpp,
        ln2_g=small_t[3:4], ln2_b=small_t[4:5])
    return loss[0, 0], dx, grads


N_CHIPS = 4
N_DEV = 8
W_IN_SH = D_IN // N_CHIPS
BIG = (("w_in", D_MODEL, W_IN_SH), ("w_out", (D_SSM + D_CONF) // N_CHIPS, D_MODEL),
       ("w_ple_gate", D_MODEL // N_CHIPS, D_MODEL), ("w_ple_proj", D_PLE, D_MODEL // N_CHIPS))
SEGS = ((R_XBC, 0, D_SSM), (R_XBC + D_SSM, 2048, 256), (R_XBC + D_SSM + 256, 2304, 256), (R_Z, 1024, D_SSM),
        (R_GLU, 2576, 2 * D_CONF), (R_CG, 4624, D_CONF), (R_DT, 2560, N_HEADS))
ROWS_CW4 = 2
ROWS_CW31 = 8
ROWS_CONV = 16
SMALL_ROWS = 56
SMALL_LAYOUT = (("ln_emb_g", 0, 1), ("ln_emb_b", 1, 1), ("ssm_conv_b", 2, 2), ("dt_bias", 4, 1), ("a_log", 5, 1),
                ("d_skip", 6, 1), ("ssm_norm_g", 7, 1), ("b_glu", 8, 2), ("conf_conv_b", 10, 1), ("conf_ln_g", 11, 1),
                ("conf_ln_b", 12, 1), ("b_out", 13, 1), ("ln1_g", 14, 1), ("ln1_b", 15, 1), ("ln2_g", 16, 1), ("ln2_b", 17, 1))
CONV_LAYOUT = (("ssm_conv_w", 18, 6, (SSM_K, D_XBC)), ("conf_conv_w", 24, 31, (CONF_K, D_CONF)))


def _rows_of(v, rows):
    flat = v.reshape(-1)
    return jnp.pad(flat, (0, rows * D_MODEL - flat.shape[0])).reshape(rows, D_MODEL)


LOSS_ROW = 55


def _pack_small(d, loss_share=None):
    parts = [_rows_of(d[n], r) for n, _, r in SMALL_LAYOUT]
    for n, _, r, _ in CONV_LAYOUT:
        parts.append(_rows_of(d[n], r) if n in d else jnp.zeros((r, D_MODEL), F32))
    parts.append(_rows_of(jnp.zeros((1,), F32) if loss_share is None else loss_share, SMALL_ROWS - LOSS_ROW))
    return jnp.concatenate(parts, axis=0)


def _unpack_small(a, like):
    return {n: a[r0:r0 + r].reshape(-1)[:like[n].size].reshape(like[n].shape) for n, r0, r in SMALL_LAYOUT}


def _w_r_from_shards(w4):
    parts = []
    for _, o, wd in SEGS:
        lo, hi = o, o + wd
        while lo < hi:
            s = lo // W_IN_SH
            e = min(hi, (s + 1) * W_IN_SH)
            parts.append(w4[s][:, lo - s * W_IN_SH:e - s * W_IN_SH])
            lo = e
    parts.append(jnp.zeros((D_MODEL, LANES - N_HEADS), w4.dtype))
    return jnp.concatenate(parts, axis=1)


def _shards_from_dw_r(dw_r):
    slabs = []
    for s in range(N_CHIPS):
        lo, hi = s * W_IN_SH, (s + 1) * W_IN_SH
        parts = []
        for kcol, o, wd in sorted(SEGS, key=lambda t: t[1]):
            a, b = max(lo, o), min(hi, o + wd)
            if a < b:
                parts.append(dw_r[:, kcol + a - o:kcol + b - o])
        slabs.append(jnp.concatenate(parts, axis=1))
    return jnp.stack(slabs)


def _row_chunks(rows, n):
    return [(j * (rows // n), rows // n) for j in range(n)]


def _my_place():
    return lax.axis_index("x"), lax.axis_index("y"), lax.axis_index("c")


MESH_ID = pl.DeviceIdType.MESH
ANY = pl.BlockSpec(memory_space=pl.ANY)
IN_VMEM = pl.BlockSpec(memory_space=pltpu.VMEM)
CHIP_FLIPS = ((1, 0), (0, 1), (1, 1))


def _remote(src, dst, send_sem, recv_sem, peer):
    return pltpu.make_async_remote_copy(src, dst, send_sem, recv_sem, device_id=peer, device_id_type=MESH_ID)


GATHER_CHUNKS = (4, 2, 1, 1)


def _gather_weights(shards, conv_f):
    n_big = len(BIG)
    plan = [(a, o, n, rows // 2) for a, (_, rows, _) in enumerate(BIG) for o, n in _row_chunks(rows // 2, GATHER_CHUNKS[a])]
    own_plan = [(a, o, n) for a, (_, rows, _) in enumerate(BIG) for o, n in _row_chunks(rows, 2 * GATHER_CHUNKS[a])]
    own_plan.append((n_big, 0, ROWS_CONV))

    def body(*refs):
        ins, outs = refs[0:n_big + 1], refs[n_big + 1:2 * n_big + 2]
        send_a, recv_a, send_b, recv_b, loc = refs[2 * n_big + 2:]
        x, y, c = _my_place()
        s = 2 * x + y
        sibling = (x, y, 1 - c)
        own = [pltpu.make_async_copy(ins[a].at[pl.ds(o, n)], outs[a].at[s, pl.ds(o, n)], loc.at[j])
               for j, (a, o, n) in enumerate(own_plan)]
        first = []
        for k, (fx, fy) in enumerate(CHIP_FLIPS):
            peer = (x ^ fx, y ^ fy, c)
            for j, (a, o, n, h) in enumerate(plan):
                r0 = pl.multiple_of(c * h + o, 16)
                first.append(_remote(ins[a].at[pl.ds(r0, n)], outs[a].at[s, pl.ds(r0, n)], send_a.at[k, j], recv_a.at[k, j], peer))
            first.append(_remote(ins[n_big], outs[n_big].at[s], send_a.at[k, len(plan)], recv_a.at[k, len(plan)], peer))
        for cp in own + first:
            cp.start()
        passed = []
        for k, (fx, fy) in enumerate(CHIP_FLIPS):
            peer = (x ^ fx, y ^ fy, c)
            sk = 2 * (x ^ fx) + (y ^ fy)
            for j, (a, o, n, h) in enumerate(plan):
                land = outs[a].at[sk, pl.ds(pl.multiple_of(c * h + o, 16), n)]
                _remote(land, land, send_a.at[k, j], recv_a.at[k, j], peer).wait_recv()
                cp = _remote(land, land, send_b.at[k, j], recv_b.at[k, j], sibling)
                cp.start()
                passed.append(cp)
            land = outs[n_big].at[sk]
            _remote(land, land, send_a.at[k, len(plan)], recv_a.at[k, len(plan)], peer).wait_recv()
        for k, (fx, fy) in enumerate(CHIP_FLIPS):
            sk = 2 * (x ^ fx) + (y ^ fy)
            for j, (a, o, n, h) in enumerate(plan):
                land = outs[a].at[sk, pl.ds(pl.multiple_of((1 - c) * h + o, 16), n)]
                _remote(land, land, send_b.at[k, j], recv_b.at[k, j], sibling).wait_recv()
        for cp in first + passed:
            cp.wait_send()
        for cp in own:
            cp.wait()

    arrays = list(shards) + [conv_f]
    return pl.pallas_call(
        body, name="gather_weights", in_specs=[IN_VMEM] * len(arrays), out_specs=[ANY] * len(arrays),
        out_shape=[jax.ShapeDtypeStruct((N_CHIPS,) + a.shape, a.dtype) for a in arrays],
        scratch_shapes=[pltpu.SemaphoreType.DMA((3, len(plan) + 1)), pltpu.SemaphoreType.DMA((3, len(plan) + 1)),
                        pltpu.SemaphoreType.DMA((3, len(plan))), pltpu.SemaphoreType.DMA((3, len(plan))),
                        pltpu.SemaphoreType.DMA((len(own_plan),))],
    )(*arrays)


CORES_CHUNKS = (4, 2, 1, 1)
CHIPS_CHUNKS = (2, 1, 1, 1)
SHARE_CHUNKS = (8, 4, 2, 1)


def _exchange_cores(gbig, spack):
    n_big = len(BIG)
    plan = [(a, t, o, n, rows // 2) for a, (_, rows, _) in enumerate(BIG) for t in range(N_CHIPS)
            for o, n in _row_chunks(rows // 2, CORES_CHUNKS[a])]

    def body(*refs):
        g_refs, s_ref = refs[0:n_big], refs[n_big]
        theirs, small_all = refs[n_big + 1:2 * n_big + 1], refs[2 * n_big + 1]
        send_sems, recv_sems, send_small, recv_small, loc_sem = refs[2 * n_big + 2:]
        x, y, c = _my_place()
        me = 4 * x + 2 * y + c
        own = pltpu.make_async_copy(s_ref, small_all.at[me], loc_sem)
        sends = []
        for j, (a, t, o, n, h) in enumerate(plan):
            give = pl.multiple_of((1 - c) * h + o, SUBLANES)
            sends.append(_remote(g_refs[a].at[t, pl.ds(give, n)], theirs[a].at[t, pl.ds(o, n)], send_sems.at[j], recv_sems.at[j],
                                 (x, y, 1 - c)))
        for m in range(1, N_DEV):
            peer = (x ^ (m >> 2), y ^ ((m >> 1) & 1), c ^ (m & 1))
            sends.append(_remote(s_ref, small_all.at[me], send_small.at[m - 1], recv_small.at[m - 1], peer))
        own.start()
        for cp in sends:
            cp.start()
        for cp in sends:
            cp.wait()
        own.wait()

    return pl.pallas_call(
        body, name="exchange_cores", in_specs=[ANY] * n_big + [IN_VMEM], out_specs=[ANY] * (n_big + 1),
        out_shape=[jax.ShapeDtypeStruct((N_CHIPS, rows // 2, cols), F32) for _, rows, cols in BIG]
        + [jax.ShapeDtypeStruct((N_DEV, SMALL_ROWS, D_MODEL), F32)],
        scratch_shapes=[pltpu.SemaphoreType.DMA((len(plan),)), pltpu.SemaphoreType.DMA((len(plan),)),
                        pltpu.SemaphoreType.DMA((N_DEV - 1,)), pltpu.SemaphoreType.DMA((N_DEV - 1,)), pltpu.SemaphoreType.DMA],
    )(*gbig, spack)


def _exchange_chips(psums):
    n_big = len(BIG)
    plan = [(a, o, n) for a, (_, rows, _) in enumerate(BIG) for o, n in _row_chunks(rows // 2, CHIPS_CHUNKS[a])]

    def body(*refs):
        p_refs, got = refs[0:n_big], refs[n_big:2 * n_big]
        send_sems, recv_sems = refs[2 * n_big:]
        x, y, c = _my_place()
        sends = []
        for k, (fx, fy) in enumerate(CHIP_FLIPS):
            tx, ty = x ^ fx, y ^ fy
            for j, (a, o, n) in enumerate(plan):
                sends.append(_remote(p_refs[a].at[2 * tx + ty, pl.ds(o, n)], got[a].at[k, pl.ds(o, n)], send_sems.at[k, j],
                                     recv_sems.at[k, j], (tx, ty, c)))
        for cp in sends:
            cp.start()
        for cp in sends:
            cp.wait()

    return pl.pallas_call(
        body, name="exchange_chips", in_specs=[ANY] * n_big, out_specs=[ANY] * n_big,
        out_shape=[jax.ShapeDtypeStruct((3, rows // 2, cols), BF16) for _, rows, cols in BIG],
        scratch_shapes=[pltpu.SemaphoreType.DMA((3, len(plan))), pltpu.SemaphoreType.DMA((3, len(plan)))],
    )(*psums)


def _share_halves(tots):
    n_big = len(BIG)
    plan = [(a, o, n) for a, (_, rows, _) in enumerate(BIG) for o, n in _row_chunks(rows // 2, SHARE_CHUNKS[a])]

    def body(*refs):
        t_refs, both = refs[0:n_big], refs[n_big:2 * n_big]
        send_sems, recv_sems, loc = refs[2 * n_big:]
        x, y, c = _my_place()
        own = [pltpu.make_async_copy(t_refs[a].at[pl.ds(o, n)], both[a].at[c, pl.ds(o, n)], loc.at[j])
               for j, (a, o, n) in enumerate(plan)]
        sends = [_remote(t_refs[a].at[pl.ds(o, n)], both[a].at[c, pl.ds(o, n)], send_sems.at[j], recv_sems.at[j], (x, y, 1 - c))
                 for j, (a, o, n) in enumerate(plan)]
        for cp in own + sends:
            cp.start()
        for cp in sends:
            cp.wait()
        for cp in own:
            cp.wait()

    return pl.pallas_call(
        body, name="share_halves", in_specs=[IN_VMEM] * n_big, out_specs=[ANY] * n_big,
        out_shape=[jax.ShapeDtypeStruct((2, rows // 2, cols), F32) for _, rows, cols in BIG],
        scratch_shapes=[pltpu.SemaphoreType.DMA((len(plan),)), pltpu.SemaphoreType.DMA((len(plan),)),
                        pltpu.SemaphoreType.DMA((len(plan),))],
    )(*tots)


def _sum_rows(half):
    return half if half <= 256 else 256


def _add_cores(core, g, theirs, name):
    _, rows, cols = g.shape
    half = rows // 2
    tr = _sum_rows(half)
    nb = half // tr

    def body(c_ref, g_ref, t_ref, o_ref):
        o_ref[...] = (g_ref[...] + t_ref[...]).astype(BF16)

    grid_spec = pltpu.PrefetchScalarGridSpec(
        num_scalar_prefetch=1, grid=(N_CHIPS, nb),
        in_specs=[pl.BlockSpec((1, tr, cols), lambda t, i, c_ref: (t, c_ref[0] * nb + i, 0)),
                  pl.BlockSpec((1, tr, cols), lambda t, i, c_ref: (t, i, 0))],
        out_specs=pl.BlockSpec((1, tr, cols), lambda t, i, c_ref: (t, i, 0)))
    return pl.pallas_call(body, grid_spec=grid_spec, name=name, out_shape=jax.ShapeDtypeStruct((N_CHIPS, half, cols), BF16),
                          compiler_params=pltpu.CompilerParams(vmem_limit_bytes=VMEM_LIMIT))(core, g, theirs)


def _add_chips(chip, psum, got, name):
    _, half, cols = psum.shape
    tr = _sum_rows(half)

    def body(s_ref, p_ref, g_ref, o_ref):
        o_ref[...] = ((p_ref[0].astype(F32) + g_ref[0].astype(F32)) + g_ref[1].astype(F32)) + g_ref[2].astype(F32)

    grid_spec = pltpu.PrefetchScalarGridSpec(
        num_scalar_prefetch=1, grid=(half // tr,),
        in_specs=[pl.BlockSpec((1, tr, cols), lambda i, s_ref: (s_ref[0], i, 0)),
                  pl.BlockSpec((3, tr, cols), lambda i, s_ref: (0, i, 0))],
        out_specs=pl.BlockSpec((tr, cols), lambda i, s_ref: (i, 0)))
    return pl.pallas_call(body, grid_spec=grid_spec, name=name, out_shape=jax.ShapeDtypeStruct((half, cols), F32),
                          compiler_params=pltpu.CompilerParams(vmem_limit_bytes=VMEM_LIMIT))(chip, psum, got)


def _adam_math(w, g, m, v):
    m = ADAM_B1 * m + (1.0 - ADAM_B1) * g
    v = ADAM_B2 * v + (1.0 - ADAM_B2) * (g * g)
    m_hat = m / (1.0 - ADAM_B1 ** ADAM_STEP)
    v_hat = v / (1.0 - ADAM_B2 ** ADAM_STEP)
    return -ADAM_LR * (m_hat / (jnp.sqrt(v_hat) + ADAM_EPS) + ADAM_WD * w), m, v


def _adam(w, g, m, v, name):
    rows, cols = w.shape
    tr = rows if rows <= 256 else 256

    def body(w_ref, g_ref, m_ref, v_ref, d_ref, nm_ref, nv_ref):
        d_ref[...], nm_ref[...], nv_ref[...] = _adam_math(w_ref[...], g_ref[...], m_ref[...], v_ref[...])

    spec = pl.BlockSpec((tr, cols), lambda i: (i, 0))
    return pl.pallas_call(body, grid=(rows // tr,), name=name, in_specs=[spec] * 4, out_specs=[spec] * 3,
                          out_shape=[jax.ShapeDtypeStruct(w.shape, F32)] * 3, compiler_params=_params(seq=False))(w, g, m, v)


def _adam_small(w, parts, m, v):
    def body(w_ref, p_ref, m_ref, v_ref, g_ref, d_ref, nm_ref, nv_ref):
        g = p_ref[0]
        for k in range(1, N_DEV):
            g = g + p_ref[k]
        g_ref[...] = g
        d_ref[...], nm_ref[...], nv_ref[...] = _adam_math(w_ref[...], g, m_ref[...], v_ref[...])

    return pl.pallas_call(body, name="adam_small", out_shape=[jax.ShapeDtypeStruct(w.shape, F32)] * 4)(w, parts, m, v)


def kernel(x, p, ln_emb_g, ln_emb_b, w_in, ssm_conv_w, ssm_conv_b, dt_bias, a_log, d_skip, ssm_norm_g, b_glu, conf_conv_w, conf_conv_b, conf_ln_g, conf_ln_b, w_out, b_out, ln1_g, ln1_b, w_ple_gate, w_ple_proj, ln2_g, ln2_b, loss_target, m_ln_emb_g, m_ln_emb_b, m_w_in, m_ssm_conv_w, m_ssm_conv_b, m_dt_bias, m_a_log, m_d_skip, m_ssm_norm_g, m_b_glu, m_conf_conv_w, m_conf_conv_b, m_conf_ln_g, m_conf_ln_b, m_w_out, m_b_out, m_ln1_g, m_ln1_b, m_w_ple_gate, m_w_ple_proj, m_ln2_g, m_ln2_b, v_ln_emb_g, v_ln_emb_b, v_w_in, v_ssm_conv_w, v_ssm_conv_b, v_dt_bias, v_a_log, v_d_skip, v_ssm_norm_g, v_b_glu, v_conf_conv_w, v_conf_conv_b, v_conf_ln_g, v_conf_ln_b, v_w_out, v_b_out, v_ln1_g, v_ln1_b, v_w_ple_gate, v_w_ple_proj, v_ln2_g, v_ln2_b):
    order = ("ln_emb_g", "ln_emb_b", "w_in", "ssm_conv_w", "ssm_conv_b", "dt_bias", "a_log", "d_skip", "ssm_norm_g", "b_glu",
             "conf_conv_w", "conf_conv_b", "conf_ln_g", "conf_ln_b", "w_out", "b_out", "ln1_g", "ln1_b", "w_ple_gate",
             "w_ple_proj", "ln2_g", "ln2_b")
    w = dict(zip(order, (ln_emb_g, ln_emb_b, w_in, ssm_conv_w, ssm_conv_b, dt_bias, a_log, d_skip, ssm_norm_g, b_glu,
                         conf_conv_w, conf_conv_b, conf_ln_g, conf_ln_b, w_out, b_out, ln1_g, ln1_b, w_ple_gate, w_ple_proj,
                         ln2_g, ln2_b)))
    m = dict(zip(order, (m_ln_emb_g, m_ln_emb_b, m_w_in, m_ssm_conv_w, m_ssm_conv_b, m_dt_bias, m_a_log, m_d_skip,
                         m_ssm_norm_g, m_b_glu, m_conf_conv_w, m_conf_conv_b, m_conf_ln_g, m_conf_ln_b, m_w_out, m_b_out,
                         m_ln1_g, m_ln1_b, m_w_ple_gate, m_w_ple_proj, m_ln2_g, m_ln2_b)))
    v = dict(zip(order, (v_ln_emb_g, v_ln_emb_b, v_w_in, v_ssm_conv_w, v_ssm_conv_b, v_dt_bias, v_a_log, v_d_skip,
                         v_ssm_norm_g, v_b_glu, v_conf_conv_w, v_conf_conv_b, v_conf_ln_g, v_conf_ln_b, v_w_out, v_b_out,
                         v_ln1_g, v_ln1_b, v_w_ple_gate, v_w_ple_proj, v_ln2_g, v_ln2_b)))

    conv_f = jnp.concatenate([_rows_of(w["ssm_conv_w"], ROWS_CW4), _rows_of(w["conf_conv_w"], ROWS_CW31),
                              jnp.zeros((ROWS_CONV - ROWS_CW4 - ROWS_CW31, D_MODEL), F32)], axis=0)
    w_in_all, w_out_all, wpg_all, wpp_all, conv_all = _gather_weights([w[n][0].astype(BF16) for n, _, _ in BIG], conv_f)
    w_r = _w_r_from_shards(w_in_all)
    w_out_b = w_out_all.reshape(D_SSM + D_CONF, D_MODEL)
    wpg_b = wpg_all.reshape(D_MODEL, D_MODEL)
    wpp_b = wpp_all.transpose(1, 0, 2).reshape(D_PLE, D_MODEL)
    cw4 = conv_all[:, 0:ROWS_CW4].reshape(N_CHIPS, -1)[:, :SSM_K * D_XBC // N_CHIPS]
    cw4 = cw4.reshape(N_CHIPS, SSM_K, D_XBC // N_CHIPS).transpose(1, 0, 2).reshape(SSM_K, D_XBC)
    cw31 = conv_all[:, ROWS_CW4:ROWS_CW4 + ROWS_CW31].reshape(N_CHIPS, -1)[:, :CONF_K * D_CONF // N_CHIPS]
    cw31 = cw31.reshape(N_CHIPS, CONF_K, D_CONF // N_CHIPS).transpose(1, 0, 2).reshape(CONF_K, D_CONF)

    small_names = [n for n, _, _ in SMALL_LAYOUT]
    sm = {n: w[n] for n in small_names}
    loss_part, dx, grads = _local_grads(x[0], p[0, 0], loss_target[0], w_r, w_out_b, wpg_b, wpp_b, cw4, cw31, sm)

    core = lax.axis_index("c").astype(jnp.int32).reshape(1)
    chip_i = 2 * lax.axis_index("x") + lax.axis_index("y")
    chip = chip_i.astype(jnp.int32).reshape(1)
    gbig = [_shards_from_dw_r(grads["w_in"]), grads["w_out"].reshape(N_CHIPS, -1, D_MODEL),
            grads["w_ple_gate"].reshape(N_CHIPS, -1, D_MODEL),
            grads["w_ple_proj"].reshape(D_PLE, N_CHIPS, D_MODEL // N_CHIPS).transpose(1, 0, 2)]
    spack = _pack_small({n: grads[n] for n in small_names + [n for n, _, _, _ in CONV_LAYOUT]}, loss_part)
    *theirs, small_all = _exchange_cores(gbig, spack)
    psums = [_add_cores(core, g, t, "add_cores_" + n) for g, t, (n, _, _) in zip(gbig, theirs, BIG)]
    got = _exchange_chips(psums)
    tots = [_add_chips(chip, ps, gt, "add_chips_" + n) for ps, gt, (n, _, _) in zip(psums, got, BIG)]
    both = _share_halves(tots)

    out_g, out_d, out_m, out_v = {}, {}, {}, {}
    g_s, d_s, m_s, v_s = _adam_small(_pack_small(sm), small_all, _pack_small({n: m[n] for n in small_names}),
                                     _pack_small({n: v[n] for n in small_names}))
    loss = g_s[LOSS_ROW, 0]
    g_shard = {n: b.reshape(rows, cols) for b, (n, rows, cols) in zip(both, BIG)}
    for n, r0, r, shape in CONV_LAYOUT:
        whole = g_s[r0:r0 + r].reshape(-1)[:shape[0] * shape[1]].reshape(shape)
        g_shard[n] = lax.dynamic_slice_in_dim(whole, chip_i * (shape[1] // N_CHIPS), shape[1] // N_CHIPS, axis=1)
    for n, g in g_shard.items():
        shape = w[n].shape
        d, nm, nv = _adam(w[n][0], g, m[n][0], v[n][0], "adam_" + n)
        out_g[n], out_d[n], out_m[n], out_v[n] = g[None], d[None], nm[None], nv[None]
    for dst, src in ((out_g, g_s), (out_d, d_s), (out_m, m_s), (out_v, v_s)):
        dst.update(_unpack_small(src, sm))
    return (loss, dx[None], *[out_g[n] for n in order], *[out_d[n] for n in order], *[out_m[n] for n in order],
            *[out_v[n] for n in order])
```

```python
import functools

import jax
import jax.numpy as jnp
from jax import lax
from jax.experimental import pallas as pl
from jax.experimental.pallas import tpu as pltpu

F32 = jnp.float32
BF16 = jnp.bfloat16

D_MODEL = 1024
D_PLE = 256
D_SSM = 1024
D_CONF = 1024
N_HEADS = 16
HEAD_DIM = 64
N_GROUPS = 2
N_STATE = 128
CHUNK = 128
SSM_K = 4
CONF_K = 31
D_XBC = D_SSM + 2 * N_GROUPS * N_STATE
D_IN = 5648
R_XBC, R_Z, R_GLU, R_CG, R_DT, D_INR = 0, 1536, 2560, 4608, 5632, 5760
LN_EPS = 1e-5
RMS_EPS = 1e-5
ALPHA = 2.0 ** 0.25
ADAM_LR, ADAM_B1, ADAM_B2, ADAM_EPS, ADAM_WD, ADAM_STEP = 0.001, 0.9, 0.999, 1e-08, 0.01, 10
NEG_BIG = -1e30
LANES = 128
SUBLANES = 8
VMEM_LIMIT = 56 * 1024 * 1024
HIGHEST = lax.Precision.HIGHEST
NT_DIMS = (((1,), (1,)), ((), ()))
TN_DIMS = (((0,), (0,)), ((), ()))


def _sig(v):
    return jax.nn.sigmoid(v)


def _dsilu(v, s):
    return s * (1.0 + v * (1.0 - s))


def _ln_stats(v):
    mu = jnp.mean(v, axis=-1, keepdims=True)
    c = v - mu
    var = jnp.mean(c * c, axis=-1, keepdims=True)
    rstd = lax.rsqrt(var + LN_EPS)
    return c * rstd, rstd


def _ln_bwd(dxhat, xhat, rstd):
    m1 = jnp.mean(dxhat, axis=-1, keepdims=True)
    m2 = jnp.mean(dxhat * xhat, axis=-1, keepdims=True)
    return rstd * (dxhat - m1 - xhat * m2)


def _softplus(v):
    return jnp.maximum(v, 0.0) + jnp.log1p(jnp.exp(-jnp.abs(v)))


def _colsum(v):
    return jnp.sum(v, axis=0, keepdims=True)


def _dot(a, b):
    return jnp.dot(a, b, preferred_element_type=F32)


def _dot_nt(a, b):
    return lax.dot_general(a, b, NT_DIMS, preferred_element_type=F32)


def _tile(tl, c, rev_of=None):
    if rev_of is None:
        return pl.BlockSpec((tl, c), lambda i: (i, 0))
    return pl.BlockSpec((tl, c), lambda i: (rev_of - 1 - i, 0))


def _full(shape, single=False):
    nd = len(shape)
    if single:
        return pl.BlockSpec(shape, lambda i: (0,) * nd, pipeline_mode=pl.Buffered(1))
    return pl.BlockSpec(shape, lambda i: (0,) * nd)


def _params(seq=True):
    return pltpu.CompilerParams(dimension_semantics=("arbitrary",) if seq else ("parallel",), vmem_limit_bytes=VMEM_LIMIT)


def _ln_inproj(x, g, b, w_r, later, tl):
    n_tok = x.shape[0]
    n_t = n_tok // tl
    n_l = len(later)
    plan, own_plan = _gather_plans([(rows, GATHER_CHUNKS[a]) for a, (_, rows, _) in enumerate(BIG)][1:])

    def body(x_ref, g_ref, b_ref, w_ref, *rest):
        part_refs = rest[0:n_l]
        h0b_ref, xbc_ref, z_ref, glu_ref, cg_ref, dtr_ref = rest[n_l:n_l + 6]
        all_refs, sems = rest[n_l + 6:2 * n_l + 6], rest[2 * n_l + 6:]
        i = pl.program_id(0)

        @pl.when(i == 0)
        def _():
            _gather_stage(0, part_refs, all_refs, plan, own_plan, sems)

        xhat, _ = _ln_stats(x_ref[...])
        hb = (xhat * g_ref[...] + b_ref[...]).astype(BF16)
        h0b_ref[...] = hb
        xbc_ref[...] = _dot(hb, w_ref[:, R_XBC:R_Z])
        z_ref[...] = _dot(hb, w_ref[:, R_Z:R_GLU])
        glu_ref[...] = _dot(hb, w_ref[:, R_GLU:R_CG])
        cg_ref[...] = _dot(hb, w_ref[:, R_CG:R_DT])
        dtr_ref[...] = _dot(hb, w_ref[:, R_DT:D_INR])

        @pl.when(i == (3 * n_t) // 4)
        def _():
            _gather_stage(1, part_refs, all_refs, plan, own_plan, sems)

        @pl.when(i == n_t - 1)
        def _():
            _gather_stage(2, part_refs, all_refs, plan, own_plan, sems)

    widths = (D_MODEL, D_XBC, D_SSM, 2 * D_CONF, D_CONF, LANES)
    dtypes = (BF16, F32, F32, F32, F32, F32)
    return pl.pallas_call(
        body, grid=(n_t,), name="ln_inproj",
        in_specs=[_tile(tl, D_MODEL), _full((1, D_MODEL)), _full((1, D_MODEL)), _full((D_MODEL, D_INR), single=True)]
        + [IN_VMEM] * n_l,
        out_specs=[_tile(tl, w) for w in widths] + [ANY] * n_l,
        out_shape=[jax.ShapeDtypeStruct((n_tok, w), dt) for w, dt in zip(widths, dtypes)]
        + [jax.ShapeDtypeStruct((N_CHIPS,) + a.shape, a.dtype) for a in later],
        scratch_shapes=_gather_sems(plan, own_plan),
        compiler_params=_params(),
    )(x, g, b, w_r, *later)


def _chunk_common(adt_c):
    row = lax.broadcasted_iota(jnp.int32, (CHUNK, CHUNK), 0)
    col = lax.broadcasted_iota(jnp.int32, (CHUNK, CHUNK), 1)
    tril = row >= col
    acs = jnp.dot(tril.astype(F32), adt_c, precision=HIGHEST, preferred_element_type=F32)
    last = acs[CHUNK - 1:CHUNK, :]
    return dict(row=row, col=col, tril=tril, lo=col < HEAD_DIM, acs=acs, acs_t=acs.T, e=jnp.exp(acs),
                dec=jnp.exp(last - acs), cd=jnp.exp(last))


def _decay_mask(cm, h):
    return jnp.exp(jnp.where(cm["tril"], cm["acs"][:, h:h + 1] - cm["acs_t"][h:h + 1, :], NEG_BIG))


def _head_lane_matrix():
    return (jnp.arange(D_SSM)[None, :] // HEAD_DIM == jnp.arange(LANES)[:, None]).astype(BF16)


def _per_head_lanes(v, exp_ref):
    hi = v.astype(BF16)
    lo = (v - hi.astype(F32)).astype(BF16)
    return _dot(hi, exp_ref[...]) + _dot(lo, exp_ref[...])


SSM_BWD_SHIFTS = (1, 2, 3)


def _ssd_fwd(xbc_in, z, dtr, cw, cb, dtb, alog, dsk, gnorm, tl):
    n_tok = xbc_in.shape[0]
    nq = tl // CHUNK

    def body(xin_ref, z_ref, dtr_ref, cw_ref, cb_ref, dtb_ref, alog_ref, dsk_ref, gn_ref, exp_ref,
             pre_ref, y_ref, yssm_ref, hprev_ref, buf, hst):
        @pl.when(pl.program_id(0) == 0)
        def _():
            buf[0:SUBLANES, :] = jnp.zeros((SUBLANES, D_XBC), F32)
            hst[...] = jnp.zeros_like(hst)

        buf[SUBLANES:SUBLANES + tl, :] = xin_ref[...]
        pre = cb_ref[...] + jnp.zeros((tl, D_XBC), F32)
        for k in range(SSM_K):
            off = SUBLANES - (SSM_K - 1) + k
            pre = pre + buf[off:off + tl, :] * cw_ref[k:k + 1, :]
        buf[0:SUBLANES, :] = buf[tl:tl + SUBLANES, :]
        pre_ref[...] = pre
        xbc = pre * _sig(pre)
        dt = _softplus(dtr_ref[...] + dtb_ref[...])
        a = -jnp.exp(alog_ref[...])
        adt = dt * a
        for q in range(nq):
            r0 = q * CHUNK
            cm = _chunk_common(adt[r0:r0 + CHUNK, :])
            dt_x = _per_head_lanes(dt[r0:r0 + CHUNK, :], exp_ref)
            e_x = _per_head_lanes(cm["e"], exp_ref)
            dec_x = _per_head_lanes(cm["dec"], exp_ref)
            for g in range(N_GROUPS):
                bg = xbc[r0:r0 + CHUNK, D_SSM + g * N_STATE:D_SSM + (g + 1) * N_STATE].astype(BF16)
                cg_ = xbc[r0:r0 + CHUNK, D_SSM + (N_GROUPS + g) * N_STATE:D_SSM + (N_GROUPS + g + 1) * N_STATE].astype(BF16)
                gm = _dot_nt(cg_, bg)
                for k in range(N_HEADS // N_GROUPS // 2):
                    ha = (N_HEADS // N_GROUPS) * g + 2 * k
                    c0 = ha * HEAD_DIM
                    xh2 = xbc[r0:r0 + CHUNK, c0:c0 + LANES]
                    x2 = xh2 * dt_x[:, c0:c0 + LANES]
                    x2b = x2.astype(BF16)
                    ya = _dot((gm * _decay_mask(cm, ha)).astype(BF16), x2b)
                    yb = _dot((gm * _decay_mask(cm, ha + 1)).astype(BF16), x2b)
                    h2 = hst[c0:c0 + LANES, :]
                    hprev_ref[q, c0:c0 + LANES, :] = h2
                    z2 = _dot_nt(cg_, h2.astype(BF16))
                    y2 = jnp.where(cm["lo"], ya, yb) + z2 * e_x[:, c0:c0 + LANES] + dsk_ref[:, c0:c0 + LANES] * xh2
                    y_ref[r0:r0 + CHUNK, c0:c0 + LANES] = y2
                    s2 = _dot((x2 * dec_x[:, c0:c0 + LANES]).T.astype(BF16), bg)
                    cd2 = jnp.where(cm["row"] < HEAD_DIM, cm["cd"][:, ha:ha + 1], cm["cd"][:, ha + 1:ha + 2])
                    hst[c0:c0 + LANES, :] = cd2 * h2 + s2
        yv = y_ref[...]
        zv = z_ref[...]
        yz = yv * (zv * _sig(zv))
        gw = D_SSM // N_GROUPS
        for g in range(N_GROUPS):
            seg = yz[:, g * gw:(g + 1) * gw]
            r = lax.rsqrt(jnp.mean(seg * seg, axis=-1, keepdims=True) + RMS_EPS)
            yssm_ref[:, g * gw:(g + 1) * gw] = (seg * r * gn_ref[:, g * gw:(g + 1) * gw]).astype(BF16)

    return pl.pallas_call(
        body, grid=(n_tok // tl,), name="ssd_fwd",
        in_specs=[_tile(tl, D_XBC), _tile(tl, D_SSM), _tile(tl, LANES), _full((SUBLANES, D_XBC)), _full((1, D_XBC)),
                  _full((1, LANES)), _full((1, LANES)), _full((1, D_SSM)), _full((1, D_SSM)), _full((LANES, D_SSM))],
        out_specs=[_tile(tl, D_XBC), _tile(tl, D_SSM), _tile(tl, D_SSM),
                   pl.BlockSpec((nq, D_SSM, N_STATE), lambda i: (i, 0, 0))],
        out_shape=[jax.ShapeDtypeStruct((n_tok, D_XBC), F32), jax.ShapeDtypeStruct((n_tok, D_SSM), F32),
                   jax.ShapeDtypeStruct((n_tok, D_SSM), BF16), jax.ShapeDtypeStruct((n_tok // CHUNK, D_SSM, N_STATE), F32)],
        scratch_shapes=[pltpu.VMEM((tl + SUBLANES, D_XBC), F32), pltpu.VMEM((D_SSM, N_STATE), F32)],
        compiler_params=_params(),
    )(xbc_in, z, dtr, cw, cb, dtb, alog, dsk, gnorm, _head_lane_matrix())


def _ssd_bwd(dys, y, z, pre, xbc_in, dtr, hprev, cw, dtb, alog, dsk, gnorm, tl):
    n_tok = y.shape[0]
    n_t = n_tok // tl
    nq = tl // CHUNK

    def body(dys_ref, y_ref, z_ref, pre_ref, xin_ref, dtr_ref, hprev_ref, cw_ref, dtb_ref, alog_ref, dsk_ref, gn_ref,
             exp_ref, redx_ref, redq_ref,
             dxin_ref, dz_ref, ddtr_ref, dcw_ref, dcb_ref, dgn_ref, ddsk_ref, da_ref, ddtb_ref,
             dxs, dh, cs_s, dskc, shifts, dwp, sums_q, sums_s):
        i = pl.program_id(0)

        @pl.when(i == 0)
        def _():
            dcw_ref[...] = jnp.zeros_like(dcw_ref)
            dwp[...] = jnp.zeros_like(dwp)
            dcb_ref[...] = jnp.zeros_like(dcb_ref)
            dgn_ref[...] = jnp.zeros_like(dgn_ref)
            da_ref[...] = jnp.zeros_like(da_ref)
            ddtb_ref[...] = jnp.zeros_like(ddtb_ref)
            dskc[...] = jnp.zeros_like(dskc)
            dh[...] = jnp.zeros_like(dh)
            dxs[tl:tl + SUBLANES, :] = jnp.zeros((SUBLANES, D_XBC), F32)

        yv = y_ref[...]
        zv = z_ref[...]
        dysv = dys_ref[...]
        sz = _sig(zv)
        silz = zv * sz
        yz = yv * silz
        gw = D_SSM // N_GROUPS
        dyz_parts = []
        for g in range(N_GROUPS):
            sl = slice(g * gw, (g + 1) * gw)
            seg = yz[:, sl]
            r = lax.rsqrt(jnp.mean(seg * seg, axis=-1, keepdims=True) + RMS_EPS)
            yzn = seg * r
            dgn_ref[:, sl] += _colsum(dysv[:, sl] * yzn)
            dyzn = dysv[:, sl] * gn_ref[:, sl]
            dyz_parts.append(r * (dyzn - yzn * jnp.mean(dyzn * yzn, axis=-1, keepdims=True)))
        dyz = jnp.concatenate(dyz_parts, axis=1)
        dy = dyz * silz
        dz_ref[...] = dyz * yv * _dsilu(zv, sz)

        prev = pre_ref[...]
        sp = _sig(prev)
        xbc = prev * sp
        dskc[...] += _colsum(dy * xbc[:, 0:D_SSM])
        dt_in = dtr_ref[...] + dtb_ref[...]
        dt = _softplus(dt_in)
        dsp = _sig(dt_in)
        a = -jnp.exp(alog_ref[...])
        adt = dt * a
        for q in reversed(range(nq)):
            r0 = q * CHUNK
            cm = _chunk_common(adt[r0:r0 + CHUNK, :])
            row, col, lo = cm["row"], cm["col"], cm["lo"]
            triu = (col >= row).astype(F32)
            dt_c = dt[r0:r0 + CHUNK, :]
            dt_x = _per_head_lanes(dt_c, exp_ref)
            e_x = _per_head_lanes(cm["e"], exp_ref)
            dec_x = _per_head_lanes(cm["dec"], exp_ref)
            dcd_row = jnp.zeros((1, LANES), F32)
            for g in range(N_GROUPS):
                bcol = D_SSM + g * N_STATE
                ccol = D_SSM + (N_GROUPS + g) * N_STATE
                bg = xbc[r0:r0 + CHUNK, bcol:bcol + N_STATE].astype(BF16)
                cg_ = xbc[r0:r0 + CHUNK, ccol:ccol + N_STATE].astype(BF16)
                gm = _dot_nt(cg_, bg)
                dgm = jnp.zeros((CHUNK, CHUNK), F32)
                dbg = jnp.zeros((CHUNK, N_STATE), F32)
                dcg = jnp.zeros((CHUNK, N_STATE), F32)
                for k in range(N_HEADS // N_GROUPS // 2):
                    ha = (N_HEADS // N_GROUPS) * g + 2 * k
                    hb = ha + 1
                    c0 = ha * HEAD_DIM
                    xh2 = xbc[r0:r0 + CHUNK, c0:c0 + LANES]
                    dt2 = dt_x[:, c0:c0 + LANES]
                    x2 = xh2 * dt2
                    x2b = x2.astype(BF16)
                    la = _decay_mask(cm, ha)
                    lb = _decay_mask(cm, hb)
                    ma = gm * la
                    mb = gm * lb
                    dy2 = dy[r0:r0 + CHUNK, c0:c0 + LANES]
                    dy2b = dy2.astype(BF16)
                    dma = _dot_nt(jnp.where(lo, dy2, 0.0).astype(BF16), x2b)
                    dmb = _dot_nt(jnp.where(lo, 0.0, dy2).astype(BF16), x2b)
                    dx2 = jnp.where(lo, _dot(ma.T.astype(BF16), dy2b), _dot(mb.T.astype(BF16), dy2b))
                    dgm = dgm + dma * la + dmb * lb
                    sums_q[:, ha * CHUNK:(ha + 1) * CHUNK] = (dma * ma).astype(BF16)
                    sums_q[:, hb * CHUNK:(hb + 1) * CHUNK] = (dmb * mb).astype(BF16)
                    h2 = hprev_ref[q, c0:c0 + LANES, :]
                    h2b = h2.astype(BF16)
                    dz2 = dy2 * e_x[:, c0:c0 + LANES]
                    dcg = dcg + _dot(dz2.astype(BF16), h2b)
                    sums_s[0, :, c0:c0 + LANES] = (dz2 * _dot_nt(cg_, h2b)).astype(BF16)
                    dhn = dh[c0:c0 + LANES, :]
                    dhnb = dhn.astype(BF16)
                    cd_a = cm["cd"][:, ha:ha + 1]
                    cd_b = cm["cd"][:, hb:hb + 1]
                    top = row < HEAD_DIM
                    hh = dhn * h2
                    dcd_a = jnp.sum(_colsum(jnp.where(top, hh, 0.0)), axis=1, keepdims=True)
                    dcd_b = jnp.sum(_colsum(jnp.where(top, 0.0, hh)), axis=1, keepdims=True)
                    dcd_row = dcd_row + jnp.where(col[0:1, :] == ha, dcd_a * cd_a, 0.0) + jnp.where(col[0:1, :] == hb, dcd_b * cd_b, 0.0)
                    dh[c0:c0 + LANES, :] = jnp.where(top, cd_a, cd_b) * dhn + _dot(dz2.T.astype(BF16), cg_)
                    w2 = _dot_nt(bg, dhnb)
                    dec2 = dec_x[:, c0:c0 + LANES]
                    dx2 = dx2 + dec2 * w2
                    sums_s[1, :, c0:c0 + LANES] = (x2 * w2).astype(BF16)
                    dbg = dbg + _dot((x2 * dec2).astype(BF16), dhnb)
                    sums_s[2, :, c0:c0 + LANES] = (dx2 * xh2).astype(BF16)
                    dxs[r0:r0 + CHUNK, c0:c0 + LANES] = dx2 * dt2 + dsk_ref[:, c0:c0 + LANES] * dy2
                dgmb = dgm.astype(BF16)
                dxs[r0:r0 + CHUNK, bcol:bcol + N_STATE] = dbg + _dot(dgm.T.astype(BF16), cg_)
                dxs[r0:r0 + CHUNK, ccol:ccol + N_STATE] = dcg + _dot(dgmb, bg)
            q_all = sums_q[...]
            q_cols = _dot(jnp.ones((SUBLANES, CHUNK), BF16), q_all)
            cs_s[...] = jnp.zeros_like(cs_s)
            for h in range(N_HEADS):
                cs_s[h:h + 1, :] = q_cols[0:1, h * CHUNK:(h + 1) * CHUNK]
            de = _dot(sums_s[0], redx_ref[...])
            dd = _dot(sums_s[1], redx_ref[...]) * cm["dec"]
            ddtx = _dot(sums_s[2], redx_ref[...])
            is_last = row == CHUNK - 1
            dacs = _dot(q_all, redq_ref[...]) - cs_s[...].T + de - dd + jnp.where(is_last, dcd_row + _colsum(dd), 0.0)
            dadt = jnp.dot(triu, dacs, precision=HIGHEST, preferred_element_type=F32)
            da_ref[...] += _colsum(dadt * dt_c)
            ddtr_c = (dadt * a + ddtx) * dsp[r0:r0 + CHUNK, :]
            ddtr_ref[r0:r0 + CHUNK, :] = ddtr_c
            ddtb_ref[...] += _colsum(ddtr_c)

        dpre = dxs[0:tl, :] * _dsilu(prev, sp)
        dxs[0:tl, :] = dpre
        dcb_ref[...] += _colsum(dpre)
        _shift_copies(dxs, shifts, tl, SSM_BWD_SHIFTS)

        def strip(rb, carry):
            i0 = pl.multiple_of(rb * CONV_RS, CONV_RS)
            for c0 in range(0, D_XBC, CONV_CS):
                xin_s = xin_ref[pl.ds(i0, CONV_RS), c0:c0 + CONV_CS]
                acc = jnp.zeros((CONV_RS, CONV_CS), F32)
                for k in range(SSM_K):
                    sh = _tap_rows(dxs, shifts, SSM_K - 1 - k, i0, c0, SSM_BWD_SHIFTS)
                    acc = acc + sh * cw_ref[k:k + 1, c0:c0 + CONV_CS]
                    t = xin_s * sh
                    dwp[k * SUBLANES:(k + 1) * SUBLANES, c0:c0 + CONV_CS] += _fold_rows(t)
                dxin_ref[pl.ds(i0, CONV_RS), c0:c0 + CONV_CS] = acc
            return carry

        lax.fori_loop(0, tl // CONV_RS, strip, 0)
        dxs[tl:tl + SUBLANES, :] = dxs[0:SUBLANES, :]

        @pl.when(i == n_t - 1)
        def _():
            for k in range(SSM_K):
                dcw_ref[k:k + 1, :] = _colsum(dwp[k * SUBLANES:(k + 1) * SUBLANES, :])
            da_ref[...] = da_ref[...] * a
            sel = (lax.broadcasted_iota(jnp.int32, (D_SSM, LANES), 0) // HEAD_DIM
                   == lax.broadcasted_iota(jnp.int32, (D_SSM, LANES), 1)).astype(F32)
            rows = jnp.broadcast_to(dskc[...], (SUBLANES, D_SSM))
            ddsk_ref[...] = jnp.dot(rows, sel, precision=HIGHEST, preferred_element_type=F32)[0:1, :]

    head_lanes = _head_lane_matrix()
    per_head = (jnp.arange(N_HEADS * CHUNK)[:, None] // CHUNK == jnp.arange(LANES)[None, :]).astype(BF16)
    rev = functools.partial(_tile, tl, rev_of=n_t)
    return pl.pallas_call(
        body, grid=(n_t,), name="ssd_bwd",
        in_specs=[rev(D_SSM), rev(D_SSM), rev(D_SSM), rev(D_XBC), rev(D_XBC), rev(LANES),
                  pl.BlockSpec((nq, D_SSM, N_STATE), lambda i: (n_t - 1 - i, 0, 0)),
                  _full((SUBLANES, D_XBC)), _full((1, LANES)), _full((1, LANES)), _full((1, D_SSM)), _full((1, D_SSM)),
                  _full((LANES, D_SSM)), _full((D_SSM, LANES)), _full((N_HEADS * CHUNK, LANES))],
        out_specs=[rev(D_XBC), rev(D_SSM), rev(LANES), _full((SUBLANES, D_XBC)), _full((1, D_XBC)), _full((1, D_SSM)),
                   _full((1, LANES)), _full((1, LANES)), _full((1, LANES))],
        out_shape=[jax.ShapeDtypeStruct((n_tok, D_XBC), F32), jax.ShapeDtypeStruct((n_tok, D_SSM), F32),
                   jax.ShapeDtypeStruct((n_tok, LANES), F32), jax.ShapeDtypeStruct((SUBLANES, D_XBC), F32),
                   jax.ShapeDtypeStruct((1, D_XBC), F32), jax.ShapeDtypeStruct((1, D_SSM), F32),
                   jax.ShapeDtypeStruct((1, LANES), F32), jax.ShapeDtypeStruct((1, LANES), F32),
                   jax.ShapeDtypeStruct((1, LANES), F32)],
        scratch_shapes=[pltpu.VMEM((tl + SUBLANES, D_XBC), F32), pltpu.VMEM((D_SSM, N_STATE), F32),
                        pltpu.VMEM((CHUNK, LANES), F32), pltpu.VMEM((1, D_SSM), F32),
                        pltpu.VMEM((len(SSM_BWD_SHIFTS), tl, D_XBC), F32), pltpu.VMEM((SSM_K * SUBLANES, D_XBC), F32),
                        pltpu.VMEM((CHUNK, N_HEADS * CHUNK), BF16), pltpu.VMEM((3, CHUNK, D_SSM), BF16)],
        compiler_params=_params(),
    )(dys, y, z, pre, xbc_in, dtr, hprev, cw, dtb, alog, dsk, gnorm, head_lanes, head_lanes.T, per_head)


CONF_HALO = 32
CONV_RS = 32
CONV_CS = 256


ALL_SHIFTS = tuple(range(1, SUBLANES))


def _shift_copies(buf, shifts, n_rows, residues=ALL_SHIFTS):
    for j, r in enumerate(residues):
        shifts[j, 0:n_rows, :] = buf[r:r + n_rows, :]


def _fold_rows(t):
    part = t[0:SUBLANES]
    for j in range(1, t.shape[0] // SUBLANES):
        part = part + t[j * SUBLANES:(j + 1) * SUBLANES]
    return part


def _tap_rows(buf, shifts, off, i0, c0, residues=ALL_SHIFTS):
    q, r = divmod(off, SUBLANES)
    rows = pl.ds(pl.multiple_of(i0 + SUBLANES * q, SUBLANES), CONV_RS)
    if r == 0:
        return buf[rows, c0:c0 + CONV_CS]
    return shifts[residues.index(r), rows, c0:c0 + CONV_CS]


def _conf_fwd(glu, cgate, bglu, cw, cb, lg, lb, tl):
    n_tok = glu.shape[0]

    def body(glu_ref, cg_ref, bglu_ref, cw_ref, cb_ref, lg_ref, lb_ref, u0_ref, u1_ref, yc_ref, buf, shifts):
        @pl.when(pl.program_id(0) == 0)
        def _():
            buf[0:CONF_HALO, :] = jnp.zeros((CONF_HALO, D_CONF), F32)

        gl = glu_ref[...] + bglu_ref[...]
        u0 = gl[:, 0:D_CONF] * _sig(gl[:, D_CONF:2 * D_CONF])
        u0_ref[...] = u0
        buf[CONF_HALO:CONF_HALO + tl, :] = u0
        _shift_copies(buf, shifts, tl + CONF_HALO - SUBLANES)

        def strip(rb, carry):
            i0 = pl.multiple_of(rb * CONV_RS, CONV_RS)
            for c0 in range(0, D_CONF, CONV_CS):
                acc = jnp.broadcast_to(cb_ref[:, c0:c0 + CONV_CS], (CONV_RS, CONV_CS))
                for k in range(CONF_K):
                    acc = acc + _tap_rows(buf, shifts, CONF_HALO - (CONF_K - 1) + k, i0, c0) * cw_ref[k:k + 1, c0:c0 + CONV_CS]
                u1_ref[pl.ds(i0, CONV_RS), c0:c0 + CONV_CS] = acc
            return carry

        lax.fori_loop(0, tl // CONV_RS, strip, 0)
        buf[0:CONF_HALO, :] = buf[tl:tl + CONF_HALO, :]
        xhat, _ = _ln_stats(u1_ref[...])
        n = xhat * lg_ref[...] + lb_ref[...]
        cgv = cg_ref[...]
        yc_ref[...] = (n * _sig(n) * (cgv * _sig(cgv))).astype(BF16)

    return pl.pallas_call(
        body, grid=(n_tok // tl,), name="conf_fwd",
        in_specs=[_tile(tl, 2 * D_CONF), _tile(tl, D_CONF), _full((1, 2 * D_CONF)), _full((CONF_HALO, D_CONF)),
                  _full((1, D_CONF)), _full((1, D_CONF)), _full((1, D_CONF))],
        out_specs=[_tile(tl, D_CONF)] * 3,
        out_shape=[jax.ShapeDtypeStruct((n_tok, D_CONF), F32), jax.ShapeDtypeStruct((n_tok, D_CONF), F32),
                   jax.ShapeDtypeStruct((n_tok, D_CONF), BF16)],
        scratch_shapes=[pltpu.VMEM((tl + CONF_HALO, D_CONF), F32),
                        pltpu.VMEM((SUBLANES - 1, tl + CONF_HALO - SUBLANES, D_CONF), F32)],
        compiler_params=_params(),
    )(glu, cgate, bglu, cw, cb, lg, lb)


def _conf_bwd(dyc, dyc_block, u0, u1, glu, cgate, bglu, cw, lg, lb, tl):
    n_tok = glu.shape[0]
    n_t = n_tok // tl

    def body(dyc_ref, u0_ref, u1_ref, glu_ref, cg_ref, bglu_ref, cw_ref, lg_ref, lb_ref,
             dglu_ref, dcg_ref, dcw_ref, dbglu_ref, small_ref, buf, shifts, du0_s, dwp):
        @pl.when(pl.program_id(0) == 0)
        def _():
            dwp[...] = jnp.zeros_like(dwp)
            dbglu_ref[...] = jnp.zeros_like(dbglu_ref)
            small_ref[...] = jnp.zeros_like(small_ref)
            buf[tl:tl + CONF_HALO, :] = jnp.zeros((CONF_HALO, D_CONF), F32)

        xhat, rstd = _ln_stats(u1_ref[...])
        n = xhat * lg_ref[...] + lb_ref[...]
        sn = _sig(n)
        cgv = cg_ref[...]
        scg = _sig(cgv)
        dycv = dyc_ref[...]
        dcg_ref[...] = dycv * (n * sn) * _dsilu(cgv, scg)
        dn = dycv * (cgv * scg) * _dsilu(n, sn)
        small_ref[0:1, :] += _colsum(dn * xhat)
        small_ref[1:2, :] += _colsum(dn)
        du1 = _ln_bwd(dn * lg_ref[...], xhat, rstd)
        small_ref[2:3, :] += _colsum(du1)
        buf[0:tl, :] = du1
        _shift_copies(buf, shifts, tl + CONF_HALO - SUBLANES)

        def strip(rb, carry):
            i0 = pl.multiple_of(rb * CONV_RS, CONV_RS)
            for c0 in range(0, D_CONF, CONV_CS):
                u0s = u0_ref[pl.ds(i0, CONV_RS), c0:c0 + CONV_CS]
                acc = jnp.zeros((CONV_RS, CONV_CS), F32)
                for k in range(CONF_K):
                    sh = _tap_rows(buf, shifts, CONF_K - 1 - k, i0, c0)
                    acc = acc + sh * cw_ref[k:k + 1, c0:c0 + CONV_CS]
                    t = u0s * sh
                    dwp[k * SUBLANES:(k + 1) * SUBLANES, c0:c0 + CONV_CS] += _fold_rows(t)
                du0_s[pl.ds(i0, CONV_RS), c0:c0 + CONV_CS] = acc
            return carry

        lax.fori_loop(0, tl // CONV_RS, strip, 0)
        du0 = du0_s[...]
        buf[tl:tl + CONF_HALO, :] = buf[0:CONF_HALO, :]
        gl = glu_ref[...] + bglu_ref[...]
        sg = _sig(gl[:, D_CONF:2 * D_CONF])
        dgv = du0 * sg
        dgg = du0 * gl[:, 0:D_CONF] * sg * (1.0 - sg)
        dglu_ref[:, 0:D_CONF] = dgv
        dglu_ref[:, D_CONF:2 * D_CONF] = dgg
        dbglu_ref[:, 0:D_CONF] += _colsum(dgv)
        dbglu_ref[:, D_CONF:2 * D_CONF] += _colsum(dgg)

        @pl.when(pl.program_id(0) == n_t - 1)
        def _():
            for k in range(CONF_HALO):
                dcw_ref[k:k + 1, :] = _colsum(dwp[k * SUBLANES:(k + 1) * SUBLANES, :])

    rev = functools.partial(_tile, tl, rev_of=n_t)
    return pl.pallas_call(
        body, grid=(n_t,), name="conf_bwd",
        in_specs=[pl.BlockSpec((tl, D_CONF), lambda i: (n_t - 1 - i, dyc_block)),
                  rev(D_CONF), rev(D_CONF), rev(2 * D_CONF), rev(D_CONF), _full((1, 2 * D_CONF)),
                  _full((CONF_HALO, D_CONF)), _full((1, D_CONF)), _full((1, D_CONF))],
        out_specs=[rev(2 * D_CONF), rev(D_CONF), _full((CONF_HALO, D_CONF)), _full((1, 2 * D_CONF)), _full((SUBLANES, D_CONF))],
        out_shape=[jax.ShapeDtypeStruct((n_tok, 2 * D_CONF), F32), jax.ShapeDtypeStruct((n_tok, D_CONF), F32),
                   jax.ShapeDtypeStruct((CONF_HALO, D_CONF), F32), jax.ShapeDtypeStruct((1, 2 * D_CONF), F32),
                   jax.ShapeDtypeStruct((SUBLANES, D_CONF), F32)],
        scratch_shapes=[pltpu.VMEM((tl + CONF_HALO, D_CONF), F32),
                        pltpu.VMEM((SUBLANES - 1, tl + CONF_HALO - SUBLANES, D_CONF), F32),
                        pltpu.VMEM((tl, D_CONF), F32), pltpu.VMEM((CONF_HALO * SUBLANES, D_CONF), F32)],
        compiler_params=_params(),
    )(dyc, u0, u1, glu, cgate, bglu, cw, lg, lb)


def _tail(x, yssm, yconf, p, tgt, vec, w_out, wpg, wpp, tl):
    n_tok = x.shape[0]

    def body(x_ref, ys_ref, yc_ref, p_ref, t_ref, vec_ref, wo_ref, wg_ref, wp_ref,
             dmix_ref, dr1_ref, dr1b_ref, h1b_ref, dgb_ref, dpb_ref, small_ref, loss_ref):
        @pl.when(pl.program_id(0) == 0)
        def _():
            small_ref[...] = jnp.zeros_like(small_ref)
            loss_ref[...] = jnp.zeros_like(loss_ref)

        xh0, _ = _ln_stats(x_ref[...])
        h0 = xh0 * vec_ref[0:1, :] + vec_ref[1:2, :]
        out = _dot(ys_ref[...], wo_ref[0:D_SSM, :]) + _dot(yc_ref[...], wo_ref[D_SSM:D_SSM + D_CONF, :]) + vec_ref[2:3, :]
        xh1, rstd1 = _ln_stats(ALPHA * h0 + out)
        h1 = xh1 * vec_ref[3:4, :] + vec_ref[4:5, :]
        h1b = h1.astype(BF16)
        h1b_ref[...] = h1b
        gate = _sig(_dot(h1b, wg_ref[...]))
        ple = _dot(p_ref[...].astype(BF16), wp_ref[...])
        xh2, rstd2 = _ln_stats(ALPHA * h1 + gate * ple)
        h2 = xh2 * vec_ref[5:6, :] + vec_ref[6:7, :]
        diff = h2 - t_ref[...]
        part = jnp.sum(jnp.sum(diff * diff, axis=1, keepdims=True), axis=0, keepdims=True) * (0.5 / D_MODEL)
        loss_ref[...] += jnp.broadcast_to(part, loss_ref.shape)
        dh2 = diff * (1.0 / D_MODEL)
        small_ref[3:4, :] += _colsum(dh2 * xh2)
        small_ref[4:5, :] += _colsum(dh2)
        dr2 = _ln_bwd(dh2 * vec_ref[5:6, :], xh2, rstd2)
        dgpre = (dr2 * ple * gate * (1.0 - gate)).astype(BF16)
        dgb_ref[...] = dgpre
        dpb_ref[...] = (dr2 * gate).astype(BF16)
        dh1 = ALPHA * dr2 + _dot_nt(dgpre, wg_ref[...])
        small_ref[1:2, :] += _colsum(dh1 * xh1)
        small_ref[2:3, :] += _colsum(dh1)
        dr1 = _ln_bwd(dh1 * vec_ref[3:4, :], xh1, rstd1)
        small_ref[0:1, :] += _colsum(dr1)
        dr1_ref[...] = dr1
        dr1b = dr1.astype(BF16)
        dr1b_ref[...] = dr1b
        dmix_ref[...] = _dot_nt(dr1b, wo_ref[...])

    d_mix = D_SSM + D_CONF
    return pl.pallas_call(
        body, grid=(n_tok // tl,), name="tail",
        in_specs=[_tile(tl, D_MODEL), _tile(tl, D_SSM), _tile(tl, D_CONF), _tile(tl, D_PLE), _tile(tl, D_MODEL),
                  _full((SUBLANES, D_MODEL)), _full((d_mix, D_MODEL), True), _full((D_MODEL, D_MODEL), True),
                  _full((D_PLE, D_MODEL), True)],
        out_specs=[_tile(tl, d_mix), _tile(tl, D_MODEL), _tile(tl, D_MODEL), _tile(tl, D_MODEL), _tile(tl, D_MODEL),
                   _tile(tl, D_MODEL), _full((SUBLANES, D_MODEL)), _full((SUBLANES, LANES))],
        out_shape=[jax.ShapeDtypeStruct((n_tok, d_mix), F32), jax.ShapeDtypeStruct((n_tok, D_MODEL), F32),
                   jax.ShapeDtypeStruct((n_tok, D_MODEL), BF16), jax.ShapeDtypeStruct((n_tok, D_MODEL), BF16),
                   jax.ShapeDtypeStruct((n_tok, D_MODEL), BF16), jax.ShapeDtypeStruct((n_tok, D_MODEL), BF16),
                   jax.ShapeDtypeStruct((SUBLANES, D_MODEL), F32), jax.ShapeDtypeStruct((SUBLANES, LANES), F32)],
        compiler_params=_params(),
    )(x, yssm, yconf, p, tgt, vec, w_out, wpg, wpp)


def _inproj_bwd(dxin, dz, dglu, dcg, ddtr, dr1, x, g, b, w_r, tl):
    n_tok = x.shape[0]

    def body(dxin_ref, dz_ref, dglu_ref, dcg_ref, ddtr_ref, dr1_ref, x_ref, g_ref, b_ref, w_ref,
             dx_ref, dpb_ref, small_ref):
        @pl.when(pl.program_id(0) == 0)
        def _():
            small_ref[...] = jnp.zeros_like(small_ref)

        dh0 = ALPHA * dr1_ref[...]
        for ref, lo, hi in ((dxin_ref, R_XBC, R_Z), (dz_ref, R_Z, R_GLU), (dglu_ref, R_GLU, R_CG), (dcg_ref, R_CG, R_DT),
                            (ddtr_ref, R_DT, D_INR)):
            piece = ref[...].astype(BF16)
            dpb_ref[:, lo:hi] = piece
            dh0 = dh0 + _dot_nt(piece, w_ref[:, lo:hi])
        xhat, rstd = _ln_stats(x_ref[...])
        small_ref[0:1, :] += _colsum(dh0 * xhat)
        small_ref[1:2, :] += _colsum(dh0)
        dx_ref[...] = _ln_bwd(dh0 * g_ref[...], xhat, rstd)

    return pl.pallas_call(
        body, grid=(n_tok // tl,), name="inproj_bwd",
        in_specs=[_tile(tl, D_XBC), _tile(tl, D_SSM), _tile(tl, 2 * D_CONF), _tile(tl, D_CONF), _tile(tl, LANES),
                  _tile(tl, D_MODEL), _tile(tl, D_MODEL), _full((1, D_MODEL)), _full((1, D_MODEL)),
                  _full((D_MODEL, D_INR), True)],
        out_specs=[_tile(tl, D_MODEL), _tile(tl, D_INR), _full((SUBLANES, D_MODEL))],
        out_shape=[jax.ShapeDtypeStruct((n_tok, D_MODEL), F32), jax.ShapeDtypeStruct((n_tok, D_INR), BF16),
                   jax.ShapeDtypeStruct((SUBLANES, D_MODEL), F32)],
        compiler_params=_params(),
    )(dxin, dz, dglu, dcg, ddtr, dr1, x, g, b, w_r)


def _tn_matmul(a, b, name, tn, tl):
    n_tok, m = a.shape
    n = b.shape[1]

    def body(a_ref, b_ref, o_ref):
        @pl.when(pl.program_id(1) == 0)
        def _():
            o_ref[...] = jnp.zeros_like(o_ref)

        o_ref[...] += lax.dot_general(a_ref[...], b_ref[...], TN_DIMS, preferred_element_type=F32)

    return pl.pallas_call(
        body, grid=(n // tn, n_tok // tl), name=name,
        in_specs=[pl.BlockSpec((tl, m), lambda j, l: (l, 0)), pl.BlockSpec((tl, tn), lambda j, l: (l, j))],
        out_specs=pl.BlockSpec((m, tn), lambda j, l: (0, j)),
        out_shape=jax.ShapeDtypeStruct((m, n), F32),
        compiler_params=pltpu.CompilerParams(dimension_semantics=("parallel", "arbitrary"), vmem_limit_bytes=VMEM_LIMIT),
    )(a, b)


def _pad_rows(a, rows):
    return jnp.pad(a, ((0, rows - a.shape[0]), (0, 0)))


def _pad_lanes(a):
    return jnp.pad(a, ((0, 0), (0, LANES - a.shape[1])))


def _local_grads(x, p, tgt, w_r, later, ssm_cw, conf_cw, sm):
    n_tok = x.shape[0]
    tl = min(256, n_tok)
    tlm = min(512, n_tok)
    row = lambda v: v.reshape(1, -1)
    g_e, b_e = row(sm["ln_emb_g"]), row(sm["ln_emb_b"])
    h0b, xbc_in, z, glu, cgate, dtr, w_out_all, wpg_all, wpp_all = _ln_inproj(x, g_e, b_e, w_r, later, tlm)
    w_out_b = w_out_all.reshape(D_SSM + D_CONF, D_MODEL)
    wpg_b = wpg_all.reshape(D_MODEL, D_MODEL)
    wpp_b = wpp_all.transpose(1, 0, 2).reshape(D_PLE, D_MODEL)

    cw4 = _pad_rows(ssm_cw, SUBLANES)
    dtb, alog = _pad_lanes(sm["dt_bias"]), _pad_lanes(sm["a_log"])
    dsk = jnp.repeat(sm["d_skip"], HEAD_DIM, axis=1)
    pre, y, yssm, hprev = _ssd_fwd(xbc_in, z, dtr, cw4, sm["ssm_conv_b"], dtb, alog, dsk, sm["ssm_norm_g"], tl)

    cw31 = _pad_rows(conf_cw, CONF_HALO)
    u0, u1, yconf = _conf_fwd(glu, cgate, sm["b_glu"], cw31, sm["conf_conv_b"], sm["conf_ln_g"], sm["conf_ln_b"], tl)

    vec = jnp.concatenate([g_e, b_e, sm["b_out"], sm["ln1_g"], sm["ln1_b"], sm["ln2_g"], sm["ln2_b"],
                           jnp.zeros((1, D_MODEL), F32)], axis=0)
    dmix, dr1, dr1b, h1b, dgb, dpb, small_t, loss = _tail(
        x, yssm, yconf, p, tgt, vec, w_out_b, wpg_b, wpp_b, tlm)

    dglu, dcg, dcw31, dbglu, small_c = _conf_bwd(dmix, D_SSM // D_CONF, u0, u1, glu, cgate, sm["b_glu"], cw31,
                                                  sm["conf_ln_g"], sm["conf_ln_b"], tl)
    dxin, dz, ddtr, dcw4, dcb4, dgn, ddsk, dalog, ddtb = _ssd_bwd(
        dmix, y, z, pre, xbc_in, dtr, hprev, cw4, dtb, alog, dsk, sm["ssm_norm_g"], tl)
    dx, dprojb, small_e = _inproj_bwd(dxin, dz, dglu, dcg, ddtr, dr1, x, g_e, b_e, w_r, tl)

    dw_r = _tn_matmul(h0b, dprojb, "dw_in", D_INR // 3, tlm)
    dw_out = jnp.concatenate([_tn_matmul(yssm, dr1b, "dw_out_ssm", D_MODEL, tlm),
                              _tn_matmul(yconf, dr1b, "dw_out_conf", D_MODEL, tlm)], axis=0)
    dwpg = _tn_matmul(h1b, dgb, "dw_ple_gate", D_MODEL, tlm)
    dwpp = _tn_matmul(p.astype(BF16), dpb, "dw_ple_proj", D_MODEL, tlm)
    grads = dict(
        ln_emb_g=small_e[0], ln_emb_b=small_e[1], w_in=dw_r, ssm_conv_w=dcw4[0:SSM_K], ssm_conv_b=dcb4,
        dt_bias=ddtb[:, 0:N_HEADS], a_log=dalog[:, 0:N_HEADS], d_skip=ddsk[:, 0:N_HEADS], ssm_norm_g=dgn, b_glu=dbglu,
        conf_conv_w=dcw31[0:CONF_K], conf_conv_b=small_c[2:3], conf_ln_g=small_c[0:1], conf_ln_b=small_c[1:2],
        w_out=dw_out, b_out=small_t[0:1], ln1_g=small_t[1:2], ln1_b=small_t[2:3], w_ple_gate=dwpg, w_ple_proj=dwpp,
        ln2_g=small_t[3:4], ln2_b=small_t[4:5])
    return loss[0, 0], dx, grads


N_CHIPS = 4
N_DEV = 8
W_IN_SH = D_IN // N_CHIPS
BIG = (("w_in", D_MODEL, W_IN_SH), ("w_out", (D_SSM + D_CONF) // N_CHIPS, D_MODEL),
       ("w_ple_gate", D_MODEL // N_CHIPS, D_MODEL), ("w_ple_proj", D_PLE, D_MODEL // N_CHIPS))
SEGS = ((R_XBC, 0, D_SSM), (R_XBC + D_SSM, 2048, 256), (R_XBC + D_SSM + 256, 2304, 256), (R_Z, 1024, D_SSM),
        (R_GLU, 2576, 2 * D_CONF), (R_CG, 4624, D_CONF), (R_DT, 2560, N_HEADS))
ROWS_CW4 = 2
ROWS_CW31 = 8
ROWS_CONV = 16
SMALL_ROWS = 56
SMALL_LAYOUT = (("ln_emb_g", 0, 1), ("ln_emb_b", 1, 1), ("ssm_conv_b", 2, 2), ("dt_bias", 4, 1), ("a_log", 5, 1),
                ("d_skip", 6, 1), ("ssm_norm_g", 7, 1), ("b_glu", 8, 2), ("conf_conv_b", 10, 1), ("conf_ln_g", 11, 1),
                ("conf_ln_b", 12, 1), ("b_out", 13, 1), ("ln1_g", 14, 1), ("ln1_b", 15, 1), ("ln2_g", 16, 1), ("ln2_b", 17, 1))
CONV_LAYOUT = (("ssm_conv_w", 18, 6, (SSM_K, D_XBC)), ("conf_conv_w", 24, 31, (CONF_K, D_CONF)))


def _rows_of(v, rows):
    flat = v.reshape(-1)
    return jnp.pad(flat, (0, rows * D_MODEL - flat.shape[0])).reshape(rows, D_MODEL)


LOSS_ROW = 55


def _pack_small(d, loss_share=None):
    parts = [_rows_of(d[n], r) for n, _, r in SMALL_LAYOUT]
    for n, _, r, _ in CONV_LAYOUT:
        parts.append(_rows_of(d[n], r) if n in d else jnp.zeros((r, D_MODEL), F32))
    parts.append(_rows_of(jnp.zeros((1,), F32) if loss_share is None else loss_share, SMALL_ROWS - LOSS_ROW))
    return jnp.concatenate(parts, axis=0)


def _unpack_small(a, like):
    return {n: a[r0:r0 + r].reshape(-1)[:like[n].size].reshape(like[n].shape) for n, r0, r in SMALL_LAYOUT}


def _w_r_from_shards(w4):
    parts = []
    for _, o, wd in SEGS:
        lo, hi = o, o + wd
        while lo < hi:
            s = lo // W_IN_SH
            e = min(hi, (s + 1) * W_IN_SH)
            parts.append(w4[s][:, lo - s * W_IN_SH:e - s * W_IN_SH])
            lo = e
    parts.append(jnp.zeros((D_MODEL, LANES - N_HEADS), w4.dtype))
    return jnp.concatenate(parts, axis=1)


def _shards_from_dw_r(dw_r):
    slabs = []
    for s in range(N_CHIPS):
        lo, hi = s * W_IN_SH, (s + 1) * W_IN_SH
        parts = []
        for kcol, o, wd in sorted(SEGS, key=lambda t: t[1]):
            a, b = max(lo, o), min(hi, o + wd)
            if a < b:
                parts.append(dw_r[:, kcol + a - o:kcol + b - o])
        slabs.append(jnp.concatenate(parts, axis=1))
    return jnp.stack(slabs)


def _row_chunks(rows, n):
    return [(j * (rows // n), rows // n) for j in range(n)]


def _my_place():
    return lax.axis_index("x"), lax.axis_index("y"), lax.axis_index("c")


MESH_ID = pl.DeviceIdType.MESH
ANY = pl.BlockSpec(memory_space=pl.ANY)
IN_VMEM = pl.BlockSpec(memory_space=pltpu.VMEM)
CHIP_FLIPS = ((1, 0), (0, 1), (1, 1))


def _remote(src, dst, send_sem, recv_sem, peer):
    return pltpu.make_async_remote_copy(src, dst, send_sem, recv_sem, device_id=peer, device_id_type=MESH_ID)


GATHER_CHUNKS = (4, 2, 1, 1)


def _gather_plans(kinds):
    plan = [(a, o, n, rows // 2) for a, (rows, ch) in enumerate(kinds) for o, n in _row_chunks(rows // 2, ch)]
    own_plan = [(a, o, n) for a, (rows, ch) in enumerate(kinds) for o, n in _row_chunks(rows, 2 * ch)]
    return plan, own_plan


def _gather_sems(plan, own_plan):
    hop = pltpu.SemaphoreType.DMA((3, len(plan)))
    return [hop, hop, hop, hop, pltpu.SemaphoreType.DMA((len(own_plan),))]


def _gather_stage(stage, ins, outs, plan, own_plan, sems):
    send_a, recv_a, send_b, recv_b, loc = sems
    x, y, c = _my_place()
    s = 2 * x + y
    sibling = (x, y, 1 - c)
    own = [pltpu.make_async_copy(ins[a].at[pl.ds(o, n)], outs[a].at[s, pl.ds(o, n)], loc.at[j])
           for j, (a, o, n) in enumerate(own_plan)]
    first, arrive, passed, arrive_b = [], [], [], []
    for k, (fx, fy) in enumerate(CHIP_FLIPS):
        peer = (x ^ fx, y ^ fy, c)
        sk = 2 * (x ^ fx) + (y ^ fy)
        for j, (a, o, n, h) in enumerate(plan):
            mine = pl.ds(pl.multiple_of(c * h + o, 16), n)
            theirs = pl.ds(pl.multiple_of((1 - c) * h + o, 16), n)
            first.append(_remote(ins[a].at[mine], outs[a].at[s, mine], send_a.at[k, j], recv_a.at[k, j], peer))
            land = outs[a].at[sk, mine]
            arrive.append(_remote(land, land, send_a.at[k, j], recv_a.at[k, j], peer))
            passed.append(_remote(land, land, send_b.at[k, j], recv_b.at[k, j], sibling))
            land_b = outs[a].at[sk, theirs]
            arrive_b.append(_remote(land_b, land_b, send_b.at[k, j], recv_b.at[k, j], sibling))
    if stage == 0:
        for cp in own + first:
            cp.start()
    elif stage == 1:
        for got, fwd in zip(arrive, passed):
            got.wait_recv()
            fwd.start()
    else:
        for got in arrive_b:
            got.wait_recv()
        for cp in first + passed:
            cp.wait_send()
        for cp in own:
            cp.wait()


def _gather_weights(w_in_b, conv_f):
    plan, own_plan = _gather_plans([(BIG[0][1], GATHER_CHUNKS[0])])

    def body(w_ref, conv_ref, w_all, conv_all, *sems):
        conv_send, conv_recv, conv_loc = sems[5:]
        x, y, c = _my_place()
        s = 2 * x + y
        conv_own = pltpu.make_async_copy(conv_ref, conv_all.at[s], conv_loc)
        conv_out = [_remote(conv_ref, conv_all.at[s], conv_send.at[k], conv_recv.at[k], (x ^ fx, y ^ fy, c))
                    for k, (fx, fy) in enumerate(CHIP_FLIPS)]
        _gather_stage(0, [w_ref], [w_all], plan, own_plan, sems[0:5])
        for cp in [conv_own] + conv_out:
            cp.start()
        _gather_stage(1, [w_ref], [w_all], plan, own_plan, sems[0:5])
        _gather_stage(2, [w_ref], [w_all], plan, own_plan, sems[0:5])
        for cp in conv_out:
            cp.wait()
        conv_own.wait()

    arrays = [w_in_b, conv_f]
    return pl.pallas_call(
        body, name="gather_weights", in_specs=[IN_VMEM] * 2, out_specs=[ANY] * 2,
        out_shape=[jax.ShapeDtypeStruct((N_CHIPS,) + a.shape, a.dtype) for a in arrays],
        scratch_shapes=_gather_sems(plan, own_plan)
        + [pltpu.SemaphoreType.DMA((3,)), pltpu.SemaphoreType.DMA((3,)), pltpu.SemaphoreType.DMA],
    )(*arrays)


CORES_CHUNKS = (4, 2, 1, 1)
CHIPS_CHUNKS = (2, 1, 1, 1)
SHARE_CHUNKS = (8, 4, 2, 1)


def _exchange_cores(gbig, spack):
    n_big = len(BIG)
    plan = [(a, t, o, n, rows // 2) for a, (_, rows, _) in enumerate(BIG) for t in range(N_CHIPS)
            for o, n in _row_chunks(rows // 2, CORES_CHUNKS[a])]

    def body(*refs):
        g_refs, s_ref = refs[0:n_big], refs[n_big]
        theirs, small_all = refs[n_big + 1:2 * n_big + 1], refs[2 * n_big + 1]
        send_sems, recv_sems, send_small, recv_small, loc_sem = refs[2 * n_big + 2:]
        x, y, c = _my_place()
        me = 4 * x + 2 * y + c
        own = pltpu.make_async_copy(s_ref, small_all.at[me], loc_sem)
        sends = []
        for j, (a, t, o, n, h) in enumerate(plan):
            give = pl.multiple_of((1 - c) * h + o, SUBLANES)
            sends.append(_remote(g_refs[a].at[t, pl.ds(give, n)], theirs[a].at[t, pl.ds(o, n)], send_sems.at[j], recv_sems.at[j],
                                 (x, y, 1 - c)))
        for m in range(1, N_DEV):
            peer = (x ^ (m >> 2), y ^ ((m >> 1) & 1), c ^ (m & 1))
            sends.append(_remote(s_ref, small_all.at[me], send_small.at[m - 1], recv_small.at[m - 1], peer))
        own.start()
        for cp in sends:
            cp.start()
        for cp in sends:
            cp.wait()
        own.wait()

    return pl.pallas_call(
        body, name="exchange_cores", in_specs=[ANY] * n_big + [IN_VMEM], out_specs=[ANY] * (n_big + 1),
        out_shape=[jax.ShapeDtypeStruct((N_CHIPS, rows // 2, cols), F32) for _, rows, cols in BIG]
        + [jax.ShapeDtypeStruct((N_DEV, SMALL_ROWS, D_MODEL), F32)],
        scratch_shapes=[pltpu.SemaphoreType.DMA((len(plan),)), pltpu.SemaphoreType.DMA((len(plan),)),
                        pltpu.SemaphoreType.DMA((N_DEV - 1,)), pltpu.SemaphoreType.DMA((N_DEV - 1,)), pltpu.SemaphoreType.DMA],
    )(*gbig, spack)


def _exchange_chips(psums):
    n_big = len(BIG)
    plan = [(a, o, n) for a, (_, rows, _) in enumerate(BIG) for o, n in _row_chunks(rows // 2, CHIPS_CHUNKS[a])]

    def body(*refs):
        p_refs, got = refs[0:n_big], refs[n_big:2 * n_big]
        send_sems, recv_sems = refs[2 * n_big:]
        x, y, c = _my_place()
        sends = []
        for k, (fx, fy) in enumerate(CHIP_FLIPS):
            tx, ty = x ^ fx, y ^ fy
            for j, (a, o, n) in enumerate(plan):
                sends.append(_remote(p_refs[a].at[2 * tx + ty, pl.ds(o, n)], got[a].at[k, pl.ds(o, n)], send_sems.at[k, j],
                                     recv_sems.at[k, j], (tx, ty, c)))
        for cp in sends:
            cp.start()
        for cp in sends:
            cp.wait()

    return pl.pallas_call(
        body, name="exchange_chips", in_specs=[ANY] * n_big, out_specs=[ANY] * n_big,
        out_shape=[jax.ShapeDtypeStruct((3, rows // 2, cols), BF16) for _, rows, cols in BIG],
        scratch_shapes=[pltpu.SemaphoreType.DMA((3, len(plan))), pltpu.SemaphoreType.DMA((3, len(plan)))],
    )(*psums)


def _share_halves(tots):
    n_big = len(BIG)
    plan = [(a, o, n) for a, (_, rows, _) in enumerate(BIG) for o, n in _row_chunks(rows // 2, SHARE_CHUNKS[a])]

    def body(*refs):
        t_refs, both = refs[0:n_big], refs[n_big:2 * n_big]
        send_sems, recv_sems, loc = refs[2 * n_big:]
        x, y, c = _my_place()
        own = [pltpu.make_async_copy(t_refs[a].at[pl.ds(o, n)], both[a].at[c, pl.ds(o, n)], loc.at[j])
               for j, (a, o, n) in enumerate(plan)]
        sends = [_remote(t_refs[a].at[pl.ds(o, n)], both[a].at[c, pl.ds(o, n)], send_sems.at[j], recv_sems.at[j], (x, y, 1 - c))
                 for j, (a, o, n) in enumerate(plan)]
        for cp in own + sends:
            cp.start()
        for cp in sends:
            cp.wait()
        for cp in own:
            cp.wait()

    return pl.pallas_call(
        body, name="share_halves", in_specs=[IN_VMEM] * n_big, out_specs=[ANY] * n_big,
        out_shape=[jax.ShapeDtypeStruct((2, rows // 2, cols), F32) for _, rows, cols in BIG],
        scratch_shapes=[pltpu.SemaphoreType.DMA((len(plan),)), pltpu.SemaphoreType.DMA((len(plan),)),
                        pltpu.SemaphoreType.DMA((len(plan),))],
    )(*tots)


def _sum_rows(half):
    return half if half <= 256 else 256


def _add_cores(core, g, theirs, name):
    _, rows, cols = g.shape
    half = rows // 2
    tr = _sum_rows(half)
    nb = half // tr

    def body(c_ref, g_ref, t_ref, o_ref):
        o_ref[...] = (g_ref[...] + t_ref[...]).astype(BF16)

    grid_spec = pltpu.PrefetchScalarGridSpec(
        num_scalar_prefetch=1, grid=(N_CHIPS, nb),
        in_specs=[pl.BlockSpec((1, tr, cols), lambda t, i, c_ref: (t, c_ref[0] * nb + i, 0)),
                  pl.BlockSpec((1, tr, cols), lambda t, i, c_ref: (t, i, 0))],
        out_specs=pl.BlockSpec((1, tr, cols), lambda t, i, c_ref: (t, i, 0)))
    return pl.pallas_call(body, grid_spec=grid_spec, name=name, out_shape=jax.ShapeDtypeStruct((N_CHIPS, half, cols), BF16),
                          compiler_params=pltpu.CompilerParams(vmem_limit_bytes=VMEM_LIMIT))(core, g, theirs)


def _add_chips(chip, psum, got, name):
    _, half, cols = psum.shape
    tr = _sum_rows(half)

    def body(s_ref, p_ref, g_ref, o_ref):
        o_ref[...] = ((p_ref[0].astype(F32) + g_ref[0].astype(F32)) + g_ref[1].astype(F32)) + g_ref[2].astype(F32)

    grid_spec = pltpu.PrefetchScalarGridSpec(
        num_scalar_prefetch=1, grid=(half // tr,),
        in_specs=[pl.BlockSpec((1, tr, cols), lambda i, s_ref: (s_ref[0], i, 0)),
                  pl.BlockSpec((3, tr, cols), lambda i, s_ref: (0, i, 0))],
        out_specs=pl.BlockSpec((tr, cols), lambda i, s_ref: (i, 0)))
    return pl.pallas_call(body, grid_spec=grid_spec, name=name, out_shape=jax.ShapeDtypeStruct((half, cols), F32),
                          compiler_params=pltpu.CompilerParams(vmem_limit_bytes=VMEM_LIMIT))(chip, psum, got)


def _adam_math(w, g, m, v):
    m = ADAM_B1 * m + (1.0 - ADAM_B1) * g
    v = ADAM_B2 * v + (1.0 - ADAM_B2) * (g * g)
    m_hat = m / (1.0 - ADAM_B1 ** ADAM_STEP)
    v_hat = v / (1.0 - ADAM_B2 ** ADAM_STEP)
    return -ADAM_LR * (m_hat / (jnp.sqrt(v_hat) + ADAM_EPS) + ADAM_WD * w), m, v


def _adam(w, g, m, v, name):
    rows, cols = w.shape
    tr = rows if rows <= 256 else 256

    def body(w_ref, g_ref, m_ref, v_ref, d_ref, nm_ref, nv_ref):
        d_ref[...], nm_ref[...], nv_ref[...] = _adam_math(w_ref[...], g_ref[...], m_ref[...], v_ref[...])

    spec = pl.BlockSpec((tr, cols), lambda i: (i, 0))
    return pl.pallas_call(body, grid=(rows // tr,), name=name, in_specs=[spec] * 4, out_specs=[spec] * 3,
                          out_shape=[jax.ShapeDtypeStruct(w.shape, F32)] * 3, compiler_params=_params(seq=False))(w, g, m, v)


def _adam_small(w, parts, m, v):
    def body(w_ref, p_ref, m_ref, v_ref, g_ref, d_ref, nm_ref, nv_ref):
        g = p_ref[0]
        for k in range(1, N_DEV):
            g = g + p_ref[k]
        g_ref[...] = g
        d_ref[...], nm_ref[...], nv_ref[...] = _adam_math(w_ref[...], g, m_ref[...], v_ref[...])

    return pl.pallas_call(body, name="adam_small", out_shape=[jax.ShapeDtypeStruct(w.shape, F32)] * 4)(w, parts, m, v)


def kernel(x, p, ln_emb_g, ln_emb_b, w_in, ssm_conv_w, ssm_conv_b, dt_bias, a_log, d_skip, ssm_norm_g, b_glu, conf_conv_w, conf_conv_b, conf_ln_g, conf_ln_b, w_out, b_out, ln1_g, ln1_b, w_ple_gate, w_ple_proj, ln2_g, ln2_b, loss_target, m_ln_emb_g, m_ln_emb_b, m_w_in, m_ssm_conv_w, m_ssm_conv_b, m_dt_bias, m_a_log, m_d_skip, m_ssm_norm_g, m_b_glu, m_conf_conv_w, m_conf_conv_b, m_conf_ln_g, m_conf_ln_b, m_w_out, m_b_out, m_ln1_g, m_ln1_b, m_w_ple_gate, m_w_ple_proj, m_ln2_g, m_ln2_b, v_ln_emb_g, v_ln_emb_b, v_w_in, v_ssm_conv_w, v_ssm_conv_b, v_dt_bias, v_a_log, v_d_skip, v_ssm_norm_g, v_b_glu, v_conf_conv_w, v_conf_conv_b, v_conf_ln_g, v_conf_ln_b, v_w_out, v_b_out, v_ln1_g, v_ln1_b, v_w_ple_gate, v_w_ple_proj, v_ln2_g, v_ln2_b):
    order = ("ln_emb_g", "ln_emb_b", "w_in", "ssm_conv_w", "ssm_conv_b", "dt_bias", "a_log", "d_skip", "ssm_norm_g", "b_glu",
             "conf_conv_w", "conf_conv_b", "conf_ln_g", "conf_ln_b", "w_out", "b_out", "ln1_g", "ln1_b", "w_ple_gate",
             "w_ple_proj", "ln2_g", "ln2_b")
    w = dict(zip(order, (ln_emb_g, ln_emb_b, w_in, ssm_conv_w, ssm_conv_b, dt_bias, a_log, d_skip, ssm_norm_g, b_glu,
                         conf_conv_w, conf_conv_b, conf_ln_g, conf_ln_b, w_out, b_out, ln1_g, ln1_b, w_ple_gate, w_ple_proj,
                         ln2_g, ln2_b)))
    m = dict(zip(order, (m_ln_emb_g, m_ln_emb_b, m_w_in, m_ssm_conv_w, m_ssm_conv_b, m_dt_bias, m_a_log, m_d_skip,
                         m_ssm_norm_g, m_b_glu, m_conf_conv_w, m_conf_conv_b, m_conf_ln_g, m_conf_ln_b, m_w_out, m_b_out,
                         m_ln1_g, m_ln1_b, m_w_ple_gate, m_w_ple_proj, m_ln2_g, m_ln2_b)))
    v = dict(zip(order, (v_ln_emb_g, v_ln_emb_b, v_w_in, v_ssm_conv_w, v_ssm_conv_b, v_dt_bias, v_a_log, v_d_skip,
                         v_ssm_norm_g, v_b_glu, v_conf_conv_w, v_conf_conv_b, v_conf_ln_g, v_conf_ln_b, v_w_out, v_b_out,
                         v_ln1_g, v_ln1_b, v_w_ple_gate, v_w_ple_proj, v_ln2_g, v_ln2_b)))

    conv_f = jnp.concatenate([_rows_of(w["ssm_conv_w"], ROWS_CW4), _rows_of(w["conf_conv_w"], ROWS_CW31),
                              jnp.zeros((ROWS_CONV - ROWS_CW4 - ROWS_CW31, D_MODEL), F32)], axis=0)
    shards_b = [w[n][0].astype(BF16) for n, _, _ in BIG]
    w_in_all, conv_all = _gather_weights(shards_b[0], conv_f)
    w_r = _w_r_from_shards(w_in_all)
    cw4 = conv_all[:, 0:ROWS_CW4].reshape(N_CHIPS, -1)[:, :SSM_K * D_XBC // N_CHIPS]
    cw4 = cw4.reshape(N_CHIPS, SSM_K, D_XBC // N_CHIPS).transpose(1, 0, 2).reshape(SSM_K, D_XBC)
    cw31 = conv_all[:, ROWS_CW4:ROWS_CW4 + ROWS_CW31].reshape(N_CHIPS, -1)[:, :CONF_K * D_CONF // N_CHIPS]
    cw31 = cw31.reshape(N_CHIPS, CONF_K, D_CONF // N_CHIPS).transpose(1, 0, 2).reshape(CONF_K, D_CONF)

    small_names = [n for n, _, _ in SMALL_LAYOUT]
    sm = {n: w[n] for n in small_names}
    loss_part, dx, grads = _local_grads(x[0], p[0, 0], loss_target[0], w_r, shards_b[1:], cw4, cw31, sm)

    core = lax.axis_index("c").astype(jnp.int32).reshape(1)
    chip_i = 2 * lax.axis_index("x") + lax.axis_index("y")
    chip = chip_i.astype(jnp.int32).reshape(1)
    gbig = [_shards_from_dw_r(grads["w_in"]), grads["w_out"].reshape(N_CHIPS, -1, D_MODEL),
            grads["w_ple_gate"].reshape(N_CHIPS, -1, D_MODEL),
            grads["w_ple_proj"].reshape(D_PLE, N_CHIPS, D_MODEL // N_CHIPS).transpose(1, 0, 2)]
    spack = _pack_small({n: grads[n] for n in small_names + [n for n, _, _, _ in CONV_LAYOUT]}, loss_part)
    *theirs, small_all = _exchange_cores(gbig, spack)
    psums = [_add_cores(core, g, t, "add_cores_" + n) for g, t, (n, _, _) in zip(gbig, theirs, BIG)]
    got = _exchange_chips(psums)
    tots = [_add_chips(chip, ps, gt, "add_chips_" + n) for ps, gt, (n, _, _) in zip(psums, got, BIG)]
    both = _share_halves(tots)

    out_g, out_d, out_m, out_v = {}, {}, {}, {}
    g_s, d_s, m_s, v_s = _adam_small(_pack_small(sm), small_all, _pack_small({n: m[n] for n in small_names}),
                                     _pack_small({n: v[n] for n in small_names}))
    loss = g_s[LOSS_ROW, 0]
    g_shard = {n: b.reshape(rows, cols) for b, (n, rows, cols) in zip(both, BIG)}
    for n, r0, r, shape in CONV_LAYOUT:
        whole = g_s[r0:r0 + r].reshape(-1)[:shape[0] * shape[1]].reshape(shape)
        g_shard[n] = lax.dynamic_slice_in_dim(whole, chip_i * (shape[1] // N_CHIPS), shape[1] // N_CHIPS, axis=1)
    for n, g in g_shard.items():
        shape = w[n].shape
        d, nm, nv = _adam(w[n][0], g, m[n][0], v[n][0], "adam_" + n)
        out_g[n], out_d[n], out_m[n], out_v[n] = g[None], d[None], nm[None], nv[None]
    for dst, src in ((out_g, g_s), (out_d, d_s), (out_m, m_s), (out_v, v_s)):
        dst.update(_unpack_small(src, sm))
    return (loss, dx[None], *[out_g[n] for n in order], *[out_d[n] for n in order], *[out_m[n] for n in order],
            *[out_v[n] for n in order])
```

```python
import functools

import jax
import jax.numpy as jnp
from jax import lax
from jax.experimental import pallas as pl
from jax.experimental.pallas import tpu as pltpu

F32 = jnp.float32
BF16 = jnp.bfloat16

D_MODEL = 1024
D_PLE = 256
D_SSM = 1024
D_CONF = 1024
N_HEADS = 16
HEAD_DIM = 64
N_GROUPS = 2
N_STATE = 128
CHUNK = 128
SSM_K = 4
CONF_K = 31
D_XBC = D_SSM + 2 * N_GROUPS * N_STATE
D_IN = 5648
R_XBC, R_Z, R_GLU, R_CG, R_DT, D_INR = 0, 1536, 2560, 4608, 5632, 5760
LN_EPS = 1e-5
RMS_EPS = 1e-5
ALPHA = 2.0 ** 0.25
ADAM_LR, ADAM_B1, ADAM_B2, ADAM_EPS, ADAM_WD, ADAM_STEP = 0.001, 0.9, 0.999, 1e-08, 0.01, 10
NEG_BIG = -1e30
LANES = 128
SUBLANES = 8
VMEM_LIMIT = 56 * 1024 * 1024
HIGHEST = lax.Precision.HIGHEST
NT_DIMS = (((1,), (1,)), ((), ()))
TN_DIMS = (((0,), (0,)), ((), ()))


def _sig(v):
    return jax.nn.sigmoid(v)


def _dsilu(v, s):
    return s * (1.0 + v * (1.0 - s))


def _ln_stats(v):
    mu = jnp.mean(v, axis=-1, keepdims=True)
    c = v - mu
    var = jnp.mean(c * c, axis=-1, keepdims=True)
    rstd = lax.rsqrt(var + LN_EPS)
    return c * rstd, rstd


def _ln_bwd(dxhat, xhat, rstd):
    m1 = jnp.mean(dxhat, axis=-1, keepdims=True)
    m2 = jnp.mean(dxhat * xhat, axis=-1, keepdims=True)
    return rstd * (dxhat - m1 - xhat * m2)


def _softplus(v):
    return jnp.maximum(v, 0.0) + jnp.log1p(jnp.exp(-jnp.abs(v)))


def _colsum(v):
    return jnp.sum(v, axis=0, keepdims=True)


def _dot(a, b):
    return jnp.dot(a, b, preferred_element_type=F32)


def _dot_nt(a, b):
    return lax.dot_general(a, b, NT_DIMS, preferred_element_type=F32)


def _tile(tl, c, rev_of=None):
    if rev_of is None:
        return pl.BlockSpec((tl, c), lambda i: (i, 0))
    return pl.BlockSpec((tl, c), lambda i: (rev_of - 1 - i, 0))


def _full(shape, single=False):
    nd = len(shape)
    if single:
        return pl.BlockSpec(shape, lambda i: (0,) * nd, pipeline_mode=pl.Buffered(1))
    return pl.BlockSpec(shape, lambda i: (0,) * nd)


def _params(seq=True):
    return pltpu.CompilerParams(dimension_semantics=("arbitrary",) if seq else ("parallel",), vmem_limit_bytes=VMEM_LIMIT)


def _ln_inproj(x, g, b, w_r, later, tl):
    n_tok = x.shape[0]
    n_t = n_tok // tl
    n_l = len(later)
    plan, own_plan = _gather_plans([(rows, GATHER_CHUNKS[a]) for a, (_, rows, _) in enumerate(BIG)][1:])

    def body(x_ref, g_ref, b_ref, w_ref, *rest):
        part_refs = rest[0:n_l]
        h0b_ref, xbc_ref, z_ref, glu_ref, cg_ref, dtr_ref = rest[n_l:n_l + 6]
        all_refs, sems = rest[n_l + 6:2 * n_l + 6], rest[2 * n_l + 6:]
        i = pl.program_id(0)

        @pl.when(i == 0)
        def _():
            _gather_stage(0, part_refs, all_refs, plan, own_plan, sems)

        xhat, _ = _ln_stats(x_ref[...])
        hb = (xhat * g_ref[...] + b_ref[...]).astype(BF16)
        h0b_ref[...] = hb
        xbc_ref[...] = _dot(hb, w_ref[:, R_XBC:R_Z])
        z_ref[...] = _dot(hb, w_ref[:, R_Z:R_GLU])
        glu_ref[...] = _dot(hb, w_ref[:, R_GLU:R_CG])
        cg_ref[...] = _dot(hb, w_ref[:, R_CG:R_DT])
        dtr_ref[...] = _dot(hb, w_ref[:, R_DT:D_INR])

        @pl.when(i == (3 * n_t) // 4)
        def _():
            _gather_stage(1, part_refs, all_refs, plan, own_plan, sems)

        @pl.when(i == n_t - 1)
        def _():
            _gather_stage(2, part_refs, all_refs, plan, own_plan, sems)

    widths = (D_MODEL, D_XBC, D_SSM, 2 * D_CONF, D_CONF, LANES)
    dtypes = (BF16, F32, F32, F32, F32, F32)
    return pl.pallas_call(
        body, grid=(n_t,), name="ln_inproj",
        in_specs=[_tile(tl, D_MODEL), _full((1, D_MODEL)), _full((1, D_MODEL)), _full((D_MODEL, D_INR), single=True)]
        + [IN_VMEM] * n_l,
        out_specs=[_tile(tl, w) for w in widths] + [ANY] * n_l,
        out_shape=[jax.ShapeDtypeStruct((n_tok, w), dt) for w, dt in zip(widths, dtypes)]
        + [jax.ShapeDtypeStruct((N_CHIPS,) + a.shape, a.dtype) for a in later],
        scratch_shapes=_gather_sems(plan, own_plan),
        compiler_params=_params(),
    )(x, g, b, w_r, *later)


def _chunk_common(adt_c):
    row = lax.broadcasted_iota(jnp.int32, (CHUNK, CHUNK), 0)
    col = lax.broadcasted_iota(jnp.int32, (CHUNK, CHUNK), 1)
    tril = row >= col
    acs = jnp.dot(tril.astype(F32), adt_c, precision=HIGHEST, preferred_element_type=F32)
    last = acs[CHUNK - 1:CHUNK, :]
    return dict(row=row, col=col, tril=tril, lo=col < HEAD_DIM, acs=acs, acs_t=acs.T, e=jnp.exp(acs),
                dec=jnp.exp(last - acs), cd=jnp.exp(last))


def _decay_mask(cm, h):
    return jnp.exp(jnp.where(cm["tril"], cm["acs"][:, h:h + 1] - cm["acs_t"][h:h + 1, :], NEG_BIG))


def _head_lane_matrix():
    return (jnp.arange(D_SSM)[None, :] // HEAD_DIM == jnp.arange(LANES)[:, None]).astype(BF16)


def _per_head_lanes(v, exp_ref):
    hi = v.astype(BF16)
    lo = (v - hi.astype(F32)).astype(BF16)
    return _dot(hi, exp_ref[...]) + _dot(lo, exp_ref[...])


SSM_BWD_SHIFTS = (1, 2, 3)


def _ssd_fwd(xbc_in, z, dtr, cw, cb, dtb, alog, dsk, gnorm, tl):
    n_tok = xbc_in.shape[0]
    nq = tl // CHUNK

    def body(xin_ref, z_ref, dtr_ref, cw_ref, cb_ref, dtb_ref, alog_ref, dsk_ref, gn_ref, exp_ref,
             pre_ref, y_ref, yssm_ref, hprev_ref, buf, hst):
        @pl.when(pl.program_id(0) == 0)
        def _():
            buf[0:SUBLANES, :] = jnp.zeros((SUBLANES, D_XBC), F32)
            hst[...] = jnp.zeros_like(hst)

        buf[SUBLANES:SUBLANES + tl, :] = xin_ref[...]
        pre = cb_ref[...] + jnp.zeros((tl, D_XBC), F32)
        for k in range(SSM_K):
            off = SUBLANES - (SSM_K - 1) + k
            pre = pre + buf[off:off + tl, :] * cw_ref[k:k + 1, :]
        buf[0:SUBLANES, :] = buf[tl:tl + SUBLANES, :]
        pre_ref[...] = pre
        xbc = pre * _sig(pre)
        dt = _softplus(dtr_ref[...] + dtb_ref[...])
        a = -jnp.exp(alog_ref[...])
        adt = dt * a
        for q in range(nq):
            r0 = q * CHUNK
            cm = _chunk_common(adt[r0:r0 + CHUNK, :])
            dt_x = _per_head_lanes(dt[r0:r0 + CHUNK, :], exp_ref)
            e_x = _per_head_lanes(cm["e"], exp_ref)
            dec_x = _per_head_lanes(cm["dec"], exp_ref)
            for g in range(N_GROUPS):
                bg = xbc[r0:r0 + CHUNK, D_SSM + g * N_STATE:D_SSM + (g + 1) * N_STATE].astype(BF16)
                cg_ = xbc[r0:r0 + CHUNK, D_SSM + (N_GROUPS + g) * N_STATE:D_SSM + (N_GROUPS + g + 1) * N_STATE].astype(BF16)
                gm = _dot_nt(cg_, bg)
                for k in range(N_HEADS // N_GROUPS // 2):
                    ha = (N_HEADS // N_GROUPS) * g + 2 * k
                    c0 = ha * HEAD_DIM
                    xh2 = xbc[r0:r0 + CHUNK, c0:c0 + LANES]
                    x2 = xh2 * dt_x[:, c0:c0 + LANES]
                    x2b = x2.astype(BF16)
                    ya = _dot((gm * _decay_mask(cm, ha)).astype(BF16), x2b)
                    yb = _dot((gm * _decay_mask(cm, ha + 1)).astype(BF16), x2b)
                    h2 = hst[c0:c0 + LANES, :]
                    hprev_ref[q, c0:c0 + LANES, :] = h2
                    z2 = _dot_nt(cg_, h2.astype(BF16))
                    y2 = jnp.where(cm["lo"], ya, yb) + z2 * e_x[:, c0:c0 + LANES] + dsk_ref[:, c0:c0 + LANES] * xh2
                    y_ref[r0:r0 + CHUNK, c0:c0 + LANES] = y2
                    s2 = _dot((x2 * dec_x[:, c0:c0 + LANES]).T.astype(BF16), bg)
                    cd2 = jnp.where(cm["row"] < HEAD_DIM, cm["cd"][:, ha:ha + 1], cm["cd"][:, ha + 1:ha + 2])
                    hst[c0:c0 + LANES, :] = cd2 * h2 + s2
        yv = y_ref[...]
        zv = z_ref[...]
        yz = yv * (zv * _sig(zv))
        gw = D_SSM // N_GROUPS
        for g in range(N_GROUPS):
            seg = yz[:, g * gw:(g + 1) * gw]
            r = lax.rsqrt(jnp.mean(seg * seg, axis=-1, keepdims=True) + RMS_EPS)
            yssm_ref[:, g * gw:(g + 1) * gw] = (seg * r * gn_ref[:, g * gw:(g + 1) * gw]).astype(BF16)

    return pl.pallas_call(
        body, grid=(n_tok // tl,), name="ssd_fwd",
        in_specs=[_tile(tl, D_XBC), _tile(tl, D_SSM), _tile(tl, LANES), _full((SUBLANES, D_XBC)), _full((1, D_XBC)),
                  _full((1, LANES)), _full((1, LANES)), _full((1, D_SSM)), _full((1, D_SSM)), _full((LANES, D_SSM))],
        out_specs=[_tile(tl, D_XBC), _tile(tl, D_SSM), _tile(tl, D_SSM),
                   pl.BlockSpec((nq, D_SSM, N_STATE), lambda i: (i, 0, 0))],
        out_shape=[jax.ShapeDtypeStruct((n_tok, D_XBC), F32), jax.ShapeDtypeStruct((n_tok, D_SSM), F32),
                   jax.ShapeDtypeStruct((n_tok, D_SSM), BF16), jax.ShapeDtypeStruct((n_tok // CHUNK, D_SSM, N_STATE), F32)],
        scratch_shapes=[pltpu.VMEM((tl + SUBLANES, D_XBC), F32), pltpu.VMEM((D_SSM, N_STATE), F32)],
        compiler_params=_params(),
    )(xbc_in, z, dtr, cw, cb, dtb, alog, dsk, gnorm, _head_lane_matrix())


def _ssd_bwd(dys, y, z, pre, xbc_in, dtr, hprev, cw, dtb, alog, dsk, gnorm, tl):
    n_tok = y.shape[0]
    n_t = n_tok // tl
    nq = tl // CHUNK

    def body(dys_ref, y_ref, z_ref, pre_ref, xin_ref, dtr_ref, hprev_ref, cw_ref, dtb_ref, alog_ref, dsk_ref, gn_ref,
             exp_ref, redx_ref, redq_ref,
             dxin_ref, dz_ref, ddtr_ref, dcw_ref, dcb_ref, dgn_ref, ddsk_ref, da_ref, ddtb_ref,
             dxs, dh, cs_s, dskc, shifts, dwp, sums_q, sums_s):
        i = pl.program_id(0)

        @pl.when(i == 0)
        def _():
            dcw_ref[...] = jnp.zeros_like(dcw_ref)
            dwp[...] = jnp.zeros_like(dwp)
            dcb_ref[...] = jnp.zeros_like(dcb_ref)
            dgn_ref[...] = jnp.zeros_like(dgn_ref)
            da_ref[...] = jnp.zeros_like(da_ref)
            ddtb_ref[...] = jnp.zeros_like(ddtb_ref)
            dskc[...] = jnp.zeros_like(dskc)
            dh[...] = jnp.zeros_like(dh)
            dxs[tl:tl + SUBLANES, :] = jnp.zeros((SUBLANES, D_XBC), F32)

        yv = y_ref[...]
        zv = z_ref[...]
        dysv = dys_ref[...]
        sz = _sig(zv)
        silz = zv * sz
        yz = yv * silz
        gw = D_SSM // N_GROUPS
        dyz_parts = []
        for g in range(N_GROUPS):
            sl = slice(g * gw, (g + 1) * gw)
            seg = yz[:, sl]
            r = lax.rsqrt(jnp.mean(seg * seg, axis=-1, keepdims=True) + RMS_EPS)
            yzn = seg * r
            dgn_ref[:, sl] += _colsum(dysv[:, sl] * yzn)
            dyzn = dysv[:, sl] * gn_ref[:, sl]
            dyz_parts.append(r * (dyzn - yzn * jnp.mean(dyzn * yzn, axis=-1, keepdims=True)))
        dyz = jnp.concatenate(dyz_parts, axis=1)
        dy = dyz * silz
        dz_ref[...] = dyz * yv * _dsilu(zv, sz)

        prev = pre_ref[...]
        sp = _sig(prev)
        xbc = prev * sp
        dskc[...] += _colsum(dy * xbc[:, 0:D_SSM])
        dt_in = dtr_ref[...] + dtb_ref[...]
        dt = _softplus(dt_in)
        dsp = _sig(dt_in)
        a = -jnp.exp(alog_ref[...])
        adt = dt * a
        for q in reversed(range(nq)):
            r0 = q * CHUNK
            cm = _chunk_common(adt[r0:r0 + CHUNK, :])
            row, col, lo = cm["row"], cm["col"], cm["lo"]
            triu = (col >= row).astype(F32)
            dt_c = dt[r0:r0 + CHUNK, :]
            dt_x = _per_head_lanes(dt_c, exp_ref)
            e_x = _per_head_lanes(cm["e"], exp_ref)
            dec_x = _per_head_lanes(cm["dec"], exp_ref)
            dcd_row = jnp.zeros((1, LANES), F32)
            for g in range(N_GROUPS):
                bcol = D_SSM + g * N_STATE
                ccol = D_SSM + (N_GROUPS + g) * N_STATE
                bg = xbc[r0:r0 + CHUNK, bcol:bcol + N_STATE].astype(BF16)
                cg_ = xbc[r0:r0 + CHUNK, ccol:ccol + N_STATE].astype(BF16)
                gm = _dot_nt(cg_, bg)
                dgm = jnp.zeros((CHUNK, CHUNK), F32)
                dbg = jnp.zeros((CHUNK, N_STATE), F32)
                dcg = jnp.zeros((CHUNK, N_STATE), F32)
                for k in range(N_HEADS // N_GROUPS // 2):
                    ha = (N_HEADS // N_GROUPS) * g + 2 * k
                    hb = ha + 1
                    c0 = ha * HEAD_DIM
                    xh2 = xbc[r0:r0 + CHUNK, c0:c0 + LANES]
                    dt2 = dt_x[:, c0:c0 + LANES]
                    x2 = xh2 * dt2
                    x2b = x2.astype(BF16)
                    la = _decay_mask(cm, ha)
                    lb = _decay_mask(cm, hb)
                    ma = gm * la
                    mb = gm * lb
                    dy2 = dy[r0:r0 + CHUNK, c0:c0 + LANES]
                    dy2b = dy2.astype(BF16)
                    dma = _dot_nt(jnp.where(lo, dy2, 0.0).astype(BF16), x2b)
                    dmb = _dot_nt(jnp.where(lo, 0.0, dy2).astype(BF16), x2b)
                    dx2 = jnp.where(lo, _dot(ma.T.astype(BF16), dy2b), _dot(mb.T.astype(BF16), dy2b))
                    dgm = dgm + dma * la + dmb * lb
                    sums_q[:, ha * CHUNK:(ha + 1) * CHUNK] = (dma * ma).astype(BF16)
                    sums_q[:, hb * CHUNK:(hb + 1) * CHUNK] = (dmb * mb).astype(BF16)
                    h2 = hprev_ref[q, c0:c0 + LANES, :]
                    h2b = h2.astype(BF16)
                    dz2 = dy2 * e_x[:, c0:c0 + LANES]
                    dcg = dcg + _dot(dz2.astype(BF16), h2b)
                    sums_s[0, :, c0:c0 + LANES] = (dz2 * _dot_nt(cg_, h2b)).astype(BF16)
                    dhn = dh[c0:c0 + LANES, :]
                    dhnb = dhn.astype(BF16)
                    cd_a = cm["cd"][:, ha:ha + 1]
                    cd_b = cm["cd"][:, hb:hb + 1]
                    top = row < HEAD_DIM
                    hh = dhn * h2
                    dcd_a = jnp.sum(_colsum(jnp.where(top, hh, 0.0)), axis=1, keepdims=True)
                    dcd_b = jnp.sum(_colsum(jnp.where(top, 0.0, hh)), axis=1, keepdims=True)
                    dcd_row = dcd_row + jnp.where(col[0:1, :] == ha, dcd_a * cd_a, 0.0) + jnp.where(col[0:1, :] == hb, dcd_b * cd_b, 0.0)
                    dh[c0:c0 + LANES, :] = jnp.where(top, cd_a, cd_b) * dhn + _dot(dz2.T.astype(BF16), cg_)
                    w2 = _dot_nt(bg, dhnb)
                    dec2 = dec_x[:, c0:c0 + LANES]
                    dx2 = dx2 + dec2 * w2
                    sums_s[1, :, c0:c0 + LANES] = (x2 * w2).astype(BF16)
                    dbg = dbg + _dot((x2 * dec2).astype(BF16), dhnb)
                    sums_s[2, :, c0:c0 + LANES] = (dx2 * xh2).astype(BF16)
                    dxs[r0:r0 + CHUNK, c0:c0 + LANES] = dx2 * dt2 + dsk_ref[:, c0:c0 + LANES] * dy2
                dgmb = dgm.astype(BF16)
                dxs[r0:r0 + CHUNK, bcol:bcol + N_STATE] = dbg + _dot(dgm.T.astype(BF16), cg_)
                dxs[r0:r0 + CHUNK, ccol:ccol + N_STATE] = dcg + _dot(dgmb, bg)
            q_all = sums_q[...]
            q_cols = _dot(jnp.ones((SUBLANES, CHUNK), BF16), q_all)
            cs_s[...] = jnp.zeros_like(cs_s)
            for h in range(N_HEADS):
                cs_s[h:h + 1, :] = q_cols[0:1, h * CHUNK:(h + 1) * CHUNK]
            de = _dot(sums_s[0], redx_ref[...])
            dd = _dot(sums_s[1], redx_ref[...]) * cm["dec"]
            ddtx = _dot(sums_s[2], redx_ref[...])
            is_last = row == CHUNK - 1
            dacs = _dot(q_all, redq_ref[...]) - cs_s[...].T + de - dd + jnp.where(is_last, dcd_row + _colsum(dd), 0.0)
            dadt = jnp.dot(triu, dacs, precision=HIGHEST, preferred_element_type=F32)
            da_ref[...] += _colsum(dadt * dt_c)
            ddtr_c = (dadt * a + ddtx) * dsp[r0:r0 + CHUNK, :]
            ddtr_ref[r0:r0 + CHUNK, :] = ddtr_c
            ddtb_ref[...] += _colsum(ddtr_c)

        dpre = dxs[0:tl, :] * _dsilu(prev, sp)
        dxs[0:tl, :] = dpre
        dcb_ref[...] += _colsum(dpre)
        _shift_copies(dxs, shifts, tl, SSM_BWD_SHIFTS)

        def strip(rb, carry):
            i0 = pl.multiple_of(rb * CONV_RS, CONV_RS)
            for c0 in range(0, D_XBC, CONV_CS):
                xin_s = xin_ref[pl.ds(i0, CONV_RS), c0:c0 + CONV_CS]
                acc = jnp.zeros((CONV_RS, CONV_CS), F32)
                for k in range(SSM_K):
                    sh = _tap_rows(dxs, shifts, SSM_K - 1 - k, i0, c0, SSM_BWD_SHIFTS)
                    acc = acc + sh * cw_ref[k:k + 1, c0:c0 + CONV_CS]
                    t = xin_s * sh
                    dwp[k * SUBLANES:(k + 1) * SUBLANES, c0:c0 + CONV_CS] += _fold_rows(t)
                dxin_ref[pl.ds(i0, CONV_RS), c0:c0 + CONV_CS] = acc
            return carry

        lax.fori_loop(0, tl // CONV_RS, strip, 0)
        dxs[tl:tl + SUBLANES, :] = dxs[0:SUBLANES, :]

        @pl.when(i == n_t - 1)
        def _():
            for k in range(SSM_K):
                dcw_ref[k:k + 1, :] = _colsum(dwp[k * SUBLANES:(k + 1) * SUBLANES, :])
            da_ref[...] = da_ref[...] * a
            sel = (lax.broadcasted_iota(jnp.int32, (D_SSM, LANES), 0) // HEAD_DIM
                   == lax.broadcasted_iota(jnp.int32, (D_SSM, LANES), 1)).astype(F32)
            rows = jnp.broadcast_to(dskc[...], (SUBLANES, D_SSM))
            ddsk_ref[...] = jnp.dot(rows, sel, precision=HIGHEST, preferred_element_type=F32)[0:1, :]

    head_lanes = _head_lane_matrix()
    per_head = (jnp.arange(N_HEADS * CHUNK)[:, None] // CHUNK == jnp.arange(LANES)[None, :]).astype(BF16)
    rev = functools.partial(_tile, tl, rev_of=n_t)
    return pl.pallas_call(
        body, grid=(n_t,), name="ssd_bwd",
        in_specs=[rev(D_SSM), rev(D_SSM), rev(D_SSM), rev(D_XBC), rev(D_XBC), rev(LANES),
                  pl.BlockSpec((nq, D_SSM, N_STATE), lambda i: (n_t - 1 - i, 0, 0)),
                  _full((SUBLANES, D_XBC)), _full((1, LANES)), _full((1, LANES)), _full((1, D_SSM)), _full((1, D_SSM)),
                  _full((LANES, D_SSM)), _full((D_SSM, LANES)), _full((N_HEADS * CHUNK, LANES))],
        out_specs=[rev(D_XBC), rev(D_SSM), rev(LANES), _full((SUBLANES, D_XBC)), _full((1, D_XBC)), _full((1, D_SSM)),
                   _full((1, LANES)), _full((1, LANES)), _full((1, LANES))],
        out_shape=[jax.ShapeDtypeStruct((n_tok, D_XBC), F32), jax.ShapeDtypeStruct((n_tok, D_SSM), F32),
                   jax.ShapeDtypeStruct((n_tok, LANES), F32), jax.ShapeDtypeStruct((SUBLANES, D_XBC), F32),
                   jax.ShapeDtypeStruct((1, D_XBC), F32), jax.ShapeDtypeStruct((1, D_SSM), F32),
                   jax.ShapeDtypeStruct((1, LANES), F32), jax.ShapeDtypeStruct((1, LANES), F32),
                   jax.ShapeDtypeStruct((1, LANES), F32)],
        scratch_shapes=[pltpu.VMEM((tl + SUBLANES, D_XBC), F32), pltpu.VMEM((D_SSM, N_STATE), F32),
                        pltpu.VMEM((CHUNK, LANES), F32), pltpu.VMEM((1, D_SSM), F32),
                        pltpu.VMEM((len(SSM_BWD_SHIFTS), tl, D_XBC), F32), pltpu.VMEM((SSM_K * SUBLANES, D_XBC), F32),
                        pltpu.VMEM((CHUNK, N_HEADS * CHUNK), BF16), pltpu.VMEM((3, CHUNK, D_SSM), BF16)],
        compiler_params=_params(),
    )(dys, y, z, pre, xbc_in, dtr, hprev, cw, dtb, alog, dsk, gnorm, head_lanes, head_lanes.T, per_head)


CONF_HALO = 32
CONV_RS = 32
CONV_CS = 256


ALL_SHIFTS = tuple(range(1, SUBLANES))


def _shift_copies(buf, shifts, n_rows, residues=ALL_SHIFTS):
    for j, r in enumerate(residues):
        shifts[j, 0:n_rows, :] = buf[r:r + n_rows, :]


def _fold_rows(t):
    part = t[0:SUBLANES]
    for j in range(1, t.shape[0] // SUBLANES):
        part = part + t[j * SUBLANES:(j + 1) * SUBLANES]
    return part


def _tap_rows(buf, shifts, off, i0, c0, residues=ALL_SHIFTS):
    q, r = divmod(off, SUBLANES)
    rows = pl.ds(pl.multiple_of(i0 + SUBLANES * q, SUBLANES), CONV_RS)
    if r == 0:
        return buf[rows, c0:c0 + CONV_CS]
    return shifts[residues.index(r), rows, c0:c0 + CONV_CS]


def _conf_fwd(glu, cgate, bglu, cw, cb, lg, lb, tl):
    n_tok = glu.shape[0]

    def body(glu_ref, cg_ref, bglu_ref, cw_ref, cb_ref, lg_ref, lb_ref, u0_ref, u1_ref, yc_ref, buf, shifts):
        @pl.when(pl.program_id(0) == 0)
        def _():
            buf[0:CONF_HALO, :] = jnp.zeros((CONF_HALO, D_CONF), F32)

        gl = glu_ref[...] + bglu_ref[...]
        u0 = gl[:, 0:D_CONF] * _sig(gl[:, D_CONF:2 * D_CONF])
        u0_ref[...] = u0
        buf[CONF_HALO:CONF_HALO + tl, :] = u0
        _shift_copies(buf, shifts, tl + CONF_HALO - SUBLANES)

        def strip(rb, carry):
            i0 = pl.multiple_of(rb * CONV_RS, CONV_RS)
            for c0 in range(0, D_CONF, CONV_CS):
                acc = jnp.broadcast_to(cb_ref[:, c0:c0 + CONV_CS], (CONV_RS, CONV_CS))
                for k in range(CONF_K):
                    acc = acc + _tap_rows(buf, shifts, CONF_HALO - (CONF_K - 1) + k, i0, c0) * cw_ref[k:k + 1, c0:c0 + CONV_CS]
                u1_ref[pl.ds(i0, CONV_RS), c0:c0 + CONV_CS] = acc
            return carry

        lax.fori_loop(0, tl // CONV_RS, strip, 0)
        buf[0:CONF_HALO, :] = buf[tl:tl + CONF_HALO, :]
        xhat, _ = _ln_stats(u1_ref[...])
        n = xhat * lg_ref[...] + lb_ref[...]
        cgv = cg_ref[...]
        yc_ref[...] = (n * _sig(n) * (cgv * _sig(cgv))).astype(BF16)

    return pl.pallas_call(
        body, grid=(n_tok // tl,), name="conf_fwd",
        in_specs=[_tile(tl, 2 * D_CONF), _tile(tl, D_CONF), _full((1, 2 * D_CONF)), _full((CONF_HALO, D_CONF)),
                  _full((1, D_CONF)), _full((1, D_CONF)), _full((1, D_CONF))],
        out_specs=[_tile(tl, D_CONF)] * 3,
        out_shape=[jax.ShapeDtypeStruct((n_tok, D_CONF), F32), jax.ShapeDtypeStruct((n_tok, D_CONF), F32),
                   jax.ShapeDtypeStruct((n_tok, D_CONF), BF16)],
        scratch_shapes=[pltpu.VMEM((tl + CONF_HALO, D_CONF), F32),
                        pltpu.VMEM((SUBLANES - 1, tl + CONF_HALO - SUBLANES, D_CONF), F32)],
        compiler_params=_params(),
    )(glu, cgate, bglu, cw, cb, lg, lb)


def _conf_bwd(dyc, dyc_block, u0, u1, glu, cgate, bglu, cw, lg, lb, tl):
    n_tok = glu.shape[0]
    n_t = n_tok // tl

    def body(dyc_ref, u0_ref, u1_ref, glu_ref, cg_ref, bglu_ref, cw_ref, lg_ref, lb_ref,
             dglu_ref, dcg_ref, dcw_ref, dbglu_ref, small_ref, buf, shifts, du0_s, dwp):
        @pl.when(pl.program_id(0) == 0)
        def _():
            dwp[...] = jnp.zeros_like(dwp)
            dbglu_ref[...] = jnp.zeros_like(dbglu_ref)
            small_ref[...] = jnp.zeros_like(small_ref)
            buf[tl:tl + CONF_HALO, :] = jnp.zeros((CONF_HALO, D_CONF), F32)

        xhat, rstd = _ln_stats(u1_ref[...])
        n = xhat * lg_ref[...] + lb_ref[...]
        sn = _sig(n)
        cgv = cg_ref[...]
        scg = _sig(cgv)
        dycv = dyc_ref[...]
        dcg_ref[...] = dycv * (n * sn) * _dsilu(cgv, scg)
        dn = dycv * (cgv * scg) * _dsilu(n, sn)
        small_ref[0:1, :] += _colsum(dn * xhat)
        small_ref[1:2, :] += _colsum(dn)
        du1 = _ln_bwd(dn * lg_ref[...], xhat, rstd)
        small_ref[2:3, :] += _colsum(du1)
        buf[0:tl, :] = du1
        _shift_copies(buf, shifts, tl + CONF_HALO - SUBLANES)

        def strip(rb, carry):
            i0 = pl.multiple_of(rb * CONV_RS, CONV_RS)
            for c0 in range(0, D_CONF, CONV_CS):
                u0s = u0_ref[pl.ds(i0, CONV_RS), c0:c0 + CONV_CS]
                acc = jnp.zeros((CONV_RS, CONV_CS), F32)
                for k in range(CONF_K):
                    sh = _tap_rows(buf, shifts, CONF_K - 1 - k, i0, c0)
                    acc = acc + sh * cw_ref[k:k + 1, c0:c0 + CONV_CS]
                    t = u0s * sh
                    dwp[k * SUBLANES:(k + 1) * SUBLANES, c0:c0 + CONV_CS] += _fold_rows(t)
                du0_s[pl.ds(i0, CONV_RS), c0:c0 + CONV_CS] = acc
            return carry

        lax.fori_loop(0, tl // CONV_RS, strip, 0)
        du0 = du0_s[...]
        buf[tl:tl + CONF_HALO, :] = buf[0:CONF_HALO, :]
        gl = glu_ref[...] + bglu_ref[...]
        sg = _sig(gl[:, D_CONF:2 * D_CONF])
        dgv = du0 * sg
        dgg = du0 * gl[:, 0:D_CONF] * sg * (1.0 - sg)
        dglu_ref[:, 0:D_CONF] = dgv
        dglu_ref[:, D_CONF:2 * D_CONF] = dgg
        dbglu_ref[:, 0:D_CONF] += _colsum(dgv)
        dbglu_ref[:, D_CONF:2 * D_CONF] += _colsum(dgg)

        @pl.when(pl.program_id(0) == n_t - 1)
        def _():
            for k in range(CONF_HALO):
                dcw_ref[k:k + 1, :] = _colsum(dwp[k * SUBLANES:(k + 1) * SUBLANES, :])

    rev = functools.partial(_tile, tl, rev_of=n_t)
    return pl.pallas_call(
        body, grid=(n_t,), name="conf_bwd",
        in_specs=[pl.BlockSpec((tl, D_CONF), lambda i: (n_t - 1 - i, dyc_block)),
                  rev(D_CONF), rev(D_CONF), rev(2 * D_CONF), rev(D_CONF), _full((1, 2 * D_CONF)),
                  _full((CONF_HALO, D_CONF)), _full((1, D_CONF)), _full((1, D_CONF))],
        out_specs=[rev(2 * D_CONF), rev(D_CONF), _full((CONF_HALO, D_CONF)), _full((1, 2 * D_CONF)), _full((SUBLANES, D_CONF))],
        out_shape=[jax.ShapeDtypeStruct((n_tok, 2 * D_CONF), F32), jax.ShapeDtypeStruct((n_tok, D_CONF), F32),
                   jax.ShapeDtypeStruct((CONF_HALO, D_CONF), F32), jax.ShapeDtypeStruct((1, 2 * D_CONF), F32),
                   jax.ShapeDtypeStruct((SUBLANES, D_CONF), F32)],
        scratch_shapes=[pltpu.VMEM((tl + CONF_HALO, D_CONF), F32),
                        pltpu.VMEM((SUBLANES - 1, tl + CONF_HALO - SUBLANES, D_CONF), F32),
                        pltpu.VMEM((tl, D_CONF), F32), pltpu.VMEM((CONF_HALO * SUBLANES, D_CONF), F32)],
        compiler_params=_params(),
    )(dyc, u0, u1, glu, cgate, bglu, cw, lg, lb)


def _tail(x, yssm, yconf, p, tgt, vec, w_out, wpg, wpp, tl):
    n_tok = x.shape[0]

    def body(x_ref, ys_ref, yc_ref, p_ref, t_ref, vec_ref, wo_ref, wg_ref, wp_ref,
             dmix_ref, dr1_ref, dr1b_ref, h1b_ref, dgb_ref, dpb_ref, small_ref, loss_ref):
        @pl.when(pl.program_id(0) == 0)
        def _():
            small_ref[...] = jnp.zeros_like(small_ref)
            loss_ref[...] = jnp.zeros_like(loss_ref)

        xh0, _ = _ln_stats(x_ref[...])
        h0 = xh0 * vec_ref[0:1, :] + vec_ref[1:2, :]
        out = _dot(ys_ref[...], wo_ref[0:D_SSM, :]) + _dot(yc_ref[...], wo_ref[D_SSM:D_SSM + D_CONF, :]) + vec_ref[2:3, :]
        xh1, rstd1 = _ln_stats(ALPHA * h0 + out)
        h1 = xh1 * vec_ref[3:4, :] + vec_ref[4:5, :]
        h1b = h1.astype(BF16)
        h1b_ref[...] = h1b
        gate = _sig(_dot(h1b, wg_ref[...]))
        ple = _dot(p_ref[...].astype(BF16), wp_ref[...])
        xh2, rstd2 = _ln_stats(ALPHA * h1 + gate * ple)
        h2 = xh2 * vec_ref[5:6, :] + vec_ref[6:7, :]
        diff = h2 - t_ref[...]
        part = jnp.sum(jnp.sum(diff * diff, axis=1, keepdims=True), axis=0, keepdims=True) * (0.5 / D_MODEL)
        loss_ref[...] += jnp.broadcast_to(part, loss_ref.shape)
        dh2 = diff * (1.0 / D_MODEL)
        small_ref[3:4, :] += _colsum(dh2 * xh2)
        small_ref[4:5, :] += _colsum(dh2)
        dr2 = _ln_bwd(dh2 * vec_ref[5:6, :], xh2, rstd2)
        dgpre = (dr2 * ple * gate * (1.0 - gate)).astype(BF16)
        dgb_ref[...] = dgpre
        dpb_ref[...] = (dr2 * gate).astype(BF16)
        dh1 = ALPHA * dr2 + _dot_nt(dgpre, wg_ref[...])
        small_ref[1:2, :] += _colsum(dh1 * xh1)
        small_ref[2:3, :] += _colsum(dh1)
        dr1 = _ln_bwd(dh1 * vec_ref[3:4, :], xh1, rstd1)
        small_ref[0:1, :] += _colsum(dr1)
        dr1_ref[...] = dr1
        dr1b = dr1.astype(BF16)
        dr1b_ref[...] = dr1b
        dmix_ref[...] = _dot_nt(dr1b, wo_ref[...])

    d_mix = D_SSM + D_CONF
    return pl.pallas_call(
        body, grid=(n_tok // tl,), name="tail",
        in_specs=[_tile(tl, D_MODEL), _tile(tl, D_SSM), _tile(tl, D_CONF), _tile(tl, D_PLE), _tile(tl, D_MODEL),
                  _full((SUBLANES, D_MODEL)), _full((d_mix, D_MODEL), True), _full((D_MODEL, D_MODEL), True),
                  _full((D_PLE, D_MODEL), True)],
        out_specs=[_tile(tl, d_mix), _tile(tl, D_MODEL), _tile(tl, D_MODEL), _tile(tl, D_MODEL), _tile(tl, D_MODEL),
                   _tile(tl, D_MODEL), _full((SUBLANES, D_MODEL)), _full((SUBLANES, LANES))],
        out_shape=[jax.ShapeDtypeStruct((n_tok, d_mix), F32), jax.ShapeDtypeStruct((n_tok, D_MODEL), F32),
                   jax.ShapeDtypeStruct((n_tok, D_MODEL), BF16), jax.ShapeDtypeStruct((n_tok, D_MODEL), BF16),
                   jax.ShapeDtypeStruct((n_tok, D_MODEL), BF16), jax.ShapeDtypeStruct((n_tok, D_MODEL), BF16),
                   jax.ShapeDtypeStruct((SUBLANES, D_MODEL), F32), jax.ShapeDtypeStruct((SUBLANES, LANES), F32)],
        compiler_params=_params(),
    )(x, yssm, yconf, p, tgt, vec, w_out, wpg, wpp)


def _inproj_bwd(dxin, dz, dglu, dcg, ddtr, dr1, x, g, b, w_r, tl):
    n_tok = x.shape[0]

    def body(dxin_ref, dz_ref, dglu_ref, dcg_ref, ddtr_ref, dr1_ref, x_ref, g_ref, b_ref, w_ref,
             dx_ref, dpb_ref, small_ref):
        @pl.when(pl.program_id(0) == 0)
        def _():
            small_ref[...] = jnp.zeros_like(small_ref)

        dh0 = ALPHA * dr1_ref[...]
        for ref, lo, hi in ((dxin_ref, R_XBC, R_Z), (dz_ref, R_Z, R_GLU), (dglu_ref, R_GLU, R_CG), (dcg_ref, R_CG, R_DT),
                            (ddtr_ref, R_DT, D_INR)):
            piece = ref[...].astype(BF16)
            dpb_ref[:, lo:hi] = piece
            dh0 = dh0 + _dot_nt(piece, w_ref[:, lo:hi])
        xhat, rstd = _ln_stats(x_ref[...])
        small_ref[0:1, :] += _colsum(dh0 * xhat)
        small_ref[1:2, :] += _colsum(dh0)
        dx_ref[...] = _ln_bwd(dh0 * g_ref[...], xhat, rstd)

    return pl.pallas_call(
        body, grid=(n_tok // tl,), name="inproj_bwd",
        in_specs=[_tile(tl, D_XBC), _tile(tl, D_SSM), _tile(tl, 2 * D_CONF), _tile(tl, D_CONF), _tile(tl, LANES),
                  _tile(tl, D_MODEL), _tile(tl, D_MODEL), _full((1, D_MODEL)), _full((1, D_MODEL)),
                  _full((D_MODEL, D_INR), True)],
        out_specs=[_tile(tl, D_MODEL), _tile(tl, D_INR), _full((SUBLANES, D_MODEL))],
        out_shape=[jax.ShapeDtypeStruct((n_tok, D_MODEL), F32), jax.ShapeDtypeStruct((n_tok, D_INR), BF16),
                   jax.ShapeDtypeStruct((SUBLANES, D_MODEL), F32)],
        compiler_params=_params(),
    )(dxin, dz, dglu, dcg, ddtr, dr1, x, g, b, w_r)


def _tn_matmul(a, b, name, tn, tl):
    n_tok, m = a.shape
    n = b.shape[1]

    def body(a_ref, b_ref, o_ref):
        @pl.when(pl.program_id(1) == 0)
        def _():
            o_ref[...] = jnp.zeros_like(o_ref)

        o_ref[...] += lax.dot_general(a_ref[...], b_ref[...], TN_DIMS, preferred_element_type=F32)

    return pl.pallas_call(
        body, grid=(n // tn, n_tok // tl), name=name,
        in_specs=[pl.BlockSpec((tl, m), lambda j, l: (l, 0)), pl.BlockSpec((tl, tn), lambda j, l: (l, j))],
        out_specs=pl.BlockSpec((m, tn), lambda j, l: (0, j)),
        out_shape=jax.ShapeDtypeStruct((m, n), F32),
        compiler_params=pltpu.CompilerParams(dimension_semantics=("parallel", "arbitrary"), vmem_limit_bytes=VMEM_LIMIT),
    )(a, b)


def _pad_rows(a, rows):
    return jnp.pad(a, ((0, rows - a.shape[0]), (0, 0)))


def _pad_lanes(a):
    return jnp.pad(a, ((0, 0), (0, LANES - a.shape[1])))


def _local_grads(x, p, tgt, w_r, later, ssm_cw, conf_cw, sm):
    n_tok = x.shape[0]
    tl = min(256, n_tok)
    tlm = min(512, n_tok)
    row = lambda v: v.reshape(1, -1)
    g_e, b_e = row(sm["ln_emb_g"]), row(sm["ln_emb_b"])
    h0b, xbc_in, z, glu, cgate, dtr, w_out_all, wpg_all, wpp_all = _ln_inproj(x, g_e, b_e, w_r, later, tlm)
    w_out_b = w_out_all.reshape(D_SSM + D_CONF, D_MODEL)
    wpg_b = wpg_all.reshape(D_MODEL, D_MODEL)
    wpp_b = wpp_all.transpose(1, 0, 2).reshape(D_PLE, D_MODEL)

    cw4 = _pad_rows(ssm_cw, SUBLANES)
    dtb, alog = _pad_lanes(sm["dt_bias"]), _pad_lanes(sm["a_log"])
    dsk = jnp.repeat(sm["d_skip"], HEAD_DIM, axis=1)
    pre, y, yssm, hprev = _ssd_fwd(xbc_in, z, dtr, cw4, sm["ssm_conv_b"], dtb, alog, dsk, sm["ssm_norm_g"], tl)

    cw31 = _pad_rows(conf_cw, CONF_HALO)
    u0, u1, yconf = _conf_fwd(glu, cgate, sm["b_glu"], cw31, sm["conf_conv_b"], sm["conf_ln_g"], sm["conf_ln_b"], tl)

    vec = jnp.concatenate([g_e, b_e, sm["b_out"], sm["ln1_g"], sm["ln1_b"], sm["ln2_g"], sm["ln2_b"],
                           jnp.zeros((1, D_MODEL), F32)], axis=0)
    dmix, dr1, dr1b, h1b, dgb, dpb, small_t, loss = _tail(
        x, yssm, yconf, p, tgt, vec, w_out_b, wpg_b, wpp_b, tlm)

    dglu, dcg, dcw31, dbglu, small_c = _conf_bwd(dmix, D_SSM // D_CONF, u0, u1, glu, cgate, sm["b_glu"], cw31,
                                                  sm["conf_ln_g"], sm["conf_ln_b"], tl)
    dxin, dz, ddtr, dcw4, dcb4, dgn, ddsk, dalog, ddtb = _ssd_bwd(
        dmix, y, z, pre, xbc_in, dtr, hprev, cw4, dtb, alog, dsk, sm["ssm_norm_g"], tl)
    dx, dprojb, small_e = _inproj_bwd(dxin, dz, dglu, dcg, ddtr, dr1, x, g_e, b_e, w_r, tl)

    dw_r = _tn_matmul(h0b, dprojb, "dw_in", D_INR // 3, tlm)
    dw_out = jnp.concatenate([_tn_matmul(yssm, dr1b, "dw_out_ssm", D_MODEL, tlm),
                              _tn_matmul(yconf, dr1b, "dw_out_conf", D_MODEL, tlm)], axis=0)
    dwpg = _tn_matmul(h1b, dgb, "dw_ple_gate", D_MODEL, tlm)
    dwpp = _tn_matmul(p.astype(BF16), dpb, "dw_ple_proj", D_MODEL, tlm)
    grads = dict(
        ln_emb_g=small_e[0], ln_emb_b=small_e[1], w_in=dw_r, ssm_conv_w=dcw4[0:SSM_K], ssm_conv_b=dcb4,
        dt_bias=ddtb[:, 0:N_HEADS], a_log=dalog[:, 0:N_HEADS], d_skip=ddsk[:, 0:N_HEADS], ssm_norm_g=dgn, b_glu=dbglu,
        conf_conv_w=dcw31[0:CONF_K], conf_conv_b=small_c[2:3], conf_ln_g=small_c[0:1], conf_ln_b=small_c[1:2],
        w_out=dw_out, b_out=small_t[0:1], ln1_g=small_t[1:2], ln1_b=small_t[2:3], w_ple_gate=dwpg, w_ple_proj=dwpp,
        ln2_g=small_t[3:4], ln2_b=small_t[4:5])
    return loss[0, 0], dx, grads


N_CHIPS = 4
N_DEV = 8
W_IN_SH = D_IN // N_CHIPS
BIG = (("w_in", D_MODEL, W_IN_SH), ("w_out", (D_SSM + D_CONF) // N_CHIPS, D_MODEL),
       ("w_ple_gate", D_MODEL // N_CHIPS, D_MODEL), ("w_ple_proj", D_PLE, D_MODEL // N_CHIPS))
SEGS = ((R_XBC, 0, D_SSM), (R_XBC + D_SSM, 2048, 256), (R_XBC + D_SSM + 256, 2304, 256), (R_Z, 1024, D_SSM),
        (R_GLU, 2576, 2 * D_CONF), (R_CG, 4624, D_CONF), (R_DT, 2560, N_HEADS))
ROWS_CW4 = 2
ROWS_CW31 = 8
ROWS_CONV = 16
SMALL_ROWS = 56
SMALL_LAYOUT = (("ln_emb_g", 0, 1), ("ln_emb_b", 1, 1), ("ssm_conv_b", 2, 2), ("dt_bias", 4, 1), ("a_log", 5, 1),
                ("d_skip", 6, 1), ("ssm_norm_g", 7, 1), ("b_glu", 8, 2), ("conf_conv_b", 10, 1), ("conf_ln_g", 11, 1),
                ("conf_ln_b", 12, 1), ("b_out", 13, 1), ("ln1_g", 14, 1), ("ln1_b", 15, 1), ("ln2_g", 16, 1), ("ln2_b", 17, 1))
CONV_LAYOUT = (("ssm_conv_w", 18, 6, (SSM_K, D_XBC)), ("conf_conv_w", 24, 31, (CONF_K, D_CONF)))


def _rows_of(v, rows):
    flat = v.reshape(-1)
    return jnp.pad(flat, (0, rows * D_MODEL - flat.shape[0])).reshape(rows, D_MODEL)


LOSS_ROW = 55


def _pack_small(d, loss_share=None):
    parts = [_rows_of(d[n], r) for n, _, r in SMALL_LAYOUT]
    for n, _, r, _ in CONV_LAYOUT:
        parts.append(_rows_of(d[n], r) if n in d else jnp.zeros((r, D_MODEL), F32))
    parts.append(_rows_of(jnp.zeros((1,), F32) if loss_share is None else loss_share, SMALL_ROWS - LOSS_ROW))
    return jnp.concatenate(parts, axis=0)


def _w_r_from_shards(w4):
    parts = []
    for _, o, wd in SEGS:
        lo, hi = o, o + wd
        while lo < hi:
            s = lo // W_IN_SH
            e = min(hi, (s + 1) * W_IN_SH)
            parts.append(w4[s][:, lo - s * W_IN_SH:e - s * W_IN_SH])
            lo = e
    parts.append(jnp.zeros((D_MODEL, LANES - N_HEADS), w4.dtype))
    return jnp.concatenate(parts, axis=1)


def _shards_from_dw_r(dw_r):
    slabs = []
    for s in range(N_CHIPS):
        lo, hi = s * W_IN_SH, (s + 1) * W_IN_SH
        parts = []
        for kcol, o, wd in sorted(SEGS, key=lambda t: t[1]):
            a, b = max(lo, o), min(hi, o + wd)
            if a < b:
                parts.append(dw_r[:, kcol + a - o:kcol + b - o])
        slabs.append(jnp.concatenate(parts, axis=1))
    return jnp.stack(slabs)


def _row_chunks(rows, n):
    return [(j * (rows // n), rows // n) for j in range(n)]


def _my_place():
    return lax.axis_index("x"), lax.axis_index("y"), lax.axis_index("c")


MESH_ID = pl.DeviceIdType.MESH
ANY = pl.BlockSpec(memory_space=pl.ANY)
IN_VMEM = pl.BlockSpec(memory_space=pltpu.VMEM)
CHIP_FLIPS = ((1, 0), (0, 1), (1, 1))


def _remote(src, dst, send_sem, recv_sem, peer):
    return pltpu.make_async_remote_copy(src, dst, send_sem, recv_sem, device_id=peer, device_id_type=MESH_ID)


GATHER_CHUNKS = (4, 2, 1, 1)


def _gather_plans(kinds):
    plan = [(a, o, n, rows // 2) for a, (rows, ch) in enumerate(kinds) for o, n in _row_chunks(rows // 2, ch)]
    own_plan = [(a, o, n) for a, (rows, ch) in enumerate(kinds) for o, n in _row_chunks(rows, 2 * ch)]
    return plan, own_plan


def _gather_sems(plan, own_plan):
    hop = pltpu.SemaphoreType.DMA((3, len(plan)))
    return [hop, hop, hop, hop, pltpu.SemaphoreType.DMA((len(own_plan),))]


def _gather_stage(stage, ins, outs, plan, own_plan, sems):
    send_a, recv_a, send_b, recv_b, loc = sems
    x, y, c = _my_place()
    s = 2 * x + y
    sibling = (x, y, 1 - c)
    own = [pltpu.make_async_copy(ins[a].at[pl.ds(o, n)], outs[a].at[s, pl.ds(o, n)], loc.at[j])
           for j, (a, o, n) in enumerate(own_plan)]
    first, arrive, passed, arrive_b = [], [], [], []
    for k, (fx, fy) in enumerate(CHIP_FLIPS):
        peer = (x ^ fx, y ^ fy, c)
        sk = 2 * (x ^ fx) + (y ^ fy)
        for j, (a, o, n, h) in enumerate(plan):
            mine = pl.ds(pl.multiple_of(c * h + o, 16), n)
            theirs = pl.ds(pl.multiple_of((1 - c) * h + o, 16), n)
            first.append(_remote(ins[a].at[mine], outs[a].at[s, mine], send_a.at[k, j], recv_a.at[k, j], peer))
            land = outs[a].at[sk, mine]
            arrive.append(_remote(land, land, send_a.at[k, j], recv_a.at[k, j], peer))
            passed.append(_remote(land, land, send_b.at[k, j], recv_b.at[k, j], sibling))
            land_b = outs[a].at[sk, theirs]
            arrive_b.append(_remote(land_b, land_b, send_b.at[k, j], recv_b.at[k, j], sibling))
    if stage == 0:
        for cp in own + first:
            cp.start()
    elif stage == 1:
        for got, fwd in zip(arrive, passed):
            got.wait_recv()
            fwd.start()
    else:
        for got in arrive_b:
            got.wait_recv()
        for cp in first + passed:
            cp.wait_send()
        for cp in own:
            cp.wait()


def _gather_weights(w_in_b, conv_f):
    plan, own_plan = _gather_plans([(BIG[0][1], GATHER_CHUNKS[0])])

    def body(w_ref, conv_ref, w_all, conv_all, *sems):
        conv_send, conv_recv, conv_loc = sems[5:]
        x, y, c = _my_place()
        s = 2 * x + y
        conv_own = pltpu.make_async_copy(conv_ref, conv_all.at[s], conv_loc)
        conv_out = [_remote(conv_ref, conv_all.at[s], conv_send.at[k], conv_recv.at[k], (x ^ fx, y ^ fy, c))
                    for k, (fx, fy) in enumerate(CHIP_FLIPS)]
        _gather_stage(0, [w_ref], [w_all], plan, own_plan, sems[0:5])
        for cp in [conv_own] + conv_out:
            cp.start()
        _gather_stage(1, [w_ref], [w_all], plan, own_plan, sems[0:5])
        _gather_stage(2, [w_ref], [w_all], plan, own_plan, sems[0:5])
        for cp in conv_out:
            cp.wait()
        conv_own.wait()

    arrays = [w_in_b, conv_f]
    return pl.pallas_call(
        body, name="gather_weights", in_specs=[IN_VMEM] * 2, out_specs=[ANY] * 2,
        out_shape=[jax.ShapeDtypeStruct((N_CHIPS,) + a.shape, a.dtype) for a in arrays],
        scratch_shapes=_gather_sems(plan, own_plan)
        + [pltpu.SemaphoreType.DMA((3,)), pltpu.SemaphoreType.DMA((3,)), pltpu.SemaphoreType.DMA],
    )(*arrays)


CORES_CHUNKS = (4, 2, 1, 1)
CHIPS_CHUNKS = (2, 1, 1, 1)
SHARE_CHUNKS = (8, 4, 2, 1)


def _exchange_cores(gbig, spack):
    n_big = len(gbig)
    plan = [(a, t, o, n, g.shape[1] // 2) for a, g in enumerate(gbig) for t in range(g.shape[0])
            for o, n in _row_chunks(g.shape[1] // 2, CORES_CHUNKS[a] * N_CHIPS // g.shape[0])]

    def body(*refs):
        g_refs, s_ref = refs[0:n_big], refs[n_big]
        theirs, small_all = refs[n_big + 1:2 * n_big + 1], refs[2 * n_big + 1]
        send_sems, recv_sems, send_small, recv_small, loc_sem = refs[2 * n_big + 2:]
        x, y, c = _my_place()
        me = 4 * x + 2 * y + c
        own = pltpu.make_async_copy(s_ref, small_all.at[me], loc_sem)
        sends = []
        for j, (a, t, o, n, h) in enumerate(plan):
            give = pl.multiple_of((1 - c) * h + o, SUBLANES)
            sends.append(_remote(g_refs[a].at[t, pl.ds(give, n)], theirs[a].at[t, pl.ds(o, n)], send_sems.at[j], recv_sems.at[j],
                                 (x, y, 1 - c)))
        for m in range(1, N_DEV):
            peer = (x ^ (m >> 2), y ^ ((m >> 1) & 1), c ^ (m & 1))
            sends.append(_remote(s_ref, small_all.at[me], send_small.at[m - 1], recv_small.at[m - 1], peer))
        own.start()
        for cp in sends:
            cp.start()
        for cp in sends:
            cp.wait()
        own.wait()

    return pl.pallas_call(
        body, name="exchange_cores", in_specs=[ANY] * n_big + [IN_VMEM], out_specs=[ANY] * (n_big + 1),
        out_shape=[jax.ShapeDtypeStruct((g.shape[0], g.shape[1] // 2, g.shape[2]), F32) for g in gbig]
        + [jax.ShapeDtypeStruct((N_DEV, SMALL_ROWS, D_MODEL), F32)],
        scratch_shapes=[pltpu.SemaphoreType.DMA((len(plan),)), pltpu.SemaphoreType.DMA((len(plan),)),
                        pltpu.SemaphoreType.DMA((N_DEV - 1,)), pltpu.SemaphoreType.DMA((N_DEV - 1,)), pltpu.SemaphoreType.DMA],
    )(*gbig, spack)


def _exchange_chips(psums):
    n_big = len(BIG)
    plan = [(a, o, n) for a, (_, rows, _) in enumerate(BIG) for o, n in _row_chunks(rows // 2, CHIPS_CHUNKS[a])]

    def body(*refs):
        p_refs, got = refs[0:n_big], refs[n_big:2 * n_big]
        send_sems, recv_sems = refs[2 * n_big:]
        x, y, c = _my_place()
        sends = []
        for k, (fx, fy) in enumerate(CHIP_FLIPS):
            tx, ty = x ^ fx, y ^ fy
            for j, (a, o, n) in enumerate(plan):
                sends.append(_remote(p_refs[a].at[2 * tx + ty, pl.ds(o, n)], got[a].at[k, pl.ds(o, n)], send_sems.at[k, j],
                                     recv_sems.at[k, j], (tx, ty, c)))
        for cp in sends:
            cp.start()
        for cp in sends:
            cp.wait()

    return pl.pallas_call(
        body, name="exchange_chips", in_specs=[ANY] * n_big, out_specs=[ANY] * n_big,
        out_shape=[jax.ShapeDtypeStruct((3, rows // 2, cols), BF16) for _, rows, cols in BIG],
        scratch_shapes=[pltpu.SemaphoreType.DMA((3, len(plan))), pltpu.SemaphoreType.DMA((3, len(plan)))],
    )(*psums)


def _share_halves(tots):
    n_big = len(BIG)
    plan = [(a, o, n) for a, (_, rows, _) in enumerate(BIG) for o, n in _row_chunks(rows // 2, SHARE_CHUNKS[a])]

    def body(*refs):
        t_refs, both = refs[0:n_big], refs[n_big:2 * n_big]
        send_sems, recv_sems, loc = refs[2 * n_big:]
        x, y, c = _my_place()
        own = [pltpu.make_async_copy(t_refs[a].at[pl.ds(o, n)], both[a].at[c, pl.ds(o, n)], loc.at[j])
               for j, (a, o, n) in enumerate(plan)]
        sends = [_remote(t_refs[a].at[pl.ds(o, n)], both[a].at[c, pl.ds(o, n)], send_sems.at[j], recv_sems.at[j], (x, y, 1 - c))
                 for j, (a, o, n) in enumerate(plan)]
        for cp in own + sends:
            cp.start()
        for cp in sends:
            cp.wait()
        for cp in own:
            cp.wait()

    return pl.pallas_call(
        body, name="share_halves", in_specs=[IN_VMEM] * n_big, out_specs=[ANY] * n_big,
        out_shape=[jax.ShapeDtypeStruct((2, rows // 2, cols), F32) for _, rows, cols in BIG],
        scratch_shapes=[pltpu.SemaphoreType.DMA((len(plan),)), pltpu.SemaphoreType.DMA((len(plan),)),
                        pltpu.SemaphoreType.DMA((len(plan),))],
    )(*tots)


def _sum_rows(half):
    return half if half <= 256 else 256


def _add_cores(core, g, theirs, name):
    n_slabs, rows, cols = g.shape
    half = rows // 2
    tr = _sum_rows(half)
    nb = half // tr

    def body(c_ref, g_ref, t_ref, o_ref):
        o_ref[...] = (g_ref[...] + t_ref[...]).astype(BF16)

    grid_spec = pltpu.PrefetchScalarGridSpec(
        num_scalar_prefetch=1, grid=(n_slabs, nb),
        in_specs=[pl.BlockSpec((1, tr, cols), lambda t, i, c_ref: (t, c_ref[0] * nb + i, 0)),
                  pl.BlockSpec((1, tr, cols), lambda t, i, c_ref: (t, i, 0))],
        out_specs=pl.BlockSpec((1, tr, cols), lambda t, i, c_ref: (t, i, 0)))
    return pl.pallas_call(body, grid_spec=grid_spec, name=name, out_shape=jax.ShapeDtypeStruct((n_slabs, half, cols), BF16),
                          compiler_params=pltpu.CompilerParams(vmem_limit_bytes=VMEM_LIMIT))(core, g, theirs)


def _add_chips(chip, psum, got, name):
    _, half, cols = psum.shape
    tr = _sum_rows(half)

    def body(s_ref, p_ref, g_ref, o_ref):
        o_ref[...] = ((p_ref[0].astype(F32) + g_ref[0].astype(F32)) + g_ref[1].astype(F32)) + g_ref[2].astype(F32)

    grid_spec = pltpu.PrefetchScalarGridSpec(
        num_scalar_prefetch=1, grid=(half // tr,),
        in_specs=[pl.BlockSpec((1, tr, cols), lambda i, s_ref: (s_ref[0], i, 0)),
                  pl.BlockSpec((3, tr, cols), lambda i, s_ref: (0, i, 0))],
        out_specs=pl.BlockSpec((tr, cols), lambda i, s_ref: (i, 0)))
    return pl.pallas_call(body, grid_spec=grid_spec, name=name, out_shape=jax.ShapeDtypeStruct((half, cols), F32),
                          compiler_params=pltpu.CompilerParams(vmem_limit_bytes=VMEM_LIMIT))(chip, psum, got)


def _adam_math(w, g, m, v):
    m = ADAM_B1 * m + (1.0 - ADAM_B1) * g
    v = ADAM_B2 * v + (1.0 - ADAM_B2) * (g * g)
    m_hat = m / (1.0 - ADAM_B1 ** ADAM_STEP)
    v_hat = v / (1.0 - ADAM_B2 ** ADAM_STEP)
    return -ADAM_LR * (m_hat / (jnp.sqrt(v_hat) + ADAM_EPS) + ADAM_WD * w), m, v


def _adam(w, g, m, v, name):
    rows, cols = w.shape
    tr = rows if rows <= 256 else 256

    def body(w_ref, g_ref, m_ref, v_ref, d_ref, nm_ref, nv_ref):
        d_ref[...], nm_ref[...], nv_ref[...] = _adam_math(w_ref[...], g_ref[...], m_ref[...], v_ref[...])

    spec = pl.BlockSpec((tr, cols), lambda i: (i, 0))
    return pl.pallas_call(body, grid=(rows // tr,), name=name, in_specs=[spec] * 4, out_specs=[spec] * 3,
                          out_shape=[jax.ShapeDtypeStruct(w.shape, F32)] * 3, compiler_params=_params(seq=False))(w, g, m, v)


def _adam_small(parts, ws, ms, vs):
    n_w = len(SMALL_LAYOUT)

    def body(p_ref, *refs):
        w_refs, m_refs, v_refs = refs[0:n_w], refs[n_w:2 * n_w], refs[2 * n_w:3 * n_w]
        sum_ref = refs[3 * n_w]
        outs = refs[3 * n_w + 1:]
        g = p_ref[0]
        for k in range(1, N_DEV):
            g = g + p_ref[k]
        sum_ref[...] = g
        for a, (_, r0, rows) in enumerate(SMALL_LAYOUT):
            width = w_refs[a].shape[1]
            for j in range(rows):
                lo, hi = j * D_MODEL, min((j + 1) * D_MODEL, width)
                gj = sum_ref[r0 + j:r0 + j + 1, 0:hi - lo]
                d, nm, nv = _adam_math(w_refs[a][:, lo:hi], gj, m_refs[a][:, lo:hi], v_refs[a][:, lo:hi])
                for out, val in zip(outs[4 * a:4 * a + 4], (gj, d, nm, nv)):
                    out[:, lo:hi] = val

    shapes = [jax.ShapeDtypeStruct(parts.shape[1:], F32)]
    for wa in ws:
        shapes += [jax.ShapeDtypeStruct(wa.shape, F32)] * 4
    res = pl.pallas_call(body, name="adam_small", out_shape=shapes)(parts, *ws, *ms, *vs)
    return res[0], [res[1 + 4 * a:5 + 4 * a] for a in range(n_w)]


def kernel(x, p, ln_emb_g, ln_emb_b, w_in, ssm_conv_w, ssm_conv_b, dt_bias, a_log, d_skip, ssm_norm_g, b_glu, conf_conv_w, conf_conv_b, conf_ln_g, conf_ln_b, w_out, b_out, ln1_g, ln1_b, w_ple_gate, w_ple_proj, ln2_g, ln2_b, loss_target, m_ln_emb_g, m_ln_emb_b, m_w_in, m_ssm_conv_w, m_ssm_conv_b, m_dt_bias, m_a_log, m_d_skip, m_ssm_norm_g, m_b_glu, m_conf_conv_w, m_conf_conv_b, m_conf_ln_g, m_conf_ln_b, m_w_out, m_b_out, m_ln1_g, m_ln1_b, m_w_ple_gate, m_w_ple_proj, m_ln2_g, m_ln2_b, v_ln_emb_g, v_ln_emb_b, v_w_in, v_ssm_conv_w, v_ssm_conv_b, v_dt_bias, v_a_log, v_d_skip, v_ssm_norm_g, v_b_glu, v_conf_conv_w, v_conf_conv_b, v_conf_ln_g, v_conf_ln_b, v_w_out, v_b_out, v_ln1_g, v_ln1_b, v_w_ple_gate, v_w_ple_proj, v_ln2_g, v_ln2_b):
    order = ("ln_emb_g", "ln_emb_b", "w_in", "ssm_conv_w", "ssm_conv_b", "dt_bias", "a_log", "d_skip", "ssm_norm_g", "b_glu",
             "conf_conv_w", "conf_conv_b", "conf_ln_g", "conf_ln_b", "w_out", "b_out", "ln1_g", "ln1_b", "w_ple_gate",
             "w_ple_proj", "ln2_g", "ln2_b")
    w = dict(zip(order, (ln_emb_g, ln_emb_b, w_in, ssm_conv_w, ssm_conv_b, dt_bias, a_log, d_skip, ssm_norm_g, b_glu,
                         conf_conv_w, conf_conv_b, conf_ln_g, conf_ln_b, w_out, b_out, ln1_g, ln1_b, w_ple_gate, w_ple_proj,
                         ln2_g, ln2_b)))
    m = dict(zip(order, (m_ln_emb_g, m_ln_emb_b, m_w_in, m_ssm_conv_w, m_ssm_conv_b, m_dt_bias, m_a_log, m_d_skip,
                         m_ssm_norm_g, m_b_glu, m_conf_conv_w, m_conf_conv_b, m_conf_ln_g, m_conf_ln_b, m_w_out, m_b_out,
                         m_ln1_g, m_ln1_b, m_w_ple_gate, m_w_ple_proj, m_ln2_g, m_ln2_b)))
    v = dict(zip(order, (v_ln_emb_g, v_ln_emb_b, v_w_in, v_ssm_conv_w, v_ssm_conv_b, v_dt_bias, v_a_log, v_d_skip,
                         v_ssm_norm_g, v_b_glu, v_conf_conv_w, v_conf_conv_b, v_conf_ln_g, v_conf_ln_b, v_w_out, v_b_out,
                         v_ln1_g, v_ln1_b, v_w_ple_gate, v_w_ple_proj, v_ln2_g, v_ln2_b)))

    conv_f = jnp.concatenate([_rows_of(w["ssm_conv_w"], ROWS_CW4), _rows_of(w["conf_conv_w"], ROWS_CW31),
                              jnp.zeros((ROWS_CONV - ROWS_CW4 - ROWS_CW31, D_MODEL), F32)], axis=0)
    shards_b = [w[n][0].astype(BF16) for n, _, _ in BIG]
    w_in_all, conv_all = _gather_weights(shards_b[0], conv_f)
    w_r = _w_r_from_shards(w_in_all)
    cw4 = conv_all[:, 0:ROWS_CW4].reshape(N_CHIPS, -1)[:, :SSM_K * D_XBC // N_CHIPS]
    cw4 = cw4.reshape(N_CHIPS, SSM_K, D_XBC // N_CHIPS).transpose(1, 0, 2).reshape(SSM_K, D_XBC)
    cw31 = conv_all[:, ROWS_CW4:ROWS_CW4 + ROWS_CW31].reshape(N_CHIPS, -1)[:, :CONF_K * D_CONF // N_CHIPS]
    cw31 = cw31.reshape(N_CHIPS, CONF_K, D_CONF // N_CHIPS).transpose(1, 0, 2).reshape(CONF_K, D_CONF)

    small_names = [n for n, _, _ in SMALL_LAYOUT]
    sm = {n: w[n] for n in small_names}
    loss_part, dx, grads = _local_grads(x[0], p[0, 0], loss_target[0], w_r, shards_b[1:], cw4, cw31, sm)

    core = lax.axis_index("c").astype(jnp.int32).reshape(1)
    chip_i = 2 * lax.axis_index("x") + lax.axis_index("y")
    chip = chip_i.astype(jnp.int32).reshape(1)
    gbig = [grads["w_in"][None], grads["w_out"].reshape(N_CHIPS, -1, D_MODEL),
            grads["w_ple_gate"].reshape(N_CHIPS, -1, D_MODEL),
            grads["w_ple_proj"].reshape(D_PLE, N_CHIPS, D_MODEL // N_CHIPS).transpose(1, 0, 2)]
    spack = _pack_small({n: grads[n] for n in small_names + [n for n, _, _, _ in CONV_LAYOUT]}, loss_part)
    *theirs, small_all = _exchange_cores(gbig, spack)
    psums = [_add_cores(core, g, t, "add_cores_" + n) for g, t, (n, _, _) in zip(gbig, theirs, BIG)]
    psums[0] = _shards_from_dw_r(psums[0][0])
    got = _exchange_chips(psums)
    tots = [_add_chips(chip, ps, gt, "add_chips_" + n) for ps, gt, (n, _, _) in zip(psums, got, BIG)]
    both = _share_halves(tots)

    out_g, out_d, out_m, out_v = {}, {}, {}, {}
    as_row = lambda a: a.reshape(1, -1)
    g_s, small_out = _adam_small(small_all, [as_row(w[n]) for n in small_names], [as_row(m[n]) for n in small_names],
                                 [as_row(v[n]) for n in small_names])
    loss = g_s[LOSS_ROW, 0]
    g_shard = {n: b.reshape(rows, cols) for b, (n, rows, cols) in zip(both, BIG)}
    for n, r0, r, shape in CONV_LAYOUT:
        whole = g_s[r0:r0 + r].reshape(-1)[:shape[0] * shape[1]].reshape(shape)
        g_shard[n] = lax.dynamic_slice_in_dim(whole, chip_i * (shape[1] // N_CHIPS), shape[1] // N_CHIPS, axis=1)
    for n, g in g_shard.items():
        shape = w[n].shape
        d, nm, nv = _adam(w[n][0], g, m[n][0], v[n][0], "adam_" + n)
        out_g[n], out_d[n], out_m[n], out_v[n] = g[None], d[None], nm[None], nv[None]
    for n, four in zip(small_names, small_out):
        out_g[n], out_d[n], out_m[n], out_v[n] = [a.reshape(w[n].shape) for a in four]
    return (loss, dx[None], *[out_g[n] for n in order], *[out_d[n] for n in order], *[out_m[n] for n in order],
            *[out_v[n] for n in order])
```

```python
import functools

import jax
import jax.numpy as jnp
from jax import lax
from jax.experimental import pallas as pl
from jax.experimental.pallas import tpu as pltpu

F32 = jnp.float32
BF16 = jnp.bfloat16

D_MODEL = 1024
D_PLE = 256
D_SSM = 1024
D_CONF = 1024
N_HEADS = 16
HEAD_DIM = 64
N_GROUPS = 2
N_STATE = 128
CHUNK = 128
SSM_K = 4
CONF_K = 31
D_XBC = D_SSM + 2 * N_GROUPS * N_STATE
D_IN = 5648
R_XBC, R_Z, R_GLU, R_CG, R_DT, D_INR = 0, 1536, 2560, 4608, 5632, 5760
LN_EPS = 1e-5
RMS_EPS = 1e-5
ALPHA = 2.0 ** 0.25
ADAM_LR, ADAM_B1, ADAM_B2, ADAM_EPS, ADAM_WD, ADAM_STEP = 0.001, 0.9, 0.999, 1e-08, 0.01, 10
NEG_BIG = -1e30
LANES = 128
SUBLANES = 8
VMEM_LIMIT = 56 * 1024 * 1024
HIGHEST = lax.Precision.HIGHEST
NT_DIMS = (((1,), (1,)), ((), ()))
TN_DIMS = (((0,), (0,)), ((), ()))


def _sig(v):
    return jax.nn.sigmoid(v)


def _dsilu(v, s):
    return s * (1.0 + v * (1.0 - s))


def _ln_stats(v):
    mu = jnp.mean(v, axis=-1, keepdims=True)
    c = v - mu
    var = jnp.mean(c * c, axis=-1, keepdims=True)
    rstd = lax.rsqrt(var + LN_EPS)
    return c * rstd, rstd


def _ln_bwd(dxhat, xhat, rstd):
    m1 = jnp.mean(dxhat, axis=-1, keepdims=True)
    m2 = jnp.mean(dxhat * xhat, axis=-1, keepdims=True)
    return rstd * (dxhat - m1 - xhat * m2)


def _softplus(v):
    return jnp.maximum(v, 0.0) + jnp.log1p(jnp.exp(-jnp.abs(v)))


def _colsum(v):
    return jnp.sum(v, axis=0, keepdims=True)


def _dot(a, b):
    return jnp.dot(a, b, preferred_element_type=F32)


def _dot_nt(a, b):
    return lax.dot_general(a, b, NT_DIMS, preferred_element_type=F32)


def _tile(tl, c, rev_of=None):
    if rev_of is None:
        return pl.BlockSpec((tl, c), lambda i: (i, 0))
    return pl.BlockSpec((tl, c), lambda i: (rev_of - 1 - i, 0))


def _full(shape, single=False):
    nd = len(shape)
    if single:
        return pl.BlockSpec(shape, lambda i: (0,) * nd, pipeline_mode=pl.Buffered(1))
    return pl.BlockSpec(shape, lambda i: (0,) * nd)


def _params(seq=True):
    return pltpu.CompilerParams(dimension_semantics=("arbitrary",) if seq else ("parallel",), vmem_limit_bytes=VMEM_LIMIT)


def _ln_inproj(x, g, b, w_r, later, tl):
    n_tok = x.shape[0]
    n_t = n_tok // tl
    n_l = len(later)
    plan, own_plan = _gather_plans([(rows, GATHER_CHUNKS[a]) for a, (_, rows, _) in enumerate(BIG)][1:])

    def body(x_ref, g_ref, b_ref, w_ref, *rest):
        part_refs = rest[0:n_l]
        h0b_ref, xbc_ref, z_ref, glu_ref, cg_ref, dtr_ref = rest[n_l:n_l + 6]
        all_refs, sems = rest[n_l + 6:2 * n_l + 6], rest[2 * n_l + 6:]
        i = pl.program_id(0)

        @pl.when(i == 0)
        def _():
            _gather_stage(0, part_refs, all_refs, plan, own_plan, sems)

        xhat, _ = _ln_stats(x_ref[...])
        hb = (xhat * g_ref[...] + b_ref[...]).astype(BF16)
        h0b_ref[...] = hb
        xbc_ref[...] = _dot(hb, w_ref[:, R_XBC:R_Z])
        z_ref[...] = _dot(hb, w_ref[:, R_Z:R_GLU])
        glu_ref[...] = _dot(hb, w_ref[:, R_GLU:R_CG])
        cg_ref[...] = _dot(hb, w_ref[:, R_CG:R_DT])
        dtr_ref[...] = _dot(hb, w_ref[:, R_DT:D_INR])

        @pl.when(i == (3 * n_t) // 4)
        def _():
            _gather_stage(1, part_refs, all_refs, plan, own_plan, sems)

        @pl.when(i == n_t - 1)
        def _():
            _gather_stage(2, part_refs, all_refs, plan, own_plan, sems)

    widths = (D_MODEL, D_XBC, D_SSM, 2 * D_CONF, D_CONF, LANES)
    dtypes = (BF16, F32, F32, F32, F32, F32)
    return pl.pallas_call(
        body, grid=(n_t,), name="ln_inproj",
        in_specs=[_tile(tl, D_MODEL), _full((1, D_MODEL)), _full((1, D_MODEL)), _full((D_MODEL, D_INR), single=True)]
        + [IN_VMEM] * n_l,
        out_specs=[_tile(tl, w) for w in widths] + [ANY] * n_l,
        out_shape=[jax.ShapeDtypeStruct((n_tok, w), dt) for w, dt in zip(widths, dtypes)]
        + [jax.ShapeDtypeStruct((N_CHIPS,) + a.shape, a.dtype) for a in later],
        scratch_shapes=_gather_sems(plan, own_plan),
        compiler_params=_params(),
    )(x, g, b, w_r, *later)


def _chunk_common(adt_c):
    row = lax.broadcasted_iota(jnp.int32, (CHUNK, CHUNK), 0)
    col = lax.broadcasted_iota(jnp.int32, (CHUNK, CHUNK), 1)
    tril = row >= col
    acs = jnp.dot(tril.astype(F32), adt_c, precision=HIGHEST, preferred_element_type=F32)
    last = acs[CHUNK - 1:CHUNK, :]
    return dict(row=row, col=col, tril=tril, lo=col < HEAD_DIM, acs=acs, acs_t=acs.T, e=jnp.exp(acs),
                dec=jnp.exp(last - acs), cd=jnp.exp(last))


def _decay_mask(cm, h):
    return jnp.exp(jnp.where(cm["tril"], cm["acs"][:, h:h + 1] - cm["acs_t"][h:h + 1, :], NEG_BIG))


def _head_lane_matrix():
    return (jnp.arange(D_SSM)[None, :] // HEAD_DIM == jnp.arange(LANES)[:, None]).astype(BF16)


def _per_head_lanes(v, exp_ref):
    hi = v.astype(BF16)
    lo = (v - hi.astype(F32)).astype(BF16)
    return _dot(hi, exp_ref[...]) + _dot(lo, exp_ref[...])


SSM_BWD_SHIFTS = (1, 2, 3)


def _ssd_fwd(xbc_in, z, dtr, cw, cb, dtb, alog, dsk, gnorm, tl):
    n_tok = xbc_in.shape[0]
    nq = tl // CHUNK

    def body(xin_ref, z_ref, dtr_ref, cw_ref, cb_ref, dtb_ref, alog_ref, dsk_ref, gn_ref, exp_ref,
             pre_ref, y_ref, yssm_ref, hprev_ref, buf, hst):
        @pl.when(pl.program_id(0) == 0)
        def _():
            buf[0:SUBLANES, :] = jnp.zeros((SUBLANES, D_XBC), F32)
            hst[...] = jnp.zeros_like(hst)

        buf[SUBLANES:SUBLANES + tl, :] = xin_ref[...]
        pre = cb_ref[...] + jnp.zeros((tl, D_XBC), F32)
        for k in range(SSM_K):
            off = SUBLANES - (SSM_K - 1) + k
            pre = pre + buf[off:off + tl, :] * cw_ref[k:k + 1, :]
        buf[0:SUBLANES, :] = buf[tl:tl + SUBLANES, :]
        pre_ref[...] = pre
        xbc = pre * _sig(pre)
        dt = _softplus(dtr_ref[...] + dtb_ref[...])
        a = -jnp.exp(alog_ref[...])
        adt = dt * a
        for q in range(nq):
            r0 = q * CHUNK
            cm = _chunk_common(adt[r0:r0 + CHUNK, :])
            dt_x = _per_head_lanes(dt[r0:r0 + CHUNK, :], exp_ref)
            e_x = _per_head_lanes(cm["e"], exp_ref)
            dec_x = _per_head_lanes(cm["dec"], exp_ref)
            for g in range(N_GROUPS):
                bg = xbc[r0:r0 + CHUNK, D_SSM + g * N_STATE:D_SSM + (g + 1) * N_STATE].astype(BF16)
                cg_ = xbc[r0:r0 + CHUNK, D_SSM + (N_GROUPS + g) * N_STATE:D_SSM + (N_GROUPS + g + 1) * N_STATE].astype(BF16)
                gm = _dot_nt(cg_, bg)
                for k in range(N_HEADS // N_GROUPS // 2):
                    ha = (N_HEADS // N_GROUPS) * g + 2 * k
                    c0 = ha * HEAD_DIM
                    xh2 = xbc[r0:r0 + CHUNK, c0:c0 + LANES]
                    x2 = xh2 * dt_x[:, c0:c0 + LANES]
                    x2b = x2.astype(BF16)
                    ya = _dot((gm * _decay_mask(cm, ha)).astype(BF16), x2b)
                    yb = _dot((gm * _decay_mask(cm, ha + 1)).astype(BF16), x2b)
                    h2 = hst[c0:c0 + LANES, :]
                    hprev_ref[q, c0:c0 + LANES, :] = h2
                    z2 = _dot_nt(cg_, h2.astype(BF16))
                    y2 = jnp.where(cm["lo"], ya, yb) + z2 * e_x[:, c0:c0 + LANES] + dsk_ref[:, c0:c0 + LANES] * xh2
                    y_ref[r0:r0 + CHUNK, c0:c0 + LANES] = y2
                    s2 = _dot((x2 * dec_x[:, c0:c0 + LANES]).T.astype(BF16), bg)
                    cd2 = jnp.where(cm["row"] < HEAD_DIM, cm["cd"][:, ha:ha + 1], cm["cd"][:, ha + 1:ha + 2])
                    hst[c0:c0 + LANES, :] = cd2 * h2 + s2
        yv = y_ref[...]
        zv = z_ref[...]
        yz = yv * (zv * _sig(zv))
        gw = D_SSM // N_GROUPS
        for g in range(N_GROUPS):
            seg = yz[:, g * gw:(g + 1) * gw]
            r = lax.rsqrt(jnp.mean(seg * seg, axis=-1, keepdims=True) + RMS_EPS)
            yssm_ref[:, g * gw:(g + 1) * gw] = (seg * r * gn_ref[:, g * gw:(g + 1) * gw]).astype(BF16)

    return pl.pallas_call(
        body, grid=(n_tok // tl,), name="ssd_fwd",
        in_specs=[_tile(tl, D_XBC), _tile(tl, D_SSM), _tile(tl, LANES), _full((SUBLANES, D_XBC)), _full((1, D_XBC)),
                  _full((1, LANES)), _full((1, LANES)), _full((1, D_SSM)), _full((1, D_SSM)), _full((LANES, D_SSM))],
        out_specs=[_tile(tl, D_XBC), _tile(tl, D_SSM), _tile(tl, D_SSM),
                   pl.BlockSpec((nq, D_SSM, N_STATE), lambda i: (i, 0, 0))],
        out_shape=[jax.ShapeDtypeStruct((n_tok, D_XBC), F32), jax.ShapeDtypeStruct((n_tok, D_SSM), F32),
                   jax.ShapeDtypeStruct((n_tok, D_SSM), BF16), jax.ShapeDtypeStruct((n_tok // CHUNK, D_SSM, N_STATE), F32)],
        scratch_shapes=[pltpu.VMEM((tl + SUBLANES, D_XBC), F32), pltpu.VMEM((D_SSM, N_STATE), F32)],
        compiler_params=_params(),
    )(xbc_in, z, dtr, cw, cb, dtb, alog, dsk, gnorm, _head_lane_matrix())


def _ssd_bwd(dys, y, z, pre, xbc_in, dtr, hprev, cw, dtb, alog, dsk, gnorm, specs, grads_b, tl):
    n_tok = y.shape[0]
    n_t = n_tok // tl
    nq = tl // CHUNK
    n_g, n_r = len(grads_b), len(specs)

    def body(dys_ref, y_ref, z_ref, pre_ref, xin_ref, dtr_ref, hprev_ref, cw_ref, dtb_ref, alog_ref, dsk_ref, gn_ref,
             exp_ref, redx_ref, redq_ref, *rest):
        g_refs, rest = rest[0:n_g], rest[n_g:]
        dxin_ref, dz_ref, ddtr_ref, dcw_ref, dcb_ref, dgn_ref, ddsk_ref, da_ref, ddtb_ref = rest[0:9]
        recvs, rest = rest[9:9 + n_r], rest[9 + n_r:]
        dxs, dh, cs_s, dskc, shifts, dwp, sums_q, sums_s = rest[0:8]
        sems = rest[8:]
        i = pl.program_id(0)

        @pl.when(i == 0)
        def _():
            _direct_stage(0, specs, g_refs, recvs, sems)
            dcw_ref[...] = jnp.zeros_like(dcw_ref)
            dwp[...] = jnp.zeros_like(dwp)
            dcb_ref[...] = jnp.zeros_like(dcb_ref)
            dgn_ref[...] = jnp.zeros_like(dgn_ref)
            da_ref[...] = jnp.zeros_like(da_ref)
            ddtb_ref[...] = jnp.zeros_like(ddtb_ref)
            dskc[...] = jnp.zeros_like(dskc)
            dh[...] = jnp.zeros_like(dh)
            dxs[tl:tl + SUBLANES, :] = jnp.zeros((SUBLANES, D_XBC), F32)

        yv = y_ref[...]
        zv = z_ref[...]
        dysv = dys_ref[...]
        sz = _sig(zv)
        silz = zv * sz
        yz = yv * silz
        gw = D_SSM // N_GROUPS
        dyz_parts = []
        for g in range(N_GROUPS):
            sl = slice(g * gw, (g + 1) * gw)
            seg = yz[:, sl]
            r = lax.rsqrt(jnp.mean(seg * seg, axis=-1, keepdims=True) + RMS_EPS)
            yzn = seg * r
            dgn_ref[:, sl] += _colsum(dysv[:, sl] * yzn)
            dyzn = dysv[:, sl] * gn_ref[:, sl]
            dyz_parts.append(r * (dyzn - yzn * jnp.mean(dyzn * yzn, axis=-1, keepdims=True)))
        dyz = jnp.concatenate(dyz_parts, axis=1)
        dy = dyz * silz
        dz_ref[...] = dyz * yv * _dsilu(zv, sz)

        prev = pre_ref[...]
        sp = _sig(prev)
        xbc = prev * sp
        dskc[...] += _colsum(dy * xbc[:, 0:D_SSM])
        dt_in = dtr_ref[...] + dtb_ref[...]
        dt = _softplus(dt_in)
        dsp = _sig(dt_in)
        a = -jnp.exp(alog_ref[...])
        adt = dt * a
        for q in reversed(range(nq)):
            r0 = q * CHUNK
            cm = _chunk_common(adt[r0:r0 + CHUNK, :])
            row, col, lo = cm["row"], cm["col"], cm["lo"]
            triu = (col >= row).astype(F32)
            dt_c = dt[r0:r0 + CHUNK, :]
            dt_x = _per_head_lanes(dt_c, exp_ref)
            e_x = _per_head_lanes(cm["e"], exp_ref)
            dec_x = _per_head_lanes(cm["dec"], exp_ref)
            dcd_row = jnp.zeros((1, LANES), F32)
            for g in range(N_GROUPS):
                bcol = D_SSM + g * N_STATE
                ccol = D_SSM + (N_GROUPS + g) * N_STATE
                bg = xbc[r0:r0 + CHUNK, bcol:bcol + N_STATE].astype(BF16)
                cg_ = xbc[r0:r0 + CHUNK, ccol:ccol + N_STATE].astype(BF16)
                gm = _dot_nt(cg_, bg)
                dgm = jnp.zeros((CHUNK, CHUNK), F32)
                dbg = jnp.zeros((CHUNK, N_STATE), F32)
                dcg = jnp.zeros((CHUNK, N_STATE), F32)
                for k in range(N_HEADS // N_GROUPS // 2):
                    ha = (N_HEADS // N_GROUPS) * g + 2 * k
                    hb = ha + 1
                    c0 = ha * HEAD_DIM
                    xh2 = xbc[r0:r0 + CHUNK, c0:c0 + LANES]
                    dt2 = dt_x[:, c0:c0 + LANES]
                    x2 = xh2 * dt2
                    x2b = x2.astype(BF16)
                    la = _decay_mask(cm, ha)
                    lb = _decay_mask(cm, hb)
                    ma = gm * la
                    mb = gm * lb
                    dy2 = dy[r0:r0 + CHUNK, c0:c0 + LANES]
                    dy2b = dy2.astype(BF16)
                    dma = _dot_nt(jnp.where(lo, dy2, 0.0).astype(BF16), x2b)
                    dmb = _dot_nt(jnp.where(lo, 0.0, dy2).astype(BF16), x2b)
                    dx2 = jnp.where(lo, _dot(ma.T.astype(BF16), dy2b), _dot(mb.T.astype(BF16), dy2b))
                    dgm = dgm + dma * la + dmb * lb
                    sums_q[:, ha * CHUNK:(ha + 1) * CHUNK] = (dma * ma).astype(BF16)
                    sums_q[:, hb * CHUNK:(hb + 1) * CHUNK] = (dmb * mb).astype(BF16)
                    h2 = hprev_ref[q, c0:c0 + LANES, :]
                    h2b = h2.astype(BF16)
                    dz2 = dy2 * e_x[:, c0:c0 + LANES]
                    dcg = dcg + _dot(dz2.astype(BF16), h2b)
                    sums_s[0, :, c0:c0 + LANES] = (dz2 * _dot_nt(cg_, h2b)).astype(BF16)
                    dhn = dh[c0:c0 + LANES, :]
                    dhnb = dhn.astype(BF16)
                    cd_a = cm["cd"][:, ha:ha + 1]
                    cd_b = cm["cd"][:, hb:hb + 1]
                    top = row < HEAD_DIM
                    hh = dhn * h2
                    dcd_a = jnp.sum(_colsum(jnp.where(top, hh, 0.0)), axis=1, keepdims=True)
                    dcd_b = jnp.sum(_colsum(jnp.where(top, 0.0, hh)), axis=1, keepdims=True)
                    dcd_row = dcd_row + jnp.where(col[0:1, :] == ha, dcd_a * cd_a, 0.0) + jnp.where(col[0:1, :] == hb, dcd_b * cd_b, 0.0)
                    dh[c0:c0 + LANES, :] = jnp.where(top, cd_a, cd_b) * dhn + _dot(dz2.T.astype(BF16), cg_)
                    w2 = _dot_nt(bg, dhnb)
                    dec2 = dec_x[:, c0:c0 + LANES]
                    dx2 = dx2 + dec2 * w2
                    sums_s[1, :, c0:c0 + LANES] = (x2 * w2).astype(BF16)
                    dbg = dbg + _dot((x2 * dec2).astype(BF16), dhnb)
                    sums_s[2, :, c0:c0 + LANES] = (dx2 * xh2).astype(BF16)
                    dxs[r0:r0 + CHUNK, c0:c0 + LANES] = dx2 * dt2 + dsk_ref[:, c0:c0 + LANES] * dy2
                dgmb = dgm.astype(BF16)
                dxs[r0:r0 + CHUNK, bcol:bcol + N_STATE] = dbg + _dot(dgm.T.astype(BF16), cg_)
                dxs[r0:r0 + CHUNK, ccol:ccol + N_STATE] = dcg + _dot(dgmb, bg)
            q_all = sums_q[...]
            q_cols = _dot(jnp.ones((SUBLANES, CHUNK), BF16), q_all)
            cs_s[...] = jnp.zeros_like(cs_s)
            for h in range(N_HEADS):
                cs_s[h:h + 1, :] = q_cols[0:1, h * CHUNK:(h + 1) * CHUNK]
            de = _dot(sums_s[0], redx_ref[...])
            dd = _dot(sums_s[1], redx_ref[...]) * cm["dec"]
            ddtx = _dot(sums_s[2], redx_ref[...])
            is_last = row == CHUNK - 1
            dacs = _dot(q_all, redq_ref[...]) - cs_s[...].T + de - dd + jnp.where(is_last, dcd_row + _colsum(dd), 0.0)
            dadt = jnp.dot(triu, dacs, precision=HIGHEST, preferred_element_type=F32)
            da_ref[...] += _colsum(dadt * dt_c)
            ddtr_c = (dadt * a + ddtx) * dsp[r0:r0 + CHUNK, :]
            ddtr_ref[r0:r0 + CHUNK, :] = ddtr_c
            ddtb_ref[...] += _colsum(ddtr_c)

        dpre = dxs[0:tl, :] * _dsilu(prev, sp)
        dxs[0:tl, :] = dpre
        dcb_ref[...] += _colsum(dpre)
        _shift_copies(dxs, shifts, tl, SSM_BWD_SHIFTS)

        def strip(rb, carry):
            i0 = pl.multiple_of(rb * CONV_RS, CONV_RS)
            for c0 in range(0, D_XBC, CONV_CS):
                xin_s = xin_ref[pl.ds(i0, CONV_RS), c0:c0 + CONV_CS]
                acc = jnp.zeros((CONV_RS, CONV_CS), F32)
                for k in range(SSM_K):
                    sh = _tap_rows(dxs, shifts, SSM_K - 1 - k, i0, c0, SSM_BWD_SHIFTS)
                    acc = acc + sh * cw_ref[k:k + 1, c0:c0 + CONV_CS]
                    t = xin_s * sh
                    dwp[k * SUBLANES:(k + 1) * SUBLANES, c0:c0 + CONV_CS] += _fold_rows(t)
                dxin_ref[pl.ds(i0, CONV_RS), c0:c0 + CONV_CS] = acc
            return carry

        lax.fori_loop(0, tl // CONV_RS, strip, 0)
        dxs[tl:tl + SUBLANES, :] = dxs[0:SUBLANES, :]

        @pl.when(i == n_t - 1)
        def _():
            for k in range(SSM_K):
                dcw_ref[k:k + 1, :] = _colsum(dwp[k * SUBLANES:(k + 1) * SUBLANES, :])
            da_ref[...] = da_ref[...] * a
            sel = (lax.broadcasted_iota(jnp.int32, (D_SSM, LANES), 0) // HEAD_DIM
                   == lax.broadcasted_iota(jnp.int32, (D_SSM, LANES), 1)).astype(F32)
            rows = jnp.broadcast_to(dskc[...], (SUBLANES, D_SSM))
            ddsk_ref[...] = jnp.dot(rows, sel, precision=HIGHEST, preferred_element_type=F32)[0:1, :]
            _direct_stage(1, specs, g_refs, recvs, sems)

    head_lanes = _head_lane_matrix()
    per_head = (jnp.arange(N_HEADS * CHUNK)[:, None] // CHUNK == jnp.arange(LANES)[None, :]).astype(BF16)
    rev = functools.partial(_tile, tl, rev_of=n_t)
    return pl.pallas_call(
        body, grid=(n_t,), name="ssd_bwd",
        in_specs=[rev(D_SSM), rev(D_SSM), rev(D_SSM), rev(D_XBC), rev(D_XBC), rev(LANES),
                  pl.BlockSpec((nq, D_SSM, N_STATE), lambda i: (n_t - 1 - i, 0, 0)),
                  _full((SUBLANES, D_XBC)), _full((1, LANES)), _full((1, LANES)), _full((1, D_SSM)), _full((1, D_SSM)),
                  _full((LANES, D_SSM)), _full((D_SSM, LANES)), _full((N_HEADS * CHUNK, LANES))] + [IN_VMEM] * n_g,
        out_specs=[rev(D_XBC), rev(D_SSM), rev(LANES), _full((SUBLANES, D_XBC)), _full((1, D_XBC)), _full((1, D_SSM)),
                   _full((1, LANES)), _full((1, LANES)), _full((1, LANES))] + [ANY] * n_r,
        out_shape=[jax.ShapeDtypeStruct((n_tok, D_XBC), F32), jax.ShapeDtypeStruct((n_tok, D_SSM), F32),
                   jax.ShapeDtypeStruct((n_tok, LANES), F32), jax.ShapeDtypeStruct((SUBLANES, D_XBC), F32),
                   jax.ShapeDtypeStruct((1, D_XBC), F32), jax.ShapeDtypeStruct((1, D_SSM), F32),
                   jax.ShapeDtypeStruct((1, LANES), F32), jax.ShapeDtypeStruct((1, LANES), F32),
                   jax.ShapeDtypeStruct((1, LANES), F32)]
        + [jax.ShapeDtypeStruct((N_DEV, rows, cols), BF16) for rows, cols, _, _ in specs],
        scratch_shapes=[pltpu.VMEM((tl + SUBLANES, D_XBC), F32), pltpu.VMEM((D_SSM, N_STATE), F32),
                        pltpu.VMEM((CHUNK, LANES), F32), pltpu.VMEM((1, D_SSM), F32),
                        pltpu.VMEM((len(SSM_BWD_SHIFTS), tl, D_XBC), F32), pltpu.VMEM((SSM_K * SUBLANES, D_XBC), F32),
                        pltpu.VMEM((CHUNK, N_HEADS * CHUNK), BF16), pltpu.VMEM((3, CHUNK, D_SSM), BF16)]
        + _direct_sems(specs),
        compiler_params=_params(),
    )(dys, y, z, pre, xbc_in, dtr, hprev, cw, dtb, alog, dsk, gnorm, head_lanes, head_lanes.T, per_head, *grads_b)


CONF_HALO = 32
CONV_RS = 32
CONV_CS = 256


ALL_SHIFTS = tuple(range(1, SUBLANES))


def _shift_copies(buf, shifts, n_rows, residues=ALL_SHIFTS):
    for j, r in enumerate(residues):
        shifts[j, 0:n_rows, :] = buf[r:r + n_rows, :]


def _fold_rows(t):
    part = t[0:SUBLANES]
    for j in range(1, t.shape[0] // SUBLANES):
        part = part + t[j * SUBLANES:(j + 1) * SUBLANES]
    return part


def _tap_rows(buf, shifts, off, i0, c0, residues=ALL_SHIFTS):
    q, r = divmod(off, SUBLANES)
    rows = pl.ds(pl.multiple_of(i0 + SUBLANES * q, SUBLANES), CONV_RS)
    if r == 0:
        return buf[rows, c0:c0 + CONV_CS]
    return shifts[residues.index(r), rows, c0:c0 + CONV_CS]


def _conf_fwd(glu, cgate, bglu, cw, cb, lg, lb, tl):
    n_tok = glu.shape[0]

    def body(glu_ref, cg_ref, bglu_ref, cw_ref, cb_ref, lg_ref, lb_ref, u0_ref, u1_ref, yc_ref, buf, shifts):
        @pl.when(pl.program_id(0) == 0)
        def _():
            buf[0:CONF_HALO, :] = jnp.zeros((CONF_HALO, D_CONF), F32)

        gl = glu_ref[...] + bglu_ref[...]
        u0 = gl[:, 0:D_CONF] * _sig(gl[:, D_CONF:2 * D_CONF])
        u0_ref[...] = u0
        buf[CONF_HALO:CONF_HALO + tl, :] = u0
        _shift_copies(buf, shifts, tl + CONF_HALO - SUBLANES)

        def strip(rb, carry):
            i0 = pl.multiple_of(rb * CONV_RS, CONV_RS)
            for c0 in range(0, D_CONF, CONV_CS):
                acc = jnp.broadcast_to(cb_ref[:, c0:c0 + CONV_CS], (CONV_RS, CONV_CS))
                for k in range(CONF_K):
                    acc = acc + _tap_rows(buf, shifts, CONF_HALO - (CONF_K - 1) + k, i0, c0) * cw_ref[k:k + 1, c0:c0 + CONV_CS]
                u1_ref[pl.ds(i0, CONV_RS), c0:c0 + CONV_CS] = acc
            return carry

        lax.fori_loop(0, tl // CONV_RS, strip, 0)
        buf[0:CONF_HALO, :] = buf[tl:tl + CONF_HALO, :]
        xhat, _ = _ln_stats(u1_ref[...])
        n = xhat * lg_ref[...] + lb_ref[...]
        cgv = cg_ref[...]
        yc_ref[...] = (n * _sig(n) * (cgv * _sig(cgv))).astype(BF16)

    return pl.pallas_call(
        body, grid=(n_tok // tl,), name="conf_fwd",
        in_specs=[_tile(tl, 2 * D_CONF), _tile(tl, D_CONF), _full((1, 2 * D_CONF)), _full((CONF_HALO, D_CONF)),
                  _full((1, D_CONF)), _full((1, D_CONF)), _full((1, D_CONF))],
        out_specs=[_tile(tl, D_CONF)] * 3,
        out_shape=[jax.ShapeDtypeStruct((n_tok, D_CONF), F32), jax.ShapeDtypeStruct((n_tok, D_CONF), F32),
                   jax.ShapeDtypeStruct((n_tok, D_CONF), BF16)],
        scratch_shapes=[pltpu.VMEM((tl + CONF_HALO, D_CONF), F32),
                        pltpu.VMEM((SUBLANES - 1, tl + CONF_HALO - SUBLANES, D_CONF), F32)],
        compiler_params=_params(),
    )(glu, cgate, bglu, cw, cb, lg, lb)


def _conf_bwd(dyc, dyc_block, u0, u1, glu, cgate, bglu, cw, lg, lb, specs, grads_b, tl):
    n_tok = glu.shape[0]
    n_t = n_tok // tl
    n_g, n_r = len(grads_b), len(specs)

    def body(dyc_ref, u0_ref, u1_ref, glu_ref, cg_ref, bglu_ref, cw_ref, lg_ref, lb_ref, *rest):
        g_refs, rest = rest[0:n_g], rest[n_g:]
        dglu_ref, dcg_ref, dcw_ref, dbglu_ref, small_ref = rest[0:5]
        recvs, rest = rest[5:5 + n_r], rest[5 + n_r:]
        buf, shifts, du0_s, dwp = rest[0:4]
        sems = rest[4:]

        @pl.when(pl.program_id(0) == 0)
        def _():
            _direct_stage(0, specs, g_refs, recvs, sems)
            dwp[...] = jnp.zeros_like(dwp)
            dbglu_ref[...] = jnp.zeros_like(dbglu_ref)
            small_ref[...] = jnp.zeros_like(small_ref)
            buf[tl:tl + CONF_HALO, :] = jnp.zeros((CONF_HALO, D_CONF), F32)

        xhat, rstd = _ln_stats(u1_ref[...])
        n = xhat * lg_ref[...] + lb_ref[...]
        sn = _sig(n)
        cgv = cg_ref[...]
        scg = _sig(cgv)
        dycv = dyc_ref[...]
        dcg_ref[...] = dycv * (n * sn) * _dsilu(cgv, scg)
        dn = dycv * (cgv * scg) * _dsilu(n, sn)
        small_ref[0:1, :] += _colsum(dn * xhat)
        small_ref[1:2, :] += _colsum(dn)
        du1 = _ln_bwd(dn * lg_ref[...], xhat, rstd)
        small_ref[2:3, :] += _colsum(du1)
        buf[0:tl, :] = du1
        _shift_copies(buf, shifts, tl + CONF_HALO - SUBLANES)

        def strip(rb, carry):
            i0 = pl.multiple_of(rb * CONV_RS, CONV_RS)
            for c0 in range(0, D_CONF, CONV_CS):
                u0s = u0_ref[pl.ds(i0, CONV_RS), c0:c0 + CONV_CS]
                acc = jnp.zeros((CONV_RS, CONV_CS), F32)
                for k in range(CONF_K):
                    sh = _tap_rows(buf, shifts, CONF_K - 1 - k, i0, c0)
                    acc = acc + sh * cw_ref[k:k + 1, c0:c0 + CONV_CS]
                    t = u0s * sh
                    dwp[k * SUBLANES:(k + 1) * SUBLANES, c0:c0 + CONV_CS] += _fold_rows(t)
                du0_s[pl.ds(i0, CONV_RS), c0:c0 + CONV_CS] = acc
            return carry

        lax.fori_loop(0, tl // CONV_RS, strip, 0)
        du0 = du0_s[...]
        buf[tl:tl + CONF_HALO, :] = buf[0:CONF_HALO, :]
        gl = glu_ref[...] + bglu_ref[...]
        sg = _sig(gl[:, D_CONF:2 * D_CONF])
        dgv = du0 * sg
        dgg = du0 * gl[:, 0:D_CONF] * sg * (1.0 - sg)
        dglu_ref[:, 0:D_CONF] = dgv
        dglu_ref[:, D_CONF:2 * D_CONF] = dgg
        dbglu_ref[:, 0:D_CONF] += _colsum(dgv)
        dbglu_ref[:, D_CONF:2 * D_CONF] += _colsum(dgg)

        @pl.when(pl.program_id(0) == n_t - 1)
        def _():
            for k in range(CONF_HALO):
                dcw_ref[k:k + 1, :] = _colsum(dwp[k * SUBLANES:(k + 1) * SUBLANES, :])
            _direct_stage(1, specs, g_refs, recvs, sems)

    rev = functools.partial(_tile, tl, rev_of=n_t)
    return pl.pallas_call(
        body, grid=(n_t,), name="conf_bwd",
        in_specs=[pl.BlockSpec((tl, D_CONF), lambda i: (n_t - 1 - i, dyc_block)),
                  rev(D_CONF), rev(D_CONF), rev(2 * D_CONF), rev(D_CONF), _full((1, 2 * D_CONF)),
                  _full((CONF_HALO, D_CONF)), _full((1, D_CONF)), _full((1, D_CONF))] + [IN_VMEM] * n_g,
        out_specs=[rev(2 * D_CONF), rev(D_CONF), _full((CONF_HALO, D_CONF)), _full((1, 2 * D_CONF)), _full((SUBLANES, D_CONF))]
        + [ANY] * n_r,
        out_shape=[jax.ShapeDtypeStruct((n_tok, 2 * D_CONF), F32), jax.ShapeDtypeStruct((n_tok, D_CONF), F32),
                   jax.ShapeDtypeStruct((CONF_HALO, D_CONF), F32), jax.ShapeDtypeStruct((1, 2 * D_CONF), F32),
                   jax.ShapeDtypeStruct((SUBLANES, D_CONF), F32)]
        + [jax.ShapeDtypeStruct((N_DEV, rows, cols), BF16) for rows, cols, _, _ in specs],
        scratch_shapes=[pltpu.VMEM((tl + CONF_HALO, D_CONF), F32),
                        pltpu.VMEM((SUBLANES - 1, tl + CONF_HALO - SUBLANES, D_CONF), F32),
                        pltpu.VMEM((tl, D_CONF), F32), pltpu.VMEM((CONF_HALO * SUBLANES, D_CONF), F32)] + _direct_sems(specs),
        compiler_params=_params(),
    )(dyc, u0, u1, glu, cgate, bglu, cw, lg, lb, *grads_b)


def _tail(x, yssm, yconf, p, tgt, vec, w_out, wpg, wpp, tl):
    n_tok = x.shape[0]

    def body(x_ref, ys_ref, yc_ref, p_ref, t_ref, vec_ref, wo_ref, wg_ref, wp_ref,
             dmix_ref, dr1_ref, dr1b_ref, h1b_ref, dgb_ref, dpb_ref, small_ref, loss_ref):
        @pl.when(pl.program_id(0) == 0)
        def _():
            small_ref[...] = jnp.zeros_like(small_ref)
            loss_ref[...] = jnp.zeros_like(loss_ref)

        xh0, _ = _ln_stats(x_ref[...])
        h0 = xh0 * vec_ref[0:1, :] + vec_ref[1:2, :]
        out = _dot(ys_ref[...], wo_ref[0:D_SSM, :]) + _dot(yc_ref[...], wo_ref[D_SSM:D_SSM + D_CONF, :]) + vec_ref[2:3, :]
        xh1, rstd1 = _ln_stats(ALPHA * h0 + out)
        h1 = xh1 * vec_ref[3:4, :] + vec_ref[4:5, :]
        h1b = h1.astype(BF16)
        h1b_ref[...] = h1b
        gate = _sig(_dot(h1b, wg_ref[...]))
        ple = _dot(p_ref[...].astype(BF16), wp_ref[...])
        xh2, rstd2 = _ln_stats(ALPHA * h1 + gate * ple)
        h2 = xh2 * vec_ref[5:6, :] + vec_ref[6:7, :]
        diff = h2 - t_ref[...]
        part = jnp.sum(jnp.sum(diff * diff, axis=1, keepdims=True), axis=0, keepdims=True) * (0.5 / D_MODEL)
        loss_ref[...] += jnp.broadcast_to(part, loss_ref.shape)
        dh2 = diff * (1.0 / D_MODEL)
        small_ref[3:4, :] += _colsum(dh2 * xh2)
        small_ref[4:5, :] += _colsum(dh2)
        dr2 = _ln_bwd(dh2 * vec_ref[5:6, :], xh2, rstd2)
        dgpre = (dr2 * ple * gate * (1.0 - gate)).astype(BF16)
        dgb_ref[...] = dgpre
        dpb_ref[...] = (dr2 * gate).astype(BF16)
        dh1 = ALPHA * dr2 + _dot_nt(dgpre, wg_ref[...])
        small_ref[1:2, :] += _colsum(dh1 * xh1)
        small_ref[2:3, :] += _colsum(dh1)
        dr1 = _ln_bwd(dh1 * vec_ref[3:4, :], xh1, rstd1)
        small_ref[0:1, :] += _colsum(dr1)
        dr1_ref[...] = dr1
        dr1b = dr1.astype(BF16)
        dr1b_ref[...] = dr1b
        dmix_ref[...] = _dot_nt(dr1b, wo_ref[...])

    d_mix = D_SSM + D_CONF
    return pl.pallas_call(
        body, grid=(n_tok // tl,), name="tail",
        in_specs=[_tile(tl, D_MODEL), _tile(tl, D_SSM), _tile(tl, D_CONF), _tile(tl, D_PLE), _tile(tl, D_MODEL),
                  _full((SUBLANES, D_MODEL)), _full((d_mix, D_MODEL), True), _full((D_MODEL, D_MODEL), True),
                  _full((D_PLE, D_MODEL), True)],
        out_specs=[_tile(tl, d_mix), _tile(tl, D_MODEL), _tile(tl, D_MODEL), _tile(tl, D_MODEL), _tile(tl, D_MODEL),
                   _tile(tl, D_MODEL), _full((SUBLANES, D_MODEL)), _full((SUBLANES, LANES))],
        out_shape=[jax.ShapeDtypeStruct((n_tok, d_mix), F32), jax.ShapeDtypeStruct((n_tok, D_MODEL), F32),
                   jax.ShapeDtypeStruct((n_tok, D_MODEL), BF16), jax.ShapeDtypeStruct((n_tok, D_MODEL), BF16),
                   jax.ShapeDtypeStruct((n_tok, D_MODEL), BF16), jax.ShapeDtypeStruct((n_tok, D_MODEL), BF16),
                   jax.ShapeDtypeStruct((SUBLANES, D_MODEL), F32), jax.ShapeDtypeStruct((SUBLANES, LANES), F32)],
        compiler_params=_params(),
    )(x, yssm, yconf, p, tgt, vec, w_out, wpg, wpp)


def _inproj_bwd(dxin, dz, dglu, dcg, ddtr, dr1, x, g, b, w_r, tl):
    n_tok = x.shape[0]

    def body(dxin_ref, dz_ref, dglu_ref, dcg_ref, ddtr_ref, dr1_ref, x_ref, g_ref, b_ref, w_ref,
             dx_ref, dpb_ref, small_ref):
        @pl.when(pl.program_id(0) == 0)
        def _():
            small_ref[...] = jnp.zeros_like(small_ref)

        dh0 = ALPHA * dr1_ref[...]
        for ref, lo, hi in ((dxin_ref, R_XBC, R_Z), (dz_ref, R_Z, R_GLU), (dglu_ref, R_GLU, R_CG), (dcg_ref, R_CG, R_DT),
                            (ddtr_ref, R_DT, D_INR)):
            piece = ref[...].astype(BF16)
            dpb_ref[:, lo:hi] = piece
            dh0 = dh0 + _dot_nt(piece, w_ref[:, lo:hi])
        xhat, rstd = _ln_stats(x_ref[...])
        small_ref[0:1, :] += _colsum(dh0 * xhat)
        small_ref[1:2, :] += _colsum(dh0)
        dx_ref[...] = _ln_bwd(dh0 * g_ref[...], xhat, rstd)

    return pl.pallas_call(
        body, grid=(n_tok // tl,), name="inproj_bwd",
        in_specs=[_tile(tl, D_XBC), _tile(tl, D_SSM), _tile(tl, 2 * D_CONF), _tile(tl, D_CONF), _tile(tl, LANES),
                  _tile(tl, D_MODEL), _tile(tl, D_MODEL), _full((1, D_MODEL)), _full((1, D_MODEL)),
                  _full((D_MODEL, D_INR), True)],
        out_specs=[_tile(tl, D_MODEL), _tile(tl, D_INR), _full((SUBLANES, D_MODEL))],
        out_shape=[jax.ShapeDtypeStruct((n_tok, D_MODEL), F32), jax.ShapeDtypeStruct((n_tok, D_INR), BF16),
                   jax.ShapeDtypeStruct((SUBLANES, D_MODEL), F32)],
        compiler_params=_params(),
    )(dxin, dz, dglu, dcg, ddtr, dr1, x, g, b, w_r)


def _tn_matmul(a, b, name, tn, tl, out_dtype=F32):
    n_tok, m = a.shape
    n = b.shape[1]
    n_l = n_tok // tl
    direct = out_dtype == F32

    def body(a_ref, b_ref, o_ref, *scratch):
        acc = o_ref if direct else scratch[0]

        @pl.when(pl.program_id(1) == 0)
        def _():
            acc[...] = jnp.zeros_like(acc)

        acc[...] += lax.dot_general(a_ref[...], b_ref[...], TN_DIMS, preferred_element_type=F32)
        if not direct:
            @pl.when(pl.program_id(1) == n_l - 1)
            def _():
                o_ref[...] = acc[...].astype(out_dtype)

    return pl.pallas_call(
        body, grid=(n // tn, n_l), name=name,
        in_specs=[pl.BlockSpec((tl, m), lambda j, l: (l, 0)), pl.BlockSpec((tl, tn), lambda j, l: (l, j))],
        out_specs=pl.BlockSpec((m, tn), lambda j, l: (0, j)),
        out_shape=jax.ShapeDtypeStruct((m, n), out_dtype),
        scratch_shapes=[] if direct else [pltpu.VMEM((m, tn), F32)],
        compiler_params=pltpu.CompilerParams(dimension_semantics=("parallel", "arbitrary"), vmem_limit_bytes=VMEM_LIMIT),
    )(a, b)


def _pad_rows(a, rows):
    return jnp.pad(a, ((0, rows - a.shape[0]), (0, 0)))


def _pad_lanes(a):
    return jnp.pad(a, ((0, 0), (0, LANES - a.shape[1])))


def _local_grads(x, p, tgt, w_r, later, ssm_cw, conf_cw, sm):
    n_tok = x.shape[0]
    tl = min(256, n_tok)
    tlm = min(512, n_tok)
    row = lambda v: v.reshape(1, -1)
    g_e, b_e = row(sm["ln_emb_g"]), row(sm["ln_emb_b"])
    h0b, xbc_in, z, glu, cgate, dtr, w_out_all, wpg_all, wpp_all = _ln_inproj(x, g_e, b_e, w_r, later, tlm)
    w_out_b = w_out_all.reshape(D_SSM + D_CONF, D_MODEL)
    wpg_b = wpg_all.reshape(D_MODEL, D_MODEL)
    wpp_b = wpp_all.transpose(1, 0, 2).reshape(D_PLE, D_MODEL)

    cw4 = _pad_rows(ssm_cw, SUBLANES)
    dtb, alog = _pad_lanes(sm["dt_bias"]), _pad_lanes(sm["a_log"])
    dsk = jnp.repeat(sm["d_skip"], HEAD_DIM, axis=1)
    pre, y, yssm, hprev = _ssd_fwd(xbc_in, z, dtr, cw4, sm["ssm_conv_b"], dtb, alog, dsk, sm["ssm_norm_g"], tl)

    cw31 = _pad_rows(conf_cw, CONF_HALO)
    u0, u1, yconf = _conf_fwd(glu, cgate, sm["b_glu"], cw31, sm["conf_conv_b"], sm["conf_ln_g"], sm["conf_ln_b"], tl)

    vec = jnp.concatenate([g_e, b_e, sm["b_out"], sm["ln1_g"], sm["ln1_b"], sm["ln2_g"], sm["ln2_b"],
                           jnp.zeros((1, D_MODEL), F32)], axis=0)
    dmix, dr1, dr1b, h1b, dgb, dpb, small_t, loss = _tail(
        x, yssm, yconf, p, tgt, vec, w_out_b, wpg_b, wpp_b, tlm)

    dwo = [_tn_matmul(yssm, dr1b, "dw_out_ssm", D_MODEL, tlm, BF16), _tn_matmul(yconf, dr1b, "dw_out_conf", D_MODEL, tlm, BF16)]
    dwp_ = [_tn_matmul(h1b, dgb, "dw_ple_gate", D_MODEL, tlm, BF16),
            _tn_matmul(p.astype(BF16), dpb, "dw_ple_proj", D_MODEL, tlm, BF16)]
    dglu, dcg, dcw31, dbglu, small_c, dw_out = _conf_bwd(dmix, D_SSM // D_CONF, u0, u1, glu, cgate, sm["b_glu"], cw31,
                                                          sm["conf_ln_g"], sm["conf_ln_b"], [W_OUT_DIRECT], dwo, tl)
    dxin, dz, ddtr, dcw4, dcb4, dgn, ddsk, dalog, ddtb, dwpg, dwpp = _ssd_bwd(
        dmix, y, z, pre, xbc_in, dtr, hprev, cw4, dtb, alog, dsk, sm["ssm_norm_g"], [WPG_DIRECT, WPP_DIRECT], dwp_, tl)
    dx, dprojb, small_e = _inproj_bwd(dxin, dz, dglu, dcg, ddtr, dr1, x, g_e, b_e, w_r, tl)

    dw_r = _tn_matmul(h0b, dprojb, "dw_in", D_INR // 3, tlm)
    grads = dict(
        ln_emb_g=small_e[0], ln_emb_b=small_e[1], w_in=dw_r, ssm_conv_w=dcw4[0:SSM_K], ssm_conv_b=dcb4,
        dt_bias=ddtb[:, 0:N_HEADS], a_log=dalog[:, 0:N_HEADS], d_skip=ddsk[:, 0:N_HEADS], ssm_norm_g=dgn, b_glu=dbglu,
        conf_conv_w=dcw31[0:CONF_K], conf_conv_b=small_c[2:3], conf_ln_g=small_c[0:1], conf_ln_b=small_c[1:2],
        w_out=dw_out, b_out=small_t[0:1], ln1_g=small_t[1:2], ln1_b=small_t[2:3], w_ple_gate=dwpg, w_ple_proj=dwpp,
        ln2_g=small_t[3:4], ln2_b=small_t[4:5])
    return loss[0, 0], dx, grads


N_CHIPS = 4
N_DEV = 8
W_IN_SH = D_IN // N_CHIPS
BIG = (("w_in", D_MODEL, W_IN_SH), ("w_out", (D_SSM + D_CONF) // N_CHIPS, D_MODEL),
       ("w_ple_gate", D_MODEL // N_CHIPS, D_MODEL), ("w_ple_proj", D_PLE, D_MODEL // N_CHIPS))
SEGS = ((R_XBC, 0, D_SSM), (R_XBC + D_SSM, 2048, 256), (R_XBC + D_SSM + 256, 2304, 256), (R_Z, 1024, D_SSM),
        (R_GLU, 2576, 2 * D_CONF), (R_CG, 4624, D_CONF), (R_DT, 2560, N_HEADS))
ROWS_CW4 = 2
ROWS_CW31 = 8
ROWS_CONV = 16
SMALL_ROWS = 56
SMALL_LAYOUT = (("ln_emb_g", 0, 1), ("ln_emb_b", 1, 1), ("ssm_conv_b", 2, 2), ("dt_bias", 4, 1), ("a_log", 5, 1),
                ("d_skip", 6, 1), ("ssm_norm_g", 7, 1), ("b_glu", 8, 2), ("conf_conv_b", 10, 1), ("conf_ln_g", 11, 1),
                ("conf_ln_b", 12, 1), ("b_out", 13, 1), ("ln1_g", 14, 1), ("ln1_b", 15, 1), ("ln2_g", 16, 1), ("ln2_b", 17, 1))
CONV_LAYOUT = (("ssm_conv_w", 18, 6, (SSM_K, D_XBC)), ("conf_conv_w", 24, 31, (CONF_K, D_CONF)))


def _rows_of(v, rows):
    flat = v.reshape(-1)
    return jnp.pad(flat, (0, rows * D_MODEL - flat.shape[0])).reshape(rows, D_MODEL)


LOSS_ROW = 55


def _pack_small(d, loss_share=None):
    parts = [_rows_of(d[n], r) for n, _, r in SMALL_LAYOUT]
    for n, _, r, _ in CONV_LAYOUT:
        parts.append(_rows_of(d[n], r) if n in d else jnp.zeros((r, D_MODEL), F32))
    parts.append(_rows_of(jnp.zeros((1,), F32) if loss_share is None else loss_share, SMALL_ROWS - LOSS_ROW))
    return jnp.concatenate(parts, axis=0)


def _w_r_from_shards(w4):
    parts = []
    for _, o, wd in SEGS:
        lo, hi = o, o + wd
        while lo < hi:
            s = lo // W_IN_SH
            e = min(hi, (s + 1) * W_IN_SH)
            parts.append(w4[s][:, lo - s * W_IN_SH:e - s * W_IN_SH])
            lo = e
    parts.append(jnp.zeros((D_MODEL, LANES - N_HEADS), w4.dtype))
    return jnp.concatenate(parts, axis=1)


def _shards_from_dw_r(dw_r):
    slabs = []
    for s in range(N_CHIPS):
        lo, hi = s * W_IN_SH, (s + 1) * W_IN_SH
        parts = []
        for kcol, o, wd in sorted(SEGS, key=lambda t: t[1]):
            a, b = max(lo, o), min(hi, o + wd)
            if a < b:
                parts.append(dw_r[:, kcol + a - o:kcol + b - o])
        slabs.append(jnp.concatenate(parts, axis=1))
    return jnp.stack(slabs)


def _row_chunks(rows, n):
    return [(j * (rows // n), rows // n) for j in range(n)]


def _my_place():
    return lax.axis_index("x"), lax.axis_index("y"), lax.axis_index("c")


MESH_ID = pl.DeviceIdType.MESH
ANY = pl.BlockSpec(memory_space=pl.ANY)
IN_VMEM = pl.BlockSpec(memory_space=pltpu.VMEM)
CHIP_FLIPS = ((1, 0), (0, 1), (1, 1))


def _remote(src, dst, send_sem, recv_sem, peer):
    return pltpu.make_async_remote_copy(src, dst, send_sem, recv_sem, device_id=peer, device_id_type=MESH_ID)


GATHER_CHUNKS = (4, 2, 1, 1)


def _gather_plans(kinds):
    plan = [(a, o, n, rows // 2) for a, (rows, ch) in enumerate(kinds) for o, n in _row_chunks(rows // 2, ch)]
    own_plan = [(a, o, n) for a, (rows, ch) in enumerate(kinds) for o, n in _row_chunks(rows, 2 * ch)]
    return plan, own_plan


def _gather_sems(plan, own_plan):
    hop = pltpu.SemaphoreType.DMA((3, len(plan)))
    return [hop, hop, hop, hop, pltpu.SemaphoreType.DMA((len(own_plan),))]


def _gather_stage(stage, ins, outs, plan, own_plan, sems):
    send_a, recv_a, send_b, recv_b, loc = sems
    x, y, c = _my_place()
    s = 2 * x + y
    sibling = (x, y, 1 - c)
    own = [pltpu.make_async_copy(ins[a].at[pl.ds(o, n)], outs[a].at[s, pl.ds(o, n)], loc.at[j])
           for j, (a, o, n) in enumerate(own_plan)]
    first, arrive, passed, arrive_b = [], [], [], []
    for k, (fx, fy) in enumerate(CHIP_FLIPS):
        peer = (x ^ fx, y ^ fy, c)
        sk = 2 * (x ^ fx) + (y ^ fy)
        for j, (a, o, n, h) in enumerate(plan):
            mine = pl.ds(pl.multiple_of(c * h + o, 16), n)
            theirs = pl.ds(pl.multiple_of((1 - c) * h + o, 16), n)
            first.append(_remote(ins[a].at[mine], outs[a].at[s, mine], send_a.at[k, j], recv_a.at[k, j], peer))
            land = outs[a].at[sk, mine]
            arrive.append(_remote(land, land, send_a.at[k, j], recv_a.at[k, j], peer))
            passed.append(_remote(land, land, send_b.at[k, j], recv_b.at[k, j], sibling))
            land_b = outs[a].at[sk, theirs]
            arrive_b.append(_remote(land_b, land_b, send_b.at[k, j], recv_b.at[k, j], sibling))
    if stage == 0:
        for cp in own + first:
            cp.start()
    elif stage == 1:
        for got, fwd in zip(arrive, passed):
            got.wait_recv()
            fwd.start()
    else:
        for got in arrive_b:
            got.wait_recv()
        for cp in first + passed:
            cp.wait_send()
        for cp in own:
            cp.wait()


W_OUT_DIRECT = (512, D_MODEL, 2, ((0, 0, 0), (0, 512, 0), (1, 0, 0), (1, 512, 0)))
WPG_DIRECT = (256, D_MODEL, 1, ((0, 0, 0), (0, 256, 0), (0, 512, 0), (0, 768, 0)))
WPP_DIRECT = (D_PLE, 256, 1, ((1, 0, 0), (1, 0, 256), (1, 0, 512), (1, 0, 768)))


def _direct_sems(specs):
    out = []
    for _, _, copies, _ in specs:
        out += [pltpu.SemaphoreType.DMA((N_CHIPS, 2, copies)), pltpu.SemaphoreType.DMA((N_DEV, copies)),
                pltpu.SemaphoreType.DMA((copies,))]
    return out


def _direct_stage(stage, specs, srcs, recvs, sems):
    x, y, c = _my_place()
    me = 4 * x + 2 * y + c
    s = 2 * x + y
    for r, (rows, cols, copies, where) in enumerate(specs):
        send_sems, recv_sems, loc_sems = sems[3 * r:3 * r + 3]
        for k, (o, n) in enumerate(_row_chunks(rows, copies)):
            for t, (si, row0, col0) in enumerate(where):
                src = srcs[si].at[pl.ds(row0 + o, n), pl.ds(col0, cols)]
                dst = recvs[r].at[me, pl.ds(o, n)]
                for cc in range(2):
                    to_self = jnp.logical_and(s == t, c == cc)
                    away = _remote(src, dst, send_sems.at[t, cc, k], recv_sems.at[me, k], (t // 2, t % 2, cc))
                    here = pltpu.make_async_copy(src, dst, loc_sems.at[k])

                    @pl.when(to_self)
                    def _():
                        here.start() if stage == 0 else here.wait()

                    @pl.when(jnp.logical_not(to_self))
                    def _():
                        away.start() if stage == 0 else away.wait_send()
            if stage == 1:
                for j in range(N_DEV):
                    land = recvs[r].at[j, pl.ds(o, n)]

                    @pl.when(j != me)
                    def _():
                        _remote(land, land, send_sems.at[0, 0, k], recv_sems.at[j, k], (0, 0, 0)).wait_recv()


def _gather_weights(w_in_b, conv_f):
    plan, own_plan = _gather_plans([(BIG[0][1], GATHER_CHUNKS[0])])

    def body(w_ref, conv_ref, w_all, conv_all, *sems):
        conv_send, conv_recv, conv_loc = sems[5:]
        x, y, c = _my_place()
        s = 2 * x + y
        conv_own = pltpu.make_async_copy(conv_ref, conv_all.at[s], conv_loc)
        conv_out = [_remote(conv_ref, conv_all.at[s], conv_send.at[k], conv_recv.at[k], (x ^ fx, y ^ fy, c))
                    for k, (fx, fy) in enumerate(CHIP_FLIPS)]
        _gather_stage(0, [w_ref], [w_all], plan, own_plan, sems[0:5])
        for cp in [conv_own] + conv_out:
            cp.start()
        _gather_stage(1, [w_ref], [w_all], plan, own_plan, sems[0:5])
        _gather_stage(2, [w_ref], [w_all], plan, own_plan, sems[0:5])
        for cp in conv_out:
            cp.wait()
        conv_own.wait()

    arrays = [w_in_b, conv_f]
    return pl.pallas_call(
        body, name="gather_weights", in_specs=[IN_VMEM] * 2, out_specs=[ANY] * 2,
        out_shape=[jax.ShapeDtypeStruct((N_CHIPS,) + a.shape, a.dtype) for a in arrays],
        scratch_shapes=_gather_sems(plan, own_plan)
        + [pltpu.SemaphoreType.DMA((3,)), pltpu.SemaphoreType.DMA((3,)), pltpu.SemaphoreType.DMA],
    )(*arrays)


CORES_CHUNKS = (4, 2, 1, 1)
CHIPS_CHUNKS = (2, 1, 1, 1)
SHARE_CHUNKS = (8, 4, 2, 1)


def _exchange_cores(gbig, spack):
    n_big = len(gbig)
    plan = [(a, t, o, n, g.shape[1] // 2) for a, g in enumerate(gbig) for t in range(g.shape[0])
            for o, n in _row_chunks(g.shape[1] // 2, CORES_CHUNKS[a] * N_CHIPS // g.shape[0])]

    def body(*refs):
        g_refs, s_ref = refs[0:n_big], refs[n_big]
        theirs, small_all = refs[n_big + 1:2 * n_big + 1], refs[2 * n_big + 1]
        send_sems, recv_sems, send_small, recv_small, loc_sem = refs[2 * n_big + 2:]
        x, y, c = _my_place()
        me = 4 * x + 2 * y + c
        own = pltpu.make_async_copy(s_ref, small_all.at[me], loc_sem)
        sends = []
        for j, (a, t, o, n, h) in enumerate(plan):
            give = pl.multiple_of((1 - c) * h + o, SUBLANES)
            sends.append(_remote(g_refs[a].at[t, pl.ds(give, n)], theirs[a].at[t, pl.ds(o, n)], send_sems.at[j], recv_sems.at[j],
                                 (x, y, 1 - c)))
        for m in range(1, N_DEV):
            peer = (x ^ (m >> 2), y ^ ((m >> 1) & 1), c ^ (m & 1))
            sends.append(_remote(s_ref, small_all.at[me], send_small.at[m - 1], recv_small.at[m - 1], peer))
        own.start()
        for cp in sends:
            cp.start()
        for cp in sends:
            cp.wait()
        own.wait()

    return pl.pallas_call(
        body, name="exchange_cores", in_specs=[ANY] * n_big + [IN_VMEM], out_specs=[ANY] * (n_big + 1),
        out_shape=[jax.ShapeDtypeStruct((g.shape[0], g.shape[1] // 2, g.shape[2]), F32) for g in gbig]
        + [jax.ShapeDtypeStruct((N_DEV, SMALL_ROWS, D_MODEL), F32)],
        scratch_shapes=[pltpu.SemaphoreType.DMA((len(plan),)), pltpu.SemaphoreType.DMA((len(plan),)),
                        pltpu.SemaphoreType.DMA((N_DEV - 1,)), pltpu.SemaphoreType.DMA((N_DEV - 1,)), pltpu.SemaphoreType.DMA],
    )(*gbig, spack)


def _exchange_chips(psums):
    n_big = len(psums)
    plan = [(a, o, n) for a, ps in enumerate(psums) for o, n in _row_chunks(ps.shape[1], CHIPS_CHUNKS[a])]

    def body(*refs):
        p_refs, got = refs[0:n_big], refs[n_big:2 * n_big]
        send_sems, recv_sems = refs[2 * n_big:]
        x, y, c = _my_place()
        sends = []
        for k, (fx, fy) in enumerate(CHIP_FLIPS):
            tx, ty = x ^ fx, y ^ fy
            for j, (a, o, n) in enumerate(plan):
                sends.append(_remote(p_refs[a].at[2 * tx + ty, pl.ds(o, n)], got[a].at[k, pl.ds(o, n)], send_sems.at[k, j],
                                     recv_sems.at[k, j], (tx, ty, c)))
        for cp in sends:
            cp.start()
        for cp in sends:
            cp.wait()

    return pl.pallas_call(
        body, name="exchange_chips", in_specs=[ANY] * n_big, out_specs=[ANY] * n_big,
        out_shape=[jax.ShapeDtypeStruct((3,) + ps.shape[1:], BF16) for ps in psums],
        scratch_shapes=[pltpu.SemaphoreType.DMA((3, len(plan))), pltpu.SemaphoreType.DMA((3, len(plan)))],
    )(*psums)


def _share_halves(tots):
    n_big = len(tots)
    plan = [(a, o, n) for a, tt in enumerate(tots) for o, n in _row_chunks(tt.shape[0], SHARE_CHUNKS[a])]

    def body(*refs):
        t_refs, both = refs[0:n_big], refs[n_big:2 * n_big]
        send_sems, recv_sems, loc = refs[2 * n_big:]
        x, y, c = _my_place()
        own = [pltpu.make_async_copy(t_refs[a].at[pl.ds(o, n)], both[a].at[c, pl.ds(o, n)], loc.at[j])
               for j, (a, o, n) in enumerate(plan)]
        sends = [_remote(t_refs[a].at[pl.ds(o, n)], both[a].at[c, pl.ds(o, n)], send_sems.at[j], recv_sems.at[j], (x, y, 1 - c))
                 for j, (a, o, n) in enumerate(plan)]
        for cp in own + sends:
            cp.start()
        for cp in sends:
            cp.wait()
        for cp in own:
            cp.wait()

    return pl.pallas_call(
        body, name="share_halves", in_specs=[IN_VMEM] * n_big, out_specs=[ANY] * n_big,
        out_shape=[jax.ShapeDtypeStruct((2,) + tt.shape, F32) for tt in tots],
        scratch_shapes=[pltpu.SemaphoreType.DMA((len(plan),)), pltpu.SemaphoreType.DMA((len(plan),)),
                        pltpu.SemaphoreType.DMA((len(plan),))],
    )(*tots)


def _sum_rows(half):
    return half if half <= 256 else 256


def _add_cores(core, g, theirs, name):
    n_slabs, rows, cols = g.shape
    half = rows // 2
    tr = _sum_rows(half)
    nb = half // tr

    def body(c_ref, g_ref, t_ref, o_ref):
        o_ref[...] = (g_ref[...] + t_ref[...]).astype(BF16)

    grid_spec = pltpu.PrefetchScalarGridSpec(
        num_scalar_prefetch=1, grid=(n_slabs, nb),
        in_specs=[pl.BlockSpec((1, tr, cols), lambda t, i, c_ref: (t, c_ref[0] * nb + i, 0)),
                  pl.BlockSpec((1, tr, cols), lambda t, i, c_ref: (t, i, 0))],
        out_specs=pl.BlockSpec((1, tr, cols), lambda t, i, c_ref: (t, i, 0)))
    return pl.pallas_call(body, grid_spec=grid_spec, name=name, out_shape=jax.ShapeDtypeStruct((n_slabs, half, cols), BF16),
                          compiler_params=pltpu.CompilerParams(vmem_limit_bytes=VMEM_LIMIT))(core, g, theirs)


def _add_chips(chip, psum, got, name):
    _, half, cols = psum.shape
    tr = _sum_rows(half)

    def body(s_ref, p_ref, g_ref, o_ref):
        o_ref[...] = ((p_ref[0].astype(F32) + g_ref[0].astype(F32)) + g_ref[1].astype(F32)) + g_ref[2].astype(F32)

    grid_spec = pltpu.PrefetchScalarGridSpec(
        num_scalar_prefetch=1, grid=(half // tr,),
        in_specs=[pl.BlockSpec((1, tr, cols), lambda i, s_ref: (s_ref[0], i, 0)),
                  pl.BlockSpec((3, tr, cols), lambda i, s_ref: (0, i, 0))],
        out_specs=pl.BlockSpec((tr, cols), lambda i, s_ref: (i, 0)))
    return pl.pallas_call(body, grid_spec=grid_spec, name=name, out_shape=jax.ShapeDtypeStruct((half, cols), F32),
                          compiler_params=pltpu.CompilerParams(vmem_limit_bytes=VMEM_LIMIT))(chip, psum, got)


def _adam_math(w, g, m, v):
    m = ADAM_B1 * m + (1.0 - ADAM_B1) * g
    v = ADAM_B2 * v + (1.0 - ADAM_B2) * (g * g)
    m_hat = m / (1.0 - ADAM_B1 ** ADAM_STEP)
    v_hat = v / (1.0 - ADAM_B2 ** ADAM_STEP)
    return -ADAM_LR * (m_hat / (jnp.sqrt(v_hat) + ADAM_EPS) + ADAM_WD * w), m, v


def _adam(w, g, m, v, name):
    rows, cols = w.shape
    tr = rows if rows <= 256 else 256

    def body(w_ref, g_ref, m_ref, v_ref, d_ref, nm_ref, nv_ref):
        d_ref[...], nm_ref[...], nv_ref[...] = _adam_math(w_ref[...], g_ref[...], m_ref[...], v_ref[...])

    spec = pl.BlockSpec((tr, cols), lambda i: (i, 0))
    return pl.pallas_call(body, grid=(rows // tr,), name=name, in_specs=[spec] * 4, out_specs=[spec] * 3,
                          out_shape=[jax.ShapeDtypeStruct(w.shape, F32)] * 3, compiler_params=_params(seq=False))(w, g, m, v)


def _adam_sum(w, parts, m, v, name):
    rows, cols = w.shape
    tr = rows if rows <= 256 else 256

    def body(w_ref, p_ref, m_ref, v_ref, g_ref, d_ref, nm_ref, nv_ref):
        g = p_ref[0].astype(F32)
        for j in range(1, N_DEV):
            g = g + p_ref[j].astype(F32)
        g_ref[...] = g
        d_ref[...], nm_ref[...], nv_ref[...] = _adam_math(w_ref[...], g, m_ref[...], v_ref[...])

    spec = pl.BlockSpec((tr, cols), lambda i: (i, 0))
    return pl.pallas_call(
        body, grid=(rows // tr,), name=name,
        in_specs=[spec, pl.BlockSpec((N_DEV, tr, cols), lambda i: (0, i, 0)), spec, spec], out_specs=[spec] * 4,
        out_shape=[jax.ShapeDtypeStruct(w.shape, F32)] * 4, compiler_params=_params(seq=False))(w, parts, m, v)


def _adam_small(parts, ws, ms, vs):
    n_w = len(SMALL_LAYOUT)

    def body(p_ref, *refs):
        w_refs, m_refs, v_refs = refs[0:n_w], refs[n_w:2 * n_w], refs[2 * n_w:3 * n_w]
        sum_ref = refs[3 * n_w]
        outs = refs[3 * n_w + 1:]
        g = p_ref[0]
        for k in range(1, N_DEV):
            g = g + p_ref[k]
        sum_ref[...] = g
        for a, (_, r0, rows) in enumerate(SMALL_LAYOUT):
            width = w_refs[a].shape[1]
            for j in range(rows):
                lo, hi = j * D_MODEL, min((j + 1) * D_MODEL, width)
                gj = sum_ref[r0 + j:r0 + j + 1, 0:hi - lo]
                d, nm, nv = _adam_math(w_refs[a][:, lo:hi], gj, m_refs[a][:, lo:hi], v_refs[a][:, lo:hi])
                for out, val in zip(outs[4 * a:4 * a + 4], (gj, d, nm, nv)):
                    out[:, lo:hi] = val

    shapes = [jax.ShapeDtypeStruct(parts.shape[1:], F32)]
    for wa in ws:
        shapes += [jax.ShapeDtypeStruct(wa.shape, F32)] * 4
    res = pl.pallas_call(body, name="adam_small", out_shape=shapes)(parts, *ws, *ms, *vs)
    return res[0], [res[1 + 4 * a:5 + 4 * a] for a in range(n_w)]


def kernel(x, p, ln_emb_g, ln_emb_b, w_in, ssm_conv_w, ssm_conv_b, dt_bias, a_log, d_skip, ssm_norm_g, b_glu, conf_conv_w, conf_conv_b, conf_ln_g, conf_ln_b, w_out, b_out, ln1_g, ln1_b, w_ple_gate, w_ple_proj, ln2_g, ln2_b, loss_target, m_ln_emb_g, m_ln_emb_b, m_w_in, m_ssm_conv_w, m_ssm_conv_b, m_dt_bias, m_a_log, m_d_skip, m_ssm_norm_g, m_b_glu, m_conf_conv_w, m_conf_conv_b, m_conf_ln_g, m_conf_ln_b, m_w_out, m_b_out, m_ln1_g, m_ln1_b, m_w_ple_gate, m_w_ple_proj, m_ln2_g, m_ln2_b, v_ln_emb_g, v_ln_emb_b, v_w_in, v_ssm_conv_w, v_ssm_conv_b, v_dt_bias, v_a_log, v_d_skip, v_ssm_norm_g, v_b_glu, v_conf_conv_w, v_conf_conv_b, v_conf_ln_g, v_conf_ln_b, v_w_out, v_b_out, v_ln1_g, v_ln1_b, v_w_ple_gate, v_w_ple_proj, v_ln2_g, v_ln2_b):
    order = ("ln_emb_g", "ln_emb_b", "w_in", "ssm_conv_w", "ssm_conv_b", "dt_bias", "a_log", "d_skip", "ssm_norm_g", "b_glu",
             "conf_conv_w", "conf_conv_b", "conf_ln_g", "conf_ln_b", "w_out", "b_out", "ln1_g", "ln1_b", "w_ple_gate",
             "w_ple_proj", "ln2_g", "ln2_b")
    w = dict(zip(order, (ln_emb_g, ln_emb_b, w_in, ssm_conv_w, ssm_conv_b, dt_bias, a_log, d_skip, ssm_norm_g, b_glu,
                         conf_conv_w, conf_conv_b, conf_ln_g, conf_ln_b, w_out, b_out, ln1_g, ln1_b, w_ple_gate, w_ple_proj,
                         ln2_g, ln2_b)))
    m = dict(zip(order, (m_ln_emb_g, m_ln_emb_b, m_w_in, m_ssm_conv_w, m_ssm_conv_b, m_dt_bias, m_a_log, m_d_skip,
                         m_ssm_norm_g, m_b_glu, m_conf_conv_w, m_conf_conv_b, m_conf_ln_g, m_conf_ln_b, m_w_out, m_b_out,
                         m_ln1_g, m_ln1_b, m_w_ple_gate, m_w_ple_proj, m_ln2_g, m_ln2_b)))
    v = dict(zip(order, (v_ln_emb_g, v_ln_emb_b, v_w_in, v_ssm_conv_w, v_ssm_conv_b, v_dt_bias, v_a_log, v_d_skip,
                         v_ssm_norm_g, v_b_glu, v_conf_conv_w, v_conf_conv_b, v_conf_ln_g, v_conf_ln_b, v_w_out, v_b_out,
                         v_ln1_g, v_ln1_b, v_w_ple_gate, v_w_ple_proj, v_ln2_g, v_ln2_b)))

    conv_f = jnp.concatenate([_rows_of(w["ssm_conv_w"], ROWS_CW4), _rows_of(w["conf_conv_w"], ROWS_CW31),
                              jnp.zeros((ROWS_CONV - ROWS_CW4 - ROWS_CW31, D_MODEL), F32)], axis=0)
    shards_b = [w[n][0].astype(BF16) for n, _, _ in BIG]
    w_in_all, conv_all = _gather_weights(shards_b[0], conv_f)
    w_r = _w_r_from_shards(w_in_all)
    cw4 = conv_all[:, 0:ROWS_CW4].reshape(N_CHIPS, -1)[:, :SSM_K * D_XBC // N_CHIPS]
    cw4 = cw4.reshape(N_CHIPS, SSM_K, D_XBC // N_CHIPS).transpose(1, 0, 2).reshape(SSM_K, D_XBC)
    cw31 = conv_all[:, ROWS_CW4:ROWS_CW4 + ROWS_CW31].reshape(N_CHIPS, -1)[:, :CONF_K * D_CONF // N_CHIPS]
    cw31 = cw31.reshape(N_CHIPS, CONF_K, D_CONF // N_CHIPS).transpose(1, 0, 2).reshape(CONF_K, D_CONF)

    small_names = [n for n, _, _ in SMALL_LAYOUT]
    sm = {n: w[n] for n in small_names}
    loss_part, dx, grads = _local_grads(x[0], p[0, 0], loss_target[0], w_r, shards_b[1:], cw4, cw31, sm)

    core = lax.axis_index("c").astype(jnp.int32).reshape(1)
    chip_i = 2 * lax.axis_index("x") + lax.axis_index("y")
    chip = chip_i.astype(jnp.int32).reshape(1)
    gbig = [grads["w_in"][None]]
    spack = _pack_small({n: grads[n] for n in small_names + [n for n, _, _, _ in CONV_LAYOUT]}, loss_part)
    *theirs, small_all = _exchange_cores(gbig, spack)
    psums = [_add_cores(core, g, t, "add_cores_" + n) for g, t, (n, _, _) in zip(gbig, theirs, BIG)]
    psums[0] = _shards_from_dw_r(psums[0][0])
    got = _exchange_chips(psums)
    tots = [_add_chips(chip, ps, gt, "add_chips_" + n) for ps, gt, (n, _, _) in zip(psums, got, BIG)]
    both = _share_halves(tots)

    out_g, out_d, out_m, out_v = {}, {}, {}, {}
    as_row = lambda a: a.reshape(1, -1)
    g_s, small_out = _adam_small(small_all, [as_row(w[n]) for n in small_names], [as_row(m[n]) for n in small_names],
                                 [as_row(v[n]) for n in small_names])
    loss = g_s[LOSS_ROW, 0]
    g_shard = {n: b.reshape(rows, cols) for b, (n, rows, cols) in zip(both, BIG)}
    for n, r0, r, shape in CONV_LAYOUT:
        whole = g_s[r0:r0 + r].reshape(-1)[:shape[0] * shape[1]].reshape(shape)
        g_shard[n] = lax.dynamic_slice_in_dim(whole, chip_i * (shape[1] // N_CHIPS), shape[1] // N_CHIPS, axis=1)
    for n, g in g_shard.items():
        d, nm, nv = _adam(w[n][0], g, m[n][0], v[n][0], "adam_" + n)
        out_g[n], out_d[n], out_m[n], out_v[n] = g[None], d[None], nm[None], nv[None]
    for n, _, _ in BIG[1:]:
        out_g[n], out_d[n], out_m[n], out_v[n] = [a[None] for a in _adam_sum(w[n][0], grads[n], m[n][0], v[n][0], "adam_" + n)]
    for n, four in zip(small_names, small_out):
        out_g[n], out_d[n], out_m[n], out_v[n] = [a.reshape(w[n].shape) for a in four]
    return (loss, dx[None], *[out_g[n] for n in order], *[out_d[n] for n in order], *[out_m[n] for n in order],
            *[out_v[n] for n in order])
```

```python
import functools

import jax
import jax.numpy as jnp
from jax import lax
from jax.experimental import pallas as pl
from jax.experimental.pallas import tpu as pltpu

F32 = jnp.float32
BF16 = jnp.bfloat16

D_MODEL = 1024
D_PLE = 256
D_SSM = 1024
D_CONF = 1024
N_HEADS = 16
HEAD_DIM = 64
N_GROUPS = 2
N_STATE = 128
CHUNK = 128
SSM_K = 4
CONF_K = 31
D_XBC = D_SSM + 2 * N_GROUPS * N_STATE
D_IN = 5648
R_XBC, R_Z, R_GLU, R_CG, R_DT, D_INR = 0, 1536, 2560, 4608, 5632, 5760
LN_EPS = 1e-5
RMS_EPS = 1e-5
ALPHA = 2.0 ** 0.25
ADAM_LR, ADAM_B1, ADAM_B2, ADAM_EPS, ADAM_WD, ADAM_STEP = 0.001, 0.9, 0.999, 1e-08, 0.01, 10
NEG_BIG = -1e30
LANES = 128
SUBLANES = 8
VMEM_LIMIT = 56 * 1024 * 1024
HIGHEST = lax.Precision.HIGHEST
NT_DIMS = (((1,), (1,)), ((), ()))
TN_DIMS = (((0,), (0,)), ((), ()))


def _sig(v):
    return jax.nn.sigmoid(v)


def _dsilu(v, s):
    return s * (1.0 + v * (1.0 - s))


def _ln_stats(v):
    mu = jnp.mean(v, axis=-1, keepdims=True)
    c = v - mu
    var = jnp.mean(c * c, axis=-1, keepdims=True)
    rstd = lax.rsqrt(var + LN_EPS)
    return c * rstd, rstd


def _ln_bwd(dxhat, xhat, rstd):
    m1 = jnp.mean(dxhat, axis=-1, keepdims=True)
    m2 = jnp.mean(dxhat * xhat, axis=-1, keepdims=True)
    return rstd * (dxhat - m1 - xhat * m2)


def _softplus(v):
    return jnp.maximum(v, 0.0) + jnp.log1p(jnp.exp(-jnp.abs(v)))


def _colsum(v):
    return jnp.sum(v, axis=0, keepdims=True)


def _dot(a, b):
    return jnp.dot(a, b, preferred_element_type=F32)


def _dot_nt(a, b):
    return lax.dot_general(a, b, NT_DIMS, preferred_element_type=F32)


def _tile(tl, c, rev_of=None):
    if rev_of is None:
        return pl.BlockSpec((tl, c), lambda i: (i, 0))
    return pl.BlockSpec((tl, c), lambda i: (rev_of - 1 - i, 0))


def _full(shape, single=False):
    nd = len(shape)
    if single:
        return pl.BlockSpec(shape, lambda i: (0,) * nd, pipeline_mode=pl.Buffered(1))
    return pl.BlockSpec(shape, lambda i: (0,) * nd)


def _params(seq=True):
    return pltpu.CompilerParams(dimension_semantics=("arbitrary",) if seq else ("parallel",), vmem_limit_bytes=VMEM_LIMIT)


def _ln_inproj(x, g, b, w_r, later, tl):
    n_tok = x.shape[0]
    n_t = n_tok // tl
    n_l = len(later)
    plan, own_plan = _gather_plans([(rows, GATHER_CHUNKS[a]) for a, (_, rows, _) in enumerate(BIG)][1:])

    def body(x_ref, g_ref, b_ref, w_ref, *rest):
        part_refs = rest[0:n_l]
        h0b_ref, xbc_ref, z_ref, glu_ref, cg_ref, dtr_ref = rest[n_l:n_l + 6]
        all_refs, sems = rest[n_l + 6:2 * n_l + 6], rest[2 * n_l + 6:]
        i = pl.program_id(0)

        @pl.when(i == 0)
        def _():
            _gather_stage(0, part_refs, all_refs, plan, own_plan, sems)

        xhat, _ = _ln_stats(x_ref[...])
        hb = (xhat * g_ref[...] + b_ref[...]).astype(BF16)
        h0b_ref[...] = hb
        xbc_ref[...] = _dot(hb, w_ref[:, R_XBC:R_Z])
        z_ref[...] = _dot(hb, w_ref[:, R_Z:R_GLU])
        glu_ref[...] = _dot(hb, w_ref[:, R_GLU:R_CG])
        cg_ref[...] = _dot(hb, w_ref[:, R_CG:R_DT])
        dtr_ref[...] = _dot(hb, w_ref[:, R_DT:D_INR])

        @pl.when(i == (3 * n_t) // 4)
        def _():
            _gather_stage(1, part_refs, all_refs, plan, own_plan, sems)

        @pl.when(i == n_t - 1)
        def _():
            _gather_stage(2, part_refs, all_refs, plan, own_plan, sems)

    widths = (D_MODEL, D_XBC, D_SSM, 2 * D_CONF, D_CONF, LANES)
    dtypes = (BF16, F32, F32, F32, F32, F32)
    return pl.pallas_call(
        body, grid=(n_t,), name="ln_inproj",
        in_specs=[_tile(tl, D_MODEL), _full((1, D_MODEL)), _full((1, D_MODEL)), _full((D_MODEL, D_INR), single=True)]
        + [IN_VMEM] * n_l,
        out_specs=[_tile(tl, w) for w in widths] + [ANY] * n_l,
        out_shape=[jax.ShapeDtypeStruct((n_tok, w), dt) for w, dt in zip(widths, dtypes)]
        + [jax.ShapeDtypeStruct((N_CHIPS,) + a.shape, a.dtype) for a in later],
        scratch_shapes=_gather_sems(plan, own_plan),
        compiler_params=_params(),
    )(x, g, b, w_r, *later)


def _chunk_common(adt_c):
    row = lax.broadcasted_iota(jnp.int32, (CHUNK, CHUNK), 0)
    col = lax.broadcasted_iota(jnp.int32, (CHUNK, CHUNK), 1)
    tril = row >= col
    acs = jnp.dot(tril.astype(F32), adt_c, precision=HIGHEST, preferred_element_type=F32)
    last = acs[CHUNK - 1:CHUNK, :]
    return dict(row=row, col=col, tril=tril, lo=col < HEAD_DIM, acs=acs, acs_t=acs.T, e=jnp.exp(acs),
                dec=jnp.exp(last - acs), cd=jnp.exp(last))


def _decay_mask(cm, h):
    return jnp.exp(jnp.where(cm["tril"], cm["acs"][:, h:h + 1] - cm["acs_t"][h:h + 1, :], NEG_BIG))


def _head_lane_matrix():
    return (jnp.arange(D_SSM)[None, :] // HEAD_DIM == jnp.arange(LANES)[:, None]).astype(BF16)


def _per_head_lanes(v, exp_ref):
    hi = v.astype(BF16)
    lo = (v - hi.astype(F32)).astype(BF16)
    return _dot(hi, exp_ref[...]) + _dot(lo, exp_ref[...])


SSM_BWD_SHIFTS = (1, 2, 3)


def _ssd_fwd(xbc_in, z, dtr, cw, cb, dtb, alog, dsk, gnorm, tl):
    n_tok = xbc_in.shape[0]
    nq = tl // CHUNK

    def body(xin_ref, z_ref, dtr_ref, cw_ref, cb_ref, dtb_ref, alog_ref, dsk_ref, gn_ref, exp_ref,
             pre_ref, y_ref, yssm_ref, hprev_ref, buf, hst):
        @pl.when(pl.program_id(0) == 0)
        def _():
            buf[0:SUBLANES, :] = jnp.zeros((SUBLANES, D_XBC), F32)
            hst[...] = jnp.zeros_like(hst)

        buf[SUBLANES:SUBLANES + tl, :] = xin_ref[...]
        pre = cb_ref[...] + jnp.zeros((tl, D_XBC), F32)
        for k in range(SSM_K):
            off = SUBLANES - (SSM_K - 1) + k
            pre = pre + buf[off:off + tl, :] * cw_ref[k:k + 1, :]
        buf[0:SUBLANES, :] = buf[tl:tl + SUBLANES, :]
        pre_ref[...] = pre
        xbc = pre * _sig(pre)
        dt = _softplus(dtr_ref[...] + dtb_ref[...])
        a = -jnp.exp(alog_ref[...])
        adt = dt * a
        for q in range(nq):
            r0 = q * CHUNK
            cm = _chunk_common(adt[r0:r0 + CHUNK, :])
            dt_x = _per_head_lanes(dt[r0:r0 + CHUNK, :], exp_ref)
            e_x = _per_head_lanes(cm["e"], exp_ref)
            dec_x = _per_head_lanes(cm["dec"], exp_ref)
            for g in range(N_GROUPS):
                bg = xbc[r0:r0 + CHUNK, D_SSM + g * N_STATE:D_SSM + (g + 1) * N_STATE].astype(BF16)
                cg_ = xbc[r0:r0 + CHUNK, D_SSM + (N_GROUPS + g) * N_STATE:D_SSM + (N_GROUPS + g + 1) * N_STATE].astype(BF16)
                gm = _dot_nt(cg_, bg)
                for k in range(N_HEADS // N_GROUPS // 2):
                    ha = (N_HEADS // N_GROUPS) * g + 2 * k
                    c0 = ha * HEAD_DIM
                    xh2 = xbc[r0:r0 + CHUNK, c0:c0 + LANES]
                    x2 = xh2 * dt_x[:, c0:c0 + LANES]
                    x2b = x2.astype(BF16)
                    ya = _dot((gm * _decay_mask(cm, ha)).astype(BF16), x2b)
                    yb = _dot((gm * _decay_mask(cm, ha + 1)).astype(BF16), x2b)
                    h2 = hst[c0:c0 + LANES, :]
                    hprev_ref[q, c0:c0 + LANES, :] = h2
                    z2 = _dot_nt(cg_, h2.astype(BF16))
                    y2 = jnp.where(cm["lo"], ya, yb) + z2 * e_x[:, c0:c0 + LANES] + dsk_ref[:, c0:c0 + LANES] * xh2
                    y_ref[r0:r0 + CHUNK, c0:c0 + LANES] = y2
                    s2 = _dot((x2 * dec_x[:, c0:c0 + LANES]).T.astype(BF16), bg)
                    cd2 = jnp.where(cm["row"] < HEAD_DIM, cm["cd"][:, ha:ha + 1], cm["cd"][:, ha + 1:ha + 2])
                    hst[c0:c0 + LANES, :] = cd2 * h2 + s2
        yv = y_ref[...]
        zv = z_ref[...]
        yz = yv * (zv * _sig(zv))
        gw = D_SSM // N_GROUPS
        for g in range(N_GROUPS):
            seg = yz[:, g * gw:(g + 1) * gw]
            r = lax.rsqrt(jnp.mean(seg * seg, axis=-1, keepdims=True) + RMS_EPS)
            yssm_ref[:, g * gw:(g + 1) * gw] = (seg * r * gn_ref[:, g * gw:(g + 1) * gw]).astype(BF16)

    return pl.pallas_call(
        body, grid=(n_tok // tl,), name="ssd_fwd",
        in_specs=[_tile(tl, D_XBC), _tile(tl, D_SSM), _tile(tl, LANES), _full((SUBLANES, D_XBC)), _full((1, D_XBC)),
                  _full((1, LANES)), _full((1, LANES)), _full((1, D_SSM)), _full((1, D_SSM)), _full((LANES, D_SSM))],
        out_specs=[_tile(tl, D_XBC), _tile(tl, D_SSM), _tile(tl, D_SSM),
                   pl.BlockSpec((nq, D_SSM, N_STATE), lambda i: (i, 0, 0))],
        out_shape=[jax.ShapeDtypeStruct((n_tok, D_XBC), F32), jax.ShapeDtypeStruct((n_tok, D_SSM), F32),
                   jax.ShapeDtypeStruct((n_tok, D_SSM), BF16), jax.ShapeDtypeStruct((n_tok // CHUNK, D_SSM, N_STATE), F32)],
        scratch_shapes=[pltpu.VMEM((tl + SUBLANES, D_XBC), F32), pltpu.VMEM((D_SSM, N_STATE), F32)],
        compiler_params=_params(),
    )(xbc_in, z, dtr, cw, cb, dtb, alog, dsk, gnorm, _head_lane_matrix())


def _ssd_bwd(dys, y, z, pre, xbc_in, dtr, hprev, cw, dtb, alog, dsk, gnorm, specs, grads_b, tl):
    n_tok = y.shape[0]
    n_t = n_tok // tl
    nq = tl // CHUNK
    n_g, n_r = len(grads_b), len(specs)

    def body(dys_ref, y_ref, z_ref, pre_ref, xin_ref, dtr_ref, hprev_ref, cw_ref, dtb_ref, alog_ref, dsk_ref, gn_ref,
             exp_ref, redx_ref, redq_ref, *rest):
        g_refs, rest = rest[0:n_g], rest[n_g:]
        dxin_ref, dz_ref, ddtr_ref, dcw_ref, dcb_ref, dgn_ref, ddsk_ref, da_ref, ddtb_ref = rest[0:9]
        recvs, rest = rest[9:9 + n_r], rest[9 + n_r:]
        dxs, dh, cs_s, dskc, shifts, dwp, sums_q, sums_s = rest[0:8]
        sems = rest[8:]
        i = pl.program_id(0)

        @pl.when(i == 0)
        def _():
            _direct_stage(0, specs, g_refs, recvs, sems)
            dcw_ref[...] = jnp.zeros_like(dcw_ref)
            dwp[...] = jnp.zeros_like(dwp)
            dcb_ref[...] = jnp.zeros_like(dcb_ref)
            dgn_ref[...] = jnp.zeros_like(dgn_ref)
            da_ref[...] = jnp.zeros_like(da_ref)
            ddtb_ref[...] = jnp.zeros_like(ddtb_ref)
            dskc[...] = jnp.zeros_like(dskc)
            dh[...] = jnp.zeros_like(dh)
            dxs[tl:tl + SUBLANES, :] = jnp.zeros((SUBLANES, D_XBC), F32)

        yv = y_ref[...]
        zv = z_ref[...]
        dysv = dys_ref[...]
        sz = _sig(zv)
        silz = zv * sz
        yz = yv * silz
        gw = D_SSM // N_GROUPS
        dyz_parts = []
        for g in range(N_GROUPS):
            sl = slice(g * gw, (g + 1) * gw)
            seg = yz[:, sl]
            r = lax.rsqrt(jnp.mean(seg * seg, axis=-1, keepdims=True) + RMS_EPS)
            yzn = seg * r
            dgn_ref[:, sl] += _colsum(dysv[:, sl] * yzn)
            dyzn = dysv[:, sl] * gn_ref[:, sl]
            dyz_parts.append(r * (dyzn - yzn * jnp.mean(dyzn * yzn, axis=-1, keepdims=True)))
        dyz = jnp.concatenate(dyz_parts, axis=1)
        dy = dyz * silz
        dz_ref[...] = dyz * yv * _dsilu(zv, sz)

        prev = pre_ref[...]
        sp = _sig(prev)
        xbc = prev * sp
        dskc[...] += _colsum(dy * xbc[:, 0:D_SSM])
        dt_in = dtr_ref[...] + dtb_ref[...]
        dt = _softplus(dt_in)
        dsp = _sig(dt_in)
        a = -jnp.exp(alog_ref[...])
        adt = dt * a
        for q in reversed(range(nq)):
            r0 = q * CHUNK
            cm = _chunk_common(adt[r0:r0 + CHUNK, :])
            row, col, lo = cm["row"], cm["col"], cm["lo"]
            triu = (col >= row).astype(F32)
            dt_c = dt[r0:r0 + CHUNK, :]
            dt_x = _per_head_lanes(dt_c, exp_ref)
            e_x = _per_head_lanes(cm["e"], exp_ref)
            dec_x = _per_head_lanes(cm["dec"], exp_ref)
            dcd_row = jnp.zeros((1, LANES), F32)
            for g in range(N_GROUPS):
                bcol = D_SSM + g * N_STATE
                ccol = D_SSM + (N_GROUPS + g) * N_STATE
                bg = xbc[r0:r0 + CHUNK, bcol:bcol + N_STATE].astype(BF16)
                cg_ = xbc[r0:r0 + CHUNK, ccol:ccol + N_STATE].astype(BF16)
                gm = _dot_nt(cg_, bg)
                dgm = jnp.zeros((CHUNK, CHUNK), F32)
                dbg = jnp.zeros((CHUNK, N_STATE), F32)
                dcg = jnp.zeros((CHUNK, N_STATE), F32)
                for k in range(N_HEADS // N_GROUPS // 2):
                    ha = (N_HEADS // N_GROUPS) * g + 2 * k
                    hb = ha + 1
                    c0 = ha * HEAD_DIM
                    xh2 = xbc[r0:r0 + CHUNK, c0:c0 + LANES]
                    dt2 = dt_x[:, c0:c0 + LANES]
                    x2 = xh2 * dt2
                    x2b = x2.astype(BF16)
                    la = _decay_mask(cm, ha)
                    lb = _decay_mask(cm, hb)
                    ma = gm * la
                    mb = gm * lb
                    dy2 = dy[r0:r0 + CHUNK, c0:c0 + LANES]
                    dy2b = dy2.astype(BF16)
                    dma = _dot_nt(jnp.where(lo, dy2, 0.0).astype(BF16), x2b)
                    dmb = _dot_nt(jnp.where(lo, 0.0, dy2).astype(BF16), x2b)
                    dx2 = jnp.where(lo, _dot(ma.T.astype(BF16), dy2b), _dot(mb.T.astype(BF16), dy2b))
                    dgm = dgm + dma * la + dmb * lb
                    sums_q[:, ha * CHUNK:(ha + 1) * CHUNK] = (dma * ma).astype(BF16)
                    sums_q[:, hb * CHUNK:(hb + 1) * CHUNK] = (dmb * mb).astype(BF16)
                    h2 = hprev_ref[q, c0:c0 + LANES, :]
                    h2b = h2.astype(BF16)
                    dz2 = dy2 * e_x[:, c0:c0 + LANES]
                    dcg = dcg + _dot(dz2.astype(BF16), h2b)
                    sums_s[0, :, c0:c0 + LANES] = (dz2 * _dot_nt(cg_, h2b)).astype(BF16)
                    dhn = dh[c0:c0 + LANES, :]
                    dhnb = dhn.astype(BF16)
                    cd_a = cm["cd"][:, ha:ha + 1]
                    cd_b = cm["cd"][:, hb:hb + 1]
                    top = row < HEAD_DIM
                    hh = dhn * h2
                    dcd_a = jnp.sum(_colsum(jnp.where(top, hh, 0.0)), axis=1, keepdims=True)
                    dcd_b = jnp.sum(_colsum(jnp.where(top, 0.0, hh)), axis=1, keepdims=True)
                    dcd_row = dcd_row + jnp.where(col[0:1, :] == ha, dcd_a * cd_a, 0.0) + jnp.where(col[0:1, :] == hb, dcd_b * cd_b, 0.0)
                    dh[c0:c0 + LANES, :] = jnp.where(top, cd_a, cd_b) * dhn + _dot(dz2.T.astype(BF16), cg_)
                    w2 = _dot_nt(bg, dhnb)
                    dec2 = dec_x[:, c0:c0 + LANES]
                    dx2 = dx2 + dec2 * w2
                    sums_s[1, :, c0:c0 + LANES] = (x2 * w2).astype(BF16)
                    dbg = dbg + _dot((x2 * dec2).astype(BF16), dhnb)
                    sums_s[2, :, c0:c0 + LANES] = (dx2 * xh2).astype(BF16)
                    dxs[r0:r0 + CHUNK, c0:c0 + LANES] = dx2 * dt2 + dsk_ref[:, c0:c0 + LANES] * dy2
                dgmb = dgm.astype(BF16)
                dxs[r0:r0 + CHUNK, bcol:bcol + N_STATE] = dbg + _dot(dgm.T.astype(BF16), cg_)
                dxs[r0:r0 + CHUNK, ccol:ccol + N_STATE] = dcg + _dot(dgmb, bg)
            q_all = sums_q[...]
            q_cols = _dot(jnp.ones((SUBLANES, CHUNK), BF16), q_all)
            cs_s[...] = jnp.zeros_like(cs_s)
            for h in range(N_HEADS):
                cs_s[h:h + 1, :] = q_cols[0:1, h * CHUNK:(h + 1) * CHUNK]
            de = _dot(sums_s[0], redx_ref[...])
            dd = _dot(sums_s[1], redx_ref[...]) * cm["dec"]
            ddtx = _dot(sums_s[2], redx_ref[...])
            is_last = row == CHUNK - 1
            dacs = _dot(q_all, redq_ref[...]) - cs_s[...].T + de - dd + jnp.where(is_last, dcd_row + _colsum(dd), 0.0)
            dadt = jnp.dot(triu, dacs, precision=HIGHEST, preferred_element_type=F32)
            da_ref[...] += _colsum(dadt * dt_c)
            ddtr_c = (dadt * a + ddtx) * dsp[r0:r0 + CHUNK, :]
            ddtr_ref[r0:r0 + CHUNK, :] = ddtr_c
            ddtb_ref[...] += _colsum(ddtr_c)

        dpre = dxs[0:tl, :] * _dsilu(prev, sp)
        dxs[0:tl, :] = dpre
        dcb_ref[...] += _colsum(dpre)
        _shift_copies(dxs, shifts, tl, SSM_BWD_SHIFTS)

        def strip(rb, carry):
            i0 = pl.multiple_of(rb * CONV_RS, CONV_RS)
            for c0 in range(0, D_XBC, CONV_CS):
                xin_s = xin_ref[pl.ds(i0, CONV_RS), c0:c0 + CONV_CS]
                acc = jnp.zeros((CONV_RS, CONV_CS), F32)
                for k in range(SSM_K):
                    sh = _tap_rows(dxs, shifts, SSM_K - 1 - k, i0, c0, SSM_BWD_SHIFTS)
                    acc = acc + sh * cw_ref[k:k + 1, c0:c0 + CONV_CS]
                    t = xin_s * sh
                    dwp[k * SUBLANES:(k + 1) * SUBLANES, c0:c0 + CONV_CS] += _fold_rows(t)
                dxin_ref[pl.ds(i0, CONV_RS), c0:c0 + CONV_CS] = acc
            return carry

        lax.fori_loop(0, tl // CONV_RS, strip, 0)
        dxs[tl:tl + SUBLANES, :] = dxs[0:SUBLANES, :]

        @pl.when(i == n_t - 1)
        def _():
            for k in range(SSM_K):
                dcw_ref[k:k + 1, :] = _colsum(dwp[k * SUBLANES:(k + 1) * SUBLANES, :])
            da_ref[...] = da_ref[...] * a
            sel = (lax.broadcasted_iota(jnp.int32, (D_SSM, LANES), 0) // HEAD_DIM
                   == lax.broadcasted_iota(jnp.int32, (D_SSM, LANES), 1)).astype(F32)
            rows = jnp.broadcast_to(dskc[...], (SUBLANES, D_SSM))
            ddsk_ref[...] = jnp.dot(rows, sel, precision=HIGHEST, preferred_element_type=F32)[0:1, :]
            _direct_stage(1, specs, g_refs, recvs, sems)

    head_lanes = _head_lane_matrix()
    per_head = (jnp.arange(N_HEADS * CHUNK)[:, None] // CHUNK == jnp.arange(LANES)[None, :]).astype(BF16)
    rev = functools.partial(_tile, tl, rev_of=n_t)
    return pl.pallas_call(
        body, grid=(n_t,), name="ssd_bwd",
        in_specs=[rev(D_SSM), rev(D_SSM), rev(D_SSM), rev(D_XBC), rev(D_XBC), rev(LANES),
                  pl.BlockSpec((nq, D_SSM, N_STATE), lambda i: (n_t - 1 - i, 0, 0)),
                  _full((SUBLANES, D_XBC)), _full((1, LANES)), _full((1, LANES)), _full((1, D_SSM)), _full((1, D_SSM)),
                  _full((LANES, D_SSM)), _full((D_SSM, LANES)), _full((N_HEADS * CHUNK, LANES))] + [IN_VMEM] * n_g,
        out_specs=[rev(D_XBC), rev(D_SSM), rev(LANES), _full((SUBLANES, D_XBC)), _full((1, D_XBC)), _full((1, D_SSM)),
                   _full((1, LANES)), _full((1, LANES)), _full((1, LANES))] + [ANY] * n_r,
        out_shape=[jax.ShapeDtypeStruct((n_tok, D_XBC), F32), jax.ShapeDtypeStruct((n_tok, D_SSM), F32),
                   jax.ShapeDtypeStruct((n_tok, LANES), F32), jax.ShapeDtypeStruct((SUBLANES, D_XBC), F32),
                   jax.ShapeDtypeStruct((1, D_XBC), F32), jax.ShapeDtypeStruct((1, D_SSM), F32),
                   jax.ShapeDtypeStruct((1, LANES), F32), jax.ShapeDtypeStruct((1, LANES), F32),
                   jax.ShapeDtypeStruct((1, LANES), F32)]
        + [jax.ShapeDtypeStruct((N_DEV, rows, cols), BF16) for rows, cols, _, _ in specs],
        scratch_shapes=[pltpu.VMEM((tl + SUBLANES, D_XBC), F32), pltpu.VMEM((D_SSM, N_STATE), F32),
                        pltpu.VMEM((CHUNK, LANES), F32), pltpu.VMEM((1, D_SSM), F32),
                        pltpu.VMEM((len(SSM_BWD_SHIFTS), tl, D_XBC), F32), pltpu.VMEM((SSM_K * SUBLANES, D_XBC), F32),
                        pltpu.VMEM((CHUNK, N_HEADS * CHUNK), BF16), pltpu.VMEM((3, CHUNK, D_SSM), BF16)]
        + _direct_sems(specs),
        compiler_params=_params(),
    )(dys, y, z, pre, xbc_in, dtr, hprev, cw, dtb, alog, dsk, gnorm, head_lanes, head_lanes.T, per_head, *grads_b)


CONF_HALO = 32
CONV_RS = 32
CONV_CS = 256


ALL_SHIFTS = tuple(range(1, SUBLANES))


def _shift_copies(buf, shifts, n_rows, residues=ALL_SHIFTS):
    for j, r in enumerate(residues):
        shifts[j, 0:n_rows, :] = buf[r:r + n_rows, :]


def _fold_rows(t):
    part = t[0:SUBLANES]
    for j in range(1, t.shape[0] // SUBLANES):
        part = part + t[j * SUBLANES:(j + 1) * SUBLANES]
    return part


def _tap_rows(buf, shifts, off, i0, c0, residues=ALL_SHIFTS):
    q, r = divmod(off, SUBLANES)
    rows = pl.ds(pl.multiple_of(i0 + SUBLANES * q, SUBLANES), CONV_RS)
    if r == 0:
        return buf[rows, c0:c0 + CONV_CS]
    return shifts[residues.index(r), rows, c0:c0 + CONV_CS]


def _conf_fwd(glu, cgate, bglu, cw, cb, lg, lb, tl):
    n_tok = glu.shape[0]

    def body(glu_ref, cg_ref, bglu_ref, cw_ref, cb_ref, lg_ref, lb_ref, u0_ref, u1_ref, yc_ref, buf, shifts):
        @pl.when(pl.program_id(0) == 0)
        def _():
            buf[0:CONF_HALO, :] = jnp.zeros((CONF_HALO, D_CONF), F32)

        gl = glu_ref[...] + bglu_ref[...]
        u0 = gl[:, 0:D_CONF] * _sig(gl[:, D_CONF:2 * D_CONF])
        u0_ref[...] = u0
        buf[CONF_HALO:CONF_HALO + tl, :] = u0
        _shift_copies(buf, shifts, tl + CONF_HALO - SUBLANES)

        def strip(rb, carry):
            i0 = pl.multiple_of(rb * CONV_RS, CONV_RS)
            for c0 in range(0, D_CONF, CONV_CS):
                acc = jnp.broadcast_to(cb_ref[:, c0:c0 + CONV_CS], (CONV_RS, CONV_CS))
                for k in range(CONF_K):
                    acc = acc + _tap_rows(buf, shifts, CONF_HALO - (CONF_K - 1) + k, i0, c0) * cw_ref[k:k + 1, c0:c0 + CONV_CS]
                u1_ref[pl.ds(i0, CONV_RS), c0:c0 + CONV_CS] = acc
            return carry

        lax.fori_loop(0, tl // CONV_RS, strip, 0)
        buf[0:CONF_HALO, :] = buf[tl:tl + CONF_HALO, :]
        xhat, _ = _ln_stats(u1_ref[...])
        n = xhat * lg_ref[...] + lb_ref[...]
        cgv = cg_ref[...]
        yc_ref[...] = (n * _sig(n) * (cgv * _sig(cgv))).astype(BF16)

    return pl.pallas_call(
        body, grid=(n_tok // tl,), name="conf_fwd",
        in_specs=[_tile(tl, 2 * D_CONF), _tile(tl, D_CONF), _full((1, 2 * D_CONF)), _full((CONF_HALO, D_CONF)),
                  _full((1, D_CONF)), _full((1, D_CONF)), _full((1, D_CONF))],
        out_specs=[_tile(tl, D_CONF)] * 3,
        out_shape=[jax.ShapeDtypeStruct((n_tok, D_CONF), F32), jax.ShapeDtypeStruct((n_tok, D_CONF), F32),
                   jax.ShapeDtypeStruct((n_tok, D_CONF), BF16)],
        scratch_shapes=[pltpu.VMEM((tl + CONF_HALO, D_CONF), F32),
                        pltpu.VMEM((SUBLANES - 1, tl + CONF_HALO - SUBLANES, D_CONF), F32)],
        compiler_params=_params(),
    )(glu, cgate, bglu, cw, cb, lg, lb)


def _conf_bwd(dyc, dyc_block, u0, u1, glu, cgate, bglu, cw, lg, lb, specs, grads_b, tl):
    n_tok = glu.shape[0]
    n_t = n_tok // tl
    n_g, n_r = len(grads_b), len(specs)

    def body(dyc_ref, u0_ref, u1_ref, glu_ref, cg_ref, bglu_ref, cw_ref, lg_ref, lb_ref, *rest):
        g_refs, rest = rest[0:n_g], rest[n_g:]
        dglu_ref, dcg_ref, dcw_ref, dbglu_ref, small_ref = rest[0:5]
        recvs, rest = rest[5:5 + n_r], rest[5 + n_r:]
        buf, shifts, du0_s, dwp = rest[0:4]
        sems = rest[4:]

        @pl.when(pl.program_id(0) == 0)
        def _():
            _direct_stage(0, specs, g_refs, recvs, sems)
            dwp[...] = jnp.zeros_like(dwp)
            dbglu_ref[...] = jnp.zeros_like(dbglu_ref)
            small_ref[...] = jnp.zeros_like(small_ref)
            buf[tl:tl + CONF_HALO, :] = jnp.zeros((CONF_HALO, D_CONF), F32)

        xhat, rstd = _ln_stats(u1_ref[...])
        n = xhat * lg_ref[...] + lb_ref[...]
        sn = _sig(n)
        cgv = cg_ref[...]
        scg = _sig(cgv)
        dycv = dyc_ref[...]
        dcg_ref[...] = dycv * (n * sn) * _dsilu(cgv, scg)
        dn = dycv * (cgv * scg) * _dsilu(n, sn)
        small_ref[0:1, :] += _colsum(dn * xhat)
        small_ref[1:2, :] += _colsum(dn)
        du1 = _ln_bwd(dn * lg_ref[...], xhat, rstd)
        small_ref[2:3, :] += _colsum(du1)
        buf[0:tl, :] = du1
        _shift_copies(buf, shifts, tl + CONF_HALO - SUBLANES)

        def strip(rb, carry):
            i0 = pl.multiple_of(rb * CONV_RS, CONV_RS)
            for c0 in range(0, D_CONF, CONV_CS):
                u0s = u0_ref[pl.ds(i0, CONV_RS), c0:c0 + CONV_CS]
                acc = jnp.zeros((CONV_RS, CONV_CS), F32)
                for k in range(CONF_K):
                    sh = _tap_rows(buf, shifts, CONF_K - 1 - k, i0, c0)
                    acc = acc + sh * cw_ref[k:k + 1, c0:c0 + CONV_CS]
                    t = u0s * sh
                    dwp[k * SUBLANES:(k + 1) * SUBLANES, c0:c0 + CONV_CS] += _fold_rows(t)
                du0_s[pl.ds(i0, CONV_RS), c0:c0 + CONV_CS] = acc
            return carry

        lax.fori_loop(0, tl // CONV_RS, strip, 0)
        du0 = du0_s[...]
        buf[tl:tl + CONF_HALO, :] = buf[0:CONF_HALO, :]
        gl = glu_ref[...] + bglu_ref[...]
        sg = _sig(gl[:, D_CONF:2 * D_CONF])
        dgv = du0 * sg
        dgg = du0 * gl[:, 0:D_CONF] * sg * (1.0 - sg)
        dglu_ref[:, 0:D_CONF] = dgv
        dglu_ref[:, D_CONF:2 * D_CONF] = dgg
        dbglu_ref[:, 0:D_CONF] += _colsum(dgv)
        dbglu_ref[:, D_CONF:2 * D_CONF] += _colsum(dgg)

        @pl.when(pl.program_id(0) == n_t - 1)
        def _():
            for k in range(CONF_HALO):
                dcw_ref[k:k + 1, :] = _colsum(dwp[k * SUBLANES:(k + 1) * SUBLANES, :])
            _direct_stage(1, specs, g_refs, recvs, sems)

    rev = functools.partial(_tile, tl, rev_of=n_t)
    return pl.pallas_call(
        body, grid=(n_t,), name="conf_bwd",
        in_specs=[pl.BlockSpec((tl, D_CONF), lambda i: (n_t - 1 - i, dyc_block)),
                  rev(D_CONF), rev(D_CONF), rev(2 * D_CONF), rev(D_CONF), _full((1, 2 * D_CONF)),
                  _full((CONF_HALO, D_CONF)), _full((1, D_CONF)), _full((1, D_CONF))] + [IN_VMEM] * n_g,
        out_specs=[rev(2 * D_CONF), rev(D_CONF), _full((CONF_HALO, D_CONF)), _full((1, 2 * D_CONF)), _full((SUBLANES, D_CONF))]
        + [ANY] * n_r,
        out_shape=[jax.ShapeDtypeStruct((n_tok, 2 * D_CONF), F32), jax.ShapeDtypeStruct((n_tok, D_CONF), F32),
                   jax.ShapeDtypeStruct((CONF_HALO, D_CONF), F32), jax.ShapeDtypeStruct((1, 2 * D_CONF), F32),
                   jax.ShapeDtypeStruct((SUBLANES, D_CONF), F32)]
        + [jax.ShapeDtypeStruct((N_DEV, rows, cols), BF16) for rows, cols, _, _ in specs],
        scratch_shapes=[pltpu.VMEM((tl + CONF_HALO, D_CONF), F32),
                        pltpu.VMEM((SUBLANES - 1, tl + CONF_HALO - SUBLANES, D_CONF), F32),
                        pltpu.VMEM((tl, D_CONF), F32), pltpu.VMEM((CONF_HALO * SUBLANES, D_CONF), F32)] + _direct_sems(specs),
        compiler_params=_params(),
    )(dyc, u0, u1, glu, cgate, bglu, cw, lg, lb, *grads_b)


def _tail(x, yssm, yconf, p, tgt, vec, w_out, wpg, wpp, tl):
    n_tok = x.shape[0]

    def body(x_ref, ys_ref, yc_ref, p_ref, t_ref, vec_ref, wo_ref, wg_ref, wp_ref,
             dmix_ref, dr1_ref, dr1b_ref, h1b_ref, dgb_ref, dpb_ref, small_ref, loss_ref):
        @pl.when(pl.program_id(0) == 0)
        def _():
            small_ref[...] = jnp.zeros_like(small_ref)
            loss_ref[...] = jnp.zeros_like(loss_ref)

        xh0, _ = _ln_stats(x_ref[...])
        h0 = xh0 * vec_ref[0:1, :] + vec_ref[1:2, :]
        out = _dot(ys_ref[...], wo_ref[0:D_SSM, :]) + _dot(yc_ref[...], wo_ref[D_SSM:D_SSM + D_CONF, :]) + vec_ref[2:3, :]
        xh1, rstd1 = _ln_stats(ALPHA * h0 + out)
        h1 = xh1 * vec_ref[3:4, :] + vec_ref[4:5, :]
        h1b = h1.astype(BF16)
        h1b_ref[...] = h1b
        gate = _sig(_dot(h1b, wg_ref[...]))
        ple = _dot(p_ref[...].astype(BF16), wp_ref[...])
        xh2, rstd2 = _ln_stats(ALPHA * h1 + gate * ple)
        h2 = xh2 * vec_ref[5:6, :] + vec_ref[6:7, :]
        diff = h2 - t_ref[...]
        part = jnp.sum(jnp.sum(diff * diff, axis=1, keepdims=True), axis=0, keepdims=True) * (0.5 / D_MODEL)
        loss_ref[...] += jnp.broadcast_to(part, loss_ref.shape)
        dh2 = diff * (1.0 / D_MODEL)
        small_ref[3:4, :] += _colsum(dh2 * xh2)
        small_ref[4:5, :] += _colsum(dh2)
        dr2 = _ln_bwd(dh2 * vec_ref[5:6, :], xh2, rstd2)
        dgpre = (dr2 * ple * gate * (1.0 - gate)).astype(BF16)
        dgb_ref[...] = dgpre
        dpb_ref[...] = (dr2 * gate).astype(BF16)
        dh1 = ALPHA * dr2 + _dot_nt(dgpre, wg_ref[...])
        small_ref[1:2, :] += _colsum(dh1 * xh1)
        small_ref[2:3, :] += _colsum(dh1)
        dr1 = _ln_bwd(dh1 * vec_ref[3:4, :], xh1, rstd1)
        small_ref[0:1, :] += _colsum(dr1)
        dr1_ref[...] = dr1
        dr1b = dr1.astype(BF16)
        dr1b_ref[...] = dr1b
        dmix_ref[...] = _dot_nt(dr1b, wo_ref[...])

    d_mix = D_SSM + D_CONF
    return pl.pallas_call(
        body, grid=(n_tok // tl,), name="tail",
        in_specs=[_tile(tl, D_MODEL), _tile(tl, D_SSM), _tile(tl, D_CONF), _tile(tl, D_PLE), _tile(tl, D_MODEL),
                  _full((SUBLANES, D_MODEL)), _full((d_mix, D_MODEL), True), _full((D_MODEL, D_MODEL), True),
                  _full((D_PLE, D_MODEL), True)],
        out_specs=[_tile(tl, d_mix), _tile(tl, D_MODEL), _tile(tl, D_MODEL), _tile(tl, D_MODEL), _tile(tl, D_MODEL),
                   _tile(tl, D_MODEL), _full((SUBLANES, D_MODEL)), _full((SUBLANES, LANES))],
        out_shape=[jax.ShapeDtypeStruct((n_tok, d_mix), F32), jax.ShapeDtypeStruct((n_tok, D_MODEL), F32),
                   jax.ShapeDtypeStruct((n_tok, D_MODEL), BF16), jax.ShapeDtypeStruct((n_tok, D_MODEL), BF16),
                   jax.ShapeDtypeStruct((n_tok, D_MODEL), BF16), jax.ShapeDtypeStruct((n_tok, D_MODEL), BF16),
                   jax.ShapeDtypeStruct((SUBLANES, D_MODEL), F32), jax.ShapeDtypeStruct((SUBLANES, LANES), F32)],
        compiler_params=_params(),
    )(x, yssm, yconf, p, tgt, vec, w_out, wpg, wpp)


def _inproj_bwd(dxin, dz, dglu, dcg, ddtr, dr1, x, g, b, w_r, tl):
    n_tok = x.shape[0]

    def body(dxin_ref, dz_ref, dglu_ref, dcg_ref, ddtr_ref, dr1_ref, x_ref, g_ref, b_ref, w_ref,
             dx_ref, dpb_ref, small_ref):
        @pl.when(pl.program_id(0) == 0)
        def _():
            small_ref[...] = jnp.zeros_like(small_ref)

        dh0 = ALPHA * dr1_ref[...]
        for ref, lo, hi in ((dxin_ref, R_XBC, R_Z), (dz_ref, R_Z, R_GLU), (dglu_ref, R_GLU, R_CG), (dcg_ref, R_CG, R_DT),
                            (ddtr_ref, R_DT, D_INR)):
            piece = ref[...].astype(BF16)
            dpb_ref[:, lo:hi] = piece
            dh0 = dh0 + _dot_nt(piece, w_ref[:, lo:hi])
        xhat, rstd = _ln_stats(x_ref[...])
        small_ref[0:1, :] += _colsum(dh0 * xhat)
        small_ref[1:2, :] += _colsum(dh0)
        dx_ref[...] = _ln_bwd(dh0 * g_ref[...], xhat, rstd)

    return pl.pallas_call(
        body, grid=(n_tok // tl,), name="inproj_bwd",
        in_specs=[_tile(tl, D_XBC), _tile(tl, D_SSM), _tile(tl, 2 * D_CONF), _tile(tl, D_CONF), _tile(tl, LANES),
                  _tile(tl, D_MODEL), _tile(tl, D_MODEL), _full((1, D_MODEL)), _full((1, D_MODEL)),
                  _full((D_MODEL, D_INR), True)],
        out_specs=[_tile(tl, D_MODEL), _tile(tl, D_INR), _full((SUBLANES, D_MODEL))],
        out_shape=[jax.ShapeDtypeStruct((n_tok, D_MODEL), F32), jax.ShapeDtypeStruct((n_tok, D_INR), BF16),
                   jax.ShapeDtypeStruct((SUBLANES, D_MODEL), F32)],
        compiler_params=_params(),
    )(dxin, dz, dglu, dcg, ddtr, dr1, x, g, b, w_r)


def _tn_matmul(a, b, name, tn, tl, out_dtype=F32):
    n_tok, m = a.shape
    n = b.shape[1]
    n_l = n_tok // tl
    direct = out_dtype == F32

    def body(a_ref, b_ref, o_ref, *scratch):
        acc = o_ref if direct else scratch[0]

        @pl.when(pl.program_id(1) == 0)
        def _():
            acc[...] = jnp.zeros_like(acc)

        acc[...] += lax.dot_general(a_ref[...], b_ref[...], TN_DIMS, preferred_element_type=F32)
        if not direct:
            @pl.when(pl.program_id(1) == n_l - 1)
            def _():
                o_ref[...] = acc[...].astype(out_dtype)

    return pl.pallas_call(
        body, grid=(n // tn, n_l), name=name,
        in_specs=[pl.BlockSpec((tl, m), lambda j, l: (l, 0)), pl.BlockSpec((tl, tn), lambda j, l: (l, j))],
        out_specs=pl.BlockSpec((m, tn), lambda j, l: (0, j)),
        out_shape=jax.ShapeDtypeStruct((m, n), out_dtype),
        scratch_shapes=[] if direct else [pltpu.VMEM((m, tn), F32)],
        compiler_params=pltpu.CompilerParams(dimension_semantics=("parallel", "arbitrary"), vmem_limit_bytes=VMEM_LIMIT),
    )(a, b)


def _pad_rows(a, rows):
    return jnp.pad(a, ((0, rows - a.shape[0]), (0, 0)))


def _pad_lanes(a):
    return jnp.pad(a, ((0, 0), (0, LANES - a.shape[1])))


def _local_grads(x, p, tgt, w_r, later, ssm_cw, conf_cw, sm):
    n_tok = x.shape[0]
    tl = min(256, n_tok)
    tlm = min(512, n_tok)
    row = lambda v: v.reshape(1, -1)
    g_e, b_e = row(sm["ln_emb_g"]), row(sm["ln_emb_b"])
    h0b, xbc_in, z, glu, cgate, dtr, w_out_all, wpg_all, wpp_all = _ln_inproj(x, g_e, b_e, w_r, later, tlm)
    w_out_b = w_out_all.reshape(D_SSM + D_CONF, D_MODEL)
    wpg_b = wpg_all.reshape(D_MODEL, D_MODEL)
    wpp_b = wpp_all.transpose(1, 0, 2).reshape(D_PLE, D_MODEL)

    cw4 = _pad_rows(ssm_cw, SUBLANES)
    dtb, alog = _pad_lanes(sm["dt_bias"]), _pad_lanes(sm["a_log"])
    dsk = jnp.repeat(sm["d_skip"], HEAD_DIM, axis=1)
    pre, y, yssm, hprev = _ssd_fwd(xbc_in, z, dtr, cw4, sm["ssm_conv_b"], dtb, alog, dsk, sm["ssm_norm_g"], tl)

    cw31 = _pad_rows(conf_cw, CONF_HALO)
    u0, u1, yconf = _conf_fwd(glu, cgate, sm["b_glu"], cw31, sm["conf_conv_b"], sm["conf_ln_g"], sm["conf_ln_b"], tl)

    vec = jnp.concatenate([g_e, b_e, sm["b_out"], sm["ln1_g"], sm["ln1_b"], sm["ln2_g"], sm["ln2_b"],
                           jnp.zeros((1, D_MODEL), F32)], axis=0)
    dmix, dr1, dr1b, h1b, dgb, dpb, small_t, loss = _tail(
        x, yssm, yconf, p, tgt, vec, w_out_b, wpg_b, wpp_b, tlm)

    dwo = [_tn_matmul(yssm, dr1b, "dw_out_ssm", D_MODEL, tlm, BF16), _tn_matmul(yconf, dr1b, "dw_out_conf", D_MODEL, tlm, BF16)]
    dwp_ = [_tn_matmul(h1b, dgb, "dw_ple_gate", D_MODEL, tlm, BF16),
            _tn_matmul(p.astype(BF16), dpb, "dw_ple_proj", D_MODEL, tlm, BF16)]
    dglu, dcg, dcw31, dbglu, small_c, dw_out = _conf_bwd(dmix, D_SSM // D_CONF, u0, u1, glu, cgate, sm["b_glu"], cw31,
                                                          sm["conf_ln_g"], sm["conf_ln_b"], [W_OUT_DIRECT], dwo, tl)
    dxin, dz, ddtr, dcw4, dcb4, dgn, ddsk, dalog, ddtb, dwpg, dwpp = _ssd_bwd(
        dmix, y, z, pre, xbc_in, dtr, hprev, cw4, dtb, alog, dsk, sm["ssm_norm_g"], [WPG_DIRECT, WPP_DIRECT], dwp_, tl)
    dx, dprojb, small_e = _inproj_bwd(dxin, dz, dglu, dcg, ddtr, dr1, x, g_e, b_e, w_r, tl)

    dw_r = _tn_matmul(h0b, dprojb, "dw_in", D_INR // 3, tlm)
    grads = dict(
        ln_emb_g=small_e[0], ln_emb_b=small_e[1], w_in=dw_r, ssm_conv_w=dcw4[0:SSM_K], ssm_conv_b=dcb4,
        dt_bias=ddtb[:, 0:N_HEADS], a_log=dalog[:, 0:N_HEADS], d_skip=ddsk[:, 0:N_HEADS], ssm_norm_g=dgn, b_glu=dbglu,
        conf_conv_w=dcw31[0:CONF_K], conf_conv_b=small_c[2:3], conf_ln_g=small_c[0:1], conf_ln_b=small_c[1:2],
        w_out=dw_out, b_out=small_t[0:1], ln1_g=small_t[1:2], ln1_b=small_t[2:3], w_ple_gate=dwpg, w_ple_proj=dwpp,
        ln2_g=small_t[3:4], ln2_b=small_t[4:5])
    return loss[0, 0], dx, grads


N_CHIPS = 4
N_DEV = 8
W_IN_SH = D_IN // N_CHIPS
BIG = (("w_in", D_MODEL, W_IN_SH), ("w_out", (D_SSM + D_CONF) // N_CHIPS, D_MODEL),
       ("w_ple_gate", D_MODEL // N_CHIPS, D_MODEL), ("w_ple_proj", D_PLE, D_MODEL // N_CHIPS))
SEGS = ((R_XBC, 0, D_SSM), (R_XBC + D_SSM, 2048, 256), (R_XBC + D_SSM + 256, 2304, 256), (R_Z, 1024, D_SSM),
        (R_GLU, 2576, 2 * D_CONF), (R_CG, 4624, D_CONF), (R_DT, 2560, N_HEADS))
ROWS_CW4 = 2
ROWS_CW31 = 8
ROWS_CONV = 16
SMALL_ROWS = 56
SMALL_LAYOUT = (("ln_emb_g", 0, 1), ("ln_emb_b", 1, 1), ("ssm_conv_b", 2, 2), ("dt_bias", 4, 1), ("a_log", 5, 1),
                ("d_skip", 6, 1), ("ssm_norm_g", 7, 1), ("b_glu", 8, 2), ("conf_conv_b", 10, 1), ("conf_ln_g", 11, 1),
                ("conf_ln_b", 12, 1), ("b_out", 13, 1), ("ln1_g", 14, 1), ("ln1_b", 15, 1), ("ln2_g", 16, 1), ("ln2_b", 17, 1))
CONV_LAYOUT = (("ssm_conv_w", 18, 6, (SSM_K, D_XBC)), ("conf_conv_w", 24, 31, (CONF_K, D_CONF)))


def _rows_of(v, rows):
    flat = v.reshape(-1)
    return jnp.pad(flat, (0, rows * D_MODEL - flat.shape[0])).reshape(rows, D_MODEL)


LOSS_ROW = 55


def _pack_small(d, loss_share=None):
    parts = [_rows_of(d[n], r) for n, _, r in SMALL_LAYOUT]
    for n, _, r, _ in CONV_LAYOUT:
        parts.append(_rows_of(d[n], r) if n in d else jnp.zeros((r, D_MODEL), F32))
    parts.append(_rows_of(jnp.zeros((1,), F32) if loss_share is None else loss_share, SMALL_ROWS - LOSS_ROW))
    return jnp.concatenate(parts, axis=0)


def _w_r_from_shards(w4):
    parts = []
    for _, o, wd in SEGS:
        lo, hi = o, o + wd
        while lo < hi:
            s = lo // W_IN_SH
            e = min(hi, (s + 1) * W_IN_SH)
            parts.append(w4[s][:, lo - s * W_IN_SH:e - s * W_IN_SH])
            lo = e
    parts.append(jnp.zeros((D_MODEL, LANES - N_HEADS), w4.dtype))
    return jnp.concatenate(parts, axis=1)


def _shards_from_dw_r(dw_r):
    slabs = []
    for s in range(N_CHIPS):
        lo, hi = s * W_IN_SH, (s + 1) * W_IN_SH
        parts = []
        for kcol, o, wd in sorted(SEGS, key=lambda t: t[1]):
            a, b = max(lo, o), min(hi, o + wd)
            if a < b:
                parts.append(dw_r[:, kcol + a - o:kcol + b - o])
        slabs.append(jnp.concatenate(parts, axis=1))
    return jnp.stack(slabs)


def _row_chunks(rows, n):
    return [(j * (rows // n), rows // n) for j in range(n)]


def _my_place():
    return lax.axis_index("x"), lax.axis_index("y"), lax.axis_index("c")


MESH_ID = pl.DeviceIdType.MESH
ANY = pl.BlockSpec(memory_space=pl.ANY)
IN_VMEM = pl.BlockSpec(memory_space=pltpu.VMEM)
CHIP_FLIPS = ((1, 0), (0, 1), (1, 1))


def _remote(src, dst, send_sem, recv_sem, peer):
    return pltpu.make_async_remote_copy(src, dst, send_sem, recv_sem, device_id=peer, device_id_type=MESH_ID)


GATHER_CHUNKS = (4, 2, 1, 1)


def _gather_plans(kinds):
    plan = [(a, o, n, rows // 2) for a, (rows, ch) in enumerate(kinds) for o, n in _row_chunks(rows // 2, ch)]
    own_plan = [(a, o, n) for a, (rows, ch) in enumerate(kinds) for o, n in _row_chunks(rows, 2 * ch)]
    return plan, own_plan


def _gather_sems(plan, own_plan):
    hop = pltpu.SemaphoreType.DMA((3, len(plan)))
    return [hop, hop, hop, hop, pltpu.SemaphoreType.DMA((len(own_plan),))]


def _gather_stage(stage, ins, outs, plan, own_plan, sems):
    send_a, recv_a, send_b, recv_b, loc = sems
    x, y, c = _my_place()
    s = 2 * x + y
    sibling = (x, y, 1 - c)
    own = [pltpu.make_async_copy(ins[a].at[pl.ds(o, n)], outs[a].at[s, pl.ds(o, n)], loc.at[j])
           for j, (a, o, n) in enumerate(own_plan)]
    first, arrive, passed, arrive_b = [], [], [], []
    for k, (fx, fy) in enumerate(CHIP_FLIPS):
        peer = (x ^ fx, y ^ fy, c)
        sk = 2 * (x ^ fx) + (y ^ fy)
        for j, (a, o, n, h) in enumerate(plan):
            mine = pl.ds(pl.multiple_of(c * h + o, 16), n)
            theirs = pl.ds(pl.multiple_of((1 - c) * h + o, 16), n)
            first.append(_remote(ins[a].at[mine], outs[a].at[s, mine], send_a.at[k, j], recv_a.at[k, j], peer))
            land = outs[a].at[sk, mine]
            arrive.append(_remote(land, land, send_a.at[k, j], recv_a.at[k, j], peer))
            passed.append(_remote(land, land, send_b.at[k, j], recv_b.at[k, j], sibling))
            land_b = outs[a].at[sk, theirs]
            arrive_b.append(_remote(land_b, land_b, send_b.at[k, j], recv_b.at[k, j], sibling))
    if stage == 0:
        for cp in own + first:
            cp.start()
    elif stage == 1:
        for got, fwd in zip(arrive, passed):
            got.wait_recv()
            fwd.start()
    else:
        for got in arrive_b:
            got.wait_recv()
        for cp in first + passed:
            cp.wait_send()
        for cp in own:
            cp.wait()


W_OUT_DIRECT = (512, D_MODEL, 2, ((0, 0, 0), (0, 512, 0), (1, 0, 0), (1, 512, 0)))
WPG_DIRECT = (256, D_MODEL, 1, ((0, 0, 0), (0, 256, 0), (0, 512, 0), (0, 768, 0)))
WPP_DIRECT = (D_PLE, 256, 1, ((1, 0, 0), (1, 0, 256), (1, 0, 512), (1, 0, 768)))


def _direct_sems(specs):
    out = []
    for _, _, copies, _ in specs:
        out += [pltpu.SemaphoreType.DMA((N_CHIPS, 2, copies)), pltpu.SemaphoreType.DMA((N_DEV, copies)),
                pltpu.SemaphoreType.DMA((copies,))]
    return out


def _direct_stage(stage, specs, srcs, recvs, sems):
    x, y, c = _my_place()
    me = 4 * x + 2 * y + c
    s = 2 * x + y
    for r, (rows, cols, copies, where) in enumerate(specs):
        send_sems, recv_sems, loc_sems = sems[3 * r:3 * r + 3]
        for k, (o, n) in enumerate(_row_chunks(rows, copies)):
            for t, (si, row0, col0) in enumerate(where):
                src = srcs[si].at[pl.ds(row0 + o, n), pl.ds(col0, cols)]
                dst = recvs[r].at[me, pl.ds(o, n)]
                for cc in range(2):
                    to_self = jnp.logical_and(s == t, c == cc)
                    away = _remote(src, dst, send_sems.at[t, cc, k], recv_sems.at[me, k], (t // 2, t % 2, cc))
                    here = pltpu.make_async_copy(src, dst, loc_sems.at[k])

                    @pl.when(to_self)
                    def _():
                        here.start() if stage == 0 else here.wait()

                    @pl.when(jnp.logical_not(to_self))
                    def _():
                        away.start() if stage == 0 else away.wait_send()
            if stage == 1:
                for j in range(N_DEV):
                    land = recvs[r].at[j, pl.ds(o, n)]

                    @pl.when(j != me)
                    def _():
                        _remote(land, land, send_sems.at[0, 0, k], recv_sems.at[j, k], (0, 0, 0)).wait_recv()


def _gather_weights(w_in_b, conv_f):
    plan, own_plan = _gather_plans([(BIG[0][1], GATHER_CHUNKS[0])])

    def body(w_ref, conv_ref, w_all, conv_all, *sems):
        conv_send, conv_recv, conv_loc = sems[5:]
        x, y, c = _my_place()
        s = 2 * x + y
        conv_own = pltpu.make_async_copy(conv_ref, conv_all.at[s], conv_loc)
        conv_out = [_remote(conv_ref, conv_all.at[s], conv_send.at[k], conv_recv.at[k], (x ^ fx, y ^ fy, c))
                    for k, (fx, fy) in enumerate(CHIP_FLIPS)]
        _gather_stage(0, [w_ref], [w_all], plan, own_plan, sems[0:5])
        for cp in [conv_own] + conv_out:
            cp.start()
        _gather_stage(1, [w_ref], [w_all], plan, own_plan, sems[0:5])
        _gather_stage(2, [w_ref], [w_all], plan, own_plan, sems[0:5])
        for cp in conv_out:
            cp.wait()
        conv_own.wait()

    arrays = [w_in_b, conv_f]
    return pl.pallas_call(
        body, name="gather_weights", in_specs=[IN_VMEM] * 2, out_specs=[ANY] * 2,
        out_shape=[jax.ShapeDtypeStruct((N_CHIPS,) + a.shape, a.dtype) for a in arrays],
        scratch_shapes=_gather_sems(plan, own_plan)
        + [pltpu.SemaphoreType.DMA((3,)), pltpu.SemaphoreType.DMA((3,)), pltpu.SemaphoreType.DMA],
    )(*arrays)


CORES_CHUNKS = (4, 2, 1, 1)
CHIPS_CHUNKS = (2, 1, 1, 1)
SHARE_CHUNKS = (8, 4, 2, 1)


def _exchange_cores(gbig, spack):
    n_big = len(gbig)
    plan = [(a, t, o, n, g.shape[1] // 2) for a, g in enumerate(gbig) for t in range(g.shape[0])
            for o, n in _row_chunks(g.shape[1] // 2, CORES_CHUNKS[a] * N_CHIPS // g.shape[0])]

    def body(*refs):
        g_refs, s_ref = refs[0:n_big], refs[n_big]
        theirs, small_all = refs[n_big + 1:2 * n_big + 1], refs[2 * n_big + 1]
        send_sems, recv_sems, send_small, recv_small, loc_sem = refs[2 * n_big + 2:]
        x, y, c = _my_place()
        me = 4 * x + 2 * y + c
        own = pltpu.make_async_copy(s_ref, small_all.at[me], loc_sem)
        sends = []
        for j, (a, t, o, n, h) in enumerate(plan):
            give = pl.multiple_of((1 - c) * h + o, SUBLANES)
            sends.append(_remote(g_refs[a].at[t, pl.ds(give, n)], theirs[a].at[t, pl.ds(o, n)], send_sems.at[j], recv_sems.at[j],
                                 (x, y, 1 - c)))
        for m in range(1, N_DEV):
            peer = (x ^ (m >> 2), y ^ ((m >> 1) & 1), c ^ (m & 1))
            sends.append(_remote(s_ref, small_all.at[me], send_small.at[m - 1], recv_small.at[m - 1], peer))
        own.start()
        for cp in sends:
            cp.start()
        for cp in sends:
            cp.wait()
        own.wait()

    return pl.pallas_call(
        body, name="exchange_cores", in_specs=[ANY] * n_big + [IN_VMEM], out_specs=[ANY] * (n_big + 1),
        out_shape=[jax.ShapeDtypeStruct((g.shape[0], g.shape[1] // 2, g.shape[2]), F32) for g in gbig]
        + [jax.ShapeDtypeStruct((N_DEV, SMALL_ROWS, D_MODEL), F32)],
        scratch_shapes=[pltpu.SemaphoreType.DMA((len(plan),)), pltpu.SemaphoreType.DMA((len(plan),)),
                        pltpu.SemaphoreType.DMA((N_DEV - 1,)), pltpu.SemaphoreType.DMA((N_DEV - 1,)), pltpu.SemaphoreType.DMA],
    )(*gbig, spack)


def _exchange_chips(psums):
    n_big = len(psums)
    plan = [(a, o, n) for a, ps in enumerate(psums) for o, n in _row_chunks(ps.shape[1], CHIPS_CHUNKS[a])]

    def body(*refs):
        p_refs, got = refs[0:n_big], refs[n_big:2 * n_big]
        send_sems, recv_sems = refs[2 * n_big:]
        x, y, c = _my_place()
        sends = []
        for k, (fx, fy) in enumerate(CHIP_FLIPS):
            tx, ty = x ^ fx, y ^ fy
            for j, (a, o, n) in enumerate(plan):
                sends.append(_remote(p_refs[a].at[2 * tx + ty, pl.ds(o, n)], got[a].at[k, pl.ds(o, n)], send_sems.at[k, j],
                                     recv_sems.at[k, j], (tx, ty, c)))
        for cp in sends:
            cp.start()
        for cp in sends:
            cp.wait()

    return pl.pallas_call(
        body, name="exchange_chips", in_specs=[ANY] * n_big, out_specs=[ANY] * n_big,
        out_shape=[jax.ShapeDtypeStruct((3,) + ps.shape[1:], BF16) for ps in psums],
        scratch_shapes=[pltpu.SemaphoreType.DMA((3, len(plan))), pltpu.SemaphoreType.DMA((3, len(plan)))],
    )(*psums)


def _share_halves(tots):
    n_big = len(tots)
    plan = [(a, o, n) for a, tt in enumerate(tots) for o, n in _row_chunks(tt.shape[0], SHARE_CHUNKS[a])]

    def body(*refs):
        t_refs, both = refs[0:n_big], refs[n_big:2 * n_big]
        send_sems, recv_sems, loc = refs[2 * n_big:]
        x, y, c = _my_place()
        own = [pltpu.make_async_copy(t_refs[a].at[pl.ds(o, n)], both[a].at[c, pl.ds(o, n)], loc.at[j])
               for j, (a, o, n) in enumerate(plan)]
        sends = [_remote(t_refs[a].at[pl.ds(o, n)], both[a].at[c, pl.ds(o, n)], send_sems.at[j], recv_sems.at[j], (x, y, 1 - c))
                 for j, (a, o, n) in enumerate(plan)]
        for cp in own + sends:
            cp.start()
        for cp in sends:
            cp.wait()
        for cp in own:
            cp.wait()

    return pl.pallas_call(
        body, name="share_halves", in_specs=[IN_VMEM] * n_big, out_specs=[ANY] * n_big,
        out_shape=[jax.ShapeDtypeStruct((2,) + tt.shape, F32) for tt in tots],
        scratch_shapes=[pltpu.SemaphoreType.DMA((len(plan),)), pltpu.SemaphoreType.DMA((len(plan),)),
                        pltpu.SemaphoreType.DMA((len(plan),))],
    )(*tots)


def _sum_rows(half):
    return half if half <= 256 else 256


def _add_cores(core, g, theirs, name):
    n_slabs, rows, cols = g.shape
    half = rows // 2
    tr = _sum_rows(half)
    nb = half // tr

    def body(c_ref, g_ref, t_ref, o_ref):
        o_ref[...] = (g_ref[...] + t_ref[...]).astype(BF16)

    grid_spec = pltpu.PrefetchScalarGridSpec(
        num_scalar_prefetch=1, grid=(n_slabs, nb),
        in_specs=[pl.BlockSpec((1, tr, cols), lambda t, i, c_ref: (t, c_ref[0] * nb + i, 0)),
                  pl.BlockSpec((1, tr, cols), lambda t, i, c_ref: (t, i, 0))],
        out_specs=pl.BlockSpec((1, tr, cols), lambda t, i, c_ref: (t, i, 0)))
    return pl.pallas_call(body, grid_spec=grid_spec, name=name, out_shape=jax.ShapeDtypeStruct((n_slabs, half, cols), BF16),
                          compiler_params=pltpu.CompilerParams(vmem_limit_bytes=VMEM_LIMIT))(core, g, theirs)


def _add_chips(chip, psum, got, name):
    _, half, cols = psum.shape
    tr = _sum_rows(half)

    def body(s_ref, p_ref, g_ref, o_ref):
        o_ref[...] = ((p_ref[0].astype(F32) + g_ref[0].astype(F32)) + g_ref[1].astype(F32)) + g_ref[2].astype(F32)

    grid_spec = pltpu.PrefetchScalarGridSpec(
        num_scalar_prefetch=1, grid=(half // tr,),
        in_specs=[pl.BlockSpec((1, tr, cols), lambda i, s_ref: (s_ref[0], i, 0)),
                  pl.BlockSpec((3, tr, cols), lambda i, s_ref: (0, i, 0))],
        out_specs=pl.BlockSpec((tr, cols), lambda i, s_ref: (i, 0)))
    return pl.pallas_call(body, grid_spec=grid_spec, name=name, out_shape=jax.ShapeDtypeStruct((half, cols), F32),
                          compiler_params=pltpu.CompilerParams(vmem_limit_bytes=VMEM_LIMIT))(chip, psum, got)


def _adam_math(w, g, m, v):
    m = ADAM_B1 * m + (1.0 - ADAM_B1) * g
    v = ADAM_B2 * v + (1.0 - ADAM_B2) * (g * g)
    m_hat = m / (1.0 - ADAM_B1 ** ADAM_STEP)
    v_hat = v / (1.0 - ADAM_B2 ** ADAM_STEP)
    return -ADAM_LR * (m_hat / (jnp.sqrt(v_hat) + ADAM_EPS) + ADAM_WD * w), m, v


def _adam(w, g, m, v, name):
    rows, cols = w.shape

    def body(w_ref, g_ref, m_ref, v_ref, d_ref, nm_ref, nv_ref):
        d_ref[...], nm_ref[...], nv_ref[...] = _adam_math(w_ref[...], g_ref[...], m_ref[...], v_ref[...])

    if rows <= 256 or rows % 256 == 0:
        tr = min(rows, 256)
        n_blocks, spec = rows // tr, pl.BlockSpec((tr, cols), lambda i: (i, 0))
    else:
        n_blocks, spec = cols // 256, pl.BlockSpec((rows, 256), lambda i: (0, i))
    return pl.pallas_call(body, grid=(n_blocks,), name=name, in_specs=[spec] * 4, out_specs=[spec] * 3,
                          out_shape=[jax.ShapeDtypeStruct(w.shape, F32)] * 3, compiler_params=_params(seq=False))(w, g, m, v)


def _adam_sum(w, parts, m, v, name):
    rows, cols = w.shape
    tr = rows if rows <= 256 else 256

    def body(w_ref, p_ref, m_ref, v_ref, g_ref, d_ref, nm_ref, nv_ref):
        g = p_ref[0].astype(F32)
        for j in range(1, N_DEV):
            g = g + p_ref[j].astype(F32)
        g_ref[...] = g
        d_ref[...], nm_ref[...], nv_ref[...] = _adam_math(w_ref[...], g, m_ref[...], v_ref[...])

    spec = pl.BlockSpec((tr, cols), lambda i: (i, 0))
    return pl.pallas_call(
        body, grid=(rows // tr,), name=name,
        in_specs=[spec, pl.BlockSpec((N_DEV, tr, cols), lambda i: (0, i, 0)), spec, spec], out_specs=[spec] * 4,
        out_shape=[jax.ShapeDtypeStruct(w.shape, F32)] * 4, compiler_params=_params(seq=False))(w, parts, m, v)


def _adam_small(parts, ws, ms, vs):
    n_w = len(SMALL_LAYOUT)

    def body(p_ref, *refs):
        w_refs, m_refs, v_refs = refs[0:n_w], refs[n_w:2 * n_w], refs[2 * n_w:3 * n_w]
        sum_ref = refs[3 * n_w]
        outs = refs[3 * n_w + 1:]
        g = p_ref[0]
        for k in range(1, N_DEV):
            g = g + p_ref[k]
        sum_ref[...] = g
        for a, (_, r0, rows) in enumerate(SMALL_LAYOUT):
            width = w_refs[a].shape[1]
            for j in range(rows):
                lo, hi = j * D_MODEL, min((j + 1) * D_MODEL, width)
                gj = sum_ref[r0 + j:r0 + j + 1, 0:hi - lo]
                d, nm, nv = _adam_math(w_refs[a][:, lo:hi], gj, m_refs[a][:, lo:hi], v_refs[a][:, lo:hi])
                for out, val in zip(outs[4 * a:4 * a + 4], (gj, d, nm, nv)):
                    out[:, lo:hi] = val

    shapes = [jax.ShapeDtypeStruct(parts.shape[1:], F32)]
    for wa in ws:
        shapes += [jax.ShapeDtypeStruct(wa.shape, F32)] * 4
    res = pl.pallas_call(body, name="adam_small", out_shape=shapes)(parts, *ws, *ms, *vs)
    return res[0], [res[1 + 4 * a:5 + 4 * a] for a in range(n_w)]


def kernel(x, p, ln_emb_g, ln_emb_b, w_in, ssm_conv_w, ssm_conv_b, dt_bias, a_log, d_skip, ssm_norm_g, b_glu, conf_conv_w, conf_conv_b, conf_ln_g, conf_ln_b, w_out, b_out, ln1_g, ln1_b, w_ple_gate, w_ple_proj, ln2_g, ln2_b, loss_target, m_ln_emb_g, m_ln_emb_b, m_w_in, m_ssm_conv_w, m_ssm_conv_b, m_dt_bias, m_a_log, m_d_skip, m_ssm_norm_g, m_b_glu, m_conf_conv_w, m_conf_conv_b, m_conf_ln_g, m_conf_ln_b, m_w_out, m_b_out, m_ln1_g, m_ln1_b, m_w_ple_gate, m_w_ple_proj, m_ln2_g, m_ln2_b, v_ln_emb_g, v_ln_emb_b, v_w_in, v_ssm_conv_w, v_ssm_conv_b, v_dt_bias, v_a_log, v_d_skip, v_ssm_norm_g, v_b_glu, v_conf_conv_w, v_conf_conv_b, v_conf_ln_g, v_conf_ln_b, v_w_out, v_b_out, v_ln1_g, v_ln1_b, v_w_ple_gate, v_w_ple_proj, v_ln2_g, v_ln2_b):
    order = ("ln_emb_g", "ln_emb_b", "w_in", "ssm_conv_w", "ssm_conv_b", "dt_bias", "a_log", "d_skip", "ssm_norm_g", "b_glu",
             "conf_conv_w", "conf_conv_b", "conf_ln_g", "conf_ln_b", "w_out", "b_out", "ln1_g", "ln1_b", "w_ple_gate",
             "w_ple_proj", "ln2_g", "ln2_b")
    w = dict(zip(order, (ln_emb_g, ln_emb_b, w_in, ssm_conv_w, ssm_conv_b, dt_bias, a_log, d_skip, ssm_norm_g, b_glu,
                         conf_conv_w, conf_conv_b, conf_ln_g, conf_ln_b, w_out, b_out, ln1_g, ln1_b, w_ple_gate, w_ple_proj,
                         ln2_g, ln2_b)))
    m = dict(zip(order, (m_ln_emb_g, m_ln_emb_b, m_w_in, m_ssm_conv_w, m_ssm_conv_b, m_dt_bias, m_a_log, m_d_skip,
                         m_ssm_norm_g, m_b_glu, m_conf_conv_w, m_conf_conv_b, m_conf_ln_g, m_conf_ln_b, m_w_out, m_b_out,
                         m_ln1_g, m_ln1_b, m_w_ple_gate, m_w_ple_proj, m_ln2_g, m_ln2_b)))
    v = dict(zip(order, (v_ln_emb_g, v_ln_emb_b, v_w_in, v_ssm_conv_w, v_ssm_conv_b, v_dt_bias, v_a_log, v_d_skip,
                         v_ssm_norm_g, v_b_glu, v_conf_conv_w, v_conf_conv_b, v_conf_ln_g, v_conf_ln_b, v_w_out, v_b_out,
                         v_ln1_g, v_ln1_b, v_w_ple_gate, v_w_ple_proj, v_ln2_g, v_ln2_b)))

    conv_f = jnp.concatenate([_rows_of(w["ssm_conv_w"], ROWS_CW4), _rows_of(w["conf_conv_w"], ROWS_CW31),
                              jnp.zeros((ROWS_CONV - ROWS_CW4 - ROWS_CW31, D_MODEL), F32)], axis=0)
    shards_b = [w[n][0].astype(BF16) for n, _, _ in BIG]
    w_in_all, conv_all = _gather_weights(shards_b[0], conv_f)
    w_r = _w_r_from_shards(w_in_all)
    cw4 = conv_all[:, 0:ROWS_CW4].reshape(N_CHIPS, -1)[:, :SSM_K * D_XBC // N_CHIPS]
    cw4 = cw4.reshape(N_CHIPS, SSM_K, D_XBC // N_CHIPS).transpose(1, 0, 2).reshape(SSM_K, D_XBC)
    cw31 = conv_all[:, ROWS_CW4:ROWS_CW4 + ROWS_CW31].reshape(N_CHIPS, -1)[:, :CONF_K * D_CONF // N_CHIPS]
    cw31 = cw31.reshape(N_CHIPS, CONF_K, D_CONF // N_CHIPS).transpose(1, 0, 2).reshape(CONF_K, D_CONF)

    small_names = [n for n, _, _ in SMALL_LAYOUT]
    sm = {n: w[n] for n in small_names}
    loss_part, dx, grads = _local_grads(x[0], p[0, 0], loss_target[0], w_r, shards_b[1:], cw4, cw31, sm)

    core = lax.axis_index("c").astype(jnp.int32).reshape(1)
    chip_i = 2 * lax.axis_index("x") + lax.axis_index("y")
    chip = chip_i.astype(jnp.int32).reshape(1)
    gbig = [grads["w_in"][None]]
    spack = _pack_small({n: grads[n] for n in small_names + [n for n, _, _, _ in CONV_LAYOUT]}, loss_part)
    *theirs, small_all = _exchange_cores(gbig, spack)
    psums = [_add_cores(core, g, t, "add_cores_" + n) for g, t, (n, _, _) in zip(gbig, theirs, BIG)]
    psums[0] = _shards_from_dw_r(psums[0][0])
    got = _exchange_chips(psums)
    tots = [_add_chips(chip, ps, gt, "add_chips_" + n) for ps, gt, (n, _, _) in zip(psums, got, BIG)]
    both = _share_halves(tots)

    out_g, out_d, out_m, out_v = {}, {}, {}, {}
    as_row = lambda a: a.reshape(1, -1)
    g_s, small_out = _adam_small(small_all, [as_row(w[n]) for n in small_names], [as_row(m[n]) for n in small_names],
                                 [as_row(v[n]) for n in small_names])
    loss = g_s[LOSS_ROW, 0]
    g_shard = {n: b.reshape(rows, cols) for b, (n, rows, cols) in zip(both, BIG)}
    for n, r0, r, shape in CONV_LAYOUT:
        whole = g_s[r0:r0 + r].reshape(-1)[:shape[0] * shape[1]].reshape(shape)
        g_shard[n] = lax.dynamic_slice_in_dim(whole, chip_i * (shape[1] // N_CHIPS), shape[1] // N_CHIPS, axis=1)
    for n, g in g_shard.items():
        if n == "w_in":
            tr_ = lambda a: jnp.swapaxes(a, 0, 1)
            d, nm, nv = [tr_(a) for a in _adam(tr_(w[n][0]), tr_(g), tr_(m[n][0]), tr_(v[n][0]), "adam_" + n)]
        else:
            d, nm, nv = _adam(w[n][0], g, m[n][0], v[n][0], "adam_" + n)
        out_g[n], out_d[n], out_m[n], out_v[n] = g[None], d[None], nm[None], nv[None]
    for n, _, _ in BIG[1:]:
        out_g[n], out_d[n], out_m[n], out_v[n] = [a[None] for a in _adam_sum(w[n][0], grads[n], m[n][0], v[n][0], "adam_" + n)]
    for n, four in zip(small_names, small_out):
        out_g[n], out_d[n], out_m[n], out_v[n] = [a.reshape(w[n].shape) for a in four]
    return (loss, dx[None], *[out_g[n] for n in order], *[out_d[n] for n in order], *[out_m[n] for n in order],
            *[out_v[n] for n in order])
```

```python
import functools

import jax
import jax.numpy as jnp
from jax import lax
from jax.experimental import pallas as pl
from jax.experimental.pallas import tpu as pltpu

F32 = jnp.float32
BF16 = jnp.bfloat16

D_MODEL = 1024
D_PLE = 256
D_SSM = 1024
D_CONF = 1024
N_HEADS = 16
HEAD_DIM = 64
N_GROUPS = 2
N_STATE = 128
CHUNK = 128
SSM_K = 4
CONF_K = 31
D_XBC = D_SSM + 2 * N_GROUPS * N_STATE
D_IN = 5648
R_XBC, R_Z, R_GLU, R_CG, R_DT, D_INR = 0, 1536, 2560, 4608, 5632, 5760
LN_EPS = 1e-5
RMS_EPS = 1e-5
ALPHA = 2.0 ** 0.25
ADAM_LR, ADAM_B1, ADAM_B2, ADAM_EPS, ADAM_WD, ADAM_STEP = 0.001, 0.9, 0.999, 1e-08, 0.01, 10
NEG_BIG = -1e30
LANES = 128
SUBLANES = 8
VMEM_LIMIT = 56 * 1024 * 1024
HIGHEST = lax.Precision.HIGHEST
NT_DIMS = (((1,), (1,)), ((), ()))
TN_DIMS = (((0,), (0,)), ((), ()))


def _sig(v):
    return jax.nn.sigmoid(v)


def _dsilu(v, s):
    return s * (1.0 + v * (1.0 - s))


def _ln_stats(v):
    mu = jnp.mean(v, axis=-1, keepdims=True)
    c = v - mu
    var = jnp.mean(c * c, axis=-1, keepdims=True)
    rstd = lax.rsqrt(var + LN_EPS)
    return c * rstd, rstd


def _ln_bwd(dxhat, xhat, rstd):
    m1 = jnp.mean(dxhat, axis=-1, keepdims=True)
    m2 = jnp.mean(dxhat * xhat, axis=-1, keepdims=True)
    return rstd * (dxhat - m1 - xhat * m2)


def _softplus(v):
    return jnp.maximum(v, 0.0) + jnp.log1p(jnp.exp(-jnp.abs(v)))


def _colsum(v):
    return jnp.sum(v, axis=0, keepdims=True)


def _dot(a, b):
    return jnp.dot(a, b, preferred_element_type=F32)


def _dot_nt(a, b):
    return lax.dot_general(a, b, NT_DIMS, preferred_element_type=F32)


def _tile(tl, c, rev_of=None):
    if rev_of is None:
        return pl.BlockSpec((tl, c), lambda i: (i, 0))
    return pl.BlockSpec((tl, c), lambda i: (rev_of - 1 - i, 0))


def _full(shape, single=False):
    nd = len(shape)
    if single:
        return pl.BlockSpec(shape, lambda i: (0,) * nd, pipeline_mode=pl.Buffered(1))
    return pl.BlockSpec(shape, lambda i: (0,) * nd)


def _params(seq=True):
    return pltpu.CompilerParams(dimension_semantics=("arbitrary",) if seq else ("parallel",), vmem_limit_bytes=VMEM_LIMIT)


def _ln_inproj(x, g, b, w_r, later, tl):
    n_tok = x.shape[0]
    n_t = n_tok // tl
    n_l = len(later)
    plan, own_plan = _gather_plans([(rows, GATHER_CHUNKS[a]) for a, (_, rows, _) in enumerate(BIG)][1:])

    def body(x_ref, g_ref, b_ref, w_ref, *rest):
        part_refs = rest[0:n_l]
        h0b_ref, xbc_ref, z_ref, glu_ref, cg_ref, dtr_ref = rest[n_l:n_l + 6]
        all_refs, sems = rest[n_l + 6:2 * n_l + 6], rest[2 * n_l + 6:]
        i = pl.program_id(0)

        @pl.when(i == 0)
        def _():
            _gather_stage(0, part_refs, all_refs, plan, own_plan, sems)

        xhat, _ = _ln_stats(x_ref[...])
        hb = (xhat * g_ref[...] + b_ref[...]).astype(BF16)
        h0b_ref[...] = hb
        xbc_ref[...] = _dot(hb, w_ref[:, R_XBC:R_Z])
        z_ref[...] = _dot(hb, w_ref[:, R_Z:R_GLU])
        glu_ref[...] = _dot(hb, w_ref[:, R_GLU:R_CG])
        cg_ref[...] = _dot(hb, w_ref[:, R_CG:R_DT])
        dtr_ref[...] = _dot(hb, w_ref[:, R_DT:D_INR])

        @pl.when(i == (3 * n_t) // 4)
        def _():
            _gather_stage(1, part_refs, all_refs, plan, own_plan, sems)

        @pl.when(i == n_t - 1)
        def _():
            _gather_stage(2, part_refs, all_refs, plan, own_plan, sems)

    widths = (D_MODEL, D_XBC, D_SSM, 2 * D_CONF, D_CONF, LANES)
    dtypes = (BF16, F32, F32, F32, F32, F32)
    return pl.pallas_call(
        body, grid=(n_t,), name="ln_inproj",
        in_specs=[_tile(tl, D_MODEL), _full((1, D_MODEL)), _full((1, D_MODEL)), _full((D_MODEL, D_INR), single=True)]
        + [IN_VMEM] * n_l,
        out_specs=[_tile(tl, w) for w in widths] + [ANY] * n_l,
        out_shape=[jax.ShapeDtypeStruct((n_tok, w), dt) for w, dt in zip(widths, dtypes)]
        + [jax.ShapeDtypeStruct((N_CHIPS,) + a.shape, a.dtype) for a in later],
        scratch_shapes=_gather_sems(plan, own_plan),
        compiler_params=_params(),
    )(x, g, b, w_r, *later)


def _chunk_common(adt_c):
    row = lax.broadcasted_iota(jnp.int32, (CHUNK, CHUNK), 0)
    col = lax.broadcasted_iota(jnp.int32, (CHUNK, CHUNK), 1)
    tril = row >= col
    acs = jnp.dot(tril.astype(F32), adt_c, precision=HIGHEST, preferred_element_type=F32)
    last = acs[CHUNK - 1:CHUNK, :]
    return dict(row=row, col=col, tril=tril, lo=col < HEAD_DIM, acs=acs, acs_t=acs.T, e=jnp.exp(acs),
                dec=jnp.exp(last - acs), cd=jnp.exp(last))


def _decay_mask(cm, h):
    return jnp.exp(jnp.where(cm["tril"], cm["acs"][:, h:h + 1] - cm["acs_t"][h:h + 1, :], NEG_BIG))


def _head_lane_matrix():
    return (jnp.arange(D_SSM)[None, :] // HEAD_DIM == jnp.arange(LANES)[:, None]).astype(BF16)


def _per_head_lanes(v, exp_ref):
    hi = v.astype(BF16)
    lo = (v - hi.astype(F32)).astype(BF16)
    return _dot(hi, exp_ref[...]) + _dot(lo, exp_ref[...])


SSM_BWD_SHIFTS = (1, 2, 3)


def _ssd_fwd(xbc_in, z, dtr, cw, cb, dtb, alog, dsk, gnorm, tl):
    n_tok = xbc_in.shape[0]
    nq = tl // CHUNK

    def body(xin_ref, z_ref, dtr_ref, cw_ref, cb_ref, dtb_ref, alog_ref, dsk_ref, gn_ref, exp_ref,
             pre_ref, y_ref, yssm_ref, hprev_ref, buf, hst):
        @pl.when(pl.program_id(0) == 0)
        def _():
            buf[0:SUBLANES, :] = jnp.zeros((SUBLANES, D_XBC), F32)
            hst[...] = jnp.zeros_like(hst)

        buf[SUBLANES:SUBLANES + tl, :] = xin_ref[...]
        pre = cb_ref[...] + jnp.zeros((tl, D_XBC), F32)
        for k in range(SSM_K):
            off = SUBLANES - (SSM_K - 1) + k
            pre = pre + buf[off:off + tl, :] * cw_ref[k:k + 1, :]
        buf[0:SUBLANES, :] = buf[tl:tl + SUBLANES, :]
        pre_ref[...] = pre
        xbc = pre * _sig(pre)
        dt = _softplus(dtr_ref[...] + dtb_ref[...])
        a = -jnp.exp(alog_ref[...])
        adt = dt * a
        for q in range(nq):
            r0 = q * CHUNK
            cm = _chunk_common(adt[r0:r0 + CHUNK, :])
            dt_x = _per_head_lanes(dt[r0:r0 + CHUNK, :], exp_ref)
            e_x = _per_head_lanes(cm["e"], exp_ref)
            dec_x = _per_head_lanes(cm["dec"], exp_ref)
            for g in range(N_GROUPS):
                bg = xbc[r0:r0 + CHUNK, D_SSM + g * N_STATE:D_SSM + (g + 1) * N_STATE].astype(BF16)
                cg_ = xbc[r0:r0 + CHUNK, D_SSM + (N_GROUPS + g) * N_STATE:D_SSM + (N_GROUPS + g + 1) * N_STATE].astype(BF16)
                gm = _dot_nt(cg_, bg)
                for k in range(N_HEADS // N_GROUPS // 2):
                    ha = (N_HEADS // N_GROUPS) * g + 2 * k
                    c0 = ha * HEAD_DIM
                    xh2 = xbc[r0:r0 + CHUNK, c0:c0 + LANES]
                    x2 = xh2 * dt_x[:, c0:c0 + LANES]
                    x2b = x2.astype(BF16)
                    ya = _dot((gm * _decay_mask(cm, ha)).astype(BF16), x2b)
                    yb = _dot((gm * _decay_mask(cm, ha + 1)).astype(BF16), x2b)
                    h2 = hst[c0:c0 + LANES, :]
                    hprev_ref[q, c0:c0 + LANES, :] = h2
                    z2 = _dot_nt(cg_, h2.astype(BF16))
                    y2 = jnp.where(cm["lo"], ya, yb) + z2 * e_x[:, c0:c0 + LANES] + dsk_ref[:, c0:c0 + LANES] * xh2
                    y_ref[r0:r0 + CHUNK, c0:c0 + LANES] = y2
                    s2 = _dot((x2 * dec_x[:, c0:c0 + LANES]).T.astype(BF16), bg)
                    cd2 = jnp.where(cm["row"] < HEAD_DIM, cm["cd"][:, ha:ha + 1], cm["cd"][:, ha + 1:ha + 2])
                    hst[c0:c0 + LANES, :] = cd2 * h2 + s2
        yv = y_ref[...]
        zv = z_ref[...]
        yz = yv * (zv * _sig(zv))
        gw = D_SSM // N_GROUPS
        for g in range(N_GROUPS):
            seg = yz[:, g * gw:(g + 1) * gw]
            r = lax.rsqrt(jnp.mean(seg * seg, axis=-1, keepdims=True) + RMS_EPS)
            yssm_ref[:, g * gw:(g + 1) * gw] = (seg * r * gn_ref[:, g * gw:(g + 1) * gw]).astype(BF16)

    return pl.pallas_call(
        body, grid=(n_tok // tl,), name="ssd_fwd",
        in_specs=[_tile(tl, D_XBC), _tile(tl, D_SSM), _tile(tl, LANES), _full((SUBLANES, D_XBC)), _full((1, D_XBC)),
                  _full((1, LANES)), _full((1, LANES)), _full((1, D_SSM)), _full((1, D_SSM)), _full((LANES, D_SSM))],
        out_specs=[_tile(tl, D_XBC), _tile(tl, D_SSM), _tile(tl, D_SSM),
                   pl.BlockSpec((nq, D_SSM, N_STATE), lambda i: (i, 0, 0))],
        out_shape=[jax.ShapeDtypeStruct((n_tok, D_XBC), F32), jax.ShapeDtypeStruct((n_tok, D_SSM), F32),
                   jax.ShapeDtypeStruct((n_tok, D_SSM), BF16), jax.ShapeDtypeStruct((n_tok // CHUNK, D_SSM, N_STATE), F32)],
        scratch_shapes=[pltpu.VMEM((tl + SUBLANES, D_XBC), F32), pltpu.VMEM((D_SSM, N_STATE), F32)],
        compiler_params=_params(),
    )(xbc_in, z, dtr, cw, cb, dtb, alog, dsk, gnorm, _head_lane_matrix())


def _ssd_bwd(dys, y, z, pre, xbc_in, dtr, hprev, cw, dtb, alog, dsk, gnorm, specs, grads_b, tl):
    n_tok = y.shape[0]
    n_t = n_tok // tl
    nq = tl // CHUNK
    n_g, n_r = len(grads_b), len(specs)

    def body(dys_ref, y_ref, z_ref, pre_ref, xin_ref, dtr_ref, hprev_ref, cw_ref, dtb_ref, alog_ref, dsk_ref, gn_ref,
             exp_ref, redx_ref, redq_ref, *rest):
        g_refs, rest = rest[0:n_g], rest[n_g:]
        dxin_ref, dz_ref, ddtr_ref, dcw_ref, dcb_ref, dgn_ref, ddsk_ref, da_ref, ddtb_ref = rest[0:9]
        recvs, rest = rest[9:9 + n_r], rest[9 + n_r:]
        dxs, dh, cs_s, dskc, shifts, dwp, sums_q, sums_s = rest[0:8]
        sems = rest[8:]
        i = pl.program_id(0)

        @pl.when(i == 0)
        def _():
            _direct_stage(0, specs, g_refs, recvs, sems)
            dcw_ref[...] = jnp.zeros_like(dcw_ref)
            dwp[...] = jnp.zeros_like(dwp)
            dcb_ref[...] = jnp.zeros_like(dcb_ref)
            dgn_ref[...] = jnp.zeros_like(dgn_ref)
            da_ref[...] = jnp.zeros_like(da_ref)
            ddtb_ref[...] = jnp.zeros_like(ddtb_ref)
            dskc[...] = jnp.zeros_like(dskc)
            dh[...] = jnp.zeros_like(dh)
            dxs[tl:tl + SUBLANES, :] = jnp.zeros((SUBLANES, D_XBC), F32)

        yv = y_ref[...]
        zv = z_ref[...]
        dysv = dys_ref[...]
        sz = _sig(zv)
        silz = zv * sz
        yz = yv * silz
        gw = D_SSM // N_GROUPS
        dyz_parts = []
        for g in range(N_GROUPS):
            sl = slice(g * gw, (g + 1) * gw)
            seg = yz[:, sl]
            r = lax.rsqrt(jnp.mean(seg * seg, axis=-1, keepdims=True) + RMS_EPS)
            yzn = seg * r
            dgn_ref[:, sl] += _colsum(dysv[:, sl] * yzn)
            dyzn = dysv[:, sl] * gn_ref[:, sl]
            dyz_parts.append(r * (dyzn - yzn * jnp.mean(dyzn * yzn, axis=-1, keepdims=True)))
        dyz = jnp.concatenate(dyz_parts, axis=1)
        dy = dyz * silz
        dz_ref[...] = (dyz * yv * _dsilu(zv, sz)).astype(BF16)

        prev = pre_ref[...]
        sp = _sig(prev)
        xbc = prev * sp
        dskc[...] += _colsum(dy * xbc[:, 0:D_SSM])
        dt_in = dtr_ref[...] + dtb_ref[...]
        dt = _softplus(dt_in)
        dsp = _sig(dt_in)
        a = -jnp.exp(alog_ref[...])
        adt = dt * a
        for q in reversed(range(nq)):
            r0 = q * CHUNK
            cm = _chunk_common(adt[r0:r0 + CHUNK, :])
            row, col, lo = cm["row"], cm["col"], cm["lo"]
            triu = (col >= row).astype(F32)
            dt_c = dt[r0:r0 + CHUNK, :]
            dt_x = _per_head_lanes(dt_c, exp_ref)
            e_x = _per_head_lanes(cm["e"], exp_ref)
            dec_x = _per_head_lanes(cm["dec"], exp_ref)
            dcd_row = jnp.zeros((1, LANES), F32)
            for g in range(N_GROUPS):
                bcol = D_SSM + g * N_STATE
                ccol = D_SSM + (N_GROUPS + g) * N_STATE
                bg = xbc[r0:r0 + CHUNK, bcol:bcol + N_STATE].astype(BF16)
                cg_ = xbc[r0:r0 + CHUNK, ccol:ccol + N_STATE].astype(BF16)
                gm = _dot_nt(cg_, bg)
                dgm = jnp.zeros((CHUNK, CHUNK), F32)
                dbg = jnp.zeros((CHUNK, N_STATE), F32)
                dcg = jnp.zeros((CHUNK, N_STATE), F32)
                for k in range(N_HEADS // N_GROUPS // 2):
                    ha = (N_HEADS // N_GROUPS) * g + 2 * k
                    hb = ha + 1
                    c0 = ha * HEAD_DIM
                    xh2 = xbc[r0:r0 + CHUNK, c0:c0 + LANES]
                    dt2 = dt_x[:, c0:c0 + LANES]
                    x2 = xh2 * dt2
                    x2b = x2.astype(BF16)
                    la = _decay_mask(cm, ha)
                    lb = _decay_mask(cm, hb)
                    ma = gm * la
                    mb = gm * lb
                    dy2 = dy[r0:r0 + CHUNK, c0:c0 + LANES]
                    dy2b = dy2.astype(BF16)
                    dma = _dot_nt(jnp.where(lo, dy2, 0.0).astype(BF16), x2b)
                    dmb = _dot_nt(jnp.where(lo, 0.0, dy2).astype(BF16), x2b)
                    dx2 = jnp.where(lo, _dot(ma.T.astype(BF16), dy2b), _dot(mb.T.astype(BF16), dy2b))
                    dgm = dgm + dma * la + dmb * lb
                    sums_q[:, ha * CHUNK:(ha + 1) * CHUNK] = (dma * ma).astype(BF16)
                    sums_q[:, hb * CHUNK:(hb + 1) * CHUNK] = (dmb * mb).astype(BF16)
                    h2 = hprev_ref[q, c0:c0 + LANES, :]
                    h2b = h2.astype(BF16)
                    dz2 = dy2 * e_x[:, c0:c0 + LANES]
                    dcg = dcg + _dot(dz2.astype(BF16), h2b)
                    sums_s[0, :, c0:c0 + LANES] = (dz2 * _dot_nt(cg_, h2b)).astype(BF16)
                    dhn = dh[c0:c0 + LANES, :]
                    dhnb = dhn.astype(BF16)
                    cd_a = cm["cd"][:, ha:ha + 1]
                    cd_b = cm["cd"][:, hb:hb + 1]
                    top = row < HEAD_DIM
                    hh = dhn * h2
                    dcd_a = jnp.sum(_colsum(jnp.where(top, hh, 0.0)), axis=1, keepdims=True)
                    dcd_b = jnp.sum(_colsum(jnp.where(top, 0.0, hh)), axis=1, keepdims=True)
                    dcd_row = dcd_row + jnp.where(col[0:1, :] == ha, dcd_a * cd_a, 0.0) + jnp.where(col[0:1, :] == hb, dcd_b * cd_b, 0.0)
                    dh[c0:c0 + LANES, :] = jnp.where(top, cd_a, cd_b) * dhn + _dot(dz2.T.astype(BF16), cg_)
                    w2 = _dot_nt(bg, dhnb)
                    dec2 = dec_x[:, c0:c0 + LANES]
                    dx2 = dx2 + dec2 * w2
                    sums_s[1, :, c0:c0 + LANES] = (x2 * w2).astype(BF16)
                    dbg = dbg + _dot((x2 * dec2).astype(BF16), dhnb)
                    sums_s[2, :, c0:c0 + LANES] = (dx2 * xh2).astype(BF16)
                    dxs[r0:r0 + CHUNK, c0:c0 + LANES] = dx2 * dt2 + dsk_ref[:, c0:c0 + LANES] * dy2
                dgmb = dgm.astype(BF16)
                dxs[r0:r0 + CHUNK, bcol:bcol + N_STATE] = dbg + _dot(dgm.T.astype(BF16), cg_)
                dxs[r0:r0 + CHUNK, ccol:ccol + N_STATE] = dcg + _dot(dgmb, bg)
            q_all = sums_q[...]
            q_cols = _dot(jnp.ones((SUBLANES, CHUNK), BF16), q_all)
            cs_s[...] = jnp.zeros_like(cs_s)
            for h in range(N_HEADS):
                cs_s[h:h + 1, :] = q_cols[0:1, h * CHUNK:(h + 1) * CHUNK]
            de = _dot(sums_s[0], redx_ref[...])
            dd = _dot(sums_s[1], redx_ref[...]) * cm["dec"]
            ddtx = _dot(sums_s[2], redx_ref[...])
            is_last = row == CHUNK - 1
            dacs = _dot(q_all, redq_ref[...]) - cs_s[...].T + de - dd + jnp.where(is_last, dcd_row + _colsum(dd), 0.0)
            dadt = jnp.dot(triu, dacs, precision=HIGHEST, preferred_element_type=F32)
            da_ref[...] += _colsum(dadt * dt_c)
            ddtr_c = (dadt * a + ddtx) * dsp[r0:r0 + CHUNK, :]
            ddtr_ref[r0:r0 + CHUNK, :] = ddtr_c.astype(BF16)
            ddtb_ref[...] += _colsum(ddtr_c)

        dpre = dxs[0:tl, :] * _dsilu(prev, sp)
        dxs[0:tl, :] = dpre
        dcb_ref[...] += _colsum(dpre)
        _shift_copies(dxs, shifts, tl, SSM_BWD_SHIFTS)

        def strip(rb, carry):
            i0 = pl.multiple_of(rb * CONV_RS, CONV_RS)
            for c0 in range(0, D_XBC, CONV_CS):
                xin_s = xin_ref[pl.ds(i0, CONV_RS), c0:c0 + CONV_CS]
                acc = jnp.zeros((CONV_RS, CONV_CS), F32)
                for k in range(SSM_K):
                    sh = _tap_rows(dxs, shifts, SSM_K - 1 - k, i0, c0, SSM_BWD_SHIFTS)
                    acc = acc + sh * cw_ref[k:k + 1, c0:c0 + CONV_CS]
                    t = xin_s * sh
                    dwp[k * SUBLANES:(k + 1) * SUBLANES, c0:c0 + CONV_CS] += _fold_rows(t)
                dxin_ref[pl.ds(i0, CONV_RS), c0:c0 + CONV_CS] = acc.astype(BF16)
            return carry

        lax.fori_loop(0, tl // CONV_RS, strip, 0)
        dxs[tl:tl + SUBLANES, :] = dxs[0:SUBLANES, :]

        @pl.when(i == n_t - 1)
        def _():
            for k in range(SSM_K):
                dcw_ref[k:k + 1, :] = _colsum(dwp[k * SUBLANES:(k + 1) * SUBLANES, :])
            da_ref[...] = da_ref[...] * a
            sel = (lax.broadcasted_iota(jnp.int32, (D_SSM, LANES), 0) // HEAD_DIM
                   == lax.broadcasted_iota(jnp.int32, (D_SSM, LANES), 1)).astype(F32)
            rows = jnp.broadcast_to(dskc[...], (SUBLANES, D_SSM))
            ddsk_ref[...] = jnp.dot(rows, sel, precision=HIGHEST, preferred_element_type=F32)[0:1, :]
            _direct_stage(1, specs, g_refs, recvs, sems)

    head_lanes = _head_lane_matrix()
    per_head = (jnp.arange(N_HEADS * CHUNK)[:, None] // CHUNK == jnp.arange(LANES)[None, :]).astype(BF16)
    rev = functools.partial(_tile, tl, rev_of=n_t)
    return pl.pallas_call(
        body, grid=(n_t,), name="ssd_bwd",
        in_specs=[rev(D_SSM), rev(D_SSM), rev(D_SSM), rev(D_XBC), rev(D_XBC), rev(LANES),
                  pl.BlockSpec((nq, D_SSM, N_STATE), lambda i: (n_t - 1 - i, 0, 0)),
                  _full((SUBLANES, D_XBC)), _full((1, LANES)), _full((1, LANES)), _full((1, D_SSM)), _full((1, D_SSM)),
                  _full((LANES, D_SSM)), _full((D_SSM, LANES)), _full((N_HEADS * CHUNK, LANES))] + [IN_VMEM] * n_g,
        out_specs=[rev(D_XBC), rev(D_SSM), rev(LANES), _full((SUBLANES, D_XBC)), _full((1, D_XBC)), _full((1, D_SSM)),
                   _full((1, LANES)), _full((1, LANES)), _full((1, LANES))] + [ANY] * n_r,
        out_shape=[jax.ShapeDtypeStruct((n_tok, D_XBC), BF16), jax.ShapeDtypeStruct((n_tok, D_SSM), BF16),
                   jax.ShapeDtypeStruct((n_tok, LANES), BF16), jax.ShapeDtypeStruct((SUBLANES, D_XBC), F32),
                   jax.ShapeDtypeStruct((1, D_XBC), F32), jax.ShapeDtypeStruct((1, D_SSM), F32),
                   jax.ShapeDtypeStruct((1, LANES), F32), jax.ShapeDtypeStruct((1, LANES), F32),
                   jax.ShapeDtypeStruct((1, LANES), F32)]
        + [jax.ShapeDtypeStruct((N_DEV, rows, cols), BF16) for rows, cols, _, _ in specs],
        scratch_shapes=[pltpu.VMEM((tl + SUBLANES, D_XBC), F32), pltpu.VMEM((D_SSM, N_STATE), F32),
                        pltpu.VMEM((CHUNK, LANES), F32), pltpu.VMEM((1, D_SSM), F32),
                        pltpu.VMEM((len(SSM_BWD_SHIFTS), tl, D_XBC), F32), pltpu.VMEM((SSM_K * SUBLANES, D_XBC), F32),
                        pltpu.VMEM((CHUNK, N_HEADS * CHUNK), BF16), pltpu.VMEM((3, CHUNK, D_SSM), BF16)]
        + _direct_sems(specs),
        compiler_params=_params(),
    )(dys, y, z, pre, xbc_in, dtr, hprev, cw, dtb, alog, dsk, gnorm, head_lanes, head_lanes.T, per_head, *grads_b)


CONF_HALO = 32
CONV_RS = 32
CONV_CS = 256


ALL_SHIFTS = tuple(range(1, SUBLANES))


def _shift_copies(buf, shifts, n_rows, residues=ALL_SHIFTS):
    for j, r in enumerate(residues):
        shifts[j, 0:n_rows, :] = buf[r:r + n_rows, :]


def _fold_rows(t):
    part = t[0:SUBLANES]
    for j in range(1, t.shape[0] // SUBLANES):
        part = part + t[j * SUBLANES:(j + 1) * SUBLANES]
    return part


def _tap_rows(buf, shifts, off, i0, c0, residues=ALL_SHIFTS):
    q, r = divmod(off, SUBLANES)
    rows = pl.ds(pl.multiple_of(i0 + SUBLANES * q, SUBLANES), CONV_RS)
    if r == 0:
        return buf[rows, c0:c0 + CONV_CS]
    return shifts[residues.index(r), rows, c0:c0 + CONV_CS]


def _conf_fwd(glu, cgate, bglu, cw, cb, lg, lb, tl):
    n_tok = glu.shape[0]

    def body(glu_ref, cg_ref, bglu_ref, cw_ref, cb_ref, lg_ref, lb_ref, u0_ref, u1_ref, yc_ref, buf, shifts):
        @pl.when(pl.program_id(0) == 0)
        def _():
            buf[0:CONF_HALO, :] = jnp.zeros((CONF_HALO, D_CONF), F32)

        gl = glu_ref[...] + bglu_ref[...]
        u0 = gl[:, 0:D_CONF] * _sig(gl[:, D_CONF:2 * D_CONF])
        u0_ref[...] = u0
        buf[CONF_HALO:CONF_HALO + tl, :] = u0
        _shift_copies(buf, shifts, tl + CONF_HALO - SUBLANES)

        def strip(rb, carry):
            i0 = pl.multiple_of(rb * CONV_RS, CONV_RS)
            for c0 in range(0, D_CONF, CONV_CS):
                acc = jnp.broadcast_to(cb_ref[:, c0:c0 + CONV_CS], (CONV_RS, CONV_CS))
                for k in range(CONF_K):
                    acc = acc + _tap_rows(buf, shifts, CONF_HALO - (CONF_K - 1) + k, i0, c0) * cw_ref[k:k + 1, c0:c0 + CONV_CS]
                u1_ref[pl.ds(i0, CONV_RS), c0:c0 + CONV_CS] = acc
            return carry

        lax.fori_loop(0, tl // CONV_RS, strip, 0)
        buf[0:CONF_HALO, :] = buf[tl:tl + CONF_HALO, :]
        xhat, _ = _ln_stats(u1_ref[...])
        n = xhat * lg_ref[...] + lb_ref[...]
        cgv = cg_ref[...]
        yc_ref[...] = (n * _sig(n) * (cgv * _sig(cgv))).astype(BF16)

    return pl.pallas_call(
        body, grid=(n_tok // tl,), name="conf_fwd",
        in_specs=[_tile(tl, 2 * D_CONF), _tile(tl, D_CONF), _full((1, 2 * D_CONF)), _full((CONF_HALO, D_CONF)),
                  _full((1, D_CONF)), _full((1, D_CONF)), _full((1, D_CONF))],
        out_specs=[_tile(tl, D_CONF)] * 3,
        out_shape=[jax.ShapeDtypeStruct((n_tok, D_CONF), F32), jax.ShapeDtypeStruct((n_tok, D_CONF), F32),
                   jax.ShapeDtypeStruct((n_tok, D_CONF), BF16)],
        scratch_shapes=[pltpu.VMEM((tl + CONF_HALO, D_CONF), F32),
                        pltpu.VMEM((SUBLANES - 1, tl + CONF_HALO - SUBLANES, D_CONF), F32)],
        compiler_params=_params(),
    )(glu, cgate, bglu, cw, cb, lg, lb)


def _conf_bwd(dyc, dyc_block, u0, u1, glu, cgate, bglu, cw, lg, lb, specs, grads_b, tl):
    n_tok = glu.shape[0]
    n_t = n_tok // tl
    n_g, n_r = len(grads_b), len(specs)

    def body(dyc_ref, u0_ref, u1_ref, glu_ref, cg_ref, bglu_ref, cw_ref, lg_ref, lb_ref, *rest):
        g_refs, rest = rest[0:n_g], rest[n_g:]
        dglu_ref, dcg_ref, dcw_ref, dbglu_ref, small_ref = rest[0:5]
        recvs, rest = rest[5:5 + n_r], rest[5 + n_r:]
        buf, shifts, du0_s, dwp = rest[0:4]
        sems = rest[4:]

        @pl.when(pl.program_id(0) == 0)
        def _():
            _direct_stage(0, specs, g_refs, recvs, sems)
            dwp[...] = jnp.zeros_like(dwp)
            dbglu_ref[...] = jnp.zeros_like(dbglu_ref)
            small_ref[...] = jnp.zeros_like(small_ref)
            buf[tl:tl + CONF_HALO, :] = jnp.zeros((CONF_HALO, D_CONF), F32)

        xhat, rstd = _ln_stats(u1_ref[...])
        n = xhat * lg_ref[...] + lb_ref[...]
        sn = _sig(n)
        cgv = cg_ref[...]
        scg = _sig(cgv)
        dycv = dyc_ref[...]
        dcg_ref[...] = (dycv * (n * sn) * _dsilu(cgv, scg)).astype(BF16)
        dn = dycv * (cgv * scg) * _dsilu(n, sn)
        small_ref[0:1, :] += _colsum(dn * xhat)
        small_ref[1:2, :] += _colsum(dn)
        du1 = _ln_bwd(dn * lg_ref[...], xhat, rstd)
        small_ref[2:3, :] += _colsum(du1)
        buf[0:tl, :] = du1
        _shift_copies(buf, shifts, tl + CONF_HALO - SUBLANES)

        def strip(rb, carry):
            i0 = pl.multiple_of(rb * CONV_RS, CONV_RS)
            for c0 in range(0, D_CONF, CONV_CS):
                u0s = u0_ref[pl.ds(i0, CONV_RS), c0:c0 + CONV_CS]
                acc = jnp.zeros((CONV_RS, CONV_CS), F32)
                for k in range(CONF_K):
                    sh = _tap_rows(buf, shifts, CONF_K - 1 - k, i0, c0)
                    acc = acc + sh * cw_ref[k:k + 1, c0:c0 + CONV_CS]
                    t = u0s * sh
                    dwp[k * SUBLANES:(k + 1) * SUBLANES, c0:c0 + CONV_CS] += _fold_rows(t)
                du0_s[pl.ds(i0, CONV_RS), c0:c0 + CONV_CS] = acc
            return carry

        lax.fori_loop(0, tl // CONV_RS, strip, 0)
        du0 = du0_s[...]
        buf[tl:tl + CONF_HALO, :] = buf[0:CONF_HALO, :]
        gl = glu_ref[...] + bglu_ref[...]
        sg = _sig(gl[:, D_CONF:2 * D_CONF])
        dgv = du0 * sg
        dgg = du0 * gl[:, 0:D_CONF] * sg * (1.0 - sg)
        dglu_ref[:, 0:D_CONF] = dgv.astype(BF16)
        dglu_ref[:, D_CONF:2 * D_CONF] = dgg.astype(BF16)
        dbglu_ref[:, 0:D_CONF] += _colsum(dgv)
        dbglu_ref[:, D_CONF:2 * D_CONF] += _colsum(dgg)

        @pl.when(pl.program_id(0) == n_t - 1)
        def _():
            for k in range(CONF_HALO):
                dcw_ref[k:k + 1, :] = _colsum(dwp[k * SUBLANES:(k + 1) * SUBLANES, :])
            _direct_stage(1, specs, g_refs, recvs, sems)

    rev = functools.partial(_tile, tl, rev_of=n_t)
    return pl.pallas_call(
        body, grid=(n_t,), name="conf_bwd",
        in_specs=[pl.BlockSpec((tl, D_CONF), lambda i: (n_t - 1 - i, dyc_block)),
                  rev(D_CONF), rev(D_CONF), rev(2 * D_CONF), rev(D_CONF), _full((1, 2 * D_CONF)),
                  _full((CONF_HALO, D_CONF)), _full((1, D_CONF)), _full((1, D_CONF))] + [IN_VMEM] * n_g,
        out_specs=[rev(2 * D_CONF), rev(D_CONF), _full((CONF_HALO, D_CONF)), _full((1, 2 * D_CONF)), _full((SUBLANES, D_CONF))]
        + [ANY] * n_r,
        out_shape=[jax.ShapeDtypeStruct((n_tok, 2 * D_CONF), BF16), jax.ShapeDtypeStruct((n_tok, D_CONF), BF16),
                   jax.ShapeDtypeStruct((CONF_HALO, D_CONF), F32), jax.ShapeDtypeStruct((1, 2 * D_CONF), F32),
                   jax.ShapeDtypeStruct((SUBLANES, D_CONF), F32)]
        + [jax.ShapeDtypeStruct((N_DEV, rows, cols), BF16) for rows, cols, _, _ in specs],
        scratch_shapes=[pltpu.VMEM((tl + CONF_HALO, D_CONF), F32),
                        pltpu.VMEM((SUBLANES - 1, tl + CONF_HALO - SUBLANES, D_CONF), F32),
                        pltpu.VMEM((tl, D_CONF), F32), pltpu.VMEM((CONF_HALO * SUBLANES, D_CONF), F32)] + _direct_sems(specs),
        compiler_params=_params(),
    )(dyc, u0, u1, glu, cgate, bglu, cw, lg, lb, *grads_b)


def _tail(x, yssm, yconf, p, tgt, vec, w_out, wpg, wpp, tl):
    n_tok = x.shape[0]

    def body(x_ref, ys_ref, yc_ref, p_ref, t_ref, vec_ref, wo_ref, wg_ref, wp_ref,
             dmix_ref, dr1_ref, dr1b_ref, h1b_ref, dgb_ref, dpb_ref, small_ref, loss_ref):
        @pl.when(pl.program_id(0) == 0)
        def _():
            small_ref[...] = jnp.zeros_like(small_ref)
            loss_ref[...] = jnp.zeros_like(loss_ref)

        xh0, _ = _ln_stats(x_ref[...])
        h0 = xh0 * vec_ref[0:1, :] + vec_ref[1:2, :]
        out = _dot(ys_ref[...], wo_ref[0:D_SSM, :]) + _dot(yc_ref[...], wo_ref[D_SSM:D_SSM + D_CONF, :]) + vec_ref[2:3, :]
        xh1, rstd1 = _ln_stats(ALPHA * h0 + out)
        h1 = xh1 * vec_ref[3:4, :] + vec_ref[4:5, :]
        h1b = h1.astype(BF16)
        h1b_ref[...] = h1b
        gate = _sig(_dot(h1b, wg_ref[...]))
        ple = _dot(p_ref[...].astype(BF16), wp_ref[...])
        xh2, rstd2 = _ln_stats(ALPHA * h1 + gate * ple)
        h2 = xh2 * vec_ref[5:6, :] + vec_ref[6:7, :]
        diff = h2 - t_ref[...]
        part = jnp.sum(jnp.sum(diff * diff, axis=1, keepdims=True), axis=0, keepdims=True) * (0.5 / D_MODEL)
        loss_ref[...] += jnp.broadcast_to(part, loss_ref.shape)
        dh2 = diff * (1.0 / D_MODEL)
        small_ref[3:4, :] += _colsum(dh2 * xh2)
        small_ref[4:5, :] += _colsum(dh2)
        dr2 = _ln_bwd(dh2 * vec_ref[5:6, :], xh2, rstd2)
        dgpre = (dr2 * ple * gate * (1.0 - gate)).astype(BF16)
        dgb_ref[...] = dgpre
        dpb_ref[...] = (dr2 * gate).astype(BF16)
        dh1 = ALPHA * dr2 + _dot_nt(dgpre, wg_ref[...])
        small_ref[1:2, :] += _colsum(dh1 * xh1)
        small_ref[2:3, :] += _colsum(dh1)
        dr1 = _ln_bwd(dh1 * vec_ref[3:4, :], xh1, rstd1)
        small_ref[0:1, :] += _colsum(dr1)
        dr1_ref[...] = dr1
        dr1b = dr1.astype(BF16)
        dr1b_ref[...] = dr1b
        dmix_ref[...] = _dot_nt(dr1b, wo_ref[...])

    d_mix = D_SSM + D_CONF
    return pl.pallas_call(
        body, grid=(n_tok // tl,), name="tail",
        in_specs=[_tile(tl, D_MODEL), _tile(tl, D_SSM), _tile(tl, D_CONF), _tile(tl, D_PLE), _tile(tl, D_MODEL),
                  _full((SUBLANES, D_MODEL)), _full((d_mix, D_MODEL), True), _full((D_MODEL, D_MODEL), True),
                  _full((D_PLE, D_MODEL), True)],
        out_specs=[_tile(tl, d_mix), _tile(tl, D_MODEL), _tile(tl, D_MODEL), _tile(tl, D_MODEL), _tile(tl, D_MODEL),
                   _tile(tl, D_MODEL), _full((SUBLANES, D_MODEL)), _full((SUBLANES, LANES))],
        out_shape=[jax.ShapeDtypeStruct((n_tok, d_mix), F32), jax.ShapeDtypeStruct((n_tok, D_MODEL), F32),
                   jax.ShapeDtypeStruct((n_tok, D_MODEL), BF16), jax.ShapeDtypeStruct((n_tok, D_MODEL), BF16),
                   jax.ShapeDtypeStruct((n_tok, D_MODEL), BF16), jax.ShapeDtypeStruct((n_tok, D_MODEL), BF16),
                   jax.ShapeDtypeStruct((SUBLANES, D_MODEL), F32), jax.ShapeDtypeStruct((SUBLANES, LANES), F32)],
        compiler_params=_params(),
    )(x, yssm, yconf, p, tgt, vec, w_out, wpg, wpp)


def _inproj_bwd(dxin, dz, dglu, dcg, ddtr, dr1, x, g, b, w_r, early, tl):
    n_tok = x.shape[0]
    n_t = n_tok // tl

    def body(dxin_ref, dz_ref, dglu_ref, dcg_ref, ddtr_ref, dr1_ref, x_ref, g_ref, b_ref, w_ref, early_ref,
             dx_ref, dpb_ref, small_ref, early_all, send_sems, recv_sems, loc_sem):
        @pl.when(pl.program_id(0) == 0)
        def _():
            _gather8_stage(0, early_ref, early_all, send_sems, recv_sems, loc_sem)
            small_ref[...] = jnp.zeros_like(small_ref)

        dh0 = ALPHA * dr1_ref[...]
        for ref, lo, hi in ((dxin_ref, R_XBC, R_Z), (dz_ref, R_Z, R_GLU), (dglu_ref, R_GLU, R_CG), (dcg_ref, R_CG, R_DT),
                            (ddtr_ref, R_DT, D_INR)):
            piece = ref[...].astype(BF16)
            dpb_ref[:, lo:hi] = piece
            dh0 = dh0 + _dot_nt(piece, w_ref[:, lo:hi])
        xhat, rstd = _ln_stats(x_ref[...])
        small_ref[0:1, :] += _colsum(dh0 * xhat)
        small_ref[1:2, :] += _colsum(dh0)
        dx_ref[...] = _ln_bwd(dh0 * g_ref[...], xhat, rstd)

        @pl.when(pl.program_id(0) == n_t - 1)
        def _():
            _gather8_stage(1, early_ref, early_all, send_sems, recv_sems, loc_sem)

    dma = pltpu.SemaphoreType.DMA
    return pl.pallas_call(
        body, grid=(n_t,), name="inproj_bwd",
        in_specs=[_tile(tl, D_XBC), _tile(tl, D_SSM), _tile(tl, 2 * D_CONF), _tile(tl, D_CONF), _tile(tl, LANES),
                  _tile(tl, D_MODEL), _tile(tl, D_MODEL), _full((1, D_MODEL)), _full((1, D_MODEL)),
                  _full((D_MODEL, D_INR), True), IN_VMEM],
        out_specs=[_tile(tl, D_MODEL), _tile(tl, D_INR), _full((SUBLANES, D_MODEL)), ANY],
        out_shape=[jax.ShapeDtypeStruct((n_tok, D_MODEL), F32), jax.ShapeDtypeStruct((n_tok, D_INR), BF16),
                   jax.ShapeDtypeStruct((SUBLANES, D_MODEL), F32), jax.ShapeDtypeStruct((N_DEV,) + early.shape, F32)],
        scratch_shapes=[dma((N_DEV - 1,)), dma((N_DEV - 1,)), dma],
        compiler_params=_params(),
    )(dxin, dz, dglu, dcg, ddtr, dr1, x, g, b, w_r, early)


def _tn_matmul(a, b, name, tn, tl, out_dtype=F32):
    n_tok, m = a.shape
    n = b.shape[1]
    n_l = n_tok // tl
    direct = out_dtype == F32

    def body(a_ref, b_ref, o_ref, *scratch):
        acc = o_ref if direct else scratch[0]

        @pl.when(pl.program_id(1) == 0)
        def _():
            acc[...] = jnp.zeros_like(acc)

        acc[...] += lax.dot_general(a_ref[...], b_ref[...], TN_DIMS, preferred_element_type=F32)
        if not direct:
            @pl.when(pl.program_id(1) == n_l - 1)
            def _():
                o_ref[...] = acc[...].astype(out_dtype)

    return pl.pallas_call(
        body, grid=(n // tn, n_l), name=name,
        in_specs=[pl.BlockSpec((tl, m), lambda j, l: (l, 0)), pl.BlockSpec((tl, tn), lambda j, l: (l, j))],
        out_specs=pl.BlockSpec((m, tn), lambda j, l: (0, j)),
        out_shape=jax.ShapeDtypeStruct((m, n), out_dtype),
        scratch_shapes=[] if direct else [pltpu.VMEM((m, tn), F32)],
        compiler_params=pltpu.CompilerParams(dimension_semantics=("parallel", "arbitrary"), vmem_limit_bytes=VMEM_LIMIT),
    )(a, b)


def _pad_rows(a, rows):
    return jnp.pad(a, ((0, rows - a.shape[0]), (0, 0)))


def _pad_lanes(a):
    return jnp.pad(a, ((0, 0), (0, LANES - a.shape[1])))


def _local_grads(x, p, tgt, w_r, later, ssm_cw, conf_cw, sm):
    n_tok = x.shape[0]
    tl = min(256, n_tok)
    tlm = min(512, n_tok)
    row = lambda v: v.reshape(1, -1)
    g_e, b_e = row(sm["ln_emb_g"]), row(sm["ln_emb_b"])
    h0b, xbc_in, z, glu, cgate, dtr, w_out_all, wpg_all, wpp_all = _ln_inproj(x, g_e, b_e, w_r, later, tlm)
    w_out_b = w_out_all.reshape(D_SSM + D_CONF, D_MODEL)
    wpg_b = wpg_all.reshape(D_MODEL, D_MODEL)
    wpp_b = wpp_all.transpose(1, 0, 2).reshape(D_PLE, D_MODEL)

    cw4 = _pad_rows(ssm_cw, SUBLANES)
    dtb, alog = _pad_lanes(sm["dt_bias"]), _pad_lanes(sm["a_log"])
    dsk = jnp.repeat(sm["d_skip"], HEAD_DIM, axis=1)
    pre, y, yssm, hprev = _ssd_fwd(xbc_in, z, dtr, cw4, sm["ssm_conv_b"], dtb, alog, dsk, sm["ssm_norm_g"], tl)

    cw31 = _pad_rows(conf_cw, CONF_HALO)
    u0, u1, yconf = _conf_fwd(glu, cgate, sm["b_glu"], cw31, sm["conf_conv_b"], sm["conf_ln_g"], sm["conf_ln_b"], tl)

    vec = jnp.concatenate([g_e, b_e, sm["b_out"], sm["ln1_g"], sm["ln1_b"], sm["ln2_g"], sm["ln2_b"],
                           jnp.zeros((1, D_MODEL), F32)], axis=0)
    dmix, dr1, dr1b, h1b, dgb, dpb, small_t, loss = _tail(
        x, yssm, yconf, p, tgt, vec, w_out_b, wpg_b, wpp_b, tlm)

    dwo = [_tn_matmul(yssm, dr1b, "dw_out_ssm", D_MODEL, tlm, BF16), _tn_matmul(yconf, dr1b, "dw_out_conf", D_MODEL, tlm, BF16)]
    dwp_ = [_tn_matmul(h1b, dgb, "dw_ple_gate", D_MODEL, tlm, BF16),
            _tn_matmul(p.astype(BF16), dpb, "dw_ple_proj", D_MODEL, tlm, BF16)]
    dglu, dcg, dcw31, dbglu, small_c, dw_out = _conf_bwd(dmix, D_SSM // D_CONF, u0, u1, glu, cgate, sm["b_glu"], cw31,
                                                          sm["conf_ln_g"], sm["conf_ln_b"], [W_OUT_DIRECT], dwo, tl)
    dxin, dz, ddtr, dcw4, dcb4, dgn, ddsk, dalog, ddtb, dwpg, dwpp = _ssd_bwd(
        dmix, y, z, pre, xbc_in, dtr, hprev, cw4, dtb, alog, dsk, sm["ssm_norm_g"], [WPG_DIRECT, WPP_DIRECT], dwp_, tl)
    grads = dict(
        ssm_conv_w=dcw4[0:SSM_K], ssm_conv_b=dcb4,
        dt_bias=ddtb[:, 0:N_HEADS], a_log=dalog[:, 0:N_HEADS], d_skip=ddsk[:, 0:N_HEADS], ssm_norm_g=dgn, b_glu=dbglu,
        conf_conv_w=dcw31[0:CONF_K], conf_conv_b=small_c[2:3], conf_ln_g=small_c[0:1], conf_ln_b=small_c[1:2],
        b_out=small_t[0:1], ln1_g=small_t[1:2], ln1_b=small_t[2:3], ln2_g=small_t[3:4], ln2_b=small_t[4:5])
    early = _pack_small({n: a for n, a in grads.items() if _small_row(n) >= EARLY_ROW}, loss[0, 0])[EARLY_ROW:]
    dx, dprojb, small_e, early_all = _inproj_bwd(dxin, dz, dglu, dcg, ddtr, dr1, x, g_e, b_e, w_r, early, tlm)
    late = _pack_small({"ln_emb_g": small_e[0], "ln_emb_b": small_e[1],
                        **{n: a for n, a in grads.items() if _small_row(n) < EARLY_ROW}})[:EARLY_ROW]

    dw_r = _tn_matmul(h0b, dprojb, "dw_in", D_INR // 3, tlm)
    return dx, dict(w_in=dw_r, w_out=dw_out, w_ple_gate=dwpg, w_ple_proj=dwpp, early=early_all, late=late)


N_CHIPS = 4
N_DEV = 8
W_IN_SH = D_IN // N_CHIPS
BIG = (("w_in", D_MODEL, W_IN_SH), ("w_out", (D_SSM + D_CONF) // N_CHIPS, D_MODEL),
       ("w_ple_gate", D_MODEL // N_CHIPS, D_MODEL), ("w_ple_proj", D_PLE, D_MODEL // N_CHIPS))
SEGS = ((R_XBC, 0, D_SSM), (R_XBC + D_SSM, 2048, 256), (R_XBC + D_SSM + 256, 2304, 256), (R_Z, 1024, D_SSM),
        (R_GLU, 2576, 2 * D_CONF), (R_CG, 4624, D_CONF), (R_DT, 2560, N_HEADS))
ROWS_CW4 = 2
ROWS_CW31 = 8
ROWS_CONV = 16
SMALL_ROWS = 56
SMALL_LAYOUT = (("ln_emb_g", 0, 1), ("ln_emb_b", 1, 1), ("ssm_conv_b", 2, 2), ("dt_bias", 4, 1), ("a_log", 5, 1),
                ("d_skip", 6, 1), ("ssm_norm_g", 7, 1), ("b_glu", 8, 2), ("conf_conv_b", 10, 1), ("conf_ln_g", 11, 1),
                ("conf_ln_b", 12, 1), ("b_out", 13, 1), ("ln1_g", 14, 1), ("ln1_b", 15, 1), ("ln2_g", 16, 1), ("ln2_b", 17, 1))
CONV_LAYOUT = (("ssm_conv_w", 18, 6, (SSM_K, D_XBC)), ("conf_conv_w", 24, 31, (CONF_K, D_CONF)))


def _rows_of(v, rows):
    flat = v.reshape(-1)
    return jnp.pad(flat, (0, rows * D_MODEL - flat.shape[0])).reshape(rows, D_MODEL)


LOSS_ROW = 55


EARLY_ROW = 8


def _small_row(name):
    return {n: r0 for n, r0, *_ in SMALL_LAYOUT + CONV_LAYOUT}[name]


def _pack_small(d, loss_share=None):
    parts = [_rows_of(d[n], r) if n in d else jnp.zeros((r, D_MODEL), F32) for n, _, r in SMALL_LAYOUT]
    for n, _, r, _ in CONV_LAYOUT:
        parts.append(_rows_of(d[n], r) if n in d else jnp.zeros((r, D_MODEL), F32))
    parts.append(_rows_of(jnp.zeros((1,), F32) if loss_share is None else loss_share, SMALL_ROWS - LOSS_ROW))
    return jnp.concatenate(parts, axis=0)


def _gather8_stage(stage, s_ref, all_ref, send_sems, recv_sems, loc_sem):
    x, y, c = _my_place()
    me = 4 * x + 2 * y + c
    copies = [pltpu.make_async_copy(s_ref, all_ref.at[me], loc_sem)]
    for mk in range(1, N_DEV):
        peer = (x ^ (mk >> 2), y ^ ((mk >> 1) & 1), c ^ (mk & 1))
        copies.append(_remote(s_ref, all_ref.at[me], send_sems.at[mk - 1], recv_sems.at[mk - 1], peer))
    for cp in copies:
        cp.start() if stage == 0 else cp.wait()


def _w_r_from_shards(w4):
    parts = []
    for _, o, wd in SEGS:
        lo, hi = o, o + wd
        while lo < hi:
            s = lo // W_IN_SH
            e = min(hi, (s + 1) * W_IN_SH)
            parts.append(w4[s][:, lo - s * W_IN_SH:e - s * W_IN_SH])
            lo = e
    parts.append(jnp.zeros((D_MODEL, LANES - N_HEADS), w4.dtype))
    return jnp.concatenate(parts, axis=1)


def _row_chunks(rows, n):
    return [(j * (rows // n), rows // n) for j in range(n)]


def _my_place():
    return lax.axis_index("x"), lax.axis_index("y"), lax.axis_index("c")


MESH_ID = pl.DeviceIdType.MESH
ANY = pl.BlockSpec(memory_space=pl.ANY)
IN_VMEM = pl.BlockSpec(memory_space=pltpu.VMEM)
CHIP_FLIPS = ((1, 0), (0, 1), (1, 1))


def _remote(src, dst, send_sem, recv_sem, peer):
    return pltpu.make_async_remote_copy(src, dst, send_sem, recv_sem, device_id=peer, device_id_type=MESH_ID)


GATHER_CHUNKS = (4, 2, 1, 1)


def _gather_plans(kinds):
    plan = [(a, o, n, rows // 2) for a, (rows, ch) in enumerate(kinds) for o, n in _row_chunks(rows // 2, ch)]
    own_plan = [(a, o, n) for a, (rows, ch) in enumerate(kinds) for o, n in _row_chunks(rows, 2 * ch)]
    return plan, own_plan


def _gather_sems(plan, own_plan):
    hop = pltpu.SemaphoreType.DMA((3, len(plan)))
    return [hop, hop, hop, hop, pltpu.SemaphoreType.DMA((len(own_plan),))]


def _gather_stage(stage, ins, outs, plan, own_plan, sems):
    send_a, recv_a, send_b, recv_b, loc = sems
    x, y, c = _my_place()
    s = 2 * x + y
    sibling = (x, y, 1 - c)
    own = [pltpu.make_async_copy(ins[a].at[pl.ds(o, n)], outs[a].at[s, pl.ds(o, n)], loc.at[j])
           for j, (a, o, n) in enumerate(own_plan)]
    first, arrive, passed, arrive_b = [], [], [], []
    for k, (fx, fy) in enumerate(CHIP_FLIPS):
        peer = (x ^ fx, y ^ fy, c)
        sk = 2 * (x ^ fx) + (y ^ fy)
        for j, (a, o, n, h) in enumerate(plan):
            mine = pl.ds(pl.multiple_of(c * h + o, 16), n)
            theirs = pl.ds(pl.multiple_of((1 - c) * h + o, 16), n)
            first.append(_remote(ins[a].at[mine], outs[a].at[s, mine], send_a.at[k, j], recv_a.at[k, j], peer))
            land = outs[a].at[sk, mine]
            arrive.append(_remote(land, land, send_a.at[k, j], recv_a.at[k, j], peer))
            passed.append(_remote(land, land, send_b.at[k, j], recv_b.at[k, j], sibling))
            land_b = outs[a].at[sk, theirs]
            arrive_b.append(_remote(land_b, land_b, send_b.at[k, j], recv_b.at[k, j], sibling))
    if stage == 0:
        for cp in own + first:
            cp.start()
    elif stage == 1:
        for got, fwd in zip(arrive, passed):
            got.wait_recv()
            fwd.start()
    else:
        for got in arrive_b:
            got.wait_recv()
        for cp in first + passed:
            cp.wait_send()
        for cp in own:
            cp.wait()


W_OUT_DIRECT = (512, D_MODEL, 2, ((0, 0, 0), (0, 512, 0), (1, 0, 0), (1, 512, 0)))
WPG_DIRECT = (256, D_MODEL, 1, ((0, 0, 0), (0, 256, 0), (0, 512, 0), (0, 768, 0)))
WPP_DIRECT = (D_PLE, 256, 1, ((1, 0, 0), (1, 0, 256), (1, 0, 512), (1, 0, 768)))


def _direct_sems(specs):
    out = []
    for _, _, copies, _ in specs:
        out += [pltpu.SemaphoreType.DMA((N_CHIPS, 2, copies)), pltpu.SemaphoreType.DMA((N_DEV, copies)),
                pltpu.SemaphoreType.DMA((copies,))]
    return out


def _direct_stage(stage, specs, srcs, recvs, sems):
    x, y, c = _my_place()
    me = 4 * x + 2 * y + c
    s = 2 * x + y
    for r, (rows, cols, copies, where) in enumerate(specs):
        send_sems, recv_sems, loc_sems = sems[3 * r:3 * r + 3]
        for k, (o, n) in enumerate(_row_chunks(rows, copies)):
            for t, (si, row0, col0) in enumerate(where):
                src = srcs[si].at[pl.ds(row0 + o, n), pl.ds(col0, cols)]
                dst = recvs[r].at[me, pl.ds(o, n)]
                for cc in range(2):
                    to_self = jnp.logical_and(s == t, c == cc)
                    away = _remote(src, dst, send_sems.at[t, cc, k], recv_sems.at[me, k], (t // 2, t % 2, cc))
                    here = pltpu.make_async_copy(src, dst, loc_sems.at[k])

                    @pl.when(to_self)
                    def _():
                        here.start() if stage == 0 else here.wait()

                    @pl.when(jnp.logical_not(to_self))
                    def _():
                        away.start() if stage == 0 else away.wait_send()
            if stage == 1:
                for j in range(N_DEV):
                    land = recvs[r].at[j, pl.ds(o, n)]

                    @pl.when(j != me)
                    def _():
                        _remote(land, land, send_sems.at[0, 0, k], recv_sems.at[j, k], (0, 0, 0)).wait_recv()


def _gather_weights(w_in_b, conv_f):
    plan, own_plan = _gather_plans([(BIG[0][1], GATHER_CHUNKS[0])])

    def body(w_ref, conv_ref, w_all, conv_all, *sems):
        conv_send, conv_recv, conv_loc = sems[5:]
        x, y, c = _my_place()
        s = 2 * x + y
        conv_own = pltpu.make_async_copy(conv_ref, conv_all.at[s], conv_loc)
        conv_out = [_remote(conv_ref, conv_all.at[s], conv_send.at[k], conv_recv.at[k], (x ^ fx, y ^ fy, c))
                    for k, (fx, fy) in enumerate(CHIP_FLIPS)]
        _gather_stage(0, [w_ref], [w_all], plan, own_plan, sems[0:5])
        for cp in [conv_own] + conv_out:
            cp.start()
        _gather_stage(1, [w_ref], [w_all], plan, own_plan, sems[0:5])
        _gather_stage(2, [w_ref], [w_all], plan, own_plan, sems[0:5])
        for cp in conv_out:
            cp.wait()
        conv_own.wait()

    arrays = [w_in_b, conv_f]
    return pl.pallas_call(
        body, name="gather_weights", in_specs=[IN_VMEM] * 2, out_specs=[ANY] * 2,
        out_shape=[jax.ShapeDtypeStruct((N_CHIPS,) + a.shape, a.dtype) for a in arrays],
        scratch_shapes=_gather_sems(plan, own_plan)
        + [pltpu.SemaphoreType.DMA((3,)), pltpu.SemaphoreType.DMA((3,)), pltpu.SemaphoreType.DMA],
    )(*arrays)


SHARE_CHUNKS = 8


OWN_TILES = (((0, 1024), (1536, 512)), ((1024, 512), (1920, 640), (2560, 256), (5632, 128)), ((2688, 1536),), ((4096, 1536),))
OWN_W = 1536
RED_CHUNKS = 8


def _own_columns(t, rows):
    if t == 0:
        return rows[:, 0:W_IN_SH]
    if t == 1:
        return jnp.concatenate([rows[:, 516:1152], rows[:, 0:512], rows[:, 1408:1424], rows[:, 1152:1400]], axis=1)
    first = (2808 - 2688) if t == 2 else (4220 - 4096)
    return rows[:, first:first + W_IN_SH]


def _reduce_w_in(dw_r):
    half = D_MODEL // 2
    rc = half // RED_CHUNKS
    max_r = max(len(r) for r in OWN_TILES)

    def body(g_ref, got_ref, theirs, a_buf, b_buf, sum_buf, d2d_send, d2d_recv, ld_sems, ici_send, ici_recv, own_sems):
        x, y, c = _my_place()
        s = 2 * x + y
        sibling = (x, y, 1 - c)
        swap = [_remote(g_ref.at[pl.ds(pl.multiple_of((1 - c) * half + k * rc, SUBLANES), rc)], theirs.at[pl.ds(k * rc, rc)],
                        d2d_send.at[k], d2d_recv.at[k], sibling) for k in range(RED_CHUNKS)]
        for cp in swap:
            cp.start()

        def tiles(k, t, r):
            k0, wd = OWN_TILES[t][r]
            at = sum(w_ for _, w_ in OWN_TILES[t][:r])
            src = sum_buf.at[k, pl.ds(0, rc), pl.ds(k0, wd)]
            dst = got_ref.at[s, pl.ds(k * rc, rc), pl.ds(at, wd)]
            return (_remote(src, dst, ici_send.at[t, r, k], ici_recv.at[s, r, k], (t // 2, t % 2, c)),
                    pltpu.make_async_copy(src, dst, own_sems.at[r, k]))

        mine = [pltpu.make_async_copy(g_ref.at[pl.ds(pl.multiple_of(c * half + k * rc, SUBLANES), rc)], a_buf.at[k % 2],
                                      ld_sems.at[0, k % 2]) for k in range(RED_CHUNKS)]
        other = [pltpu.make_async_copy(theirs.at[pl.ds(k * rc, rc)], b_buf.at[k % 2], ld_sems.at[1, k % 2])
                 for k in range(RED_CHUNKS)]
        mine[0].start()
        for k in range(RED_CHUNKS):
            slot = k % 2
            swap[k].wait_recv()
            other[k].start()
            if k + 1 < RED_CHUNKS:
                mine[k + 1].start()
            mine[k].wait()
            other[k].wait()
            sum_buf[k] = (a_buf[slot] + b_buf[slot]).astype(BF16)
            for t in range(N_CHIPS):
                for r in range(len(OWN_TILES[t])):
                    away, here = tiles(k, t, r)

                    @pl.when(t != s)
                    def _():
                        away.start()

                    @pl.when(t == s)
                    def _():
                        here.start()

        for k in range(RED_CHUNKS):
            swap[k].wait_send()
            for t in range(N_CHIPS):
                for r, (_, wd) in enumerate(OWN_TILES[t]):
                    away, here = tiles(k, t, r)
                    at = sum(w_ for _, w_ in OWN_TILES[t][:r])

                    @pl.when(t != s)
                    def _():
                        away.wait_send()

                    @pl.when(t == s)
                    def _():
                        here.wait()
                        for j in range(N_CHIPS):
                            if j != t:
                                land = got_ref.at[j, pl.ds(k * rc, rc), pl.ds(at, wd)]
                                _remote(land, land, ici_send.at[0, 0, 0], ici_recv.at[j, r, k], (0, 0, 0)).wait_recv()

    dma = pltpu.SemaphoreType.DMA
    got, _ = pl.pallas_call(
        body, name="reduce_w_in", in_specs=[ANY], out_specs=[ANY, ANY],
        out_shape=[jax.ShapeDtypeStruct((N_CHIPS, half, OWN_W), BF16), jax.ShapeDtypeStruct((half, D_INR), F32)],
        scratch_shapes=[pltpu.VMEM((2, rc, D_INR), F32), pltpu.VMEM((2, rc, D_INR), F32), pltpu.VMEM((RED_CHUNKS, rc, D_INR), BF16),
                        dma((RED_CHUNKS,)), dma((RED_CHUNKS,)), dma((2, 2)), dma((N_CHIPS, max_r, RED_CHUNKS)),
                        dma((N_CHIPS, max_r, RED_CHUNKS)), dma((max_r, RED_CHUNKS))],
        compiler_params=pltpu.CompilerParams(vmem_limit_bytes=VMEM_LIMIT),
    )(dw_r)
    return got


def _add_slots(got):
    _, rows, cols = got.shape
    tr = 128

    def body(g_ref, o_ref):
        o_ref[...] = ((g_ref[0].astype(F32) + g_ref[1].astype(F32)) + g_ref[2].astype(F32)) + g_ref[3].astype(F32)

    return pl.pallas_call(body, grid=(rows // tr,), name="add_chips_w_in",
                          in_specs=[pl.BlockSpec((N_CHIPS, tr, cols), lambda i: (0, i, 0))],
                          out_specs=pl.BlockSpec((tr, cols), lambda i: (i, 0)),
                          out_shape=jax.ShapeDtypeStruct((rows, cols), F32), compiler_params=_params(seq=False))(got)


def _share_halves(tot, spack):
    plan = _row_chunks(tot.shape[0], SHARE_CHUNKS)

    def body(t_ref, s_ref, both, small_all, send_sems, recv_sems, loc, send_small, recv_small, loc_small):
        x, y, c = _my_place()
        own = [pltpu.make_async_copy(t_ref.at[pl.ds(o, n)], both.at[c, pl.ds(o, n)], loc.at[j]) for j, (o, n) in enumerate(plan)]
        sends = [_remote(t_ref.at[pl.ds(o, n)], both.at[c, pl.ds(o, n)], send_sems.at[j], recv_sems.at[j], (x, y, 1 - c))
                 for j, (o, n) in enumerate(plan)]
        _gather8_stage(0, s_ref, small_all, send_small, recv_small, loc_small)
        for cp in own + sends:
            cp.start()
        for cp in sends:
            cp.wait()
        for cp in own:
            cp.wait()
        _gather8_stage(1, s_ref, small_all, send_small, recv_small, loc_small)

    dma = pltpu.SemaphoreType.DMA
    return pl.pallas_call(
        body, name="share_halves", in_specs=[IN_VMEM, IN_VMEM], out_specs=[ANY, ANY],
        out_shape=[jax.ShapeDtypeStruct((2,) + tot.shape, F32), jax.ShapeDtypeStruct((N_DEV,) + spack.shape, F32)],
        scratch_shapes=[dma((len(plan),)), dma((len(plan),)), dma((len(plan),)), dma((N_DEV - 1,)), dma((N_DEV - 1,)), dma],
    )(tot, spack)


def _adam_math(w, g, m, v):
    m = ADAM_B1 * m + (1.0 - ADAM_B1) * g
    v = ADAM_B2 * v + (1.0 - ADAM_B2) * (g * g)
    m_hat = m / (1.0 - ADAM_B1 ** ADAM_STEP)
    v_hat = v / (1.0 - ADAM_B2 ** ADAM_STEP)
    return -ADAM_LR * (m_hat / (jnp.sqrt(v_hat) + ADAM_EPS) + ADAM_WD * w), m, v


def _adam(w, g, m, v, name):
    rows, cols = w.shape

    def body(w_ref, g_ref, m_ref, v_ref, d_ref, nm_ref, nv_ref):
        d_ref[...], nm_ref[...], nv_ref[...] = _adam_math(w_ref[...], g_ref[...], m_ref[...], v_ref[...])

    if rows <= 256 or rows % 256 == 0:
        tr = min(rows, 256)
        n_blocks, spec = rows // tr, pl.BlockSpec((tr, cols), lambda i: (i, 0))
    else:
        n_blocks, spec = cols // 256, pl.BlockSpec((rows, 256), lambda i: (0, i))
    return pl.pallas_call(body, grid=(n_blocks,), name=name, in_specs=[spec] * 4, out_specs=[spec] * 3,
                          out_shape=[jax.ShapeDtypeStruct(w.shape, F32)] * 3, compiler_params=_params(seq=False))(w, g, m, v)


def _adam_sum(w, parts, m, v, name):
    rows, cols = w.shape
    tr = rows if rows <= 256 else 256

    def body(w_ref, p_ref, m_ref, v_ref, g_ref, d_ref, nm_ref, nv_ref):
        g = p_ref[0].astype(F32)
        for j in range(1, N_DEV):
            g = g + p_ref[j].astype(F32)
        g_ref[...] = g
        d_ref[...], nm_ref[...], nv_ref[...] = _adam_math(w_ref[...], g, m_ref[...], v_ref[...])

    spec = pl.BlockSpec((tr, cols), lambda i: (i, 0))
    return pl.pallas_call(
        body, grid=(rows // tr,), name=name,
        in_specs=[spec, pl.BlockSpec((N_DEV, tr, cols), lambda i: (0, i, 0)), spec, spec], out_specs=[spec] * 4,
        out_shape=[jax.ShapeDtypeStruct(w.shape, F32)] * 4, compiler_params=_params(seq=False))(w, parts, m, v)


def _adam_small(late, early, ws, ms, vs):
    n_w = len(SMALL_LAYOUT)

    def body(late_ref, early_ref, *refs):
        w_refs, m_refs, v_refs = refs[0:n_w], refs[n_w:2 * n_w], refs[2 * n_w:3 * n_w]
        sum_ref = refs[3 * n_w]
        outs = refs[3 * n_w + 1:]
        for p_ref, r0, r1 in ((late_ref, 0, EARLY_ROW), (early_ref, EARLY_ROW, SMALL_ROWS)):
            g = p_ref[0]
            for k in range(1, N_DEV):
                g = g + p_ref[k]
            sum_ref[r0:r1, :] = g
        for a, (_, r0, rows) in enumerate(SMALL_LAYOUT):
            width = w_refs[a].shape[1]
            for j in range(rows):
                lo, hi = j * D_MODEL, min((j + 1) * D_MODEL, width)
                gj = sum_ref[r0 + j:r0 + j + 1, 0:hi - lo]
                d, nm, nv = _adam_math(w_refs[a][:, lo:hi], gj, m_refs[a][:, lo:hi], v_refs[a][:, lo:hi])
                for out, val in zip(outs[4 * a:4 * a + 4], (gj, d, nm, nv)):
                    out[:, lo:hi] = val

    shapes = [jax.ShapeDtypeStruct((SMALL_ROWS, D_MODEL), F32)]
    for wa in ws:
        shapes += [jax.ShapeDtypeStruct(wa.shape, F32)] * 4
    res = pl.pallas_call(body, name="adam_small", out_shape=shapes)(late, early, *ws, *ms, *vs)
    return res[0], [res[1 + 4 * a:5 + 4 * a] for a in range(n_w)]


def kernel(x, p, ln_emb_g, ln_emb_b, w_in, ssm_conv_w, ssm_conv_b, dt_bias, a_log, d_skip, ssm_norm_g, b_glu, conf_conv_w, conf_conv_b, conf_ln_g, conf_ln_b, w_out, b_out, ln1_g, ln1_b, w_ple_gate, w_ple_proj, ln2_g, ln2_b, loss_target, m_ln_emb_g, m_ln_emb_b, m_w_in, m_ssm_conv_w, m_ssm_conv_b, m_dt_bias, m_a_log, m_d_skip, m_ssm_norm_g, m_b_glu, m_conf_conv_w, m_conf_conv_b, m_conf_ln_g, m_conf_ln_b, m_w_out, m_b_out, m_ln1_g, m_ln1_b, m_w_ple_gate, m_w_ple_proj, m_ln2_g, m_ln2_b, v_ln_emb_g, v_ln_emb_b, v_w_in, v_ssm_conv_w, v_ssm_conv_b, v_dt_bias, v_a_log, v_d_skip, v_ssm_norm_g, v_b_glu, v_conf_conv_w, v_conf_conv_b, v_conf_ln_g, v_conf_ln_b, v_w_out, v_b_out, v_ln1_g, v_ln1_b, v_w_ple_gate, v_w_ple_proj, v_ln2_g, v_ln2_b):
    order = ("ln_emb_g", "ln_emb_b", "w_in", "ssm_conv_w", "ssm_conv_b", "dt_bias", "a_log", "d_skip", "ssm_norm_g", "b_glu",
             "conf_conv_w", "conf_conv_b", "conf_ln_g", "conf_ln_b", "w_out", "b_out", "ln1_g", "ln1_b", "w_ple_gate",
             "w_ple_proj", "ln2_g", "ln2_b")
    w = dict(zip(order, (ln_emb_g, ln_emb_b, w_in, ssm_conv_w, ssm_conv_b, dt_bias, a_log, d_skip, ssm_norm_g, b_glu,
                         conf_conv_w, conf_conv_b, conf_ln_g, conf_ln_b, w_out, b_out, ln1_g, ln1_b, w_ple_gate, w_ple_proj,
                         ln2_g, ln2_b)))
    m = dict(zip(order, (m_ln_emb_g, m_ln_emb_b, m_w_in, m_ssm_conv_w, m_ssm_conv_b, m_dt_bias, m_a_log, m_d_skip,
                         m_ssm_norm_g, m_b_glu, m_conf_conv_w, m_conf_conv_b, m_conf_ln_g, m_conf_ln_b, m_w_out, m_b_out,
                         m_ln1_g, m_ln1_b, m_w_ple_gate, m_w_ple_proj, m_ln2_g, m_ln2_b)))
    v = dict(zip(order, (v_ln_emb_g, v_ln_emb_b, v_w_in, v_ssm_conv_w, v_ssm_conv_b, v_dt_bias, v_a_log, v_d_skip,
                         v_ssm_norm_g, v_b_glu, v_conf_conv_w, v_conf_conv_b, v_conf_ln_g, v_conf_ln_b, v_w_out, v_b_out,
                         v_ln1_g, v_ln1_b, v_w_ple_gate, v_w_ple_proj, v_ln2_g, v_ln2_b)))

    conv_f = jnp.concatenate([_rows_of(w["ssm_conv_w"], ROWS_CW4), _rows_of(w["conf_conv_w"], ROWS_CW31),
                              jnp.zeros((ROWS_CONV - ROWS_CW4 - ROWS_CW31, D_MODEL), F32)], axis=0)
    shards_b = [w[n][0].astype(BF16) for n, _, _ in BIG]
    w_in_all, conv_all = _gather_weights(shards_b[0], conv_f)
    w_r = _w_r_from_shards(w_in_all)
    cw4 = conv_all[:, 0:ROWS_CW4].reshape(N_CHIPS, -1)[:, :SSM_K * D_XBC // N_CHIPS]
    cw4 = cw4.reshape(N_CHIPS, SSM_K, D_XBC // N_CHIPS).transpose(1, 0, 2).reshape(SSM_K, D_XBC)
    cw31 = conv_all[:, ROWS_CW4:ROWS_CW4 + ROWS_CW31].reshape(N_CHIPS, -1)[:, :CONF_K * D_CONF // N_CHIPS]
    cw31 = cw31.reshape(N_CHIPS, CONF_K, D_CONF // N_CHIPS).transpose(1, 0, 2).reshape(CONF_K, D_CONF)

    small_names = [n for n, _, _ in SMALL_LAYOUT]
    sm = {n: w[n] for n in small_names}
    dx, grads = _local_grads(x[0], p[0, 0], loss_target[0], w_r, shards_b[1:], cw4, cw31, sm)

    chip_i = 2 * lax.axis_index("x") + lax.axis_index("y")
    mine = lax.switch(chip_i, [functools.partial(_own_columns, t) for t in range(N_CHIPS)], _add_slots(_reduce_w_in(grads["w_in"])))
    both, late_all = _share_halves(mine, grads["late"])
    g_w_in = both.reshape(D_MODEL, W_IN_SH)

    out_g, out_d, out_m, out_v = {}, {}, {}, {}
    as_row = lambda a: a.reshape(1, -1)
    g_s, small_out = _adam_small(late_all, grads["early"], [as_row(w[n]) for n in small_names],
                                 [as_row(m[n]) for n in small_names], [as_row(v[n]) for n in small_names])
    loss = g_s[LOSS_ROW, 0]
    g_shard = {"w_in": g_w_in}
    for n, r0, r, shape in CONV_LAYOUT:
        whole = g_s[r0:r0 + r].reshape(-1)[:shape[0] * shape[1]].reshape(shape)
        g_shard[n] = lax.dynamic_slice_in_dim(whole, chip_i * (shape[1] // N_CHIPS), shape[1] // N_CHIPS, axis=1)
    for n, g in g_shard.items():
        if n == "w_in":
            tr_ = lambda a: jnp.swapaxes(a, 0, 1)
            d, nm, nv = [tr_(a) for a in _adam(tr_(w[n][0]), tr_(g), tr_(m[n][0]), tr_(v[n][0]), "adam_" + n)]
        else:
            d, nm, nv = _adam(w[n][0], g, m[n][0], v[n][0], "adam_" + n)
        out_g[n], out_d[n], out_m[n], out_v[n] = g[None], d[None], nm[None], nv[None]
    for n, _, _ in BIG[1:]:
        out_g[n], out_d[n], out_m[n], out_v[n] = [a[None] for a in _adam_sum(w[n][0], grads[n], m[n][0], v[n][0], "adam_" + n)]
    for n, four in zip(small_names, small_out):
        out_g[n], out_d[n], out_m[n], out_v[n] = [a.reshape(w[n].shape) for a in four]
    return (loss, dx[None], *[out_g[n] for n in order], *[out_d[n] for n in order], *[out_m[n] for n in order],
            *[out_v[n] for n in order])
```

```python
import functools

import jax
import jax.numpy as jnp
from jax import lax
from jax.experimental import pallas as pl
from jax.experimental.pallas import tpu as pltpu

F32 = jnp.float32
BF16 = jnp.bfloat16

D_MODEL = 1024
D_PLE = 256
D_SSM = 1024
D_CONF = 1024
N_HEADS = 16
HEAD_DIM = 64
N_GROUPS = 2
N_STATE = 128
CHUNK = 128
SSM_K = 4
CONF_K = 31
D_XBC = D_SSM + 2 * N_GROUPS * N_STATE
D_IN = 5648
R_XBC, R_Z, R_GLU, R_CG, R_DT, D_INR = 0, 1536, 2560, 4608, 5632, 5760
LN_EPS = 1e-5
RMS_EPS = 1e-5
ALPHA = 2.0 ** 0.25
ADAM_LR, ADAM_B1, ADAM_B2, ADAM_EPS, ADAM_WD, ADAM_STEP = 0.001, 0.9, 0.999, 1e-08, 0.01, 10
NEG_BIG = -1e30
LANES = 128
SUBLANES = 8
VMEM_LIMIT = 56 * 1024 * 1024
HIGHEST = lax.Precision.HIGHEST
NT_DIMS = (((1,), (1,)), ((), ()))
TN_DIMS = (((0,), (0,)), ((), ()))


def _sig(v):
    return jax.nn.sigmoid(v)


def _dsilu(v, s):
    return s * (1.0 + v * (1.0 - s))


def _ln_stats(v):
    mu = jnp.mean(v, axis=-1, keepdims=True)
    c = v - mu
    var = jnp.mean(c * c, axis=-1, keepdims=True)
    rstd = lax.rsqrt(var + LN_EPS)
    return c * rstd, rstd


def _ln_bwd(dxhat, xhat, rstd):
    m1 = jnp.mean(dxhat, axis=-1, keepdims=True)
    m2 = jnp.mean(dxhat * xhat, axis=-1, keepdims=True)
    return rstd * (dxhat - m1 - xhat * m2)


def _softplus(v):
    return jnp.maximum(v, 0.0) + jnp.log1p(jnp.exp(-jnp.abs(v)))


def _colsum(v):
    return jnp.sum(v, axis=0, keepdims=True)


def _dot(a, b):
    return jnp.dot(a, b, preferred_element_type=F32)


def _dot_nt(a, b):
    return lax.dot_general(a, b, NT_DIMS, preferred_element_type=F32)


def _tile(tl, c, rev_of=None):
    if rev_of is None:
        return pl.BlockSpec((tl, c), lambda i: (i, 0))
    return pl.BlockSpec((tl, c), lambda i: (rev_of - 1 - i, 0))


def _full(shape, single=False):
    nd = len(shape)
    if single:
        return pl.BlockSpec(shape, lambda i: (0,) * nd, pipeline_mode=pl.Buffered(1))
    return pl.BlockSpec(shape, lambda i: (0,) * nd)


def _params(seq=True):
    return pltpu.CompilerParams(dimension_semantics=("arbitrary",) if seq else ("parallel",), vmem_limit_bytes=VMEM_LIMIT)


def _ln_inproj(x, g, b, w4, later, tl):
    n_tok = x.shape[0]
    n_t = n_tok // tl
    n_l = len(later)
    plan, own_plan = _gather_plans([(rows, GATHER_CHUNKS[a]) for a, (_, rows, _) in enumerate(BIG)][1:])

    def body(x_ref, g_ref, b_ref, w4_ref, *rest):
        part_refs = rest[0:n_l]
        h0b_ref, xbc_ref, z_ref, glu_ref, cg_ref, dtr_ref = rest[n_l:n_l + 6]
        all_refs, w_out = rest[n_l + 6:2 * n_l + 6], rest[2 * n_l + 6]
        w_ref, w_sem = rest[2 * n_l + 7:2 * n_l + 9]
        sems = rest[2 * n_l + 9:]
        i = pl.program_id(0)
        keep = pltpu.make_async_copy(w_ref, w_out, w_sem)

        @pl.when(i == 0)
        def _():
            _gather_stage(0, part_refs, all_refs, plan, own_plan, sems)
            for kcol, o, wd in SEGS:
                lo, hi = o, o + wd
                while lo < hi:
                    sh = lo // W_IN_SH
                    e = min(hi, (sh + 1) * W_IN_SH)
                    w_ref[:, kcol + lo - o:kcol + e - o] = w4_ref[sh, :, lo - sh * W_IN_SH:e - sh * W_IN_SH]
                    lo = e
            w_ref[:, R_DT + N_HEADS:D_INR] = jnp.zeros((D_MODEL, D_INR - R_DT - N_HEADS), BF16)
            keep.start()

        xhat, _ = _ln_stats(x_ref[...])
        hb = (xhat * g_ref[...] + b_ref[...]).astype(BF16)
        h0b_ref[...] = hb
        xbc_ref[...] = _dot(hb, w_ref[:, R_XBC:R_Z])
        z_ref[...] = _dot(hb, w_ref[:, R_Z:R_GLU])
        glu_ref[...] = _dot(hb, w_ref[:, R_GLU:R_CG])
        cg_ref[...] = _dot(hb, w_ref[:, R_CG:R_DT])
        dtr_ref[...] = _dot(hb, w_ref[:, R_DT:D_INR])

        @pl.when(i == (3 * n_t) // 4)
        def _():
            _gather_stage(1, part_refs, all_refs, plan, own_plan, sems)

        @pl.when(i == n_t - 1)
        def _():
            _gather_stage(2, part_refs, all_refs, plan, own_plan, sems)
            keep.wait()

    widths = (D_MODEL, D_XBC, D_SSM, 2 * D_CONF, D_CONF, LANES)
    dtypes = (BF16, F32, F32, F32, F32, F32)
    return pl.pallas_call(
        body, grid=(n_t,), name="ln_inproj",
        in_specs=[_tile(tl, D_MODEL), _full((1, D_MODEL)), _full((1, D_MODEL)), IN_VMEM] + [IN_VMEM] * n_l,
        out_specs=[_tile(tl, w) for w in widths] + [ANY] * (n_l + 1),
        out_shape=[jax.ShapeDtypeStruct((n_tok, w), dt) for w, dt in zip(widths, dtypes)]
        + [jax.ShapeDtypeStruct((N_CHIPS,) + a.shape, a.dtype) for a in later]
        + [jax.ShapeDtypeStruct((D_MODEL, D_INR), BF16)],
        scratch_shapes=[pltpu.VMEM((D_MODEL, D_INR), BF16), pltpu.SemaphoreType.DMA] + _gather_sems(plan, own_plan),
        compiler_params=_params(),
    )(x, g, b, w4, *later)


def _chunk_common(adt_c):
    row = lax.broadcasted_iota(jnp.int32, (CHUNK, CHUNK), 0)
    col = lax.broadcasted_iota(jnp.int32, (CHUNK, CHUNK), 1)
    tril = row >= col
    acs = jnp.dot(tril.astype(F32), adt_c, precision=HIGHEST, preferred_element_type=F32)
    last = acs[CHUNK - 1:CHUNK, :]
    return dict(row=row, col=col, tril=tril, lo=col < HEAD_DIM, acs=acs, acs_t=acs.T, e=jnp.exp(acs),
                dec=jnp.exp(last - acs), cd=jnp.exp(last))


def _decay_mask(cm, h):
    return jnp.exp(jnp.where(cm["tril"], cm["acs"][:, h:h + 1] - cm["acs_t"][h:h + 1, :], NEG_BIG))


def _head_lane_matrix():
    return (jnp.arange(D_SSM)[None, :] // HEAD_DIM == jnp.arange(LANES)[:, None]).astype(BF16)


def _per_head_lanes(v, exp_ref):
    hi = v.astype(BF16)
    lo = (v - hi.astype(F32)).astype(BF16)
    return _dot(hi, exp_ref[...]) + _dot(lo, exp_ref[...])


SSM_BWD_SHIFTS = (1, 2, 3)


def _ssd_fwd(xbc_in, z, dtr, cw, cb, dtb, alog, dsk, gnorm, tl):
    n_tok = xbc_in.shape[0]
    nq = tl // CHUNK

    def body(xin_ref, z_ref, dtr_ref, cw_ref, cb_ref, dtb_ref, alog_ref, dsk_ref, gn_ref, exp_ref,
             pre_ref, y_ref, yssm_ref, hprev_ref, buf, hst):
        @pl.when(pl.program_id(0) == 0)
        def _():
            buf[0:SUBLANES, :] = jnp.zeros((SUBLANES, D_XBC), F32)
            hst[...] = jnp.zeros_like(hst)

        buf[SUBLANES:SUBLANES + tl, :] = xin_ref[...]
        pre = cb_ref[...] + jnp.zeros((tl, D_XBC), F32)
        for k in range(SSM_K):
            off = SUBLANES - (SSM_K - 1) + k
            pre = pre + buf[off:off + tl, :] * cw_ref[k:k + 1, :]
        buf[0:SUBLANES, :] = buf[tl:tl + SUBLANES, :]
        pre_ref[...] = pre
        xbc = pre * _sig(pre)
        dt = _softplus(dtr_ref[...] + dtb_ref[...])
        a = -jnp.exp(alog_ref[...])
        adt = dt * a
        for q in range(nq):
            r0 = q * CHUNK
            cm = _chunk_common(adt[r0:r0 + CHUNK, :])
            dt_x = _per_head_lanes(dt[r0:r0 + CHUNK, :], exp_ref)
            e_x = _per_head_lanes(cm["e"], exp_ref)
            dec_x = _per_head_lanes(cm["dec"], exp_ref)
            for g in range(N_GROUPS):
                bg = xbc[r0:r0 + CHUNK, D_SSM + g * N_STATE:D_SSM + (g + 1) * N_STATE].astype(BF16)
                cg_ = xbc[r0:r0 + CHUNK, D_SSM + (N_GROUPS + g) * N_STATE:D_SSM + (N_GROUPS + g + 1) * N_STATE].astype(BF16)
                gm = _dot_nt(cg_, bg)
                for k in range(N_HEADS // N_GROUPS // 2):
                    ha = (N_HEADS // N_GROUPS) * g + 2 * k
                    c0 = ha * HEAD_DIM
                    xh2 = xbc[r0:r0 + CHUNK, c0:c0 + LANES]
                    x2 = xh2 * dt_x[:, c0:c0 + LANES]
                    x2b = x2.astype(BF16)
                    ya = _dot((gm * _decay_mask(cm, ha)).astype(BF16), x2b)
                    yb = _dot((gm * _decay_mask(cm, ha + 1)).astype(BF16), x2b)
                    h2 = hst[c0:c0 + LANES, :]
                    hprev_ref[q, c0:c0 + LANES, :] = h2
                    z2 = _dot_nt(cg_, h2.astype(BF16))
                    y2 = jnp.where(cm["lo"], ya, yb) + z2 * e_x[:, c0:c0 + LANES] + dsk_ref[:, c0:c0 + LANES] * xh2
                    y_ref[r0:r0 + CHUNK, c0:c0 + LANES] = y2
                    s2 = _dot((x2 * dec_x[:, c0:c0 + LANES]).T.astype(BF16), bg)
                    cd2 = jnp.where(cm["row"] < HEAD_DIM, cm["cd"][:, ha:ha + 1], cm["cd"][:, ha + 1:ha + 2])
                    hst[c0:c0 + LANES, :] = cd2 * h2 + s2
        yv = y_ref[...]
        zv = z_ref[...]
        yz = yv * (zv * _sig(zv))
        gw = D_SSM // N_GROUPS
        for g in range(N_GROUPS):
            seg = yz[:, g * gw:(g + 1) * gw]
            r = lax.rsqrt(jnp.mean(seg * seg, axis=-1, keepdims=True) + RMS_EPS)
            yssm_ref[:, g * gw:(g + 1) * gw] = (seg * r * gn_ref[:, g * gw:(g + 1) * gw]).astype(BF16)

    return pl.pallas_call(
        body, grid=(n_tok // tl,), name="ssd_fwd",
        in_specs=[_tile(tl, D_XBC), _tile(tl, D_SSM), _tile(tl, LANES), _full((SUBLANES, D_XBC)), _full((1, D_XBC)),
                  _full((1, LANES)), _full((1, LANES)), _full((1, D_SSM)), _full((1, D_SSM)), _full((LANES, D_SSM))],
        out_specs=[_tile(tl, D_XBC), _tile(tl, D_SSM), _tile(tl, D_SSM),
                   pl.BlockSpec((nq, D_SSM, N_STATE), lambda i: (i, 0, 0))],
        out_shape=[jax.ShapeDtypeStruct((n_tok, D_XBC), F32), jax.ShapeDtypeStruct((n_tok, D_SSM), F32),
                   jax.ShapeDtypeStruct((n_tok, D_SSM), BF16), jax.ShapeDtypeStruct((n_tok // CHUNK, D_SSM, N_STATE), F32)],
        scratch_shapes=[pltpu.VMEM((tl + SUBLANES, D_XBC), F32), pltpu.VMEM((D_SSM, N_STATE), F32)],
        compiler_params=_params(),
    )(xbc_in, z, dtr, cw, cb, dtb, alog, dsk, gnorm, _head_lane_matrix())


def _ssd_bwd(dys, y, z, pre, xbc_in, dtr, hprev, cw, dtb, alog, dsk, gnorm, specs, grads_b, tl):
    n_tok = y.shape[0]
    n_t = n_tok // tl
    nq = tl // CHUNK
    n_g, n_r = len(grads_b), len(specs)

    def body(dys_ref, y_ref, z_ref, pre_ref, xin_ref, dtr_ref, hprev_ref, cw_ref, dtb_ref, alog_ref, dsk_ref, gn_ref,
             exp_ref, redx_ref, redq_ref, *rest):
        g_refs, rest = rest[0:n_g], rest[n_g:]
        dxin_ref, dz_ref, ddtr_ref, dcw_ref, dcb_ref, dgn_ref, ddsk_ref, da_ref, ddtb_ref = rest[0:9]
        recvs, rest = rest[9:9 + n_r], rest[9 + n_r:]
        dxs, dh, cs_s, dskc, shifts, dwp, sums_q, sums_s = rest[0:8]
        sems = rest[8:]
        i = pl.program_id(0)

        @pl.when(i == 0)
        def _():
            _direct_stage(0, specs, g_refs, recvs, sems)
            dcw_ref[...] = jnp.zeros_like(dcw_ref)
            dwp[...] = jnp.zeros_like(dwp)
            dcb_ref[...] = jnp.zeros_like(dcb_ref)
            dgn_ref[...] = jnp.zeros_like(dgn_ref)
            da_ref[...] = jnp.zeros_like(da_ref)
            ddtb_ref[...] = jnp.zeros_like(ddtb_ref)
            dskc[...] = jnp.zeros_like(dskc)
            dh[...] = jnp.zeros_like(dh)
            dxs[tl:tl + SUBLANES, :] = jnp.zeros((SUBLANES, D_XBC), F32)

        yv = y_ref[...]
        zv = z_ref[...]
        dysv = dys_ref[...]
        sz = _sig(zv)
        silz = zv * sz
        yz = yv * silz
        gw = D_SSM // N_GROUPS
        dyz_parts = []
        for g in range(N_GROUPS):
            sl = slice(g * gw, (g + 1) * gw)
            seg = yz[:, sl]
            r = lax.rsqrt(jnp.mean(seg * seg, axis=-1, keepdims=True) + RMS_EPS)
            yzn = seg * r
            dgn_ref[:, sl] += _colsum(dysv[:, sl] * yzn)
            dyzn = dysv[:, sl] * gn_ref[:, sl]
            dyz_parts.append(r * (dyzn - yzn * jnp.mean(dyzn * yzn, axis=-1, keepdims=True)))
        dyz = jnp.concatenate(dyz_parts, axis=1)
        dy = dyz * silz
        dz_ref[...] = (dyz * yv * _dsilu(zv, sz)).astype(BF16)

        prev = pre_ref[...]
        sp = _sig(prev)
        xbc = prev * sp
        dskc[...] += _colsum(dy * xbc[:, 0:D_SSM])
        dt_in = dtr_ref[...] + dtb_ref[...]
        dt = _softplus(dt_in)
        dsp = _sig(dt_in)
        a = -jnp.exp(alog_ref[...])
        adt = dt * a
        for q in reversed(range(nq)):
            r0 = q * CHUNK
            cm = _chunk_common(adt[r0:r0 + CHUNK, :])
            row, col, lo = cm["row"], cm["col"], cm["lo"]
            triu = (col >= row).astype(F32)
            dt_c = dt[r0:r0 + CHUNK, :]
            dt_x = _per_head_lanes(dt_c, exp_ref)
            e_x = _per_head_lanes(cm["e"], exp_ref)
            dec_x = _per_head_lanes(cm["dec"], exp_ref)
            dcd_row = jnp.zeros((1, LANES), F32)
            for g in range(N_GROUPS):
                bcol = D_SSM + g * N_STATE
                ccol = D_SSM + (N_GROUPS + g) * N_STATE
                bg = xbc[r0:r0 + CHUNK, bcol:bcol + N_STATE].astype(BF16)
                cg_ = xbc[r0:r0 + CHUNK, ccol:ccol + N_STATE].astype(BF16)
                gm = _dot_nt(cg_, bg)
                dgm = jnp.zeros((CHUNK, CHUNK), F32)
                dbg = jnp.zeros((CHUNK, N_STATE), F32)
                dcg = jnp.zeros((CHUNK, N_STATE), F32)
                for k in range(N_HEADS // N_GROUPS // 2):
                    ha = (N_HEADS // N_GROUPS) * g + 2 * k
                    hb = ha + 1
                    c0 = ha * HEAD_DIM
                    xh2 = xbc[r0:r0 + CHUNK, c0:c0 + LANES]
                    dt2 = dt_x[:, c0:c0 + LANES]
                    x2 = xh2 * dt2
                    x2b = x2.astype(BF16)
                    la = _decay_mask(cm, ha)
                    lb = _decay_mask(cm, hb)
                    ma = gm * la
                    mb = gm * lb
                    dy2 = dy[r0:r0 + CHUNK, c0:c0 + LANES]
                    dy2b = dy2.astype(BF16)
                    dma = _dot_nt(jnp.where(lo, dy2, 0.0).astype(BF16), x2b)
                    dmb = _dot_nt(jnp.where(lo, 0.0, dy2).astype(BF16), x2b)
                    dx2 = jnp.where(lo, _dot(ma.T.astype(BF16), dy2b), _dot(mb.T.astype(BF16), dy2b))
                    dgm = dgm + dma * la + dmb * lb
                    sums_q[:, ha * CHUNK:(ha + 1) * CHUNK] = (dma * ma).astype(BF16)
                    sums_q[:, hb * CHUNK:(hb + 1) * CHUNK] = (dmb * mb).astype(BF16)
                    h2 = hprev_ref[q, c0:c0 + LANES, :]
                    h2b = h2.astype(BF16)
                    dz2 = dy2 * e_x[:, c0:c0 + LANES]
                    dcg = dcg + _dot(dz2.astype(BF16), h2b)
                    sums_s[0, :, c0:c0 + LANES] = (dz2 * _dot_nt(cg_, h2b)).astype(BF16)
                    dhn = dh[c0:c0 + LANES, :]
                    dhnb = dhn.astype(BF16)
                    cd_a = cm["cd"][:, ha:ha + 1]
                    cd_b = cm["cd"][:, hb:hb + 1]
                    top = row < HEAD_DIM
                    hh = dhn * h2
                    dcd_a = jnp.sum(_colsum(jnp.where(top, hh, 0.0)), axis=1, keepdims=True)
                    dcd_b = jnp.sum(_colsum(jnp.where(top, 0.0, hh)), axis=1, keepdims=True)
                    dcd_row = dcd_row + jnp.where(col[0:1, :] == ha, dcd_a * cd_a, 0.0) + jnp.where(col[0:1, :] == hb, dcd_b * cd_b, 0.0)
                    dh[c0:c0 + LANES, :] = jnp.where(top, cd_a, cd_b) * dhn + _dot(dz2.T.astype(BF16), cg_)
                    w2 = _dot_nt(bg, dhnb)
                    dec2 = dec_x[:, c0:c0 + LANES]
                    dx2 = dx2 + dec2 * w2
                    sums_s[1, :, c0:c0 + LANES] = (x2 * w2).astype(BF16)
                    dbg = dbg + _dot((x2 * dec2).astype(BF16), dhnb)
                    sums_s[2, :, c0:c0 + LANES] = (dx2 * xh2).astype(BF16)
                    dxs[r0:r0 + CHUNK, c0:c0 + LANES] = dx2 * dt2 + dsk_ref[:, c0:c0 + LANES] * dy2
                dgmb = dgm.astype(BF16)
                dxs[r0:r0 + CHUNK, bcol:bcol + N_STATE] = dbg + _dot(dgm.T.astype(BF16), cg_)
                dxs[r0:r0 + CHUNK, ccol:ccol + N_STATE] = dcg + _dot(dgmb, bg)
            q_all = sums_q[...]
            q_cols = _dot(jnp.ones((SUBLANES, CHUNK), BF16), q_all)
            cs_s[...] = jnp.zeros_like(cs_s)
            for h in range(N_HEADS):
                cs_s[h:h + 1, :] = q_cols[0:1, h * CHUNK:(h + 1) * CHUNK]
            de = _dot(sums_s[0], redx_ref[...])
            dd = _dot(sums_s[1], redx_ref[...]) * cm["dec"]
            ddtx = _dot(sums_s[2], redx_ref[...])
            is_last = row == CHUNK - 1
            dacs = _dot(q_all, redq_ref[...]) - cs_s[...].T + de - dd + jnp.where(is_last, dcd_row + _colsum(dd), 0.0)
            dadt = jnp.dot(triu, dacs, precision=HIGHEST, preferred_element_type=F32)
            da_ref[...] += _colsum(dadt * dt_c)
            ddtr_c = (dadt * a + ddtx) * dsp[r0:r0 + CHUNK, :]
            ddtr_ref[r0:r0 + CHUNK, :] = ddtr_c.astype(BF16)
            ddtb_ref[...] += _colsum(ddtr_c)

        dpre = dxs[0:tl, :] * _dsilu(prev, sp)
        dxs[0:tl, :] = dpre
        dcb_ref[...] += _colsum(dpre)
        _shift_copies(dxs, shifts, tl, SSM_BWD_SHIFTS)

        def strip(rb, carry):
            i0 = pl.multiple_of(rb * CONV_RS, CONV_RS)
            for c0 in range(0, D_XBC, CONV_CS):
                xin_s = xin_ref[pl.ds(i0, CONV_RS), c0:c0 + CONV_CS]
                acc = jnp.zeros((CONV_RS, CONV_CS), F32)
                for k in range(SSM_K):
                    sh = _tap_rows(dxs, shifts, SSM_K - 1 - k, i0, c0, SSM_BWD_SHIFTS)
                    acc = acc + sh * cw_ref[k:k + 1, c0:c0 + CONV_CS]
                    t = xin_s * sh
                    dwp[k * SUBLANES:(k + 1) * SUBLANES, c0:c0 + CONV_CS] += _fold_rows(t)
                dxin_ref[pl.ds(i0, CONV_RS), c0:c0 + CONV_CS] = acc.astype(BF16)
            return carry

        lax.fori_loop(0, tl // CONV_RS, strip, 0)
        dxs[tl:tl + SUBLANES, :] = dxs[0:SUBLANES, :]

        @pl.when(i == n_t - 1)
        def _():
            for k in range(SSM_K):
                dcw_ref[k:k + 1, :] = _colsum(dwp[k * SUBLANES:(k + 1) * SUBLANES, :])
            da_ref[...] = da_ref[...] * a
            sel = (lax.broadcasted_iota(jnp.int32, (D_SSM, LANES), 0) // HEAD_DIM
                   == lax.broadcasted_iota(jnp.int32, (D_SSM, LANES), 1)).astype(F32)
            rows = jnp.broadcast_to(dskc[...], (SUBLANES, D_SSM))
            ddsk_ref[...] = jnp.dot(rows, sel, precision=HIGHEST, preferred_element_type=F32)[0:1, :]
            _direct_stage(1, specs, g_refs, recvs, sems)

    head_lanes = _head_lane_matrix()
    per_head = (jnp.arange(N_HEADS * CHUNK)[:, None] // CHUNK == jnp.arange(LANES)[None, :]).astype(BF16)
    rev = functools.partial(_tile, tl, rev_of=n_t)
    return pl.pallas_call(
        body, grid=(n_t,), name="ssd_bwd",
        in_specs=[rev(D_SSM), rev(D_SSM), rev(D_SSM), rev(D_XBC), rev(D_XBC), rev(LANES),
                  pl.BlockSpec((nq, D_SSM, N_STATE), lambda i: (n_t - 1 - i, 0, 0)),
                  _full((SUBLANES, D_XBC)), _full((1, LANES)), _full((1, LANES)), _full((1, D_SSM)), _full((1, D_SSM)),
                  _full((LANES, D_SSM)), _full((D_SSM, LANES)), _full((N_HEADS * CHUNK, LANES))] + [IN_VMEM] * n_g,
        out_specs=[rev(D_XBC), rev(D_SSM), rev(LANES), _full((SUBLANES, D_XBC)), _full((1, D_XBC)), _full((1, D_SSM)),
                   _full((1, LANES)), _full((1, LANES)), _full((1, LANES))] + [ANY] * n_r,
        out_shape=[jax.ShapeDtypeStruct((n_tok, D_XBC), BF16), jax.ShapeDtypeStruct((n_tok, D_SSM), BF16),
                   jax.ShapeDtypeStruct((n_tok, LANES), BF16), jax.ShapeDtypeStruct((SUBLANES, D_XBC), F32),
                   jax.ShapeDtypeStruct((1, D_XBC), F32), jax.ShapeDtypeStruct((1, D_SSM), F32),
                   jax.ShapeDtypeStruct((1, LANES), F32), jax.ShapeDtypeStruct((1, LANES), F32),
                   jax.ShapeDtypeStruct((1, LANES), F32)]
        + [jax.ShapeDtypeStruct((N_DEV, rows, cols), BF16) for rows, cols, _, _ in specs],
        scratch_shapes=[pltpu.VMEM((tl + SUBLANES, D_XBC), F32), pltpu.VMEM((D_SSM, N_STATE), F32),
                        pltpu.VMEM((CHUNK, LANES), F32), pltpu.VMEM((1, D_SSM), F32),
                        pltpu.VMEM((len(SSM_BWD_SHIFTS), tl, D_XBC), F32), pltpu.VMEM((SSM_K * SUBLANES, D_XBC), F32),
                        pltpu.VMEM((CHUNK, N_HEADS * CHUNK), BF16), pltpu.VMEM((3, CHUNK, D_SSM), BF16)]
        + _direct_sems(specs),
        compiler_params=_params(),
    )(dys, y, z, pre, xbc_in, dtr, hprev, cw, dtb, alog, dsk, gnorm, head_lanes, head_lanes.T, per_head, *grads_b)


CONF_HALO = 32
CONV_RS = 32
CONV_CS = 256


ALL_SHIFTS = tuple(range(1, SUBLANES))


def _shift_copies(buf, shifts, n_rows, residues=ALL_SHIFTS):
    for j, r in enumerate(residues):
        shifts[j, 0:n_rows, :] = buf[r:r + n_rows, :]


def _fold_rows(t):
    part = t[0:SUBLANES]
    for j in range(1, t.shape[0] // SUBLANES):
        part = part + t[j * SUBLANES:(j + 1) * SUBLANES]
    return part


def _tap_rows(buf, shifts, off, i0, c0, residues=ALL_SHIFTS):
    q, r = divmod(off, SUBLANES)
    rows = pl.ds(pl.multiple_of(i0 + SUBLANES * q, SUBLANES), CONV_RS)
    if r == 0:
        return buf[rows, c0:c0 + CONV_CS]
    return shifts[residues.index(r), rows, c0:c0 + CONV_CS]


def _conf_fwd(glu, cgate, bglu, cw, cb, lg, lb, tl):
    n_tok = glu.shape[0]

    def body(glu_ref, cg_ref, bglu_ref, cw_ref, cb_ref, lg_ref, lb_ref, u0_ref, u1_ref, yc_ref, buf, shifts):
        @pl.when(pl.program_id(0) == 0)
        def _():
            buf[0:CONF_HALO, :] = jnp.zeros((CONF_HALO, D_CONF), F32)

        gl = glu_ref[...] + bglu_ref[...]
        u0 = gl[:, 0:D_CONF] * _sig(gl[:, D_CONF:2 * D_CONF])
        u0_ref[...] = u0
        buf[CONF_HALO:CONF_HALO + tl, :] = u0
        _shift_copies(buf, shifts, tl + CONF_HALO - SUBLANES)

        def strip(rb, carry):
            i0 = pl.multiple_of(rb * CONV_RS, CONV_RS)
            for c0 in range(0, D_CONF, CONV_CS):
                acc = jnp.broadcast_to(cb_ref[:, c0:c0 + CONV_CS], (CONV_RS, CONV_CS))
                for k in range(CONF_K):
                    acc = acc + _tap_rows(buf, shifts, CONF_HALO - (CONF_K - 1) + k, i0, c0) * cw_ref[k:k + 1, c0:c0 + CONV_CS]
                u1_ref[pl.ds(i0, CONV_RS), c0:c0 + CONV_CS] = acc
            return carry

        lax.fori_loop(0, tl // CONV_RS, strip, 0)
        buf[0:CONF_HALO, :] = buf[tl:tl + CONF_HALO, :]
        xhat, _ = _ln_stats(u1_ref[...])
        n = xhat * lg_ref[...] + lb_ref[...]
        cgv = cg_ref[...]
        yc_ref[...] = (n * _sig(n) * (cgv * _sig(cgv))).astype(BF16)

    return pl.pallas_call(
        body, grid=(n_tok // tl,), name="conf_fwd",
        in_specs=[_tile(tl, 2 * D_CONF), _tile(tl, D_CONF), _full((1, 2 * D_CONF)), _full((CONF_HALO, D_CONF)),
                  _full((1, D_CONF)), _full((1, D_CONF)), _full((1, D_CONF))],
        out_specs=[_tile(tl, D_CONF)] * 3,
        out_shape=[jax.ShapeDtypeStruct((n_tok, D_CONF), F32), jax.ShapeDtypeStruct((n_tok, D_CONF), F32),
                   jax.ShapeDtypeStruct((n_tok, D_CONF), BF16)],
        scratch_shapes=[pltpu.VMEM((tl + CONF_HALO, D_CONF), F32),
                        pltpu.VMEM((SUBLANES - 1, tl + CONF_HALO - SUBLANES, D_CONF), F32)],
        compiler_params=_params(),
    )(glu, cgate, bglu, cw, cb, lg, lb)


def _conf_bwd(dyc, dyc_block, u0, u1, glu, cgate, bglu, cw, lg, lb, specs, grads_b, tl):
    n_tok = glu.shape[0]
    n_t = n_tok // tl
    n_g, n_r = len(grads_b), len(specs)

    def body(dyc_ref, u0_ref, u1_ref, glu_ref, cg_ref, bglu_ref, cw_ref, lg_ref, lb_ref, *rest):
        g_refs, rest = rest[0:n_g], rest[n_g:]
        dglu_ref, dcg_ref, dcw_ref, dbglu_ref, small_ref = rest[0:5]
        recvs, rest = rest[5:5 + n_r], rest[5 + n_r:]
        buf, shifts, du0_s, dwp = rest[0:4]
        sems = rest[4:]

        @pl.when(pl.program_id(0) == 0)
        def _():
            _direct_stage(0, specs, g_refs, recvs, sems)
            dwp[...] = jnp.zeros_like(dwp)
            dbglu_ref[...] = jnp.zeros_like(dbglu_ref)
            small_ref[...] = jnp.zeros_like(small_ref)
            buf[tl:tl + CONF_HALO, :] = jnp.zeros((CONF_HALO, D_CONF), F32)

        xhat, rstd = _ln_stats(u1_ref[...])
        n = xhat * lg_ref[...] + lb_ref[...]
        sn = _sig(n)
        cgv = cg_ref[...]
        scg = _sig(cgv)
        dycv = dyc_ref[...]
        dcg_ref[...] = (dycv * (n * sn) * _dsilu(cgv, scg)).astype(BF16)
        dn = dycv * (cgv * scg) * _dsilu(n, sn)
        small_ref[0:1, :] += _colsum(dn * xhat)
        small_ref[1:2, :] += _colsum(dn)
        du1 = _ln_bwd(dn * lg_ref[...], xhat, rstd)
        small_ref[2:3, :] += _colsum(du1)
        buf[0:tl, :] = du1
        _shift_copies(buf, shifts, tl + CONF_HALO - SUBLANES)

        def strip(rb, carry):
            i0 = pl.multiple_of(rb * CONV_RS, CONV_RS)
            for c0 in range(0, D_CONF, CONV_CS):
                u0s = u0_ref[pl.ds(i0, CONV_RS), c0:c0 + CONV_CS]
                acc = jnp.zeros((CONV_RS, CONV_CS), F32)
                for k in range(CONF_K):
                    sh = _tap_rows(buf, shifts, CONF_K - 1 - k, i0, c0)
                    acc = acc + sh * cw_ref[k:k + 1, c0:c0 + CONV_CS]
                    t = u0s * sh
                    dwp[k * SUBLANES:(k + 1) * SUBLANES, c0:c0 + CONV_CS] += _fold_rows(t)
                du0_s[pl.ds(i0, CONV_RS), c0:c0 + CONV_CS] = acc
            return carry

        lax.fori_loop(0, tl // CONV_RS, strip, 0)
        du0 = du0_s[...]
        buf[tl:tl + CONF_HALO, :] = buf[0:CONF_HALO, :]
        gl = glu_ref[...] + bglu_ref[...]
        sg = _sig(gl[:, D_CONF:2 * D_CONF])
        dgv = du0 * sg
        dgg = du0 * gl[:, 0:D_CONF] * sg * (1.0 - sg)
        dglu_ref[:, 0:D_CONF] = dgv.astype(BF16)
        dglu_ref[:, D_CONF:2 * D_CONF] = dgg.astype(BF16)
        dbglu_ref[:, 0:D_CONF] += _colsum(dgv)
        dbglu_ref[:, D_CONF:2 * D_CONF] += _colsum(dgg)

        @pl.when(pl.program_id(0) == n_t - 1)
        def _():
            for k in range(CONF_HALO):
                dcw_ref[k:k + 1, :] = _colsum(dwp[k * SUBLANES:(k + 1) * SUBLANES, :])
            _direct_stage(1, specs, g_refs, recvs, sems)

    rev = functools.partial(_tile, tl, rev_of=n_t)
    return pl.pallas_call(
        body, grid=(n_t,), name="conf_bwd",
        in_specs=[pl.BlockSpec((tl, D_CONF), lambda i: (n_t - 1 - i, dyc_block)),
                  rev(D_CONF), rev(D_CONF), rev(2 * D_CONF), rev(D_CONF), _full((1, 2 * D_CONF)),
                  _full((CONF_HALO, D_CONF)), _full((1, D_CONF)), _full((1, D_CONF))] + [IN_VMEM] * n_g,
        out_specs=[rev(2 * D_CONF), rev(D_CONF), _full((CONF_HALO, D_CONF)), _full((1, 2 * D_CONF)), _full((SUBLANES, D_CONF))]
        + [ANY] * n_r,
        out_shape=[jax.ShapeDtypeStruct((n_tok, 2 * D_CONF), BF16), jax.ShapeDtypeStruct((n_tok, D_CONF), BF16),
                   jax.ShapeDtypeStruct((CONF_HALO, D_CONF), F32), jax.ShapeDtypeStruct((1, 2 * D_CONF), F32),
                   jax.ShapeDtypeStruct((SUBLANES, D_CONF), F32)]
        + [jax.ShapeDtypeStruct((N_DEV, rows, cols), BF16) for rows, cols, _, _ in specs],
        scratch_shapes=[pltpu.VMEM((tl + CONF_HALO, D_CONF), F32),
                        pltpu.VMEM((SUBLANES - 1, tl + CONF_HALO - SUBLANES, D_CONF), F32),
                        pltpu.VMEM((tl, D_CONF), F32), pltpu.VMEM((CONF_HALO * SUBLANES, D_CONF), F32)] + _direct_sems(specs),
        compiler_params=_params(),
    )(dyc, u0, u1, glu, cgate, bglu, cw, lg, lb, *grads_b)


def _tail(x, yssm, yconf, p, tgt, vec, w_out, wpg, wpp, tl):
    n_tok = x.shape[0]

    def body(x_ref, ys_ref, yc_ref, p_ref, t_ref, vec_ref, wo_ref, wg_ref, wp_ref,
             dmix_ref, dr1_ref, dr1b_ref, h1b_ref, dgb_ref, dpb_ref, small_ref, loss_ref):
        @pl.when(pl.program_id(0) == 0)
        def _():
            small_ref[...] = jnp.zeros_like(small_ref)
            loss_ref[...] = jnp.zeros_like(loss_ref)

        xh0, _ = _ln_stats(x_ref[...])
        h0 = xh0 * vec_ref[0:1, :] + vec_ref[1:2, :]
        out = _dot(ys_ref[...], wo_ref[0:D_SSM, :]) + _dot(yc_ref[...], wo_ref[D_SSM:D_SSM + D_CONF, :]) + vec_ref[2:3, :]
        xh1, rstd1 = _ln_stats(ALPHA * h0 + out)
        h1 = xh1 * vec_ref[3:4, :] + vec_ref[4:5, :]
        h1b = h1.astype(BF16)
        h1b_ref[...] = h1b
        gate = _sig(_dot(h1b, wg_ref[...]))
        ple = _dot(p_ref[...].astype(BF16), wp_ref[...])
        xh2, rstd2 = _ln_stats(ALPHA * h1 + gate * ple)
        h2 = xh2 * vec_ref[5:6, :] + vec_ref[6:7, :]
        diff = h2 - t_ref[...]
        part = jnp.sum(jnp.sum(diff * diff, axis=1, keepdims=True), axis=0, keepdims=True) * (0.5 / D_MODEL)
        loss_ref[...] += jnp.broadcast_to(part, loss_ref.shape)
        dh2 = diff * (1.0 / D_MODEL)
        small_ref[3:4, :] += _colsum(dh2 * xh2)
        small_ref[4:5, :] += _colsum(dh2)
        dr2 = _ln_bwd(dh2 * vec_ref[5:6, :], xh2, rstd2)
        dgpre = (dr2 * ple * gate * (1.0 - gate)).astype(BF16)
        dgb_ref[...] = dgpre
        dpb_ref[...] = (dr2 * gate).astype(BF16)
        dh1 = ALPHA * dr2 + _dot_nt(dgpre, wg_ref[...])
        small_ref[1:2, :] += _colsum(dh1 * xh1)
        small_ref[2:3, :] += _colsum(dh1)
        dr1 = _ln_bwd(dh1 * vec_ref[3:4, :], xh1, rstd1)
        small_ref[0:1, :] += _colsum(dr1)
        dr1_ref[...] = dr1
        dr1b = dr1.astype(BF16)
        dr1b_ref[...] = dr1b
        dmix_ref[...] = _dot_nt(dr1b, wo_ref[...])

    d_mix = D_SSM + D_CONF
    return pl.pallas_call(
        body, grid=(n_tok // tl,), name="tail",
        in_specs=[_tile(tl, D_MODEL), _tile(tl, D_SSM), _tile(tl, D_CONF), _tile(tl, D_PLE), _tile(tl, D_MODEL),
                  _full((SUBLANES, D_MODEL)), _full((d_mix, D_MODEL), True), _full((D_MODEL, D_MODEL), True),
                  _full((D_PLE, D_MODEL), True)],
        out_specs=[_tile(tl, d_mix), _tile(tl, D_MODEL), _tile(tl, D_MODEL), _tile(tl, D_MODEL), _tile(tl, D_MODEL),
                   _tile(tl, D_MODEL), _full((SUBLANES, D_MODEL)), _full((SUBLANES, LANES))],
        out_shape=[jax.ShapeDtypeStruct((n_tok, d_mix), F32), jax.ShapeDtypeStruct((n_tok, D_MODEL), F32),
                   jax.ShapeDtypeStruct((n_tok, D_MODEL), BF16), jax.ShapeDtypeStruct((n_tok, D_MODEL), BF16),
                   jax.ShapeDtypeStruct((n_tok, D_MODEL), BF16), jax.ShapeDtypeStruct((n_tok, D_MODEL), BF16),
                   jax.ShapeDtypeStruct((SUBLANES, D_MODEL), F32), jax.ShapeDtypeStruct((SUBLANES, LANES), F32)],
        compiler_params=_params(),
    )(x, yssm, yconf, p, tgt, vec, w_out, wpg, wpp)


def _inproj_bwd(dxin, dz, dglu, dcg, ddtr, dr1, x, g, b, w_r, tl):
    n_tok = x.shape[0]

    def body(dxin_ref, dz_ref, dglu_ref, dcg_ref, ddtr_ref, dr1_ref, x_ref, g_ref, b_ref, w_ref,
             dx_ref, dpb_ref, small_ref):
        @pl.when(pl.program_id(0) == 0)
        def _():
            small_ref[...] = jnp.zeros_like(small_ref)

        dh0 = ALPHA * dr1_ref[...]
        for ref, lo, hi in ((dxin_ref, R_XBC, R_Z), (dz_ref, R_Z, R_GLU), (dglu_ref, R_GLU, R_CG), (dcg_ref, R_CG, R_DT),
                            (ddtr_ref, R_DT, D_INR)):
            piece = ref[...].astype(BF16)
            dpb_ref[:, lo:hi] = piece
            dh0 = dh0 + _dot_nt(piece, w_ref[:, lo:hi])
        xhat, rstd = _ln_stats(x_ref[...])
        small_ref[0:1, :] += _colsum(dh0 * xhat)
        small_ref[1:2, :] += _colsum(dh0)
        dx_ref[...] = _ln_bwd(dh0 * g_ref[...], xhat, rstd)

    return pl.pallas_call(
        body, grid=(n_tok // tl,), name="inproj_bwd",
        in_specs=[_tile(tl, D_XBC), _tile(tl, D_SSM), _tile(tl, 2 * D_CONF), _tile(tl, D_CONF), _tile(tl, LANES),
                  _tile(tl, D_MODEL), _tile(tl, D_MODEL), _full((1, D_MODEL)), _full((1, D_MODEL)),
                  _full((D_MODEL, D_INR), True)],
        out_specs=[_tile(tl, D_MODEL), _tile(tl, D_INR), _full((SUBLANES, D_MODEL))],
        out_shape=[jax.ShapeDtypeStruct((n_tok, D_MODEL), F32), jax.ShapeDtypeStruct((n_tok, D_INR), BF16),
                   jax.ShapeDtypeStruct((SUBLANES, D_MODEL), F32)],
        compiler_params=_params(),
    )(dxin, dz, dglu, dcg, ddtr, dr1, x, g, b, w_r)


def _tn_matmul(a, b, name, tn, tl, out_dtype=F32):
    n_tok, m = a.shape
    n = b.shape[1]
    n_l = n_tok // tl
    direct = out_dtype == F32

    def body(a_ref, b_ref, o_ref, *scratch):
        acc = o_ref if direct else scratch[0]

        @pl.when(pl.program_id(1) == 0)
        def _():
            acc[...] = jnp.zeros_like(acc)

        acc[...] += lax.dot_general(a_ref[...], b_ref[...], TN_DIMS, preferred_element_type=F32)
        if not direct:
            @pl.when(pl.program_id(1) == n_l - 1)
            def _():
                o_ref[...] = acc[...].astype(out_dtype)

    return pl.pallas_call(
        body, grid=(n // tn, n_l), name=name,
        in_specs=[pl.BlockSpec((tl, m), lambda j, l: (l, 0)), pl.BlockSpec((tl, tn), lambda j, l: (l, j))],
        out_specs=pl.BlockSpec((m, tn), lambda j, l: (0, j)),
        out_shape=jax.ShapeDtypeStruct((m, n), out_dtype),
        scratch_shapes=[] if direct else [pltpu.VMEM((m, tn), F32)],
        compiler_params=pltpu.CompilerParams(dimension_semantics=("parallel", "arbitrary"), vmem_limit_bytes=VMEM_LIMIT),
    )(a, b)


def _dw_in(a, b, spack, tn, tl):
    n_tok, m = a.shape
    n = b.shape[1]
    n_j, n_l = n // tn, n_tok // tl

    def body(a_ref, b_ref, s_ref, o_ref, small_all, send_sems, recv_sems, loc_sem):
        j, l = pl.program_id(0), pl.program_id(1)

        @pl.when(jnp.logical_and(j == 0, l == 0))
        def _():
            _gather8_stage(0, s_ref, small_all, send_sems, recv_sems, loc_sem)

        @pl.when(l == 0)
        def _():
            o_ref[...] = jnp.zeros_like(o_ref)

        o_ref[...] += lax.dot_general(a_ref[...], b_ref[...], TN_DIMS, preferred_element_type=F32)

        @pl.when(jnp.logical_and(j == n_j - 1, l == n_l - 1))
        def _():
            _gather8_stage(1, s_ref, small_all, send_sems, recv_sems, loc_sem)

    dma = pltpu.SemaphoreType.DMA
    return pl.pallas_call(
        body, grid=(n_j, n_l), name="dw_in",
        in_specs=[pl.BlockSpec((tl, m), lambda j, l: (l, 0)), pl.BlockSpec((tl, tn), lambda j, l: (l, j)), IN_VMEM],
        out_specs=[pl.BlockSpec((m, tn), lambda j, l: (0, j)), ANY],
        out_shape=[jax.ShapeDtypeStruct((m, n), F32), jax.ShapeDtypeStruct((N_DEV,) + spack.shape, F32)],
        scratch_shapes=[dma((N_DEV - 1,)), dma((N_DEV - 1,)), dma],
        compiler_params=pltpu.CompilerParams(dimension_semantics=("arbitrary", "arbitrary"), vmem_limit_bytes=VMEM_LIMIT),
    )(a, b, spack)


def _pad_rows(a, rows):
    return jnp.pad(a, ((0, rows - a.shape[0]), (0, 0)))


def _pad_lanes(a):
    return jnp.pad(a, ((0, 0), (0, LANES - a.shape[1])))


def _local_grads(x, p, tgt, w4, later, ssm_cw, conf_cw, sm):
    n_tok = x.shape[0]
    tl = min(256, n_tok)
    tlm = min(512, n_tok)
    row = lambda v: v.reshape(1, -1)
    g_e, b_e = row(sm["ln_emb_g"]), row(sm["ln_emb_b"])
    h0b, xbc_in, z, glu, cgate, dtr, w_out_all, wpg_all, wpp_all, w_r = _ln_inproj(x, g_e, b_e, w4, later, tl)
    w_out_b = w_out_all.reshape(D_SSM + D_CONF, D_MODEL)
    wpg_b = wpg_all.reshape(D_MODEL, D_MODEL)
    wpp_b = wpp_all.transpose(1, 0, 2).reshape(D_PLE, D_MODEL)

    cw4 = _pad_rows(ssm_cw, SUBLANES)
    dtb, alog = _pad_lanes(sm["dt_bias"]), _pad_lanes(sm["a_log"])
    dsk = jnp.repeat(sm["d_skip"], HEAD_DIM, axis=1)
    pre, y, yssm, hprev = _ssd_fwd(xbc_in, z, dtr, cw4, sm["ssm_conv_b"], dtb, alog, dsk, sm["ssm_norm_g"], tl)

    cw31 = _pad_rows(conf_cw, CONF_HALO)
    u0, u1, yconf = _conf_fwd(glu, cgate, sm["b_glu"], cw31, sm["conf_conv_b"], sm["conf_ln_g"], sm["conf_ln_b"], tl)

    vec = jnp.concatenate([g_e, b_e, sm["b_out"], sm["ln1_g"], sm["ln1_b"], sm["ln2_g"], sm["ln2_b"],
                           jnp.zeros((1, D_MODEL), F32)], axis=0)
    dmix, dr1, dr1b, h1b, dgb, dpb, small_t, loss = _tail(
        x, yssm, yconf, p, tgt, vec, w_out_b, wpg_b, wpp_b, tlm)

    dwo = [_tn_matmul(yssm, dr1b, "dw_out_ssm", D_MODEL, tlm, BF16), _tn_matmul(yconf, dr1b, "dw_out_conf", D_MODEL, tlm, BF16)]
    dwp_ = [_tn_matmul(h1b, dgb, "dw_ple_gate", D_MODEL, tlm, BF16),
            _tn_matmul(p.astype(BF16), dpb, "dw_ple_proj", D_MODEL, tlm, BF16)]
    dglu, dcg, dcw31, dbglu, small_c, dw_out = _conf_bwd(dmix, D_SSM // D_CONF, u0, u1, glu, cgate, sm["b_glu"], cw31,
                                                          sm["conf_ln_g"], sm["conf_ln_b"], [W_OUT_DIRECT], dwo, tl)
    dxin, dz, ddtr, dcw4, dcb4, dgn, ddsk, dalog, ddtb, dwpg, dwpp = _ssd_bwd(
        dmix, y, z, pre, xbc_in, dtr, hprev, cw4, dtb, alog, dsk, sm["ssm_norm_g"], [WPG_DIRECT, WPP_DIRECT], dwp_, tl)
    dx, dprojb, small_e = _inproj_bwd(dxin, dz, dglu, dcg, ddtr, dr1, x, g_e, b_e, w_r, tlm)
    spack = _pack_small(dict(
        ln_emb_g=small_e[0], ln_emb_b=small_e[1], ssm_conv_w=dcw4[0:SSM_K], ssm_conv_b=dcb4,
        dt_bias=ddtb[:, 0:N_HEADS], a_log=dalog[:, 0:N_HEADS], d_skip=ddsk[:, 0:N_HEADS], ssm_norm_g=dgn, b_glu=dbglu,
        conf_conv_w=dcw31[0:CONF_K], conf_conv_b=small_c[2:3], conf_ln_g=small_c[0:1], conf_ln_b=small_c[1:2],
        b_out=small_t[0:1], ln1_g=small_t[1:2], ln1_b=small_t[2:3], ln2_g=small_t[3:4], ln2_b=small_t[4:5]), loss[0, 0])
    dw_r, small_all = _dw_in(h0b, dprojb, spack, D_INR // 3, tlm)
    return dx, dict(w_in=dw_r, w_out=dw_out, w_ple_gate=dwpg, w_ple_proj=dwpp, small=small_all)


N_CHIPS = 4
N_DEV = 8
W_IN_SH = D_IN // N_CHIPS
BIG = (("w_in", D_MODEL, W_IN_SH), ("w_out", (D_SSM + D_CONF) // N_CHIPS, D_MODEL),
       ("w_ple_gate", D_MODEL // N_CHIPS, D_MODEL), ("w_ple_proj", D_PLE, D_MODEL // N_CHIPS))
SEGS = ((R_XBC, 0, D_SSM), (R_XBC + D_SSM, 2048, 256), (R_XBC + D_SSM + 256, 2304, 256), (R_Z, 1024, D_SSM),
        (R_GLU, 2576, 2 * D_CONF), (R_CG, 4624, D_CONF), (R_DT, 2560, N_HEADS))
ROWS_CW4 = 2
ROWS_CW31 = 8
ROWS_CONV = 16
SMALL_ROWS = 56
SMALL_LAYOUT = (("ln_emb_g", 0, 1), ("ln_emb_b", 1, 1), ("ssm_conv_b", 2, 2), ("dt_bias", 4, 1), ("a_log", 5, 1),
                ("d_skip", 6, 1), ("ssm_norm_g", 7, 1), ("b_glu", 8, 2), ("conf_conv_b", 10, 1), ("conf_ln_g", 11, 1),
                ("conf_ln_b", 12, 1), ("b_out", 13, 1), ("ln1_g", 14, 1), ("ln1_b", 15, 1), ("ln2_g", 16, 1), ("ln2_b", 17, 1))
CONV_LAYOUT = (("ssm_conv_w", 18, 6, (SSM_K, D_XBC)), ("conf_conv_w", 24, 31, (CONF_K, D_CONF)))


def _rows_of(v, rows):
    flat = v.reshape(-1)
    return jnp.pad(flat, (0, rows * D_MODEL - flat.shape[0])).reshape(rows, D_MODEL)


LOSS_ROW = 55


def _pack_small(d, loss_share):
    parts = [_rows_of(d[n], r) for n, _, r in SMALL_LAYOUT] + [_rows_of(d[n], r) for n, _, r, _ in CONV_LAYOUT]
    parts.append(_rows_of(loss_share, SMALL_ROWS - LOSS_ROW))
    return jnp.concatenate(parts, axis=0)


def _gather8_stage(stage, s_ref, all_ref, send_sems, recv_sems, loc_sem):
    x, y, c = _my_place()
    me = 4 * x + 2 * y + c
    copies = [pltpu.make_async_copy(s_ref, all_ref.at[me], loc_sem)]
    for mk in range(1, N_DEV):
        peer = (x ^ (mk >> 2), y ^ ((mk >> 1) & 1), c ^ (mk & 1))
        copies.append(_remote(s_ref, all_ref.at[me], send_sems.at[mk - 1], recv_sems.at[mk - 1], peer))
    for cp in copies:
        cp.start() if stage == 0 else cp.wait()


def _row_chunks(rows, n):
    return [(j * (rows // n), rows // n) for j in range(n)]


def _my_place():
    return lax.axis_index("x"), lax.axis_index("y"), lax.axis_index("c")


MESH_ID = pl.DeviceIdType.MESH
ANY = pl.BlockSpec(memory_space=pl.ANY)
IN_VMEM = pl.BlockSpec(memory_space=pltpu.VMEM)
CHIP_FLIPS = ((1, 0), (0, 1), (1, 1))


def _remote(src, dst, send_sem, recv_sem, peer):
    return pltpu.make_async_remote_copy(src, dst, send_sem, recv_sem, device_id=peer, device_id_type=MESH_ID)


GATHER_CHUNKS = (4, 2, 1, 1)


def _gather_plans(kinds):
    plan = [(a, o, n, rows // 2) for a, (rows, ch) in enumerate(kinds) for o, n in _row_chunks(rows // 2, ch)]
    own_plan = [(a, o, n) for a, (rows, ch) in enumerate(kinds) for o, n in _row_chunks(rows, 2 * ch)]
    return plan, own_plan


def _gather_sems(plan, own_plan):
    hop = pltpu.SemaphoreType.DMA((3, len(plan)))
    return [hop, hop, hop, hop, pltpu.SemaphoreType.DMA((len(own_plan),))]


def _gather_stage(stage, ins, outs, plan, own_plan, sems):
    send_a, recv_a, send_b, recv_b, loc = sems
    x, y, c = _my_place()
    s = 2 * x + y
    sibling = (x, y, 1 - c)
    own = [pltpu.make_async_copy(ins[a].at[pl.ds(o, n)], outs[a].at[s, pl.ds(o, n)], loc.at[j])
           for j, (a, o, n) in enumerate(own_plan)]
    first, arrive, passed, arrive_b = [], [], [], []
    for k, (fx, fy) in enumerate(CHIP_FLIPS):
        peer = (x ^ fx, y ^ fy, c)
        sk = 2 * (x ^ fx) + (y ^ fy)
        for j, (a, o, n, h) in enumerate(plan):
            mine = pl.ds(pl.multiple_of(c * h + o, 16), n)
            theirs = pl.ds(pl.multiple_of((1 - c) * h + o, 16), n)
            first.append(_remote(ins[a].at[mine], outs[a].at[s, mine], send_a.at[k, j], recv_a.at[k, j], peer))
            land = outs[a].at[sk, mine]
            arrive.append(_remote(land, land, send_a.at[k, j], recv_a.at[k, j], peer))
            passed.append(_remote(land, land, send_b.at[k, j], recv_b.at[k, j], sibling))
            land_b = outs[a].at[sk, theirs]
            arrive_b.append(_remote(land_b, land_b, send_b.at[k, j], recv_b.at[k, j], sibling))
    if stage == 0:
        for cp in own + first:
            cp.start()
    elif stage == 1:
        for got, fwd in zip(arrive, passed):
            got.wait_recv()
            fwd.start()
    else:
        for got in arrive_b:
            got.wait_recv()
        for cp in first + passed:
            cp.wait_send()
        for cp in own:
            cp.wait()


W_OUT_DIRECT = (512, D_MODEL, 2, ((0, 0, 0), (0, 512, 0), (1, 0, 0), (1, 512, 0)))
WPG_DIRECT = (256, D_MODEL, 1, ((0, 0, 0), (0, 256, 0), (0, 512, 0), (0, 768, 0)))
WPP_DIRECT = (D_PLE, 256, 1, ((1, 0, 0), (1, 0, 256), (1, 0, 512), (1, 0, 768)))


def _direct_sems(specs):
    out = []
    for _, _, copies, _ in specs:
        out += [pltpu.SemaphoreType.DMA((N_CHIPS, 2, copies)), pltpu.SemaphoreType.DMA((N_DEV, copies)),
                pltpu.SemaphoreType.DMA((copies,))]
    return out


def _direct_stage(stage, specs, srcs, recvs, sems):
    x, y, c = _my_place()
    me = 4 * x + 2 * y + c
    s = 2 * x + y
    for r, (rows, cols, copies, where) in enumerate(specs):
        send_sems, recv_sems, loc_sems = sems[3 * r:3 * r + 3]
        for k, (o, n) in enumerate(_row_chunks(rows, copies)):
            for t, (si, row0, col0) in enumerate(where):
                src = srcs[si].at[pl.ds(row0 + o, n), pl.ds(col0, cols)]
                dst = recvs[r].at[me, pl.ds(o, n)]
                for cc in range(2):
                    to_self = jnp.logical_and(s == t, c == cc)
                    away = _remote(src, dst, send_sems.at[t, cc, k], recv_sems.at[me, k], (t // 2, t % 2, cc))
                    here = pltpu.make_async_copy(src, dst, loc_sems.at[k])

                    @pl.when(to_self)
                    def _():
                        here.start() if stage == 0 else here.wait()

                    @pl.when(jnp.logical_not(to_self))
                    def _():
                        away.start() if stage == 0 else away.wait_send()
            if stage == 1:
                for j in range(N_DEV):
                    land = recvs[r].at[j, pl.ds(o, n)]

                    @pl.when(j != me)
                    def _():
                        _remote(land, land, send_sems.at[0, 0, k], recv_sems.at[j, k], (0, 0, 0)).wait_recv()


def _gather_weights(w_in_b, conv_f):
    plan, own_plan = _gather_plans([(BIG[0][1], GATHER_CHUNKS[0])])

    def body(w_ref, conv_ref, w_all, conv_all, *sems):
        conv_send, conv_recv, conv_loc = sems[5:]
        x, y, c = _my_place()
        s = 2 * x + y
        conv_own = pltpu.make_async_copy(conv_ref, conv_all.at[s], conv_loc)
        conv_out = [_remote(conv_ref, conv_all.at[s], conv_send.at[k], conv_recv.at[k], (x ^ fx, y ^ fy, c))
                    for k, (fx, fy) in enumerate(CHIP_FLIPS)]
        _gather_stage(0, [w_ref], [w_all], plan, own_plan, sems[0:5])
        for cp in [conv_own] + conv_out:
            cp.start()
        _gather_stage(1, [w_ref], [w_all], plan, own_plan, sems[0:5])
        _gather_stage(2, [w_ref], [w_all], plan, own_plan, sems[0:5])
        for cp in conv_out:
            cp.wait()
        conv_own.wait()

    arrays = [w_in_b, conv_f]
    return pl.pallas_call(
        body, name="gather_weights", in_specs=[IN_VMEM] * 2, out_specs=[ANY] * 2,
        out_shape=[jax.ShapeDtypeStruct((N_CHIPS,) + a.shape, a.dtype) for a in arrays],
        scratch_shapes=_gather_sems(plan, own_plan)
        + [pltpu.SemaphoreType.DMA((3,)), pltpu.SemaphoreType.DMA((3,)), pltpu.SemaphoreType.DMA],
    )(*arrays)


SHARE_CHUNKS = 4


OWN_TILES = (((0, 1024), (1536, 512)), ((1024, 512), (1920, 640), (2560, 256), (5632, 128)), ((2688, 1536),), ((4096, 1536),))
OWN_W = 1536
RED_CHUNKS = 8


def _own_columns(t, rows):
    if t == 0:
        return rows[:, 0:W_IN_SH]
    if t == 1:
        return jnp.concatenate([rows[:, 516:1152], rows[:, 0:512], rows[:, 1408:1424], rows[:, 1152:1400]], axis=1)
    first = (2808 - 2688) if t == 2 else (4220 - 4096)
    return rows[:, first:first + W_IN_SH]


def _reduce_w_in(dw_r):
    half = D_MODEL // 2
    rc = half // RED_CHUNKS
    max_r = max(len(r) for r in OWN_TILES)

    def body(g_ref, got_ref, theirs, a_buf, b_buf, sum_buf, d2d_send, d2d_recv, ld_sems, ici_send, ici_recv, own_sems):
        x, y, c = _my_place()
        s = 2 * x + y
        sibling = (x, y, 1 - c)
        swap = [_remote(g_ref.at[pl.ds(pl.multiple_of((1 - c) * half + k * rc, SUBLANES), rc)], theirs.at[pl.ds(k * rc, rc)],
                        d2d_send.at[k], d2d_recv.at[k], sibling) for k in range(RED_CHUNKS)]
        for cp in swap:
            cp.start()

        def tiles(k, t, r):
            k0, wd = OWN_TILES[t][r]
            at = sum(w_ for _, w_ in OWN_TILES[t][:r])
            src = sum_buf.at[k, pl.ds(0, rc), pl.ds(k0, wd)]
            dst = got_ref.at[s, pl.ds(k * rc, rc), pl.ds(at, wd)]
            return (_remote(src, dst, ici_send.at[t, r, k], ici_recv.at[s, r, k], (t // 2, t % 2, c)),
                    pltpu.make_async_copy(src, dst, own_sems.at[r, k]))

        mine = [pltpu.make_async_copy(g_ref.at[pl.ds(pl.multiple_of(c * half + k * rc, SUBLANES), rc)], a_buf.at[k % 2],
                                      ld_sems.at[0, k % 2]) for k in range(RED_CHUNKS)]
        other = [pltpu.make_async_copy(theirs.at[pl.ds(k * rc, rc)], b_buf.at[k % 2], ld_sems.at[1, k % 2])
                 for k in range(RED_CHUNKS)]
        mine[0].start()
        for k in range(RED_CHUNKS):
            slot = k % 2
            swap[k].wait_recv()
            other[k].start()
            if k + 1 < RED_CHUNKS:
                mine[k + 1].start()
            mine[k].wait()
            other[k].wait()
            sum_buf[k] = (a_buf[slot] + b_buf[slot]).astype(BF16)
            for t in range(N_CHIPS):
                for r in range(len(OWN_TILES[t])):
                    away, here = tiles(k, t, r)

                    @pl.when(t != s)
                    def _():
                        away.start()

                    @pl.when(t == s)
                    def _():
                        here.start()

        for k in range(RED_CHUNKS):
            swap[k].wait_send()
            for t in range(N_CHIPS):
                for r, (_, wd) in enumerate(OWN_TILES[t]):
                    away, here = tiles(k, t, r)
                    at = sum(w_ for _, w_ in OWN_TILES[t][:r])

                    @pl.when(t != s)
                    def _():
                        away.wait_send()

                    @pl.when(t == s)
                    def _():
                        here.wait()
                        for j in range(N_CHIPS):
                            if j != t:
                                land = got_ref.at[j, pl.ds(k * rc, rc), pl.ds(at, wd)]
                                _remote(land, land, ici_send.at[0, 0, 0], ici_recv.at[j, r, k], (0, 0, 0)).wait_recv()

    dma = pltpu.SemaphoreType.DMA
    got, _ = pl.pallas_call(
        body, name="reduce_w_in", in_specs=[ANY], out_specs=[ANY, ANY],
        out_shape=[jax.ShapeDtypeStruct((N_CHIPS, half, OWN_W), BF16), jax.ShapeDtypeStruct((half, D_INR), F32)],
        scratch_shapes=[pltpu.VMEM((2, rc, D_INR), F32), pltpu.VMEM((2, rc, D_INR), F32), pltpu.VMEM((RED_CHUNKS, rc, D_INR), BF16),
                        dma((RED_CHUNKS,)), dma((RED_CHUNKS,)), dma((2, 2)), dma((N_CHIPS, max_r, RED_CHUNKS)),
                        dma((N_CHIPS, max_r, RED_CHUNKS)), dma((max_r, RED_CHUNKS))],
        compiler_params=pltpu.CompilerParams(vmem_limit_bytes=VMEM_LIMIT),
    )(dw_r)
    return got


def _add_slots(got):
    _, rows, cols = got.shape
    tr = 128

    def body(g_ref, o_ref):
        o_ref[...] = ((g_ref[0].astype(F32) + g_ref[1].astype(F32)) + g_ref[2].astype(F32)) + g_ref[3].astype(F32)

    return pl.pallas_call(body, grid=(rows // tr,), name="add_chips_w_in",
                          in_specs=[pl.BlockSpec((N_CHIPS, tr, cols), lambda i: (0, i, 0))],
                          out_specs=pl.BlockSpec((tr, cols), lambda i: (i, 0)),
                          out_shape=jax.ShapeDtypeStruct((rows, cols), F32), compiler_params=_params(seq=False))(got)


def _share_halves(tot):
    n_rows, half = tot.shape
    plan = _row_chunks(half, SHARE_CHUNKS)

    def body(t_ref, both, send_sems, recv_sems, loc):
        x, y, c = _my_place()
        rows = pl.ds(0, n_rows)

        def place(o, n):
            return both.at[rows, pl.ds(pl.multiple_of(c * half + o, LANES), n)]

        own = [pltpu.make_async_copy(t_ref.at[rows, pl.ds(o, n)], place(o, n), loc.at[j]) for j, (o, n) in enumerate(plan)]
        sends = [_remote(t_ref.at[rows, pl.ds(o, n)], place(o, n), send_sems.at[j], recv_sems.at[j], (x, y, 1 - c))
                 for j, (o, n) in enumerate(plan)]
        for cp in own + sends:
            cp.start()
        for cp in sends:
            cp.wait()
        for cp in own:
            cp.wait()

    dma = pltpu.SemaphoreType.DMA
    return pl.pallas_call(
        body, name="share_halves", in_specs=[IN_VMEM], out_specs=ANY,
        out_shape=jax.ShapeDtypeStruct((n_rows, 2 * half), F32),
        scratch_shapes=[dma((len(plan),)), dma((len(plan),)), dma((len(plan),))],
    )(tot)


def _adam_math(w, g, m, v):
    m = ADAM_B1 * m + (1.0 - ADAM_B1) * g
    v = ADAM_B2 * v + (1.0 - ADAM_B2) * (g * g)
    m_hat = m / (1.0 - ADAM_B1 ** ADAM_STEP)
    v_hat = v / (1.0 - ADAM_B2 ** ADAM_STEP)
    return -ADAM_LR * (m_hat / (jnp.sqrt(v_hat) + ADAM_EPS) + ADAM_WD * w), m, v


def _adam(w, g, m, v, name):
    rows, cols = w.shape

    def body(w_ref, g_ref, m_ref, v_ref, d_ref, nm_ref, nv_ref):
        d_ref[...], nm_ref[...], nv_ref[...] = _adam_math(w_ref[...], g_ref[...], m_ref[...], v_ref[...])

    if rows <= 256 or rows % 256 == 0:
        tr = min(rows, 256)
        n_blocks, spec = rows // tr, pl.BlockSpec((tr, cols), lambda i: (i, 0))
    else:
        n_blocks, spec = cols // 256, pl.BlockSpec((rows, 256), lambda i: (0, i))
    return pl.pallas_call(body, grid=(n_blocks,), name=name, in_specs=[spec] * 4, out_specs=[spec] * 3,
                          out_shape=[jax.ShapeDtypeStruct(w.shape, F32)] * 3, compiler_params=_params(seq=False))(w, g, m, v)


def _adam_sum(w, parts, m, v, name):
    rows, cols = w.shape
    tr = rows if rows <= 256 else 256

    def body(w_ref, p_ref, m_ref, v_ref, g_ref, d_ref, nm_ref, nv_ref):
        g = p_ref[0].astype(F32)
        for j in range(1, N_DEV):
            g = g + p_ref[j].astype(F32)
        g_ref[...] = g
        d_ref[...], nm_ref[...], nv_ref[...] = _adam_math(w_ref[...], g, m_ref[...], v_ref[...])

    spec = pl.BlockSpec((tr, cols), lambda i: (i, 0))
    return pl.pallas_call(
        body, grid=(rows // tr,), name=name,
        in_specs=[spec, pl.BlockSpec((N_DEV, tr, cols), lambda i: (0, i, 0)), spec, spec], out_specs=[spec] * 4,
        out_shape=[jax.ShapeDtypeStruct(w.shape, F32)] * 4, compiler_params=_params(seq=False))(w, parts, m, v)


def _adam_small(parts, ws, ms, vs):
    n_w = len(SMALL_LAYOUT)

    def body(p_ref, *refs):
        w_refs, m_refs, v_refs = refs[0:n_w], refs[n_w:2 * n_w], refs[2 * n_w:3 * n_w]
        sum_ref = refs[3 * n_w]
        outs = refs[3 * n_w + 1:]
        g = p_ref[0]
        for k in range(1, N_DEV):
            g = g + p_ref[k]
        sum_ref[...] = g
        for a, (_, r0, rows) in enumerate(SMALL_LAYOUT):
            width = w_refs[a].shape[1]
            for j in range(rows):
                lo, hi = j * D_MODEL, min((j + 1) * D_MODEL, width)
                gj = sum_ref[r0 + j:r0 + j + 1, 0:hi - lo]
                d, nm, nv = _adam_math(w_refs[a][:, lo:hi], gj, m_refs[a][:, lo:hi], v_refs[a][:, lo:hi])
                for out, val in zip(outs[4 * a:4 * a + 4], (gj, d, nm, nv)):
                    out[:, lo:hi] = val

    shapes = [jax.ShapeDtypeStruct((SMALL_ROWS, D_MODEL), F32)]
    for wa in ws:
        shapes += [jax.ShapeDtypeStruct(wa.shape, F32)] * 4
    res = pl.pallas_call(body, name="adam_small", out_shape=shapes)(parts, *ws, *ms, *vs)
    return res[0], [res[1 + 4 * a:5 + 4 * a] for a in range(n_w)]


def kernel(x, p, ln_emb_g, ln_emb_b, w_in, ssm_conv_w, ssm_conv_b, dt_bias, a_log, d_skip, ssm_norm_g, b_glu, conf_conv_w, conf_conv_b, conf_ln_g, conf_ln_b, w_out, b_out, ln1_g, ln1_b, w_ple_gate, w_ple_proj, ln2_g, ln2_b, loss_target, m_ln_emb_g, m_ln_emb_b, m_w_in, m_ssm_conv_w, m_ssm_conv_b, m_dt_bias, m_a_log, m_d_skip, m_ssm_norm_g, m_b_glu, m_conf_conv_w, m_conf_conv_b, m_conf_ln_g, m_conf_ln_b, m_w_out, m_b_out, m_ln1_g, m_ln1_b, m_w_ple_gate, m_w_ple_proj, m_ln2_g, m_ln2_b, v_ln_emb_g, v_ln_emb_b, v_w_in, v_ssm_conv_w, v_ssm_conv_b, v_dt_bias, v_a_log, v_d_skip, v_ssm_norm_g, v_b_glu, v_conf_conv_w, v_conf_conv_b, v_conf_ln_g, v_conf_ln_b, v_w_out, v_b_out, v_ln1_g, v_ln1_b, v_w_ple_gate, v_w_ple_proj, v_ln2_g, v_ln2_b):
    order = ("ln_emb_g", "ln_emb_b", "w_in", "ssm_conv_w", "ssm_conv_b", "dt_bias", "a_log", "d_skip", "ssm_norm_g", "b_glu",
             "conf_conv_w", "conf_conv_b", "conf_ln_g", "conf_ln_b", "w_out", "b_out", "ln1_g", "ln1_b", "w_ple_gate",
             "w_ple_proj", "ln2_g", "ln2_b")
    w = dict(zip(order, (ln_emb_g, ln_emb_b, w_in, ssm_conv_w, ssm_conv_b, dt_bias, a_log, d_skip, ssm_norm_g, b_glu,
                         conf_conv_w, conf_conv_b, conf_ln_g, conf_ln_b, w_out, b_out, ln1_g, ln1_b, w_ple_gate, w_ple_proj,
                         ln2_g, ln2_b)))
    m = dict(zip(order, (m_ln_emb_g, m_ln_emb_b, m_w_in, m_ssm_conv_w, m_ssm_conv_b, m_dt_bias, m_a_log, m_d_skip,
                         m_ssm_norm_g, m_b_glu, m_conf_conv_w, m_conf_conv_b, m_conf_ln_g, m_conf_ln_b, m_w_out, m_b_out,
                         m_ln1_g, m_ln1_b, m_w_ple_gate, m_w_ple_proj, m_ln2_g, m_ln2_b)))
    v = dict(zip(order, (v_ln_emb_g, v_ln_emb_b, v_w_in, v_ssm_conv_w, v_ssm_conv_b, v_dt_bias, v_a_log, v_d_skip,
                         v_ssm_norm_g, v_b_glu, v_conf_conv_w, v_conf_conv_b, v_conf_ln_g, v_conf_ln_b, v_w_out, v_b_out,
                         v_ln1_g, v_ln1_b, v_w_ple_gate, v_w_ple_proj, v_ln2_g, v_ln2_b)))

    conv_f = jnp.concatenate([_rows_of(w["ssm_conv_w"], ROWS_CW4), _rows_of(w["conf_conv_w"], ROWS_CW31),
                              jnp.zeros((ROWS_CONV - ROWS_CW4 - ROWS_CW31, D_MODEL), F32)], axis=0)
    shards_b = [w[n][0].astype(BF16) for n, _, _ in BIG]
    w_in_all, conv_all = _gather_weights(shards_b[0], conv_f)
    cw4 = conv_all[:, 0:ROWS_CW4].reshape(N_CHIPS, -1)[:, :SSM_K * D_XBC // N_CHIPS]
    cw4 = cw4.reshape(N_CHIPS, SSM_K, D_XBC // N_CHIPS).transpose(1, 0, 2).reshape(SSM_K, D_XBC)
    cw31 = conv_all[:, ROWS_CW4:ROWS_CW4 + ROWS_CW31].reshape(N_CHIPS, -1)[:, :CONF_K * D_CONF // N_CHIPS]
    cw31 = cw31.reshape(N_CHIPS, CONF_K, D_CONF // N_CHIPS).transpose(1, 0, 2).reshape(CONF_K, D_CONF)

    small_names = [n for n, _, _ in SMALL_LAYOUT]
    sm = {n: w[n] for n in small_names}
    dx, grads = _local_grads(x[0], p[0, 0], loss_target[0], w_in_all, shards_b[1:], cw4, cw31, sm)

    chip_i = 2 * lax.axis_index("x") + lax.axis_index("y")
    mine = lax.switch(chip_i, [functools.partial(_own_columns, t) for t in range(N_CHIPS)], _add_slots(_reduce_w_in(grads["w_in"])))
    g_w_in_t = _share_halves(jnp.swapaxes(mine, 0, 1))

    out_g, out_d, out_m, out_v = {}, {}, {}, {}
    as_row = lambda a: a.reshape(1, -1)
    g_s, small_out = _adam_small(grads["small"], [as_row(w[n]) for n in small_names], [as_row(m[n]) for n in small_names],
                                 [as_row(v[n]) for n in small_names])
    loss = g_s[LOSS_ROW, 0]
    tr_ = lambda a: jnp.swapaxes(a, 0, 1)
    n = "w_in"
    d, nm, nv = _adam(tr_(w[n][0]), g_w_in_t, tr_(m[n][0]), tr_(v[n][0]), "adam_" + n)
    out_g[n], out_d[n], out_m[n], out_v[n] = [tr_(a)[None] for a in (g_w_in_t, d, nm, nv)]
    for n, r0, r, shape in CONV_LAYOUT:
        whole = g_s[r0:r0 + r].reshape(-1)[:shape[0] * shape[1]].reshape(shape)
        g = lax.dynamic_slice_in_dim(whole, chip_i * (shape[1] // N_CHIPS), shape[1] // N_CHIPS, axis=1)
        d, nm, nv = _adam(w[n][0], g, m[n][0], v[n][0], "adam_" + n)
        out_g[n], out_d[n], out_m[n], out_v[n] = g[None], d[None], nm[None], nv[None]
    for n, _, _ in BIG[1:]:
        out_g[n], out_d[n], out_m[n], out_v[n] = [a[None] for a in _adam_sum(w[n][0], grads[n], m[n][0], v[n][0], "adam_" + n)]
    for n, four in zip(small_names, small_out):
        out_g[n], out_d[n], out_m[n], out_v[n] = [a.reshape(w[n].shape) for a in four]
    return (loss, dx[None], *[out_g[n] for n in order], *[out_d[n] for n in order], *[out_m[n] for n in order],
            *[out_v[n] for n in order])
```

```python
import functools

import jax
import jax.numpy as jnp
from jax import lax
from jax.experimental import pallas as pl
from jax.experimental.pallas import tpu as pltpu

F32 = jnp.float32
BF16 = jnp.bfloat16

D_MODEL = 1024
D_PLE = 256
D_SSM = 1024
D_CONF = 1024
N_HEADS = 16
HEAD_DIM = 64
N_GROUPS = 2
N_STATE = 128
CHUNK = 128
SSM_K = 4
CONF_K = 31
D_XBC = D_SSM + 2 * N_GROUPS * N_STATE
D_IN = 5648
R_XBC, R_Z, R_GLU, R_CG, R_DT, D_INR = 0, 1536, 2560, 4608, 5632, 5760
LN_EPS = 1e-5
RMS_EPS = 1e-5
ALPHA = 2.0 ** 0.25
ADAM_LR, ADAM_B1, ADAM_B2, ADAM_EPS, ADAM_WD, ADAM_STEP = 0.001, 0.9, 0.999, 1e-08, 0.01, 10
NEG_BIG = -1e30
LANES = 128
SUBLANES = 8
VMEM_LIMIT = 56 * 1024 * 1024
HIGHEST = lax.Precision.HIGHEST
NT_DIMS = (((1,), (1,)), ((), ()))
TN_DIMS = (((0,), (0,)), ((), ()))


def _sig(v):
    return jax.nn.sigmoid(v)


def _dsilu(v, s):
    return s * (1.0 + v * (1.0 - s))


def _ln_stats(v):
    mu = jnp.mean(v, axis=-1, keepdims=True)
    c = v - mu
    var = jnp.mean(c * c, axis=-1, keepdims=True)
    rstd = lax.rsqrt(var + LN_EPS)
    return c * rstd, rstd


def _ln_bwd(dxhat, xhat, rstd):
    m1 = jnp.mean(dxhat, axis=-1, keepdims=True)
    m2 = jnp.mean(dxhat * xhat, axis=-1, keepdims=True)
    return rstd * (dxhat - m1 - xhat * m2)


def _softplus(v):
    return jnp.maximum(v, 0.0) + jnp.log1p(jnp.exp(-jnp.abs(v)))


def _colsum(v):
    return jnp.sum(v, axis=0, keepdims=True)


def _dot(a, b):
    return jnp.dot(a, b, preferred_element_type=F32)


def _dot_nt(a, b):
    return lax.dot_general(a, b, NT_DIMS, preferred_element_type=F32)


def _tile(tl, c, rev_of=None):
    if rev_of is None:
        return pl.BlockSpec((tl, c), lambda i: (i, 0))
    return pl.BlockSpec((tl, c), lambda i: (rev_of - 1 - i, 0))


def _full(shape, single=False):
    nd = len(shape)
    if single:
        return pl.BlockSpec(shape, lambda i: (0,) * nd, pipeline_mode=pl.Buffered(1))
    return pl.BlockSpec(shape, lambda i: (0,) * nd)


def _params(seq=True):
    return pltpu.CompilerParams(dimension_semantics=("arbitrary",) if seq else ("parallel",), vmem_limit_bytes=VMEM_LIMIT)


def _ln_inproj(x, g, b, w4, later, tl):
    n_tok = x.shape[0]
    n_t = n_tok // tl
    n_l = len(later)
    plan, own_plan = _gather_plans([(rows, GATHER_CHUNKS[a]) for a, (_, rows, _) in enumerate(BIG)][1:])

    def body(x_ref, g_ref, b_ref, w4_ref, *rest):
        part_refs = rest[0:n_l]
        h0b_ref, xbc_ref, z_ref, glu_ref, cg_ref, dtr_ref = rest[n_l:n_l + 6]
        all_refs, w_out = rest[n_l + 6:2 * n_l + 6], rest[2 * n_l + 6]
        w_ref, w_sem = rest[2 * n_l + 7:2 * n_l + 9]
        sems = rest[2 * n_l + 9:]
        i = pl.program_id(0)
        keep = pltpu.make_async_copy(w_ref, w_out, w_sem)

        @pl.when(i == 0)
        def _():
            _gather_stage(0, part_refs, all_refs, plan, own_plan, sems)
            for kcol, o, wd in SEGS:
                lo, hi = o, o + wd
                while lo < hi:
                    sh = lo // W_IN_SH
                    e = min(hi, (sh + 1) * W_IN_SH)
                    w_ref[:, kcol + lo - o:kcol + e - o] = w4_ref[sh, :, lo - sh * W_IN_SH:e - sh * W_IN_SH]
                    lo = e
            w_ref[:, R_DT + N_HEADS:D_INR] = jnp.zeros((D_MODEL, D_INR - R_DT - N_HEADS), BF16)
            keep.start()

        xhat, _ = _ln_stats(x_ref[...])
        hb = (xhat * g_ref[...] + b_ref[...]).astype(BF16)
        h0b_ref[...] = hb
        xbc_ref[...] = _dot(hb, w_ref[:, R_XBC:R_Z])
        z_ref[...] = _dot(hb, w_ref[:, R_Z:R_GLU])
        glu_ref[...] = _dot(hb, w_ref[:, R_GLU:R_CG])
        cg_ref[...] = _dot(hb, w_ref[:, R_CG:R_DT])
        dtr_ref[...] = _dot(hb, w_ref[:, R_DT:D_INR])

        @pl.when(i == (3 * n_t) // 4)
        def _():
            _gather_stage(1, part_refs, all_refs, plan, own_plan, sems)

        @pl.when(i == n_t - 1)
        def _():
            _gather_stage(2, part_refs, all_refs, plan, own_plan, sems)
            keep.wait()

    widths = (D_MODEL, D_XBC, D_SSM, 2 * D_CONF, D_CONF, LANES)
    dtypes = (BF16, F32, F32, F32, F32, F32)
    return pl.pallas_call(
        body, grid=(n_t,), name="ln_inproj",
        in_specs=[_tile(tl, D_MODEL), _full((1, D_MODEL)), _full((1, D_MODEL)), IN_VMEM] + [IN_VMEM] * n_l,
        out_specs=[_tile(tl, w) for w in widths] + [ANY] * (n_l + 1),
        out_shape=[jax.ShapeDtypeStruct((n_tok, w), dt) for w, dt in zip(widths, dtypes)]
        + [jax.ShapeDtypeStruct((N_CHIPS,) + a.shape, a.dtype) for a in later]
        + [jax.ShapeDtypeStruct((D_MODEL, D_INR), BF16)],
        scratch_shapes=[pltpu.VMEM((D_MODEL, D_INR), BF16), pltpu.SemaphoreType.DMA] + _gather_sems(plan, own_plan),
        compiler_params=_params(),
    )(x, g, b, w4, *later)


def _chunk_common(adt_c):
    row = lax.broadcasted_iota(jnp.int32, (CHUNK, CHUNK), 0)
    col = lax.broadcasted_iota(jnp.int32, (CHUNK, CHUNK), 1)
    tril = row >= col
    acs = jnp.dot(tril.astype(F32), adt_c, precision=HIGHEST, preferred_element_type=F32)
    last = acs[CHUNK - 1:CHUNK, :]
    return dict(row=row, col=col, tril=tril, lo=col < HEAD_DIM, acs=acs, acs_t=acs.T, e=jnp.exp(acs),
                dec=jnp.exp(last - acs), cd=jnp.exp(last))


def _decay_mask(cm, h):
    return jnp.exp(jnp.where(cm["tril"], cm["acs"][:, h:h + 1] - cm["acs_t"][h:h + 1, :], NEG_BIG))


def _head_lane_matrix():
    return (jnp.arange(D_SSM)[None, :] // HEAD_DIM == jnp.arange(LANES)[:, None]).astype(BF16)


def _per_head_lanes(v, exp_ref):
    hi = v.astype(BF16)
    lo = (v - hi.astype(F32)).astype(BF16)
    return _dot(hi, exp_ref[...]) + _dot(lo, exp_ref[...])


def _split_store(ref, first, cols, t):
    hi = t.astype(BF16)
    ref[first, :, cols] = hi
    ref[first + 1, :, cols] = (t - hi.astype(F32)).astype(BF16)


SSM_BWD_SHIFTS = (1, 2, 3)


def _ssd_fwd(xbc_in, z, dtr, cw, cb, dtb, alog, dsk, gnorm, tl):
    n_tok = xbc_in.shape[0]
    nq = tl // CHUNK

    def body(xin_ref, z_ref, dtr_ref, cw_ref, cb_ref, dtb_ref, alog_ref, dsk_ref, gn_ref, exp_ref,
             pre_ref, y_ref, yssm_ref, hprev_ref, buf, hst):
        @pl.when(pl.program_id(0) == 0)
        def _():
            buf[0:SUBLANES, :] = jnp.zeros((SUBLANES, D_XBC), F32)
            hst[...] = jnp.zeros_like(hst)

        buf[SUBLANES:SUBLANES + tl, :] = xin_ref[...]
        pre = cb_ref[...] + jnp.zeros((tl, D_XBC), F32)
        for k in range(SSM_K):
            off = SUBLANES - (SSM_K - 1) + k
            pre = pre + buf[off:off + tl, :] * cw_ref[k:k + 1, :]
        buf[0:SUBLANES, :] = buf[tl:tl + SUBLANES, :]
        pre_ref[...] = pre
        xbc = pre * _sig(pre)
        dt = _softplus(dtr_ref[...] + dtb_ref[...])
        a = -jnp.exp(alog_ref[...])
        adt = dt * a
        for q in range(nq):
            r0 = q * CHUNK
            cm = _chunk_common(adt[r0:r0 + CHUNK, :])
            dt_x = _per_head_lanes(dt[r0:r0 + CHUNK, :], exp_ref)
            e_x = _per_head_lanes(cm["e"], exp_ref)
            dec_x = _per_head_lanes(cm["dec"], exp_ref)
            for g in range(N_GROUPS):
                bg = xbc[r0:r0 + CHUNK, D_SSM + g * N_STATE:D_SSM + (g + 1) * N_STATE].astype(BF16)
                cg_ = xbc[r0:r0 + CHUNK, D_SSM + (N_GROUPS + g) * N_STATE:D_SSM + (N_GROUPS + g + 1) * N_STATE].astype(BF16)
                gm = _dot_nt(cg_, bg)
                for k in range(N_HEADS // N_GROUPS // 2):
                    ha = (N_HEADS // N_GROUPS) * g + 2 * k
                    c0 = ha * HEAD_DIM
                    xh2 = xbc[r0:r0 + CHUNK, c0:c0 + LANES]
                    x2 = xh2 * dt_x[:, c0:c0 + LANES]
                    x2b = x2.astype(BF16)
                    ya = _dot((gm * _decay_mask(cm, ha)).astype(BF16), x2b)
                    yb = _dot((gm * _decay_mask(cm, ha + 1)).astype(BF16), x2b)
                    h2 = hst[c0:c0 + LANES, :]
                    hprev_ref[q, c0:c0 + LANES, :] = h2
                    z2 = _dot_nt(cg_, h2.astype(BF16))
                    y2 = jnp.where(cm["lo"], ya, yb) + z2 * e_x[:, c0:c0 + LANES] + dsk_ref[:, c0:c0 + LANES] * xh2
                    y_ref[r0:r0 + CHUNK, c0:c0 + LANES] = y2
                    s2 = _dot((x2 * dec_x[:, c0:c0 + LANES]).T.astype(BF16), bg)
                    cd2 = jnp.where(cm["row"] < HEAD_DIM, cm["cd"][:, ha:ha + 1], cm["cd"][:, ha + 1:ha + 2])
                    hst[c0:c0 + LANES, :] = cd2 * h2 + s2
        yv = y_ref[...]
        zv = z_ref[...]
        yz = yv * (zv * _sig(zv))
        gw = D_SSM // N_GROUPS
        for g in range(N_GROUPS):
            seg = yz[:, g * gw:(g + 1) * gw]
            r = lax.rsqrt(jnp.mean(seg * seg, axis=-1, keepdims=True) + RMS_EPS)
            yssm_ref[:, g * gw:(g + 1) * gw] = (seg * r * gn_ref[:, g * gw:(g + 1) * gw]).astype(BF16)

    return pl.pallas_call(
        body, grid=(n_tok // tl,), name="ssd_fwd",
        in_specs=[_tile(tl, D_XBC), _tile(tl, D_SSM), _tile(tl, LANES), _full((SUBLANES, D_XBC)), _full((1, D_XBC)),
                  _full((1, LANES)), _full((1, LANES)), _full((1, D_SSM)), _full((1, D_SSM)), _full((LANES, D_SSM))],
        out_specs=[_tile(tl, D_XBC), _tile(tl, D_SSM), _tile(tl, D_SSM),
                   pl.BlockSpec((nq, D_SSM, N_STATE), lambda i: (i, 0, 0))],
        out_shape=[jax.ShapeDtypeStruct((n_tok, D_XBC), F32), jax.ShapeDtypeStruct((n_tok, D_SSM), F32),
                   jax.ShapeDtypeStruct((n_tok, D_SSM), BF16), jax.ShapeDtypeStruct((n_tok // CHUNK, D_SSM, N_STATE), F32)],
        scratch_shapes=[pltpu.VMEM((tl + SUBLANES, D_XBC), F32), pltpu.VMEM((D_SSM, N_STATE), F32)],
        compiler_params=_params(),
    )(xbc_in, z, dtr, cw, cb, dtb, alog, dsk, gnorm, _head_lane_matrix())


def _ssd_bwd(dys, y, z, pre, xbc_in, dtr, hprev, cw, dtb, alog, dsk, gnorm, specs, grads_b, tl):
    n_tok = y.shape[0]
    n_t = n_tok // tl
    nq = tl // CHUNK
    n_g, n_r = len(grads_b), len(specs)

    def body(dys_ref, y_ref, z_ref, pre_ref, xin_ref, dtr_ref, hprev_ref, cw_ref, dtb_ref, alog_ref, dsk_ref, gn_ref,
             exp_ref, redx_ref, redq_ref, *rest):
        g_refs, rest = rest[0:n_g], rest[n_g:]
        dxin_ref, dz_ref, ddtr_ref, dcw_ref, dcb_ref, dgn_ref, ddsk_ref, da_ref, ddtb_ref = rest[0:9]
        recvs, rest = rest[9:9 + n_r], rest[9 + n_r:]
        dxs, dh, cs_s, dskc, shifts, dwp, sums_q, sums_s = rest[0:8]
        sems = rest[8:]
        i = pl.program_id(0)

        @pl.when(i == 0)
        def _():
            _direct_stage(0, specs, g_refs, recvs, sems)
            dcw_ref[...] = jnp.zeros_like(dcw_ref)
            dwp[...] = jnp.zeros_like(dwp)
            dcb_ref[...] = jnp.zeros_like(dcb_ref)
            dgn_ref[...] = jnp.zeros_like(dgn_ref)
            da_ref[...] = jnp.zeros_like(da_ref)
            ddtb_ref[...] = jnp.zeros_like(ddtb_ref)
            dskc[...] = jnp.zeros_like(dskc)
            dh[...] = jnp.zeros_like(dh)
            dxs[tl:tl + SUBLANES, :] = jnp.zeros((SUBLANES, D_XBC), F32)

        yv = y_ref[...]
        zv = z_ref[...]
        dysv = dys_ref[...]
        sz = _sig(zv)
        silz = zv * sz
        yz = yv * silz
        gw = D_SSM // N_GROUPS
        dyz_parts = []
        for g in range(N_GROUPS):
            sl = slice(g * gw, (g + 1) * gw)
            seg = yz[:, sl]
            r = lax.rsqrt(jnp.mean(seg * seg, axis=-1, keepdims=True) + RMS_EPS)
            yzn = seg * r
            dgn_ref[:, sl] += _colsum(dysv[:, sl] * yzn)
            dyzn = dysv[:, sl] * gn_ref[:, sl]
            dyz_parts.append(r * (dyzn - yzn * jnp.mean(dyzn * yzn, axis=-1, keepdims=True)))
        dyz = jnp.concatenate(dyz_parts, axis=1)
        dy = dyz * silz
        dz_ref[...] = (dyz * yv * _dsilu(zv, sz)).astype(BF16)

        prev = pre_ref[...]
        sp = _sig(prev)
        xbc = prev * sp
        dskc[...] += _colsum(dy * xbc[:, 0:D_SSM])
        dt_in = dtr_ref[...] + dtb_ref[...]
        dt = _softplus(dt_in)
        dsp = _sig(dt_in)
        a = -jnp.exp(alog_ref[...])
        adt = dt * a
        for q in reversed(range(nq)):
            r0 = q * CHUNK
            cm = _chunk_common(adt[r0:r0 + CHUNK, :])
            row, col, lo = cm["row"], cm["col"], cm["lo"]
            triu = (col >= row).astype(F32)
            dt_c = dt[r0:r0 + CHUNK, :]
            dt_x = _per_head_lanes(dt_c, exp_ref)
            e_x = _per_head_lanes(cm["e"], exp_ref)
            dec_x = _per_head_lanes(cm["dec"], exp_ref)
            dcd_row = jnp.zeros((1, LANES), F32)
            for g in range(N_GROUPS):
                bcol = D_SSM + g * N_STATE
                ccol = D_SSM + (N_GROUPS + g) * N_STATE
                bg = xbc[r0:r0 + CHUNK, bcol:bcol + N_STATE].astype(BF16)
                cg_ = xbc[r0:r0 + CHUNK, ccol:ccol + N_STATE].astype(BF16)
                gm = _dot_nt(cg_, bg)
                dgm = jnp.zeros((CHUNK, CHUNK), F32)
                dbg = jnp.zeros((CHUNK, N_STATE), F32)
                dcg = jnp.zeros((CHUNK, N_STATE), F32)
                for k in range(N_HEADS // N_GROUPS // 2):
                    ha = (N_HEADS // N_GROUPS) * g + 2 * k
                    hb = ha + 1
                    c0 = ha * HEAD_DIM
                    xh2 = xbc[r0:r0 + CHUNK, c0:c0 + LANES]
                    dt2 = dt_x[:, c0:c0 + LANES]
                    x2 = xh2 * dt2
                    x2b = x2.astype(BF16)
                    la = _decay_mask(cm, ha)
                    lb = _decay_mask(cm, hb)
                    ma = gm * la
                    mb = gm * lb
                    dy2 = dy[r0:r0 + CHUNK, c0:c0 + LANES]
                    dy2b = dy2.astype(BF16)
                    dma = _dot_nt(jnp.where(lo, dy2, 0.0).astype(BF16), x2b)
                    dmb = _dot_nt(jnp.where(lo, 0.0, dy2).astype(BF16), x2b)
                    dx2 = jnp.where(lo, _dot(ma.T.astype(BF16), dy2b), _dot(mb.T.astype(BF16), dy2b))
                    dgm = dgm + dma * la + dmb * lb
                    _split_store(sums_q, 0, slice(ha * CHUNK, (ha + 1) * CHUNK), dma * ma)
                    _split_store(sums_q, 0, slice(hb * CHUNK, (hb + 1) * CHUNK), dmb * mb)
                    h2 = hprev_ref[q, c0:c0 + LANES, :]
                    h2b = h2.astype(BF16)
                    dz2 = dy2 * e_x[:, c0:c0 + LANES]
                    dcg = dcg + _dot(dz2.astype(BF16), h2b)
                    _split_store(sums_s, 0, slice(c0, c0 + LANES), dz2 * _dot_nt(cg_, h2b))
                    dhn = dh[c0:c0 + LANES, :]
                    dhnb = dhn.astype(BF16)
                    cd_a = cm["cd"][:, ha:ha + 1]
                    cd_b = cm["cd"][:, hb:hb + 1]
                    top = row < HEAD_DIM
                    hh = dhn * h2
                    dcd_a = jnp.sum(_colsum(jnp.where(top, hh, 0.0)), axis=1, keepdims=True)
                    dcd_b = jnp.sum(_colsum(jnp.where(top, 0.0, hh)), axis=1, keepdims=True)
                    dcd_row = dcd_row + jnp.where(col[0:1, :] == ha, dcd_a * cd_a, 0.0) + jnp.where(col[0:1, :] == hb, dcd_b * cd_b, 0.0)
                    dh[c0:c0 + LANES, :] = jnp.where(top, cd_a, cd_b) * dhn + _dot(dz2.T.astype(BF16), cg_)
                    w2 = _dot_nt(bg, dhnb)
                    dec2 = dec_x[:, c0:c0 + LANES]
                    dx2 = dx2 + dec2 * w2
                    _split_store(sums_s, 2, slice(c0, c0 + LANES), x2 * w2)
                    dbg = dbg + _dot((x2 * dec2).astype(BF16), dhnb)
                    _split_store(sums_s, 4, slice(c0, c0 + LANES), dx2 * xh2)
                    dxs[r0:r0 + CHUNK, c0:c0 + LANES] = dx2 * dt2 + dsk_ref[:, c0:c0 + LANES] * dy2
                dgmb = dgm.astype(BF16)
                dxs[r0:r0 + CHUNK, bcol:bcol + N_STATE] = dbg + _dot(dgm.T.astype(BF16), cg_)
                dxs[r0:r0 + CHUNK, ccol:ccol + N_STATE] = dcg + _dot(dgmb, bg)
            ones = jnp.ones((SUBLANES, CHUNK), BF16)
            q_cols = _dot(ones, sums_q[0]) + _dot(ones, sums_q[1])
            cs_s[...] = jnp.zeros_like(cs_s)
            for h in range(N_HEADS):
                cs_s[h:h + 1, :] = q_cols[0:1, h * CHUNK:(h + 1) * CHUNK]
            q_rows = _dot(sums_q[0], redq_ref[...]) + _dot(sums_q[1], redq_ref[...])
            de = _dot(sums_s[0], redx_ref[...]) + _dot(sums_s[1], redx_ref[...])
            dd = (_dot(sums_s[2], redx_ref[...]) + _dot(sums_s[3], redx_ref[...])) * cm["dec"]
            ddtx = _dot(sums_s[4], redx_ref[...]) + _dot(sums_s[5], redx_ref[...])
            is_last = row == CHUNK - 1
            dacs = q_rows - cs_s[...].T + de - dd + jnp.where(is_last, dcd_row + _colsum(dd), 0.0)
            dadt = jnp.dot(triu, dacs, precision=HIGHEST, preferred_element_type=F32)
            da_ref[...] += _colsum(dadt * dt_c)
            ddtr_c = (dadt * a + ddtx) * dsp[r0:r0 + CHUNK, :]
            ddtr_ref[r0:r0 + CHUNK, :] = ddtr_c.astype(BF16)
            ddtb_ref[...] += _colsum(ddtr_c)

        dpre = dxs[0:tl, :] * _dsilu(prev, sp)
        dxs[0:tl, :] = dpre
        dcb_ref[...] += _colsum(dpre)
        _shift_copies(dxs, shifts, tl, SSM_BWD_SHIFTS)

        def strip(rb, carry):
            i0 = pl.multiple_of(rb * CONV_RS, CONV_RS)
            for c0 in range(0, D_XBC, CONV_CS):
                xin_s = xin_ref[pl.ds(i0, CONV_RS), c0:c0 + CONV_CS]
                acc = jnp.zeros((CONV_RS, CONV_CS), F32)
                for k in range(SSM_K):
                    sh = _tap_rows(dxs, shifts, SSM_K - 1 - k, i0, c0, SSM_BWD_SHIFTS)
                    acc = acc + sh * cw_ref[k:k + 1, c0:c0 + CONV_CS]
                    t = xin_s * sh
                    dwp[k * SUBLANES:(k + 1) * SUBLANES, c0:c0 + CONV_CS] += _fold_rows(t)
                dxin_ref[pl.ds(i0, CONV_RS), c0:c0 + CONV_CS] = acc.astype(BF16)
            return carry

        lax.fori_loop(0, tl // CONV_RS, strip, 0)
        dxs[tl:tl + SUBLANES, :] = dxs[0:SUBLANES, :]

        @pl.when(i == n_t - 1)
        def _():
            for k in range(SSM_K):
                dcw_ref[k:k + 1, :] = _colsum(dwp[k * SUBLANES:(k + 1) * SUBLANES, :])
            da_ref[...] = da_ref[...] * a
            sel = (lax.broadcasted_iota(jnp.int32, (D_SSM, LANES), 0) // HEAD_DIM
                   == lax.broadcasted_iota(jnp.int32, (D_SSM, LANES), 1)).astype(F32)
            rows = jnp.broadcast_to(dskc[...], (SUBLANES, D_SSM))
            ddsk_ref[...] = jnp.dot(rows, sel, precision=HIGHEST, preferred_element_type=F32)[0:1, :]
            _direct_stage(1, specs, g_refs, recvs, sems)

    head_lanes = _head_lane_matrix()
    per_head = (jnp.arange(N_HEADS * CHUNK)[:, None] // CHUNK == jnp.arange(LANES)[None, :]).astype(BF16)
    rev = functools.partial(_tile, tl, rev_of=n_t)
    return pl.pallas_call(
        body, grid=(n_t,), name="ssd_bwd",
        in_specs=[rev(D_SSM), rev(D_SSM), rev(D_SSM), rev(D_XBC), rev(D_XBC), rev(LANES),
                  pl.BlockSpec((nq, D_SSM, N_STATE), lambda i: (n_t - 1 - i, 0, 0)),
                  _full((SUBLANES, D_XBC)), _full((1, LANES)), _full((1, LANES)), _full((1, D_SSM)), _full((1, D_SSM)),
                  _full((LANES, D_SSM)), _full((D_SSM, LANES)), _full((N_HEADS * CHUNK, LANES))] + [IN_VMEM] * n_g,
        out_specs=[rev(D_XBC), rev(D_SSM), rev(LANES), _full((SUBLANES, D_XBC)), _full((1, D_XBC)), _full((1, D_SSM)),
                   _full((1, LANES)), _full((1, LANES)), _full((1, LANES))] + [ANY] * n_r,
        out_shape=[jax.ShapeDtypeStruct((n_tok, D_XBC), BF16), jax.ShapeDtypeStruct((n_tok, D_SSM), BF16),
                   jax.ShapeDtypeStruct((n_tok, LANES), BF16), jax.ShapeDtypeStruct((SUBLANES, D_XBC), F32),
                   jax.ShapeDtypeStruct((1, D_XBC), F32), jax.ShapeDtypeStruct((1, D_SSM), F32),
                   jax.ShapeDtypeStruct((1, LANES), F32), jax.ShapeDtypeStruct((1, LANES), F32),
                   jax.ShapeDtypeStruct((1, LANES), F32)]
        + [jax.ShapeDtypeStruct((N_DEV, rows, cols), BF16) for rows, cols, _, _ in specs],
        scratch_shapes=[pltpu.VMEM((tl + SUBLANES, D_XBC), F32), pltpu.VMEM((D_SSM, N_STATE), F32),
                        pltpu.VMEM((CHUNK, LANES), F32), pltpu.VMEM((1, D_SSM), F32),
                        pltpu.VMEM((len(SSM_BWD_SHIFTS), tl, D_XBC), F32), pltpu.VMEM((SSM_K * SUBLANES, D_XBC), F32),
                        pltpu.VMEM((2, CHUNK, N_HEADS * CHUNK), BF16), pltpu.VMEM((6, CHUNK, D_SSM), BF16)]
        + _direct_sems(specs),
        compiler_params=_params(),
    )(dys, y, z, pre, xbc_in, dtr, hprev, cw, dtb, alog, dsk, gnorm, head_lanes, head_lanes.T, per_head, *grads_b)


CONF_HALO = 32
CONV_RS = 32
CONV_CS = 256


ALL_SHIFTS = tuple(range(1, SUBLANES))


def _shift_copies(buf, shifts, n_rows, residues=ALL_SHIFTS):
    for j, r in enumerate(residues):
        shifts[j, 0:n_rows, :] = buf[r:r + n_rows, :]


def _fold_rows(t):
    part = t[0:SUBLANES]
    for j in range(1, t.shape[0] // SUBLANES):
        part = part + t[j * SUBLANES:(j + 1) * SUBLANES]
    return part


def _tap_rows(buf, shifts, off, i0, c0, residues=ALL_SHIFTS):
    q, r = divmod(off, SUBLANES)
    rows = pl.ds(pl.multiple_of(i0 + SUBLANES * q, SUBLANES), CONV_RS)
    if r == 0:
        return buf[rows, c0:c0 + CONV_CS]
    return shifts[residues.index(r), rows, c0:c0 + CONV_CS]


def _conf_fwd(glu, cgate, bglu, cw, cb, lg, lb, tl):
    n_tok = glu.shape[0]

    def body(glu_ref, cg_ref, bglu_ref, cw_ref, cb_ref, lg_ref, lb_ref, u0_ref, u1_ref, yc_ref, buf, shifts):
        @pl.when(pl.program_id(0) == 0)
        def _():
            buf[0:CONF_HALO, :] = jnp.zeros((CONF_HALO, D_CONF), F32)

        gl = glu_ref[...] + bglu_ref[...]
        u0 = gl[:, 0:D_CONF] * _sig(gl[:, D_CONF:2 * D_CONF])
        u0_ref[...] = u0
        buf[CONF_HALO:CONF_HALO + tl, :] = u0
        _shift_copies(buf, shifts, tl + CONF_HALO - SUBLANES)

        def strip(rb, carry):
            i0 = pl.multiple_of(rb * CONV_RS, CONV_RS)
            for c0 in range(0, D_CONF, CONV_CS):
                acc = jnp.broadcast_to(cb_ref[:, c0:c0 + CONV_CS], (CONV_RS, CONV_CS))
                for k in range(CONF_K):
                    acc = acc + _tap_rows(buf, shifts, CONF_HALO - (CONF_K - 1) + k, i0, c0) * cw_ref[k:k + 1, c0:c0 + CONV_CS]
                u1_ref[pl.ds(i0, CONV_RS), c0:c0 + CONV_CS] = acc
            return carry

        lax.fori_loop(0, tl // CONV_RS, strip, 0)
        buf[0:CONF_HALO, :] = buf[tl:tl + CONF_HALO, :]
        xhat, _ = _ln_stats(u1_ref[...])
        n = xhat * lg_ref[...] + lb_ref[...]
        cgv = cg_ref[...]
        yc_ref[...] = (n * _sig(n) * (cgv * _sig(cgv))).astype(BF16)

    return pl.pallas_call(
        body, grid=(n_tok // tl,), name="conf_fwd",
        in_specs=[_tile(tl, 2 * D_CONF), _tile(tl, D_CONF), _full((1, 2 * D_CONF)), _full((CONF_HALO, D_CONF)),
                  _full((1, D_CONF)), _full((1, D_CONF)), _full((1, D_CONF))],
        out_specs=[_tile(tl, D_CONF)] * 3,
        out_shape=[jax.ShapeDtypeStruct((n_tok, D_CONF), F32), jax.ShapeDtypeStruct((n_tok, D_CONF), F32),
                   jax.ShapeDtypeStruct((n_tok, D_CONF), BF16)],
        scratch_shapes=[pltpu.VMEM((tl + CONF_HALO, D_CONF), F32),
                        pltpu.VMEM((SUBLANES - 1, tl + CONF_HALO - SUBLANES, D_CONF), F32)],
        compiler_params=_params(),
    )(glu, cgate, bglu, cw, cb, lg, lb)


def _conf_bwd(dyc, dyc_block, u0, u1, glu, cgate, bglu, cw, lg, lb, specs, grads_b, tl):
    n_tok = glu.shape[0]
    n_t = n_tok // tl
    n_g, n_r = len(grads_b), len(specs)

    def body(dyc_ref, u0_ref, u1_ref, glu_ref, cg_ref, bglu_ref, cw_ref, lg_ref, lb_ref, *rest):
        g_refs, rest = rest[0:n_g], rest[n_g:]
        dglu_ref, dcg_ref, dcw_ref, dbglu_ref, small_ref = rest[0:5]
        recvs, rest = rest[5:5 + n_r], rest[5 + n_r:]
        buf, shifts, du0_s, dwp = rest[0:4]
        sems = rest[4:]

        @pl.when(pl.program_id(0) == 0)
        def _():
            _direct_stage(0, specs, g_refs, recvs, sems)
            dwp[...] = jnp.zeros_like(dwp)
            dbglu_ref[...] = jnp.zeros_like(dbglu_ref)
            small_ref[...] = jnp.zeros_like(small_ref)
            buf[tl:tl + CONF_HALO, :] = jnp.zeros((CONF_HALO, D_CONF), F32)

        xhat, rstd = _ln_stats(u1_ref[...])
        n = xhat * lg_ref[...] + lb_ref[...]
        sn = _sig(n)
        cgv = cg_ref[...]
        scg = _sig(cgv)
        dycv = dyc_ref[...]
        dcg_ref[...] = (dycv * (n * sn) * _dsilu(cgv, scg)).astype(BF16)
        dn = dycv * (cgv * scg) * _dsilu(n, sn)
        small_ref[0:1, :] += _colsum(dn * xhat)
        small_ref[1:2, :] += _colsum(dn)
        du1 = _ln_bwd(dn * lg_ref[...], xhat, rstd)
        small_ref[2:3, :] += _colsum(du1)
        buf[0:tl, :] = du1
        _shift_copies(buf, shifts, tl + CONF_HALO - SUBLANES)

        def strip(rb, carry):
            i0 = pl.multiple_of(rb * CONV_RS, CONV_RS)
            for c0 in range(0, D_CONF, CONV_CS):
                u0s = u0_ref[pl.ds(i0, CONV_RS), c0:c0 + CONV_CS]
                acc = jnp.zeros((CONV_RS, CONV_CS), F32)
                for k in range(CONF_K):
                    sh = _tap_rows(buf, shifts, CONF_K - 1 - k, i0, c0)
                    acc = acc + sh * cw_ref[k:k + 1, c0:c0 + CONV_CS]
                    t = u0s * sh
                    dwp[k * SUBLANES:(k + 1) * SUBLANES, c0:c0 + CONV_CS] += _fold_rows(t)
                du0_s[pl.ds(i0, CONV_RS), c0:c0 + CONV_CS] = acc
            return carry

        lax.fori_loop(0, tl // CONV_RS, strip, 0)
        du0 = du0_s[...]
        buf[tl:tl + CONF_HALO, :] = buf[0:CONF_HALO, :]
        gl = glu_ref[...] + bglu_ref[...]
        sg = _sig(gl[:, D_CONF:2 * D_CONF])
        dgv = du0 * sg
        dgg = du0 * gl[:, 0:D_CONF] * sg * (1.0 - sg)
        dglu_ref[:, 0:D_CONF] = dgv.astype(BF16)
        dglu_ref[:, D_CONF:2 * D_CONF] = dgg.astype(BF16)
        dbglu_ref[:, 0:D_CONF] += _colsum(dgv)
        dbglu_ref[:, D_CONF:2 * D_CONF] += _colsum(dgg)

        @pl.when(pl.program_id(0) == n_t - 1)
        def _():
            for k in range(CONF_HALO):
                dcw_ref[k:k + 1, :] = _colsum(dwp[k * SUBLANES:(k + 1) * SUBLANES, :])
            _direct_stage(1, specs, g_refs, recvs, sems)

    rev = functools.partial(_tile, tl, rev_of=n_t)
    return pl.pallas_call(
        body, grid=(n_t,), name="conf_bwd",
        in_specs=[pl.BlockSpec((tl, D_CONF), lambda i: (n_t - 1 - i, dyc_block)),
                  rev(D_CONF), rev(D_CONF), rev(2 * D_CONF), rev(D_CONF), _full((1, 2 * D_CONF)),
                  _full((CONF_HALO, D_CONF)), _full((1, D_CONF)), _full((1, D_CONF))] + [IN_VMEM] * n_g,
        out_specs=[rev(2 * D_CONF), rev(D_CONF), _full((CONF_HALO, D_CONF)), _full((1, 2 * D_CONF)), _full((SUBLANES, D_CONF))]
        + [ANY] * n_r,
        out_shape=[jax.ShapeDtypeStruct((n_tok, 2 * D_CONF), BF16), jax.ShapeDtypeStruct((n_tok, D_CONF), BF16),
                   jax.ShapeDtypeStruct((CONF_HALO, D_CONF), F32), jax.ShapeDtypeStruct((1, 2 * D_CONF), F32),
                   jax.ShapeDtypeStruct((SUBLANES, D_CONF), F32)]
        + [jax.ShapeDtypeStruct((N_DEV, rows, cols), BF16) for rows, cols, _, _ in specs],
        scratch_shapes=[pltpu.VMEM((tl + CONF_HALO, D_CONF), F32),
                        pltpu.VMEM((SUBLANES - 1, tl + CONF_HALO - SUBLANES, D_CONF), F32),
                        pltpu.VMEM((tl, D_CONF), F32), pltpu.VMEM((CONF_HALO * SUBLANES, D_CONF), F32)] + _direct_sems(specs),
        compiler_params=_params(),
    )(dyc, u0, u1, glu, cgate, bglu, cw, lg, lb, *grads_b)


def _tail(x, yssm, yconf, p, tgt, vec, w_out, wpg, wpp, tl):
    n_tok = x.shape[0]

    def body(x_ref, ys_ref, yc_ref, p_ref, t_ref, vec_ref, wo_ref, wg_ref, wp_ref,
             dmix_ref, dr1_ref, dr1b_ref, h1b_ref, dgb_ref, dpb_ref, small_ref, loss_ref):
        @pl.when(pl.program_id(0) == 0)
        def _():
            small_ref[...] = jnp.zeros_like(small_ref)
            loss_ref[...] = jnp.zeros_like(loss_ref)

        xh0, _ = _ln_stats(x_ref[...])
        h0 = xh0 * vec_ref[0:1, :] + vec_ref[1:2, :]
        out = _dot(ys_ref[...], wo_ref[0:D_SSM, :]) + _dot(yc_ref[...], wo_ref[D_SSM:D_SSM + D_CONF, :]) + vec_ref[2:3, :]
        xh1, rstd1 = _ln_stats(ALPHA * h0 + out)
        h1 = xh1 * vec_ref[3:4, :] + vec_ref[4:5, :]
        h1b = h1.astype(BF16)
        h1b_ref[...] = h1b
        gate = _sig(_dot(h1b, wg_ref[...]))
        ple = _dot(p_ref[...].astype(BF16), wp_ref[...])
        xh2, rstd2 = _ln_stats(ALPHA * h1 + gate * ple)
        h2 = xh2 * vec_ref[5:6, :] + vec_ref[6:7, :]
        diff = h2 - t_ref[...]
        part = jnp.sum(jnp.sum(diff * diff, axis=1, keepdims=True), axis=0, keepdims=True) * (0.5 / D_MODEL)
        loss_ref[...] += jnp.broadcast_to(part, loss_ref.shape)
        dh2 = diff * (1.0 / D_MODEL)
        small_ref[3:4, :] += _colsum(dh2 * xh2)
        small_ref[4:5, :] += _colsum(dh2)
        dr2 = _ln_bwd(dh2 * vec_ref[5:6, :], xh2, rstd2)
        dgpre = (dr2 * ple * gate * (1.0 - gate)).astype(BF16)
        dgb_ref[...] = dgpre
        dpb_ref[...] = (dr2 * gate).astype(BF16)
        dh1 = ALPHA * dr2 + _dot_nt(dgpre, wg_ref[...])
        small_ref[1:2, :] += _colsum(dh1 * xh1)
        small_ref[2:3, :] += _colsum(dh1)
        dr1 = _ln_bwd(dh1 * vec_ref[3:4, :], xh1, rstd1)
        small_ref[0:1, :] += _colsum(dr1)
        dr1_ref[...] = dr1
        dr1b = dr1.astype(BF16)
        dr1b_ref[...] = dr1b
        dmix_ref[...] = _dot_nt(dr1b, wo_ref[...])

    d_mix = D_SSM + D_CONF
    return pl.pallas_call(
        body, grid=(n_tok // tl,), name="tail",
        in_specs=[_tile(tl, D_MODEL), _tile(tl, D_SSM), _tile(tl, D_CONF), _tile(tl, D_PLE), _tile(tl, D_MODEL),
                  _full((SUBLANES, D_MODEL)), _full((d_mix, D_MODEL), True), _full((D_MODEL, D_MODEL), True),
                  _full((D_PLE, D_MODEL), True)],
        out_specs=[_tile(tl, d_mix), _tile(tl, D_MODEL), _tile(tl, D_MODEL), _tile(tl, D_MODEL), _tile(tl, D_MODEL),
                   _tile(tl, D_MODEL), _full((SUBLANES, D_MODEL)), _full((SUBLANES, LANES))],
        out_shape=[jax.ShapeDtypeStruct((n_tok, d_mix), F32), jax.ShapeDtypeStruct((n_tok, D_MODEL), F32),
                   jax.ShapeDtypeStruct((n_tok, D_MODEL), BF16), jax.ShapeDtypeStruct((n_tok, D_MODEL), BF16),
                   jax.ShapeDtypeStruct((n_tok, D_MODEL), BF16), jax.ShapeDtypeStruct((n_tok, D_MODEL), BF16),
                   jax.ShapeDtypeStruct((SUBLANES, D_MODEL), F32), jax.ShapeDtypeStruct((SUBLANES, LANES), F32)],
        compiler_params=_params(),
    )(x, yssm, yconf, p, tgt, vec, w_out, wpg, wpp)


def _inproj_bwd(dxin, dz, dglu, dcg, ddtr, dr1, x, g, b, w_r, tl):
    n_tok = x.shape[0]

    def body(dxin_ref, dz_ref, dglu_ref, dcg_ref, ddtr_ref, dr1_ref, x_ref, g_ref, b_ref, w_ref,
             dx_ref, dpb_ref, small_ref):
        @pl.when(pl.program_id(0) == 0)
        def _():
            small_ref[...] = jnp.zeros_like(small_ref)

        dh0 = ALPHA * dr1_ref[...]
        for ref, lo, hi in ((dxin_ref, R_XBC, R_Z), (dz_ref, R_Z, R_GLU), (dglu_ref, R_GLU, R_CG), (dcg_ref, R_CG, R_DT),
                            (ddtr_ref, R_DT, D_INR)):
            piece = ref[...].astype(BF16)
            dpb_ref[:, lo:hi] = piece
            dh0 = dh0 + _dot_nt(piece, w_ref[:, lo:hi])
        xhat, rstd = _ln_stats(x_ref[...])
        small_ref[0:1, :] += _colsum(dh0 * xhat)
        small_ref[1:2, :] += _colsum(dh0)
        dx_ref[...] = _ln_bwd(dh0 * g_ref[...], xhat, rstd)

    return pl.pallas_call(
        body, grid=(n_tok // tl,), name="inproj_bwd",
        in_specs=[_tile(tl, D_XBC), _tile(tl, D_SSM), _tile(tl, 2 * D_CONF), _tile(tl, D_CONF), _tile(tl, LANES),
                  _tile(tl, D_MODEL), _tile(tl, D_MODEL), _full((1, D_MODEL)), _full((1, D_MODEL)),
                  _full((D_MODEL, D_INR), True)],
        out_specs=[_tile(tl, D_MODEL), _tile(tl, D_INR), _full((SUBLANES, D_MODEL))],
        out_shape=[jax.ShapeDtypeStruct((n_tok, D_MODEL), F32), jax.ShapeDtypeStruct((n_tok, D_INR), BF16),
                   jax.ShapeDtypeStruct((SUBLANES, D_MODEL), F32)],
        compiler_params=_params(),
    )(dxin, dz, dglu, dcg, ddtr, dr1, x, g, b, w_r)


def _tn_matmul(a, b, name, tn, tl, out_dtype=F32):
    n_tok, m = a.shape
    n = b.shape[1]
    n_l = n_tok // tl
    direct = out_dtype == F32

    def body(a_ref, b_ref, o_ref, *scratch):
        acc = o_ref if direct else scratch[0]

        @pl.when(pl.program_id(1) == 0)
        def _():
            acc[...] = jnp.zeros_like(acc)

        acc[...] += lax.dot_general(a_ref[...], b_ref[...], TN_DIMS, preferred_element_type=F32)
        if not direct:
            @pl.when(pl.program_id(1) == n_l - 1)
            def _():
                o_ref[...] = acc[...].astype(out_dtype)

    return pl.pallas_call(
        body, grid=(n // tn, n_l), name=name,
        in_specs=[pl.BlockSpec((tl, m), lambda j, l: (l, 0)), pl.BlockSpec((tl, tn), lambda j, l: (l, j))],
        out_specs=pl.BlockSpec((m, tn), lambda j, l: (0, j)),
        out_shape=jax.ShapeDtypeStruct((m, n), out_dtype),
        scratch_shapes=[] if direct else [pltpu.VMEM((m, tn), F32)],
        compiler_params=pltpu.CompilerParams(dimension_semantics=("parallel", "arbitrary"), vmem_limit_bytes=VMEM_LIMIT),
    )(a, b)


def _dw_in(a, b, spack, tn, tl):
    n_tok, m = a.shape
    n = b.shape[1]
    n_j, n_l = n // tn, n_tok // tl

    def body(a_ref, b_ref, s_ref, o_ref, small_all, send_sems, recv_sems, loc_sem):
        j, l = pl.program_id(0), pl.program_id(1)

        @pl.when(jnp.logical_and(j == 0, l == 0))
        def _():
            _gather8_stage(0, s_ref, small_all, send_sems, recv_sems, loc_sem)

        @pl.when(l == 0)
        def _():
            o_ref[...] = jnp.zeros_like(o_ref)

        o_ref[...] += lax.dot_general(a_ref[...], b_ref[...], TN_DIMS, preferred_element_type=F32)

        @pl.when(jnp.logical_and(j == n_j - 1, l == n_l - 1))
        def _():
            _gather8_stage(1, s_ref, small_all, send_sems, recv_sems, loc_sem)

    dma = pltpu.SemaphoreType.DMA
    return pl.pallas_call(
        body, grid=(n_j, n_l), name="dw_in",
        in_specs=[pl.BlockSpec((tl, m), lambda j, l: (l, 0)), pl.BlockSpec((tl, tn), lambda j, l: (l, j)), IN_VMEM],
        out_specs=[pl.BlockSpec((m, tn), lambda j, l: (0, j)), ANY],
        out_shape=[jax.ShapeDtypeStruct((m, n), F32), jax.ShapeDtypeStruct((N_DEV,) + spack.shape, F32)],
        scratch_shapes=[dma((N_DEV - 1,)), dma((N_DEV - 1,)), dma],
        compiler_params=pltpu.CompilerParams(dimension_semantics=("arbitrary", "arbitrary"), vmem_limit_bytes=VMEM_LIMIT),
    )(a, b, spack)


def _pad_rows(a, rows):
    return jnp.pad(a, ((0, rows - a.shape[0]), (0, 0)))


def _pad_lanes(a):
    return jnp.pad(a, ((0, 0), (0, LANES - a.shape[1])))


def _local_grads(x, p, tgt, w4, later, ssm_cw, conf_cw, sm):
    n_tok = x.shape[0]
    tl = min(256, n_tok)
    tlm = min(512, n_tok)
    row = lambda v: v.reshape(1, -1)
    g_e, b_e = row(sm["ln_emb_g"]), row(sm["ln_emb_b"])
    h0b, xbc_in, z, glu, cgate, dtr, w_out_all, wpg_all, wpp_all, w_r = _ln_inproj(x, g_e, b_e, w4, later, tl)
    w_out_b = w_out_all.reshape(D_SSM + D_CONF, D_MODEL)
    wpg_b = wpg_all.reshape(D_MODEL, D_MODEL)
    wpp_b = wpp_all.transpose(1, 0, 2).reshape(D_PLE, D_MODEL)

    cw4 = _pad_rows(ssm_cw, SUBLANES)
    dtb, alog = _pad_lanes(sm["dt_bias"]), _pad_lanes(sm["a_log"])
    dsk = jnp.repeat(sm["d_skip"], HEAD_DIM, axis=1)
    pre, y, yssm, hprev = _ssd_fwd(xbc_in, z, dtr, cw4, sm["ssm_conv_b"], dtb, alog, dsk, sm["ssm_norm_g"], tl)

    cw31 = _pad_rows(conf_cw, CONF_HALO)
    u0, u1, yconf = _conf_fwd(glu, cgate, sm["b_glu"], cw31, sm["conf_conv_b"], sm["conf_ln_g"], sm["conf_ln_b"], tl)

    vec = jnp.concatenate([g_e, b_e, sm["b_out"], sm["ln1_g"], sm["ln1_b"], sm["ln2_g"], sm["ln2_b"],
                           jnp.zeros((1, D_MODEL), F32)], axis=0)
    dmix, dr1, dr1b, h1b, dgb, dpb, small_t, loss = _tail(
        x, yssm, yconf, p, tgt, vec, w_out_b, wpg_b, wpp_b, tlm)

    dwo = [_tn_matmul(yssm, dr1b, "dw_out_ssm", D_MODEL, tlm, BF16), _tn_matmul(yconf, dr1b, "dw_out_conf", D_MODEL, tlm, BF16)]
    dwp_ = [_tn_matmul(h1b, dgb, "dw_ple_gate", D_MODEL, tlm, BF16),
            _tn_matmul(p.astype(BF16), dpb, "dw_ple_proj", D_MODEL, tlm, BF16)]
    dglu, dcg, dcw31, dbglu, small_c, dw_out = _conf_bwd(dmix, D_SSM // D_CONF, u0, u1, glu, cgate, sm["b_glu"], cw31,
                                                          sm["conf_ln_g"], sm["conf_ln_b"], [W_OUT_DIRECT], dwo, tl)
    dxin, dz, ddtr, dcw4, dcb4, dgn, ddsk, dalog, ddtb, dwpg, dwpp = _ssd_bwd(
        dmix, y, z, pre, xbc_in, dtr, hprev, cw4, dtb, alog, dsk, sm["ssm_norm_g"], [WPG_DIRECT, WPP_DIRECT], dwp_, tl)
    dx, dprojb, small_e = _inproj_bwd(dxin, dz, dglu, dcg, ddtr, dr1, x, g_e, b_e, w_r, tlm)
    spack = _pack_small(dict(
        ln_emb_g=small_e[0], ln_emb_b=small_e[1], ssm_conv_w=dcw4[0:SSM_K], ssm_conv_b=dcb4,
        dt_bias=ddtb[:, 0:N_HEADS], a_log=dalog[:, 0:N_HEADS], d_skip=ddsk[:, 0:N_HEADS], ssm_norm_g=dgn, b_glu=dbglu,
        conf_conv_w=dcw31[0:CONF_K], conf_conv_b=small_c[2:3], conf_ln_g=small_c[0:1], conf_ln_b=small_c[1:2],
        b_out=small_t[0:1], ln1_g=small_t[1:2], ln1_b=small_t[2:3], ln2_g=small_t[3:4], ln2_b=small_t[4:5]), loss[0, 0])
    dw_r, small_all = _dw_in(h0b, dprojb, spack, D_INR // 3, tlm)
    return dx, dict(w_in=dw_r, w_out=dw_out, w_ple_gate=dwpg, w_ple_proj=dwpp, small=small_all)


N_CHIPS = 4
N_DEV = 8
W_IN_SH = D_IN // N_CHIPS
BIG = (("w_in", D_MODEL, W_IN_SH), ("w_out", (D_SSM + D_CONF) // N_CHIPS, D_MODEL),
       ("w_ple_gate", D_MODEL // N_CHIPS, D_MODEL), ("w_ple_proj", D_PLE, D_MODEL // N_CHIPS))
SEGS = ((R_XBC, 0, D_SSM), (R_XBC + D_SSM, 2048, 256), (R_XBC + D_SSM + 256, 2304, 256), (R_Z, 1024, D_SSM),
        (R_GLU, 2576, 2 * D_CONF), (R_CG, 4624, D_CONF), (R_DT, 2560, N_HEADS))
ROWS_CW4 = 2
ROWS_CW31 = 8
ROWS_CONV = 16
SMALL_ROWS = 56
SMALL_LAYOUT = (("ln_emb_g", 0, 1), ("ln_emb_b", 1, 1), ("ssm_conv_b", 2, 2), ("dt_bias", 4, 1), ("a_log", 5, 1),
                ("d_skip", 6, 1), ("ssm_norm_g", 7, 1), ("b_glu", 8, 2), ("conf_conv_b", 10, 1), ("conf_ln_g", 11, 1),
                ("conf_ln_b", 12, 1), ("b_out", 13, 1), ("ln1_g", 14, 1), ("ln1_b", 15, 1), ("ln2_g", 16, 1), ("ln2_b", 17, 1))
CONV_LAYOUT = (("ssm_conv_w", 18, 6, (SSM_K, D_XBC)), ("conf_conv_w", 24, 31, (CONF_K, D_CONF)))


def _rows_of(v, rows):
    flat = v.reshape(-1)
    return jnp.pad(flat, (0, rows * D_MODEL - flat.shape[0])).reshape(rows, D_MODEL)


LOSS_ROW = 55


def _pack_small(d, loss_share):
    parts = [_rows_of(d[n], r) for n, _, r in SMALL_LAYOUT] + [_rows_of(d[n], r) for n, _, r, _ in CONV_LAYOUT]
    parts.append(_rows_of(loss_share, SMALL_ROWS - LOSS_ROW))
    return jnp.concatenate(parts, axis=0)


def _gather8_stage(stage, s_ref, all_ref, send_sems, recv_sems, loc_sem):
    x, y, c = _my_place()
    me = 4 * x + 2 * y + c
    copies = [pltpu.make_async_copy(s_ref, all_ref.at[me], loc_sem)]
    for mk in range(1, N_DEV):
        peer = (x ^ (mk >> 2), y ^ ((mk >> 1) & 1), c ^ (mk & 1))
        copies.append(_remote(s_ref, all_ref.at[me], send_sems.at[mk - 1], recv_sems.at[mk - 1], peer))
    for cp in copies:
        cp.start() if stage == 0 else cp.wait()


def _row_chunks(rows, n):
    return [(j * (rows // n), rows // n) for j in range(n)]


def _my_place():
    return lax.axis_index("x"), lax.axis_index("y"), lax.axis_index("c")


MESH_ID = pl.DeviceIdType.MESH
ANY = pl.BlockSpec(memory_space=pl.ANY)
IN_VMEM = pl.BlockSpec(memory_space=pltpu.VMEM)
CHIP_FLIPS = ((1, 0), (0, 1), (1, 1))


def _remote(src, dst, send_sem, recv_sem, peer):
    return pltpu.make_async_remote_copy(src, dst, send_sem, recv_sem, device_id=peer, device_id_type=MESH_ID)


GATHER_CHUNKS = (4, 2, 1, 1)


def _gather_plans(kinds):
    plan = [(a, o, n, rows // 2) for a, (rows, ch) in enumerate(kinds) for o, n in _row_chunks(rows // 2, ch)]
    own_plan = [(a, o, n) for a, (rows, ch) in enumerate(kinds) for o, n in _row_chunks(rows, 2 * ch)]
    return plan, own_plan


def _gather_sems(plan, own_plan):
    hop = pltpu.SemaphoreType.DMA((3, len(plan)))
    return [hop, hop, hop, hop, pltpu.SemaphoreType.DMA((len(own_plan),))]


def _gather_stage(stage, ins, outs, plan, own_plan, sems):
    send_a, recv_a, send_b, recv_b, loc = sems
    x, y, c = _my_place()
    s = 2 * x + y
    sibling = (x, y, 1 - c)
    own = [pltpu.make_async_copy(ins[a].at[pl.ds(o, n)], outs[a].at[s, pl.ds(o, n)], loc.at[j])
           for j, (a, o, n) in enumerate(own_plan)]
    first, arrive, passed, arrive_b = [], [], [], []
    for k, (fx, fy) in enumerate(CHIP_FLIPS):
        peer = (x ^ fx, y ^ fy, c)
        sk = 2 * (x ^ fx) + (y ^ fy)
        for j, (a, o, n, h) in enumerate(plan):
            mine = pl.ds(pl.multiple_of(c * h + o, 16), n)
            theirs = pl.ds(pl.multiple_of((1 - c) * h + o, 16), n)
            first.append(_remote(ins[a].at[mine], outs[a].at[s, mine], send_a.at[k, j], recv_a.at[k, j], peer))
            land = outs[a].at[sk, mine]
            arrive.append(_remote(land, land, send_a.at[k, j], recv_a.at[k, j], peer))
            passed.append(_remote(land, land, send_b.at[k, j], recv_b.at[k, j], sibling))
            land_b = outs[a].at[sk, theirs]
            arrive_b.append(_remote(land_b, land_b, send_b.at[k, j], recv_b.at[k, j], sibling))
    if stage == 0:
        for cp in own + first:
            cp.start()
    elif stage == 1:
        for got, fwd in zip(arrive, passed):
            got.wait_recv()
            fwd.start()
    else:
        for got in arrive_b:
            got.wait_recv()
        for cp in first + passed:
            cp.wait_send()
        for cp in own:
            cp.wait()


W_OUT_DIRECT = (512, D_MODEL, 2, ((0, 0, 0), (0, 512, 0), (1, 0, 0), (1, 512, 0)))
WPG_DIRECT = (256, D_MODEL, 1, ((0, 0, 0), (0, 256, 0), (0, 512, 0), (0, 768, 0)))
WPP_DIRECT = (D_PLE, 256, 1, ((1, 0, 0), (1, 0, 256), (1, 0, 512), (1, 0, 768)))


def _direct_sems(specs):
    out = []
    for _, _, copies, _ in specs:
        out += [pltpu.SemaphoreType.DMA((N_CHIPS, 2, copies)), pltpu.SemaphoreType.DMA((N_DEV, copies)),
                pltpu.SemaphoreType.DMA((copies,))]
    return out


def _direct_stage(stage, specs, srcs, recvs, sems):
    x, y, c = _my_place()
    me = 4 * x + 2 * y + c
    s = 2 * x + y
    for r, (rows, cols, copies, where) in enumerate(specs):
        send_sems, recv_sems, loc_sems = sems[3 * r:3 * r + 3]
        for k, (o, n) in enumerate(_row_chunks(rows, copies)):
            for t, (si, row0, col0) in enumerate(where):
                src = srcs[si].at[pl.ds(row0 + o, n), pl.ds(col0, cols)]
                dst = recvs[r].at[me, pl.ds(o, n)]
                for cc in range(2):
                    to_self = jnp.logical_and(s == t, c == cc)
                    away = _remote(src, dst, send_sems.at[t, cc, k], recv_sems.at[me, k], (t // 2, t % 2, cc))
                    here = pltpu.make_async_copy(src, dst, loc_sems.at[k])

                    @pl.when(to_self)
                    def _():
                        here.start() if stage == 0 else here.wait()

                    @pl.when(jnp.logical_not(to_self))
                    def _():
                        away.start() if stage == 0 else away.wait_send()
            if stage == 1:
                for j in range(N_DEV):
                    land = recvs[r].at[j, pl.ds(o, n)]

                    @pl.when(j != me)
                    def _():
                        _remote(land, land, send_sems.at[0, 0, k], recv_sems.at[j, k], (0, 0, 0)).wait_recv()


def _gather_weights(w_in_b, conv_f):
    plan, own_plan = _gather_plans([(BIG[0][1], GATHER_CHUNKS[0])])

    def body(w_ref, conv_ref, w_all, conv_all, *sems):
        conv_send, conv_recv, conv_loc = sems[5:]
        x, y, c = _my_place()
        s = 2 * x + y
        conv_own = pltpu.make_async_copy(conv_ref, conv_all.at[s], conv_loc)
        conv_out = [_remote(conv_ref, conv_all.at[s], conv_send.at[k], conv_recv.at[k], (x ^ fx, y ^ fy, c))
                    for k, (fx, fy) in enumerate(CHIP_FLIPS)]
        _gather_stage(0, [w_ref], [w_all], plan, own_plan, sems[0:5])
        for cp in [conv_own] + conv_out:
            cp.start()
        _gather_stage(1, [w_ref], [w_all], plan, own_plan, sems[0:5])
        _gather_stage(2, [w_ref], [w_all], plan, own_plan, sems[0:5])
        for cp in conv_out:
            cp.wait()
        conv_own.wait()

    arrays = [w_in_b, conv_f]
    return pl.pallas_call(
        body, name="gather_weights", in_specs=[IN_VMEM] * 2, out_specs=[ANY] * 2,
        out_shape=[jax.ShapeDtypeStruct((N_CHIPS,) + a.shape, a.dtype) for a in arrays],
        scratch_shapes=_gather_sems(plan, own_plan)
        + [pltpu.SemaphoreType.DMA((3,)), pltpu.SemaphoreType.DMA((3,)), pltpu.SemaphoreType.DMA],
    )(*arrays)


SHARE_CHUNKS = 4


OWN_TILES = (((0, 1024), (1536, 512)), ((1024, 512), (1920, 640), (2560, 256), (5632, 128)), ((2688, 1536),), ((4096, 1536),))
OWN_W = 1536
RED_CHUNKS = 8


def _own_columns(t, rows):
    if t == 0:
        return rows[:, 0:W_IN_SH]
    if t == 1:
        return jnp.concatenate([rows[:, 516:1152], rows[:, 0:512], rows[:, 1408:1424], rows[:, 1152:1400]], axis=1)
    first = (2808 - 2688) if t == 2 else (4220 - 4096)
    return rows[:, first:first + W_IN_SH]


def _reduce_w_in(dw_r):
    half = D_MODEL // 2
    rc = half // RED_CHUNKS
    max_r = max(len(r) for r in OWN_TILES)

    def body(g_ref, got_ref, theirs, a_buf, b_buf, sum_buf, d2d_send, d2d_recv, ld_sems, ici_send, ici_recv, own_sems):
        x, y, c = _my_place()
        s = 2 * x + y
        sibling = (x, y, 1 - c)
        swap = [_remote(g_ref.at[pl.ds(pl.multiple_of((1 - c) * half + k * rc, SUBLANES), rc)], theirs.at[pl.ds(k * rc, rc)],
                        d2d_send.at[k], d2d_recv.at[k], sibling) for k in range(RED_CHUNKS)]
        for cp in swap:
            cp.start()

        def tiles(k, t, r):
            k0, wd = OWN_TILES[t][r]
            at = sum(w_ for _, w_ in OWN_TILES[t][:r])
            src = sum_buf.at[k, pl.ds(0, rc), pl.ds(k0, wd)]
            dst = got_ref.at[s, pl.ds(k * rc, rc), pl.ds(at, wd)]
            return (_remote(src, dst, ici_send.at[t, r, k], ici_recv.at[s, r, k], (t // 2, t % 2, c)),
                    pltpu.make_async_copy(src, dst, own_sems.at[r, k]))

        mine = [pltpu.make_async_copy(g_ref.at[pl.ds(pl.multiple_of(c * half + k * rc, SUBLANES), rc)], a_buf.at[k % 2],
                                      ld_sems.at[0, k % 2]) for k in range(RED_CHUNKS)]
        other = [pltpu.make_async_copy(theirs.at[pl.ds(k * rc, rc)], b_buf.at[k % 2], ld_sems.at[1, k % 2])
                 for k in range(RED_CHUNKS)]
        mine[0].start()
        for k in range(RED_CHUNKS):
            slot = k % 2
            swap[k].wait_recv()
            other[k].start()
            if k + 1 < RED_CHUNKS:
                mine[k + 1].start()
            mine[k].wait()
            other[k].wait()
            sum_buf[k] = (a_buf[slot] + b_buf[slot]).astype(BF16)
            for t in range(N_CHIPS):
                for r in range(len(OWN_TILES[t])):
                    away, here = tiles(k, t, r)

                    @pl.when(t != s)
                    def _():
                        away.start()

                    @pl.when(t == s)
                    def _():
                        here.start()

        for k in range(RED_CHUNKS):
            swap[k].wait_send()
            for t in range(N_CHIPS):
                for r, (_, wd) in enumerate(OWN_TILES[t]):
                    away, here = tiles(k, t, r)
                    at = sum(w_ for _, w_ in OWN_TILES[t][:r])

                    @pl.when(t != s)
                    def _():
                        away.wait_send()

                    @pl.when(t == s)
                    def _():
                        here.wait()
                        for j in range(N_CHIPS):
                            if j != t:
                                land = got_ref.at[j, pl.ds(k * rc, rc), pl.ds(at, wd)]
                                _remote(land, land, ici_send.at[0, 0, 0], ici_recv.at[j, r, k], (0, 0, 0)).wait_recv()

    dma = pltpu.SemaphoreType.DMA
    got, _ = pl.pallas_call(
        body, name="reduce_w_in", in_specs=[ANY], out_specs=[ANY, ANY],
        out_shape=[jax.ShapeDtypeStruct((N_CHIPS, half, OWN_W), BF16), jax.ShapeDtypeStruct((half, D_INR), F32)],
        scratch_shapes=[pltpu.VMEM((2, rc, D_INR), F32), pltpu.VMEM((2, rc, D_INR), F32), pltpu.VMEM((RED_CHUNKS, rc, D_INR), BF16),
                        dma((RED_CHUNKS,)), dma((RED_CHUNKS,)), dma((2, 2)), dma((N_CHIPS, max_r, RED_CHUNKS)),
                        dma((N_CHIPS, max_r, RED_CHUNKS)), dma((max_r, RED_CHUNKS))],
        compiler_params=pltpu.CompilerParams(vmem_limit_bytes=VMEM_LIMIT),
    )(dw_r)
    return got


def _add_slots(got):
    _, rows, cols = got.shape
    tr = 128

    def body(g_ref, o_ref):
        o_ref[...] = ((g_ref[0].astype(F32) + g_ref[1].astype(F32)) + g_ref[2].astype(F32)) + g_ref[3].astype(F32)

    return pl.pallas_call(body, grid=(rows // tr,), name="add_chips_w_in",
                          in_specs=[pl.BlockSpec((N_CHIPS, tr, cols), lambda i: (0, i, 0))],
                          out_specs=pl.BlockSpec((tr, cols), lambda i: (i, 0)),
                          out_shape=jax.ShapeDtypeStruct((rows, cols), F32), compiler_params=_params(seq=False))(got)


def _share_halves(tot):
    n_rows, half = tot.shape
    plan = _row_chunks(half, SHARE_CHUNKS)

    def body(t_ref, both, send_sems, recv_sems, loc):
        x, y, c = _my_place()
        rows = pl.ds(0, n_rows)

        def place(o, n):
            return both.at[rows, pl.ds(pl.multiple_of(c * half + o, LANES), n)]

        own = [pltpu.make_async_copy(t_ref.at[rows, pl.ds(o, n)], place(o, n), loc.at[j]) for j, (o, n) in enumerate(plan)]
        sends = [_remote(t_ref.at[rows, pl.ds(o, n)], place(o, n), send_sems.at[j], recv_sems.at[j], (x, y, 1 - c))
                 for j, (o, n) in enumerate(plan)]
        for cp in own + sends:
            cp.start()
        for cp in sends:
            cp.wait()
        for cp in own:
            cp.wait()

    dma = pltpu.SemaphoreType.DMA
    return pl.pallas_call(
        body, name="share_halves", in_specs=[IN_VMEM], out_specs=ANY,
        out_shape=jax.ShapeDtypeStruct((n_rows, 2 * half), F32),
        scratch_shapes=[dma((len(plan),)), dma((len(plan),)), dma((len(plan),))],
    )(tot)


def _adam_math(w, g, m, v):
    m = ADAM_B1 * m + (1.0 - ADAM_B1) * g
    v = ADAM_B2 * v + (1.0 - ADAM_B2) * (g * g)
    m_hat = m / (1.0 - ADAM_B1 ** ADAM_STEP)
    v_hat = v / (1.0 - ADAM_B2 ** ADAM_STEP)
    return -ADAM_LR * (m_hat / (jnp.sqrt(v_hat) + ADAM_EPS) + ADAM_WD * w), m, v


def _adam(w, g, m, v, name):
    rows, cols = w.shape

    def body(w_ref, g_ref, m_ref, v_ref, d_ref, nm_ref, nv_ref):
        d_ref[...], nm_ref[...], nv_ref[...] = _adam_math(w_ref[...], g_ref[...], m_ref[...], v_ref[...])

    if rows <= 256 or rows % 256 == 0:
        tr = min(rows, 256)
        n_blocks, spec = rows // tr, pl.BlockSpec((tr, cols), lambda i: (i, 0))
    else:
        n_blocks, spec = cols // 256, pl.BlockSpec((rows, 256), lambda i: (0, i))
    return pl.pallas_call(body, grid=(n_blocks,), name=name, in_specs=[spec] * 4, out_specs=[spec] * 3,
                          out_shape=[jax.ShapeDtypeStruct(w.shape, F32)] * 3, compiler_params=_params(seq=False))(w, g, m, v)


def _adam_sum(w, parts, m, v, name):
    rows, cols = w.shape
    tr = rows if rows <= 256 else 256

    def body(w_ref, p_ref, m_ref, v_ref, g_ref, d_ref, nm_ref, nv_ref):
        g = p_ref[0].astype(F32)
        for j in range(1, N_DEV):
            g = g + p_ref[j].astype(F32)
        g_ref[...] = g
        d_ref[...], nm_ref[...], nv_ref[...] = _adam_math(w_ref[...], g, m_ref[...], v_ref[...])

    spec = pl.BlockSpec((tr, cols), lambda i: (i, 0))
    return pl.pallas_call(
        body, grid=(rows // tr,), name=name,
        in_specs=[spec, pl.BlockSpec((N_DEV, tr, cols), lambda i: (0, i, 0)), spec, spec], out_specs=[spec] * 4,
        out_shape=[jax.ShapeDtypeStruct(w.shape, F32)] * 4, compiler_params=_params(seq=False))(w, parts, m, v)


def _adam_small(parts, ws, ms, vs):
    n_w = len(SMALL_LAYOUT)

    def body(p_ref, *refs):
        w_refs, m_refs, v_refs = refs[0:n_w], refs[n_w:2 * n_w], refs[2 * n_w:3 * n_w]
        sum_ref = refs[3 * n_w]
        outs = refs[3 * n_w + 1:]
        g = p_ref[0]
        for k in range(1, N_DEV):
            g = g + p_ref[k]
        sum_ref[...] = g
        for a, (_, r0, rows) in enumerate(SMALL_LAYOUT):
            width = w_refs[a].shape[1]
            for j in range(rows):
                lo, hi = j * D_MODEL, min((j + 1) * D_MODEL, width)
                gj = sum_ref[r0 + j:r0 + j + 1, 0:hi - lo]
                d, nm, nv = _adam_math(w_refs[a][:, lo:hi], gj, m_refs[a][:, lo:hi], v_refs[a][:, lo:hi])
                for out, val in zip(outs[4 * a:4 * a + 4], (gj, d, nm, nv)):
                    out[:, lo:hi] = val

    shapes = [jax.ShapeDtypeStruct((SMALL_ROWS, D_MODEL), F32)]
    for wa in ws:
        shapes += [jax.ShapeDtypeStruct(wa.shape, F32)] * 4
    res = pl.pallas_call(body, name="adam_small", out_shape=shapes)(parts, *ws, *ms, *vs)
    return res[0], [res[1 + 4 * a:5 + 4 * a] for a in range(n_w)]


def kernel(x, p, ln_emb_g, ln_emb_b, w_in, ssm_conv_w, ssm_conv_b, dt_bias, a_log, d_skip, ssm_norm_g, b_glu, conf_conv_w, conf_conv_b, conf_ln_g, conf_ln_b, w_out, b_out, ln1_g, ln1_b, w_ple_gate, w_ple_proj, ln2_g, ln2_b, loss_target, m_ln_emb_g, m_ln_emb_b, m_w_in, m_ssm_conv_w, m_ssm_conv_b, m_dt_bias, m_a_log, m_d_skip, m_ssm_norm_g, m_b_glu, m_conf_conv_w, m_conf_conv_b, m_conf_ln_g, m_conf_ln_b, m_w_out, m_b_out, m_ln1_g, m_ln1_b, m_w_ple_gate, m_w_ple_proj, m_ln2_g, m_ln2_b, v_ln_emb_g, v_ln_emb_b, v_w_in, v_ssm_conv_w, v_ssm_conv_b, v_dt_bias, v_a_log, v_d_skip, v_ssm_norm_g, v_b_glu, v_conf_conv_w, v_conf_conv_b, v_conf_ln_g, v_conf_ln_b, v_w_out, v_b_out, v_ln1_g, v_ln1_b, v_w_ple_gate, v_w_ple_proj, v_ln2_g, v_ln2_b):
    order = ("ln_emb_g", "ln_emb_b", "w_in", "ssm_conv_w", "ssm_conv_b", "dt_bias", "a_log", "d_skip", "ssm_norm_g", "b_glu",
             "conf_conv_w", "conf_conv_b", "conf_ln_g", "conf_ln_b", "w_out", "b_out", "ln1_g", "ln1_b", "w_ple_gate",
             "w_ple_proj", "ln2_g", "ln2_b")
    w = dict(zip(order, (ln_emb_g, ln_emb_b, w_in, ssm_conv_w, ssm_conv_b, dt_bias, a_log, d_skip, ssm_norm_g, b_glu,
                         conf_conv_w, conf_conv_b, conf_ln_g, conf_ln_b, w_out, b_out, ln1_g, ln1_b, w_ple_gate, w_ple_proj,
                         ln2_g, ln2_b)))
    m = dict(zip(order, (m_ln_emb_g, m_ln_emb_b, m_w_in, m_ssm_conv_w, m_ssm_conv_b, m_dt_bias, m_a_log, m_d_skip,
                         m_ssm_norm_g, m_b_glu, m_conf_conv_w, m_conf_conv_b, m_conf_ln_g, m_conf_ln_b, m_w_out, m_b_out,
                         m_ln1_g, m_ln1_b, m_w_ple_gate, m_w_ple_proj, m_ln2_g, m_ln2_b)))
    v = dict(zip(order, (v_ln_emb_g, v_ln_emb_b, v_w_in, v_ssm_conv_w, v_ssm_conv_b, v_dt_bias, v_a_log, v_d_skip,
                         v_ssm_norm_g, v_b_glu, v_conf_conv_w, v_conf_conv_b, v_conf_ln_g, v_conf_ln_b, v_w_out, v_b_out,
                         v_ln1_g, v_ln1_b, v_w_ple_gate, v_w_ple_proj, v_ln2_g, v_ln2_b)))

    conv_f = jnp.concatenate([_rows_of(w["ssm_conv_w"], ROWS_CW4), _rows_of(w["conf_conv_w"], ROWS_CW31),
                              jnp.zeros((ROWS_CONV - ROWS_CW4 - ROWS_CW31, D_MODEL), F32)], axis=0)
    shards_b = [w[n][0].astype(BF16) for n, _, _ in BIG]
    w_in_all, conv_all = _gather_weights(shards_b[0], conv_f)
    cw4 = conv_all[:, 0:ROWS_CW4].reshape(N_CHIPS, -1)[:, :SSM_K * D_XBC // N_CHIPS]
    cw4 = cw4.reshape(N_CHIPS, SSM_K, D_XBC // N_CHIPS).transpose(1, 0, 2).reshape(SSM_K, D_XBC)
    cw31 = conv_all[:, ROWS_CW4:ROWS_CW4 + ROWS_CW31].reshape(N_CHIPS, -1)[:, :CONF_K * D_CONF // N_CHIPS]
    cw31 = cw31.reshape(N_CHIPS, CONF_K, D_CONF // N_CHIPS).transpose(1, 0, 2).reshape(CONF_K, D_CONF)

    small_names = [n for n, _, _ in SMALL_LAYOUT]
    sm = {n: w[n] for n in small_names}
    dx, grads = _local_grads(x[0], p[0, 0], loss_target[0], w_in_all, shards_b[1:], cw4, cw31, sm)

    chip_i = 2 * lax.axis_index("x") + lax.axis_index("y")
    mine = lax.switch(chip_i, [functools.partial(_own_columns, t) for t in range(N_CHIPS)], _add_slots(_reduce_w_in(grads["w_in"])))
    g_w_in_t = _share_halves(jnp.swapaxes(mine, 0, 1))

    out_g, out_d, out_m, out_v = {}, {}, {}, {}
    as_row = lambda a: a.reshape(1, -1)
    g_s, small_out = _adam_small(grads["small"], [as_row(w[n]) for n in small_names], [as_row(m[n]) for n in small_names],
                                 [as_row(v[n]) for n in small_names])
    loss = g_s[LOSS_ROW, 0]
    tr_ = lambda a: jnp.swapaxes(a, 0, 1)
    n = "w_in"
    d, nm, nv = _adam(tr_(w[n][0]), g_w_in_t, tr_(m[n][0]), tr_(v[n][0]), "adam_" + n)
    out_g[n], out_d[n], out_m[n], out_v[n] = [tr_(a)[None] for a in (g_w_in_t, d, nm, nv)]
    for n, r0, r, shape in CONV_LAYOUT:
        whole = g_s[r0:r0 + r].reshape(-1)[:shape[0] * shape[1]].reshape(shape)
        g = lax.dynamic_slice_in_dim(whole, chip_i * (shape[1] // N_CHIPS), shape[1] // N_CHIPS, axis=1)
        d, nm, nv = _adam(w[n][0], g, m[n][0], v[n][0], "adam_" + n)
        out_g[n], out_d[n], out_m[n], out_v[n] = g[None], d[None], nm[None], nv[None]
    for n, _, _ in BIG[1:]:
        out_g[n], out_d[n], out_m[n], out_v[n] = [a[None] for a in _adam_sum(w[n][0], grads[n], m[n][0], v[n][0], "adam_" + n)]
    for n, four in zip(small_names, small_out):
        out_g[n], out_d[n], out_m[n], out_v[n] = [a.reshape(w[n].shape) for a in four]
    return (loss, dx[None], *[out_g[n] for n in order], *[out_d[n] for n in order], *[out_m[n] for n in order],
            *[out_v[n] for n in order])
```

```python
import functools

import jax
import jax.numpy as jnp
from jax import lax
from jax.experimental import pallas as pl
from jax.experimental.pallas import tpu as pltpu

F32 = jnp.float32
BF16 = jnp.bfloat16

D_MODEL = 1024
D_PLE = 256
D_SSM = 1024
D_CONF = 1024
N_HEADS = 16
HEAD_DIM = 64
N_GROUPS = 2
N_STATE = 128
CHUNK = 128
SSM_K = 4
CONF_K = 31
D_XBC = D_SSM + 2 * N_GROUPS * N_STATE
D_IN = 5648
R_XBC, R_Z, R_GLU, R_CG, R_DT, D_INR = 0, 1536, 2560, 4608, 5632, 5760
LN_EPS = 1e-5
RMS_EPS = 1e-5
ALPHA = 2.0 ** 0.25
ADAM_LR, ADAM_B1, ADAM_B2, ADAM_EPS, ADAM_WD, ADAM_STEP = 0.001, 0.9, 0.999, 1e-08, 0.01, 10
NEG_BIG = -1e30
LANES = 128
SUBLANES = 8
VMEM_LIMIT = 56 * 1024 * 1024
HIGHEST = lax.Precision.HIGHEST
NT_DIMS = (((1,), (1,)), ((), ()))
TN_DIMS = (((0,), (0,)), ((), ()))


def _sig(v):
    return jax.nn.sigmoid(v)


def _dsilu(v, s):
    return s * (1.0 + v * (1.0 - s))


def _ln_stats(v):
    mu = jnp.mean(v, axis=-1, keepdims=True)
    c = v - mu
    var = jnp.mean(c * c, axis=-1, keepdims=True)
    rstd = lax.rsqrt(var + LN_EPS)
    return c * rstd, rstd


def _ln_bwd(dxhat, xhat, rstd):
    m1 = jnp.mean(dxhat, axis=-1, keepdims=True)
    m2 = jnp.mean(dxhat * xhat, axis=-1, keepdims=True)
    return rstd * (dxhat - m1 - xhat * m2)


def _softplus(v):
    return jnp.maximum(v, 0.0) + jnp.log1p(jnp.exp(-jnp.abs(v)))


def _colsum(v):
    return jnp.sum(v, axis=0, keepdims=True)


def _dot(a, b):
    return jnp.dot(a, b, preferred_element_type=F32)


def _dot_nt(a, b):
    return lax.dot_general(a, b, NT_DIMS, preferred_element_type=F32)


def _tile(tl, c, rev_of=None):
    if rev_of is None:
        return pl.BlockSpec((tl, c), lambda i: (i, 0))
    return pl.BlockSpec((tl, c), lambda i: (rev_of - 1 - i, 0))


def _full(shape, single=False):
    nd = len(shape)
    if single:
        return pl.BlockSpec(shape, lambda i: (0,) * nd, pipeline_mode=pl.Buffered(1))
    return pl.BlockSpec(shape, lambda i: (0,) * nd)


def _params(seq=True):
    return pltpu.CompilerParams(dimension_semantics=("arbitrary",) if seq else ("parallel",), vmem_limit_bytes=VMEM_LIMIT)


def _ln_inproj(x, g, b, w4, later, tl):
    n_tok = x.shape[0]
    n_t = n_tok // tl
    n_l = len(later)
    plan, own_plan = _gather_plans([(rows, GATHER_CHUNKS[a]) for a, (_, rows, _) in enumerate(BIG)][1:])

    def body(x_ref, g_ref, b_ref, w4_ref, *rest):
        part_refs = rest[0:n_l]
        h0b_ref, xbc_ref, z_ref, glu_ref, cg_ref, dtr_ref = rest[n_l:n_l + 6]
        all_refs, w_out = rest[n_l + 6:2 * n_l + 6], rest[2 * n_l + 6]
        w_ref, w_sem = rest[2 * n_l + 7:2 * n_l + 9]
        sems = rest[2 * n_l + 9:]
        i = pl.program_id(0)
        keep = pltpu.make_async_copy(w_ref, w_out, w_sem)

        @pl.when(i == 0)
        def _():
            _gather_stage(0, part_refs, all_refs, plan, own_plan, sems)
            for kcol, o, wd in SEGS:
                lo, hi = o, o + wd
                while lo < hi:
                    sh = lo // W_IN_SH
                    e = min(hi, (sh + 1) * W_IN_SH)
                    w_ref[:, kcol + lo - o:kcol + e - o] = w4_ref[sh, :, lo - sh * W_IN_SH:e - sh * W_IN_SH]
                    lo = e
            w_ref[:, R_DT + N_HEADS:D_INR] = jnp.zeros((D_MODEL, D_INR - R_DT - N_HEADS), BF16)
            keep.start()

        xhat, _ = _ln_stats(x_ref[...])
        hb = (xhat * g_ref[...] + b_ref[...]).astype(BF16)
        h0b_ref[...] = hb
        xbc_ref[...] = _dot(hb, w_ref[:, R_XBC:R_Z])
        z_ref[...] = _dot(hb, w_ref[:, R_Z:R_GLU])
        glu_ref[...] = _dot(hb, w_ref[:, R_GLU:R_CG])
        cg_ref[...] = _dot(hb, w_ref[:, R_CG:R_DT])
        dtr_ref[...] = _dot(hb, w_ref[:, R_DT:D_INR])

        @pl.when(i == (3 * n_t) // 4)
        def _():
            _gather_stage(1, part_refs, all_refs, plan, own_plan, sems)

        @pl.when(i == n_t - 1)
        def _():
            _gather_stage(2, part_refs, all_refs, plan, own_plan, sems)
            keep.wait()

    widths = (D_MODEL, D_XBC, D_SSM, 2 * D_CONF, D_CONF, LANES)
    dtypes = (BF16, F32, F32, F32, F32, F32)
    return pl.pallas_call(
        body, grid=(n_t,), name="ln_inproj",
        in_specs=[_tile(tl, D_MODEL), _full((1, D_MODEL)), _full((1, D_MODEL)), IN_VMEM] + [IN_VMEM] * n_l,
        out_specs=[_tile(tl, w) for w in widths] + [ANY] * (n_l + 1),
        out_shape=[jax.ShapeDtypeStruct((n_tok, w), dt) for w, dt in zip(widths, dtypes)]
        + [jax.ShapeDtypeStruct((N_CHIPS,) + a.shape, a.dtype) for a in later]
        + [jax.ShapeDtypeStruct((D_MODEL, D_INR), BF16)],
        scratch_shapes=[pltpu.VMEM((D_MODEL, D_INR), BF16), pltpu.SemaphoreType.DMA] + _gather_sems(plan, own_plan),
        compiler_params=_params(),
    )(x, g, b, w4, *later)


def _chunk_common(adt_c):
    row = lax.broadcasted_iota(jnp.int32, (CHUNK, CHUNK), 0)
    col = lax.broadcasted_iota(jnp.int32, (CHUNK, CHUNK), 1)
    tril = row >= col
    acs = jnp.dot(tril.astype(F32), adt_c, precision=HIGHEST, preferred_element_type=F32)
    last = acs[CHUNK - 1:CHUNK, :]
    return dict(row=row, col=col, tril=tril, lo=col < HEAD_DIM, acs=acs, acs_t=acs.T, e=jnp.exp(acs),
                dec=jnp.exp(last - acs), cd=jnp.exp(last))


def _decay_mask(cm, h):
    return jnp.exp(jnp.where(cm["tril"], cm["acs"][:, h:h + 1] - cm["acs_t"][h:h + 1, :], NEG_BIG))


def _head_lane_matrix():
    return (jnp.arange(D_SSM)[None, :] // HEAD_DIM == jnp.arange(LANES)[:, None]).astype(BF16)


def _per_head_lanes(v, exp_ref):
    hi = v.astype(BF16)
    lo = (v - hi.astype(F32)).astype(BF16)
    return _dot(hi, exp_ref[...]) + _dot(lo, exp_ref[...])


SSM_BWD_SHIFTS = (1, 2, 3)


def _ssd_fwd(xbc_in, z, dtr, cw, cb, dtb, alog, dsk, gnorm, tl):
    n_tok = xbc_in.shape[0]
    nq = tl // CHUNK

    def body(xin_ref, z_ref, dtr_ref, cw_ref, cb_ref, dtb_ref, alog_ref, dsk_ref, gn_ref, exp_ref,
             pre_ref, y_ref, yssm_ref, hprev_ref, buf, hst):
        @pl.when(pl.program_id(0) == 0)
        def _():
            buf[0:SUBLANES, :] = jnp.zeros((SUBLANES, D_XBC), F32)
            hst[...] = jnp.zeros_like(hst)

        buf[SUBLANES:SUBLANES + tl, :] = xin_ref[...]
        pre = cb_ref[...] + jnp.zeros((tl, D_XBC), F32)
        for k in range(SSM_K):
            off = SUBLANES - (SSM_K - 1) + k
            pre = pre + buf[off:off + tl, :] * cw_ref[k:k + 1, :]
        buf[0:SUBLANES, :] = buf[tl:tl + SUBLANES, :]
        pre_ref[...] = pre
        xbc = pre * _sig(pre)
        dt = _softplus(dtr_ref[...] + dtb_ref[...])
        a = -jnp.exp(alog_ref[...])
        adt = dt * a
        for q in range(nq):
            r0 = q * CHUNK
            cm = _chunk_common(adt[r0:r0 + CHUNK, :])
            dt_x = _per_head_lanes(dt[r0:r0 + CHUNK, :], exp_ref)
            e_x = _per_head_lanes(cm["e"], exp_ref)
            dec_x = _per_head_lanes(cm["dec"], exp_ref)
            for g in range(N_GROUPS):
                bg = xbc[r0:r0 + CHUNK, D_SSM + g * N_STATE:D_SSM + (g + 1) * N_STATE].astype(BF16)
                cg_ = xbc[r0:r0 + CHUNK, D_SSM + (N_GROUPS + g) * N_STATE:D_SSM + (N_GROUPS + g + 1) * N_STATE].astype(BF16)
                gm = _dot_nt(cg_, bg)
                for k in range(N_HEADS // N_GROUPS // 2):
                    ha = (N_HEADS // N_GROUPS) * g + 2 * k
                    c0 = ha * HEAD_DIM
                    xh2 = xbc[r0:r0 + CHUNK, c0:c0 + LANES]
                    x2 = xh2 * dt_x[:, c0:c0 + LANES]
                    x2b = x2.astype(BF16)
                    ya = _dot((gm * _decay_mask(cm, ha)).astype(BF16), x2b)
                    yb = _dot((gm * _decay_mask(cm, ha + 1)).astype(BF16), x2b)
                    h2 = hst[c0:c0 + LANES, :]
                    hprev_ref[q, c0:c0 + LANES, :] = h2
                    z2 = _dot_nt(cg_, h2.astype(BF16))
                    y2 = jnp.where(cm["lo"], ya, yb) + z2 * e_x[:, c0:c0 + LANES] + dsk_ref[:, c0:c0 + LANES] * xh2
                    y_ref[r0:r0 + CHUNK, c0:c0 + LANES] = y2
                    s2 = _dot((x2 * dec_x[:, c0:c0 + LANES]).T.astype(BF16), bg)
                    cd2 = jnp.where(cm["row"] < HEAD_DIM, cm["cd"][:, ha:ha + 1], cm["cd"][:, ha + 1:ha + 2])
                    hst[c0:c0 + LANES, :] = cd2 * h2 + s2
        yv = y_ref[...]
        zv = z_ref[...]
        yz = yv * (zv * _sig(zv))
        gw = D_SSM // N_GROUPS
        for g in range(N_GROUPS):
            seg = yz[:, g * gw:(g + 1) * gw]
            r = lax.rsqrt(jnp.mean(seg * seg, axis=-1, keepdims=True) + RMS_EPS)
            yssm_ref[:, g * gw:(g + 1) * gw] = (seg * r * gn_ref[:, g * gw:(g + 1) * gw]).astype(BF16)

    return pl.pallas_call(
        body, grid=(n_tok // tl,), name="ssd_fwd",
        in_specs=[_tile(tl, D_XBC), _tile(tl, D_SSM), _tile(tl, LANES), _full((SUBLANES, D_XBC)), _full((1, D_XBC)),
                  _full((1, LANES)), _full((1, LANES)), _full((1, D_SSM)), _full((1, D_SSM)), _full((LANES, D_SSM))],
        out_specs=[_tile(tl, D_XBC), _tile(tl, D_SSM), _tile(tl, D_SSM),
                   pl.BlockSpec((nq, D_SSM, N_STATE), lambda i: (i, 0, 0))],
        out_shape=[jax.ShapeDtypeStruct((n_tok, D_XBC), F32), jax.ShapeDtypeStruct((n_tok, D_SSM), F32),
                   jax.ShapeDtypeStruct((n_tok, D_SSM), BF16), jax.ShapeDtypeStruct((n_tok // CHUNK, D_SSM, N_STATE), F32)],
        scratch_shapes=[pltpu.VMEM((tl + SUBLANES, D_XBC), F32), pltpu.VMEM((D_SSM, N_STATE), F32)],
        compiler_params=_params(),
    )(xbc_in, z, dtr, cw, cb, dtb, alog, dsk, gnorm, _head_lane_matrix())


def _ssd_bwd(dys, y, z, pre, xbc_in, dtr, hprev, cw, dtb, alog, dsk, gnorm, specs, grads_b, tl):
    n_tok = y.shape[0]
    n_t = n_tok // tl
    nq = tl // CHUNK
    n_g, n_r = len(grads_b), len(specs)

    def body(dys_ref, y_ref, z_ref, pre_ref, xin_ref, dtr_ref, hprev_ref, cw_ref, dtb_ref, alog_ref, dsk_ref, gn_ref,
             exp_ref, redx_ref, redq_ref, *rest):
        g_refs, rest = rest[0:n_g], rest[n_g:]
        dxin_ref, dz_ref, ddtr_ref, dcw_ref, dcb_ref, dgn_ref, ddsk_ref, da_ref, ddtb_ref = rest[0:9]
        recvs, rest = rest[9:9 + n_r], rest[9 + n_r:]
        dxs, dh, cs_s, dskc, shifts, dwp, sums_q, sums_s = rest[0:8]
        sems = rest[8:]
        i = pl.program_id(0)

        @pl.when(i == 0)
        def _():
            _direct_stage(0, specs, g_refs, recvs, sems)
            dcw_ref[...] = jnp.zeros_like(dcw_ref)
            dwp[...] = jnp.zeros_like(dwp)
            dcb_ref[...] = jnp.zeros_like(dcb_ref)
            dgn_ref[...] = jnp.zeros_like(dgn_ref)
            da_ref[...] = jnp.zeros_like(da_ref)
            ddtb_ref[...] = jnp.zeros_like(ddtb_ref)
            dskc[...] = jnp.zeros_like(dskc)
            dh[...] = jnp.zeros_like(dh)
            dxs[tl:tl + SUBLANES, :] = jnp.zeros((SUBLANES, D_XBC), F32)

        yv = y_ref[...]
        zv = z_ref[...]
        dysv = dys_ref[...]
        sz = _sig(zv)
        silz = zv * sz
        yz = yv * silz
        gw = D_SSM // N_GROUPS
        dyz_parts = []
        for g in range(N_GROUPS):
            sl = slice(g * gw, (g + 1) * gw)
            seg = yz[:, sl]
            r = lax.rsqrt(jnp.mean(seg * seg, axis=-1, keepdims=True) + RMS_EPS)
            yzn = seg * r
            dgn_ref[:, sl] += _colsum(dysv[:, sl] * yzn)
            dyzn = dysv[:, sl] * gn_ref[:, sl]
            dyz_parts.append(r * (dyzn - yzn * jnp.mean(dyzn * yzn, axis=-1, keepdims=True)))
        dyz = jnp.concatenate(dyz_parts, axis=1)
        dy = dyz * silz
        dz_ref[...] = (dyz * yv * _dsilu(zv, sz)).astype(BF16)

        prev = pre_ref[...]
        sp = _sig(prev)
        xbc = prev * sp
        dskc[...] += _colsum(dy * xbc[:, 0:D_SSM])
        dt_in = dtr_ref[...] + dtb_ref[...]
        dt = _softplus(dt_in)
        dsp = _sig(dt_in)
        a = -jnp.exp(alog_ref[...])
        adt = dt * a
        for q in reversed(range(nq)):
            r0 = q * CHUNK
            cm = _chunk_common(adt[r0:r0 + CHUNK, :])
            row, col, lo = cm["row"], cm["col"], cm["lo"]
            triu = (col >= row).astype(F32)
            dt_c = dt[r0:r0 + CHUNK, :]
            dt_x = _per_head_lanes(dt_c, exp_ref)
            e_x = _per_head_lanes(cm["e"], exp_ref)
            dec_x = _per_head_lanes(cm["dec"], exp_ref)
            dcd_row = jnp.zeros((1, LANES), F32)
            for g in range(N_GROUPS):
                bcol = D_SSM + g * N_STATE
                ccol = D_SSM + (N_GROUPS + g) * N_STATE
                bg = xbc[r0:r0 + CHUNK, bcol:bcol + N_STATE].astype(BF16)
                cg_ = xbc[r0:r0 + CHUNK, ccol:ccol + N_STATE].astype(BF16)
                gm = _dot_nt(cg_, bg)
                dgm = jnp.zeros((CHUNK, CHUNK), F32)
                dbg = jnp.zeros((CHUNK, N_STATE), F32)
                dcg = jnp.zeros((CHUNK, N_STATE), F32)
                for k in range(N_HEADS // N_GROUPS // 2):
                    ha = (N_HEADS // N_GROUPS) * g + 2 * k
                    hb = ha + 1
                    c0 = ha * HEAD_DIM
                    xh2 = xbc[r0:r0 + CHUNK, c0:c0 + LANES]
                    dt2 = dt_x[:, c0:c0 + LANES]
                    x2 = xh2 * dt2
                    x2b = x2.astype(BF16)
                    la = _decay_mask(cm, ha)
                    lb = _decay_mask(cm, hb)
                    ma = gm * la
                    mb = gm * lb
                    dy2 = dy[r0:r0 + CHUNK, c0:c0 + LANES]
                    dy2b = dy2.astype(BF16)
                    dma = _dot_nt(jnp.where(lo, dy2, 0.0).astype(BF16), x2b)
                    dmb = _dot_nt(jnp.where(lo, 0.0, dy2).astype(BF16), x2b)
                    dx2 = jnp.where(lo, _dot(ma.T.astype(BF16), dy2b), _dot(mb.T.astype(BF16), dy2b))
                    dgm = dgm + dma * la + dmb * lb
                    sums_q[:, ha * CHUNK:(ha + 1) * CHUNK] = (dma * ma).astype(BF16)
                    sums_q[:, hb * CHUNK:(hb + 1) * CHUNK] = (dmb * mb).astype(BF16)
                    h2 = hprev_ref[q, c0:c0 + LANES, :]
                    h2b = h2.astype(BF16)
                    dz2 = dy2 * e_x[:, c0:c0 + LANES]
                    dcg = dcg + _dot(dz2.astype(BF16), h2b)
                    sums_s[0, :, c0:c0 + LANES] = (dz2 * _dot_nt(cg_, h2b)).astype(BF16)
                    dhn = dh[c0:c0 + LANES, :]
                    dhnb = dhn.astype(BF16)
                    cd_a = cm["cd"][:, ha:ha + 1]
                    cd_b = cm["cd"][:, hb:hb + 1]
                    top = row < HEAD_DIM
                    hh = dhn * h2
                    dcd_a = jnp.sum(_colsum(jnp.where(top, hh, 0.0)), axis=1, keepdims=True)
                    dcd_b = jnp.sum(_colsum(jnp.where(top, 0.0, hh)), axis=1, keepdims=True)
                    dcd_row = dcd_row + jnp.where(col[0:1, :] == ha, dcd_a * cd_a, 0.0) + jnp.where(col[0:1, :] == hb, dcd_b * cd_b, 0.0)
                    dh[c0:c0 + LANES, :] = jnp.where(top, cd_a, cd_b) * dhn + _dot(dz2.T.astype(BF16), cg_)
                    w2 = _dot_nt(bg, dhnb)
                    dec2 = dec_x[:, c0:c0 + LANES]
                    dx2 = dx2 + dec2 * w2
                    sums_s[1, :, c0:c0 + LANES] = (x2 * w2).astype(BF16)
                    dbg = dbg + _dot((x2 * dec2).astype(BF16), dhnb)
                    sums_s[2, :, c0:c0 + LANES] = (dx2 * xh2).astype(BF16)
                    dxs[r0:r0 + CHUNK, c0:c0 + LANES] = dx2 * dt2 + dsk_ref[:, c0:c0 + LANES] * dy2
                dgmb = dgm.astype(BF16)
                dxs[r0:r0 + CHUNK, bcol:bcol + N_STATE] = dbg + _dot(dgm.T.astype(BF16), cg_)
                dxs[r0:r0 + CHUNK, ccol:ccol + N_STATE] = dcg + _dot(dgmb, bg)
            q_all = sums_q[...]
            q_cols = _dot(jnp.ones((SUBLANES, CHUNK), BF16), q_all)
            cs_s[...] = jnp.zeros_like(cs_s)
            for h in range(N_HEADS):
                cs_s[h:h + 1, :] = q_cols[0:1, h * CHUNK:(h + 1) * CHUNK]
            de = _dot(sums_s[0], redx_ref[...])
            dd = _dot(sums_s[1], redx_ref[...]) * cm["dec"]
            ddtx = _dot(sums_s[2], redx_ref[...])
            is_last = row == CHUNK - 1
            dacs = _dot(q_all, redq_ref[...]) - cs_s[...].T + de - dd + jnp.where(is_last, dcd_row + _colsum(dd), 0.0)
            dadt = jnp.dot(triu, dacs, precision=HIGHEST, preferred_element_type=F32)
            da_ref[...] += _colsum(dadt * dt_c)
            ddtr_c = (dadt * a + ddtx) * dsp[r0:r0 + CHUNK, :]
            ddtr_ref[r0:r0 + CHUNK, :] = ddtr_c.astype(BF16)
            ddtb_ref[...] += _colsum(ddtr_c)

        dpre = dxs[0:tl, :] * _dsilu(prev, sp)
        dxs[0:tl, :] = dpre
        dcb_ref[...] += _colsum(dpre)
        _shift_copies(dxs, shifts, tl, SSM_BWD_SHIFTS)

        def strip(rb, carry):
            i0 = pl.multiple_of(rb * CONV_RS, CONV_RS)
            for c0 in range(0, D_XBC, CONV_CS):
                xin_s = xin_ref[pl.ds(i0, CONV_RS), c0:c0 + CONV_CS]
                acc = jnp.zeros((CONV_RS, CONV_CS), F32)
                for k in range(SSM_K):
                    sh = _tap_rows(dxs, shifts, SSM_K - 1 - k, i0, c0, SSM_BWD_SHIFTS)
                    acc = acc + sh * cw_ref[k:k + 1, c0:c0 + CONV_CS]
                    t = xin_s * sh
                    dwp[k * SUBLANES:(k + 1) * SUBLANES, c0:c0 + CONV_CS] += _fold_rows(t)
                dxin_ref[pl.ds(i0, CONV_RS), c0:c0 + CONV_CS] = acc.astype(BF16)
            return carry

        lax.fori_loop(0, tl // CONV_RS, strip, 0)
        dxs[tl:tl + SUBLANES, :] = dxs[0:SUBLANES, :]

        @pl.when(i == n_t - 1)
        def _():
            for k in range(SSM_K):
                dcw_ref[k:k + 1, :] = _colsum(dwp[k * SUBLANES:(k + 1) * SUBLANES, :])
            da_ref[...] = da_ref[...] * a
            sel = (lax.broadcasted_iota(jnp.int32, (D_SSM, LANES), 0) // HEAD_DIM
                   == lax.broadcasted_iota(jnp.int32, (D_SSM, LANES), 1)).astype(F32)
            rows = jnp.broadcast_to(dskc[...], (SUBLANES, D_SSM))
            ddsk_ref[...] = jnp.dot(rows, sel, precision=HIGHEST, preferred_element_type=F32)[0:1, :]
            _direct_stage(1, specs, g_refs, recvs, sems)

    head_lanes = _head_lane_matrix()
    per_head = (jnp.arange(N_HEADS * CHUNK)[:, None] // CHUNK == jnp.arange(LANES)[None, :]).astype(BF16)
    rev = functools.partial(_tile, tl, rev_of=n_t)
    return pl.pallas_call(
        body, grid=(n_t,), name="ssd_bwd",
        in_specs=[rev(D_SSM), rev(D_SSM), rev(D_SSM), rev(D_XBC), rev(D_XBC), rev(LANES),
                  pl.BlockSpec((nq, D_SSM, N_STATE), lambda i: (n_t - 1 - i, 0, 0)),
                  _full((SUBLANES, D_XBC)), _full((1, LANES)), _full((1, LANES)), _full((1, D_SSM)), _full((1, D_SSM)),
                  _full((LANES, D_SSM)), _full((D_SSM, LANES)), _full((N_HEADS * CHUNK, LANES))] + [IN_VMEM] * n_g,
        out_specs=[rev(D_XBC), rev(D_SSM), rev(LANES), _full((SUBLANES, D_XBC)), _full((1, D_XBC)), _full((1, D_SSM)),
                   _full((1, LANES)), _full((1, LANES)), _full((1, LANES))] + [ANY] * n_r,
        out_shape=[jax.ShapeDtypeStruct((n_tok, D_XBC), BF16), jax.ShapeDtypeStruct((n_tok, D_SSM), BF16),
                   jax.ShapeDtypeStruct((n_tok, LANES), BF16), jax.ShapeDtypeStruct((SUBLANES, D_XBC), F32),
                   jax.ShapeDtypeStruct((1, D_XBC), F32), jax.ShapeDtypeStruct((1, D_SSM), F32),
                   jax.ShapeDtypeStruct((1, LANES), F32), jax.ShapeDtypeStruct((1, LANES), F32),
                   jax.ShapeDtypeStruct((1, LANES), F32)]
        + [jax.ShapeDtypeStruct((N_DEV, rows, cols), BF16) for rows, cols, _, _ in specs],
        scratch_shapes=[pltpu.VMEM((tl + SUBLANES, D_XBC), F32), pltpu.VMEM((D_SSM, N_STATE), F32),
                        pltpu.VMEM((CHUNK, LANES), F32), pltpu.VMEM((1, D_SSM), F32),
                        pltpu.VMEM((len(SSM_BWD_SHIFTS), tl, D_XBC), F32), pltpu.VMEM((SSM_K * SUBLANES, D_XBC), F32),
                        pltpu.VMEM((CHUNK, N_HEADS * CHUNK), BF16), pltpu.VMEM((3, CHUNK, D_SSM), BF16)]
        + _direct_sems(specs),
        compiler_params=_params(),
    )(dys, y, z, pre, xbc_in, dtr, hprev, cw, dtb, alog, dsk, gnorm, head_lanes, head_lanes.T, per_head, *grads_b)


CONF_HALO = 32
CONV_RS = 32
CONV_CS = 256


ALL_SHIFTS = tuple(range(1, SUBLANES))


def _shift_copies(buf, shifts, n_rows, residues=ALL_SHIFTS):
    for j, r in enumerate(residues):
        shifts[j, 0:n_rows, :] = buf[r:r + n_rows, :]


def _fold_rows(t):
    part = t[0:SUBLANES]
    for j in range(1, t.shape[0] // SUBLANES):
        part = part + t[j * SUBLANES:(j + 1) * SUBLANES]
    return part


def _tap_rows(buf, shifts, off, i0, c0, residues=ALL_SHIFTS):
    q, r = divmod(off, SUBLANES)
    rows = pl.ds(pl.multiple_of(i0 + SUBLANES * q, SUBLANES), CONV_RS)
    if r == 0:
        return buf[rows, c0:c0 + CONV_CS]
    return shifts[residues.index(r), rows, c0:c0 + CONV_CS]


def _conf_fwd(glu, cgate, bglu, cw, cb, lg, lb, tl):
    n_tok = glu.shape[0]

    def body(glu_ref, cg_ref, bglu_ref, cw_ref, cb_ref, lg_ref, lb_ref, u0_ref, u1_ref, yc_ref, buf, shifts):
        @pl.when(pl.program_id(0) == 0)
        def _():
            buf[0:CONF_HALO, :] = jnp.zeros((CONF_HALO, D_CONF), F32)

        gl = glu_ref[...] + bglu_ref[...]
        u0 = gl[:, 0:D_CONF] * _sig(gl[:, D_CONF:2 * D_CONF])
        u0_ref[...] = u0
        buf[CONF_HALO:CONF_HALO + tl, :] = u0
        _shift_copies(buf, shifts, tl + CONF_HALO - SUBLANES)

        def strip(rb, carry):
            i0 = pl.multiple_of(rb * CONV_RS, CONV_RS)
            for c0 in range(0, D_CONF, CONV_CS):
                acc = jnp.broadcast_to(cb_ref[:, c0:c0 + CONV_CS], (CONV_RS, CONV_CS))
                for k in range(CONF_K):
                    acc = acc + _tap_rows(buf, shifts, CONF_HALO - (CONF_K - 1) + k, i0, c0) * cw_ref[k:k + 1, c0:c0 + CONV_CS]
                u1_ref[pl.ds(i0, CONV_RS), c0:c0 + CONV_CS] = acc
            return carry

        lax.fori_loop(0, tl // CONV_RS, strip, 0)
        buf[0:CONF_HALO, :] = buf[tl:tl + CONF_HALO, :]
        xhat, _ = _ln_stats(u1_ref[...])
        n = xhat * lg_ref[...] + lb_ref[...]
        cgv = cg_ref[...]
        yc_ref[...] = (n * _sig(n) * (cgv * _sig(cgv))).astype(BF16)

    return pl.pallas_call(
        body, grid=(n_tok // tl,), name="conf_fwd",
        in_specs=[_tile(tl, 2 * D_CONF), _tile(tl, D_CONF), _full((1, 2 * D_CONF)), _full((CONF_HALO, D_CONF)),
                  _full((1, D_CONF)), _full((1, D_CONF)), _full((1, D_CONF))],
        out_specs=[_tile(tl, D_CONF)] * 3,
        out_shape=[jax.ShapeDtypeStruct((n_tok, D_CONF), F32), jax.ShapeDtypeStruct((n_tok, D_CONF), F32),
                   jax.ShapeDtypeStruct((n_tok, D_CONF), BF16)],
        scratch_shapes=[pltpu.VMEM((tl + CONF_HALO, D_CONF), F32),
                        pltpu.VMEM((SUBLANES - 1, tl + CONF_HALO - SUBLANES, D_CONF), F32)],
        compiler_params=_params(),
    )(glu, cgate, bglu, cw, cb, lg, lb)


def _conf_bwd(dyc, dyc_block, u0, u1, glu, cgate, bglu, cw, lg, lb, specs, grads_b, tl):
    n_tok = glu.shape[0]
    n_t = n_tok // tl
    n_g, n_r = len(grads_b), len(specs)

    def body(dyc_ref, u0_ref, u1_ref, glu_ref, cg_ref, bglu_ref, cw_ref, lg_ref, lb_ref, *rest):
        g_refs, rest = rest[0:n_g], rest[n_g:]
        dglu_ref, dcg_ref, dcw_ref, dbglu_ref, small_ref = rest[0:5]
        recvs, rest = rest[5:5 + n_r], rest[5 + n_r:]
        buf, shifts, du0_s, dwp = rest[0:4]
        sems = rest[4:]

        @pl.when(pl.program_id(0) == 0)
        def _():
            _direct_stage(0, specs, g_refs, recvs, sems)
            dwp[...] = jnp.zeros_like(dwp)
            dbglu_ref[...] = jnp.zeros_like(dbglu_ref)
            small_ref[...] = jnp.zeros_like(small_ref)
            buf[tl:tl + CONF_HALO, :] = jnp.zeros((CONF_HALO, D_CONF), F32)

        xhat, rstd = _ln_stats(u1_ref[...])
        n = xhat * lg_ref[...] + lb_ref[...]
        sn = _sig(n)
        cgv = cg_ref[...]
        scg = _sig(cgv)
        dycv = dyc_ref[...]
        dcg_ref[...] = (dycv * (n * sn) * _dsilu(cgv, scg)).astype(BF16)
        dn = dycv * (cgv * scg) * _dsilu(n, sn)
        small_ref[0:1, :] += _colsum(dn * xhat)
        small_ref[1:2, :] += _colsum(dn)
        du1 = _ln_bwd(dn * lg_ref[...], xhat, rstd)
        small_ref[2:3, :] += _colsum(du1)
        buf[0:tl, :] = du1
        _shift_copies(buf, shifts, tl + CONF_HALO - SUBLANES)

        def strip(rb, carry):
            i0 = pl.multiple_of(rb * CONV_RS, CONV_RS)
            for c0 in range(0, D_CONF, CONV_CS):
                u0s = u0_ref[pl.ds(i0, CONV_RS), c0:c0 + CONV_CS]
                acc = jnp.zeros((CONV_RS, CONV_CS), F32)
                for k in range(CONF_K):
                    sh = _tap_rows(buf, shifts, CONF_K - 1 - k, i0, c0)
                    acc = acc + sh * cw_ref[k:k + 1, c0:c0 + CONV_CS]
                    t = u0s * sh
                    dwp[k * SUBLANES:(k + 1) * SUBLANES, c0:c0 + CONV_CS] += _fold_rows(t)
                du0_s[pl.ds(i0, CONV_RS), c0:c0 + CONV_CS] = acc
            return carry

        lax.fori_loop(0, tl // CONV_RS, strip, 0)
        du0 = du0_s[...]
        buf[tl:tl + CONF_HALO, :] = buf[0:CONF_HALO, :]
        gl = glu_ref[...] + bglu_ref[...]
        sg = _sig(gl[:, D_CONF:2 * D_CONF])
        dgv = du0 * sg
        dgg = du0 * gl[:, 0:D_CONF] * sg * (1.0 - sg)
        dglu_ref[:, 0:D_CONF] = dgv.astype(BF16)
        dglu_ref[:, D_CONF:2 * D_CONF] = dgg.astype(BF16)
        dbglu_ref[:, 0:D_CONF] += _colsum(dgv)
        dbglu_ref[:, D_CONF:2 * D_CONF] += _colsum(dgg)

        @pl.when(pl.program_id(0) == n_t - 1)
        def _():
            for k in range(CONF_HALO):
                dcw_ref[k:k + 1, :] = _colsum(dwp[k * SUBLANES:(k + 1) * SUBLANES, :])
            _direct_stage(1, specs, g_refs, recvs, sems)

    rev = functools.partial(_tile, tl, rev_of=n_t)
    return pl.pallas_call(
        body, grid=(n_t,), name="conf_bwd",
        in_specs=[pl.BlockSpec((tl, D_CONF), lambda i: (n_t - 1 - i, dyc_block)),
                  rev(D_CONF), rev(D_CONF), rev(2 * D_CONF), rev(D_CONF), _full((1, 2 * D_CONF)),
                  _full((CONF_HALO, D_CONF)), _full((1, D_CONF)), _full((1, D_CONF))] + [IN_VMEM] * n_g,
        out_specs=[rev(2 * D_CONF), rev(D_CONF), _full((CONF_HALO, D_CONF)), _full((1, 2 * D_CONF)), _full((SUBLANES, D_CONF))]
        + [ANY] * n_r,
        out_shape=[jax.ShapeDtypeStruct((n_tok, 2 * D_CONF), BF16), jax.ShapeDtypeStruct((n_tok, D_CONF), BF16),
                   jax.ShapeDtypeStruct((CONF_HALO, D_CONF), F32), jax.ShapeDtypeStruct((1, 2 * D_CONF), F32),
                   jax.ShapeDtypeStruct((SUBLANES, D_CONF), F32)]
        + [jax.ShapeDtypeStruct((N_DEV, rows, cols), BF16) for rows, cols, _, _ in specs],
        scratch_shapes=[pltpu.VMEM((tl + CONF_HALO, D_CONF), F32),
                        pltpu.VMEM((SUBLANES - 1, tl + CONF_HALO - SUBLANES, D_CONF), F32),
                        pltpu.VMEM((tl, D_CONF), F32), pltpu.VMEM((CONF_HALO * SUBLANES, D_CONF), F32)] + _direct_sems(specs),
        compiler_params=_params(),
    )(dyc, u0, u1, glu, cgate, bglu, cw, lg, lb, *grads_b)


def _tail(x, yssm, yconf, p, tgt, vec, w_out, wpg, wpp, tl):
    n_tok = x.shape[0]

    def body(x_ref, ys_ref, yc_ref, p_ref, t_ref, vec_ref, wo_ref, wg_ref, wp_ref,
             dmix_ref, dr1_ref, dr1b_ref, h1b_ref, dgb_ref, dpb_ref, small_ref, loss_ref):
        @pl.when(pl.program_id(0) == 0)
        def _():
            small_ref[...] = jnp.zeros_like(small_ref)
            loss_ref[...] = jnp.zeros_like(loss_ref)

        xh0, _ = _ln_stats(x_ref[...])
        h0 = xh0 * vec_ref[0:1, :] + vec_ref[1:2, :]
        out = _dot(ys_ref[...], wo_ref[0:D_SSM, :]) + _dot(yc_ref[...], wo_ref[D_SSM:D_SSM + D_CONF, :]) + vec_ref[2:3, :]
        xh1, rstd1 = _ln_stats(ALPHA * h0 + out)
        h1 = xh1 * vec_ref[3:4, :] + vec_ref[4:5, :]
        h1b = h1.astype(BF16)
        h1b_ref[...] = h1b
        gate = _sig(_dot(h1b, wg_ref[...]))
        ple = _dot(p_ref[...].astype(BF16), wp_ref[...])
        xh2, rstd2 = _ln_stats(ALPHA * h1 + gate * ple)
        h2 = xh2 * vec_ref[5:6, :] + vec_ref[6:7, :]
        diff = h2 - t_ref[...]
        part = jnp.sum(jnp.sum(diff * diff, axis=1, keepdims=True), axis=0, keepdims=True) * (0.5 / D_MODEL)
        loss_ref[...] += jnp.broadcast_to(part, loss_ref.shape)
        dh2 = diff * (1.0 / D_MODEL)
        small_ref[3:4, :] += _colsum(dh2 * xh2)
        small_ref[4:5, :] += _colsum(dh2)
        dr2 = _ln_bwd(dh2 * vec_ref[5:6, :], xh2, rstd2)
        dgpre = (dr2 * ple * gate * (1.0 - gate)).astype(BF16)
        dgb_ref[...] = dgpre
        dpb_ref[...] = (dr2 * gate).astype(BF16)
        dh1 = ALPHA * dr2 + _dot_nt(dgpre, wg_ref[...])
        small_ref[1:2, :] += _colsum(dh1 * xh1)
        small_ref[2:3, :] += _colsum(dh1)
        dr1 = _ln_bwd(dh1 * vec_ref[3:4, :], xh1, rstd1)
        small_ref[0:1, :] += _colsum(dr1)
        dr1_ref[...] = dr1
        dr1b = dr1.astype(BF16)
        dr1b_ref[...] = dr1b
        dmix_ref[...] = _dot_nt(dr1b, wo_ref[...])

    d_mix = D_SSM + D_CONF
    return pl.pallas_call(
        body, grid=(n_tok // tl,), name="tail",
        in_specs=[_tile(tl, D_MODEL), _tile(tl, D_SSM), _tile(tl, D_CONF), _tile(tl, D_PLE), _tile(tl, D_MODEL),
                  _full((SUBLANES, D_MODEL)), _full((d_mix, D_MODEL), True), _full((D_MODEL, D_MODEL), True),
                  _full((D_PLE, D_MODEL), True)],
        out_specs=[_tile(tl, d_mix), _tile(tl, D_MODEL), _tile(tl, D_MODEL), _tile(tl, D_MODEL), _tile(tl, D_MODEL),
                   _tile(tl, D_MODEL), _full((SUBLANES, D_MODEL)), _full((SUBLANES, LANES))],
        out_shape=[jax.ShapeDtypeStruct((n_tok, d_mix), F32), jax.ShapeDtypeStruct((n_tok, D_MODEL), F32),
                   jax.ShapeDtypeStruct((n_tok, D_MODEL), BF16), jax.ShapeDtypeStruct((n_tok, D_MODEL), BF16),
                   jax.ShapeDtypeStruct((n_tok, D_MODEL), BF16), jax.ShapeDtypeStruct((n_tok, D_MODEL), BF16),
                   jax.ShapeDtypeStruct((SUBLANES, D_MODEL), F32), jax.ShapeDtypeStruct((SUBLANES, LANES), F32)],
        compiler_params=_params(),
    )(x, yssm, yconf, p, tgt, vec, w_out, wpg, wpp)


def _inproj_bwd(dxin, dz, dglu, dcg, ddtr, dr1, x, g, b, w_r, tl):
    n_tok = x.shape[0]

    def body(dxin_ref, dz_ref, dglu_ref, dcg_ref, ddtr_ref, dr1_ref, x_ref, g_ref, b_ref, w_ref,
             dx_ref, dpb_ref, small_ref):
        @pl.when(pl.program_id(0) == 0)
        def _():
            small_ref[...] = jnp.zeros_like(small_ref)

        dh0 = ALPHA * dr1_ref[...]
        for ref, lo, hi in ((dxin_ref, R_XBC, R_Z), (dz_ref, R_Z, R_GLU), (dglu_ref, R_GLU, R_CG), (dcg_ref, R_CG, R_DT),
                            (ddtr_ref, R_DT, D_INR)):
            piece = ref[...].astype(BF16)
            dpb_ref[:, lo:hi] = piece
            dh0 = dh0 + _dot_nt(piece, w_ref[:, lo:hi])
        xhat, rstd = _ln_stats(x_ref[...])
        small_ref[0:1, :] += _colsum(dh0 * xhat)
        small_ref[1:2, :] += _colsum(dh0)
        dx_ref[...] = _ln_bwd(dh0 * g_ref[...], xhat, rstd)

    return pl.pallas_call(
        body, grid=(n_tok // tl,), name="inproj_bwd",
        in_specs=[_tile(tl, D_XBC), _tile(tl, D_SSM), _tile(tl, 2 * D_CONF), _tile(tl, D_CONF), _tile(tl, LANES),
                  _tile(tl, D_MODEL), _tile(tl, D_MODEL), _full((1, D_MODEL)), _full((1, D_MODEL)),
                  _full((D_MODEL, D_INR), True)],
        out_specs=[_tile(tl, D_MODEL), _tile(tl, D_INR), _full((SUBLANES, D_MODEL))],
        out_shape=[jax.ShapeDtypeStruct((n_tok, D_MODEL), F32), jax.ShapeDtypeStruct((n_tok, D_INR), BF16),
                   jax.ShapeDtypeStruct((SUBLANES, D_MODEL), F32)],
        compiler_params=_params(),
    )(dxin, dz, dglu, dcg, ddtr, dr1, x, g, b, w_r)


def _tn_matmul(a, b, name, tn, tl, out_dtype=F32):
    n_tok, m = a.shape
    n = b.shape[1]
    n_l = n_tok // tl
    direct = out_dtype == F32

    def body(a_ref, b_ref, o_ref, *scratch):
        acc = o_ref if direct else scratch[0]

        @pl.when(pl.program_id(1) == 0)
        def _():
            acc[...] = jnp.zeros_like(acc)

        acc[...] += lax.dot_general(a_ref[...], b_ref[...], TN_DIMS, preferred_element_type=F32)
        if not direct:
            @pl.when(pl.program_id(1) == n_l - 1)
            def _():
                o_ref[...] = acc[...].astype(out_dtype)

    return pl.pallas_call(
        body, grid=(n // tn, n_l), name=name,
        in_specs=[pl.BlockSpec((tl, m), lambda j, l: (l, 0)), pl.BlockSpec((tl, tn), lambda j, l: (l, j))],
        out_specs=pl.BlockSpec((m, tn), lambda j, l: (0, j)),
        out_shape=jax.ShapeDtypeStruct((m, n), out_dtype),
        scratch_shapes=[] if direct else [pltpu.VMEM((m, tn), F32)],
        compiler_params=pltpu.CompilerParams(dimension_semantics=("parallel", "arbitrary"), vmem_limit_bytes=VMEM_LIMIT),
    )(a, b)


def _dw_in(a, b, spack, tn, tl):
    n_tok, m = a.shape
    n = b.shape[1]
    n_j, n_l = n // tn, n_tok // tl

    def body(a_ref, b_ref, s_ref, o_ref, small_all, send_sems, recv_sems, loc_sem):
        j, l = pl.program_id(0), pl.program_id(1)

        @pl.when(jnp.logical_and(j == 0, l == 0))
        def _():
            _gather8_stage(0, s_ref, small_all, send_sems, recv_sems, loc_sem)

        @pl.when(l == 0)
        def _():
            o_ref[...] = jnp.zeros_like(o_ref)

        o_ref[...] += lax.dot_general(a_ref[...], b_ref[...], TN_DIMS, preferred_element_type=F32)

        @pl.when(jnp.logical_and(j == n_j - 1, l == n_l - 1))
        def _():
            _gather8_stage(1, s_ref, small_all, send_sems, recv_sems, loc_sem)

    dma = pltpu.SemaphoreType.DMA
    return pl.pallas_call(
        body, grid=(n_j, n_l), name="dw_in",
        in_specs=[pl.BlockSpec((tl, m), lambda j, l: (l, 0)), pl.BlockSpec((tl, tn), lambda j, l: (l, j)), IN_VMEM],
        out_specs=[pl.BlockSpec((m, tn), lambda j, l: (0, j)), ANY],
        out_shape=[jax.ShapeDtypeStruct((m, n), F32), jax.ShapeDtypeStruct((N_DEV,) + spack.shape, F32)],
        scratch_shapes=[dma((N_DEV - 1,)), dma((N_DEV - 1,)), dma],
        compiler_params=pltpu.CompilerParams(dimension_semantics=("arbitrary", "arbitrary"), vmem_limit_bytes=VMEM_LIMIT),
    )(a, b, spack)


def _pad_rows(a, rows):
    return jnp.pad(a, ((0, rows - a.shape[0]), (0, 0)))


def _pad_lanes(a):
    return jnp.pad(a, ((0, 0), (0, LANES - a.shape[1])))


def _local_grads(x, p, tgt, w4, later, ssm_cw, conf_cw, sm):
    n_tok = x.shape[0]
    tl = min(256, n_tok)
    tlm = min(512, n_tok)
    row = lambda v: v.reshape(1, -1)
    g_e, b_e = row(sm["ln_emb_g"]), row(sm["ln_emb_b"])
    h0b, xbc_in, z, glu, cgate, dtr, w_out_all, wpg_all, wpp_all, w_r = _ln_inproj(x, g_e, b_e, w4, later, tl)
    w_out_b = w_out_all.reshape(D_SSM + D_CONF, D_MODEL)
    wpg_b = wpg_all.reshape(D_MODEL, D_MODEL)
    wpp_b = wpp_all.transpose(1, 0, 2).reshape(D_PLE, D_MODEL)

    cw4 = _pad_rows(ssm_cw, SUBLANES)
    dtb, alog = _pad_lanes(sm["dt_bias"]), _pad_lanes(sm["a_log"])
    dsk = jnp.repeat(sm["d_skip"], HEAD_DIM, axis=1)
    pre, y, yssm, hprev = _ssd_fwd(xbc_in, z, dtr, cw4, sm["ssm_conv_b"], dtb, alog, dsk, sm["ssm_norm_g"], tl)

    cw31 = _pad_rows(conf_cw, CONF_HALO)
    u0, u1, yconf = _conf_fwd(glu, cgate, sm["b_glu"], cw31, sm["conf_conv_b"], sm["conf_ln_g"], sm["conf_ln_b"], tl)

    vec = jnp.concatenate([g_e, b_e, sm["b_out"], sm["ln1_g"], sm["ln1_b"], sm["ln2_g"], sm["ln2_b"],
                           jnp.zeros((1, D_MODEL), F32)], axis=0)
    dmix, dr1, dr1b, h1b, dgb, dpb, small_t, loss = _tail(
        x, yssm, yconf, p, tgt, vec, w_out_b, wpg_b, wpp_b, tlm)

    tlw = min(2048, n_tok)
    dwo = [_tn_matmul(yssm, dr1b, "dw_out_ssm", D_MODEL, tlw, BF16), _tn_matmul(yconf, dr1b, "dw_out_conf", D_MODEL, tlw, BF16)]
    dwp_ = [_tn_matmul(h1b, dgb, "dw_ple_gate", D_MODEL, tlw, BF16),
            _tn_matmul(p.astype(BF16), dpb, "dw_ple_proj", D_MODEL, tlw, BF16)]
    dglu, dcg, dcw31, dbglu, small_c, dw_out = _conf_bwd(dmix, D_SSM // D_CONF, u0, u1, glu, cgate, sm["b_glu"], cw31,
                                                          sm["conf_ln_g"], sm["conf_ln_b"], [W_OUT_DIRECT], dwo, tl)
    dxin, dz, ddtr, dcw4, dcb4, dgn, ddsk, dalog, ddtb, dwpg, dwpp = _ssd_bwd(
        dmix, y, z, pre, xbc_in, dtr, hprev, cw4, dtb, alog, dsk, sm["ssm_norm_g"], [WPG_DIRECT, WPP_DIRECT], dwp_, tl)
    dx, dprojb, small_e = _inproj_bwd(dxin, dz, dglu, dcg, ddtr, dr1, x, g_e, b_e, w_r, tlm)
    spack = _pack_small(dict(
        ln_emb_g=small_e[0], ln_emb_b=small_e[1], ssm_conv_w=dcw4[0:SSM_K], ssm_conv_b=dcb4,
        dt_bias=ddtb[:, 0:N_HEADS], a_log=dalog[:, 0:N_HEADS], d_skip=ddsk[:, 0:N_HEADS], ssm_norm_g=dgn, b_glu=dbglu,
        conf_conv_w=dcw31[0:CONF_K], conf_conv_b=small_c[2:3], conf_ln_g=small_c[0:1], conf_ln_b=small_c[1:2],
        b_out=small_t[0:1], ln1_g=small_t[1:2], ln1_b=small_t[2:3], ln2_g=small_t[3:4], ln2_b=small_t[4:5]), loss[0, 0])
    dw_r, small_all = _dw_in(h0b, dprojb, spack, D_INR // 3, tlw)
    return dx, dict(w_in=dw_r, w_out=dw_out, w_ple_gate=dwpg, w_ple_proj=dwpp, small=small_all)


N_CHIPS = 4
N_DEV = 8
W_IN_SH = D_IN // N_CHIPS
BIG = (("w_in", D_MODEL, W_IN_SH), ("w_out", (D_SSM + D_CONF) // N_CHIPS, D_MODEL),
       ("w_ple_gate", D_MODEL // N_CHIPS, D_MODEL), ("w_ple_proj", D_PLE, D_MODEL // N_CHIPS))
SEGS = ((R_XBC, 0, D_SSM), (R_XBC + D_SSM, 2048, 256), (R_XBC + D_SSM + 256, 2304, 256), (R_Z, 1024, D_SSM),
        (R_GLU, 2576, 2 * D_CONF), (R_CG, 4624, D_CONF), (R_DT, 2560, N_HEADS))
ROWS_CW4 = 2
ROWS_CW31 = 8
ROWS_CONV = 16
SMALL_ROWS = 56
SMALL_LAYOUT = (("ln_emb_g", 0, 1), ("ln_emb_b", 1, 1), ("ssm_conv_b", 2, 2), ("dt_bias", 4, 1), ("a_log", 5, 1),
                ("d_skip", 6, 1), ("ssm_norm_g", 7, 1), ("b_glu", 8, 2), ("conf_conv_b", 10, 1), ("conf_ln_g", 11, 1),
                ("conf_ln_b", 12, 1), ("b_out", 13, 1), ("ln1_g", 14, 1), ("ln1_b", 15, 1), ("ln2_g", 16, 1), ("ln2_b", 17, 1))
CONV_LAYOUT = (("ssm_conv_w", 18, 6, (SSM_K, D_XBC)), ("conf_conv_w", 24, 31, (CONF_K, D_CONF)))


def _rows_of(v, rows):
    flat = v.reshape(-1)
    return jnp.pad(flat, (0, rows * D_MODEL - flat.shape[0])).reshape(rows, D_MODEL)


LOSS_ROW = 55


def _pack_small(d, loss_share):
    parts = [_rows_of(d[n], r) for n, _, r in SMALL_LAYOUT] + [_rows_of(d[n], r) for n, _, r, _ in CONV_LAYOUT]
    parts.append(_rows_of(loss_share, SMALL_ROWS - LOSS_ROW))
    return jnp.concatenate(parts, axis=0)


def _gather8_stage(stage, s_ref, all_ref, send_sems, recv_sems, loc_sem):
    x, y, c = _my_place()
    me = 4 * x + 2 * y + c
    copies = [pltpu.make_async_copy(s_ref, all_ref.at[me], loc_sem)]
    for mk in range(1, N_DEV):
        peer = (x ^ (mk >> 2), y ^ ((mk >> 1) & 1), c ^ (mk & 1))
        copies.append(_remote(s_ref, all_ref.at[me], send_sems.at[mk - 1], recv_sems.at[mk - 1], peer))
    for cp in copies:
        cp.start() if stage == 0 else cp.wait()


def _row_chunks(rows, n):
    return [(j * (rows // n), rows // n) for j in range(n)]


def _my_place():
    return lax.axis_index("x"), lax.axis_index("y"), lax.axis_index("c")


MESH_ID = pl.DeviceIdType.MESH
ANY = pl.BlockSpec(memory_space=pl.ANY)
IN_VMEM = pl.BlockSpec(memory_space=pltpu.VMEM)
CHIP_FLIPS = ((1, 0), (0, 1), (1, 1))


def _remote(src, dst, send_sem, recv_sem, peer):
    return pltpu.make_async_remote_copy(src, dst, send_sem, recv_sem, device_id=peer, device_id_type=MESH_ID)


GATHER_CHUNKS = (4, 2, 1, 1)


def _gather_plans(kinds):
    plan = [(a, o, n, rows // 2) for a, (rows, ch) in enumerate(kinds) for o, n in _row_chunks(rows // 2, ch)]
    own_plan = [(a, o, n) for a, (rows, ch) in enumerate(kinds) for o, n in _row_chunks(rows, 2 * ch)]
    return plan, own_plan


def _gather_sems(plan, own_plan):
    hop = pltpu.SemaphoreType.DMA((3, len(plan)))
    return [hop, hop, hop, hop, pltpu.SemaphoreType.DMA((len(own_plan),))]


def _gather_stage(stage, ins, outs, plan, own_plan, sems):
    send_a, recv_a, send_b, recv_b, loc = sems
    x, y, c = _my_place()
    s = 2 * x + y
    sibling = (x, y, 1 - c)
    own = [pltpu.make_async_copy(ins[a].at[pl.ds(o, n)], outs[a].at[s, pl.ds(o, n)], loc.at[j])
           for j, (a, o, n) in enumerate(own_plan)]
    first, arrive, passed, arrive_b = [], [], [], []
    for k, (fx, fy) in enumerate(CHIP_FLIPS):
        peer = (x ^ fx, y ^ fy, c)
        sk = 2 * (x ^ fx) + (y ^ fy)
        for j, (a, o, n, h) in enumerate(plan):
            mine = pl.ds(pl.multiple_of(c * h + o, 16), n)
            theirs = pl.ds(pl.multiple_of((1 - c) * h + o, 16), n)
            first.append(_remote(ins[a].at[mine], outs[a].at[s, mine], send_a.at[k, j], recv_a.at[k, j], peer))
            land = outs[a].at[sk, mine]
            arrive.append(_remote(land, land, send_a.at[k, j], recv_a.at[k, j], peer))
            passed.append(_remote(land, land, send_b.at[k, j], recv_b.at[k, j], sibling))
            land_b = outs[a].at[sk, theirs]
            arrive_b.append(_remote(land_b, land_b, send_b.at[k, j], recv_b.at[k, j], sibling))
    if stage == 0:
        for cp in own + first:
            cp.start()
    elif stage == 1:
        for got, fwd in zip(arrive, passed):
            got.wait_recv()
            fwd.start()
    else:
        for got in arrive_b:
            got.wait_recv()
        for cp in first + passed:
            cp.wait_send()
        for cp in own:
            cp.wait()


W_OUT_DIRECT = (512, D_MODEL, 2, ((0, 0, 0), (0, 512, 0), (1, 0, 0), (1, 512, 0)))
WPG_DIRECT = (256, D_MODEL, 1, ((0, 0, 0), (0, 256, 0), (0, 512, 0), (0, 768, 0)))
WPP_DIRECT = (D_PLE, 256, 1, ((1, 0, 0), (1, 0, 256), (1, 0, 512), (1, 0, 768)))


def _direct_sems(specs):
    out = []
    for _, _, copies, _ in specs:
        out += [pltpu.SemaphoreType.DMA((N_CHIPS, 2, copies)), pltpu.SemaphoreType.DMA((N_DEV, copies)),
                pltpu.SemaphoreType.DMA((copies,))]
    return out


def _direct_stage(stage, specs, srcs, recvs, sems):
    x, y, c = _my_place()
    me = 4 * x + 2 * y + c
    s = 2 * x + y
    for r, (rows, cols, copies, where) in enumerate(specs):
        send_sems, recv_sems, loc_sems = sems[3 * r:3 * r + 3]
        for k, (o, n) in enumerate(_row_chunks(rows, copies)):
            for t, (si, row0, col0) in enumerate(where):
                src = srcs[si].at[pl.ds(row0 + o, n), pl.ds(col0, cols)]
                dst = recvs[r].at[me, pl.ds(o, n)]
                for cc in range(2):
                    to_self = jnp.logical_and(s == t, c == cc)
                    away = _remote(src, dst, send_sems.at[t, cc, k], recv_sems.at[me, k], (t // 2, t % 2, cc))
                    here = pltpu.make_async_copy(src, dst, loc_sems.at[k])

                    @pl.when(to_self)
                    def _():
                        here.start() if stage == 0 else here.wait()

                    @pl.when(jnp.logical_not(to_self))
                    def _():
                        away.start() if stage == 0 else away.wait_send()
            if stage == 1:
                for j in range(N_DEV):
                    land = recvs[r].at[j, pl.ds(o, n)]

                    @pl.when(j != me)
                    def _():
                        _remote(land, land, send_sems.at[0, 0, k], recv_sems.at[j, k], (0, 0, 0)).wait_recv()


def _gather_weights(w_in_b, conv_f):
    plan, own_plan = _gather_plans([(BIG[0][1], GATHER_CHUNKS[0])])

    def body(w_ref, conv_ref, w_all, conv_all, *sems):
        conv_send, conv_recv, conv_loc = sems[5:]
        x, y, c = _my_place()
        s = 2 * x + y
        conv_own = pltpu.make_async_copy(conv_ref, conv_all.at[s], conv_loc)
        conv_out = [_remote(conv_ref, conv_all.at[s], conv_send.at[k], conv_recv.at[k], (x ^ fx, y ^ fy, c))
                    for k, (fx, fy) in enumerate(CHIP_FLIPS)]
        _gather_stage(0, [w_ref], [w_all], plan, own_plan, sems[0:5])
        for cp in [conv_own] + conv_out:
            cp.start()
        _gather_stage(1, [w_ref], [w_all], plan, own_plan, sems[0:5])
        _gather_stage(2, [w_ref], [w_all], plan, own_plan, sems[0:5])
        for cp in conv_out:
            cp.wait()
        conv_own.wait()

    arrays = [w_in_b, conv_f]
    return pl.pallas_call(
        body, name="gather_weights", in_specs=[IN_VMEM] * 2, out_specs=[ANY] * 2,
        out_shape=[jax.ShapeDtypeStruct((N_CHIPS,) + a.shape, a.dtype) for a in arrays],
        scratch_shapes=_gather_sems(plan, own_plan)
        + [pltpu.SemaphoreType.DMA((3,)), pltpu.SemaphoreType.DMA((3,)), pltpu.SemaphoreType.DMA],
    )(*arrays)


SHARE_CHUNKS = 4


OWN_TILES = (((0, 1024), (1536, 512)), ((1024, 512), (1920, 640), (2560, 256), (5632, 128)), ((2688, 1536),), ((4096, 1536),))
OWN_W = 1536
RED_CHUNKS = 8


def _own_columns(t, rows):
    if t == 0:
        return rows[:, 0:W_IN_SH]
    if t == 1:
        return jnp.concatenate([rows[:, 516:1152], rows[:, 0:512], rows[:, 1408:1424], rows[:, 1152:1400]], axis=1)
    first = (2808 - 2688) if t == 2 else (4220 - 4096)
    return rows[:, first:first + W_IN_SH]


def _reduce_w_in(dw_r):
    half = D_MODEL // 2
    rc = half // RED_CHUNKS
    max_r = max(len(r) for r in OWN_TILES)

    def body(g_ref, got_ref, theirs, a_buf, b_buf, sum_buf, d2d_send, d2d_recv, ld_sems, ici_send, ici_recv, own_sems):
        x, y, c = _my_place()
        s = 2 * x + y
        sibling = (x, y, 1 - c)
        swap = [_remote(g_ref.at[pl.ds(pl.multiple_of((1 - c) * half + k * rc, SUBLANES), rc)], theirs.at[pl.ds(k * rc, rc)],
                        d2d_send.at[k], d2d_recv.at[k], sibling) for k in range(RED_CHUNKS)]
        for cp in swap:
            cp.start()

        def tiles(k, t, r):
            k0, wd = OWN_TILES[t][r]
            at = sum(w_ for _, w_ in OWN_TILES[t][:r])
            src = sum_buf.at[k, pl.ds(0, rc), pl.ds(k0, wd)]
            dst = got_ref.at[s, pl.ds(k * rc, rc), pl.ds(at, wd)]
            return (_remote(src, dst, ici_send.at[t, r, k], ici_recv.at[s, r, k], (t // 2, t % 2, c)),
                    pltpu.make_async_copy(src, dst, own_sems.at[r, k]))

        mine = [pltpu.make_async_copy(g_ref.at[pl.ds(pl.multiple_of(c * half + k * rc, SUBLANES), rc)], a_buf.at[k % 2],
                                      ld_sems.at[0, k % 2]) for k in range(RED_CHUNKS)]
        other = [pltpu.make_async_copy(theirs.at[pl.ds(k * rc, rc)], b_buf.at[k % 2], ld_sems.at[1, k % 2])
                 for k in range(RED_CHUNKS)]
        mine[0].start()
        for k in range(RED_CHUNKS):
            slot = k % 2
            swap[k].wait_recv()
            other[k].start()
            if k + 1 < RED_CHUNKS:
                mine[k + 1].start()
            mine[k].wait()
            other[k].wait()
            sum_buf[k] = (a_buf[slot] + b_buf[slot]).astype(BF16)
            for t in range(N_CHIPS):
                for r in range(len(OWN_TILES[t])):
                    away, here = tiles(k, t, r)

                    @pl.when(t != s)
                    def _():
                        away.start()

                    @pl.when(t == s)
                    def _():
                        here.start()

        for k in range(RED_CHUNKS):
            swap[k].wait_send()
            for t in range(N_CHIPS):
                for r, (_, wd) in enumerate(OWN_TILES[t]):
                    away, here = tiles(k, t, r)
                    at = sum(w_ for _, w_ in OWN_TILES[t][:r])

                    @pl.when(t != s)
                    def _():
                        away.wait_send()

                    @pl.when(t == s)
                    def _():
                        here.wait()
                        for j in range(N_CHIPS):
                            if j != t:
                                land = got_ref.at[j, pl.ds(k * rc, rc), pl.ds(at, wd)]
                                _remote(land, land, ici_send.at[0, 0, 0], ici_recv.at[j, r, k], (0, 0, 0)).wait_recv()

    dma = pltpu.SemaphoreType.DMA
    got, _ = pl.pallas_call(
        body, name="reduce_w_in", in_specs=[ANY], out_specs=[ANY, ANY],
        out_shape=[jax.ShapeDtypeStruct((N_CHIPS, half, OWN_W), BF16), jax.ShapeDtypeStruct((half, D_INR), F32)],
        scratch_shapes=[pltpu.VMEM((2, rc, D_INR), F32), pltpu.VMEM((2, rc, D_INR), F32), pltpu.VMEM((RED_CHUNKS, rc, D_INR), BF16),
                        dma((RED_CHUNKS,)), dma((RED_CHUNKS,)), dma((2, 2)), dma((N_CHIPS, max_r, RED_CHUNKS)),
                        dma((N_CHIPS, max_r, RED_CHUNKS)), dma((max_r, RED_CHUNKS))],
        compiler_params=pltpu.CompilerParams(vmem_limit_bytes=VMEM_LIMIT),
    )(dw_r)
    return got


def _add_slots(got):
    _, rows, cols = got.shape
    tr = 128

    def body(g_ref, o_ref):
        o_ref[...] = ((g_ref[0].astype(F32) + g_ref[1].astype(F32)) + g_ref[2].astype(F32)) + g_ref[3].astype(F32)

    return pl.pallas_call(body, grid=(rows // tr,), name="add_chips_w_in",
                          in_specs=[pl.BlockSpec((N_CHIPS, tr, cols), lambda i: (0, i, 0))],
                          out_specs=pl.BlockSpec((tr, cols), lambda i: (i, 0)),
                          out_shape=jax.ShapeDtypeStruct((rows, cols), F32), compiler_params=_params(seq=False))(got)


def _share_halves(tot):
    n_rows, half = tot.shape
    plan = _row_chunks(half, SHARE_CHUNKS)

    def body(t_ref, both, send_sems, recv_sems, loc):
        x, y, c = _my_place()
        rows = pl.ds(0, n_rows)

        def place(o, n):
            return both.at[rows, pl.ds(pl.multiple_of(c * half + o, LANES), n)]

        own = [pltpu.make_async_copy(t_ref.at[rows, pl.ds(o, n)], place(o, n), loc.at[j]) for j, (o, n) in enumerate(plan)]
        sends = [_remote(t_ref.at[rows, pl.ds(o, n)], place(o, n), send_sems.at[j], recv_sems.at[j], (x, y, 1 - c))
                 for j, (o, n) in enumerate(plan)]
        for cp in own + sends:
            cp.start()
        for cp in sends:
            cp.wait()
        for cp in own:
            cp.wait()

    dma = pltpu.SemaphoreType.DMA
    return pl.pallas_call(
        body, name="share_halves", in_specs=[IN_VMEM], out_specs=ANY,
        out_shape=jax.ShapeDtypeStruct((n_rows, 2 * half), F32),
        scratch_shapes=[dma((len(plan),)), dma((len(plan),)), dma((len(plan),))],
    )(tot)


def _adam_math(w, g, m, v):
    m = ADAM_B1 * m + (1.0 - ADAM_B1) * g
    v = ADAM_B2 * v + (1.0 - ADAM_B2) * (g * g)
    m_hat = m / (1.0 - ADAM_B1 ** ADAM_STEP)
    v_hat = v / (1.0 - ADAM_B2 ** ADAM_STEP)
    return -ADAM_LR * (m_hat / (jnp.sqrt(v_hat) + ADAM_EPS) + ADAM_WD * w), m, v


def _adam(w, g, m, v, name):
    rows, cols = w.shape

    def body(w_ref, g_ref, m_ref, v_ref, d_ref, nm_ref, nv_ref):
        d_ref[...], nm_ref[...], nv_ref[...] = _adam_math(w_ref[...], g_ref[...], m_ref[...], v_ref[...])

    if rows <= 256 or rows % 256 == 0:
        tr = min(rows, 256)
        n_blocks, spec = rows // tr, pl.BlockSpec((tr, cols), lambda i: (i, 0))
    else:
        n_blocks, spec = cols // 256, pl.BlockSpec((rows, 256), lambda i: (0, i))
    return pl.pallas_call(body, grid=(n_blocks,), name=name, in_specs=[spec] * 4, out_specs=[spec] * 3,
                          out_shape=[jax.ShapeDtypeStruct(w.shape, F32)] * 3, compiler_params=_params(seq=False))(w, g, m, v)


def _adam_sum(w, parts, m, v, name):
    rows, cols = w.shape
    tr = rows if rows <= 256 else 256

    def body(w_ref, p_ref, m_ref, v_ref, g_ref, d_ref, nm_ref, nv_ref):
        g = p_ref[0].astype(F32)
        for j in range(1, N_DEV):
            g = g + p_ref[j].astype(F32)
        g_ref[...] = g
        d_ref[...], nm_ref[...], nv_ref[...] = _adam_math(w_ref[...], g, m_ref[...], v_ref[...])

    spec = pl.BlockSpec((tr, cols), lambda i: (i, 0))
    return pl.pallas_call(
        body, grid=(rows // tr,), name=name,
        in_specs=[spec, pl.BlockSpec((N_DEV, tr, cols), lambda i: (0, i, 0)), spec, spec], out_specs=[spec] * 4,
        out_shape=[jax.ShapeDtypeStruct(w.shape, F32)] * 4, compiler_params=_params(seq=False))(w, parts, m, v)


def _adam_small(parts, ws, ms, vs):
    n_w = len(SMALL_LAYOUT)

    def body(p_ref, *refs):
        w_refs, m_refs, v_refs = refs[0:n_w], refs[n_w:2 * n_w], refs[2 * n_w:3 * n_w]
        sum_ref = refs[3 * n_w]
        outs = refs[3 * n_w + 1:]
        g = p_ref[0]
        for k in range(1, N_DEV):
            g = g + p_ref[k]
        sum_ref[...] = g
        for a, (_, r0, rows) in enumerate(SMALL_LAYOUT):
            width = w_refs[a].shape[1]
            for j in range(rows):
                lo, hi = j * D_MODEL, min((j + 1) * D_MODEL, width)
                gj = sum_ref[r0 + j:r0 + j + 1, 0:hi - lo]
                d, nm, nv = _adam_math(w_refs[a][:, lo:hi], gj, m_refs[a][:, lo:hi], v_refs[a][:, lo:hi])
                for out, val in zip(outs[4 * a:4 * a + 4], (gj, d, nm, nv)):
                    out[:, lo:hi] = val

    shapes = [jax.ShapeDtypeStruct((SMALL_ROWS, D_MODEL), F32)]
    for wa in ws:
        shapes += [jax.ShapeDtypeStruct(wa.shape, F32)] * 4
    res = pl.pallas_call(body, name="adam_small", out_shape=shapes)(parts, *ws, *ms, *vs)
    return res[0], [res[1 + 4 * a:5 + 4 * a] for a in range(n_w)]


def kernel(x, p, ln_emb_g, ln_emb_b, w_in, ssm_conv_w, ssm_conv_b, dt_bias, a_log, d_skip, ssm_norm_g, b_glu, conf_conv_w, conf_conv_b, conf_ln_g, conf_ln_b, w_out, b_out, ln1_g, ln1_b, w_ple_gate, w_ple_proj, ln2_g, ln2_b, loss_target, m_ln_emb_g, m_ln_emb_b, m_w_in, m_ssm_conv_w, m_ssm_conv_b, m_dt_bias, m_a_log, m_d_skip, m_ssm_norm_g, m_b_glu, m_conf_conv_w, m_conf_conv_b, m_conf_ln_g, m_conf_ln_b, m_w_out, m_b_out, m_ln1_g, m_ln1_b, m_w_ple_gate, m_w_ple_proj, m_ln2_g, m_ln2_b, v_ln_emb_g, v_ln_emb_b, v_w_in, v_ssm_conv_w, v_ssm_conv_b, v_dt_bias, v_a_log, v_d_skip, v_ssm_norm_g, v_b_glu, v_conf_conv_w, v_conf_conv_b, v_conf_ln_g, v_conf_ln_b, v_w_out, v_b_out, v_ln1_g, v_ln1_b, v_w_ple_gate, v_w_ple_proj, v_ln2_g, v_ln2_b):
    order = ("ln_emb_g", "ln_emb_b", "w_in", "ssm_conv_w", "ssm_conv_b", "dt_bias", "a_log", "d_skip", "ssm_norm_g", "b_glu",
             "conf_conv_w", "conf_conv_b", "conf_ln_g", "conf_ln_b", "w_out", "b_out", "ln1_g", "ln1_b", "w_ple_gate",
             "w_ple_proj", "ln2_g", "ln2_b")
    w = dict(zip(order, (ln_emb_g, ln_emb_b, w_in, ssm_conv_w, ssm_conv_b, dt_bias, a_log, d_skip, ssm_norm_g, b_glu,
                         conf_conv_w, conf_conv_b, conf_ln_g, conf_ln_b, w_out, b_out, ln1_g, ln1_b, w_ple_gate, w_ple_proj,
                         ln2_g, ln2_b)))
    m = dict(zip(order, (m_ln_emb_g, m_ln_emb_b, m_w_in, m_ssm_conv_w, m_ssm_conv_b, m_dt_bias, m_a_log, m_d_skip,
                         m_ssm_norm_g, m_b_glu, m_conf_conv_w, m_conf_conv_b, m_conf_ln_g, m_conf_ln_b, m_w_out, m_b_out,
                         m_ln1_g, m_ln1_b, m_w_ple_gate, m_w_ple_proj, m_ln2_g, m_ln2_b)))
    v = dict(zip(order, (v_ln_emb_g, v_ln_emb_b, v_w_in, v_ssm_conv_w, v_ssm_conv_b, v_dt_bias, v_a_log, v_d_skip,
                         v_ssm_norm_g, v_b_glu, v_conf_conv_w, v_conf_conv_b, v_conf_ln_g, v_conf_ln_b, v_w_out, v_b_out,
                         v_ln1_g, v_ln1_b, v_w_ple_gate, v_w_ple_proj, v_ln2_g, v_ln2_b)))

    conv_f = jnp.concatenate([_rows_of(w["ssm_conv_w"], ROWS_CW4), _rows_of(w["conf_conv_w"], ROWS_CW31),
                              jnp.zeros((ROWS_CONV - ROWS_CW4 - ROWS_CW31, D_MODEL), F32)], axis=0)
    shards_b = [w[n][0].astype(BF16) for n, _, _ in BIG]
    w_in_all, conv_all = _gather_weights(shards_b[0], conv_f)
    cw4 = conv_all[:, 0:ROWS_CW4].reshape(N_CHIPS, -1)[:, :SSM_K * D_XBC // N_CHIPS]
    cw4 = cw4.reshape(N_CHIPS, SSM_K, D_XBC // N_CHIPS).transpose(1, 0, 2).reshape(SSM_K, D_XBC)
    cw31 = conv_all[:, ROWS_CW4:ROWS_CW4 + ROWS_CW31].reshape(N_CHIPS, -1)[:, :CONF_K * D_CONF // N_CHIPS]
    cw31 = cw31.reshape(N_CHIPS, CONF_K, D_CONF // N_CHIPS).transpose(1, 0, 2).reshape(CONF_K, D_CONF)

    small_names = [n for n, _, _ in SMALL_LAYOUT]
    sm = {n: w[n] for n in small_names}
    dx, grads = _local_grads(x[0], p[0, 0], loss_target[0], w_in_all, shards_b[1:], cw4, cw31, sm)

    chip_i = 2 * lax.axis_index("x") + lax.axis_index("y")
    mine = lax.switch(chip_i, [functools.partial(_own_columns, t) for t in range(N_CHIPS)], _add_slots(_reduce_w_in(grads["w_in"])))
    g_w_in_t = _share_halves(jnp.swapaxes(mine, 0, 1))

    out_g, out_d, out_m, out_v = {}, {}, {}, {}
    as_row = lambda a: a.reshape(1, -1)
    g_s, small_out = _adam_small(grads["small"], [as_row(w[n]) for n in small_names], [as_row(m[n]) for n in small_names],
                                 [as_row(v[n]) for n in small_names])
    loss = g_s[LOSS_ROW, 0]
    tr_ = lambda a: jnp.swapaxes(a, 0, 1)
    n = "w_in"
    d, nm, nv = _adam(tr_(w[n][0]), g_w_in_t, tr_(m[n][0]), tr_(v[n][0]), "adam_" + n)
    out_g[n], out_d[n], out_m[n], out_v[n] = [tr_(a)[None] for a in (g_w_in_t, d, nm, nv)]
    for n, r0, r, shape in CONV_LAYOUT:
        whole = g_s[r0:r0 + r].reshape(-1)[:shape[0] * shape[1]].reshape(shape)
        g = lax.dynamic_slice_in_dim(whole, chip_i * (shape[1] // N_CHIPS), shape[1] // N_CHIPS, axis=1)
        d, nm, nv = _adam(w[n][0], g, m[n][0], v[n][0], "adam_" + n)
        out_g[n], out_d[n], out_m[n], out_v[n] = g[None], d[None], nm[None], nv[None]
    for n, _, _ in BIG[1:]:
        out_g[n], out_d[n], out_m[n], out_v[n] = [a[None] for a in _adam_sum(w[n][0], grads[n], m[n][0], v[n][0], "adam_" + n)]
    for n, four in zip(small_names, small_out):
        out_g[n], out_d[n], out_m[n], out_v[n] = [a.reshape(w[n].shape) for a in four]
    return (loss, dx[None], *[out_g[n] for n in order], *[out_d[n] for n in order], *[out_m[n] for n in order],
            *[out_v[n] for n in order])
```

```python
import functools

import jax
import jax.numpy as jnp
from jax import lax
from jax.experimental import pallas as pl
from jax.experimental.pallas import tpu as pltpu

F32 = jnp.float32
BF16 = jnp.bfloat16

D_MODEL = 1024
D_PLE = 256
D_SSM = 1024
D_CONF = 1024
N_HEADS = 16
HEAD_DIM = 64
N_GROUPS = 2
N_STATE = 128
CHUNK = 128
SSM_K = 4
CONF_K = 31
D_XBC = D_SSM + 2 * N_GROUPS * N_STATE
D_IN = 5648
R_XBC, R_Z, R_GLU, R_CG, R_DT, D_INR = 0, 1536, 2560, 4608, 5632, 5760
LN_EPS = 1e-5
RMS_EPS = 1e-5
ALPHA = 2.0 ** 0.25
ADAM_LR, ADAM_B1, ADAM_B2, ADAM_EPS, ADAM_WD, ADAM_STEP = 0.001, 0.9, 0.999, 1e-08, 0.01, 10
NEG_BIG = -1e30
LANES = 128
SUBLANES = 8
VMEM_LIMIT = 56 * 1024 * 1024
HIGHEST = lax.Precision.HIGHEST
NT_DIMS = (((1,), (1,)), ((), ()))
TN_DIMS = (((0,), (0,)), ((), ()))


def _sig(v):
    return jax.nn.sigmoid(v)


def _dsilu(v, s):
    return s * (1.0 + v * (1.0 - s))


def _ln_stats(v):
    mu = jnp.mean(v, axis=-1, keepdims=True)
    c = v - mu
    var = jnp.mean(c * c, axis=-1, keepdims=True)
    rstd = lax.rsqrt(var + LN_EPS)
    return c * rstd, rstd


def _ln_bwd(dxhat, xhat, rstd):
    m1 = jnp.mean(dxhat, axis=-1, keepdims=True)
    m2 = jnp.mean(dxhat * xhat, axis=-1, keepdims=True)
    return rstd * (dxhat - m1 - xhat * m2)


def _softplus(v):
    return jnp.maximum(v, 0.0) + jnp.log1p(jnp.exp(-jnp.abs(v)))


def _colsum(v):
    return jnp.sum(v, axis=0, keepdims=True)


def _dot(a, b):
    return jnp.dot(a, b, preferred_element_type=F32)


def _dot_nt(a, b):
    return lax.dot_general(a, b, NT_DIMS, preferred_element_type=F32)


def _tile(tl, c, rev_of=None):
    if rev_of is None:
        return pl.BlockSpec((tl, c), lambda i: (i, 0))
    return pl.BlockSpec((tl, c), lambda i: (rev_of - 1 - i, 0))


def _full(shape, single=False):
    nd = len(shape)
    if single:
        return pl.BlockSpec(shape, lambda i: (0,) * nd, pipeline_mode=pl.Buffered(1))
    return pl.BlockSpec(shape, lambda i: (0,) * nd)


def _params(seq=True):
    return pltpu.CompilerParams(dimension_semantics=("arbitrary",) if seq else ("parallel",), vmem_limit_bytes=VMEM_LIMIT)


def _ln_inproj(x, g, b, w4, later, tl):
    n_tok = x.shape[0]
    n_t = n_tok // tl
    n_l = len(later)
    plan, own_plan = _gather_plans([(rows, GATHER_CHUNKS[a]) for a, (_, rows, _) in enumerate(BIG)][1:])

    def body(x_ref, g_ref, b_ref, w4_ref, *rest):
        part_refs = rest[0:n_l]
        h0b_ref, xbc_ref, z_ref, glu_ref, cg_ref, dtr_ref = rest[n_l:n_l + 6]
        all_refs, w_out = rest[n_l + 6:2 * n_l + 6], rest[2 * n_l + 6]
        w_ref, w_sem = rest[2 * n_l + 7:2 * n_l + 9]
        sems = rest[2 * n_l + 9:]
        i = pl.program_id(0)
        keep = pltpu.make_async_copy(w_ref, w_out, w_sem)

        @pl.when(i == 0)
        def _():
            _gather_stage(0, part_refs, all_refs, plan, own_plan, sems)
            for kcol, o, wd in SEGS:
                lo, hi = o, o + wd
                while lo < hi:
                    sh = lo // W_IN_SH
                    e = min(hi, (sh + 1) * W_IN_SH)
                    w_ref[:, kcol + lo - o:kcol + e - o] = w4_ref[sh, :, lo - sh * W_IN_SH:e - sh * W_IN_SH]
                    lo = e
            w_ref[:, R_DT + N_HEADS:D_INR] = jnp.zeros((D_MODEL, D_INR - R_DT - N_HEADS), BF16)
            keep.start()

        xhat, _ = _ln_stats(x_ref[...])
        hb = (xhat * g_ref[...] + b_ref[...]).astype(BF16)
        h0b_ref[...] = hb
        xbc_ref[...] = _dot(hb, w_ref[:, R_XBC:R_Z])
        z_ref[...] = _dot(hb, w_ref[:, R_Z:R_GLU])
        glu_ref[...] = _dot(hb, w_ref[:, R_GLU:R_CG])
        cg_ref[...] = _dot(hb, w_ref[:, R_CG:R_DT])
        dtr_ref[...] = _dot(hb, w_ref[:, R_DT:D_INR])

        @pl.when(i == (3 * n_t) // 4)
        def _():
            _gather_stage(1, part_refs, all_refs, plan, own_plan, sems)

        @pl.when(i == n_t - 1)
        def _():
            _gather_stage(2, part_refs, all_refs, plan, own_plan, sems)
            keep.wait()

    widths = (D_MODEL, D_XBC, D_SSM, 2 * D_CONF, D_CONF, LANES)
    dtypes = (BF16, F32, F32, F32, F32, F32)
    return pl.pallas_call(
        body, grid=(n_t,), name="ln_inproj",
        in_specs=[_tile(tl, D_MODEL), _full((1, D_MODEL)), _full((1, D_MODEL)), IN_VMEM] + [IN_VMEM] * n_l,
        out_specs=[_tile(tl, w) for w in widths] + [ANY] * (n_l + 1),
        out_shape=[jax.ShapeDtypeStruct((n_tok, w), dt) for w, dt in zip(widths, dtypes)]
        + [jax.ShapeDtypeStruct((N_CHIPS,) + a.shape, a.dtype) for a in later]
        + [jax.ShapeDtypeStruct((D_MODEL, D_INR), BF16)],
        scratch_shapes=[pltpu.VMEM((D_MODEL, D_INR), BF16), pltpu.SemaphoreType.DMA] + _gather_sems(plan, own_plan),
        compiler_params=_params(),
    )(x, g, b, w4, *later)


def _chunk_common(adt_c):
    row = lax.broadcasted_iota(jnp.int32, (CHUNK, CHUNK), 0)
    col = lax.broadcasted_iota(jnp.int32, (CHUNK, CHUNK), 1)
    tril = row >= col
    acs = jnp.dot(tril.astype(F32), adt_c, precision=HIGHEST, preferred_element_type=F32)
    last = acs[CHUNK - 1:CHUNK, :]
    return dict(row=row, col=col, tril=tril, lo=col < HEAD_DIM, acs=acs, acs_t=acs.T, e=jnp.exp(acs),
                dec=jnp.exp(last - acs), cd=jnp.exp(last))


def _decay_mask(cm, h):
    return jnp.exp(jnp.where(cm["tril"], cm["acs"][:, h:h + 1] - cm["acs_t"][h:h + 1, :], NEG_BIG))


def _head_lane_matrix():
    return (jnp.arange(D_SSM)[None, :] // HEAD_DIM == jnp.arange(LANES)[:, None]).astype(BF16)


def _per_head_lanes(v, exp_ref):
    hi = v.astype(BF16)
    lo = (v - hi.astype(F32)).astype(BF16)
    return _dot(hi, exp_ref[...]) + _dot(lo, exp_ref[...])


SSM_BWD_SHIFTS = (1, 2, 3)


def _ssd_fwd(xbc_in, z, dtr, cw, cb, dtb, alog, dsk, gnorm, tl):
    n_tok = xbc_in.shape[0]
    nq = tl // CHUNK

    def body(xin_ref, z_ref, dtr_ref, cw_ref, cb_ref, dtb_ref, alog_ref, dsk_ref, gn_ref, exp_ref,
             pre_ref, y_ref, yssm_ref, hprev_ref, buf, hst):
        @pl.when(pl.program_id(0) == 0)
        def _():
            buf[0:SUBLANES, :] = jnp.zeros((SUBLANES, D_XBC), F32)
            hst[...] = jnp.zeros_like(hst)

        buf[SUBLANES:SUBLANES + tl, :] = xin_ref[...]
        pre = cb_ref[...] + jnp.zeros((tl, D_XBC), F32)
        for k in range(SSM_K):
            off = SUBLANES - (SSM_K - 1) + k
            pre = pre + buf[off:off + tl, :] * cw_ref[k:k + 1, :]
        buf[0:SUBLANES, :] = buf[tl:tl + SUBLANES, :]
        pre_ref[...] = pre
        xbc = pre * _sig(pre)
        dt = _softplus(dtr_ref[...] + dtb_ref[...])
        a = -jnp.exp(alog_ref[...])
        adt = dt * a
        for q in range(nq):
            r0 = q * CHUNK
            cm = _chunk_common(adt[r0:r0 + CHUNK, :])
            dt_x = _per_head_lanes(dt[r0:r0 + CHUNK, :], exp_ref)
            e_x = _per_head_lanes(cm["e"], exp_ref)
            dec_x = _per_head_lanes(cm["dec"], exp_ref)
            for g in range(N_GROUPS):
                bg = xbc[r0:r0 + CHUNK, D_SSM + g * N_STATE:D_SSM + (g + 1) * N_STATE].astype(BF16)
                cg_ = xbc[r0:r0 + CHUNK, D_SSM + (N_GROUPS + g) * N_STATE:D_SSM + (N_GROUPS + g + 1) * N_STATE].astype(BF16)
                gm = _dot_nt(cg_, bg)
                for k in range(N_HEADS // N_GROUPS // 2):
                    ha = (N_HEADS // N_GROUPS) * g + 2 * k
                    c0 = ha * HEAD_DIM
                    xh2 = xbc[r0:r0 + CHUNK, c0:c0 + LANES]
                    x2 = xh2 * dt_x[:, c0:c0 + LANES]
                    x2b = x2.astype(BF16)
                    ya = _dot((gm * _decay_mask(cm, ha)).astype(BF16), x2b)
                    yb = _dot((gm * _decay_mask(cm, ha + 1)).astype(BF16), x2b)
                    h2 = hst[c0:c0 + LANES, :]
                    hprev_ref[q, c0:c0 + LANES, :] = h2
                    z2 = _dot_nt(cg_, h2.astype(BF16))
                    y2 = jnp.where(cm["lo"], ya, yb) + z2 * e_x[:, c0:c0 + LANES] + dsk_ref[:, c0:c0 + LANES] * xh2
                    y_ref[r0:r0 + CHUNK, c0:c0 + LANES] = y2
                    s2 = _dot((x2 * dec_x[:, c0:c0 + LANES]).T.astype(BF16), bg)
                    cd2 = jnp.where(cm["row"] < HEAD_DIM, cm["cd"][:, ha:ha + 1], cm["cd"][:, ha + 1:ha + 2])
                    hst[c0:c0 + LANES, :] = cd2 * h2 + s2
        yv = y_ref[...]
        zv = z_ref[...]
        yz = yv * (zv * _sig(zv))
        gw = D_SSM // N_GROUPS
        for g in range(N_GROUPS):
            seg = yz[:, g * gw:(g + 1) * gw]
            r = lax.rsqrt(jnp.mean(seg * seg, axis=-1, keepdims=True) + RMS_EPS)
            yssm_ref[:, g * gw:(g + 1) * gw] = (seg * r * gn_ref[:, g * gw:(g + 1) * gw]).astype(BF16)

    return pl.pallas_call(
        body, grid=(n_tok // tl,), name="ssd_fwd",
        in_specs=[_tile(tl, D_XBC), _tile(tl, D_SSM), _tile(tl, LANES), _full((SUBLANES, D_XBC)), _full((1, D_XBC)),
                  _full((1, LANES)), _full((1, LANES)), _full((1, D_SSM)), _full((1, D_SSM)), _full((LANES, D_SSM))],
        out_specs=[_tile(tl, D_XBC), _tile(tl, D_SSM), _tile(tl, D_SSM),
                   pl.BlockSpec((nq, D_SSM, N_STATE), lambda i: (i, 0, 0))],
        out_shape=[jax.ShapeDtypeStruct((n_tok, D_XBC), F32), jax.ShapeDtypeStruct((n_tok, D_SSM), F32),
                   jax.ShapeDtypeStruct((n_tok, D_SSM), BF16), jax.ShapeDtypeStruct((n_tok // CHUNK, D_SSM, N_STATE), F32)],
        scratch_shapes=[pltpu.VMEM((tl + SUBLANES, D_XBC), F32), pltpu.VMEM((D_SSM, N_STATE), F32)],
        compiler_params=_params(),
    )(xbc_in, z, dtr, cw, cb, dtb, alog, dsk, gnorm, _head_lane_matrix())


def _ssd_bwd(dys, y, z, pre, xbc_in, dtr, hprev, cw, dtb, alog, dsk, gnorm, specs, grads_b, tl):
    n_tok = y.shape[0]
    n_t = n_tok // tl
    nq = tl // CHUNK
    n_g, n_r = len(grads_b), len(specs)

    def body(dys_ref, y_ref, z_ref, pre_ref, xin_ref, dtr_ref, hprev_ref, cw_ref, dtb_ref, alog_ref, dsk_ref, gn_ref,
             exp_ref, redx_ref, redq_ref, *rest):
        g_refs, rest = rest[0:n_g], rest[n_g:]
        dxin_ref, dz_ref, ddtr_ref, dcw_ref, dcb_ref, dgn_ref, ddsk_ref, da_ref, ddtb_ref = rest[0:9]
        recvs, rest = rest[9:9 + n_r], rest[9 + n_r:]
        dxs, dh, cs_s, dskc, shifts, dwp, sums_q, sums_s = rest[0:8]
        sems = rest[8:]
        i = pl.program_id(0)

        @pl.when(i == 0)
        def _():
            _direct_stage(0, specs, g_refs, recvs, sems)
            dcw_ref[...] = jnp.zeros_like(dcw_ref)
            dwp[...] = jnp.zeros_like(dwp)
            dcb_ref[...] = jnp.zeros_like(dcb_ref)
            dgn_ref[...] = jnp.zeros_like(dgn_ref)
            da_ref[...] = jnp.zeros_like(da_ref)
            ddtb_ref[...] = jnp.zeros_like(ddtb_ref)
            dskc[...] = jnp.zeros_like(dskc)
            dh[...] = jnp.zeros_like(dh)
            dxs[tl:tl + SUBLANES, :] = jnp.zeros((SUBLANES, D_XBC), F32)

        yv = y_ref[...]
        zv = z_ref[...]
        dysv = dys_ref[...]
        sz = _sig(zv)
        silz = zv * sz
        yz = yv * silz
        gw = D_SSM // N_GROUPS
        dyz_parts = []
        for g in range(N_GROUPS):
            sl = slice(g * gw, (g + 1) * gw)
            seg = yz[:, sl]
            r = lax.rsqrt(jnp.mean(seg * seg, axis=-1, keepdims=True) + RMS_EPS)
            yzn = seg * r
            dgn_ref[:, sl] += _colsum(dysv[:, sl] * yzn)
            dyzn = dysv[:, sl] * gn_ref[:, sl]
            dyz_parts.append(r * (dyzn - yzn * jnp.mean(dyzn * yzn, axis=-1, keepdims=True)))
        dyz = jnp.concatenate(dyz_parts, axis=1)
        dy = dyz * silz
        dz_ref[...] = (dyz * yv * _dsilu(zv, sz)).astype(BF16)

        prev = pre_ref[...]
        sp = _sig(prev)
        xbc = prev * sp
        dskc[...] += _colsum(dy * xbc[:, 0:D_SSM])
        dt_in = dtr_ref[...] + dtb_ref[...]
        dt = _softplus(dt_in)
        dsp = _sig(dt_in)
        a = -jnp.exp(alog_ref[...])
        adt = dt * a
        for q in reversed(range(nq)):
            r0 = q * CHUNK
            cm = _chunk_common(adt[r0:r0 + CHUNK, :])
            row, col, lo = cm["row"], cm["col"], cm["lo"]
            triu = (col >= row).astype(F32)
            dt_c = dt[r0:r0 + CHUNK, :]
            dt_x = _per_head_lanes(dt_c, exp_ref)
            e_x = _per_head_lanes(cm["e"], exp_ref)
            dec_x = _per_head_lanes(cm["dec"], exp_ref)
            dcd_row = jnp.zeros((1, LANES), F32)
            for g in range(N_GROUPS):
                bcol = D_SSM + g * N_STATE
                ccol = D_SSM + (N_GROUPS + g) * N_STATE
                bg = xbc[r0:r0 + CHUNK, bcol:bcol + N_STATE].astype(BF16)
                cg_ = xbc[r0:r0 + CHUNK, ccol:ccol + N_STATE].astype(BF16)
                gm = _dot_nt(cg_, bg)
                dgm = jnp.zeros((CHUNK, CHUNK), F32)
                dbg = jnp.zeros((CHUNK, N_STATE), F32)
                dcg = jnp.zeros((CHUNK, N_STATE), F32)
                for k in range(N_HEADS // N_GROUPS // 2):
                    ha = (N_HEADS // N_GROUPS) * g + 2 * k
                    hb = ha + 1
                    c0 = ha * HEAD_DIM
                    xh2 = xbc[r0:r0 + CHUNK, c0:c0 + LANES]
                    dt2 = dt_x[:, c0:c0 + LANES]
                    x2 = xh2 * dt2
                    x2b = x2.astype(BF16)
                    la = _decay_mask(cm, ha)
                    lb = _decay_mask(cm, hb)
                    ma = gm * la
                    mb = gm * lb
                    dy2 = dy[r0:r0 + CHUNK, c0:c0 + LANES]
                    dy2b = dy2.astype(BF16)
                    dma = _dot_nt(jnp.where(lo, dy2, 0.0).astype(BF16), x2b)
                    dmb = _dot_nt(jnp.where(lo, 0.0, dy2).astype(BF16), x2b)
                    dx2 = jnp.where(lo, _dot(ma.T.astype(BF16), dy2b), _dot(mb.T.astype(BF16), dy2b))
                    dgm = dgm + dma * la + dmb * lb
                    sums_q[:, ha * CHUNK:(ha + 1) * CHUNK] = (dma * ma).astype(BF16)
                    sums_q[:, hb * CHUNK:(hb + 1) * CHUNK] = (dmb * mb).astype(BF16)
                    h2 = hprev_ref[q, c0:c0 + LANES, :]
                    h2b = h2.astype(BF16)
                    dz2 = dy2 * e_x[:, c0:c0 + LANES]
                    dcg = dcg + _dot(dz2.astype(BF16), h2b)
                    sums_s[0, :, c0:c0 + LANES] = (dz2 * _dot_nt(cg_, h2b)).astype(BF16)
                    dhn = dh[c0:c0 + LANES, :]
                    dhnb = dhn.astype(BF16)
                    cd_a = cm["cd"][:, ha:ha + 1]
                    cd_b = cm["cd"][:, hb:hb + 1]
                    top = row < HEAD_DIM
                    hh = dhn * h2
                    dcd_a = jnp.sum(_colsum(jnp.where(top, hh, 0.0)), axis=1, keepdims=True)
                    dcd_b = jnp.sum(_colsum(jnp.where(top, 0.0, hh)), axis=1, keepdims=True)
                    dcd_row = dcd_row + jnp.where(col[0:1, :] == ha, dcd_a * cd_a, 0.0) + jnp.where(col[0:1, :] == hb, dcd_b * cd_b, 0.0)
                    dh[c0:c0 + LANES, :] = jnp.where(top, cd_a, cd_b) * dhn + _dot(dz2.T.astype(BF16), cg_)
                    w2 = _dot_nt(bg, dhnb)
                    dec2 = dec_x[:, c0:c0 + LANES]
                    dx2 = dx2 + dec2 * w2
                    sums_s[1, :, c0:c0 + LANES] = (x2 * w2).astype(BF16)
                    dbg = dbg + _dot((x2 * dec2).astype(BF16), dhnb)
                    sums_s[2, :, c0:c0 + LANES] = (dx2 * xh2).astype(BF16)
                    dxs[r0:r0 + CHUNK, c0:c0 + LANES] = dx2 * dt2 + dsk_ref[:, c0:c0 + LANES] * dy2
                dgmb = dgm.astype(BF16)
                dxs[r0:r0 + CHUNK, bcol:bcol + N_STATE] = dbg + _dot(dgm.T.astype(BF16), cg_)
                dxs[r0:r0 + CHUNK, ccol:ccol + N_STATE] = dcg + _dot(dgmb, bg)
            q_all = sums_q[...]
            q_cols = _dot(jnp.ones((SUBLANES, CHUNK), BF16), q_all)
            cs_s[...] = jnp.zeros_like(cs_s)
            for h in range(N_HEADS):
                cs_s[h:h + 1, :] = q_cols[0:1, h * CHUNK:(h + 1) * CHUNK]
            de = _dot(sums_s[0], redx_ref[...])
            dd = _dot(sums_s[1], redx_ref[...]) * cm["dec"]
            ddtx = _dot(sums_s[2], redx_ref[...])
            is_last = row == CHUNK - 1
            dacs = _dot(q_all, redq_ref[...]) - cs_s[...].T + de - dd + jnp.where(is_last, dcd_row + _colsum(dd), 0.0)
            dadt = jnp.dot(triu, dacs, precision=HIGHEST, preferred_element_type=F32)
            da_ref[...] += _colsum(dadt * dt_c)
            ddtr_c = (dadt * a + ddtx) * dsp[r0:r0 + CHUNK, :]
            ddtr_ref[r0:r0 + CHUNK, :] = ddtr_c.astype(BF16)
            ddtb_ref[...] += _colsum(ddtr_c)

        dpre = dxs[0:tl, :] * _dsilu(prev, sp)
        dxs[0:tl, :] = dpre
        dcb_ref[...] += _colsum(dpre)
        _shift_copies(dxs, shifts, tl, SSM_BWD_SHIFTS)

        def strip(rb, carry):
            i0 = pl.multiple_of(rb * CONV_RS, CONV_RS)
            for c0 in range(0, D_XBC, CONV_CS):
                xin_s = xin_ref[pl.ds(i0, CONV_RS), c0:c0 + CONV_CS]
                acc = jnp.zeros((CONV_RS, CONV_CS), F32)
                for k in range(SSM_K):
                    sh = _tap_rows(dxs, shifts, SSM_K - 1 - k, i0, c0, SSM_BWD_SHIFTS)
                    acc = acc + sh * cw_ref[k:k + 1, c0:c0 + CONV_CS]
                    t = xin_s * sh
                    dwp[k * SUBLANES:(k + 1) * SUBLANES, c0:c0 + CONV_CS] += _fold_rows(t)
                dxin_ref[pl.ds(i0, CONV_RS), c0:c0 + CONV_CS] = acc.astype(BF16)
            return carry

        lax.fori_loop(0, tl // CONV_RS, strip, 0)
        dxs[tl:tl + SUBLANES, :] = dxs[0:SUBLANES, :]

        @pl.when(i == n_t - 1)
        def _():
            for k in range(SSM_K):
                dcw_ref[k:k + 1, :] = _colsum(dwp[k * SUBLANES:(k + 1) * SUBLANES, :])
            da_ref[...] = da_ref[...] * a
            sel = (lax.broadcasted_iota(jnp.int32, (D_SSM, LANES), 0) // HEAD_DIM
                   == lax.broadcasted_iota(jnp.int32, (D_SSM, LANES), 1)).astype(F32)
            rows = jnp.broadcast_to(dskc[...], (SUBLANES, D_SSM))
            ddsk_ref[...] = jnp.dot(rows, sel, precision=HIGHEST, preferred_element_type=F32)[0:1, :]
            _direct_stage(1, specs, g_refs, recvs, sems)

    head_lanes = _head_lane_matrix()
    per_head = (jnp.arange(N_HEADS * CHUNK)[:, None] // CHUNK == jnp.arange(LANES)[None, :]).astype(BF16)
    rev = functools.partial(_tile, tl, rev_of=n_t)
    return pl.pallas_call(
        body, grid=(n_t,), name="ssd_bwd",
        in_specs=[rev(D_SSM), rev(D_SSM), rev(D_SSM), rev(D_XBC), rev(D_XBC), rev(LANES),
                  pl.BlockSpec((nq, D_SSM, N_STATE), lambda i: (n_t - 1 - i, 0, 0)),
                  _full((SUBLANES, D_XBC)), _full((1, LANES)), _full((1, LANES)), _full((1, D_SSM)), _full((1, D_SSM)),
                  _full((LANES, D_SSM)), _full((D_SSM, LANES)), _full((N_HEADS * CHUNK, LANES))] + [IN_VMEM] * n_g,
        out_specs=[rev(D_XBC), rev(D_SSM), rev(LANES), _full((SUBLANES, D_XBC)), _full((1, D_XBC)), _full((1, D_SSM)),
                   _full((1, LANES)), _full((1, LANES)), _full((1, LANES))] + [ANY] * n_r,
        out_shape=[jax.ShapeDtypeStruct((n_tok, D_XBC), BF16), jax.ShapeDtypeStruct((n_tok, D_SSM), BF16),
                   jax.ShapeDtypeStruct((n_tok, LANES), BF16), jax.ShapeDtypeStruct((SUBLANES, D_XBC), F32),
                   jax.ShapeDtypeStruct((1, D_XBC), F32), jax.ShapeDtypeStruct((1, D_SSM), F32),
                   jax.ShapeDtypeStruct((1, LANES), F32), jax.ShapeDtypeStruct((1, LANES), F32),
                   jax.ShapeDtypeStruct((1, LANES), F32)]
        + [jax.ShapeDtypeStruct((N_DEV, rows, cols), BF16) for rows, cols, _, _ in specs],
        scratch_shapes=[pltpu.VMEM((tl + SUBLANES, D_XBC), F32), pltpu.VMEM((D_SSM, N_STATE), F32),
                        pltpu.VMEM((CHUNK, LANES), F32), pltpu.VMEM((1, D_SSM), F32),
                        pltpu.VMEM((len(SSM_BWD_SHIFTS), tl, D_XBC), F32), pltpu.VMEM((SSM_K * SUBLANES, D_XBC), F32),
                        pltpu.VMEM((CHUNK, N_HEADS * CHUNK), BF16), pltpu.VMEM((3, CHUNK, D_SSM), BF16)]
        + _direct_sems(specs),
        compiler_params=_params(),
    )(dys, y, z, pre, xbc_in, dtr, hprev, cw, dtb, alog, dsk, gnorm, head_lanes, head_lanes.T, per_head, *grads_b)


CONF_HALO = 32
CONV_RS = 32
CONV_CS = 256


ALL_SHIFTS = tuple(range(1, SUBLANES))


def _shift_copies(buf, shifts, n_rows, residues=ALL_SHIFTS):
    for j, r in enumerate(residues):
        shifts[j, 0:n_rows, :] = buf[r:r + n_rows, :]


def _halves(rows):
    return ((0, rows // 2), (rows // 2, rows))


def _fold_rows(t):
    part = t[0:SUBLANES]
    for j in range(1, t.shape[0] // SUBLANES):
        part = part + t[j * SUBLANES:(j + 1) * SUBLANES]
    return part


def _tap_rows(buf, shifts, off, i0, c0, residues=ALL_SHIFTS):
    q, r = divmod(off, SUBLANES)
    rows = pl.ds(pl.multiple_of(i0 + SUBLANES * q, SUBLANES), CONV_RS)
    if r == 0:
        return buf[rows, c0:c0 + CONV_CS]
    return shifts[residues.index(r), rows, c0:c0 + CONV_CS]


def _conf_fwd(glu, cgate, bglu, cw, cb, lg, lb, tl):
    n_tok = glu.shape[0]

    def body(glu_ref, cg_ref, bglu_ref, cw_ref, cb_ref, lg_ref, lb_ref, u0_ref, u1_ref, yc_ref, buf, shifts):
        @pl.when(pl.program_id(0) == 0)
        def _():
            buf[0:CONF_HALO, :] = jnp.zeros((CONF_HALO, D_CONF), F32)

        for r0, r1 in _halves(tl):
            gl = glu_ref[r0:r1, :] + bglu_ref[...]
            u0 = gl[:, 0:D_CONF] * _sig(gl[:, D_CONF:2 * D_CONF])
            u0_ref[r0:r1, :] = u0
            buf[CONF_HALO + r0:CONF_HALO + r1, :] = u0
        _shift_copies(buf, shifts, tl + CONF_HALO - SUBLANES)

        def strip(rb, carry):
            i0 = pl.multiple_of(rb * CONV_RS, CONV_RS)
            for c0 in range(0, D_CONF, CONV_CS):
                acc = jnp.broadcast_to(cb_ref[:, c0:c0 + CONV_CS], (CONV_RS, CONV_CS))
                for k in range(CONF_K):
                    acc = acc + _tap_rows(buf, shifts, CONF_HALO - (CONF_K - 1) + k, i0, c0) * cw_ref[k:k + 1, c0:c0 + CONV_CS]
                u1_ref[pl.ds(i0, CONV_RS), c0:c0 + CONV_CS] = acc
            return carry

        lax.fori_loop(0, tl // CONV_RS, strip, 0)
        buf[0:CONF_HALO, :] = buf[tl:tl + CONF_HALO, :]
        for r0, r1 in _halves(tl):
            xhat, _ = _ln_stats(u1_ref[r0:r1, :])
            n = xhat * lg_ref[...] + lb_ref[...]
            cgv = cg_ref[r0:r1, :]
            yc_ref[r0:r1, :] = (n * _sig(n) * (cgv * _sig(cgv))).astype(BF16)

    return pl.pallas_call(
        body, grid=(n_tok // tl,), name="conf_fwd",
        in_specs=[_tile(tl, 2 * D_CONF), _tile(tl, D_CONF), _full((1, 2 * D_CONF)), _full((CONF_HALO, D_CONF)),
                  _full((1, D_CONF)), _full((1, D_CONF)), _full((1, D_CONF))],
        out_specs=[_tile(tl, D_CONF)] * 3,
        out_shape=[jax.ShapeDtypeStruct((n_tok, D_CONF), F32), jax.ShapeDtypeStruct((n_tok, D_CONF), F32),
                   jax.ShapeDtypeStruct((n_tok, D_CONF), BF16)],
        scratch_shapes=[pltpu.VMEM((tl + CONF_HALO, D_CONF), F32),
                        pltpu.VMEM((SUBLANES - 1, tl + CONF_HALO - SUBLANES, D_CONF), F32)],
        compiler_params=_params(),
    )(glu, cgate, bglu, cw, cb, lg, lb)


def _conf_bwd(dyc, dyc_block, u0, u1, glu, cgate, bglu, cw, lg, lb, specs, grads_b, tl):
    n_tok = glu.shape[0]
    n_t = n_tok // tl
    n_g, n_r = len(grads_b), len(specs)

    def body(dyc_ref, u0_ref, u1_ref, glu_ref, cg_ref, bglu_ref, cw_ref, lg_ref, lb_ref, *rest):
        g_refs, rest = rest[0:n_g], rest[n_g:]
        dglu_ref, dcg_ref, dcw_ref, dbglu_ref, small_ref = rest[0:5]
        recvs, rest = rest[5:5 + n_r], rest[5 + n_r:]
        buf, shifts, du0_s, dwp = rest[0:4]
        sems = rest[4:]

        @pl.when(pl.program_id(0) == 0)
        def _():
            _direct_stage(0, specs, g_refs, recvs, sems)
            dwp[...] = jnp.zeros_like(dwp)
            dbglu_ref[...] = jnp.zeros_like(dbglu_ref)
            small_ref[...] = jnp.zeros_like(small_ref)
            buf[tl:tl + CONF_HALO, :] = jnp.zeros((CONF_HALO, D_CONF), F32)

        for r0, r1 in _halves(tl):
            xhat, rstd = _ln_stats(u1_ref[r0:r1, :])
            n = xhat * lg_ref[...] + lb_ref[...]
            sn = _sig(n)
            cgv = cg_ref[r0:r1, :]
            scg = _sig(cgv)
            dycv = dyc_ref[r0:r1, :]
            dcg_ref[r0:r1, :] = (dycv * (n * sn) * _dsilu(cgv, scg)).astype(BF16)
            dn = dycv * (cgv * scg) * _dsilu(n, sn)
            small_ref[0:1, :] += _colsum(dn * xhat)
            small_ref[1:2, :] += _colsum(dn)
            du1 = _ln_bwd(dn * lg_ref[...], xhat, rstd)
            small_ref[2:3, :] += _colsum(du1)
            buf[r0:r1, :] = du1
        _shift_copies(buf, shifts, tl + CONF_HALO - SUBLANES)

        def strip(rb, carry):
            i0 = pl.multiple_of(rb * CONV_RS, CONV_RS)
            for c0 in range(0, D_CONF, CONV_CS):
                u0s = u0_ref[pl.ds(i0, CONV_RS), c0:c0 + CONV_CS]
                acc = jnp.zeros((CONV_RS, CONV_CS), F32)
                for k in range(CONF_K):
                    sh = _tap_rows(buf, shifts, CONF_K - 1 - k, i0, c0)
                    acc = acc + sh * cw_ref[k:k + 1, c0:c0 + CONV_CS]
                    t = u0s * sh
                    dwp[k * SUBLANES:(k + 1) * SUBLANES, c0:c0 + CONV_CS] += _fold_rows(t)
                du0_s[pl.ds(i0, CONV_RS), c0:c0 + CONV_CS] = acc
            return carry

        lax.fori_loop(0, tl // CONV_RS, strip, 0)
        buf[tl:tl + CONF_HALO, :] = buf[0:CONF_HALO, :]
        for r0, r1 in _halves(tl):
            du0 = du0_s[r0:r1, :]
            gl = glu_ref[r0:r1, :] + bglu_ref[...]
            sg = _sig(gl[:, D_CONF:2 * D_CONF])
            dgv = du0 * sg
            dgg = du0 * gl[:, 0:D_CONF] * sg * (1.0 - sg)
            dglu_ref[r0:r1, 0:D_CONF] = dgv.astype(BF16)
            dglu_ref[r0:r1, D_CONF:2 * D_CONF] = dgg.astype(BF16)
            dbglu_ref[:, 0:D_CONF] += _colsum(dgv)
            dbglu_ref[:, D_CONF:2 * D_CONF] += _colsum(dgg)

        @pl.when(pl.program_id(0) == n_t - 1)
        def _():
            for k in range(CONF_HALO):
                dcw_ref[k:k + 1, :] = _colsum(dwp[k * SUBLANES:(k + 1) * SUBLANES, :])
            _direct_stage(1, specs, g_refs, recvs, sems)

    rev = functools.partial(_tile, tl, rev_of=n_t)
    return pl.pallas_call(
        body, grid=(n_t,), name="conf_bwd",
        in_specs=[pl.BlockSpec((tl, D_CONF), lambda i: (n_t - 1 - i, dyc_block)),
                  rev(D_CONF), rev(D_CONF), rev(2 * D_CONF), rev(D_CONF), _full((1, 2 * D_CONF)),
                  _full((CONF_HALO, D_CONF)), _full((1, D_CONF)), _full((1, D_CONF))] + [IN_VMEM] * n_g,
        out_specs=[rev(2 * D_CONF), rev(D_CONF), _full((CONF_HALO, D_CONF)), _full((1, 2 * D_CONF)), _full((SUBLANES, D_CONF))]
        + [ANY] * n_r,
        out_shape=[jax.ShapeDtypeStruct((n_tok, 2 * D_CONF), BF16), jax.ShapeDtypeStruct((n_tok, D_CONF), BF16),
                   jax.ShapeDtypeStruct((CONF_HALO, D_CONF), F32), jax.ShapeDtypeStruct((1, 2 * D_CONF), F32),
                   jax.ShapeDtypeStruct((SUBLANES, D_CONF), F32)]
        + [jax.ShapeDtypeStruct((N_DEV, rows, cols), BF16) for rows, cols, _, _ in specs],
        scratch_shapes=[pltpu.VMEM((tl + CONF_HALO, D_CONF), F32),
                        pltpu.VMEM((SUBLANES - 1, tl + CONF_HALO - SUBLANES, D_CONF), F32),
                        pltpu.VMEM((tl, D_CONF), F32), pltpu.VMEM((CONF_HALO * SUBLANES, D_CONF), F32)] + _direct_sems(specs),
        compiler_params=_params(),
    )(dyc, u0, u1, glu, cgate, bglu, cw, lg, lb, *grads_b)


def _tail(x, yssm, yconf, p, tgt, vec, w_out, wpg, wpp, tl):
    n_tok = x.shape[0]

    def body(x_ref, ys_ref, yc_ref, p_ref, t_ref, vec_ref, wo_ref, wg_ref, wp_ref,
             dmix_ref, dr1_ref, dr1b_ref, h1b_ref, dgb_ref, dpb_ref, small_ref, loss_ref):
        @pl.when(pl.program_id(0) == 0)
        def _():
            small_ref[...] = jnp.zeros_like(small_ref)
            loss_ref[...] = jnp.zeros_like(loss_ref)

        xh0, _ = _ln_stats(x_ref[...])
        h0 = xh0 * vec_ref[0:1, :] + vec_ref[1:2, :]
        out = _dot(ys_ref[...], wo_ref[0:D_SSM, :]) + _dot(yc_ref[...], wo_ref[D_SSM:D_SSM + D_CONF, :]) + vec_ref[2:3, :]
        xh1, rstd1 = _ln_stats(ALPHA * h0 + out)
        h1 = xh1 * vec_ref[3:4, :] + vec_ref[4:5, :]
        h1b = h1.astype(BF16)
        h1b_ref[...] = h1b
        gate = _sig(_dot(h1b, wg_ref[...]))
        ple = _dot(p_ref[...].astype(BF16), wp_ref[...])
        xh2, rstd2 = _ln_stats(ALPHA * h1 + gate * ple)
        h2 = xh2 * vec_ref[5:6, :] + vec_ref[6:7, :]
        diff = h2 - t_ref[...]
        part = jnp.sum(jnp.sum(diff * diff, axis=1, keepdims=True), axis=0, keepdims=True) * (0.5 / D_MODEL)
        loss_ref[...] += jnp.broadcast_to(part, loss_ref.shape)
        dh2 = diff * (1.0 / D_MODEL)
        small_ref[3:4, :] += _colsum(dh2 * xh2)
        small_ref[4:5, :] += _colsum(dh2)
        dr2 = _ln_bwd(dh2 * vec_ref[5:6, :], xh2, rstd2)
        dgpre = (dr2 * ple * gate * (1.0 - gate)).astype(BF16)
        dgb_ref[...] = dgpre
        dpb_ref[...] = (dr2 * gate).astype(BF16)
        dh1 = ALPHA * dr2 + _dot_nt(dgpre, wg_ref[...])
        small_ref[1:2, :] += _colsum(dh1 * xh1)
        small_ref[2:3, :] += _colsum(dh1)
        dr1 = _ln_bwd(dh1 * vec_ref[3:4, :], xh1, rstd1)
        small_ref[0:1, :] += _colsum(dr1)
        dr1_ref[...] = dr1
        dr1b = dr1.astype(BF16)
        dr1b_ref[...] = dr1b
        dmix_ref[...] = _dot_nt(dr1b, wo_ref[...])

    d_mix = D_SSM + D_CONF
    return pl.pallas_call(
        body, grid=(n_tok // tl,), name="tail",
        in_specs=[_tile(tl, D_MODEL), _tile(tl, D_SSM), _tile(tl, D_CONF), _tile(tl, D_PLE), _tile(tl, D_MODEL),
                  _full((SUBLANES, D_MODEL)), _full((d_mix, D_MODEL), True), _full((D_MODEL, D_MODEL), True),
                  _full((D_PLE, D_MODEL), True)],
        out_specs=[_tile(tl, d_mix), _tile(tl, D_MODEL), _tile(tl, D_MODEL), _tile(tl, D_MODEL), _tile(tl, D_MODEL),
                   _tile(tl, D_MODEL), _full((SUBLANES, D_MODEL)), _full((SUBLANES, LANES))],
        out_shape=[jax.ShapeDtypeStruct((n_tok, d_mix), F32), jax.ShapeDtypeStruct((n_tok, D_MODEL), F32),
                   jax.ShapeDtypeStruct((n_tok, D_MODEL), BF16), jax.ShapeDtypeStruct((n_tok, D_MODEL), BF16),
                   jax.ShapeDtypeStruct((n_tok, D_MODEL), BF16), jax.ShapeDtypeStruct((n_tok, D_MODEL), BF16),
                   jax.ShapeDtypeStruct((SUBLANES, D_MODEL), F32), jax.ShapeDtypeStruct((SUBLANES, LANES), F32)],
        compiler_params=_params(),
    )(x, yssm, yconf, p, tgt, vec, w_out, wpg, wpp)


def _inproj_bwd(dxin, dz, dglu, dcg, ddtr, dr1, x, g, b, w_r, tl):
    n_tok = x.shape[0]

    def body(dxin_ref, dz_ref, dglu_ref, dcg_ref, ddtr_ref, dr1_ref, x_ref, g_ref, b_ref, w_ref,
             dx_ref, dpb_ref, small_ref):
        @pl.when(pl.program_id(0) == 0)
        def _():
            small_ref[...] = jnp.zeros_like(small_ref)

        dh0 = ALPHA * dr1_ref[...]
        for ref, lo, hi in ((dxin_ref, R_XBC, R_Z), (dz_ref, R_Z, R_GLU), (dglu_ref, R_GLU, R_CG), (dcg_ref, R_CG, R_DT),
                            (ddtr_ref, R_DT, D_INR)):
            piece = ref[...].astype(BF16)
            dpb_ref[:, lo:hi] = piece
            dh0 = dh0 + _dot_nt(piece, w_ref[:, lo:hi])
        xhat, rstd = _ln_stats(x_ref[...])
        small_ref[0:1, :] += _colsum(dh0 * xhat)
        small_ref[1:2, :] += _colsum(dh0)
        dx_ref[...] = _ln_bwd(dh0 * g_ref[...], xhat, rstd)

    return pl.pallas_call(
        body, grid=(n_tok // tl,), name="inproj_bwd",
        in_specs=[_tile(tl, D_XBC), _tile(tl, D_SSM), _tile(tl, 2 * D_CONF), _tile(tl, D_CONF), _tile(tl, LANES),
                  _tile(tl, D_MODEL), _tile(tl, D_MODEL), _full((1, D_MODEL)), _full((1, D_MODEL)),
                  _full((D_MODEL, D_INR), True)],
        out_specs=[_tile(tl, D_MODEL), _tile(tl, D_INR), _full((SUBLANES, D_MODEL))],
        out_shape=[jax.ShapeDtypeStruct((n_tok, D_MODEL), F32), jax.ShapeDtypeStruct((n_tok, D_INR), BF16),
                   jax.ShapeDtypeStruct((SUBLANES, D_MODEL), F32)],
        compiler_params=_params(),
    )(dxin, dz, dglu, dcg, ddtr, dr1, x, g, b, w_r)


def _tn_matmul(a, b, name, tn, tl, out_dtype=F32):
    n_tok, m = a.shape
    n = b.shape[1]
    n_l = n_tok // tl
    direct = out_dtype == F32

    def body(a_ref, b_ref, o_ref, *scratch):
        acc = o_ref if direct else scratch[0]

        @pl.when(pl.program_id(1) == 0)
        def _():
            acc[...] = jnp.zeros_like(acc)

        acc[...] += lax.dot_general(a_ref[...], b_ref[...], TN_DIMS, preferred_element_type=F32)
        if not direct:
            @pl.when(pl.program_id(1) == n_l - 1)
            def _():
                o_ref[...] = acc[...].astype(out_dtype)

    return pl.pallas_call(
        body, grid=(n // tn, n_l), name=name,
        in_specs=[pl.BlockSpec((tl, m), lambda j, l: (l, 0)), pl.BlockSpec((tl, tn), lambda j, l: (l, j))],
        out_specs=pl.BlockSpec((m, tn), lambda j, l: (0, j)),
        out_shape=jax.ShapeDtypeStruct((m, n), out_dtype),
        scratch_shapes=[] if direct else [pltpu.VMEM((m, tn), F32)],
        compiler_params=pltpu.CompilerParams(dimension_semantics=("parallel", "arbitrary"), vmem_limit_bytes=VMEM_LIMIT),
    )(a, b)


def _dw_in(a, b, spack, tn, tl):
    n_tok, m = a.shape
    n = b.shape[1]
    n_j, n_l = n // tn, n_tok // tl

    def body(a_ref, b_ref, s_ref, o_ref, small_all, send_sems, recv_sems, loc_sem):
        j, l = pl.program_id(0), pl.program_id(1)

        @pl.when(jnp.logical_and(j == 0, l == 0))
        def _():
            _gather8_stage(0, s_ref, small_all, send_sems, recv_sems, loc_sem)

        @pl.when(l == 0)
        def _():
            o_ref[...] = jnp.zeros_like(o_ref)

        o_ref[...] += lax.dot_general(a_ref[...], b_ref[...], TN_DIMS, preferred_element_type=F32)

        @pl.when(jnp.logical_and(j == n_j - 1, l == n_l - 1))
        def _():
            _gather8_stage(1, s_ref, small_all, send_sems, recv_sems, loc_sem)

    dma = pltpu.SemaphoreType.DMA
    return pl.pallas_call(
        body, grid=(n_j, n_l), name="dw_in",
        in_specs=[pl.BlockSpec((tl, m), lambda j, l: (l, 0)), pl.BlockSpec((tl, tn), lambda j, l: (l, j)), IN_VMEM],
        out_specs=[pl.BlockSpec((m, tn), lambda j, l: (0, j)), ANY],
        out_shape=[jax.ShapeDtypeStruct((m, n), F32), jax.ShapeDtypeStruct((N_DEV,) + spack.shape, F32)],
        scratch_shapes=[dma((N_DEV - 1,)), dma((N_DEV - 1,)), dma],
        compiler_params=pltpu.CompilerParams(dimension_semantics=("arbitrary", "arbitrary"), vmem_limit_bytes=VMEM_LIMIT),
    )(a, b, spack)


def _pad_rows(a, rows):
    return jnp.pad(a, ((0, rows - a.shape[0]), (0, 0)))


def _pad_lanes(a):
    return jnp.pad(a, ((0, 0), (0, LANES - a.shape[1])))


def _local_grads(x, p, tgt, w4, later, ssm_cw, conf_cw, sm):
    n_tok = x.shape[0]
    tl = min(256, n_tok)
    tlm = min(512, n_tok)
    row = lambda v: v.reshape(1, -1)
    g_e, b_e = row(sm["ln_emb_g"]), row(sm["ln_emb_b"])
    h0b, xbc_in, z, glu, cgate, dtr, w_out_all, wpg_all, wpp_all, w_r = _ln_inproj(x, g_e, b_e, w4, later, tl)
    w_out_b = w_out_all.reshape(D_SSM + D_CONF, D_MODEL)
    wpg_b = wpg_all.reshape(D_MODEL, D_MODEL)
    wpp_b = wpp_all.transpose(1, 0, 2).reshape(D_PLE, D_MODEL)

    cw4 = _pad_rows(ssm_cw, SUBLANES)
    dtb, alog = _pad_lanes(sm["dt_bias"]), _pad_lanes(sm["a_log"])
    dsk = jnp.repeat(sm["d_skip"], HEAD_DIM, axis=1)
    pre, y, yssm, hprev = _ssd_fwd(xbc_in, z, dtr, cw4, sm["ssm_conv_b"], dtb, alog, dsk, sm["ssm_norm_g"], tl)

    cw31 = _pad_rows(conf_cw, CONF_HALO)
    u0, u1, yconf = _conf_fwd(glu, cgate, sm["b_glu"], cw31, sm["conf_conv_b"], sm["conf_ln_g"], sm["conf_ln_b"], tl)

    vec = jnp.concatenate([g_e, b_e, sm["b_out"], sm["ln1_g"], sm["ln1_b"], sm["ln2_g"], sm["ln2_b"],
                           jnp.zeros((1, D_MODEL), F32)], axis=0)
    dmix, dr1, dr1b, h1b, dgb, dpb, small_t, loss = _tail(
        x, yssm, yconf, p, tgt, vec, w_out_b, wpg_b, wpp_b, tlm)

    tlw = min(2048, n_tok)
    dwo = [_tn_matmul(yssm, dr1b, "dw_out_ssm", D_MODEL, tlw, BF16), _tn_matmul(yconf, dr1b, "dw_out_conf", D_MODEL, tlw, BF16)]
    dwp_ = [_tn_matmul(h1b, dgb, "dw_ple_gate", D_MODEL, tlw, BF16),
            _tn_matmul(p.astype(BF16), dpb, "dw_ple_proj", D_MODEL, tlw, BF16)]
    dglu, dcg, dcw31, dbglu, small_c, dw_out = _conf_bwd(dmix, D_SSM // D_CONF, u0, u1, glu, cgate, sm["b_glu"], cw31,
                                                          sm["conf_ln_g"], sm["conf_ln_b"], [W_OUT_DIRECT], dwo, tl)
    dxin, dz, ddtr, dcw4, dcb4, dgn, ddsk, dalog, ddtb, dwpg, dwpp = _ssd_bwd(
        dmix, y, z, pre, xbc_in, dtr, hprev, cw4, dtb, alog, dsk, sm["ssm_norm_g"], [WPG_DIRECT, WPP_DIRECT], dwp_, tl)
    dx, dprojb, small_e = _inproj_bwd(dxin, dz, dglu, dcg, ddtr, dr1, x, g_e, b_e, w_r, tlm)
    spack = _pack_small(dict(
        ln_emb_g=small_e[0], ln_emb_b=small_e[1], ssm_conv_w=dcw4[0:SSM_K], ssm_conv_b=dcb4,
        dt_bias=ddtb[:, 0:N_HEADS], a_log=dalog[:, 0:N_HEADS], d_skip=ddsk[:, 0:N_HEADS], ssm_norm_g=dgn, b_glu=dbglu,
        conf_conv_w=dcw31[0:CONF_K], conf_conv_b=small_c[2:3], conf_ln_g=small_c[0:1], conf_ln_b=small_c[1:2],
        b_out=small_t[0:1], ln1_g=small_t[1:2], ln1_b=small_t[2:3], ln2_g=small_t[3:4], ln2_b=small_t[4:5]), loss[0, 0])
    dw_r, small_all = _dw_in(h0b, dprojb, spack, D_INR // 3, tlw)
    return dx, dict(w_in=dw_r, w_out=dw_out, w_ple_gate=dwpg, w_ple_proj=dwpp, small=small_all)


N_CHIPS = 4
N_DEV = 8
W_IN_SH = D_IN // N_CHIPS
BIG = (("w_in", D_MODEL, W_IN_SH), ("w_out", (D_SSM + D_CONF) // N_CHIPS, D_MODEL),
       ("w_ple_gate", D_MODEL // N_CHIPS, D_MODEL), ("w_ple_proj", D_PLE, D_MODEL // N_CHIPS))
SEGS = ((R_XBC, 0, D_SSM), (R_XBC + D_SSM, 2048, 256), (R_XBC + D_SSM + 256, 2304, 256), (R_Z, 1024, D_SSM),
        (R_GLU, 2576, 2 * D_CONF), (R_CG, 4624, D_CONF), (R_DT, 2560, N_HEADS))
ROWS_CW4 = 2
ROWS_CW31 = 8
ROWS_CONV = 16
SMALL_ROWS = 56
SMALL_LAYOUT = (("ln_emb_g", 0, 1), ("ln_emb_b", 1, 1), ("ssm_conv_b", 2, 2), ("dt_bias", 4, 1), ("a_log", 5, 1),
                ("d_skip", 6, 1), ("ssm_norm_g", 7, 1), ("b_glu", 8, 2), ("conf_conv_b", 10, 1), ("conf_ln_g", 11, 1),
                ("conf_ln_b", 12, 1), ("b_out", 13, 1), ("ln1_g", 14, 1), ("ln1_b", 15, 1), ("ln2_g", 16, 1), ("ln2_b", 17, 1))
CONV_LAYOUT = (("ssm_conv_w", 18, 6, (SSM_K, D_XBC)), ("conf_conv_w", 24, 31, (CONF_K, D_CONF)))


def _rows_of(v, rows):
    flat = v.reshape(-1)
    return jnp.pad(flat, (0, rows * D_MODEL - flat.shape[0])).reshape(rows, D_MODEL)


LOSS_ROW = 55


def _pack_small(d, loss_share):
    parts = [_rows_of(d[n], r) for n, _, r in SMALL_LAYOUT] + [_rows_of(d[n], r) for n, _, r, _ in CONV_LAYOUT]
    parts.append(_rows_of(loss_share, SMALL_ROWS - LOSS_ROW))
    return jnp.concatenate(parts, axis=0)


def _gather8_stage(stage, s_ref, all_ref, send_sems, recv_sems, loc_sem):
    x, y, c = _my_place()
    me = 4 * x + 2 * y + c
    copies = [pltpu.make_async_copy(s_ref, all_ref.at[me], loc_sem)]
    for mk in range(1, N_DEV):
        peer = (x ^ (mk >> 2), y ^ ((mk >> 1) & 1), c ^ (mk & 1))
        copies.append(_remote(s_ref, all_ref.at[me], send_sems.at[mk - 1], recv_sems.at[mk - 1], peer))
    for cp in copies:
        cp.start() if stage == 0 else cp.wait()


def _row_chunks(rows, n):
    return [(j * (rows // n), rows // n) for j in range(n)]


def _my_place():
    return lax.axis_index("x"), lax.axis_index("y"), lax.axis_index("c")


MESH_ID = pl.DeviceIdType.MESH
ANY = pl.BlockSpec(memory_space=pl.ANY)
IN_VMEM = pl.BlockSpec(memory_space=pltpu.VMEM)
CHIP_FLIPS = ((1, 0), (0, 1), (1, 1))


def _remote(src, dst, send_sem, recv_sem, peer):
    return pltpu.make_async_remote_copy(src, dst, send_sem, recv_sem, device_id=peer, device_id_type=MESH_ID)


GATHER_CHUNKS = (4, 2, 1, 1)


def _gather_plans(kinds):
    plan = [(a, o, n, rows // 2) for a, (rows, ch) in enumerate(kinds) for o, n in _row_chunks(rows // 2, ch)]
    own_plan = [(a, o, n) for a, (rows, ch) in enumerate(kinds) for o, n in _row_chunks(rows, 2 * ch)]
    return plan, own_plan


def _gather_sems(plan, own_plan):
    hop = pltpu.SemaphoreType.DMA((3, len(plan)))
    return [hop, hop, hop, hop, pltpu.SemaphoreType.DMA((len(own_plan),))]


def _gather_stage(stage, ins, outs, plan, own_plan, sems):
    send_a, recv_a, send_b, recv_b, loc = sems
    x, y, c = _my_place()
    s = 2 * x + y
    sibling = (x, y, 1 - c)
    own = [pltpu.make_async_copy(ins[a].at[pl.ds(o, n)], outs[a].at[s, pl.ds(o, n)], loc.at[j])
           for j, (a, o, n) in enumerate(own_plan)]
    first, arrive, passed, arrive_b = [], [], [], []
    for k, (fx, fy) in enumerate(CHIP_FLIPS):
        peer = (x ^ fx, y ^ fy, c)
        sk = 2 * (x ^ fx) + (y ^ fy)
        for j, (a, o, n, h) in enumerate(plan):
            mine = pl.ds(pl.multiple_of(c * h + o, 16), n)
            theirs = pl.ds(pl.multiple_of((1 - c) * h + o, 16), n)
            first.append(_remote(ins[a].at[mine], outs[a].at[s, mine], send_a.at[k, j], recv_a.at[k, j], peer))
            land = outs[a].at[sk, mine]
            arrive.append(_remote(land, land, send_a.at[k, j], recv_a.at[k, j], peer))
            passed.append(_remote(land, land, send_b.at[k, j], recv_b.at[k, j], sibling))
            land_b = outs[a].at[sk, theirs]
            arrive_b.append(_remote(land_b, land_b, send_b.at[k, j], recv_b.at[k, j], sibling))
    if stage == 0:
        for cp in own + first:
            cp.start()
    elif stage == 1:
        for got, fwd in zip(arrive, passed):
            got.wait_recv()
            fwd.start()
    else:
        for got in arrive_b:
            got.wait_recv()
        for cp in first + passed:
            cp.wait_send()
        for cp in own:
            cp.wait()


W_OUT_DIRECT = (512, D_MODEL, 2, ((0, 0, 0), (0, 512, 0), (1, 0, 0), (1, 512, 0)))
WPG_DIRECT = (256, D_MODEL, 1, ((0, 0, 0), (0, 256, 0), (0, 512, 0), (0, 768, 0)))
WPP_DIRECT = (D_PLE, 256, 1, ((1, 0, 0), (1, 0, 256), (1, 0, 512), (1, 0, 768)))


def _direct_sems(specs):
    out = []
    for _, _, copies, _ in specs:
        out += [pltpu.SemaphoreType.DMA((N_CHIPS, 2, copies)), pltpu.SemaphoreType.DMA((N_DEV, copies)),
                pltpu.SemaphoreType.DMA((copies,))]
    return out


def _direct_stage(stage, specs, srcs, recvs, sems):
    x, y, c = _my_place()
    me = 4 * x + 2 * y + c
    s = 2 * x + y
    for r, (rows, cols, copies, where) in enumerate(specs):
        send_sems, recv_sems, loc_sems = sems[3 * r:3 * r + 3]
        for k, (o, n) in enumerate(_row_chunks(rows, copies)):
            for t, (si, row0, col0) in enumerate(where):
                src = srcs[si].at[pl.ds(row0 + o, n), pl.ds(col0, cols)]
                dst = recvs[r].at[me, pl.ds(o, n)]
                for cc in range(2):
                    to_self = jnp.logical_and(s == t, c == cc)
                    away = _remote(src, dst, send_sems.at[t, cc, k], recv_sems.at[me, k], (t // 2, t % 2, cc))
                    here = pltpu.make_async_copy(src, dst, loc_sems.at[k])

                    @pl.when(to_self)
                    def _():
                        here.start() if stage == 0 else here.wait()

                    @pl.when(jnp.logical_not(to_self))
                    def _():
                        away.start() if stage == 0 else away.wait_send()
            if stage == 1:
                for j in range(N_DEV):
                    land = recvs[r].at[j, pl.ds(o, n)]

                    @pl.when(j != me)
                    def _():
                        _remote(land, land, send_sems.at[0, 0, k], recv_sems.at[j, k], (0, 0, 0)).wait_recv()


def _gather_weights(w_in_b, conv_f):
    plan, own_plan = _gather_plans([(BIG[0][1], GATHER_CHUNKS[0])])

    def body(w_ref, conv_ref, w_all, conv_all, *sems):
        conv_send, conv_recv, conv_loc = sems[5:]
        x, y, c = _my_place()
        s = 2 * x + y
        conv_own = pltpu.make_async_copy(conv_ref, conv_all.at[s], conv_loc)
        conv_out = [_remote(conv_ref, conv_all.at[s], conv_send.at[k], conv_recv.at[k], (x ^ fx, y ^ fy, c))
                    for k, (fx, fy) in enumerate(CHIP_FLIPS)]
        _gather_stage(0, [w_ref], [w_all], plan, own_plan, sems[0:5])
        for cp in [conv_own] + conv_out:
            cp.start()
        _gather_stage(1, [w_ref], [w_all], plan, own_plan, sems[0:5])
        _gather_stage(2, [w_ref], [w_all], plan, own_plan, sems[0:5])
        for cp in conv_out:
            cp.wait()
        conv_own.wait()

    arrays = [w_in_b, conv_f]
    return pl.pallas_call(
        body, name="gather_weights", in_specs=[IN_VMEM] * 2, out_specs=[ANY] * 2,
        out_shape=[jax.ShapeDtypeStruct((N_CHIPS,) + a.shape, a.dtype) for a in arrays],
        scratch_shapes=_gather_sems(plan, own_plan)
        + [pltpu.SemaphoreType.DMA((3,)), pltpu.SemaphoreType.DMA((3,)), pltpu.SemaphoreType.DMA],
    )(*arrays)


SHARE_CHUNKS = 4


OWN_TILES = (((0, 1024), (1536, 512)), ((1024, 512), (1920, 640), (2560, 256), (5632, 128)), ((2688, 1536),), ((4096, 1536),))
OWN_W = 1536
RED_CHUNKS = 8


def _own_columns(t, rows):
    if t == 0:
        return rows[:, 0:W_IN_SH]
    if t == 1:
        return jnp.concatenate([rows[:, 516:1152], rows[:, 0:512], rows[:, 1408:1424], rows[:, 1152:1400]], axis=1)
    first = (2808 - 2688) if t == 2 else (4220 - 4096)
    return rows[:, first:first + W_IN_SH]


def _reduce_w_in(dw_r):
    half = D_MODEL // 2
    rc = half // RED_CHUNKS
    max_r = max(len(r) for r in OWN_TILES)

    def body(g_ref, got_ref, theirs, a_buf, b_buf, sum_buf, d2d_send, d2d_recv, ld_sems, ici_send, ici_recv, own_sems):
        x, y, c = _my_place()
        s = 2 * x + y
        sibling = (x, y, 1 - c)
        swap = [_remote(g_ref.at[pl.ds(pl.multiple_of((1 - c) * half + k * rc, SUBLANES), rc)], theirs.at[pl.ds(k * rc, rc)],
                        d2d_send.at[k], d2d_recv.at[k], sibling) for k in range(RED_CHUNKS)]
        for cp in swap:
            cp.start()

        def tiles(k, t, r):
            k0, wd = OWN_TILES[t][r]
            at = sum(w_ for _, w_ in OWN_TILES[t][:r])
            src = sum_buf.at[k, pl.ds(0, rc), pl.ds(k0, wd)]
            dst = got_ref.at[s, pl.ds(k * rc, rc), pl.ds(at, wd)]
            return (_remote(src, dst, ici_send.at[t, r, k], ici_recv.at[s, r, k], (t // 2, t % 2, c)),
                    pltpu.make_async_copy(src, dst, own_sems.at[r, k]))

        mine = [pltpu.make_async_copy(g_ref.at[pl.ds(pl.multiple_of(c * half + k * rc, SUBLANES), rc)], a_buf.at[k % 2],
                                      ld_sems.at[0, k % 2]) for k in range(RED_CHUNKS)]
        other = [pltpu.make_async_copy(theirs.at[pl.ds(k * rc, rc)], b_buf.at[k % 2], ld_sems.at[1, k % 2])
                 for k in range(RED_CHUNKS)]
        mine[0].start()
        for k in range(RED_CHUNKS):
            slot = k % 2
            swap[k].wait_recv()
            other[k].start()
            if k + 1 < RED_CHUNKS:
                mine[k + 1].start()
            mine[k].wait()
            other[k].wait()
            sum_buf[k] = (a_buf[slot] + b_buf[slot]).astype(BF16)
            for t in range(N_CHIPS):
                for r in range(len(OWN_TILES[t])):
                    away, here = tiles(k, t, r)

                    @pl.when(t != s)
                    def _():
                        away.start()

                    @pl.when(t == s)
                    def _():
                        here.start()

        for k in range(RED_CHUNKS):
            swap[k].wait_send()
            for t in range(N_CHIPS):
                for r, (_, wd) in enumerate(OWN_TILES[t]):
                    away, here = tiles(k, t, r)
                    at = sum(w_ for _, w_ in OWN_TILES[t][:r])

                    @pl.when(t != s)
                    def _():
                        away.wait_send()

                    @pl.when(t == s)
                    def _():
                        here.wait()
                        for j in range(N_CHIPS):
                            if j != t:
                                land = got_ref.at[j, pl.ds(k * rc, rc), pl.ds(at, wd)]
                                _remote(land, land, ici_send.at[0, 0, 0], ici_recv.at[j, r, k], (0, 0, 0)).wait_recv()

    dma = pltpu.SemaphoreType.DMA
    got, _ = pl.pallas_call(
        body, name="reduce_w_in", in_specs=[ANY], out_specs=[ANY, ANY],
        out_shape=[jax.ShapeDtypeStruct((N_CHIPS, half, OWN_W), BF16), jax.ShapeDtypeStruct((half, D_INR), F32)],
        scratch_shapes=[pltpu.VMEM((2, rc, D_INR), F32), pltpu.VMEM((2, rc, D_INR), F32), pltpu.VMEM((RED_CHUNKS, rc, D_INR), BF16),
                        dma((RED_CHUNKS,)), dma((RED_CHUNKS,)), dma((2, 2)), dma((N_CHIPS, max_r, RED_CHUNKS)),
                        dma((N_CHIPS, max_r, RED_CHUNKS)), dma((max_r, RED_CHUNKS))],
        compiler_params=pltpu.CompilerParams(vmem_limit_bytes=VMEM_LIMIT),
    )(dw_r)
    return got


def _add_slots(got):
    _, rows, cols = got.shape
    tr = 128

    def body(g_ref, o_ref):
        o_ref[...] = ((g_ref[0].astype(F32) + g_ref[1].astype(F32)) + g_ref[2].astype(F32)) + g_ref[3].astype(F32)

    return pl.pallas_call(body, grid=(rows // tr,), name="add_chips_w_in",
                          in_specs=[pl.BlockSpec((N_CHIPS, tr, cols), lambda i: (0, i, 0))],
                          out_specs=pl.BlockSpec((tr, cols), lambda i: (i, 0)),
                          out_shape=jax.ShapeDtypeStruct((rows, cols), F32), compiler_params=_params(seq=False))(got)


def _share_halves(tot):
    n_rows, half = tot.shape
    plan = _row_chunks(half, SHARE_CHUNKS)

    def body(t_ref, both, send_sems, recv_sems, loc):
        x, y, c = _my_place()
        rows = pl.ds(0, n_rows)

        def place(o, n):
            return both.at[rows, pl.ds(pl.multiple_of(c * half + o, LANES), n)]

        own = [pltpu.make_async_copy(t_ref.at[rows, pl.ds(o, n)], place(o, n), loc.at[j]) for j, (o, n) in enumerate(plan)]
        sends = [_remote(t_ref.at[rows, pl.ds(o, n)], place(o, n), send_sems.at[j], recv_sems.at[j], (x, y, 1 - c))
                 for j, (o, n) in enumerate(plan)]
        for cp in own + sends:
            cp.start()
        for cp in sends:
            cp.wait()
        for cp in own:
            cp.wait()

    dma = pltpu.SemaphoreType.DMA
    return pl.pallas_call(
        body, name="share_halves", in_specs=[IN_VMEM], out_specs=ANY,
        out_shape=jax.ShapeDtypeStruct((n_rows, 2 * half), F32),
        scratch_shapes=[dma((len(plan),)), dma((len(plan),)), dma((len(plan),))],
    )(tot)


def _adam_math(w, g, m, v):
    m = ADAM_B1 * m + (1.0 - ADAM_B1) * g
    v = ADAM_B2 * v + (1.0 - ADAM_B2) * (g * g)
    m_hat = m / (1.0 - ADAM_B1 ** ADAM_STEP)
    v_hat = v / (1.0 - ADAM_B2 ** ADAM_STEP)
    return -ADAM_LR * (m_hat / (jnp.sqrt(v_hat) + ADAM_EPS) + ADAM_WD * w), m, v


def _adam(w, g, m, v, name):
    rows, cols = w.shape

    def body(w_ref, g_ref, m_ref, v_ref, d_ref, nm_ref, nv_ref):
        d_ref[...], nm_ref[...], nv_ref[...] = _adam_math(w_ref[...], g_ref[...], m_ref[...], v_ref[...])

    if rows <= 256 or rows % 256 == 0:
        tr = min(rows, 256)
        n_blocks, spec = rows // tr, pl.BlockSpec((tr, cols), lambda i: (i, 0))
    else:
        n_blocks, spec = cols // 256, pl.BlockSpec((rows, 256), lambda i: (0, i))
    return pl.pallas_call(body, grid=(n_blocks,), name=name, in_specs=[spec] * 4, out_specs=[spec] * 3,
                          out_shape=[jax.ShapeDtypeStruct(w.shape, F32)] * 3, compiler_params=_params(seq=False))(w, g, m, v)


def _adam_sum(w, parts, m, v, name):
    rows, cols = w.shape
    tr = rows if rows <= 256 else 256

    def body(w_ref, p_ref, m_ref, v_ref, g_ref, d_ref, nm_ref, nv_ref):
        g = p_ref[0].astype(F32)
        for j in range(1, N_DEV):
            g = g + p_ref[j].astype(F32)
        g_ref[...] = g
        d_ref[...], nm_ref[...], nv_ref[...] = _adam_math(w_ref[...], g, m_ref[...], v_ref[...])

    spec = pl.BlockSpec((tr, cols), lambda i: (i, 0))
    return pl.pallas_call(
        body, grid=(rows // tr,), name=name,
        in_specs=[spec, pl.BlockSpec((N_DEV, tr, cols), lambda i: (0, i, 0)), spec, spec], out_specs=[spec] * 4,
        out_shape=[jax.ShapeDtypeStruct(w.shape, F32)] * 4, compiler_params=_params(seq=False))(w, parts, m, v)


def _adam_small(parts, ws, ms, vs):
    n_w = len(SMALL_LAYOUT)

    def body(p_ref, *refs):
        w_refs, m_refs, v_refs = refs[0:n_w], refs[n_w:2 * n_w], refs[2 * n_w:3 * n_w]
        sum_ref = refs[3 * n_w]
        outs = refs[3 * n_w + 1:]
        g = p_ref[0]
        for k in range(1, N_DEV):
            g = g + p_ref[k]
        sum_ref[...] = g
        for a, (_, r0, rows) in enumerate(SMALL_LAYOUT):
            width = w_refs[a].shape[1]
            for j in range(rows):
                lo, hi = j * D_MODEL, min((j + 1) * D_MODEL, width)
                gj = sum_ref[r0 + j:r0 + j + 1, 0:hi - lo]
                d, nm, nv = _adam_math(w_refs[a][:, lo:hi], gj, m_refs[a][:, lo:hi], v_refs[a][:, lo:hi])
                for out, val in zip(outs[4 * a:4 * a + 4], (gj, d, nm, nv)):
                    out[:, lo:hi] = val

    shapes = [jax.ShapeDtypeStruct((SMALL_ROWS, D_MODEL), F32)]
    for wa in ws:
        shapes += [jax.ShapeDtypeStruct(wa.shape, F32)] * 4
    res = pl.pallas_call(body, name="adam_small", out_shape=shapes)(parts, *ws, *ms, *vs)
    return res[0], [res[1 + 4 * a:5 + 4 * a] for a in range(n_w)]


def kernel(x, p, ln_emb_g, ln_emb_b, w_in, ssm_conv_w, ssm_conv_b, dt_bias, a_log, d_skip, ssm_norm_g, b_glu, conf_conv_w, conf_conv_b, conf_ln_g, conf_ln_b, w_out, b_out, ln1_g, ln1_b, w_ple_gate, w_ple_proj, ln2_g, ln2_b, loss_target, m_ln_emb_g, m_ln_emb_b, m_w_in, m_ssm_conv_w, m_ssm_conv_b, m_dt_bias, m_a_log, m_d_skip, m_ssm_norm_g, m_b_glu, m_conf_conv_w, m_conf_conv_b, m_conf_ln_g, m_conf_ln_b, m_w_out, m_b_out, m_ln1_g, m_ln1_b, m_w_ple_gate, m_w_ple_proj, m_ln2_g, m_ln2_b, v_ln_emb_g, v_ln_emb_b, v_w_in, v_ssm_conv_w, v_ssm_conv_b, v_dt_bias, v_a_log, v_d_skip, v_ssm_norm_g, v_b_glu, v_conf_conv_w, v_conf_conv_b, v_conf_ln_g, v_conf_ln_b, v_w_out, v_b_out, v_ln1_g, v_ln1_b, v_w_ple_gate, v_w_ple_proj, v_ln2_g, v_ln2_b):
    order = ("ln_emb_g", "ln_emb_b", "w_in", "ssm_conv_w", "ssm_conv_b", "dt_bias", "a_log", "d_skip", "ssm_norm_g", "b_glu",
             "conf_conv_w", "conf_conv_b", "conf_ln_g", "conf_ln_b", "w_out", "b_out", "ln1_g", "ln1_b", "w_ple_gate",
             "w_ple_proj", "ln2_g", "ln2_b")
    w = dict(zip(order, (ln_emb_g, ln_emb_b, w_in, ssm_conv_w, ssm_conv_b, dt_bias, a_log, d_skip, ssm_norm_g, b_glu,
                         conf_conv_w, conf_conv_b, conf_ln_g, conf_ln_b, w_out, b_out, ln1_g, ln1_b, w_ple_gate, w_ple_proj,
                         ln2_g, ln2_b)))
    m = dict(zip(order, (m_ln_emb_g, m_ln_emb_b, m_w_in, m_ssm_conv_w, m_ssm_conv_b, m_dt_bias, m_a_log, m_d_skip,
                         m_ssm_norm_g, m_b_glu, m_conf_conv_w, m_conf_conv_b, m_conf_ln_g, m_conf_ln_b, m_w_out, m_b_out,
                         m_ln1_g, m_ln1_b, m_w_ple_gate, m_w_ple_proj, m_ln2_g, m_ln2_b)))
    v = dict(zip(order, (v_ln_emb_g, v_ln_emb_b, v_w_in, v_ssm_conv_w, v_ssm_conv_b, v_dt_bias, v_a_log, v_d_skip,
                         v_ssm_norm_g, v_b_glu, v_conf_conv_w, v_conf_conv_b, v_conf_ln_g, v_conf_ln_b, v_w_out, v_b_out,
                         v_ln1_g, v_ln1_b, v_w_ple_gate, v_w_ple_proj, v_ln2_g, v_ln2_b)))

    conv_f = jnp.concatenate([_rows_of(w["ssm_conv_w"], ROWS_CW4), _rows_of(w["conf_conv_w"], ROWS_CW31),
                              jnp.zeros((ROWS_CONV - ROWS_CW4 - ROWS_CW31, D_MODEL), F32)], axis=0)
    shards_b = [w[n][0].astype(BF16) for n, _, _ in BIG]
    w_in_all, conv_all = _gather_weights(shards_b[0], conv_f)
    cw4 = conv_all[:, 0:ROWS_CW4].reshape(N_CHIPS, -1)[:, :SSM_K * D_XBC // N_CHIPS]
    cw4 = cw4.reshape(N_CHIPS, SSM_K, D_XBC // N_CHIPS).transpose(1, 0, 2).reshape(SSM_K, D_XBC)
    cw31 = conv_all[:, ROWS_CW4:ROWS_CW4 + ROWS_CW31].reshape(N_CHIPS, -1)[:, :CONF_K * D_CONF // N_CHIPS]
    cw31 = cw31.reshape(N_CHIPS, CONF_K, D_CONF // N_CHIPS).transpose(1, 0, 2).reshape(CONF_K, D_CONF)

    small_names = [n for n, _, _ in SMALL_LAYOUT]
    sm = {n: w[n] for n in small_names}
    dx, grads = _local_grads(x[0], p[0, 0], loss_target[0], w_in_all, shards_b[1:], cw4, cw31, sm)

    chip_i = 2 * lax.axis_index("x") + lax.axis_index("y")
    mine = lax.switch(chip_i, [functools.partial(_own_columns, t) for t in range(N_CHIPS)], _add_slots(_reduce_w_in(grads["w_in"])))
    g_w_in_t = _share_halves(jnp.swapaxes(mine, 0, 1))

    out_g, out_d, out_m, out_v = {}, {}, {}, {}
    as_row = lambda a: a.reshape(1, -1)
    g_s, small_out = _adam_small(grads["small"], [as_row(w[n]) for n in small_names], [as_row(m[n]) for n in small_names],
                                 [as_row(v[n]) for n in small_names])
    loss = g_s[LOSS_ROW, 0]
    tr_ = lambda a: jnp.swapaxes(a, 0, 1)
    n = "w_in"
    d, nm, nv = _adam(tr_(w[n][0]), g_w_in_t, tr_(m[n][0]), tr_(v[n][0]), "adam_" + n)
    out_g[n], out_d[n], out_m[n], out_v[n] = [tr_(a)[None] for a in (g_w_in_t, d, nm, nv)]
    for n, r0, r, shape in CONV_LAYOUT:
        whole = g_s[r0:r0 + r].reshape(-1)[:shape[0] * shape[1]].reshape(shape)
        g = lax.dynamic_slice_in_dim(whole, chip_i * (shape[1] // N_CHIPS), shape[1] // N_CHIPS, axis=1)
        d, nm, nv = _adam(w[n][0], g, m[n][0], v[n][0], "adam_" + n)
        out_g[n], out_d[n], out_m[n], out_v[n] = g[None], d[None], nm[None], nv[None]
    for n, _, _ in BIG[1:]:
        out_g[n], out_d[n], out_m[n], out_v[n] = [a[None] for a in _adam_sum(w[n][0], grads[n], m[n][0], v[n][0], "adam_" + n)]
    for n, four in zip(small_names, small_out):
        out_g[n], out_d[n], out_m[n], out_v[n] = [a.reshape(w[n].shape) for a in four]
    return (loss, dx[None], *[out_g[n] for n in order], *[out_d[n] for n in order], *[out_m[n] for n in order],
            *[out_v[n] for n in order])
```

```python
import functools

import jax
import jax.numpy as jnp
from jax import lax
from jax.experimental import pallas as pl
from jax.experimental.pallas import tpu as pltpu

F32 = jnp.float32
BF16 = jnp.bfloat16

D_MODEL = 1024
D_PLE = 256
D_SSM = 1024
D_CONF = 1024
N_HEADS = 16
HEAD_DIM = 64
N_GROUPS = 2
N_STATE = 128
CHUNK = 128
SSM_K = 4
CONF_K = 31
D_XBC = D_SSM + 2 * N_GROUPS * N_STATE
D_IN = 5648
R_XBC, R_Z, R_GLU, R_CG, R_DT, D_INR = 0, 1536, 2560, 4608, 5632, 5760
LN_EPS = 1e-5
RMS_EPS = 1e-5
ALPHA = 2.0 ** 0.25
ADAM_LR, ADAM_B1, ADAM_B2, ADAM_EPS, ADAM_WD, ADAM_STEP = 0.001, 0.9, 0.999, 1e-08, 0.01, 10
NEG_BIG = -1e30
LANES = 128
SUBLANES = 8
VMEM_LIMIT = 56 * 1024 * 1024
HIGHEST = lax.Precision.HIGHEST
NT_DIMS = (((1,), (1,)), ((), ()))
TN_DIMS = (((0,), (0,)), ((), ()))


def _sig(v):
    return jax.nn.sigmoid(v)


def _dsilu(v, s):
    return s * (1.0 + v * (1.0 - s))


def _ln_stats(v):
    mu = jnp.mean(v, axis=-1, keepdims=True)
    c = v - mu
    var = jnp.mean(c * c, axis=-1, keepdims=True)
    rstd = lax.rsqrt(var + LN_EPS)
    return c * rstd, rstd


def _ln_bwd(dxhat, xhat, rstd):
    m1 = jnp.mean(dxhat, axis=-1, keepdims=True)
    m2 = jnp.mean(dxhat * xhat, axis=-1, keepdims=True)
    return rstd * (dxhat - m1 - xhat * m2)


def _softplus(v):
    return jnp.maximum(v, 0.0) + jnp.log1p(jnp.exp(-jnp.abs(v)))


def _colsum(v):
    return jnp.sum(v, axis=0, keepdims=True)


def _dot(a, b):
    return jnp.dot(a, b, preferred_element_type=F32)


def _dot_nt(a, b):
    return lax.dot_general(a, b, NT_DIMS, preferred_element_type=F32)


def _tile(tl, c, rev_of=None):
    if rev_of is None:
        return pl.BlockSpec((tl, c), lambda i: (i, 0))
    return pl.BlockSpec((tl, c), lambda i: (rev_of - 1 - i, 0))


def _full(shape, single=False):
    nd = len(shape)
    if single:
        return pl.BlockSpec(shape, lambda i: (0,) * nd, pipeline_mode=pl.Buffered(1))
    return pl.BlockSpec(shape, lambda i: (0,) * nd)


def _params(seq=True):
    return pltpu.CompilerParams(dimension_semantics=("arbitrary",) if seq else ("parallel",), vmem_limit_bytes=VMEM_LIMIT)


def _ln_inproj(x, g, b, w4, later, tl):
    n_tok = x.shape[0]
    n_t = n_tok // tl
    n_l = len(later)
    plan, own_plan = _gather_plans([(rows, GATHER_CHUNKS[a]) for a, (_, rows, _) in enumerate(BIG)][1:])

    def body(x_ref, g_ref, b_ref, w4_ref, *rest):
        part_refs = rest[0:n_l]
        h0b_ref, xbc_ref, z_ref, glu_ref, cg_ref, dtr_ref = rest[n_l:n_l + 6]
        all_refs, w_out = rest[n_l + 6:2 * n_l + 6], rest[2 * n_l + 6]
        w_ref, w_sem = rest[2 * n_l + 7:2 * n_l + 9]
        sems = rest[2 * n_l + 9:]
        i = pl.program_id(0)
        keep = pltpu.make_async_copy(w_ref, w_out, w_sem)

        @pl.when(i == 0)
        def _():
            _gather_stage(0, part_refs, all_refs, plan, own_plan, sems)
            for kcol, o, wd in SEGS:
                lo, hi = o, o + wd
                while lo < hi:
                    sh = lo // W_IN_SH
                    e = min(hi, (sh + 1) * W_IN_SH)
                    w_ref[:, kcol + lo - o:kcol + e - o] = w4_ref[sh, :, lo - sh * W_IN_SH:e - sh * W_IN_SH]
                    lo = e
            w_ref[:, R_DT + N_HEADS:D_INR] = jnp.zeros((D_MODEL, D_INR - R_DT - N_HEADS), BF16)
            keep.start()

        xhat, _ = _ln_stats(x_ref[...])
        hb = (xhat * g_ref[...] + b_ref[...]).astype(BF16)
        h0b_ref[...] = hb
        xbc_ref[...] = _dot(hb, w_ref[:, R_XBC:R_Z])
        z_ref[...] = _dot(hb, w_ref[:, R_Z:R_GLU])
        glu_ref[...] = _dot(hb, w_ref[:, R_GLU:R_CG])
        cg_ref[...] = _dot(hb, w_ref[:, R_CG:R_DT])
        dtr_ref[...] = _dot(hb, w_ref[:, R_DT:D_INR])

        @pl.when(i == (3 * n_t) // 4)
        def _():
            _gather_stage(1, part_refs, all_refs, plan, own_plan, sems)

        @pl.when(i == n_t - 1)
        def _():
            _gather_stage(2, part_refs, all_refs, plan, own_plan, sems)
            keep.wait()

    widths = (D_MODEL, D_XBC, D_SSM, 2 * D_CONF, D_CONF, LANES)
    dtypes = (BF16, F32, F32, F32, F32, F32)
    return pl.pallas_call(
        body, grid=(n_t,), name="ln_inproj",
        in_specs=[_tile(tl, D_MODEL), _full((1, D_MODEL)), _full((1, D_MODEL)), IN_VMEM] + [IN_VMEM] * n_l,
        out_specs=[_tile(tl, w) for w in widths] + [ANY] * (n_l + 1),
        out_shape=[jax.ShapeDtypeStruct((n_tok, w), dt) for w, dt in zip(widths, dtypes)]
        + [jax.ShapeDtypeStruct((N_CHIPS,) + a.shape, a.dtype) for a in later]
        + [jax.ShapeDtypeStruct((D_MODEL, D_INR), BF16)],
        scratch_shapes=[pltpu.VMEM((D_MODEL, D_INR), BF16), pltpu.SemaphoreType.DMA] + _gather_sems(plan, own_plan),
        compiler_params=_params(),
    )(x, g, b, w4, *later)


def _chunk_common(adt_c):
    row = lax.broadcasted_iota(jnp.int32, (CHUNK, CHUNK), 0)
    col = lax.broadcasted_iota(jnp.int32, (CHUNK, CHUNK), 1)
    tril = row >= col
    acs = jnp.dot(tril.astype(F32), adt_c, precision=HIGHEST, preferred_element_type=F32)
    last = acs[CHUNK - 1:CHUNK, :]
    return dict(row=row, col=col, tril=tril, lo=col < HEAD_DIM, acs=acs, acs_t=acs.T, e=jnp.exp(acs),
                dec=jnp.exp(last - acs), cd=jnp.exp(last))


def _decay_mask(cm, h):
    return jnp.exp(jnp.where(cm["tril"], cm["acs"][:, h:h + 1] - cm["acs_t"][h:h + 1, :], NEG_BIG))


def _head_lane_matrix():
    return (jnp.arange(D_SSM)[None, :] // HEAD_DIM == jnp.arange(LANES)[:, None]).astype(BF16)


def _per_head_lanes(v, exp_ref):
    hi = v.astype(BF16)
    lo = (v - hi.astype(F32)).astype(BF16)
    return _dot(hi, exp_ref[...]) + _dot(lo, exp_ref[...])


SSM_BWD_SHIFTS = (1, 2, 3)


def _ssd_fwd(xbc_in, z, dtr, cw, cb, dtb, alog, dsk, gnorm, tl):
    n_tok = xbc_in.shape[0]
    nq = tl // CHUNK

    def body(xin_ref, z_ref, dtr_ref, cw_ref, cb_ref, dtb_ref, alog_ref, dsk_ref, gn_ref, exp_ref,
             pre_ref, y_ref, yssm_ref, hprev_ref, buf, hst):
        @pl.when(pl.program_id(0) == 0)
        def _():
            buf[0:SUBLANES, :] = jnp.zeros((SUBLANES, D_XBC), F32)
            hst[...] = jnp.zeros_like(hst)

        buf[SUBLANES:SUBLANES + tl, :] = xin_ref[...]
        pre = cb_ref[...] + jnp.zeros((tl, D_XBC), F32)
        for k in range(SSM_K):
            off = SUBLANES - (SSM_K - 1) + k
            pre = pre + buf[off:off + tl, :] * cw_ref[k:k + 1, :]
        buf[0:SUBLANES, :] = buf[tl:tl + SUBLANES, :]
        pre_ref[...] = pre
        xbc = pre * _sig(pre)
        dt = _softplus(dtr_ref[...] + dtb_ref[...])
        a = -jnp.exp(alog_ref[...])
        adt = dt * a
        for q in range(nq):
            r0 = q * CHUNK
            cm = _chunk_common(adt[r0:r0 + CHUNK, :])
            dt_x = _per_head_lanes(dt[r0:r0 + CHUNK, :], exp_ref)
            e_x = _per_head_lanes(cm["e"], exp_ref)
            dec_x = _per_head_lanes(cm["dec"], exp_ref)
            for g in range(N_GROUPS):
                bg = xbc[r0:r0 + CHUNK, D_SSM + g * N_STATE:D_SSM + (g + 1) * N_STATE].astype(BF16)
                cg_ = xbc[r0:r0 + CHUNK, D_SSM + (N_GROUPS + g) * N_STATE:D_SSM + (N_GROUPS + g + 1) * N_STATE].astype(BF16)
                gm = _dot_nt(cg_, bg)
                for k in range(N_HEADS // N_GROUPS // 2):
                    ha = (N_HEADS // N_GROUPS) * g + 2 * k
                    c0 = ha * HEAD_DIM
                    xh2 = xbc[r0:r0 + CHUNK, c0:c0 + LANES]
                    x2 = xh2 * dt_x[:, c0:c0 + LANES]
                    x2b = x2.astype(BF16)
                    ya = _dot((gm * _decay_mask(cm, ha)).astype(BF16), x2b)
                    yb = _dot((gm * _decay_mask(cm, ha + 1)).astype(BF16), x2b)
                    h2 = hst[c0:c0 + LANES, :]
                    hprev_ref[q, c0:c0 + LANES, :] = h2
                    z2 = _dot_nt(cg_, h2.astype(BF16))
                    y2 = jnp.where(cm["lo"], ya, yb) + z2 * e_x[:, c0:c0 + LANES] + dsk_ref[:, c0:c0 + LANES] * xh2
                    y_ref[r0:r0 + CHUNK, c0:c0 + LANES] = y2
                    s2 = _dot((x2 * dec_x[:, c0:c0 + LANES]).T.astype(BF16), bg)
                    cd2 = jnp.where(cm["row"] < HEAD_DIM, cm["cd"][:, ha:ha + 1], cm["cd"][:, ha + 1:ha + 2])
                    hst[c0:c0 + LANES, :] = cd2 * h2 + s2
        yv = y_ref[...]
        zv = z_ref[...]
        yz = yv * (zv * _sig(zv))
        gw = D_SSM // N_GROUPS
        for g in range(N_GROUPS):
            seg = yz[:, g * gw:(g + 1) * gw]
            r = lax.rsqrt(jnp.mean(seg * seg, axis=-1, keepdims=True) + RMS_EPS)
            yssm_ref[:, g * gw:(g + 1) * gw] = (seg * r * gn_ref[:, g * gw:(g + 1) * gw]).astype(BF16)

    return pl.pallas_call(
        body, grid=(n_tok // tl,), name="ssd_fwd",
        in_specs=[_tile(tl, D_XBC), _tile(tl, D_SSM), _tile(tl, LANES), _full((SUBLANES, D_XBC)), _full((1, D_XBC)),
                  _full((1, LANES)), _full((1, LANES)), _full((1, D_SSM)), _full((1, D_SSM)), _full((LANES, D_SSM))],
        out_specs=[_tile(tl, D_XBC), _tile(tl, D_SSM), _tile(tl, D_SSM),
                   pl.BlockSpec((nq, D_SSM, N_STATE), lambda i: (i, 0, 0))],
        out_shape=[jax.ShapeDtypeStruct((n_tok, D_XBC), F32), jax.ShapeDtypeStruct((n_tok, D_SSM), F32),
                   jax.ShapeDtypeStruct((n_tok, D_SSM), BF16), jax.ShapeDtypeStruct((n_tok // CHUNK, D_SSM, N_STATE), F32)],
        scratch_shapes=[pltpu.VMEM((tl + SUBLANES, D_XBC), F32), pltpu.VMEM((D_SSM, N_STATE), F32)],
        compiler_params=_params(),
    )(xbc_in, z, dtr, cw, cb, dtb, alog, dsk, gnorm, _head_lane_matrix())


def _ssd_bwd(dys, y, z, pre, xbc_in, dtr, hprev, cw, dtb, alog, dsk, gnorm, specs, grads_b, tl):
    n_tok = y.shape[0]
    n_t = n_tok // tl
    nq = tl // CHUNK
    n_g, n_r = len(grads_b), len(specs)

    def body(dys_ref, y_ref, z_ref, pre_ref, xin_ref, dtr_ref, hprev_ref, cw_ref, dtb_ref, alog_ref, dsk_ref, gn_ref,
             exp_ref, redx_ref, redq_ref, *rest):
        g_refs, rest = rest[0:n_g], rest[n_g:]
        dxin_ref, dz_ref, ddtr_ref, dcw_ref, dcb_ref, dgn_ref, ddsk_ref, da_ref, ddtb_ref = rest[0:9]
        recvs, rest = rest[9:9 + n_r], rest[9 + n_r:]
        dxs, dh, cs_s, dskc, shifts, dwp, sums_q, sums_s = rest[0:8]
        sems = rest[8:]
        i = pl.program_id(0)

        @pl.when(i == 0)
        def _():
            _direct_stage(0, specs, g_refs, recvs, sems)
            dcw_ref[...] = jnp.zeros_like(dcw_ref)
            dwp[...] = jnp.zeros_like(dwp)
            dcb_ref[...] = jnp.zeros_like(dcb_ref)
            dgn_ref[...] = jnp.zeros_like(dgn_ref)
            da_ref[...] = jnp.zeros_like(da_ref)
            ddtb_ref[...] = jnp.zeros_like(ddtb_ref)
            dskc[...] = jnp.zeros_like(dskc)
            dh[...] = jnp.zeros_like(dh)
            dxs[tl:tl + SUBLANES, :] = jnp.zeros((SUBLANES, D_XBC), F32)

        yv = y_ref[...]
        zv = z_ref[...]
        dysv = dys_ref[...]
        sz = _sig(zv)
        silz = zv * sz
        yz = yv * silz
        gw = D_SSM // N_GROUPS
        dyz_parts = []
        for g in range(N_GROUPS):
            sl = slice(g * gw, (g + 1) * gw)
            seg = yz[:, sl]
            r = lax.rsqrt(jnp.mean(seg * seg, axis=-1, keepdims=True) + RMS_EPS)
            yzn = seg * r
            dgn_ref[:, sl] += _colsum(dysv[:, sl] * yzn)
            dyzn = dysv[:, sl] * gn_ref[:, sl]
            dyz_parts.append(r * (dyzn - yzn * jnp.mean(dyzn * yzn, axis=-1, keepdims=True)))
        dyz = jnp.concatenate(dyz_parts, axis=1)
        dy = dyz * silz
        dz_ref[...] = (dyz * yv * _dsilu(zv, sz)).astype(BF16)

        prev = pre_ref[...]
        sp = _sig(prev)
        xbc = prev * sp
        dskc[...] += _colsum(dy * xbc[:, 0:D_SSM])
        dt_in = dtr_ref[...] + dtb_ref[...]
        dt = _softplus(dt_in)
        dsp = _sig(dt_in)
        a = -jnp.exp(alog_ref[...])
        adt = dt * a
        for q in reversed(range(nq)):
            r0 = q * CHUNK
            cm = _chunk_common(adt[r0:r0 + CHUNK, :])
            row, col, lo = cm["row"], cm["col"], cm["lo"]
            triu = (col >= row).astype(F32)
            dt_c = dt[r0:r0 + CHUNK, :]
            dt_x = _per_head_lanes(dt_c, exp_ref)
            e_x = _per_head_lanes(cm["e"], exp_ref)
            dec_x = _per_head_lanes(cm["dec"], exp_ref)
            dcd_row = jnp.zeros((1, LANES), F32)
            for g in range(N_GROUPS):
                bcol = D_SSM + g * N_STATE
                ccol = D_SSM + (N_GROUPS + g) * N_STATE
                bg = xbc[r0:r0 + CHUNK, bcol:bcol + N_STATE].astype(BF16)
                cg_ = xbc[r0:r0 + CHUNK, ccol:ccol + N_STATE].astype(BF16)
                gm = _dot_nt(cg_, bg)
                dgm = jnp.zeros((CHUNK, CHUNK), F32)
                dbg = jnp.zeros((CHUNK, N_STATE), F32)
                dcg = jnp.zeros((CHUNK, N_STATE), F32)
                for k in range(N_HEADS // N_GROUPS // 2):
                    ha = (N_HEADS // N_GROUPS) * g + 2 * k
                    hb = ha + 1
                    c0 = ha * HEAD_DIM
                    xh2 = xbc[r0:r0 + CHUNK, c0:c0 + LANES]
                    dt2 = dt_x[:, c0:c0 + LANES]
                    x2 = xh2 * dt2
                    x2b = x2.astype(BF16)
                    la = _decay_mask(cm, ha)
                    lb = _decay_mask(cm, hb)
                    ma = gm * la
                    mb = gm * lb
                    dy2 = dy[r0:r0 + CHUNK, c0:c0 + LANES]
                    dy2b = dy2.astype(BF16)
                    dma = _dot_nt(jnp.where(lo, dy2, 0.0).astype(BF16), x2b)
                    dmb = _dot_nt(jnp.where(lo, 0.0, dy2).astype(BF16), x2b)
                    dx2 = jnp.where(lo, _dot(ma.T.astype(BF16), dy2b), _dot(mb.T.astype(BF16), dy2b))
                    dgm = dgm + dma * la + dmb * lb
                    sums_q[:, ha * CHUNK:(ha + 1) * CHUNK] = (dma * ma).astype(BF16)
                    sums_q[:, hb * CHUNK:(hb + 1) * CHUNK] = (dmb * mb).astype(BF16)
                    h2 = hprev_ref[q, c0:c0 + LANES, :]
                    h2b = h2.astype(BF16)
                    dz2 = dy2 * e_x[:, c0:c0 + LANES]
                    dcg = dcg + _dot(dz2.astype(BF16), h2b)
                    sums_s[0, :, c0:c0 + LANES] = (dz2 * _dot_nt(cg_, h2b)).astype(BF16)
                    dhn = dh[c0:c0 + LANES, :]
                    dhnb = dhn.astype(BF16)
                    cd_a = cm["cd"][:, ha:ha + 1]
                    cd_b = cm["cd"][:, hb:hb + 1]
                    top = row < HEAD_DIM
                    hh = dhn * h2
                    dcd_a = jnp.sum(_colsum(jnp.where(top, hh, 0.0)), axis=1, keepdims=True)
                    dcd_b = jnp.sum(_colsum(jnp.where(top, 0.0, hh)), axis=1, keepdims=True)
                    dcd_row = dcd_row + jnp.where(col[0:1, :] == ha, dcd_a * cd_a, 0.0) + jnp.where(col[0:1, :] == hb, dcd_b * cd_b, 0.0)
                    dh[c0:c0 + LANES, :] = jnp.where(top, cd_a, cd_b) * dhn + _dot(dz2.T.astype(BF16), cg_)
                    w2 = _dot_nt(bg, dhnb)
                    dec2 = dec_x[:, c0:c0 + LANES]
                    dx2 = dx2 + dec2 * w2
                    sums_s[1, :, c0:c0 + LANES] = (x2 * w2).astype(BF16)
                    dbg = dbg + _dot((x2 * dec2).astype(BF16), dhnb)
                    sums_s[2, :, c0:c0 + LANES] = (dx2 * xh2).astype(BF16)
                    dxs[r0:r0 + CHUNK, c0:c0 + LANES] = dx2 * dt2 + dsk_ref[:, c0:c0 + LANES] * dy2
                dgmb = dgm.astype(BF16)
                dxs[r0:r0 + CHUNK, bcol:bcol + N_STATE] = dbg + _dot(dgm.T.astype(BF16), cg_)
                dxs[r0:r0 + CHUNK, ccol:ccol + N_STATE] = dcg + _dot(dgmb, bg)
            q_all = sums_q[...]
            q_cols = _dot(jnp.ones((SUBLANES, CHUNK), BF16), q_all)
            cs_s[...] = jnp.zeros_like(cs_s)
            for h in range(N_HEADS):
                cs_s[h:h + 1, :] = q_cols[0:1, h * CHUNK:(h + 1) * CHUNK]
            de = _dot(sums_s[0], redx_ref[...])
            dd = _dot(sums_s[1], redx_ref[...]) * cm["dec"]
            ddtx = _dot(sums_s[2], redx_ref[...])
            is_last = row == CHUNK - 1
            dacs = _dot(q_all, redq_ref[...]) - cs_s[...].T + de - dd + jnp.where(is_last, dcd_row + _colsum(dd), 0.0)
            dadt = jnp.dot(triu, dacs, precision=HIGHEST, preferred_element_type=F32)
            da_ref[...] += _colsum(dadt * dt_c)
            ddtr_c = (dadt * a + ddtx) * dsp[r0:r0 + CHUNK, :]
            ddtr_ref[r0:r0 + CHUNK, :] = ddtr_c.astype(BF16)
            ddtb_ref[...] += _colsum(ddtr_c)

        dpre = dxs[0:tl, :] * _dsilu(prev, sp)
        dxs[0:tl, :] = dpre
        dcb_ref[...] += _colsum(dpre)
        _shift_copies(dxs, shifts, tl, SSM_BWD_SHIFTS)

        def strip(rb, carry):
            i0 = pl.multiple_of(rb * CONV_RS, CONV_RS)
            for c0 in range(0, D_XBC, CONV_CS):
                xin_s = xin_ref[pl.ds(i0, CONV_RS), c0:c0 + CONV_CS]
                acc = jnp.zeros((CONV_RS, CONV_CS), F32)
                for k in range(SSM_K):
                    sh = _tap_rows(dxs, shifts, SSM_K - 1 - k, i0, c0, SSM_BWD_SHIFTS)
                    acc = acc + sh * cw_ref[k:k + 1, c0:c0 + CONV_CS]
                    t = xin_s * sh
                    dwp[k * SUBLANES:(k + 1) * SUBLANES, c0:c0 + CONV_CS] += _fold_rows(t)
                dxin_ref[pl.ds(i0, CONV_RS), c0:c0 + CONV_CS] = acc.astype(BF16)
            return carry

        lax.fori_loop(0, tl // CONV_RS, strip, 0)
        dxs[tl:tl + SUBLANES, :] = dxs[0:SUBLANES, :]

        @pl.when(i == n_t - 1)
        def _():
            for k in range(SSM_K):
                dcw_ref[k:k + 1, :] = _colsum(dwp[k * SUBLANES:(k + 1) * SUBLANES, :])
            da_ref[...] = da_ref[...] * a
            sel = (lax.broadcasted_iota(jnp.int32, (D_SSM, LANES), 0) // HEAD_DIM
                   == lax.broadcasted_iota(jnp.int32, (D_SSM, LANES), 1)).astype(F32)
            rows = jnp.broadcast_to(dskc[...], (SUBLANES, D_SSM))
            ddsk_ref[...] = jnp.dot(rows, sel, precision=HIGHEST, preferred_element_type=F32)[0:1, :]
            _direct_stage(1, specs, g_refs, recvs, sems)

    head_lanes = _head_lane_matrix()
    per_head = (jnp.arange(N_HEADS * CHUNK)[:, None] // CHUNK == jnp.arange(LANES)[None, :]).astype(BF16)
    rev = functools.partial(_tile, tl, rev_of=n_t)
    return pl.pallas_call(
        body, grid=(n_t,), name="ssd_bwd",
        in_specs=[rev(D_SSM), rev(D_SSM), rev(D_SSM), rev(D_XBC), rev(D_XBC), rev(LANES),
                  pl.BlockSpec((nq, D_SSM, N_STATE), lambda i: (n_t - 1 - i, 0, 0)),
                  _full((SUBLANES, D_XBC)), _full((1, LANES)), _full((1, LANES)), _full((1, D_SSM)), _full((1, D_SSM)),
                  _full((LANES, D_SSM)), _full((D_SSM, LANES)), _full((N_HEADS * CHUNK, LANES))] + [IN_VMEM] * n_g,
        out_specs=[rev(D_XBC), rev(D_SSM), rev(LANES), _full((SUBLANES, D_XBC)), _full((1, D_XBC)), _full((1, D_SSM)),
                   _full((1, LANES)), _full((1, LANES)), _full((1, LANES))] + [ANY] * n_r,
        out_shape=[jax.ShapeDtypeStruct((n_tok, D_XBC), BF16), jax.ShapeDtypeStruct((n_tok, D_SSM), BF16),
                   jax.ShapeDtypeStruct((n_tok, LANES), BF16), jax.ShapeDtypeStruct((SUBLANES, D_XBC), F32),
                   jax.ShapeDtypeStruct((1, D_XBC), F32), jax.ShapeDtypeStruct((1, D_SSM), F32),
                   jax.ShapeDtypeStruct((1, LANES), F32), jax.ShapeDtypeStruct((1, LANES), F32),
                   jax.ShapeDtypeStruct((1, LANES), F32)]
        + [jax.ShapeDtypeStruct((N_DEV, rows, cols), BF16) for rows, cols, _, _ in specs],
        scratch_shapes=[pltpu.VMEM((tl + SUBLANES, D_XBC), F32), pltpu.VMEM((D_SSM, N_STATE), F32),
                        pltpu.VMEM((CHUNK, LANES), F32), pltpu.VMEM((1, D_SSM), F32),
                        pltpu.VMEM((len(SSM_BWD_SHIFTS), tl, D_XBC), F32), pltpu.VMEM((SSM_K * SUBLANES, D_XBC), F32),
                        pltpu.VMEM((CHUNK, N_HEADS * CHUNK), BF16), pltpu.VMEM((3, CHUNK, D_SSM), BF16)]
        + _direct_sems(specs),
        compiler_params=_params(),
    )(dys, y, z, pre, xbc_in, dtr, hprev, cw, dtb, alog, dsk, gnorm, head_lanes, head_lanes.T, per_head, *grads_b)


CONF_HALO = 32
CONV_RS = 32
CONV_CS = 256


ALL_SHIFTS = tuple(range(1, SUBLANES))


def _shift_copies(buf, shifts, n_rows, residues=ALL_SHIFTS):
    for j, r in enumerate(residues):
        shifts[j, 0:n_rows, :] = buf[r:r + n_rows, :]


def _fold_rows(t):
    part = t[0:SUBLANES]
    for j in range(1, t.shape[0] // SUBLANES):
        part = part + t[j * SUBLANES:(j + 1) * SUBLANES]
    return part


def _tap_rows(buf, shifts, off, i0, c0, residues=ALL_SHIFTS):
    q, r = divmod(off, SUBLANES)
    rows = pl.ds(pl.multiple_of(i0 + SUBLANES * q, SUBLANES), CONV_RS)
    if r == 0:
        return buf[rows, c0:c0 + CONV_CS]
    return shifts[residues.index(r), rows, c0:c0 + CONV_CS]


def _conf_fwd(glu, cgate, bglu, cw, cb, lg, lb, tl):
    n_tok = glu.shape[0]

    def body(glu_ref, cg_ref, bglu_ref, cw_ref, cb_ref, lg_ref, lb_ref, u0_ref, u1_ref, yc_ref, buf, shifts):
        @pl.when(pl.program_id(0) == 0)
        def _():
            buf[0:CONF_HALO, :] = jnp.zeros((CONF_HALO, D_CONF), F32)

        gl = glu_ref[...] + bglu_ref[...]
        u0 = gl[:, 0:D_CONF] * _sig(gl[:, D_CONF:2 * D_CONF])
        u0_ref[...] = u0
        buf[CONF_HALO:CONF_HALO + tl, :] = u0
        _shift_copies(buf, shifts, tl + CONF_HALO - SUBLANES)

        def strip(rb, carry):
            i0 = pl.multiple_of(rb * CONV_RS, CONV_RS)
            for c0 in range(0, D_CONF, CONV_CS):
                acc = jnp.broadcast_to(cb_ref[:, c0:c0 + CONV_CS], (CONV_RS, CONV_CS))
                for k in range(CONF_K):
                    acc = acc + _tap_rows(buf, shifts, CONF_HALO - (CONF_K - 1) + k, i0, c0) * cw_ref[k:k + 1, c0:c0 + CONV_CS]
                u1_ref[pl.ds(i0, CONV_RS), c0:c0 + CONV_CS] = acc
            return carry

        lax.fori_loop(0, tl // CONV_RS, strip, 0)
        buf[0:CONF_HALO, :] = buf[tl:tl + CONF_HALO, :]
        xhat, _ = _ln_stats(u1_ref[...])
        n = xhat * lg_ref[...] + lb_ref[...]
        cgv = cg_ref[...]
        yc_ref[...] = (n * _sig(n) * (cgv * _sig(cgv))).astype(BF16)

    return pl.pallas_call(
        body, grid=(n_tok // tl,), name="conf_fwd",
        in_specs=[_tile(tl, 2 * D_CONF), _tile(tl, D_CONF), _full((1, 2 * D_CONF)), _full((CONF_HALO, D_CONF)),
                  _full((1, D_CONF)), _full((1, D_CONF)), _full((1, D_CONF))],
        out_specs=[_tile(tl, D_CONF)] * 3,
        out_shape=[jax.ShapeDtypeStruct((n_tok, D_CONF), F32), jax.ShapeDtypeStruct((n_tok, D_CONF), F32),
                   jax.ShapeDtypeStruct((n_tok, D_CONF), BF16)],
        scratch_shapes=[pltpu.VMEM((tl + CONF_HALO, D_CONF), F32),
                        pltpu.VMEM((SUBLANES - 1, tl + CONF_HALO - SUBLANES, D_CONF), F32)],
        compiler_params=_params(),
    )(glu, cgate, bglu, cw, cb, lg, lb)


def _conf_bwd(dyc, dyc_block, u0, u1, glu, cgate, bglu, cw, lg, lb, specs, grads_b, tl):
    n_tok = glu.shape[0]
    n_t = n_tok // tl
    n_g, n_r = len(grads_b), len(specs)

    def body(dyc_ref, u0_ref, u1_ref, glu_ref, cg_ref, bglu_ref, cw_ref, lg_ref, lb_ref, *rest):
        g_refs, rest = rest[0:n_g], rest[n_g:]
        dglu_ref, dcg_ref, dcw_ref, dbglu_ref, small_ref = rest[0:5]
        recvs, rest = rest[5:5 + n_r], rest[5 + n_r:]
        buf, shifts, du0_s, dwp = rest[0:4]
        sems = rest[4:]

        @pl.when(pl.program_id(0) == 0)
        def _():
            _direct_stage(0, specs, g_refs, recvs, sems)
            dwp[...] = jnp.zeros_like(dwp)
            dbglu_ref[...] = jnp.zeros_like(dbglu_ref)
            small_ref[...] = jnp.zeros_like(small_ref)
            buf[tl:tl + CONF_HALO, :] = jnp.zeros((CONF_HALO, D_CONF), F32)

        xhat, rstd = _ln_stats(u1_ref[...])
        n = xhat * lg_ref[...] + lb_ref[...]
        sn = _sig(n)
        cgv = cg_ref[...]
        scg = _sig(cgv)
        dycv = dyc_ref[...]
        dcg_ref[...] = (dycv * (n * sn) * _dsilu(cgv, scg)).astype(BF16)
        dn = dycv * (cgv * scg) * _dsilu(n, sn)
        small_ref[0:1, :] += _colsum(dn * xhat)
        small_ref[1:2, :] += _colsum(dn)
        du1 = _ln_bwd(dn * lg_ref[...], xhat, rstd)
        small_ref[2:3, :] += _colsum(du1)
        buf[0:tl, :] = du1
        _shift_copies(buf, shifts, tl + CONF_HALO - SUBLANES)

        def strip(rb, carry):
            i0 = pl.multiple_of(rb * CONV_RS, CONV_RS)
            for c0 in range(0, D_CONF, CONV_CS):
                u0s = u0_ref[pl.ds(i0, CONV_RS), c0:c0 + CONV_CS]
                acc = jnp.zeros((CONV_RS, CONV_CS), F32)
                for k in range(CONF_K):
                    sh = _tap_rows(buf, shifts, CONF_K - 1 - k, i0, c0)
                    acc = acc + sh * cw_ref[k:k + 1, c0:c0 + CONV_CS]
                    t = u0s * sh
                    dwp[k * SUBLANES:(k + 1) * SUBLANES, c0:c0 + CONV_CS] += _fold_rows(t)
                du0_s[pl.ds(i0, CONV_RS), c0:c0 + CONV_CS] = acc
            return carry

        lax.fori_loop(0, tl // CONV_RS, strip, 0)
        du0 = du0_s[...]
        buf[tl:tl + CONF_HALO, :] = buf[0:CONF_HALO, :]
        gl = glu_ref[...] + bglu_ref[...]
        sg = _sig(gl[:, D_CONF:2 * D_CONF])
        dgv = du0 * sg
        dgg = du0 * gl[:, 0:D_CONF] * sg * (1.0 - sg)
        dglu_ref[:, 0:D_CONF] = dgv.astype(BF16)
        dglu_ref[:, D_CONF:2 * D_CONF] = dgg.astype(BF16)
        dbglu_ref[:, 0:D_CONF] += _colsum(dgv)
        dbglu_ref[:, D_CONF:2 * D_CONF] += _colsum(dgg)

        @pl.when(pl.program_id(0) == n_t - 1)
        def _():
            for k in range(CONF_HALO):
                dcw_ref[k:k + 1, :] = _colsum(dwp[k * SUBLANES:(k + 1) * SUBLANES, :])
            _direct_stage(1, specs, g_refs, recvs, sems)

    rev = functools.partial(_tile, tl, rev_of=n_t)
    return pl.pallas_call(
        body, grid=(n_t,), name="conf_bwd",
        in_specs=[pl.BlockSpec((tl, D_CONF), lambda i: (n_t - 1 - i, dyc_block)),
                  rev(D_CONF), rev(D_CONF), rev(2 * D_CONF), rev(D_CONF), _full((1, 2 * D_CONF)),
                  _full((CONF_HALO, D_CONF)), _full((1, D_CONF)), _full((1, D_CONF))] + [IN_VMEM] * n_g,
        out_specs=[rev(2 * D_CONF), rev(D_CONF), _full((CONF_HALO, D_CONF)), _full((1, 2 * D_CONF)), _full((SUBLANES, D_CONF))]
        + [ANY] * n_r,
        out_shape=[jax.ShapeDtypeStruct((n_tok, 2 * D_CONF), BF16), jax.ShapeDtypeStruct((n_tok, D_CONF), BF16),
                   jax.ShapeDtypeStruct((CONF_HALO, D_CONF), F32), jax.ShapeDtypeStruct((1, 2 * D_CONF), F32),
                   jax.ShapeDtypeStruct((SUBLANES, D_CONF), F32)]
        + [jax.ShapeDtypeStruct((N_DEV, rows, cols), BF16) for rows, cols, _, _ in specs],
        scratch_shapes=[pltpu.VMEM((tl + CONF_HALO, D_CONF), F32),
                        pltpu.VMEM((SUBLANES - 1, tl + CONF_HALO - SUBLANES, D_CONF), F32),
                        pltpu.VMEM((tl, D_CONF), F32), pltpu.VMEM((CONF_HALO * SUBLANES, D_CONF), F32)] + _direct_sems(specs),
        compiler_params=_params(),
    )(dyc, u0, u1, glu, cgate, bglu, cw, lg, lb, *grads_b)


def _tail(x, yssm, yconf, p, tgt, vec, w_out, wpg, wpp, tl):
    n_tok = x.shape[0]

    def body(x_ref, ys_ref, yc_ref, p_ref, t_ref, vec_ref, wo_ref, wg_ref, wp_ref,
             dmix_ref, dr1_ref, dr1b_ref, h1b_ref, dgb_ref, dpb_ref, small_ref, loss_ref):
        @pl.when(pl.program_id(0) == 0)
        def _():
            small_ref[...] = jnp.zeros_like(small_ref)
            loss_ref[...] = jnp.zeros_like(loss_ref)

        xh0, _ = _ln_stats(x_ref[...])
        h0 = xh0 * vec_ref[0:1, :] + vec_ref[1:2, :]
        out = _dot(ys_ref[...], wo_ref[0:D_SSM, :]) + _dot(yc_ref[...], wo_ref[D_SSM:D_SSM + D_CONF, :]) + vec_ref[2:3, :]
        xh1, rstd1 = _ln_stats(ALPHA * h0 + out)
        h1 = xh1 * vec_ref[3:4, :] + vec_ref[4:5, :]
        h1b = h1.astype(BF16)
        h1b_ref[...] = h1b
        gate = _sig(_dot(h1b, wg_ref[...]))
        ple = _dot(p_ref[...].astype(BF16), wp_ref[...])
        xh2, rstd2 = _ln_stats(ALPHA * h1 + gate * ple)
        h2 = xh2 * vec_ref[5:6, :] + vec_ref[6:7, :]
        diff = h2 - t_ref[...]
        part = jnp.sum(jnp.sum(diff * diff, axis=1, keepdims=True), axis=0, keepdims=True) * (0.5 / D_MODEL)
        loss_ref[...] += jnp.broadcast_to(part, loss_ref.shape)
        dh2 = diff * (1.0 / D_MODEL)
        small_ref[3:4, :] += _colsum(dh2 * xh2)
        small_ref[4:5, :] += _colsum(dh2)
        dr2 = _ln_bwd(dh2 * vec_ref[5:6, :], xh2, rstd2)
        dgpre = (dr2 * ple * gate * (1.0 - gate)).astype(BF16)
        dgb_ref[...] = dgpre
        dpb_ref[...] = (dr2 * gate).astype(BF16)
        dh1 = ALPHA * dr2 + _dot_nt(dgpre, wg_ref[...])
        small_ref[1:2, :] += _colsum(dh1 * xh1)
        small_ref[2:3, :] += _colsum(dh1)
        dr1 = _ln_bwd(dh1 * vec_ref[3:4, :], xh1, rstd1)
        small_ref[0:1, :] += _colsum(dr1)
        dr1_ref[...] = dr1
        dr1b = dr1.astype(BF16)
        dr1b_ref[...] = dr1b
        dmix_ref[...] = _dot_nt(dr1b, wo_ref[...])

    d_mix = D_SSM + D_CONF
    return pl.pallas_call(
        body, grid=(n_tok // tl,), name="tail",
        in_specs=[_tile(tl, D_MODEL), _tile(tl, D_SSM), _tile(tl, D_CONF), _tile(tl, D_PLE), _tile(tl, D_MODEL),
                  _full((SUBLANES, D_MODEL)), _full((d_mix, D_MODEL), True), _full((D_MODEL, D_MODEL), True),
                  _full((D_PLE, D_MODEL), True)],
        out_specs=[_tile(tl, d_mix), _tile(tl, D_MODEL), _tile(tl, D_MODEL), _tile(tl, D_MODEL), _tile(tl, D_MODEL),
                   _tile(tl, D_MODEL), _full((SUBLANES, D_MODEL)), _full((SUBLANES, LANES))],
        out_shape=[jax.ShapeDtypeStruct((n_tok, d_mix), F32), jax.ShapeDtypeStruct((n_tok, D_MODEL), F32),
                   jax.ShapeDtypeStruct((n_tok, D_MODEL), BF16), jax.ShapeDtypeStruct((n_tok, D_MODEL), BF16),
                   jax.ShapeDtypeStruct((n_tok, D_MODEL), BF16), jax.ShapeDtypeStruct((n_tok, D_MODEL), BF16),
                   jax.ShapeDtypeStruct((SUBLANES, D_MODEL), F32), jax.ShapeDtypeStruct((SUBLANES, LANES), F32)],
        compiler_params=_params(),
    )(x, yssm, yconf, p, tgt, vec, w_out, wpg, wpp)


def _inproj_bwd(dxin, dz, dglu, dcg, ddtr, dr1, x, g, b, w_r, tl):
    n_tok = x.shape[0]

    def body(dxin_ref, dz_ref, dglu_ref, dcg_ref, ddtr_ref, dr1_ref, x_ref, g_ref, b_ref, w_ref,
             dx_ref, dpb_ref, small_ref):
        @pl.when(pl.program_id(0) == 0)
        def _():
            small_ref[...] = jnp.zeros_like(small_ref)

        dh0 = ALPHA * dr1_ref[...]
        for ref, lo, hi in ((dxin_ref, R_XBC, R_Z), (dz_ref, R_Z, R_GLU), (dglu_ref, R_GLU, R_CG), (dcg_ref, R_CG, R_DT),
                            (ddtr_ref, R_DT, D_INR)):
            piece = ref[...].astype(BF16)
            dpb_ref[:, lo:hi] = piece
            dh0 = dh0 + _dot_nt(piece, w_ref[:, lo:hi])
        xhat, rstd = _ln_stats(x_ref[...])
        small_ref[0:1, :] += _colsum(dh0 * xhat)
        small_ref[1:2, :] += _colsum(dh0)
        dx_ref[...] = _ln_bwd(dh0 * g_ref[...], xhat, rstd)

    return pl.pallas_call(
        body, grid=(n_tok // tl,), name="inproj_bwd",
        in_specs=[_tile(tl, D_XBC), _tile(tl, D_SSM), _tile(tl, 2 * D_CONF), _tile(tl, D_CONF), _tile(tl, LANES),
                  _tile(tl, D_MODEL), _tile(tl, D_MODEL), _full((1, D_MODEL)), _full((1, D_MODEL)),
                  _full((D_MODEL, D_INR), True)],
        out_specs=[_tile(tl, D_MODEL), _tile(tl, D_INR), _full((SUBLANES, D_MODEL))],
        out_shape=[jax.ShapeDtypeStruct((n_tok, D_MODEL), F32), jax.ShapeDtypeStruct((n_tok, D_INR), BF16),
                   jax.ShapeDtypeStruct((SUBLANES, D_MODEL), F32)],
        compiler_params=_params(),
    )(dxin, dz, dglu, dcg, ddtr, dr1, x, g, b, w_r)


def _tn_matmul(a, b, name, tn, tl, out_dtype=F32):
    n_tok, m = a.shape
    n = b.shape[1]
    n_l = n_tok // tl
    direct = out_dtype == F32

    def body(a_ref, b_ref, o_ref, *scratch):
        acc = o_ref if direct else scratch[0]

        @pl.when(pl.program_id(1) == 0)
        def _():
            acc[...] = jnp.zeros_like(acc)

        acc[...] += lax.dot_general(a_ref[...], b_ref[...], TN_DIMS, preferred_element_type=F32)
        if not direct:
            @pl.when(pl.program_id(1) == n_l - 1)
            def _():
                o_ref[...] = acc[...].astype(out_dtype)

    return pl.pallas_call(
        body, grid=(n // tn, n_l), name=name,
        in_specs=[pl.BlockSpec((tl, m), lambda j, l: (l, 0)), pl.BlockSpec((tl, tn), lambda j, l: (l, j))],
        out_specs=pl.BlockSpec((m, tn), lambda j, l: (0, j)),
        out_shape=jax.ShapeDtypeStruct((m, n), out_dtype),
        scratch_shapes=[] if direct else [pltpu.VMEM((m, tn), F32)],
        compiler_params=pltpu.CompilerParams(dimension_semantics=("parallel", "arbitrary"), vmem_limit_bytes=VMEM_LIMIT),
    )(a, b)


def _dw_in(a, b, spack, tn, tl):
    n_tok, m = a.shape
    n = b.shape[1]
    n_j, n_l = n // tn, n_tok // tl

    def body(a_ref, b_ref, s_ref, o_ref, small_all, send_sems, recv_sems, loc_sem):
        j, l = pl.program_id(0), pl.program_id(1)

        @pl.when(jnp.logical_and(j == 0, l == 0))
        def _():
            _gather8_stage(0, s_ref, small_all, send_sems, recv_sems, loc_sem)

        @pl.when(l == 0)
        def _():
            o_ref[...] = jnp.zeros_like(o_ref)

        o_ref[...] += lax.dot_general(a_ref[...], b_ref[...], TN_DIMS, preferred_element_type=F32)

        @pl.when(jnp.logical_and(j == n_j - 1, l == n_l - 1))
        def _():
            _gather8_stage(1, s_ref, small_all, send_sems, recv_sems, loc_sem)

    dma = pltpu.SemaphoreType.DMA
    return pl.pallas_call(
        body, grid=(n_j, n_l), name="dw_in",
        in_specs=[pl.BlockSpec((tl, m), lambda j, l: (l, 0)), pl.BlockSpec((tl, tn), lambda j, l: (l, j)), IN_VMEM],
        out_specs=[pl.BlockSpec((m, tn), lambda j, l: (0, j)), ANY],
        out_shape=[jax.ShapeDtypeStruct((m, n), F32), jax.ShapeDtypeStruct((N_DEV,) + spack.shape, F32)],
        scratch_shapes=[dma((N_DEV - 1,)), dma((N_DEV - 1,)), dma],
        compiler_params=pltpu.CompilerParams(dimension_semantics=("arbitrary", "arbitrary"), vmem_limit_bytes=VMEM_LIMIT),
    )(a, b, spack)


def _pad_rows(a, rows):
    return jnp.pad(a, ((0, rows - a.shape[0]), (0, 0)))


def _pad_lanes(a):
    return jnp.pad(a, ((0, 0), (0, LANES - a.shape[1])))


def _local_grads(x, p, tgt, w4, later, ssm_cw, conf_cw, sm):
    n_tok = x.shape[0]
    tl = min(256, n_tok)
    tlm = min(512, n_tok)
    row = lambda v: v.reshape(1, -1)
    g_e, b_e = row(sm["ln_emb_g"]), row(sm["ln_emb_b"])
    h0b, xbc_in, z, glu, cgate, dtr, w_out_all, wpg_all, wpp_all, w_r = _ln_inproj(x, g_e, b_e, w4, later, tl)
    w_out_b = w_out_all.reshape(D_SSM + D_CONF, D_MODEL)
    wpg_b = wpg_all.reshape(D_MODEL, D_MODEL)
    wpp_b = wpp_all.transpose(1, 0, 2).reshape(D_PLE, D_MODEL)

    cw4 = _pad_rows(ssm_cw, SUBLANES)
    dtb, alog = _pad_lanes(sm["dt_bias"]), _pad_lanes(sm["a_log"])
    dsk = jnp.repeat(sm["d_skip"], HEAD_DIM, axis=1)
    pre, y, yssm, hprev = _ssd_fwd(xbc_in, z, dtr, cw4, sm["ssm_conv_b"], dtb, alog, dsk, sm["ssm_norm_g"], tl)

    cw31 = _pad_rows(conf_cw, CONF_HALO)
    u0, u1, yconf = _conf_fwd(glu, cgate, sm["b_glu"], cw31, sm["conf_conv_b"], sm["conf_ln_g"], sm["conf_ln_b"], tl)

    vec = jnp.concatenate([g_e, b_e, sm["b_out"], sm["ln1_g"], sm["ln1_b"], sm["ln2_g"], sm["ln2_b"],
                           jnp.zeros((1, D_MODEL), F32)], axis=0)
    dmix, dr1, dr1b, h1b, dgb, dpb, small_t, loss = _tail(
        x, yssm, yconf, p, tgt, vec, w_out_b, wpg_b, wpp_b, tlm)

    tlw = min(2048, n_tok)
    dwo = [_tn_matmul(yssm, dr1b, "dw_out_ssm", D_MODEL, n_tok, BF16), _tn_matmul(yconf, dr1b, "dw_out_conf", D_MODEL, n_tok, BF16)]
    dwp_ = [_tn_matmul(h1b, dgb, "dw_ple_gate", D_MODEL, n_tok, BF16),
            _tn_matmul(p.astype(BF16), dpb, "dw_ple_proj", D_MODEL, n_tok, BF16)]
    dglu, dcg, dcw31, dbglu, small_c, dw_out = _conf_bwd(dmix, D_SSM // D_CONF, u0, u1, glu, cgate, sm["b_glu"], cw31,
                                                          sm["conf_ln_g"], sm["conf_ln_b"], [W_OUT_DIRECT], dwo, tl)
    dxin, dz, ddtr, dcw4, dcb4, dgn, ddsk, dalog, ddtb, dwpg, dwpp = _ssd_bwd(
        dmix, y, z, pre, xbc_in, dtr, hprev, cw4, dtb, alog, dsk, sm["ssm_norm_g"], [WPG_DIRECT, WPP_DIRECT], dwp_, tl)
    dx, dprojb, small_e = _inproj_bwd(dxin, dz, dglu, dcg, ddtr, dr1, x, g_e, b_e, w_r, tlm)
    spack = _pack_small(dict(
        ln_emb_g=small_e[0], ln_emb_b=small_e[1], ssm_conv_w=dcw4[0:SSM_K], ssm_conv_b=dcb4,
        dt_bias=ddtb[:, 0:N_HEADS], a_log=dalog[:, 0:N_HEADS], d_skip=ddsk[:, 0:N_HEADS], ssm_norm_g=dgn, b_glu=dbglu,
        conf_conv_w=dcw31[0:CONF_K], conf_conv_b=small_c[2:3], conf_ln_g=small_c[0:1], conf_ln_b=small_c[1:2],
        b_out=small_t[0:1], ln1_g=small_t[1:2], ln1_b=small_t[2:3], ln2_g=small_t[3:4], ln2_b=small_t[4:5]), loss[0, 0])
    dw_r, small_all = _dw_in(h0b, dprojb, spack, D_INR // 3, tlw)
    return dx, dict(w_in=dw_r, w_out=dw_out, w_ple_gate=dwpg, w_ple_proj=dwpp, small=small_all)


N_CHIPS = 4
N_DEV = 8
W_IN_SH = D_IN // N_CHIPS
BIG = (("w_in", D_MODEL, W_IN_SH), ("w_out", (D_SSM + D_CONF) // N_CHIPS, D_MODEL),
       ("w_ple_gate", D_MODEL // N_CHIPS, D_MODEL), ("w_ple_proj", D_PLE, D_MODEL // N_CHIPS))
SEGS = ((R_XBC, 0, D_SSM), (R_XBC + D_SSM, 2048, 256), (R_XBC + D_SSM + 256, 2304, 256), (R_Z, 1024, D_SSM),
        (R_GLU, 2576, 2 * D_CONF), (R_CG, 4624, D_CONF), (R_DT, 2560, N_HEADS))
ROWS_CW4 = 2
ROWS_CW31 = 8
ROWS_CONV = 16
SMALL_ROWS = 56
SMALL_LAYOUT = (("ln_emb_g", 0, 1), ("ln_emb_b", 1, 1), ("ssm_conv_b", 2, 2), ("dt_bias", 4, 1), ("a_log", 5, 1),
                ("d_skip", 6, 1), ("ssm_norm_g", 7, 1), ("b_glu", 8, 2), ("conf_conv_b", 10, 1), ("conf_ln_g", 11, 1),
                ("conf_ln_b", 12, 1), ("b_out", 13, 1), ("ln1_g", 14, 1), ("ln1_b", 15, 1), ("ln2_g", 16, 1), ("ln2_b", 17, 1))
CONV_LAYOUT = (("ssm_conv_w", 18, 6, (SSM_K, D_XBC)), ("conf_conv_w", 24, 31, (CONF_K, D_CONF)))


def _rows_of(v, rows):
    flat = v.reshape(-1)
    return jnp.pad(flat, (0, rows * D_MODEL - flat.shape[0])).reshape(rows, D_MODEL)


LOSS_ROW = 55


def _pack_small(d, loss_share):
    parts = [_rows_of(d[n], r) for n, _, r in SMALL_LAYOUT] + [_rows_of(d[n], r) for n, _, r, _ in CONV_LAYOUT]
    parts.append(_rows_of(loss_share, SMALL_ROWS - LOSS_ROW))
    return jnp.concatenate(parts, axis=0)


def _gather8_stage(stage, s_ref, all_ref, send_sems, recv_sems, loc_sem):
    x, y, c = _my_place()
    me = 4 * x + 2 * y + c
    copies = [pltpu.make_async_copy(s_ref, all_ref.at[me], loc_sem)]
    for mk in range(1, N_DEV):
        peer = (x ^ (mk >> 2), y ^ ((mk >> 1) & 1), c ^ (mk & 1))
        copies.append(_remote(s_ref, all_ref.at[me], send_sems.at[mk - 1], recv_sems.at[mk - 1], peer))
    for cp in copies:
        cp.start() if stage == 0 else cp.wait()


def _row_chunks(rows, n):
    return [(j * (rows // n), rows // n) for j in range(n)]


def _my_place():
    return lax.axis_index("x"), lax.axis_index("y"), lax.axis_index("c")


MESH_ID = pl.DeviceIdType.MESH
ANY = pl.BlockSpec(memory_space=pl.ANY)
IN_VMEM = pl.BlockSpec(memory_space=pltpu.VMEM)
CHIP_FLIPS = ((1, 0), (0, 1), (1, 1))


def _remote(src, dst, send_sem, recv_sem, peer):
    return pltpu.make_async_remote_copy(src, dst, send_sem, recv_sem, device_id=peer, device_id_type=MESH_ID)


GATHER_CHUNKS = (4, 2, 1, 1)


def _gather_plans(kinds):
    plan = [(a, o, n, rows // 2) for a, (rows, ch) in enumerate(kinds) for o, n in _row_chunks(rows // 2, ch)]
    own_plan = [(a, o, n) for a, (rows, ch) in enumerate(kinds) for o, n in _row_chunks(rows, 2 * ch)]
    return plan, own_plan


def _gather_sems(plan, own_plan):
    hop = pltpu.SemaphoreType.DMA((3, len(plan)))
    return [hop, hop, hop, hop, pltpu.SemaphoreType.DMA((len(own_plan),))]


def _gather_stage(stage, ins, outs, plan, own_plan, sems):
    send_a, recv_a, send_b, recv_b, loc = sems
    x, y, c = _my_place()
    s = 2 * x + y
    sibling = (x, y, 1 - c)
    own = [pltpu.make_async_copy(ins[a].at[pl.ds(o, n)], outs[a].at[s, pl.ds(o, n)], loc.at[j])
           for j, (a, o, n) in enumerate(own_plan)]
    first, arrive, passed, arrive_b = [], [], [], []
    for k, (fx, fy) in enumerate(CHIP_FLIPS):
        peer = (x ^ fx, y ^ fy, c)
        sk = 2 * (x ^ fx) + (y ^ fy)
        for j, (a, o, n, h) in enumerate(plan):
            mine = pl.ds(pl.multiple_of(c * h + o, 16), n)
            theirs = pl.ds(pl.multiple_of((1 - c) * h + o, 16), n)
            first.append(_remote(ins[a].at[mine], outs[a].at[s, mine], send_a.at[k, j], recv_a.at[k, j], peer))
            land = outs[a].at[sk, mine]
            arrive.append(_remote(land, land, send_a.at[k, j], recv_a.at[k, j], peer))
            passed.append(_remote(land, land, send_b.at[k, j], recv_b.at[k, j], sibling))
            land_b = outs[a].at[sk, theirs]
            arrive_b.append(_remote(land_b, land_b, send_b.at[k, j], recv_b.at[k, j], sibling))
    if stage == 0:
        for cp in own + first:
            cp.start()
    elif stage == 1:
        for got, fwd in zip(arrive, passed):
            got.wait_recv()
            fwd.start()
    else:
        for got in arrive_b:
            got.wait_recv()
        for cp in first + passed:
            cp.wait_send()
        for cp in own:
            cp.wait()


W_OUT_DIRECT = (512, D_MODEL, 2, ((0, 0, 0), (0, 512, 0), (1, 0, 0), (1, 512, 0)))
WPG_DIRECT = (256, D_MODEL, 1, ((0, 0, 0), (0, 256, 0), (0, 512, 0), (0, 768, 0)))
WPP_DIRECT = (D_PLE, 256, 1, ((1, 0, 0), (1, 0, 256), (1, 0, 512), (1, 0, 768)))


def _direct_sems(specs):
    out = []
    for _, _, copies, _ in specs:
        out += [pltpu.SemaphoreType.DMA((N_CHIPS, 2, copies)), pltpu.SemaphoreType.DMA((N_DEV, copies)),
                pltpu.SemaphoreType.DMA((copies,))]
    return out


def _direct_stage(stage, specs, srcs, recvs, sems):
    x, y, c = _my_place()
    me = 4 * x + 2 * y + c
    s = 2 * x + y
    for r, (rows, cols, copies, where) in enumerate(specs):
        send_sems, recv_sems, loc_sems = sems[3 * r:3 * r + 3]
        for k, (o, n) in enumerate(_row_chunks(rows, copies)):
            for t, (si, row0, col0) in enumerate(where):
                src = srcs[si].at[pl.ds(row0 + o, n), pl.ds(col0, cols)]
                dst = recvs[r].at[me, pl.ds(o, n)]
                for cc in range(2):
                    to_self = jnp.logical_and(s == t, c == cc)
                    away = _remote(src, dst, send_sems.at[t, cc, k], recv_sems.at[me, k], (t // 2, t % 2, cc))
                    here = pltpu.make_async_copy(src, dst, loc_sems.at[k])

                    @pl.when(to_self)
                    def _():
                        here.start() if stage == 0 else here.wait()

                    @pl.when(jnp.logical_not(to_self))
                    def _():
                        away.start() if stage == 0 else away.wait_send()
            if stage == 1:
                for j in range(N_DEV):
                    land = recvs[r].at[j, pl.ds(o, n)]

                    @pl.when(j != me)
                    def _():
                        _remote(land, land, send_sems.at[0, 0, k], recv_sems.at[j, k], (0, 0, 0)).wait_recv()


def _gather_weights(w_in_b, conv_f):
    plan, own_plan = _gather_plans([(BIG[0][1], GATHER_CHUNKS[0])])

    def body(w_ref, conv_ref, w_all, conv_all, *sems):
        conv_send, conv_recv, conv_loc = sems[5:]
        x, y, c = _my_place()
        s = 2 * x + y
        conv_own = pltpu.make_async_copy(conv_ref, conv_all.at[s], conv_loc)
        conv_out = [_remote(conv_ref, conv_all.at[s], conv_send.at[k], conv_recv.at[k], (x ^ fx, y ^ fy, c))
                    for k, (fx, fy) in enumerate(CHIP_FLIPS)]
        _gather_stage(0, [w_ref], [w_all], plan, own_plan, sems[0:5])
        for cp in [conv_own] + conv_out:
            cp.start()
        _gather_stage(1, [w_ref], [w_all], plan, own_plan, sems[0:5])
        _gather_stage(2, [w_ref], [w_all], plan, own_plan, sems[0:5])
        for cp in conv_out:
            cp.wait()
        conv_own.wait()

    arrays = [w_in_b, conv_f]
    return pl.pallas_call(
        body, name="gather_weights", in_specs=[IN_VMEM] * 2, out_specs=[ANY] * 2,
        out_shape=[jax.ShapeDtypeStruct((N_CHIPS,) + a.shape, a.dtype) for a in arrays],
        scratch_shapes=_gather_sems(plan, own_plan)
        + [pltpu.SemaphoreType.DMA((3,)), pltpu.SemaphoreType.DMA((3,)), pltpu.SemaphoreType.DMA],
    )(*arrays)


SHARE_CHUNKS = 4


OWN_TILES = (((0, 1024), (1536, 512)), ((1024, 512), (1920, 640), (2560, 256), (5632, 128)), ((2688, 1536),), ((4096, 1536),))
OWN_W = 1536
RED_CHUNKS = 8


def _own_columns(t, rows):
    if t == 0:
        return rows[:, 0:W_IN_SH]
    if t == 1:
        return jnp.concatenate([rows[:, 516:1152], rows[:, 0:512], rows[:, 1408:1424], rows[:, 1152:1400]], axis=1)
    first = (2808 - 2688) if t == 2 else (4220 - 4096)
    return rows[:, first:first + W_IN_SH]


def _reduce_w_in(dw_r):
    half = D_MODEL // 2
    rc = half // RED_CHUNKS
    max_r = max(len(r) for r in OWN_TILES)

    def body(g_ref, got_ref, theirs, a_buf, b_buf, sum_buf, d2d_send, d2d_recv, ld_sems, ici_send, ici_recv, own_sems):
        x, y, c = _my_place()
        s = 2 * x + y
        sibling = (x, y, 1 - c)
        swap = [_remote(g_ref.at[pl.ds(pl.multiple_of((1 - c) * half + k * rc, SUBLANES), rc)], theirs.at[pl.ds(k * rc, rc)],
                        d2d_send.at[k], d2d_recv.at[k], sibling) for k in range(RED_CHUNKS)]
        for cp in swap:
            cp.start()

        def tiles(k, t, r):
            k0, wd = OWN_TILES[t][r]
            at = sum(w_ for _, w_ in OWN_TILES[t][:r])
            src = sum_buf.at[k, pl.ds(0, rc), pl.ds(k0, wd)]
            dst = got_ref.at[s, pl.ds(k * rc, rc), pl.ds(at, wd)]
            return (_remote(src, dst, ici_send.at[t, r, k], ici_recv.at[s, r, k], (t // 2, t % 2, c)),
                    pltpu.make_async_copy(src, dst, own_sems.at[r, k]))

        mine = [pltpu.make_async_copy(g_ref.at[pl.ds(pl.multiple_of(c * half + k * rc, SUBLANES), rc)], a_buf.at[k % 2],
                                      ld_sems.at[0, k % 2]) for k in range(RED_CHUNKS)]
        other = [pltpu.make_async_copy(theirs.at[pl.ds(k * rc, rc)], b_buf.at[k % 2], ld_sems.at[1, k % 2])
                 for k in range(RED_CHUNKS)]
        mine[0].start()
        for k in range(RED_CHUNKS):
            slot = k % 2
            swap[k].wait_recv()
            other[k].start()
            if k + 1 < RED_CHUNKS:
                mine[k + 1].start()
            mine[k].wait()
            other[k].wait()
            sum_buf[k] = (a_buf[slot] + b_buf[slot]).astype(BF16)
            for t in range(N_CHIPS):
                for r in range(len(OWN_TILES[t])):
                    away, here = tiles(k, t, r)

                    @pl.when(t != s)
                    def _():
                        away.start()

                    @pl.when(t == s)
                    def _():
                        here.start()

        for k in range(RED_CHUNKS):
            swap[k].wait_send()
            for t in range(N_CHIPS):
                for r, (_, wd) in enumerate(OWN_TILES[t]):
                    away, here = tiles(k, t, r)
                    at = sum(w_ for _, w_ in OWN_TILES[t][:r])

                    @pl.when(t != s)
                    def _():
                        away.wait_send()

                    @pl.when(t == s)
                    def _():
                        here.wait()
                        for j in range(N_CHIPS):
                            if j != t:
                                land = got_ref.at[j, pl.ds(k * rc, rc), pl.ds(at, wd)]
                                _remote(land, land, ici_send.at[0, 0, 0], ici_recv.at[j, r, k], (0, 0, 0)).wait_recv()

    dma = pltpu.SemaphoreType.DMA
    got, _ = pl.pallas_call(
        body, name="reduce_w_in", in_specs=[ANY], out_specs=[ANY, ANY],
        out_shape=[jax.ShapeDtypeStruct((N_CHIPS, half, OWN_W), BF16), jax.ShapeDtypeStruct((half, D_INR), F32)],
        scratch_shapes=[pltpu.VMEM((2, rc, D_INR), F32), pltpu.VMEM((2, rc, D_INR), F32), pltpu.VMEM((RED_CHUNKS, rc, D_INR), BF16),
                        dma((RED_CHUNKS,)), dma((RED_CHUNKS,)), dma((2, 2)), dma((N_CHIPS, max_r, RED_CHUNKS)),
                        dma((N_CHIPS, max_r, RED_CHUNKS)), dma((max_r, RED_CHUNKS))],
        compiler_params=pltpu.CompilerParams(vmem_limit_bytes=VMEM_LIMIT),
    )(dw_r)
    return got


def _add_slots(got):
    _, rows, cols = got.shape
    tr = 128

    def body(g_ref, o_ref):
        o_ref[...] = ((g_ref[0].astype(F32) + g_ref[1].astype(F32)) + g_ref[2].astype(F32)) + g_ref[3].astype(F32)

    return pl.pallas_call(body, grid=(rows // tr,), name="add_chips_w_in",
                          in_specs=[pl.BlockSpec((N_CHIPS, tr, cols), lambda i: (0, i, 0))],
                          out_specs=pl.BlockSpec((tr, cols), lambda i: (i, 0)),
                          out_shape=jax.ShapeDtypeStruct((rows, cols), F32), compiler_params=_params(seq=False))(got)


def _share_halves(tot):
    n_rows, half = tot.shape
    plan = _row_chunks(half, SHARE_CHUNKS)

    def body(t_ref, both, send_sems, recv_sems, loc):
        x, y, c = _my_place()
        rows = pl.ds(0, n_rows)

        def place(o, n):
            return both.at[rows, pl.ds(pl.multiple_of(c * half + o, LANES), n)]

        own = [pltpu.make_async_copy(t_ref.at[rows, pl.ds(o, n)], place(o, n), loc.at[j]) for j, (o, n) in enumerate(plan)]
        sends = [_remote(t_ref.at[rows, pl.ds(o, n)], place(o, n), send_sems.at[j], recv_sems.at[j], (x, y, 1 - c))
                 for j, (o, n) in enumerate(plan)]
        for cp in own + sends:
            cp.start()
        for cp in sends:
            cp.wait()
        for cp in own:
            cp.wait()

    dma = pltpu.SemaphoreType.DMA
    return pl.pallas_call(
        body, name="share_halves", in_specs=[IN_VMEM], out_specs=ANY,
        out_shape=jax.ShapeDtypeStruct((n_rows, 2 * half), F32),
        scratch_shapes=[dma((len(plan),)), dma((len(plan),)), dma((len(plan),))],
    )(tot)


def _adam_math(w, g, m, v):
    m = ADAM_B1 * m + (1.0 - ADAM_B1) * g
    v = ADAM_B2 * v + (1.0 - ADAM_B2) * (g * g)
    m_hat = m / (1.0 - ADAM_B1 ** ADAM_STEP)
    v_hat = v / (1.0 - ADAM_B2 ** ADAM_STEP)
    return -ADAM_LR * (m_hat / (jnp.sqrt(v_hat) + ADAM_EPS) + ADAM_WD * w), m, v


def _adam(w, g, m, v, name):
    rows, cols = w.shape

    def body(w_ref, g_ref, m_ref, v_ref, d_ref, nm_ref, nv_ref):
        d_ref[...], nm_ref[...], nv_ref[...] = _adam_math(w_ref[...], g_ref[...], m_ref[...], v_ref[...])

    if rows <= 256 or rows % 256 == 0:
        tr = min(rows, 256)
        n_blocks, spec = rows // tr, pl.BlockSpec((tr, cols), lambda i: (i, 0))
    else:
        n_blocks, spec = cols // 256, pl.BlockSpec((rows, 256), lambda i: (0, i))
    return pl.pallas_call(body, grid=(n_blocks,), name=name, in_specs=[spec] * 4, out_specs=[spec] * 3,
                          out_shape=[jax.ShapeDtypeStruct(w.shape, F32)] * 3, compiler_params=_params(seq=False))(w, g, m, v)


def _adam_sum(w, parts, m, v, name):
    rows, cols = w.shape
    tr = rows if rows <= 256 else 256

    def body(w_ref, p_ref, m_ref, v_ref, g_ref, d_ref, nm_ref, nv_ref):
        g = p_ref[0].astype(F32)
        for j in range(1, N_DEV):
            g = g + p_ref[j].astype(F32)
        g_ref[...] = g
        d_ref[...], nm_ref[...], nv_ref[...] = _adam_math(w_ref[...], g, m_ref[...], v_ref[...])

    spec = pl.BlockSpec((tr, cols), lambda i: (i, 0))
    return pl.pallas_call(
        body, grid=(rows // tr,), name=name,
        in_specs=[spec, pl.BlockSpec((N_DEV, tr, cols), lambda i: (0, i, 0)), spec, spec], out_specs=[spec] * 4,
        out_shape=[jax.ShapeDtypeStruct(w.shape, F32)] * 4, compiler_params=_params(seq=False))(w, parts, m, v)


def _adam_small(parts, ws, ms, vs):
    n_w = len(SMALL_LAYOUT)

    def body(p_ref, *refs):
        w_refs, m_refs, v_refs = refs[0:n_w], refs[n_w:2 * n_w], refs[2 * n_w:3 * n_w]
        sum_ref = refs[3 * n_w]
        outs = refs[3 * n_w + 1:]
        g = p_ref[0]
        for k in range(1, N_DEV):
            g = g + p_ref[k]
        sum_ref[...] = g
        for a, (_, r0, rows) in enumerate(SMALL_LAYOUT):
            width = w_refs[a].shape[1]
            for j in range(rows):
                lo, hi = j * D_MODEL, min((j + 1) * D_MODEL, width)
                gj = sum_ref[r0 + j:r0 + j + 1, 0:hi - lo]
                d, nm, nv = _adam_math(w_refs[a][:, lo:hi], gj, m_refs[a][:, lo:hi], v_refs[a][:, lo:hi])
                for out, val in zip(outs[4 * a:4 * a + 4], (gj, d, nm, nv)):
                    out[:, lo:hi] = val

    shapes = [jax.ShapeDtypeStruct((SMALL_ROWS, D_MODEL), F32)]
    for wa in ws:
        shapes += [jax.ShapeDtypeStruct(wa.shape, F32)] * 4
    res = pl.pallas_call(body, name="adam_small", out_shape=shapes)(parts, *ws, *ms, *vs)
    return res[0], [res[1 + 4 * a:5 + 4 * a] for a in range(n_w)]


def kernel(x, p, ln_emb_g, ln_emb_b, w_in, ssm_conv_w, ssm_conv_b, dt_bias, a_log, d_skip, ssm_norm_g, b_glu, conf_conv_w, conf_conv_b, conf_ln_g, conf_ln_b, w_out, b_out, ln1_g, ln1_b, w_ple_gate, w_ple_proj, ln2_g, ln2_b, loss_target, m_ln_emb_g, m_ln_emb_b, m_w_in, m_ssm_conv_w, m_ssm_conv_b, m_dt_bias, m_a_log, m_d_skip, m_ssm_norm_g, m_b_glu, m_conf_conv_w, m_conf_conv_b, m_conf_ln_g, m_conf_ln_b, m_w_out, m_b_out, m_ln1_g, m_ln1_b, m_w_ple_gate, m_w_ple_proj, m_ln2_g, m_ln2_b, v_ln_emb_g, v_ln_emb_b, v_w_in, v_ssm_conv_w, v_ssm_conv_b, v_dt_bias, v_a_log, v_d_skip, v_ssm_norm_g, v_b_glu, v_conf_conv_w, v_conf_conv_b, v_conf_ln_g, v_conf_ln_b, v_w_out, v_b_out, v_ln1_g, v_ln1_b, v_w_ple_gate, v_w_ple_proj, v_ln2_g, v_ln2_b):
    order = ("ln_emb_g", "ln_emb_b", "w_in", "ssm_conv_w", "ssm_conv_b", "dt_bias", "a_log", "d_skip", "ssm_norm_g", "b_glu",
             "conf_conv_w", "conf_conv_b", "conf_ln_g", "conf_ln_b", "w_out", "b_out", "ln1_g", "ln1_b", "w_ple_gate",
             "w_ple_proj", "ln2_g", "ln2_b")
    w = dict(zip(order, (ln_emb_g, ln_emb_b, w_in, ssm_conv_w, ssm_conv_b, dt_bias, a_log, d_skip, ssm_norm_g, b_glu,
                         conf_conv_w, conf_conv_b, conf_ln_g, conf_ln_b, w_out, b_out, ln1_g, ln1_b, w_ple_gate, w_ple_proj,
                         ln2_g, ln2_b)))
    m = dict(zip(order, (m_ln_emb_g, m_ln_emb_b, m_w_in, m_ssm_conv_w, m_ssm_conv_b, m_dt_bias, m_a_log, m_d_skip,
                         m_ssm_norm_g, m_b_glu, m_conf_conv_w, m_conf_conv_b, m_conf_ln_g, m_conf_ln_b, m_w_out, m_b_out,
                         m_ln1_g, m_ln1_b, m_w_ple_gate, m_w_ple_proj, m_ln2_g, m_ln2_b)))
    v = dict(zip(order, (v_ln_emb_g, v_ln_emb_b, v_w_in, v_ssm_conv_w, v_ssm_conv_b, v_dt_bias, v_a_log, v_d_skip,
                         v_ssm_norm_g, v_b_glu, v_conf_conv_w, v_conf_conv_b, v_conf_ln_g, v_conf_ln_b, v_w_out, v_b_out,
                         v_ln1_g, v_ln1_b, v_w_ple_gate, v_w_ple_proj, v_ln2_g, v_ln2_b)))

    conv_f = jnp.concatenate([_rows_of(w["ssm_conv_w"], ROWS_CW4), _rows_of(w["conf_conv_w"], ROWS_CW31),
                              jnp.zeros((ROWS_CONV - ROWS_CW4 - ROWS_CW31, D_MODEL), F32)], axis=0)
    shards_b = [w[n][0].astype(BF16) for n, _, _ in BIG]
    w_in_all, conv_all = _gather_weights(shards_b[0], conv_f)
    cw4 = conv_all[:, 0:ROWS_CW4].reshape(N_CHIPS, -1)[:, :SSM_K * D_XBC // N_CHIPS]
    cw4 = cw4.reshape(N_CHIPS, SSM_K, D_XBC // N_CHIPS).transpose(1, 0, 2).reshape(SSM_K, D_XBC)
    cw31 = conv_all[:, ROWS_CW4:ROWS_CW4 + ROWS_CW31].reshape(N_CHIPS, -1)[:, :CONF_K * D_CONF // N_CHIPS]
    cw31 = cw31.reshape(N_CHIPS, CONF_K, D_CONF // N_CHIPS).transpose(1, 0, 2).reshape(CONF_K, D_CONF)

    small_names = [n for n, _, _ in SMALL_LAYOUT]
    sm = {n: w[n] for n in small_names}
    dx, grads = _local_grads(x[0], p[0, 0], loss_target[0], w_in_all, shards_b[1:], cw4, cw31, sm)

    chip_i = 2 * lax.axis_index("x") + lax.axis_index("y")
    mine = lax.switch(chip_i, [functools.partial(_own_columns, t) for t in range(N_CHIPS)], _add_slots(_reduce_w_in(grads["w_in"])))
    g_w_in_t = _share_halves(jnp.swapaxes(mine, 0, 1))

    out_g, out_d, out_m, out_v = {}, {}, {}, {}
    as_row = lambda a: a.reshape(1, -1)
    g_s, small_out = _adam_small(grads["small"], [as_row(w[n]) for n in small_names], [as_row(m[n]) for n in small_names],
                                 [as_row(v[n]) for n in small_names])
    loss = g_s[LOSS_ROW, 0]
    tr_ = lambda a: jnp.swapaxes(a, 0, 1)
    n = "w_in"
    d, nm, nv = _adam(tr_(w[n][0]), g_w_in_t, tr_(m[n][0]), tr_(v[n][0]), "adam_" + n)
    out_g[n], out_d[n], out_m[n], out_v[n] = [tr_(a)[None] for a in (g_w_in_t, d, nm, nv)]
    for n, r0, r, shape in CONV_LAYOUT:
        whole = g_s[r0:r0 + r].reshape(-1)[:shape[0] * shape[1]].reshape(shape)
        g = lax.dynamic_slice_in_dim(whole, chip_i * (shape[1] // N_CHIPS), shape[1] // N_CHIPS, axis=1)
        d, nm, nv = _adam(w[n][0], g, m[n][0], v[n][0], "adam_" + n)
        out_g[n], out_d[n], out_m[n], out_v[n] = g[None], d[None], nm[None], nv[None]
    for n, _, _ in BIG[1:]:
        out_g[n], out_d[n], out_m[n], out_v[n] = [a[None] for a in _adam_sum(w[n][0], grads[n], m[n][0], v[n][0], "adam_" + n)]
    for n, four in zip(small_names, small_out):
        out_g[n], out_d[n], out_m[n], out_v[n] = [a.reshape(w[n].shape) for a in four]
    return (loss, dx[None], *[out_g[n] for n in order], *[out_d[n] for n in order], *[out_m[n] for n in order],
            *[out_v[n] for n in order])
```
